```python
import jax, jax.numpy as jnp
from jax import lax
import numpy as np

D_MODEL = 1024
BATCH = 8
SEQ = 8192
DEPTH = 2

N_META = 16
MIX_WIDTH = D_MODEL
ATTN_WIDTH = MIX_WIDTH // 2
CONV_WIDTH = MIX_WIDTH - ATTN_WIDTH
HEAD_DIM = 64
N_Q_HEADS = ATTN_WIDTH // HEAD_DIM
N_KV_HEADS = 2
GROUP = N_Q_HEADS // N_KV_HEADS
KV_WIDTH = N_KV_HEADS * HEAD_DIM
CONV_GROUPS = 8
CONV_K = 3
WINDOW = 128
BLOCK = 128
LEAD_PAD = BLOCK - N_META
ROPE_THETA = 500000.0
ROT_DIM = HEAD_DIM // 4
D_FF = 4 * D_MODEL
IN_WIDTH = ATTN_WIDTH + 2 * KV_WIDTH + 3 * CONV_WIDTH
EPS = 1e-6

kernel_name = 'hymba_swa_sink_shortconv_sandwich'


def rmsnorm(x, g):
    x32 = x.astype(jnp.float32)
    y = x32 * lax.rsqrt(jnp.mean(x32 * x32, axis=-1, keepdims=True) + EPS)
    return y.astype(x.dtype) * g.astype(x.dtype)


def rope_tables(n_pos):
    pos = jnp.arange(n_pos, dtype=jnp.float32)
    inv_freq = jnp.power(jnp.float32(ROPE_THETA), -jnp.arange(0, ROT_DIM, 2, dtype=jnp.float32) / ROT_DIM)
    ang = pos[:, None] * inv_freq[None, :]
    return jnp.cos(ang), jnp.sin(ang)


def partial_rope(t, cos, sin):
    half = ROT_DIM // 2
    t32 = t[..., :ROT_DIM].astype(jnp.float32)
    t1, t2 = t32[..., :half], t32[..., half:]
    c, s = cos[None, :, None, :], sin[None, :, None, :]
    rot = jnp.concatenate([t1 * c - t2 * s, t2 * c + t1 * s], axis=-1).astype(t.dtype)
    return jnp.concatenate([rot, t[..., ROT_DIM:]], axis=-1)


def sliding_window_gqa_sinks(q, k, v, sink):
    bsz, L = q.shape[0], q.shape[1]
    pad = ((0, 0), (LEAD_PAD, 0), (0, 0), (0, 0))
    q, k, v = jnp.pad(q, pad), jnp.pad(k, pad), jnp.pad(v, pad)
    Lp = L + LEAD_PAD
    nb = Lp // BLOCK
    qb = q.reshape(bsz, nb, BLOCK, N_KV_HEADS, GROUP, HEAD_DIM)

    def band(t):
        tb = t.reshape(bsz, nb, BLOCK, N_KV_HEADS, HEAD_DIM)
        prev = jnp.pad(tb, ((0, 0), (1, 0), (0, 0), (0, 0), (0, 0)))[:, :-1]
        return jnp.concatenate([prev, tb], axis=2)

    kw, vw = band(k), band(v)
    s = jnp.einsum('bnqhgd,bnkhd->bnhgqk', qb, kw,
                   preferred_element_type=jnp.float32) * (HEAD_DIM ** -0.5)
    blk = jnp.arange(nb)[:, None, None]
    qpos = blk * BLOCK + jnp.arange(BLOCK)[None, :, None]
    kpos = (blk - 1) * BLOCK + jnp.arange(2 * BLOCK)[None, None, :]
    mask = (kpos <= qpos) & (qpos - kpos < WINDOW) & (kpos >= LEAD_PAD)
    s = jnp.where(mask[None, :, None, None], s, -jnp.inf)
    sk = sink.astype(jnp.float32).reshape(1, 1, N_KV_HEADS, GROUP, 1, 1)
    m = jnp.maximum(jnp.max(s, axis=-1, keepdims=True), sk)
    e = jnp.exp(s - m)
    p = e / (jnp.sum(e, axis=-1, keepdims=True) + jnp.exp(sk - m))
    o = jnp.einsum('bnhgqk,bnkhd->bnqhgd', p.astype(v.dtype), vw)
    return o.reshape(bsz, Lp, N_Q_HEADS * HEAD_DIM)[:, LEAD_PAD:]


def short_gated_conv(b_gate, c_gate, h, w):
    u = c_gate * h
    y = lax.conv_general_dilated(u, w[:, None, :].astype(u.dtype), window_strides=(1,),
                                 padding=[(CONV_K - 1, 0)],
                                 dimension_numbers=('NWC', 'WIO', 'NWC'),
                                 feature_group_count=CONV_WIDTH)
    return b_gate * y


def _fwd_setup_inputs(seed: int = 0) -> dict:
    key = jax.random.key(seed)
    ks = jax.random.split(key, 16)
    f32 = jnp.float32

    def nrm(k, shape, scale):
        return jax.random.normal(k, shape, f32) * scale

    def gain(k, shape):
        return 1.0 + 0.05 * jax.random.normal(k, shape, f32)

    return {
        'x': nrm(ks[0], (BATCH, SEQ, D_MODEL), 1.0),
        'meta_tokens': nrm(ks[1], (N_META, D_MODEL), 1.0),
        'mix_pre_g': gain(ks[2], (DEPTH, D_MODEL)),
        'w_in': nrm(ks[3], (DEPTH, D_MODEL, IN_WIDTH), D_MODEL ** -0.5),
        'conv_w': nrm(ks[4], (DEPTH, CONV_K, CONV_WIDTH), CONV_K ** -0.5),
        'sinks': nrm(ks[5], (DEPTH, N_Q_HEADS), 0.5),
        'attn_out_g': gain(ks[6], (DEPTH, ATTN_WIDTH)),
        'conv_out_g': gain(ks[7], (DEPTH, CONV_WIDTH)),
        'w_out': nrm(ks[8], (DEPTH, MIX_WIDTH, D_MODEL), MIX_WIDTH ** -0.5),
        'mix_post_g': gain(ks[9], (DEPTH, D_MODEL)),
        'mlp_pre_g': gain(ks[10], (DEPTH, D_MODEL)),
        'w_up': nrm(ks[11], (DEPTH, D_MODEL, D_FF), D_MODEL ** -0.5),
        'w_down': nrm(ks[12], (DEPTH, D_FF, D_MODEL), D_FF ** -0.5),
        'mlp_post_g': gain(ks[13], (DEPTH, D_MODEL)),
    }


def _fwd_reference(x, meta_tokens, mix_pre_g, w_in, conv_w, sinks, attn_out_g, conv_out_g,
              w_out, mix_post_g, mlp_pre_g, w_up, w_down, mlp_post_g):
    bsz = x.shape[0]
    meta = jnp.broadcast_to(meta_tokens[None].astype(x.dtype), (bsz, N_META, D_MODEL))
    h = jnp.concatenate([meta, x], axis=1)
    L = h.shape[1]
    cos, sin = rope_tables(L)
    s_q = ATTN_WIDTH
    s_k = s_q + KV_WIDTH
    s_v = s_k + KV_WIDTH
    s_b = s_v + CONV_WIDTH
    s_c = s_b + CONV_WIDTH
    for l in range(DEPTH):
        a = rmsnorm(h, mix_pre_g[l])
        proj = a @ w_in[l]
        q = proj[..., :s_q].reshape(bsz, L, N_Q_HEADS, HEAD_DIM)
        k = proj[..., s_q:s_k].reshape(bsz, L, N_KV_HEADS, HEAD_DIM)
        v = proj[..., s_k:s_v].reshape(bsz, L, N_KV_HEADS, HEAD_DIM)
        b_gate = proj[..., s_v:s_b]
        c_gate = proj[..., s_b:s_c]
        hc = proj[..., s_c:]
        q = partial_rope(q, cos, sin)
        k = partial_rope(k, cos, sin)
        y_attn = sliding_window_gqa_sinks(q, k, v, sinks[l])
        y_conv = short_gated_conv(b_gate, c_gate, hc, conv_w[l])
        y = jnp.concatenate([rmsnorm(y_attn, attn_out_g[l]),
                             rmsnorm(y_conv, conv_out_g[l])], axis=-1)
        h = h + rmsnorm(y @ w_out[l], mix_post_g[l])
        a = rmsnorm(h, mlp_pre_g[l])
        f = jnp.square(jax.nn.relu(a @ w_up[l])) @ w_down[l]
        h = h + rmsnorm(f, mlp_post_g[l])
    return h[:, N_META:]


import jax as _jax
import jax.numpy as _jnp

TWIN_FORMAT = 'train_step'
FWD_PARAMS = ['x', 'meta_tokens', 'mix_pre_g', 'w_in', 'conv_w', 'sinks', 'attn_out_g', 'conv_out_g', 'w_out', 'mix_post_g', 'mlp_pre_g', 'w_up', 'w_down', 'mlp_post_g']
TWIN_WEIGHTS = ['meta_tokens', 'mix_pre_g', 'w_in', 'conv_w', 'sinks', 'attn_out_g', 'conv_out_g', 'w_out', 'mix_post_g', 'mlp_pre_g', 'w_up', 'w_down', 'mlp_post_g']
TWIN_DIFF_INPUT = 'x'
TWIN_INPUTS = ['x', 'meta_tokens', 'mix_pre_g', 'w_in', 'conv_w', 'sinks', 'attn_out_g', 'conv_out_g', 'w_out', 'mix_post_g', 'mlp_pre_g', 'w_up', 'w_down', 'mlp_post_g', 'loss_target', 'm_meta_tokens', 'm_mix_pre_g', 'm_w_in', 'm_conv_w', 'm_sinks', 'm_attn_out_g', 'm_conv_out_g', 'm_w_out', 'm_mix_post_g', 'm_mlp_pre_g', 'm_w_up', 'm_w_down', 'm_mlp_post_g', 'v_meta_tokens', 'v_mix_pre_g', 'v_w_in', 'v_conv_w', 'v_sinks', 'v_attn_out_g', 'v_conv_out_g', 'v_w_out', 'v_mix_post_g', 'v_mlp_pre_g', 'v_w_up', 'v_w_down', 'v_mlp_post_g']
TWIN_OUTPUTS = ['loss', 'grad_x', 'grad_meta_tokens', 'grad_mix_pre_g', 'grad_w_in', 'grad_conv_w', 'grad_sinks', 'grad_attn_out_g', 'grad_conv_out_g', 'grad_w_out', 'grad_mix_post_g', 'grad_mlp_pre_g', 'grad_w_up', 'grad_w_down', 'grad_mlp_post_g', 'delta_meta_tokens', 'delta_mix_pre_g', 'delta_w_in', 'delta_conv_w', 'delta_sinks', 'delta_attn_out_g', 'delta_conv_out_g', 'delta_w_out', 'delta_mix_post_g', 'delta_mlp_pre_g', 'delta_w_up', 'delta_w_down', 'delta_mlp_post_g', 'new_m_meta_tokens', 'new_m_mix_pre_g', 'new_m_w_in', 'new_m_conv_w', 'new_m_sinks', 'new_m_attn_out_g', 'new_m_conv_out_g', 'new_m_w_out', 'new_m_mix_post_g', 'new_m_mlp_pre_g', 'new_m_w_up', 'new_m_w_down', 'new_m_mlp_post_g', 'new_v_meta_tokens', 'new_v_mix_pre_g', 'new_v_w_in', 'new_v_conv_w', 'new_v_sinks', 'new_v_attn_out_g', 'new_v_conv_out_g', 'new_v_w_out', 'new_v_mix_post_g', 'new_v_mlp_pre_g', 'new_v_w_up', 'new_v_w_down', 'new_v_mlp_post_g']
TWIN_LEAF_KINDS = {'loss': 'loss', 'grad_x': 'grad_x', 'grad_meta_tokens': 'grad_w', 'grad_mix_pre_g': 'grad_w', 'grad_w_in': 'grad_w', 'grad_conv_w': 'grad_w', 'grad_sinks': 'grad_w', 'grad_attn_out_g': 'grad_w', 'grad_conv_out_g': 'grad_w', 'grad_w_out': 'grad_w', 'grad_mix_post_g': 'grad_w', 'grad_mlp_pre_g': 'grad_w', 'grad_w_up': 'grad_w', 'grad_w_down': 'grad_w', 'grad_mlp_post_g': 'grad_w', 'delta_meta_tokens': 'delta_w', 'delta_mix_pre_g': 'delta_w', 'delta_w_in': 'delta_w', 'delta_conv_w': 'delta_w', 'delta_sinks': 'delta_w', 'delta_attn_out_g': 'delta_w', 'delta_conv_out_g': 'delta_w', 'delta_w_out': 'delta_w', 'delta_mix_post_g': 'delta_w', 'delta_mlp_pre_g': 'delta_w', 'delta_w_up': 'delta_w', 'delta_w_down': 'delta_w', 'delta_mlp_post_g': 'delta_w', 'new_m_meta_tokens': 'new_m', 'new_m_mix_pre_g': 'new_m', 'new_m_w_in': 'new_m', 'new_m_conv_w': 'new_m', 'new_m_sinks': 'new_m', 'new_m_attn_out_g': 'new_m', 'new_m_conv_out_g': 'new_m', 'new_m_w_out': 'new_m', 'new_m_mix_post_g': 'new_m', 'new_m_mlp_pre_g': 'new_m', 'new_m_w_up': 'new_m', 'new_m_w_down': 'new_m', 'new_m_mlp_post_g': 'new_m', 'new_v_meta_tokens': 'new_v', 'new_v_mix_pre_g': 'new_v', 'new_v_w_in': 'new_v', 'new_v_conv_w': 'new_v', 'new_v_sinks': 'new_v', 'new_v_attn_out_g': 'new_v', 'new_v_conv_out_g': 'new_v', 'new_v_w_out': 'new_v', 'new_v_mix_post_g': 'new_v', 'new_v_mlp_pre_g': 'new_v', 'new_v_w_up': 'new_v', 'new_v_w_down': 'new_v', 'new_v_mlp_post_g': 'new_v'}


def _forward(args):
    return _fwd_reference(*[args[k] for k in FWD_PARAMS])


def _output_shape():
    def fwd():
        inp = _fwd_setup_inputs(0)
        return _fwd_reference(*[inp[k] for k in FWD_PARAMS])
    out = _jax.eval_shape(fwd)
    return out.shape, out.dtype

N_MICROBATCH = 1
ADAM_LR = 0.001
ADAM_B1 = 0.9
ADAM_B2 = 0.999
ADAM_EPS = 1e-08
ADAM_WD = 0.01
ADAM_STEP = 10
PER_EXAMPLE_BATCH_AXIS = {'x': 0, 'loss_target': 0}
SHARED_INPUTS = []
_WEIGHT_DTYPES = {'meta_tokens': _jnp.float32, 'mix_pre_g': _jnp.float32, 'w_in': _jnp.float32, 'conv_w': _jnp.float32, 'sinks': _jnp.float32, 'attn_out_g': _jnp.float32, 'conv_out_g': _jnp.float32, 'w_out': _jnp.float32, 'mix_post_g': _jnp.float32, 'mlp_pre_g': _jnp.float32, 'w_up': _jnp.float32, 'w_down': _jnp.float32, 'mlp_post_g': _jnp.float32}
MOMENT_SCALE = {'meta_tokens': 6.440397e-01, 'mix_pre_g': 9.676429e+00, 'w_in': 7.047943e+00, 'conv_w': 1.133291e+00, 'sinks': 1.087662e+00, 'attn_out_g': 1.851339e+01, 'conv_out_g': 2.187946e+00, 'w_out': 1.099210e+01, 'mix_post_g': 6.438494e+01, 'mlp_pre_g': 5.169029e+00, 'w_up': 2.541839e+00, 'w_down': 2.199799e+01, 'mlp_post_g': 7.026380e+01}


def _to_microbatches(a, axis):
    t = _jnp.moveaxis(a, axis, 0)
    t = t.reshape((N_MICROBATCH, t.shape[0] // N_MICROBATCH) + t.shape[1:])
    return _jnp.moveaxis(t, 1, axis + 1)


def setup_inputs(seed: int = 0) -> dict:
    inp = _fwd_setup_inputs(seed)
    key = _jax.random.fold_in(_jax.random.key(seed), 7919)
    shape, _ = _output_shape()
    out = dict(inp)
    out["loss_target"] = _jax.random.normal(_jax.random.fold_in(key, 0), shape, _jnp.float32)
    for i, name in enumerate(TWIN_WEIGHTS):
        w = inp[name].astype(_jnp.float32)
        if MOMENT_SCALE is None:
            s = _jnp.sqrt(_jnp.mean(_jnp.square(w)) + 1e-30)
        else:
            s = MOMENT_SCALE[name]
        km, kv = _jax.random.split(_jax.random.fold_in(key, i + 1))
        out[name] = w
        out["m_" + name] = s * _jax.random.normal(km, w.shape, _jnp.float32)
        out["v_" + name] = (s * s) * _jax.random.uniform(kv, w.shape, _jnp.float32, 0.5, 1.5)
    if N_MICROBATCH > 1:
        for name, axis in PER_EXAMPLE_BATCH_AXIS.items():
            out[name] = _to_microbatches(out[name], axis)
    return {'x': out['x'], 'meta_tokens': out['meta_tokens'], 'mix_pre_g': out['mix_pre_g'], 'w_in': out['w_in'], 'conv_w': out['conv_w'], 'sinks': out['sinks'], 'attn_out_g': out['attn_out_g'], 'conv_out_g': out['conv_out_g'], 'w_out': out['w_out'], 'mix_post_g': out['mix_post_g'], 'mlp_pre_g': out['mlp_pre_g'], 'w_up': out['w_up'], 'w_down': out['w_down'], 'mlp_post_g': out['mlp_post_g'], 'loss_target': out['loss_target'], 'm_meta_tokens': out['m_meta_tokens'], 'm_mix_pre_g': out['m_mix_pre_g'], 'm_w_in': out['m_w_in'], 'm_conv_w': out['m_conv_w'], 'm_sinks': out['m_sinks'], 'm_attn_out_g': out['m_attn_out_g'], 'm_conv_out_g': out['m_conv_out_g'], 'm_w_out': out['m_w_out'], 'm_mix_post_g': out['m_mix_post_g'], 'm_mlp_pre_g': out['m_mlp_pre_g'], 'm_w_up': out['m_w_up'], 'm_w_down': out['m_w_down'], 'm_mlp_post_g': out['m_mlp_post_g'], 'v_meta_tokens': out['v_meta_tokens'], 'v_mix_pre_g': out['v_mix_pre_g'], 'v_w_in': out['v_w_in'], 'v_conv_w': out['v_conv_w'], 'v_sinks': out['v_sinks'], 'v_attn_out_g': out['v_attn_out_g'], 'v_conv_out_g': out['v_conv_out_g'], 'v_w_out': out['v_w_out'], 'v_mix_post_g': out['v_mix_post_g'], 'v_mlp_pre_g': out['v_mlp_pre_g'], 'v_w_up': out['v_w_up'], 'v_w_down': out['v_w_down'], 'v_mlp_post_g': out['v_mlp_post_g']}


def _loss(weights, diff, rest, loss_target):
    with _jax.named_scope("forward"):
        args = {**rest, TWIN_DIFF_INPUT: diff, **{k: w.astype(_WEIGHT_DTYPES[k]) for k, w in weights.items()}}
        y = _forward(args)
    with _jax.named_scope("loss_head"):
        err = _jnp.square(y.astype(_jnp.float32) - loss_target)
        return 0.5 * _jnp.sum(_jnp.mean(err, axis=-1)) if err.ndim else 0.5 * err


def _adamw(w, g, m, v):
    m = ADAM_B1 * m + (1.0 - ADAM_B1) * g
    v = ADAM_B2 * v + (1.0 - ADAM_B2) * _jnp.square(g)
    m_hat = m / (1.0 - ADAM_B1 ** ADAM_STEP)
    v_hat = v / (1.0 - ADAM_B2 ** ADAM_STEP)
    delta = -ADAM_LR * (m_hat / (_jnp.sqrt(v_hat) + ADAM_EPS) + ADAM_WD * w)
    return delta, m, v


def reference(x, meta_tokens, mix_pre_g, w_in, conv_w, sinks, attn_out_g, conv_out_g, w_out, mix_post_g, mlp_pre_g, w_up, w_down, mlp_post_g, loss_target, m_meta_tokens, m_mix_pre_g, m_w_in, m_conv_w, m_sinks, m_attn_out_g, m_conv_out_g, m_w_out, m_mix_post_g, m_mlp_pre_g, m_w_up, m_w_down, m_mlp_post_g, v_meta_tokens, v_mix_pre_g, v_w_in, v_conv_w, v_sinks, v_attn_out_g, v_conv_out_g, v_w_out, v_mix_post_g, v_mlp_pre_g, v_w_up, v_w_down, v_mlp_post_g):
    given = dict(x=x, meta_tokens=meta_tokens, mix_pre_g=mix_pre_g, w_in=w_in, conv_w=conv_w, sinks=sinks, attn_out_g=attn_out_g, conv_out_g=conv_out_g, w_out=w_out, mix_post_g=mix_post_g, mlp_pre_g=mlp_pre_g, w_up=w_up, w_down=w_down, mlp_post_g=mlp_post_g, loss_target=loss_target, m_meta_tokens=m_meta_tokens, m_mix_pre_g=m_mix_pre_g, m_w_in=m_w_in, m_conv_w=m_conv_w, m_sinks=m_sinks, m_attn_out_g=m_attn_out_g, m_conv_out_g=m_conv_out_g, m_w_out=m_w_out, m_mix_post_g=m_mix_post_g, m_mlp_pre_g=m_mlp_pre_g, m_w_up=m_w_up, m_w_down=m_w_down, m_mlp_post_g=m_mlp_post_g, v_meta_tokens=v_meta_tokens, v_mix_pre_g=v_mix_pre_g, v_w_in=v_w_in, v_conv_w=v_conv_w, v_sinks=v_sinks, v_attn_out_g=v_attn_out_g, v_conv_out_g=v_conv_out_g, v_w_out=v_w_out, v_mix_post_g=v_mix_post_g, v_mlp_pre_g=v_mlp_pre_g, v_w_up=v_w_up, v_w_down=v_w_down, v_mlp_post_g=v_mlp_post_g)
    weights = {n: given[n] for n in TWIN_WEIGHTS}
    shared = {n: given[n] for n in SHARED_INPUTS}
    per_example = {n: given[n] for n in ['x']}
    grad_fn = _jax.value_and_grad(_loss, argnums=(0, 1))

    def one_microbatch(ex, loss_target):
        ex = dict(ex)
        diff = ex.pop(TWIN_DIFF_INPUT)
        return grad_fn(weights, diff, {**shared, **ex}, loss_target)

    if N_MICROBATCH == 1:
        loss, (grad_w, grad_x) = one_microbatch(per_example, given["loss_target"])
    else:
        def body(carry, xs):
            loss_sum, grad_sum = carry
            l_k, (gw_k, gx_k) = one_microbatch(xs[0], xs[1])
            with _jax.named_scope("update"):
                return (loss_sum + l_k, _jax.tree.map(_jnp.add, grad_sum, gw_k)), gx_k

        init = (_jnp.zeros((), _jnp.float32), _jax.tree.map(_jnp.zeros_like, weights))
        (loss, grad_w), grad_x = _jax.lax.scan(body, init, (per_example, given["loss_target"]))
    with _jax.named_scope("update"):
        delta_w, new_m, new_v = {}, {}, {}
        for n in TWIN_WEIGHTS:
            delta_w[n], new_m[n], new_v[n] = _adamw(weights[n], grad_w[n], given["m_" + n], given["v_" + n])
    return (loss, grad_x, *[grad_w[n] for n in TWIN_WEIGHTS], *[delta_w[n] for n in TWIN_WEIGHTS],
            *[new_m[n] for n in TWIN_WEIGHTS], *[new_v[n] for n in TWIN_WEIGHTS])
```

```python
import functools

import jax
import jax.numpy as jnp
from jax import lax
from jax.experimental import pallas as pl
from jax.experimental.pallas import tpu as pltpu

F32 = jnp.float32
BF = jnp.bfloat16

D_MODEL = 1024
ATTN_W = 512
CONV_W = 512
KV_W = 128
HEAD_DIM = 64
N_Q_HEADS = 8
ROT_DIM = 16
D_FF = 4096
IN_W = 2304
N_META = 16
BLOCK = 128
LEAD_PAD = BLOCK - N_META
ROPE_THETA = 500000.0
EPS = 1e-6
N_DEV = 8
DEPTH = 2
NEG = -1e30
SCALE = HEAD_DIM ** -0.5

ADAM_LR = 0.001
ADAM_B1 = 0.9
ADAM_B2 = 0.999
ADAM_EPS = 1e-08
ADAM_WD = 0.01
ADAM_STEP = 10

ROW_MLP_POST, ROW_MLP_PRE, ROW_MIX_PRE, ROW_SINK = 0, 1, 2, 3
ROW_MIX_POST, ROW_GROUP_G, ROW_CONV = 0, 1, 2

VMEM_LIMIT = 56 * 1024 * 1024
MESH = pl.DeviceIdType.MESH


def _dot(a, b):
    return jnp.dot(a, b, preferred_element_type=F32)


def _dot_nt(a, b):
    return lax.dot_general(a, b, (((1,), (1,)), ((), ())), preferred_element_type=F32)


def _dot_tn(a, b):
    return lax.dot_general(a, b, (((0,), (0,)), ((), ())), preferred_element_type=F32)


def _rms_fwd(x, g):
    r = lax.rsqrt(jnp.mean(x * x, axis=-1, keepdims=True) + EPS)
    return x * r * g


def _rms_bwd(x, g, dy):
    r = lax.rsqrt(jnp.mean(x * x, axis=-1, keepdims=True) + EPS)
    xh = x * r
    t = dy * g
    dx = r * (t - xh * jnp.mean(t * xh, axis=-1, keepdims=True))
    dg = jnp.sum(dy * xh, axis=0, keepdims=True)
    return dx, dg


def _row_tile(lp, cands=(640, 512, 384, 256, 128)):
    for t in cands:
        if lp % t == 0:
            return t
    raise ValueError(f"row count {lp} is not a multiple of 128")


def _full(shape):
    n = len(shape)
    return pl.BlockSpec(shape, lambda *_: (0,) * n, pipeline_mode=pl.Buffered(1))


def _full_out(shape):
    n = len(shape)
    return pl.BlockSpec(shape, lambda *_: (0,) * n)


def _params(sem=("arbitrary",)):
    return pltpu.CompilerParams(dimension_semantics=sem, vmem_limit_bytes=VMEM_LIMIT)


def _rope_tables(lp):
    pos = jnp.maximum(jnp.arange(lp) - LEAD_PAD, 0).astype(F32)
    inv_freq = jnp.power(jnp.float32(ROPE_THETA), -jnp.arange(0, ROT_DIM, 2, dtype=F32) / ROT_DIM)
    ang = pos[:, None] * inv_freq[None, :]
    cos, sin = jnp.cos(ang), jnp.sin(ang)
    half = ROT_DIM // 2
    one = jnp.ones((lp, HEAD_DIM - ROT_DIM), F32)
    zero = jnp.zeros((lp, HEAD_DIM - ROT_DIM), F32)
    z8 = jnp.zeros((lp, half), F32)
    c = jnp.concatenate([cos, cos, one], axis=1)
    s1 = jnp.concatenate([-sin, z8, zero], axis=1)
    s2 = jnp.concatenate([z8, sin, zero], axis=1)
    two = lambda t: jnp.concatenate([t, t], axis=1)
    return two(c), two(s1), two(s2)


def _rope(t, c, s1, s2):
    return t * c + pltpu.roll(t, BLOCK - 8, 1) * s1 + pltpu.roll(t, 8, 1) * s2


def _rope_t(dt, c, s1, s2):
    return dt * c + pltpu.roll(dt * s1, 8, 1) + pltpu.roll(dt * s2, BLOCK - 8, 1)


def _in_proj_fwd(h, g, w_in_t, rope, tm, name):
    lp = h.shape[0]
    rc, rs1, rs2 = rope

    def body(h_ref, g_ref, w_ref, c_ref, s1_ref, s2_ref, a_ref, qkv_ref, bch_ref):
        a = _rms_fwd(h_ref[...], g_ref[...]).astype(BF)
        a_ref[...] = a
        proj = _dot_nt(a, w_ref[...])
        c, s1, s2 = c_ref[...], s1_ref[...], s2_ref[...]
        for j in range(5):
            t = proj[:, j * 128:(j + 1) * 128]
            qkv_ref[:, j * 128:(j + 1) * 128] = _rope(t, c, s1, s2).astype(BF)
        qkv_ref[:, 640:768] = proj[:, 640:768].astype(BF)
        bch_ref[...] = proj[:, 768:]

    row = lambda w: pl.BlockSpec((tm, w), lambda i: (i, 0))
    return pl.pallas_call(
        body,
        name=name,
        grid=(lp // tm,),
        in_specs=[row(D_MODEL), _full((1, D_MODEL)), _full((IN_W, D_MODEL)), row(128), row(128), row(128)],
        out_specs=[row(D_MODEL), row(768), row(3 * CONV_W)],
        out_shape=[
            jax.ShapeDtypeStruct((lp, D_MODEL), BF),
            jax.ShapeDtypeStruct((lp, 768), BF),
            jax.ShapeDtypeStruct((lp, 3 * CONV_W), F32),
        ],
        compiler_params=_params(),
    )(h, g, w_in_t, rc, rs1, rs2)


def _attn_mask(i):
    r = lax.broadcasted_iota(jnp.int32, (BLOCK, 2 * BLOCK), 0)
    c = lax.broadcasted_iota(jnp.int32, (BLOCK, 2 * BLOCK), 1)
    return (c > r) & (c <= r + BLOCK) & (c + (i - 1) * BLOCK >= LEAD_PAD)


def _head_operands(x, kvh):
    lane = lax.broadcasted_iota(jnp.int32, x.shape, 1)
    zero = jnp.zeros_like(x)
    if kvh == 0:
        lo = jnp.where(lane < HEAD_DIM, x, zero)
        return lo, pltpu.roll(lo, HEAD_DIM, 1)
    hi = jnp.where(lane >= HEAD_DIM, x, zero)
    return pltpu.roll(hi, HEAD_DIM, 1), hi


def _softmax_sink(s, sk):
    m = jnp.maximum(jnp.max(s, axis=-1, keepdims=True), sk)
    e = jnp.exp(s - m)
    es = jnp.exp(sk - m)
    den = jnp.sum(e, axis=-1, keepdims=True) + es
    return e / den, es / den


def _attn_fwd(qkv, sink, name):
    lp = qkv.shape[0]
    nb = lp // BLOCK

    def body(sink_ref, q_ref, kvc_ref, kvp_ref, o_ref):
        i = pl.program_id(0)
        mask = _attn_mask(i)
        kvc, kvp = kvc_ref[...], kvp_ref[...]
        kk = jnp.concatenate([kvp[:, :128], kvc[:, :128]], axis=0)
        vv = jnp.concatenate([kvp[:, 128:], kvc[:, 128:]], axis=0)
        for kvh in range(2):
            k_ops = _head_operands(kk, kvh)
            v_ops = _head_operands(vv, kvh)
            for jj in range(2):
                j = 2 * kvh + jj
                qp = q_ref[:, j * 128:(j + 1) * 128]
                out = jnp.zeros((BLOCK, 128), F32)
                for half in range(2):
                    s = jnp.where(mask, _dot_nt(qp, k_ops[half]) * SCALE, NEG)
                    p, _ = _softmax_sink(s, sink_ref[0, 2 * j + half])
                    out = out + _dot(p.astype(BF), v_ops[half])
                o_ref[:, j * 128:(j + 1) * 128] = out

    return pl.pallas_call(
        body,
        name=name,
        grid=(nb,),
        in_specs=[
            pl.BlockSpec(memory_space=pltpu.SMEM),
            pl.BlockSpec((BLOCK, ATTN_W), lambda i: (i, 0)),
            pl.BlockSpec((BLOCK, 256), lambda i: (i, 2)),
            pl.BlockSpec((BLOCK, 256), lambda i: (jnp.maximum(i - 1, 0), 2)),
        ],
        out_specs=pl.BlockSpec((BLOCK, ATTN_W), lambda i: (i, 0)),
        out_shape=jax.ShapeDtypeStruct((lp, ATTN_W), F32),
        compiler_params=_params(),
    )(sink, qkv, qkv, qkv)


def _mix_out_fwd(bch, y_attn, h, conv_w, g_a, g_c, w_out, g_post, tm, name):
    lp = h.shape[0]

    def body(bch_ref, ya_ref, h_ref, cw_ref, ga_ref, gc_ref, w_ref, gp_ref, yc_ref, y_ref, z_ref, h2_ref, ext):
        i = pl.program_id(0)

        @pl.when(i == 0)
        def _():
            ext[0:8, :] = jnp.zeros((8, CONV_W), F32)

        b = bch_ref[:, 0:CONV_W]
        u = bch_ref[:, CONV_W:2 * CONV_W] * bch_ref[:, 2 * CONV_W:3 * CONV_W]
        ext[8:8 + tm, :] = u
        u1 = ext[7:7 + tm, :]
        u2 = ext[6:6 + tm, :]
        yc = cw_ref[0:1, :] * u2 + cw_ref[1:2, :] * u1 + cw_ref[2:3, :] * u
        ext[0:8, :] = u[tm - 8:tm, :]
        yc_ref[...] = yc
        ya = _rms_fwd(ya_ref[...], ga_ref[...]).astype(BF)
        yb = _rms_fwd(b * yc, gc_ref[...]).astype(BF)
        y_ref[:, 0:ATTN_W] = ya
        y_ref[:, ATTN_W:] = yb
        z = _dot(ya, w_ref[0:ATTN_W, :]) + _dot(yb, w_ref[ATTN_W:, :])
        z_ref[...] = z
        h2_ref[...] = h_ref[...] + _rms_fwd(z, gp_ref[...])

    row = lambda w: pl.BlockSpec((tm, w), lambda i: (i, 0))
    return pl.pallas_call(
        body,
        name=name,
        grid=(lp // tm,),
        in_specs=[
            row(3 * CONV_W), row(ATTN_W), row(D_MODEL), _full((8, CONV_W)), _full((1, ATTN_W)), _full((1, CONV_W)),
            _full((D_MODEL, D_MODEL)), _full((1, D_MODEL)),
        ],
        out_specs=[row(CONV_W), row(D_MODEL), row(D_MODEL), row(D_MODEL)],
        out_shape=[
            jax.ShapeDtypeStruct((lp, CONV_W), F32),
            jax.ShapeDtypeStruct((lp, D_MODEL), BF),
            jax.ShapeDtypeStruct((lp, D_MODEL), F32),
            jax.ShapeDtypeStruct((lp, D_MODEL), F32),
        ],
        scratch_shapes=[pltpu.VMEM((tm + 8, CONV_W), F32)],
        compiler_params=_params(),
    )(bch, y_attn, h, conv_w, g_a, g_c, w_out, g_post)


def _mlp_fwd(h2, g_pre, w_up_t, w_down, g_post, tm, name):
    lp = h2.shape[0]

    def body(h_ref, gp_ref, wu_ref, wd_ref, gq_ref, a_ref, up_ref, f_ref, h3_ref):
        h = h_ref[...]
        a = _rms_fwd(h, gp_ref[...]).astype(BF)
        a_ref[...] = a
        up = _dot_nt(a, wu_ref[...])
        up_ref[...] = up.astype(BF)
        act = jnp.square(jnp.maximum(up, 0.0)).astype(BF)
        f = _dot(act, wd_ref[...])
        f_ref[...] = f
        h3_ref[...] = h + _rms_fwd(f, gq_ref[...])

    row = lambda w: pl.BlockSpec((tm, w), lambda i: (i, 0))
    return pl.pallas_call(
        body,
        name=name,
        grid=(lp // tm,),
        in_specs=[row(D_MODEL), _full((1, D_MODEL)), _full((D_FF, D_MODEL)), _full((D_FF, D_MODEL)), _full((1, D_MODEL))],
        out_specs=[row(D_MODEL), row(D_FF), row(D_MODEL), row(D_MODEL)],
        out_shape=[
            jax.ShapeDtypeStruct((lp, D_MODEL), BF),
            jax.ShapeDtypeStruct((lp, D_FF), BF),
            jax.ShapeDtypeStruct((lp, D_MODEL), F32),
            jax.ShapeDtypeStruct((lp, D_MODEL), F32),
        ],
        compiler_params=_params(),
    )(h2, g_pre, w_up_t, w_down, g_post)


def _loss_head(h, target, name):
    lp = h.shape[0]
    nb = lp // BLOCK

    def body(h_ref, t_ref, dh_ref, ls_ref):
        i = pl.program_id(0)

        @pl.when(i == 0)
        def _():
            dh_ref[...] = jnp.zeros((BLOCK, D_MODEL), F32)
            ls_ref[...] = jnp.zeros((8, 128), F32)

        @pl.when(i > 0)
        def _():
            d = h_ref[...] - t_ref[...]
            dh_ref[...] = d * (1.0 / D_MODEL)
            ls_ref[...] += jnp.sum(d * d)

    dh, ls = pl.pallas_call(
        body,
        name=name,
        grid=(nb,),
        in_specs=[
            pl.BlockSpec((BLOCK, D_MODEL), lambda i: (i, 0)),
            pl.BlockSpec((BLOCK, D_MODEL), lambda i: (jnp.maximum(i - 1, 0), 0)),
        ],
        out_specs=[pl.BlockSpec((BLOCK, D_MODEL), lambda i: (i, 0)), pl.BlockSpec((8, 128), lambda i: (0, 0))],
        out_shape=[jax.ShapeDtypeStruct((lp, D_MODEL), F32), jax.ShapeDtypeStruct((8, 128), F32)],
        compiler_params=_params(),
    )(h, target)
    return dh, ls[0, 0] * (0.5 / D_MODEL)


def _mlp_bwd_dx(dh3, f, up, h2, w_down, w_up_t, g_post, g_pre, tm, name):
    lp = h2.shape[0]

    def body(dh3_ref, f_ref, up_ref, h2_ref, wd_ref, wu_ref, gq_ref, gp_ref, df_ref, dup_ref, dh2_ref, dg_ref):
        i = pl.program_id(0)

        @pl.when(i == 0)
        def _():
            dg_ref[...] = jnp.zeros((8, D_MODEL), F32)

        dh3 = dh3_ref[...]
        df, dgq = _rms_bwd(f_ref[...], gq_ref[...], dh3)
        dg_ref[ROW_MLP_POST:ROW_MLP_POST + 1, :] += dgq
        df = df.astype(BF)
        df_ref[...] = df
        dact = _dot_nt(df, wd_ref[...])
        dup = (dact * (2.0 * jnp.maximum(up_ref[...].astype(F32), 0.0))).astype(BF)
        dup_ref[...] = dup
        da = _dot(dup, wu_ref[...])
        dh, dgp = _rms_bwd(h2_ref[...], gp_ref[...], da)
        dg_ref[ROW_MLP_PRE:ROW_MLP_PRE + 1, :] += dgp
        dh2_ref[...] = dh3 + dh

    row = lambda w: pl.BlockSpec((tm, w), lambda i: (i, 0))
    return pl.pallas_call(
        body,
        name=name,
        grid=(lp // tm,),
        in_specs=[
            row(D_MODEL), row(D_MODEL), row(D_FF), row(D_MODEL), _full((D_FF, D_MODEL)), _full((D_FF, D_MODEL)),
            _full((1, D_MODEL)), _full((1, D_MODEL)),
        ],
        out_specs=[row(D_MODEL), row(D_FF), row(D_MODEL), _full_out((8, D_MODEL))],
        out_shape=[
            jax.ShapeDtypeStruct((lp, D_MODEL), BF),
            jax.ShapeDtypeStruct((lp, D_FF), BF),
            jax.ShapeDtypeStruct((lp, D_MODEL), F32),
            jax.ShapeDtypeStruct((8, D_MODEL), F32),
        ],
        compiler_params=_params(),
    )(dh3, f, up, h2, w_down, w_up_t, g_post, g_pre)


def _mlp_bwd_dw(up, df, dup, a2, tm, name):
    lp = up.shape[0]
    nt = lp // tm
    nj = D_FF // D_MODEL

    def body(up_ref, df_ref, dup_ref, a_ref, dwd_ref, dwu_ref, accd, accu):
        i = pl.program_id(1)

        @pl.when(i == 0)
        def _():
            accd[...] = jnp.zeros_like(accd)
            accu[...] = jnp.zeros_like(accu)

        act = jnp.square(jnp.maximum(up_ref[...].astype(F32), 0.0)).astype(BF)
        accd[...] += _dot_tn(act, df_ref[...])
        accu[...] += _dot_tn(dup_ref[...], a_ref[...])

        @pl.when(i == nt - 1)
        def _():
            dwd_ref[...] = accd[...].astype(BF)
            dwu_ref[...] = accu[...].astype(BF)

    return pl.pallas_call(
        body,
        name=name,
        grid=(nj, nt),
        in_specs=[
            pl.BlockSpec((tm, D_MODEL), lambda j, i: (i, j)),
            pl.BlockSpec((tm, D_MODEL), lambda j, i: (i, 0)),
            pl.BlockSpec((tm, D_MODEL), lambda j, i: (i, j)),
            pl.BlockSpec((tm, D_MODEL), lambda j, i: (i, 0)),
        ],
        out_specs=[pl.BlockSpec((D_MODEL, D_MODEL), lambda j, i: (j, 0)), pl.BlockSpec((D_MODEL, D_MODEL), lambda j, i: (j, 0))],
        out_shape=[jax.ShapeDtypeStruct((D_FF, D_MODEL), BF), jax.ShapeDtypeStruct((D_FF, D_MODEL), BF)],
        scratch_shapes=[pltpu.VMEM((D_MODEL, D_MODEL), F32), pltpu.VMEM((D_MODEL, D_MODEL), F32)],
        compiler_params=_params(("arbitrary", "arbitrary")),
    )(up, df, dup, a2)


def _mix_out_bwd(dh2, z, y_attn, yc, bch, w_out, g_post, g_a, g_c, conv_w, tm, name):
    lp = dh2.shape[0]
    nt = lp // tm

    def body(dh2_ref, z_ref, ya_ref, yc_ref, bch_ref, w_ref, gp_ref, ga_ref, gc_ref, cw_ref,
             dz_ref, dya_ref, dbch_ref, dg_ref, ext):
        i = pl.program_id(0)
        dcw_ref = dg_ref.at[ROW_CONV:ROW_CONV + 3, 0:CONV_W]

        @pl.when(i == 0)
        def _():
            ext[tm:tm + 8, :] = jnp.zeros((8, CONV_W), F32)
            dg_ref[...] = jnp.zeros((8, D_MODEL), F32)

        dz, dgp = _rms_bwd(z_ref[...], gp_ref[...], dh2_ref[...])
        dg_ref[ROW_MIX_POST:ROW_MIX_POST + 1, :] += dgp
        dz = dz.astype(BF)
        dz_ref[...] = dz
        dya_n = _dot_nt(dz, w_ref[0:ATTN_W, :])
        dyb_n = _dot_nt(dz, w_ref[ATTN_W:, :])
        dya, dga = _rms_bwd(ya_ref[...], ga_ref[...], dya_n)
        dg_ref[ROW_GROUP_G:ROW_GROUP_G + 1, 0:ATTN_W] += dga
        dya_ref[...] = dya
        b = bch_ref[:, 0:CONV_W]
        c = bch_ref[:, CONV_W:2 * CONV_W]
        hc = bch_ref[:, 2 * CONV_W:3 * CONV_W]
        yc_v = yc_ref[...]
        dyconv, dgc = _rms_bwd(b * yc_v, gc_ref[...], dyb_n)
        dg_ref[ROW_GROUP_G:ROW_GROUP_G + 1, ATTN_W:] += dgc
        dbch_ref[:, 0:CONV_W] = (dyconv * yc_v).astype(BF)
        dyc = dyconv * b
        ext[0:tm, :] = dyc
        d1 = ext[1:1 + tm, :]
        d2 = ext[2:2 + tm, :]
        du = cw_ref[2:3, :] * dyc + cw_ref[1:2, :] * d1 + cw_ref[0:1, :] * d2
        ext[tm:tm + 8, :] = dyc[0:8, :]
        dbch_ref[:, CONV_W:2 * CONV_W] = (du * hc).astype(BF)
        dbch_ref[:, 2 * CONV_W:3 * CONV_W] = (du * c).astype(BF)
        u = c * hc
        dcw_ref[0:1, :] += jnp.sum(u * d2, axis=0, keepdims=True)
        dcw_ref[1:2, :] += jnp.sum(u * d1, axis=0, keepdims=True)
        dcw_ref[2:3, :] += jnp.sum(u * dyc, axis=0, keepdims=True)

    row = lambda w: pl.BlockSpec((tm, w), lambda i: (nt - 1 - i, 0))
    return pl.pallas_call(
        body,
        name=name,
        grid=(nt,),
        in_specs=[
            row(D_MODEL), row(D_MODEL), row(ATTN_W), row(CONV_W), row(3 * CONV_W), _full((D_MODEL, D_MODEL)),
            _full((1, D_MODEL)), _full((1, ATTN_W)), _full((1, CONV_W)), _full((8, CONV_W)),
        ],
        out_specs=[row(D_MODEL), row(ATTN_W), row(3 * CONV_W), _full_out((8, D_MODEL))],
        out_shape=[
            jax.ShapeDtypeStruct((lp, D_MODEL), BF),
            jax.ShapeDtypeStruct((lp, ATTN_W), F32),
            jax.ShapeDtypeStruct((lp, 3 * CONV_W), BF),
            jax.ShapeDtypeStruct((8, D_MODEL), F32),
        ],
        scratch_shapes=[pltpu.VMEM((tm + 8, CONV_W), F32)],
        compiler_params=_params(),
    )(dh2, z, y_attn, yc, bch, w_out, g_post, g_a, g_c, conv_w)


def _attn_bwd(qkv, o, do, sink, rope, name):
    lp = qkv.shape[0]
    nb = lp // BLOCK
    rc, rs1, rs2 = rope

    def body(sink_ref, q_ref, kvc_ref, kvp_ref, o_ref, do_ref, cq_ref, s1q_ref, s2q_ref, ck_ref, s1k_ref, s2k_ref,
             dq_ref, dkv_ref, dsink_ref, carry):
        i = pl.program_id(0)

        @pl.when(i == 0)
        def _():
            carry[...] = jnp.zeros_like(carry)
            dsink_ref[...] = jnp.zeros((8, 128), F32)

        def finish(tot):
            dk = _rope_t(tot[:, :128] * SCALE, ck_ref[...], s1k_ref[...], s2k_ref[...])
            dkv_ref[:, 0:128] = dk.astype(BF)
            dkv_ref[:, 128:256] = tot[:, 128:].astype(BF)

        @pl.when(i < nb)
        def _():
            mask = _attn_mask(i)
            kvc, kvp = kvc_ref[...], kvp_ref[...]
            kk = jnp.concatenate([kvp[:, :128], kvc[:, :128]], axis=0)
            vv = jnp.concatenate([kvp[:, 128:], kvc[:, 128:]], axis=0)
            lane_q = lax.broadcasted_iota(jnp.int32, (BLOCK, 128), 1)
            lane_k = lax.broadcasted_iota(jnp.int32, (2 * BLOCK, 128), 1)
            row_s = lax.broadcasted_iota(jnp.int32, (8, 128), 0)
            lane_s = jnp.where(row_s == ROW_SINK, lax.broadcasted_iota(jnp.int32, (8, 128), 1), -1)
            d_k = jnp.zeros((2 * BLOCK, 128), F32)
            d_v = jnp.zeros((2 * BLOCK, 128), F32)
            dsink = jnp.zeros((8, 128), F32)
            for kvh in range(2):
                k_ops = _head_operands(kk, kvh)
                v_ops = _head_operands(vv, kvh)
                xv = jnp.zeros((2 * BLOCK, 128), F32)
                yk = jnp.zeros((2 * BLOCK, 128), F32)
                for jj in range(2):
                    j = 2 * kvh + jj
                    qp = q_ref[:, j * 128:(j + 1) * 128]
                    dop = do_ref[:, j * 128:(j + 1) * 128]
                    prod = dop * o_ref[:, j * 128:(j + 1) * 128]
                    dob = dop.astype(BF)
                    dq = jnp.zeros((BLOCK, 128), F32)
                    for half in range(2):
                        sel = (lane_q < HEAD_DIM) if half == 0 else (lane_q >= HEAD_DIM)
                        delta = jnp.sum(jnp.where(sel, prod, 0.0), axis=-1, keepdims=True)
                        s = jnp.where(mask, _dot_nt(qp, k_ops[half]) * SCALE, NEG)
                        p, ps = _softmax_sink(s, sink_ref[0, 2 * j + half])
                        dp = _dot_nt(dob, v_ops[half])
                        ds = (p * (dp - delta)).astype(BF)
                        xv = xv + _dot_tn(p.astype(BF), jnp.where(sel, dob, jnp.zeros_like(dob)))
                        yk = yk + _dot_tn(ds, jnp.where(sel, qp, jnp.zeros_like(qp)))
                        dq = dq + _dot(ds, k_ops[half])
                        dsink = dsink + jnp.where(lane_s == 2 * j + half, -jnp.sum(ps * delta), 0.0)
                    dq = _rope_t(dq * SCALE, cq_ref[...], s1q_ref[...], s2q_ref[...])
                    dq_ref[:, j * 128:(j + 1) * 128] = dq.astype(BF)
                own = (lane_k < HEAD_DIM) if kvh == 0 else (lane_k >= HEAD_DIM)
                d_v = d_v + jnp.where(own, xv + pltpu.roll(xv, HEAD_DIM, 1), 0.0)
                d_k = d_k + jnp.where(own, yk + pltpu.roll(yk, HEAD_DIM, 1), 0.0)
            dsink_ref[...] += dsink
            prev = jnp.concatenate([d_k[:BLOCK], d_v[:BLOCK]], axis=1)
            cur = jnp.concatenate([d_k[BLOCK:], d_v[BLOCK:]], axis=1)
            finish(carry[...] + prev)
            carry[...] = cur

        @pl.when(i == nb)
        def _():
            finish(carry[...])

    qi = lambda i: jnp.minimum(i, nb - 1)
    ki = lambda i: jnp.maximum(i - 1, 0)
    tab_q = pl.BlockSpec((BLOCK, 128), lambda i: (qi(i), 0))
    tab_k = pl.BlockSpec((BLOCK, 128), lambda i: (ki(i), 0))
    return pl.pallas_call(
        body,
        name=name,
        grid=(nb + 1,),
        in_specs=[
            pl.BlockSpec(memory_space=pltpu.SMEM),
            pl.BlockSpec((BLOCK, ATTN_W), lambda i: (qi(i), 0)),
            pl.BlockSpec((BLOCK, 256), lambda i: (qi(i), 2)),
            pl.BlockSpec((BLOCK, 256), lambda i: (jnp.maximum(qi(i) - 1, 0), 2)),
            pl.BlockSpec((BLOCK, ATTN_W), lambda i: (qi(i), 0)),
            pl.BlockSpec((BLOCK, ATTN_W), lambda i: (qi(i), 0)),
            tab_q, tab_q, tab_q, tab_k, tab_k, tab_k,
        ],
        out_specs=[
            pl.BlockSpec((BLOCK, ATTN_W), lambda i: (qi(i), 0)),
            pl.BlockSpec((BLOCK, 256), lambda i: (ki(i), 0)),
            pl.BlockSpec((8, 128), lambda i: (0, 0)),
        ],
        out_shape=[
            jax.ShapeDtypeStruct((lp, ATTN_W), BF),
            jax.ShapeDtypeStruct((lp, 256), BF),
            jax.ShapeDtypeStruct((8, 128), F32),
        ],
        scratch_shapes=[pltpu.VMEM((BLOCK, 256), F32)],
        compiler_params=_params(),
    )(sink, qkv, qkv, qkv, o, do, rc, rs1, rs2, rc, rs1, rs2)


def _in_proj_bwd_dx(dq, dkv, dbch, w_in_t, h, dh2, g, tm, name):
    lp = h.shape[0]

    def body(dq_ref, dkv_ref, dbch_ref, w_ref, h_ref, dh2_ref, g_ref, dh_ref, dg_ref):
        i = pl.program_id(0)

        @pl.when(i == 0)
        def _():
            dg_ref[...] = jnp.zeros((8, D_MODEL), F32)

        da = _dot(dq_ref[...], w_ref[0:512, :]) + _dot(dkv_ref[...], w_ref[512:768, :]) + _dot(dbch_ref[...], w_ref[768:, :])
        dh, dg = _rms_bwd(h_ref[...], g_ref[...], da)
        dg_ref[ROW_MIX_PRE:ROW_MIX_PRE + 1, :] += dg
        dh_ref[...] = dh2_ref[...] + dh

    row = lambda w: pl.BlockSpec((tm, w), lambda i: (i, 0))
    return pl.pallas_call(
        body,
        name=name,
        grid=(lp // tm,),
        in_specs=[row(ATTN_W), row(256), row(3 * CONV_W), _full((IN_W, D_MODEL)), row(D_MODEL), row(D_MODEL), _full((1, D_MODEL))],
        out_specs=[row(D_MODEL), _full_out((8, D_MODEL))],
        out_shape=[jax.ShapeDtypeStruct((lp, D_MODEL), F32), jax.ShapeDtypeStruct((8, D_MODEL), F32)],
        compiler_params=_params(),
    )(dq, dkv, dbch, w_in_t, h, dh2, g)


def _mix_bwd_dw(dq, dkv, dbch, a, y, dz, tm, name):
    lp = a.shape[0]
    nt = lp // tm

    def body(dq_ref, dkv_ref, dbch_ref, a_ref, y_ref, dz_ref, dwi_ref, dwo_ref, acci, acco):
        i = pl.program_id(0)

        @pl.when(i == 0)
        def _():
            acci[...] = jnp.zeros_like(acci)
            acco[...] = jnp.zeros_like(acco)

        a_v = a_ref[...]
        acci[0:512, :] += _dot_tn(dq_ref[...], a_v)
        acci[512:768, :] += _dot_tn(dkv_ref[...], a_v)
        acci[768:, :] += _dot_tn(dbch_ref[...], a_v)
        acco[...] += _dot_tn(y_ref[...], dz_ref[...])

        @pl.when(i == nt - 1)
        def _():
            dwi_ref[...] = acci[...].astype(BF)
            dwo_ref[...] = acco[...].astype(BF)

    row = lambda w: pl.BlockSpec((tm, w), lambda i: (i, 0))
    return pl.pallas_call(
        body,
        name=name,
        grid=(nt,),
        in_specs=[row(ATTN_W), row(256), row(3 * CONV_W), row(D_MODEL), row(D_MODEL), row(D_MODEL)],
        out_specs=[_full_out((IN_W, D_MODEL)), _full_out((D_MODEL, D_MODEL))],
        out_shape=[jax.ShapeDtypeStruct((IN_W, D_MODEL), BF), jax.ShapeDtypeStruct((D_MODEL, D_MODEL), BF)],
        scratch_shapes=[pltpu.VMEM((IN_W, D_MODEL), F32), pltpu.VMEM((D_MODEL, D_MODEL), F32)],
        compiler_params=_params(),
    )(dq, dkv, dbch, a, y, dz)


def _mesh_place():
    x, y, c = lax.axis_index("x"), lax.axis_index("y"), lax.axis_index("c")
    return x, y, c, 4 * x + 2 * y + c


def _peer(x, y, c, k):
    px = 1 - x if k & 4 else x
    py = 1 - y if k & 2 else y
    pc = 1 - c if k & 1 else c
    return (px, py, pc), 4 * px + 2 * py + pc


def _gather_weights(shards, small):
    n = len(shards)
    pieces = [(t, l) for t in range(n) for l in range(DEPTH)]
    npc = len(pieces) + 1

    def body(*refs):
        ins, small_ref = refs[:n], refs[n]
        outs, small_out = refs[n + 1:2 * n + 1], refs[2 * n + 1]
        stage = refs[2 * n + 2:3 * n + 2]
        send_sems, recv_sems, local_sems = refs[3 * n + 2:]
        x, y, c, me = _mesh_place()
        for t in range(n):
            stage[t][...] = ins[t][...].astype(BF)

        def src_dst(p, slot):
            if p == npc - 1:
                return small_ref, small_out.at[slot]
            t, l = pieces[p]
            return stage[t].at[l], outs[t].at[l, slot]

        own = []
        for p in range(npc):
            s, d = src_dst(p, me)
            cp = pltpu.make_async_copy(s, d, local_sems.at[p])
            cp.start()
            own.append(cp)
        sent = []
        for k in range(1, N_DEV):
            peer, _ = _peer(x, y, c, k)
            for p in range(npc):
                s, d = src_dst(p, me)
                cp = pltpu.make_async_remote_copy(
                    src_ref=s, dst_ref=d, send_sem=send_sems.at[k - 1, p], recv_sem=recv_sems.at[k - 1, p],
                    device_id=peer, device_id_type=MESH)
                cp.start()
                sent.append(cp)
        for k in range(1, N_DEV):
            peer, pidx = _peer(x, y, c, k)
            for p in range(npc):
                s, d = src_dst(p, pidx)
                pltpu.make_async_remote_copy(
                    src_ref=s, dst_ref=d, send_sem=send_sems.at[k - 1, p], recv_sem=recv_sems.at[k - 1, p],
                    device_id=peer, device_id_type=MESH).wait_recv()
        for cp in sent:
            cp.wait_send()
        for cp in own:
            cp.wait()

    vmem = pl.BlockSpec(memory_space=pltpu.VMEM)
    hbm = pl.BlockSpec(memory_space=pl.ANY)
    out_shape = [jax.ShapeDtypeStruct((DEPTH, N_DEV) + s.shape[1:], BF) for s in shards]
    out_shape.append(jax.ShapeDtypeStruct((N_DEV,) + small.shape, F32))
    return pl.pallas_call(
        body,
        name="gather_weights",
        in_specs=[vmem] * (n + 1),
        out_specs=[hbm] * (n + 1),
        out_shape=out_shape,
        scratch_shapes=[pltpu.VMEM(s.shape, BF) for s in shards]
        + [pltpu.SemaphoreType.DMA((N_DEV - 1, npc)), pltpu.SemaphoreType.DMA((N_DEV - 1, npc)), pltpu.SemaphoreType.DMA((npc,))],
        compiler_params=pltpu.CompilerParams(vmem_limit_bytes=VMEM_LIMIT),
    )(*shards, small)


def _scatter_grads(grads):
    n = len(grads)
    pieces = [(t, l) for t in range(n) for l in range(DEPTH)]
    npc = len(pieces)

    def body(*refs):
        ins, outs = refs[:n], refs[n:2 * n]
        send_sems, recv_sems, local_sems = refs[2 * n:]
        x, y, c, me = _mesh_place()
        own = []
        for p, (t, l) in enumerate(pieces):
            cp = pltpu.make_async_copy(ins[t].at[l, me], outs[t].at[l, me], local_sems.at[p])
            cp.start()
            own.append(cp)
        sent = []
        for k in range(1, N_DEV):
            peer, pidx = _peer(x, y, c, k)
            for p, (t, l) in enumerate(pieces):
                cp = pltpu.make_async_remote_copy(
                    src_ref=ins[t].at[l, pidx], dst_ref=outs[t].at[l, me], send_sem=send_sems.at[k - 1, p],
                    recv_sem=recv_sems.at[k - 1, p], device_id=peer, device_id_type=MESH)
                cp.start()
                sent.append(cp)
        for k in range(1, N_DEV):
            peer, pidx = _peer(x, y, c, k)
            for p, (t, l) in enumerate(pieces):
                pltpu.make_async_remote_copy(
                    src_ref=ins[t].at[l, pidx], dst_ref=outs[t].at[l, pidx], send_sem=send_sems.at[k - 1, p],
                    recv_sem=recv_sems.at[k - 1, p], device_id=peer, device_id_type=MESH).wait_recv()
        for cp in sent:
            cp.wait_send()
        for cp in own:
            cp.wait()

    hbm = pl.BlockSpec(memory_space=pl.ANY)
    return pl.pallas_call(
        body,
        name="scatter_grads",
        in_specs=[hbm] * n,
        out_specs=[hbm] * n,
        out_shape=[jax.ShapeDtypeStruct(g.shape, g.dtype) for g in grads],
        scratch_shapes=[pltpu.SemaphoreType.DMA((N_DEV - 1, npc)), pltpu.SemaphoreType.DMA((N_DEV - 1, npc)),
                        pltpu.SemaphoreType.DMA((npc,))],
    )(*grads)


def _sum_small(part):
    def body(part_ref, out_ref, land, send_sems, recv_sems):
        x, y, c, me = _mesh_place()
        land[me] = part_ref[...]
        sent = []
        for k in range(1, N_DEV):
            peer, _ = _peer(x, y, c, k)
            cp = pltpu.make_async_remote_copy(
                src_ref=part_ref, dst_ref=land.at[me], send_sem=send_sems.at[k - 1], recv_sem=recv_sems.at[k - 1],
                device_id=peer, device_id_type=MESH)
            cp.start()
            sent.append(cp)
        for k in range(1, N_DEV):
            peer, pidx = _peer(x, y, c, k)
            pltpu.make_async_remote_copy(
                src_ref=part_ref, dst_ref=land.at[pidx], send_sem=send_sems.at[k - 1], recv_sem=recv_sems.at[k - 1],
                device_id=peer, device_id_type=MESH).wait_recv()
        for cp in sent:
            cp.wait_send()
        acc = land[0]
        for d in range(1, N_DEV):
            acc = acc + land[d]
        out_ref[...] = acc

    vmem = pl.BlockSpec(memory_space=pltpu.VMEM)
    return pl.pallas_call(
        body,
        name="sum_small",
        in_specs=[vmem],
        out_specs=vmem,
        out_shape=jax.ShapeDtypeStruct(part.shape, F32),
        scratch_shapes=[pltpu.VMEM((N_DEV,) + part.shape, F32), pltpu.SemaphoreType.DMA((N_DEV - 1,)),
                        pltpu.SemaphoreType.DMA((N_DEV - 1,))],
    )(part)


def _adamw(w, g, m, v):
    m = ADAM_B1 * m + (1.0 - ADAM_B1) * g
    v = ADAM_B2 * v + (1.0 - ADAM_B2) * jnp.square(g)
    m_hat = m / (1.0 - ADAM_B1 ** ADAM_STEP)
    v_hat = v / (1.0 - ADAM_B2 ** ADAM_STEP)
    delta = -ADAM_LR * (m_hat / (jnp.sqrt(v_hat) + ADAM_EPS) + ADAM_WD * w)
    return delta, m, v


def _sum_parts(recv, tr, name):
    _, _, r, wd = recv.shape

    def body(r_ref, g_ref):
        acc = r_ref[0, 0].astype(F32)
        for d in range(1, N_DEV):
            acc = acc + r_ref[0, d].astype(F32)
        g_ref[0] = acc

    return pl.pallas_call(
        body,
        name=name,
        grid=(DEPTH, r // tr),
        in_specs=[pl.BlockSpec((1, N_DEV, tr, wd), lambda l, i: (l, 0, i, 0))],
        out_specs=pl.BlockSpec((1, tr, wd), lambda l, i: (l, i, 0)),
        out_shape=jax.ShapeDtypeStruct((DEPTH, r, wd), F32),
        compiler_params=_params(("arbitrary", "arbitrary")),
    )(recv)


def _sum_adamw(recv, w, m, v, tr, name):
    _, _, r, wd = recv.shape

    def body(r_ref, w_ref, m_ref, v_ref, g_ref, d_ref, mo_ref, vo_ref):
        g = r_ref[0, 0].astype(F32)
        for d in range(1, N_DEV):
            g = g + r_ref[0, d].astype(F32)
        g_ref[0] = g
        d_ref[0], mo_ref[0], vo_ref[0] = _adamw(w_ref[0], g, m_ref[0], v_ref[0])

    blk = pl.BlockSpec((1, tr, wd), lambda l, i: (l, i, 0))
    shape = jax.ShapeDtypeStruct((DEPTH, r, wd), F32)
    return pl.pallas_call(
        body,
        name=name,
        grid=(DEPTH, r // tr),
        in_specs=[pl.BlockSpec((1, N_DEV, tr, wd), lambda l, i: (l, 0, i, 0)), blk, blk, blk],
        out_specs=[blk] * 4,
        out_shape=[shape] * 4,
        compiler_params=_params(("arbitrary", "arbitrary")),
    )(recv, w, m, v)


def _adamw_rows(w, g, m, v, tr, name):
    _, r, wd = w.shape

    def body(w_ref, g_ref, m_ref, v_ref, d_ref, mo_ref, vo_ref):
        d_ref[0], mo_ref[0], vo_ref[0] = _adamw(w_ref[0], g_ref[0], m_ref[0], v_ref[0])

    blk = pl.BlockSpec((1, tr, wd), lambda l, i: (l, i, 0))
    shape = jax.ShapeDtypeStruct(w.shape, F32)
    return pl.pallas_call(
        body,
        name=name,
        grid=(DEPTH, r // tr),
        in_specs=[blk] * 4,
        out_specs=[blk] * 3,
        out_shape=[shape] * 3,
        compiler_params=_params(("arbitrary", "arbitrary")),
    )(w, g, m, v)


def _adamw_small(ws, gs, ms, vs):
    n = len(ws)

    def body(*refs):
        w_r, g_r, m_r, v_r = refs[:n], refs[n:2 * n], refs[2 * n:3 * n], refs[3 * n:4 * n]
        d_o, m_o, v_o = refs[4 * n:5 * n], refs[5 * n:6 * n], refs[6 * n:7 * n]
        for t in range(n):
            d_o[t][...], m_o[t][...], v_o[t][...] = _adamw(w_r[t][...], g_r[t][...], m_r[t][...], v_r[t][...])

    vmem = pl.BlockSpec(memory_space=pltpu.VMEM)
    shapes = [jax.ShapeDtypeStruct(w.shape, F32) for w in ws]
    outs = pl.pallas_call(
        body,
        name="adamw_small",
        in_specs=[vmem] * (4 * n),
        out_specs=[vmem] * (3 * n),
        out_shape=shapes * 3,
    )(*ws, *gs, *ms, *vs)
    return outs[:n], outs[n:2 * n], outs[2 * n:]


def kernel(x, meta_tokens, mix_pre_g, w_in, conv_w, sinks, attn_out_g, conv_out_g, w_out, mix_post_g, mlp_pre_g, w_up, w_down, mlp_post_g, loss_target, m_meta_tokens, m_mix_pre_g, m_w_in, m_conv_w, m_sinks, m_attn_out_g, m_conv_out_g, m_w_out, m_mix_post_g, m_mlp_pre_g, m_w_up, m_w_down, m_mlp_post_g, v_meta_tokens, v_mix_pre_g, v_w_in, v_conv_w, v_sinks, v_attn_out_g, v_conv_out_g, v_w_out, v_mix_post_g, v_mlp_pre_g, v_w_up, v_w_down, v_mlp_post_g):
    seq = x.shape[1]
    lp = BLOCK + seq
    tm = _row_tile(lp)
    tm_mlp = _row_tile(lp, (320, 256, 128))
    me = 4 * lax.axis_index("x") + 2 * lax.axis_index("y") + lax.axis_index("c")
    cshard = CONV_W // N_DEV
    mshard = D_MODEL // N_DEV

    small = jnp.zeros((24, 128), F32)
    small = small.at[0:N_META, :].set(meta_tokens)
    small = small.at[N_META:N_META + 6, 0:cshard].set(conv_w.reshape(6, cshard))
    shards = [jnp.swapaxes(w_in, 1, 2), w_out, jnp.swapaxes(w_up, 1, 2), w_down]
    g_in, g_out, g_up, g_down, g_small = _gather_weights(shards, small)
    w_in_t = g_in.reshape(DEPTH, IN_W, D_MODEL)
    w_out_f = g_out.reshape(DEPTH, D_MODEL, D_MODEL)
    w_up_t = g_up.reshape(DEPTH, D_FF, D_MODEL)
    w_down_f = g_down.reshape(DEPTH, D_FF, D_MODEL)
    meta_full = jnp.swapaxes(g_small[:, 0:N_META, :], 0, 1).reshape(N_META, D_MODEL)
    cw = g_small[:, N_META:N_META + 6, 0:cshard].reshape(N_DEV, DEPTH, 3, cshard)
    cw = jnp.transpose(cw, (1, 2, 0, 3)).reshape(DEPTH, 3, CONV_W)
    conv_full = jnp.concatenate([cw, jnp.zeros((DEPTH, 5, CONV_W), F32)], axis=1)

    rope = _rope_tables(lp)
    row1 = lambda a, l: a[l].reshape(1, -1)

    h = jnp.concatenate([jnp.zeros((LEAD_PAD, D_MODEL), F32), meta_full, x[0]], axis=0)
    saved = []
    for l in range(DEPTH):
        a, qkv, bch = _in_proj_fwd(h, row1(mix_pre_g, l), w_in_t[l], rope, tm, f"in_proj_fwd_{l}")
        y_attn = _attn_fwd(qkv, row1(sinks, l), f"attn_fwd_{l}")
        yc, y, z, h2 = _mix_out_fwd(bch, y_attn, h, conv_full[l], row1(attn_out_g, l), row1(conv_out_g, l), w_out_f[l],
                                    row1(mix_post_g, l), tm, f"mix_out_fwd_{l}")
        a2, up, f, h3 = _mlp_fwd(h2, row1(mlp_pre_g, l), w_up_t[l], w_down_f[l], row1(mlp_post_g, l), tm_mlp, f"mlp_fwd_{l}")
        saved.append((h, a, qkv, bch, y_attn, yc, y, z, h2, a2, up, f))
        h = h3
    dh, loss_part = _loss_head(h, loss_target[0], "loss_head")
    loss = lax.psum(loss_part, ("x", "y", "c"))

    gw = [None] * DEPTH
    gsmall = [None] * DEPTH
    for l in reversed(range(DEPTH)):
        h0, a, qkv, bch, y_attn, yc, y, z, h2, a2, up, f = saved[l]
        df, dup, dh2, dg_mlp = _mlp_bwd_dx(
            dh, f, up, h2, w_down_f[l], w_up_t[l], row1(mlp_post_g, l), row1(mlp_pre_g, l), tm_mlp, f"mlp_bwd_dx_{l}")
        dw_down, dw_up_t = _mlp_bwd_dw(up, df, dup, a2, tm, f"mlp_bwd_dw_{l}")
        dz, dya, dbch, dg_mix = _mix_out_bwd(
            dh2, z, y_attn, yc, bch, w_out_f[l], row1(mix_post_g, l), row1(attn_out_g, l), row1(conv_out_g, l),
            conv_full[l], tm, f"mix_out_bwd_{l}")
        dq, dkv, dsink = _attn_bwd(qkv, y_attn, dya, row1(sinks, l), rope, f"attn_bwd_{l}")
        dh, dg_in = _in_proj_bwd_dx(dq, dkv, dbch, w_in_t[l], h0, dh2, row1(mix_pre_g, l), tm, f"in_proj_bwd_dx_{l}")
        dw_in_t, dw_out = _mix_bwd_dw(dq, dkv, dbch, a, y, dz, tm_mlp, f"mix_bwd_dw_{l}")
        gw[l] = (dw_in_t, dw_out, dw_up_t, dw_down)
        tile_a = dg_mlp + dg_in + jnp.pad(dsink, ((0, 0), (0, D_MODEL - 128)))
        gsmall[l] = (tile_a, dg_mix)
    grad_x = dh[BLOCK:][None]

    tot = _sum_small(jnp.concatenate([gsmall[0][0], gsmall[0][1], gsmall[1][0], gsmall[1][1], dh[LEAD_PAD:BLOCK]], axis=0))
    ta = [tot[16 * l:16 * l + 8] for l in range(DEPTH)]
    tb = [tot[16 * l + 8:16 * l + 16] for l in range(DEPTH)]
    pick = lambda tiles, r0, r1, c0, c1: jnp.stack([t[r0:r1, c0:c1] for t in tiles])
    g_mlp_post = pick(ta, ROW_MLP_POST, ROW_MLP_POST + 1, 0, D_MODEL).reshape(DEPTH, D_MODEL)
    g_mlp_pre = pick(ta, ROW_MLP_PRE, ROW_MLP_PRE + 1, 0, D_MODEL).reshape(DEPTH, D_MODEL)
    g_mix_pre = pick(ta, ROW_MIX_PRE, ROW_MIX_PRE + 1, 0, D_MODEL).reshape(DEPTH, D_MODEL)
    g_sinks = pick(ta, ROW_SINK, ROW_SINK + 1, 0, N_Q_HEADS).reshape(DEPTH, N_Q_HEADS)
    g_mix_post = pick(tb, ROW_MIX_POST, ROW_MIX_POST + 1, 0, D_MODEL).reshape(DEPTH, D_MODEL)
    g_attn_out = pick(tb, ROW_GROUP_G, ROW_GROUP_G + 1, 0, ATTN_W).reshape(DEPTH, ATTN_W)
    g_conv_out = pick(tb, ROW_GROUP_G, ROW_GROUP_G + 1, ATTN_W, D_MODEL).reshape(DEPTH, CONV_W)
    g_conv_full = pick(tb, ROW_CONV, ROW_CONV + 3, 0, CONV_W)
    g_conv = lax.dynamic_slice_in_dim(g_conv_full, me * cshard, cshard, axis=2)
    g_meta = lax.dynamic_slice_in_dim(tot[16 * DEPTH:16 * DEPTH + N_META], me * mshard, mshard, axis=1)

    parts = [jnp.stack([gw[l][t] for l in range(DEPTH)]) for t in range(4)]
    parts = [p.reshape(DEPTH, N_DEV, p.shape[1] // N_DEV, D_MODEL) for p in parts]
    r_in, r_out, r_up, r_down = _scatter_grads(parts)
    g_w_in = jnp.swapaxes(_sum_parts(r_in, 96, "sum_w_in"), 1, 2)
    g_w_up = jnp.swapaxes(_sum_parts(r_up, 128, "sum_w_up"), 1, 2)
    d_w_in, nm_w_in, nv_w_in = _adamw_rows(w_in, g_w_in, m_w_in, v_w_in, 256, "adamw_w_in")
    d_w_up, nm_w_up, nv_w_up = _adamw_rows(w_up, g_w_up, m_w_up, v_w_up, 256, "adamw_w_up")
    g_w_out, d_w_out, nm_w_out, nv_w_out = _sum_adamw(r_out, w_out, m_w_out, v_w_out, 128, "adamw_w_out")
    g_w_down, d_w_down, nm_w_down, nv_w_down = _sum_adamw(r_down, w_down, m_w_down, v_w_down, 128, "adamw_w_down")

    ws = [meta_tokens, mix_pre_g, conv_w.reshape(6, cshard), sinks, attn_out_g, conv_out_g, mix_post_g, mlp_pre_g, mlp_post_g]
    gs = [g_meta, g_mix_pre, g_conv.reshape(6, cshard), g_sinks, g_attn_out, g_conv_out, g_mix_post, g_mlp_pre, g_mlp_post]
    ms = [m_meta_tokens, m_mix_pre_g, m_conv_w.reshape(6, cshard), m_sinks, m_attn_out_g, m_conv_out_g, m_mix_post_g,
          m_mlp_pre_g, m_mlp_post_g]
    vs = [v_meta_tokens, v_mix_pre_g, v_conv_w.reshape(6, cshard), v_sinks, v_attn_out_g, v_conv_out_g, v_mix_post_g,
          v_mlp_pre_g, v_mlp_post_g]
    ds, nms, nvs = _adamw_small(ws, gs, ms, vs)

    def order(meta, mix_pre, cv, sk, a_out, c_out, mix_post, mlp_pre, mlp_post, win, wout, wup, wdown):
        return [meta, mix_pre, win, cv.reshape(DEPTH, 3, cshard), sk, a_out, c_out, wout, mix_post, mlp_pre, wup, wdown, mlp_post]

    grads = order(*gs, g_w_in, g_w_out, g_w_up, g_w_down)
    deltas = order(*ds, d_w_in, d_w_out, d_w_up, d_w_down)
    new_m = order(*nms, nm_w_in, nm_w_out, nm_w_up, nm_w_down)
    new_v = order(*nvs, nv_w_in, nv_w_out, nv_w_up, nv_w_down)
    return (loss, grad_x, *grads, *deltas, *new_m, *new_v)
```

```python
import functools

import jax
import jax.numpy as jnp
from jax import lax
from jax.experimental import pallas as pl
from jax.experimental.pallas import tpu as pltpu

F32 = jnp.float32
BF = jnp.bfloat16

D_MODEL = 1024
ATTN_W = 512
CONV_W = 512
KV_W = 128
HEAD_DIM = 64
N_Q_HEADS = 8
ROT_DIM = 16
D_FF = 4096
IN_W = 2304
N_META = 16
BLOCK = 128
LEAD_PAD = BLOCK - N_META
ROPE_THETA = 500000.0
EPS = 1e-6
N_DEV = 8
DEPTH = 2
NEG = -1e30
SCALE = HEAD_DIM ** -0.5

ADAM_LR = 0.001
ADAM_B1 = 0.9
ADAM_B2 = 0.999
ADAM_EPS = 1e-08
ADAM_WD = 0.01
ADAM_STEP = 10

ROW_MLP_POST, ROW_MLP_PRE, ROW_MIX_PRE, ROW_SINK, ROW_LOSS = 0, 1, 2, 3, 4
ROW_MIX_POST, ROW_GROUP_G, ROW_CONV = 0, 1, 2

VMEM_LIMIT = 56 * 1024 * 1024
MESH = pl.DeviceIdType.MESH


def _dot(a, b):
    return jnp.dot(a, b, preferred_element_type=F32)


def _dot_nt(a, b):
    return lax.dot_general(a, b, (((1,), (1,)), ((), ())), preferred_element_type=F32)


def _dot_tn(a, b):
    return lax.dot_general(a, b, (((0,), (0,)), ((), ())), preferred_element_type=F32)


def _rms_fwd(x, g):
    r = lax.rsqrt(jnp.mean(x * x, axis=-1, keepdims=True) + EPS)
    return x * r * g


def _rms_bwd(x, g, dy):
    r = lax.rsqrt(jnp.mean(x * x, axis=-1, keepdims=True) + EPS)
    xh = x * r
    t = dy * g
    dx = r * (t - xh * jnp.mean(t * xh, axis=-1, keepdims=True))
    dg = jnp.sum(dy * xh, axis=0, keepdims=True)
    return dx, dg


def _row_tile(lp, cands=(640, 512, 384, 256, 128)):
    for t in cands:
        if lp % t == 0:
            return t
    raise ValueError(f"row count {lp} is not a multiple of 128")


def _full(shape):
    n = len(shape)
    return pl.BlockSpec(shape, lambda *_: (0,) * n, pipeline_mode=pl.Buffered(1))


def _full_out(shape):
    n = len(shape)
    return pl.BlockSpec(shape, lambda *_: (0,) * n)


def _params(sem=("arbitrary",)):
    return pltpu.CompilerParams(dimension_semantics=sem, vmem_limit_bytes=VMEM_LIMIT)


def _rope_table(lp):
    half = ROT_DIM // 2
    pos = jnp.maximum(jnp.arange(lp) - LEAD_PAD, 0).astype(F32)
    inv_freq = jnp.power(jnp.float32(ROPE_THETA), -jnp.arange(0, ROT_DIM, 2, dtype=F32) / ROT_DIM)
    ang_t = jnp.concatenate([inv_freq, inv_freq])[:, None] * pos[None, :]
    row = lax.broadcasted_iota(jnp.int32, (ROT_DIM, lp), 0)
    cs_t = jnp.where(row < half, jnp.cos(ang_t), jnp.sin(ang_t))
    return jnp.pad(cs_t.T, ((0, 0), (0, 128 - ROT_DIM)))


def _rope_coeffs(t):
    half = ROT_DIM // 2
    lane = lax.broadcasted_iota(jnp.int32, t.shape, 1)
    cos_a = jnp.where(lane < half, t, 0.0)
    sin_a = pltpu.roll(jnp.where((lane >= half) & (lane < ROT_DIM), t, 0.0), 128 - half, 1)
    c = cos_a + pltpu.roll(cos_a, half, 1) + jnp.where((lane >= ROT_DIM) & (lane < HEAD_DIM), 1.0, 0.0)
    s2 = pltpu.roll(sin_a, half, 1)
    both = lambda u: u + pltpu.roll(u, HEAD_DIM, 1)
    return both(c), both(-sin_a), both(s2)


def _rope(t, c, s1, s2):
    return t * c + pltpu.roll(t, BLOCK - 8, 1) * s1 + pltpu.roll(t, 8, 1) * s2


def _rope_t(dt, c, s1, s2):
    return dt * c + pltpu.roll(dt * s1, 8, 1) + pltpu.roll(dt * s2, BLOCK - 8, 1)


def _in_proj_fwd(h, g, w_in_t, rope, tm, name):
    lp = h.shape[0]

    def body(h_ref, g_ref, w_ref, rope_ref, a_ref, qkv_ref, bch_ref):
        a = _rms_fwd(h_ref[...], g_ref[...]).astype(BF)
        a_ref[...] = a
        proj = _dot_nt(a, w_ref[...])
        c, s1, s2 = _rope_coeffs(rope_ref[...])
        for j in range(5):
            t = _rope(proj[:, j * 128:(j + 1) * 128], c, s1, s2)
            qkv_ref[:, j * 128:(j + 1) * 128] = (t * SCALE if j < 4 else t).astype(BF)
        qkv_ref[:, 640:768] = proj[:, 640:768].astype(BF)
        bch_ref[...] = proj[:, 768:]

    row = lambda w: pl.BlockSpec((tm, w), lambda i: (i, 0))
    return pl.pallas_call(
        body,
        name=name,
        grid=(lp // tm,),
        in_specs=[row(D_MODEL), _full((1, D_MODEL)), _full((IN_W, D_MODEL)), row(128)],
        out_specs=[row(D_MODEL), row(768), row(3 * CONV_W)],
        out_shape=[
            jax.ShapeDtypeStruct((lp, D_MODEL), BF),
            jax.ShapeDtypeStruct((lp, 768), BF),
            jax.ShapeDtypeStruct((lp, 3 * CONV_W), F32),
        ],
        compiler_params=_params(),
    )(h, g, w_in_t, rope)


def _fold_masks(i):
    r = lax.broadcasted_iota(jnp.int32, (2 * BLOCK, BLOCK), 0) & (BLOCK - 1)
    c = lax.broadcasted_iota(jnp.int32, (2 * BLOCK, BLOCK), 1)
    tri = c > r
    ok = jnp.where(tri, (i - 1) * BLOCK + c, i * BLOCK + c) >= LEAD_PAD
    return tri, ok


def _kv_operand(x, kvh):
    lane = lax.broadcasted_iota(jnp.int32, x.shape, 1)
    zero = jnp.zeros_like(x)
    if kvh == 0:
        lo = jnp.where(lane < HEAD_DIM, x, zero)
        hi = pltpu.roll(lo, HEAD_DIM, 1)
    else:
        hi = jnp.where(lane >= HEAD_DIM, x, zero)
        lo = pltpu.roll(hi, HEAD_DIM, 1)
    return jnp.concatenate([lo, hi], axis=0)


def _split4(t, tri):
    zero = jnp.zeros_like(t[0])
    return jnp.concatenate(
        [jnp.where(tri, t[0], zero), jnp.where(tri, zero, t[0]), jnp.where(tri, t[1], zero), jnp.where(tri, zero, t[1])], axis=1)


def _sink_cols(sink_ref, kvh):
    first = lax.broadcasted_iota(jnp.int32, (2 * BLOCK, 1), 0) < BLOCK
    return [jnp.where(first, sink_ref[0, 4 * kvh + half], sink_ref[0, 4 * kvh + 2 + half]) for half in range(2)]


def _folded_probs(q2, k4, tri, ok, sks):
    s = _dot_nt(q2, k4)
    es, ss = [], []
    for half in range(2):
        s_h = s[:, 2 * half * BLOCK:2 * (half + 1) * BLOCK]
        sf = jnp.where(ok, jnp.where(tri, s_h[:, :BLOCK], s_h[:, BLOCK:]), NEG)
        m = jnp.maximum(jnp.max(sf, axis=-1, keepdims=True), sks[half])
        es.append(jnp.exp(sf - m))
        ss.append(jnp.exp(sks[half] - m))
    sums = _dot(jnp.concatenate(es, axis=0).astype(BF), jnp.ones((BLOCK, BLOCK), BF))
    out = []
    for half in range(2):
        inv = 1.0 / (sums[2 * half * BLOCK:2 * (half + 1) * BLOCK] + ss[half])
        out.append((es[half] * inv, ss[half] * inv[:, 0:1]))
    return out


def _attn_fwd(qkv, sink, name):
    lp = qkv.shape[0]
    nb = lp // BLOCK

    def body(sink_ref, q_ref, kvc_ref, kvp_ref, o_ref):
        i = pl.program_id(0)
        tri, ok = _fold_masks(i)
        kvc, kvp = kvc_ref[...], kvp_ref[...]
        kk = jnp.concatenate([kvp[:, :128], kvc[:, :128]], axis=0)
        vv = jnp.concatenate([kvp[:, 128:], kvc[:, 128:]], axis=0)
        for kvh in range(2):
            q2 = jnp.concatenate([q_ref[:, 256 * kvh:256 * kvh + 128], q_ref[:, 256 * kvh + 128:256 * kvh + 256]], axis=0)
            (p_e, _), (p_o, _) = _folded_probs(q2, _kv_operand(kk, kvh), tri, ok, _sink_cols(sink_ref, kvh))
            out = _dot(_split4([p_e.astype(BF), p_o.astype(BF)], tri), _kv_operand(vv, kvh))
            o_ref[:, 256 * kvh:256 * kvh + 128] = out[:BLOCK]
            o_ref[:, 256 * kvh + 128:256 * kvh + 256] = out[BLOCK:]

    return pl.pallas_call(
        body,
        name=name,
        grid=(nb,),
        in_specs=[
            pl.BlockSpec(memory_space=pltpu.SMEM),
            pl.BlockSpec((BLOCK, ATTN_W), lambda i: (i, 0)),
            pl.BlockSpec((BLOCK, 256), lambda i: (i, 2)),
            pl.BlockSpec((BLOCK, 256), lambda i: (jnp.maximum(i - 1, 0), 2)),
        ],
        out_specs=pl.BlockSpec((BLOCK, ATTN_W), lambda i: (i, 0)),
        out_shape=jax.ShapeDtypeStruct((lp, ATTN_W), F32),
        compiler_params=_params(),
    )(sink, qkv, qkv, qkv)


def _mix_out_fwd(bch, y_attn, h, conv_w, g_a, g_c, w_out, g_post, tm, name):
    lp = h.shape[0]

    def body(bch_ref, ya_ref, h_ref, cw_ref, ga_ref, gc_ref, w_ref, gp_ref, yc_ref, y_ref, z_ref, h2_ref, ext):
        i = pl.program_id(0)

        @pl.when(i == 0)
        def _():
            ext[0:8, :] = jnp.zeros((8, CONV_W), F32)

        b = bch_ref[:, 0:CONV_W]
        u = bch_ref[:, CONV_W:2 * CONV_W] * bch_ref[:, 2 * CONV_W:3 * CONV_W]
        ext[8:8 + tm, :] = u
        u1 = ext[7:7 + tm, :]
        u2 = ext[6:6 + tm, :]
        yc = cw_ref[0:1, :] * u2 + cw_ref[1:2, :] * u1 + cw_ref[2:3, :] * u
        ext[0:8, :] = u[tm - 8:tm, :]
        yc_ref[...] = yc
        ya = _rms_fwd(ya_ref[...], ga_ref[...]).astype(BF)
        yb = _rms_fwd(b * yc, gc_ref[...]).astype(BF)
        y_ref[:, 0:ATTN_W] = ya
        y_ref[:, ATTN_W:] = yb
        z = _dot(ya, w_ref[0:ATTN_W, :]) + _dot(yb, w_ref[ATTN_W:, :])
        z_ref[...] = z
        h2_ref[...] = h_ref[...] + _rms_fwd(z, gp_ref[...])

    row = lambda w: pl.BlockSpec((tm, w), lambda i: (i, 0))
    return pl.pallas_call(
        body,
        name=name,
        grid=(lp // tm,),
        in_specs=[
            row(3 * CONV_W), row(ATTN_W), row(D_MODEL), _full((8, CONV_W)), _full((1, ATTN_W)), _full((1, CONV_W)),
            _full((D_MODEL, D_MODEL)), _full((1, D_MODEL)),
        ],
        out_specs=[row(CONV_W), row(D_MODEL), row(D_MODEL), row(D_MODEL)],
        out_shape=[
            jax.ShapeDtypeStruct((lp, CONV_W), F32),
            jax.ShapeDtypeStruct((lp, D_MODEL), BF),
            jax.ShapeDtypeStruct((lp, D_MODEL), F32),
            jax.ShapeDtypeStruct((lp, D_MODEL), F32),
        ],
        scratch_shapes=[pltpu.VMEM((tm + 8, CONV_W), F32)],
        compiler_params=_params(),
    )(bch, y_attn, h, conv_w, g_a, g_c, w_out, g_post)


def _mlp_fwd(h2, g_pre, w_up_t, w_down, g_post, tm, name):
    lp = h2.shape[0]

    def body(h_ref, gp_ref, wu_ref, wd_ref, gq_ref, a_ref, up_ref, f_ref, h3_ref):
        h = h_ref[...]
        a = _rms_fwd(h, gp_ref[...]).astype(BF)
        a_ref[...] = a
        up = _dot_nt(a, wu_ref[...])
        up_ref[...] = up.astype(BF)
        act = jnp.square(jnp.maximum(up, 0.0)).astype(BF)
        f = _dot(act, wd_ref[...])
        f_ref[...] = f
        h3_ref[...] = h + _rms_fwd(f, gq_ref[...])

    row = lambda w: pl.BlockSpec((tm, w), lambda i: (i, 0))
    return pl.pallas_call(
        body,
        name=name,
        grid=(lp // tm,),
        in_specs=[row(D_MODEL), _full((1, D_MODEL)), _full((D_FF, D_MODEL)), _full((D_FF, D_MODEL)), _full((1, D_MODEL))],
        out_specs=[row(D_MODEL), row(D_FF), row(D_MODEL), row(D_MODEL)],
        out_shape=[
            jax.ShapeDtypeStruct((lp, D_MODEL), BF),
            jax.ShapeDtypeStruct((lp, D_FF), BF),
            jax.ShapeDtypeStruct((lp, D_MODEL), F32),
            jax.ShapeDtypeStruct((lp, D_MODEL), F32),
        ],
        compiler_params=_params(),
    )(h2, g_pre, w_up_t, w_down, g_post)


def _loss_head(h, target, name):
    lp = h.shape[0]
    nb = lp // BLOCK

    def body(h_ref, t_ref, dh_ref, ls_ref):
        i = pl.program_id(0)

        @pl.when(i == 0)
        def _():
            dh_ref[...] = jnp.zeros((BLOCK, D_MODEL), F32)
            ls_ref[...] = jnp.zeros((8, 128), F32)

        @pl.when(i > 0)
        def _():
            d = h_ref[...] - t_ref[...]
            dh_ref[...] = d * (1.0 / D_MODEL)
            ls_ref[...] += jnp.sum(d * d)

    dh, ls = pl.pallas_call(
        body,
        name=name,
        grid=(nb,),
        in_specs=[
            pl.BlockSpec((BLOCK, D_MODEL), lambda i: (i, 0)),
            pl.BlockSpec((BLOCK, D_MODEL), lambda i: (jnp.maximum(i - 1, 0), 0)),
        ],
        out_specs=[pl.BlockSpec((BLOCK, D_MODEL), lambda i: (i, 0)), pl.BlockSpec((8, 128), lambda i: (0, 0))],
        out_shape=[jax.ShapeDtypeStruct((lp, D_MODEL), F32), jax.ShapeDtypeStruct((8, 128), F32)],
        compiler_params=_params(),
    )(h, target)
    return dh, ls[0, 0] * (0.5 / D_MODEL)


def _mlp_bwd_dx(dh3, f, up, h2, w_down, w_up_t, g_post, g_pre, tm, name):
    lp = h2.shape[0]

    def body(dh3_ref, f_ref, up_ref, h2_ref, wd_ref, wu_ref, gq_ref, gp_ref, df_ref, dup_ref, dh2_ref, dg_ref):
        i = pl.program_id(0)

        @pl.when(i == 0)
        def _():
            dg_ref[...] = jnp.zeros((8, D_MODEL), F32)

        dh3 = dh3_ref[...]
        df, dgq = _rms_bwd(f_ref[...], gq_ref[...], dh3)
        dg_ref[ROW_MLP_POST:ROW_MLP_POST + 1, :] += dgq
        df = df.astype(BF)
        df_ref[...] = df
        dact = _dot_nt(df, wd_ref[...])
        dup = (dact * (2.0 * jnp.maximum(up_ref[...].astype(F32), 0.0))).astype(BF)
        dup_ref[...] = dup
        da = _dot(dup, wu_ref[...])
        dh, dgp = _rms_bwd(h2_ref[...], gp_ref[...], da)
        dg_ref[ROW_MLP_PRE:ROW_MLP_PRE + 1, :] += dgp
        dh2_ref[...] = dh3 + dh

    row = lambda w: pl.BlockSpec((tm, w), lambda i: (i, 0))
    return pl.pallas_call(
        body,
        name=name,
        grid=(lp // tm,),
        in_specs=[
            row(D_MODEL), row(D_MODEL), row(D_FF), row(D_MODEL), _full((D_FF, D_MODEL)), _full((D_FF, D_MODEL)),
            _full((1, D_MODEL)), _full((1, D_MODEL)),
        ],
        out_specs=[row(D_MODEL), row(D_FF), row(D_MODEL), _full_out((8, D_MODEL))],
        out_shape=[
            jax.ShapeDtypeStruct((lp, D_MODEL), BF),
            jax.ShapeDtypeStruct((lp, D_FF), BF),
            jax.ShapeDtypeStruct((lp, D_MODEL), F32),
            jax.ShapeDtypeStruct((8, D_MODEL), F32),
        ],
        compiler_params=_params(),
    )(dh3, f, up, h2, w_down, w_up_t, g_post, g_pre)


def _mlp_bwd_dw(up, df, dup, a2, tm, name):
    lp = up.shape[0]
    nt = lp // tm
    nj = D_FF // D_MODEL

    def body(up_ref, df_ref, dup_ref, a_ref, dwd_ref, dwu_ref, accd, accu):
        i = pl.program_id(1)

        @pl.when(i == 0)
        def _():
            accd[...] = jnp.zeros_like(accd)
            accu[...] = jnp.zeros_like(accu)

        act = jnp.square(jnp.maximum(up_ref[...].astype(F32), 0.0)).astype(BF)
        accd[...] += _dot_tn(act, df_ref[...])
        accu[...] += _dot_tn(dup_ref[...], a_ref[...])

        @pl.when(i == nt - 1)
        def _():
            dwd_ref[...] = accd[...].astype(BF)
            dwu_ref[...] = accu[...].astype(BF)

    return pl.pallas_call(
        body,
        name=name,
        grid=(nj, nt),
        in_specs=[
            pl.BlockSpec((tm, D_MODEL), lambda j, i: (i, j)),
            pl.BlockSpec((tm, D_MODEL), lambda j, i: (i, 0)),
            pl.BlockSpec((tm, D_MODEL), lambda j, i: (i, j)),
            pl.BlockSpec((tm, D_MODEL), lambda j, i: (i, 0)),
        ],
        out_specs=[pl.BlockSpec((D_MODEL, D_MODEL), lambda j, i: (j, 0)), pl.BlockSpec((D_MODEL, D_MODEL), lambda j, i: (j, 0))],
        out_shape=[jax.ShapeDtypeStruct((D_FF, D_MODEL), BF), jax.ShapeDtypeStruct((D_FF, D_MODEL), BF)],
        scratch_shapes=[pltpu.VMEM((D_MODEL, D_MODEL), F32), pltpu.VMEM((D_MODEL, D_MODEL), F32)],
        compiler_params=_params(("arbitrary", "arbitrary")),
    )(up, df, dup, a2)


def _mix_out_bwd(dh2, z, y_attn, yc, bch, w_out, g_post, g_a, g_c, conv_w, tm, name):
    lp = dh2.shape[0]
    nt = lp // tm

    def body(dh2_ref, z_ref, ya_ref, yc_ref, bch_ref, w_ref, gp_ref, ga_ref, gc_ref, cw_ref,
             dz_ref, dya_ref, dbch_ref, dg_ref, ext):
        i = pl.program_id(0)
        dcw_ref = dg_ref.at[ROW_CONV:ROW_CONV + 3, 0:CONV_W]

        @pl.when(i == 0)
        def _():
            ext[tm:tm + 8, :] = jnp.zeros((8, CONV_W), F32)
            dg_ref[...] = jnp.zeros((8, D_MODEL), F32)

        dz, dgp = _rms_bwd(z_ref[...], gp_ref[...], dh2_ref[...])
        dg_ref[ROW_MIX_POST:ROW_MIX_POST + 1, :] += dgp
        dz = dz.astype(BF)
        dz_ref[...] = dz
        dya_n = _dot_nt(dz, w_ref[0:ATTN_W, :])
        dyb_n = _dot_nt(dz, w_ref[ATTN_W:, :])
        dya, dga = _rms_bwd(ya_ref[...], ga_ref[...], dya_n)
        dg_ref[ROW_GROUP_G:ROW_GROUP_G + 1, 0:ATTN_W] += dga
        dya_ref[...] = dya
        b = bch_ref[:, 0:CONV_W]
        c = bch_ref[:, CONV_W:2 * CONV_W]
        hc = bch_ref[:, 2 * CONV_W:3 * CONV_W]
        yc_v = yc_ref[...]
        dyconv, dgc = _rms_bwd(b * yc_v, gc_ref[...], dyb_n)
        dg_ref[ROW_GROUP_G:ROW_GROUP_G + 1, ATTN_W:] += dgc
        dbch_ref[:, 0:CONV_W] = (dyconv * yc_v).astype(BF)
        dyc = dyconv * b
        ext[0:tm, :] = dyc
        d1 = ext[1:1 + tm, :]
        d2 = ext[2:2 + tm, :]
        du = cw_ref[2:3, :] * dyc + cw_ref[1:2, :] * d1 + cw_ref[0:1, :] * d2
        ext[tm:tm + 8, :] = dyc[0:8, :]
        dbch_ref[:, CONV_W:2 * CONV_W] = (du * hc).astype(BF)
        dbch_ref[:, 2 * CONV_W:3 * CONV_W] = (du * c).astype(BF)
        u = c * hc
        dcw_ref[0:1, :] += jnp.sum(u * d2, axis=0, keepdims=True)
        dcw_ref[1:2, :] += jnp.sum(u * d1, axis=0, keepdims=True)
        dcw_ref[2:3, :] += jnp.sum(u * dyc, axis=0, keepdims=True)

    row = lambda w: pl.BlockSpec((tm, w), lambda i: (nt - 1 - i, 0))
    return pl.pallas_call(
        body,
        name=name,
        grid=(nt,),
        in_specs=[
            row(D_MODEL), row(D_MODEL), row(ATTN_W), row(CONV_W), row(3 * CONV_W), _full((D_MODEL, D_MODEL)),
            _full((1, D_MODEL)), _full((1, ATTN_W)), _full((1, CONV_W)), _full((8, CONV_W)),
        ],
        out_specs=[row(D_MODEL), row(ATTN_W), row(3 * CONV_W), _full_out((8, D_MODEL))],
        out_shape=[
            jax.ShapeDtypeStruct((lp, D_MODEL), BF),
            jax.ShapeDtypeStruct((lp, ATTN_W), F32),
            jax.ShapeDtypeStruct((lp, 3 * CONV_W), BF),
            jax.ShapeDtypeStruct((8, D_MODEL), F32),
        ],
        scratch_shapes=[pltpu.VMEM((tm + 8, CONV_W), F32)],
        compiler_params=_params(),
    )(dh2, z, y_attn, yc, bch, w_out, g_post, g_a, g_c, conv_w)


def _attn_bwd(qkv, o, do, sink, rope, name):
    lp = qkv.shape[0]
    nb = lp // BLOCK

    def body(sink_ref, q_ref, kvc_ref, kvp_ref, o_ref, do_ref, rq_ref, rk_ref, dq_ref, dkv_ref, dsink_ref, carry):
        i = pl.program_id(0)

        @pl.when(i == 0)
        def _():
            carry[...] = jnp.zeros_like(carry)
            dsink_ref[...] = jnp.zeros((8, 128), F32)

        def finish(tot):
            dk = _rope_t(tot[:, :128], *_rope_coeffs(rk_ref[...]))
            dkv_ref[:, 0:128] = dk.astype(BF)
            dkv_ref[:, 128:256] = tot[:, 128:].astype(BF)

        @pl.when(i < nb)
        def _():
            tri, ok = _fold_masks(i)
            kvc, kvp = kvc_ref[...], kvp_ref[...]
            kk = jnp.concatenate([kvp[:, :128], kvc[:, :128]], axis=0)
            vv = jnp.concatenate([kvp[:, 128:], kvc[:, 128:]], axis=0)
            lane = lax.broadcasted_iota(jnp.int32, (BLOCK, 128), 1)
            lane2 = lax.broadcasted_iota(jnp.int32, (2 * BLOCK, 128), 1)
            first = lax.broadcasted_iota(jnp.int32, (2 * BLOCK, 1), 0) < BLOCK
            row_s = lax.broadcasted_iota(jnp.int32, (8, 128), 0)
            lane_s = jnp.where(row_s == ROW_SINK, lax.broadcasted_iota(jnp.int32, (8, 128), 1), -1)
            rope_q = _rope_coeffs(rq_ref[...])
            dsink = jnp.zeros((8, 128), F32)
            folded = []
            for kvh in range(2):
                c0 = 256 * kvh
                q2 = jnp.concatenate([q_ref[:, c0:c0 + 128], q_ref[:, c0 + 128:c0 + 256]], axis=0)
                do2 = jnp.concatenate([do_ref[:, c0:c0 + 128], do_ref[:, c0 + 128:c0 + 256]], axis=0)
                o2 = jnp.concatenate([o_ref[:, c0:c0 + 128], o_ref[:, c0 + 128:c0 + 256]], axis=0)
                k4, v4 = _kv_operand(kk, kvh), _kv_operand(vv, kvh)
                probs = _folded_probs(q2, k4, tri, ok, _sink_cols(sink_ref, kvh))
                prod = do2 * o2
                dob = do2.astype(BF)
                dp = _dot_nt(dob, v4)
                ds, pb = [], []
                for half in range(2):
                    p, ps = probs[half]
                    sel = (lane2 < HEAD_DIM) if half == 0 else (lane2 >= HEAD_DIM)
                    delta = jnp.sum(jnp.where(sel, prod, 0.0), axis=-1, keepdims=True)
                    dp_h = dp[:, 2 * half * BLOCK:2 * (half + 1) * BLOCK]
                    ds.append((p * (jnp.where(tri, dp_h[:, :BLOCK], dp_h[:, BLOCK:]) - delta)).astype(BF))
                    pb.append(p.astype(BF))
                    t = ps * delta
                    for jj in range(2):
                        part = -jnp.sum(jnp.where(first if jj == 0 else ~first, t, 0.0))
                        dsink = dsink + jnp.where(lane_s == 4 * kvh + 2 * jj + half, part, 0.0)
                ds4, p4 = _split4(ds, tri), _split4(pb, tri)
                dq2 = _dot(ds4, k4) * SCALE
                dq_ref[:, c0:c0 + 128] = _rope_t(dq2[:BLOCK], *rope_q).astype(BF)
                dq_ref[:, c0 + 128:c0 + 256] = _rope_t(dq2[BLOCK:], *rope_q).astype(BF)
                rk, rv = _dot_tn(ds4, q2), _dot_tn(p4, dob)
                own = (lane < HEAD_DIM) if kvh == 0 else (lane >= HEAD_DIM)
                group = []
                for r in (rk, rv):
                    for blk in range(2):
                        t = jnp.where(lane < HEAD_DIM, r[blk * BLOCK:(blk + 1) * BLOCK], r[(2 + blk) * BLOCK:(3 + blk) * BLOCK])
                        group.append(jnp.where(own, t + pltpu.roll(t, HEAD_DIM, 1), 0.0))
                folded.append(group)
            dsink_ref[...] += dsink
            dk_p, dk_c, dv_p, dv_c = [folded[0][t] + folded[1][t] for t in range(4)]
            finish(carry[...] + jnp.concatenate([dk_p, dv_p], axis=1))
            carry[...] = jnp.concatenate([dk_c, dv_c], axis=1)

        @pl.when(i == nb)
        def _():
            finish(carry[...])

    qi = lambda i: jnp.minimum(i, nb - 1)
    ki = lambda i: jnp.maximum(i - 1, 0)
    tab_q = pl.BlockSpec((BLOCK, 128), lambda i: (qi(i), 0))
    tab_k = pl.BlockSpec((BLOCK, 128), lambda i: (ki(i), 0))
    return pl.pallas_call(
        body,
        name=name,
        grid=(nb + 1,),
        in_specs=[
            pl.BlockSpec(memory_space=pltpu.SMEM),
            pl.BlockSpec((BLOCK, ATTN_W), lambda i: (qi(i), 0)),
            pl.BlockSpec((BLOCK, 256), lambda i: (qi(i), 2)),
            pl.BlockSpec((BLOCK, 256), lambda i: (jnp.maximum(qi(i) - 1, 0), 2)),
            pl.BlockSpec((BLOCK, ATTN_W), lambda i: (qi(i), 0)),
            pl.BlockSpec((BLOCK, ATTN_W), lambda i: (qi(i), 0)),
            tab_q, tab_k,
        ],
        out_specs=[
            pl.BlockSpec((BLOCK, ATTN_W), lambda i: (qi(i), 0)),
            pl.BlockSpec((BLOCK, 256), lambda i: (ki(i), 0)),
            pl.BlockSpec((8, 128), lambda i: (0, 0)),
        ],
        out_shape=[
            jax.ShapeDtypeStruct((lp, ATTN_W), BF),
            jax.ShapeDtypeStruct((lp, 256), BF),
            jax.ShapeDtypeStruct((8, 128), F32),
        ],
        scratch_shapes=[pltpu.VMEM((BLOCK, 256), F32)],
        compiler_params=_params(),
    )(sink, qkv, qkv, qkv, o, do, rope, rope)


def _in_proj_bwd_dx(dq, dkv, dbch, w_in_t, h, dh2, g, tm, name):
    lp = h.shape[0]

    def body(dq_ref, dkv_ref, dbch_ref, w_ref, h_ref, dh2_ref, g_ref, dh_ref, dg_ref):
        i = pl.program_id(0)

        @pl.when(i == 0)
        def _():
            dg_ref[...] = jnp.zeros((8, D_MODEL), F32)

        da = _dot(dq_ref[...], w_ref[0:512, :]) + _dot(dkv_ref[...], w_ref[512:768, :]) + _dot(dbch_ref[...], w_ref[768:, :])
        dh, dg = _rms_bwd(h_ref[...], g_ref[...], da)
        dg_ref[ROW_MIX_PRE:ROW_MIX_PRE + 1, :] += dg
        dh_ref[...] = dh2_ref[...] + dh

    row = lambda w: pl.BlockSpec((tm, w), lambda i: (i, 0))
    return pl.pallas_call(
        body,
        name=name,
        grid=(lp // tm,),
        in_specs=[row(ATTN_W), row(256), row(3 * CONV_W), _full((IN_W, D_MODEL)), row(D_MODEL), row(D_MODEL), _full((1, D_MODEL))],
        out_specs=[row(D_MODEL), _full_out((8, D_MODEL))],
        out_shape=[jax.ShapeDtypeStruct((lp, D_MODEL), F32), jax.ShapeDtypeStruct((8, D_MODEL), F32)],
        compiler_params=_params(),
    )(dq, dkv, dbch, w_in_t, h, dh2, g)


def _mix_bwd_dw(dq, dkv, dbch, a, y, dz, tm, name):
    lp = a.shape[0]
    nt = lp // tm

    def body(dq_ref, dkv_ref, dbch_ref, a_ref, y_ref, dz_ref, dwi_ref, dwo_ref, acci, acco):
        i = pl.program_id(0)

        @pl.when(i == 0)
        def _():
            acci[...] = jnp.zeros_like(acci)
            acco[...] = jnp.zeros_like(acco)

        a_v = a_ref[...]
        acci[0:512, :] += _dot_tn(dq_ref[...], a_v)
        acci[512:768, :] += _dot_tn(dkv_ref[...], a_v)
        acci[768:, :] += _dot_tn(dbch_ref[...], a_v)
        acco[...] += _dot_tn(y_ref[...], dz_ref[...])

        @pl.when(i == nt - 1)
        def _():
            dwi_ref[...] = acci[...].astype(BF)
            dwo_ref[...] = acco[...].astype(BF)

    row = lambda w: pl.BlockSpec((tm, w), lambda i: (i, 0))
    return pl.pallas_call(
        body,
        name=name,
        grid=(nt,),
        in_specs=[row(ATTN_W), row(256), row(3 * CONV_W), row(D_MODEL), row(D_MODEL), row(D_MODEL)],
        out_specs=[_full_out((IN_W, D_MODEL)), _full_out((D_MODEL, D_MODEL))],
        out_shape=[jax.ShapeDtypeStruct((IN_W, D_MODEL), BF), jax.ShapeDtypeStruct((D_MODEL, D_MODEL), BF)],
        scratch_shapes=[pltpu.VMEM((IN_W, D_MODEL), F32), pltpu.VMEM((D_MODEL, D_MODEL), F32)],
        compiler_params=_params(),
    )(dq, dkv, dbch, a, y, dz)


def _mesh_place():
    x, y, c = lax.axis_index("x"), lax.axis_index("y"), lax.axis_index("c")
    return x, y, c, 4 * x + 2 * y + c


def _peer(x, y, c, k):
    px = 1 - x if k & 4 else x
    py = 1 - y if k & 2 else y
    pc = 1 - c if k & 1 else c
    return (px, py, pc), 4 * px + 2 * py + pc


def _gather_weights(shards, small):
    n = len(shards)
    pieces = [(t, l) for t in range(n) for l in range(DEPTH)]
    npc = len(pieces) + 1

    def body(*refs):
        ins, small_ref = refs[:n], refs[n]
        outs, small_out = refs[n + 1:2 * n + 1], refs[2 * n + 1]
        stage = refs[2 * n + 2:3 * n + 2]
        send_sems, recv_sems, local_sems = refs[3 * n + 2:]
        x, y, c, me = _mesh_place()
        for t in range(n):
            stage[t][...] = ins[t][...].astype(BF)

        def src_dst(p, slot):
            if p == npc - 1:
                return small_ref, small_out.at[slot]
            t, l = pieces[p]
            return stage[t].at[l], outs[t].at[l, slot]

        own = []
        for p in range(npc):
            s, d = src_dst(p, me)
            cp = pltpu.make_async_copy(s, d, local_sems.at[p])
            cp.start()
            own.append(cp)
        sent = []
        for k in range(1, N_DEV):
            peer, _ = _peer(x, y, c, k)
            for p in range(npc):
                s, d = src_dst(p, me)
                cp = pltpu.make_async_remote_copy(
                    src_ref=s, dst_ref=d, send_sem=send_sems.at[k - 1, p], recv_sem=recv_sems.at[k - 1, p],
                    device_id=peer, device_id_type=MESH)
                cp.start()
                sent.append(cp)
        for k in range(1, N_DEV):
            peer, pidx = _peer(x, y, c, k)
            for p in range(npc):
                s, d = src_dst(p, pidx)
                pltpu.make_async_remote_copy(
                    src_ref=s, dst_ref=d, send_sem=send_sems.at[k - 1, p], recv_sem=recv_sems.at[k - 1, p],
                    device_id=peer, device_id_type=MESH).wait_recv()
        for cp in sent:
            cp.wait_send()
        for cp in own:
            cp.wait()

    vmem = pl.BlockSpec(memory_space=pltpu.VMEM)
    hbm = pl.BlockSpec(memory_space=pl.ANY)
    out_shape = [jax.ShapeDtypeStruct((DEPTH, N_DEV) + s.shape[1:], BF) for s in shards]
    out_shape.append(jax.ShapeDtypeStruct((N_DEV,) + small.shape, F32))
    return pl.pallas_call(
        body,
        name="gather_weights",
        in_specs=[vmem] * (n + 1),
        out_specs=[hbm] * (n + 1),
        out_shape=out_shape,
        scratch_shapes=[pltpu.VMEM(s.shape, BF) for s in shards]
        + [pltpu.SemaphoreType.DMA((N_DEV - 1, npc)), pltpu.SemaphoreType.DMA((N_DEV - 1, npc)), pltpu.SemaphoreType.DMA((npc,))],
        compiler_params=pltpu.CompilerParams(vmem_limit_bytes=VMEM_LIMIT),
    )(*shards, small)


def _scatter_grads(grads):
    n = len(grads)
    pieces = [(t, l) for t in range(n) for l in range(DEPTH)]
    npc = len(pieces)

    def body(*refs):
        ins, outs = refs[:n], refs[n:2 * n]
        send_sems, recv_sems, local_sems = refs[2 * n:]
        x, y, c, me = _mesh_place()
        own = []
        for p, (t, l) in enumerate(pieces):
            cp = pltpu.make_async_copy(ins[t].at[l, me], outs[t].at[l, me], local_sems.at[p])
            cp.start()
            own.append(cp)
        sent = []
        for k in range(1, N_DEV):
            peer, pidx = _peer(x, y, c, k)
            for p, (t, l) in enumerate(pieces):
                cp = pltpu.make_async_remote_copy(
                    src_ref=ins[t].at[l, pidx], dst_ref=outs[t].at[l, me], send_sem=send_sems.at[k - 1, p],
                    recv_sem=recv_sems.at[k - 1, p], device_id=peer, device_id_type=MESH)
                cp.start()
                sent.append(cp)
        for k in range(1, N_DEV):
            peer, pidx = _peer(x, y, c, k)
            for p, (t, l) in enumerate(pieces):
                pltpu.make_async_remote_copy(
                    src_ref=ins[t].at[l, pidx], dst_ref=outs[t].at[l, pidx], send_sem=send_sems.at[k - 1, p],
                    recv_sem=recv_sems.at[k - 1, p], device_id=peer, device_id_type=MESH).wait_recv()
        for cp in sent:
            cp.wait_send()
        for cp in own:
            cp.wait()

    hbm = pl.BlockSpec(memory_space=pl.ANY)
    return pl.pallas_call(
        body,
        name="scatter_grads",
        in_specs=[hbm] * n,
        out_specs=[hbm] * n,
        out_shape=[jax.ShapeDtypeStruct(g.shape, g.dtype) for g in grads],
        scratch_shapes=[pltpu.SemaphoreType.DMA((N_DEV - 1, npc)), pltpu.SemaphoreType.DMA((N_DEV - 1, npc)),
                        pltpu.SemaphoreType.DMA((npc,))],
    )(*grads)


def _sum_small(part):
    def body(part_ref, out_ref, land, send_sems, recv_sems):
        x, y, c, me = _mesh_place()
        land[me] = part_ref[...]
        sent = []
        for k in range(1, N_DEV):
            peer, _ = _peer(x, y, c, k)
            cp = pltpu.make_async_remote_copy(
                src_ref=part_ref, dst_ref=land.at[me], send_sem=send_sems.at[k - 1], recv_sem=recv_sems.at[k - 1],
                device_id=peer, device_id_type=MESH)
            cp.start()
            sent.append(cp)
        for k in range(1, N_DEV):
            peer, pidx = _peer(x, y, c, k)
            pltpu.make_async_remote_copy(
                src_ref=part_ref, dst_ref=land.at[pidx], send_sem=send_sems.at[k - 1], recv_sem=recv_sems.at[k - 1],
                device_id=peer, device_id_type=MESH).wait_recv()
        for cp in sent:
            cp.wait_send()
        acc = land[0]
        for d in range(1, N_DEV):
            acc = acc + land[d]
        out_ref[...] = acc

    vmem = pl.BlockSpec(memory_space=pltpu.VMEM)
    return pl.pallas_call(
        body,
        name="sum_small",
        in_specs=[vmem],
        out_specs=vmem,
        out_shape=jax.ShapeDtypeStruct(part.shape, F32),
        scratch_shapes=[pltpu.VMEM((N_DEV,) + part.shape, F32), pltpu.SemaphoreType.DMA((N_DEV - 1,)),
                        pltpu.SemaphoreType.DMA((N_DEV - 1,))],
    )(part)


def _adamw(w, g, m, v):
    m = ADAM_B1 * m + (1.0 - ADAM_B1) * g
    v = ADAM_B2 * v + (1.0 - ADAM_B2) * jnp.square(g)
    m_hat = m / (1.0 - ADAM_B1 ** ADAM_STEP)
    v_hat = v / (1.0 - ADAM_B2 ** ADAM_STEP)
    delta = -ADAM_LR * (m_hat / (jnp.sqrt(v_hat) + ADAM_EPS) + ADAM_WD * w)
    return delta, m, v


def _sum_parts(recv, tr, name):
    _, _, r, wd = recv.shape

    def body(r_ref, g_ref):
        acc = r_ref[0, 0].astype(F32)
        for d in range(1, N_DEV):
            acc = acc + r_ref[0, d].astype(F32)
        g_ref[0] = acc

    return pl.pallas_call(
        body,
        name=name,
        grid=(DEPTH, r // tr),
        in_specs=[pl.BlockSpec((1, N_DEV, tr, wd), lambda l, i: (l, 0, i, 0))],
        out_specs=pl.BlockSpec((1, tr, wd), lambda l, i: (l, i, 0)),
        out_shape=jax.ShapeDtypeStruct((DEPTH, r, wd), F32),
        compiler_params=_params(("arbitrary", "arbitrary")),
    )(recv)


def _sum_adamw(recv, w, m, v, tr, name):
    _, _, r, wd = recv.shape

    def body(r_ref, w_ref, m_ref, v_ref, g_ref, d_ref, mo_ref, vo_ref):
        g = r_ref[0, 0].astype(F32)
        for d in range(1, N_DEV):
            g = g + r_ref[0, d].astype(F32)
        g_ref[0] = g
        d_ref[0], mo_ref[0], vo_ref[0] = _adamw(w_ref[0], g, m_ref[0], v_ref[0])

    blk = pl.BlockSpec((1, tr, wd), lambda l, i: (l, i, 0))
    shape = jax.ShapeDtypeStruct((DEPTH, r, wd), F32)
    return pl.pallas_call(
        body,
        name=name,
        grid=(DEPTH, r // tr),
        in_specs=[pl.BlockSpec((1, N_DEV, tr, wd), lambda l, i: (l, 0, i, 0)), blk, blk, blk],
        out_specs=[blk] * 4,
        out_shape=[shape] * 4,
        compiler_params=_params(("arbitrary", "arbitrary")),
    )(recv, w, m, v)


def _adamw_rows(w, g, m, v, tr, name):
    _, r, wd = w.shape

    def body(w_ref, g_ref, m_ref, v_ref, d_ref, mo_ref, vo_ref):
        d_ref[0], mo_ref[0], vo_ref[0] = _adamw(w_ref[0], g_ref[0], m_ref[0], v_ref[0])

    blk = pl.BlockSpec((1, tr, wd), lambda l, i: (l, i, 0))
    shape = jax.ShapeDtypeStruct(w.shape, F32)
    return pl.pallas_call(
        body,
        name=name,
        grid=(DEPTH, r // tr),
        in_specs=[blk] * 4,
        out_specs=[blk] * 3,
        out_shape=[shape] * 3,
        compiler_params=_params(("arbitrary", "arbitrary")),
    )(w, g, m, v)


def _adamw_small(ws, gs, ms, vs):
    n = len(ws)

    def body(*refs):
        w_r, g_r, m_r, v_r = refs[:n], refs[n:2 * n], refs[2 * n:3 * n], refs[3 * n:4 * n]
        d_o, m_o, v_o = refs[4 * n:5 * n], refs[5 * n:6 * n], refs[6 * n:7 * n]
        for t in range(n):
            d_o[t][...], m_o[t][...], v_o[t][...] = _adamw(w_r[t][...], g_r[t][...], m_r[t][...], v_r[t][...])

    vmem = pl.BlockSpec(memory_space=pltpu.VMEM)
    shapes = [jax.ShapeDtypeStruct(w.shape, F32) for w in ws]
    outs = pl.pallas_call(
        body,
        name="adamw_small",
        in_specs=[vmem] * (4 * n),
        out_specs=[vmem] * (3 * n),
        out_shape=shapes * 3,
    )(*ws, *gs, *ms, *vs)
    return outs[:n], outs[n:2 * n], outs[2 * n:]


def kernel(x, meta_tokens, mix_pre_g, w_in, conv_w, sinks, attn_out_g, conv_out_g, w_out, mix_post_g, mlp_pre_g, w_up, w_down, mlp_post_g, loss_target, m_meta_tokens, m_mix_pre_g, m_w_in, m_conv_w, m_sinks, m_attn_out_g, m_conv_out_g, m_w_out, m_mix_post_g, m_mlp_pre_g, m_w_up, m_w_down, m_mlp_post_g, v_meta_tokens, v_mix_pre_g, v_w_in, v_conv_w, v_sinks, v_attn_out_g, v_conv_out_g, v_w_out, v_mix_post_g, v_mlp_pre_g, v_w_up, v_w_down, v_mlp_post_g):
    seq = x.shape[1]
    lp = BLOCK + seq
    tm = _row_tile(lp)
    tm_mlp = _row_tile(lp, (320, 256, 128))
    me = 4 * lax.axis_index("x") + 2 * lax.axis_index("y") + lax.axis_index("c")
    cshard = CONV_W // N_DEV
    mshard = D_MODEL // N_DEV

    small = jnp.zeros((24, 128), F32)
    small = small.at[0:N_META, :].set(meta_tokens)
    small = small.at[N_META:N_META + 6, 0:cshard].set(conv_w.reshape(6, cshard))
    shards = [jnp.swapaxes(w_in, 1, 2), w_out, jnp.swapaxes(w_up, 1, 2), w_down]
    g_in, g_out, g_up, g_down, g_small = _gather_weights(shards, small)
    w_in_t = g_in.reshape(DEPTH, IN_W, D_MODEL)
    w_out_f = g_out.reshape(DEPTH, D_MODEL, D_MODEL)
    w_up_t = g_up.reshape(DEPTH, D_FF, D_MODEL)
    w_down_f = g_down.reshape(DEPTH, D_FF, D_MODEL)
    meta_full = jnp.swapaxes(g_small[:, 0:N_META, :], 0, 1).reshape(N_META, D_MODEL)
    cw = g_small[:, N_META:N_META + 6, 0:cshard].reshape(N_DEV, DEPTH, 3, cshard)
    cw = jnp.transpose(cw, (1, 2, 0, 3)).reshape(DEPTH, 3, CONV_W)
    conv_full = jnp.concatenate([cw, jnp.zeros((DEPTH, 5, CONV_W), F32)], axis=1)

    rope = _rope_table(lp)
    row1 = lambda a, l: a[l].reshape(1, -1)

    h = jnp.concatenate([jnp.zeros((LEAD_PAD, D_MODEL), F32), meta_full, x[0]], axis=0)
    saved = []
    for l in range(DEPTH):
        a, qkv, bch = _in_proj_fwd(h, row1(mix_pre_g, l), w_in_t[l], rope, tm, f"in_proj_fwd_{l}")
        y_attn = _attn_fwd(qkv, row1(sinks, l), f"attn_fwd_{l}")
        yc, y, z, h2 = _mix_out_fwd(bch, y_attn, h, conv_full[l], row1(attn_out_g, l), row1(conv_out_g, l), w_out_f[l],
                                    row1(mix_post_g, l), tm, f"mix_out_fwd_{l}")
        a2, up, f, h3 = _mlp_fwd(h2, row1(mlp_pre_g, l), w_up_t[l], w_down_f[l], row1(mlp_post_g, l), tm_mlp, f"mlp_fwd_{l}")
        saved.append((h, a, qkv, bch, y_attn, yc, y, z, h2, a2, up, f))
        h = h3
    dh, loss_part = _loss_head(h, loss_target[0], "loss_head")

    gw = [None] * DEPTH
    gsmall = [None] * DEPTH
    for l in reversed(range(DEPTH)):
        h0, a, qkv, bch, y_attn, yc, y, z, h2, a2, up, f = saved[l]
        df, dup, dh2, dg_mlp = _mlp_bwd_dx(
            dh, f, up, h2, w_down_f[l], w_up_t[l], row1(mlp_post_g, l), row1(mlp_pre_g, l), tm_mlp, f"mlp_bwd_dx_{l}")
        dw_down, dw_up_t = _mlp_bwd_dw(up, df, dup, a2, tm, f"mlp_bwd_dw_{l}")
        dz, dya, dbch, dg_mix = _mix_out_bwd(
            dh2, z, y_attn, yc, bch, w_out_f[l], row1(mix_post_g, l), row1(attn_out_g, l), row1(conv_out_g, l),
            conv_full[l], tm, f"mix_out_bwd_{l}")
        dq, dkv, dsink = _attn_bwd(qkv, y_attn, dya, row1(sinks, l), rope, f"attn_bwd_{l}")
        dh, dg_in = _in_proj_bwd_dx(dq, dkv, dbch, w_in_t[l], h0, dh2, row1(mix_pre_g, l), tm, f"in_proj_bwd_dx_{l}")
        dw_in_t, dw_out = _mix_bwd_dw(dq, dkv, dbch, a, y, dz, tm_mlp, f"mix_bwd_dw_{l}")
        gw[l] = (dw_in_t, dw_out, dw_up_t, dw_down)
        tile_a = dg_mlp + dg_in + jnp.pad(dsink, ((0, 0), (0, D_MODEL - 128)))
        gsmall[l] = (tile_a, dg_mix)
    grad_x = dh[BLOCK:][None]

    loss_tile = jnp.zeros((8, D_MODEL), F32).at[ROW_LOSS, 0].set(loss_part)
    tot = _sum_small(jnp.concatenate(
        [gsmall[0][0] + loss_tile, gsmall[0][1], gsmall[1][0], gsmall[1][1], dh[LEAD_PAD:BLOCK]], axis=0))
    loss = tot[ROW_LOSS, 0]
    ta = [tot[16 * l:16 * l + 8] for l in range(DEPTH)]
    tb = [tot[16 * l + 8:16 * l + 16] for l in range(DEPTH)]
    pick = lambda tiles, r0, r1, c0, c1: jnp.stack([t[r0:r1, c0:c1] for t in tiles])
    g_mlp_post = pick(ta, ROW_MLP_POST, ROW_MLP_POST + 1, 0, D_MODEL).reshape(DEPTH, D_MODEL)
    g_mlp_pre = pick(ta, ROW_MLP_PRE, ROW_MLP_PRE + 1, 0, D_MODEL).reshape(DEPTH, D_MODEL)
    g_mix_pre = pick(ta, ROW_MIX_PRE, ROW_MIX_PRE + 1, 0, D_MODEL).reshape(DEPTH, D_MODEL)
    g_sinks = pick(ta, ROW_SINK, ROW_SINK + 1, 0, N_Q_HEADS).reshape(DEPTH, N_Q_HEADS)
    g_mix_post = pick(tb, ROW_MIX_POST, ROW_MIX_POST + 1, 0, D_MODEL).reshape(DEPTH, D_MODEL)
    g_attn_out = pick(tb, ROW_GROUP_G, ROW_GROUP_G + 1, 0, ATTN_W).reshape(DEPTH, ATTN_W)
    g_conv_out = pick(tb, ROW_GROUP_G, ROW_GROUP_G + 1, ATTN_W, D_MODEL).reshape(DEPTH, CONV_W)
    g_conv_full = pick(tb, ROW_CONV, ROW_CONV + 3, 0, CONV_W)
    g_conv = lax.dynamic_slice_in_dim(g_conv_full, me * cshard, cshard, axis=2)
    g_meta = lax.dynamic_slice_in_dim(tot[16 * DEPTH:16 * DEPTH + N_META], me * mshard, mshard, axis=1)

    parts = [jnp.stack([gw[l][t] for l in range(DEPTH)]) for t in range(4)]
    parts = [p.reshape(DEPTH, N_DEV, p.shape[1] // N_DEV, D_MODEL) for p in parts]
    r_in, r_out, r_up, r_down = _scatter_grads(parts)
    g_w_in = jnp.swapaxes(_sum_parts(r_in, 96, "sum_w_in"), 1, 2)
    g_w_up = jnp.swapaxes(_sum_parts(r_up, 128, "sum_w_up"), 1, 2)
    d_w_in, nm_w_in, nv_w_in = _adamw_rows(w_in, g_w_in, m_w_in, v_w_in, 256, "adamw_w_in")
    d_w_up, nm_w_up, nv_w_up = _adamw_rows(w_up, g_w_up, m_w_up, v_w_up, 256, "adamw_w_up")
    g_w_out, d_w_out, nm_w_out, nv_w_out = _sum_adamw(r_out, w_out, m_w_out, v_w_out, 128, "adamw_w_out")
    g_w_down, d_w_down, nm_w_down, nv_w_down = _sum_adamw(r_down, w_down, m_w_down, v_w_down, 128, "adamw_w_down")

    ws = [meta_tokens, mix_pre_g, conv_w.reshape(6, cshard), sinks, attn_out_g, conv_out_g, mix_post_g, mlp_pre_g, mlp_post_g]
    gs = [g_meta, g_mix_pre, g_conv.reshape(6, cshard), g_sinks, g_attn_out, g_conv_out, g_mix_post, g_mlp_pre, g_mlp_post]
    ms = [m_meta_tokens, m_mix_pre_g, m_conv_w.reshape(6, cshard), m_sinks, m_attn_out_g, m_conv_out_g, m_mix_post_g,
          m_mlp_pre_g, m_mlp_post_g]
    vs = [v_meta_tokens, v_mix_pre_g, v_conv_w.reshape(6, cshard), v_sinks, v_attn_out_g, v_conv_out_g, v_mix_post_g,
          v_mlp_pre_g, v_mlp_post_g]
    ds, nms, nvs = _adamw_small(ws, gs, ms, vs)

    def order(meta, mix_pre, cv, sk, a_out, c_out, mix_post, mlp_pre, mlp_post, win, wout, wup, wdown):
        return [meta, mix_pre, win, cv.reshape(DEPTH, 3, cshard), sk, a_out, c_out, wout, mix_post, mlp_pre, wup, wdown, mlp_post]

    grads = order(*gs, g_w_in, g_w_out, g_w_up, g_w_down)
    deltas = order(*ds, d_w_in, d_w_out, d_w_up, d_w_down)
    new_m = order(*nms, nm_w_in, nm_w_out, nm_w_up, nm_w_down)
    new_v = order(*nvs, nv_w_in, nv_w_out, nv_w_up, nv_w_down)
    return (loss, grad_x, *grads, *deltas, *new_m, *new_v)
```

```python
import functools

import jax
import jax.numpy as jnp
from jax import lax
from jax.experimental import pallas as pl
from jax.experimental.pallas import tpu as pltpu

F32 = jnp.float32
BF = jnp.bfloat16

D_MODEL = 1024
ATTN_W = 512
CONV_W = 512
KV_W = 128
HEAD_DIM = 64
N_Q_HEADS = 8
ROT_DIM = 16
D_FF = 4096
IN_W = 2304
N_META = 16
BLOCK = 128
LEAD_PAD = BLOCK - N_META
ROPE_THETA = 500000.0
EPS = 1e-6
N_DEV = 8
DEPTH = 2
NEG = -1e30
SCALE = HEAD_DIM ** -0.5

ADAM_LR = 0.001
ADAM_B1 = 0.9
ADAM_B2 = 0.999
ADAM_EPS = 1e-08
ADAM_WD = 0.01
ADAM_STEP = 10

ROW_MLP_POST, ROW_MLP_PRE, ROW_MIX_PRE, ROW_SINK, ROW_LOSS = 0, 1, 2, 3, 4
ROW_MIX_POST, ROW_GROUP_G, ROW_CONV = 0, 1, 2

VMEM_LIMIT = 56 * 1024 * 1024
MESH = pl.DeviceIdType.MESH


def _dot(a, b):
    return jnp.dot(a, b, preferred_element_type=F32)


def _dot_nt(a, b):
    return lax.dot_general(a, b, (((1,), (1,)), ((), ())), preferred_element_type=F32)


def _dot_tn(a, b):
    return lax.dot_general(a, b, (((0,), (0,)), ((), ())), preferred_element_type=F32)


def _rms_fwd(x, g):
    r = lax.rsqrt(jnp.mean(x * x, axis=-1, keepdims=True) + EPS)
    return x * r * g


def _rms_bwd(x, g, dy):
    r = lax.rsqrt(jnp.mean(x * x, axis=-1, keepdims=True) + EPS)
    xh = x * r
    t = dy * g
    dx = r * (t - xh * jnp.mean(t * xh, axis=-1, keepdims=True))
    dg = jnp.sum(dy * xh, axis=0, keepdims=True)
    return dx, dg


def _row_tile(lp, cands=(640, 512, 384, 256, 128)):
    for t in cands:
        if lp % t == 0:
            return t
    raise ValueError(f"row count {lp} is not a multiple of 128")


def _full(shape):
    n = len(shape)
    return pl.BlockSpec(shape, lambda *_: (0,) * n, pipeline_mode=pl.Buffered(1))


def _full_out(shape):
    n = len(shape)
    return pl.BlockSpec(shape, lambda *_: (0,) * n)


def _params(sem=("arbitrary",)):
    return pltpu.CompilerParams(dimension_semantics=sem, vmem_limit_bytes=VMEM_LIMIT)


def _rope_table(lp):
    half = ROT_DIM // 2
    pos = jnp.maximum(jnp.arange(lp) - LEAD_PAD, 0).astype(F32)
    inv_freq = jnp.power(jnp.float32(ROPE_THETA), -jnp.arange(0, ROT_DIM, 2, dtype=F32) / ROT_DIM)
    ang_t = jnp.concatenate([inv_freq, inv_freq])[:, None] * pos[None, :]
    row = lax.broadcasted_iota(jnp.int32, (ROT_DIM, lp), 0)
    cs_t = jnp.where(row < half, jnp.cos(ang_t), jnp.sin(ang_t))
    return jnp.pad(cs_t.T, ((0, 0), (0, 128 - ROT_DIM)))


def _rope_coeffs(t):
    half = ROT_DIM // 2
    lane = lax.broadcasted_iota(jnp.int32, t.shape, 1)
    cos_a = jnp.where(lane < half, t, 0.0)
    sin_a = pltpu.roll(jnp.where((lane >= half) & (lane < ROT_DIM), t, 0.0), 128 - half, 1)
    c = cos_a + pltpu.roll(cos_a, half, 1) + jnp.where((lane >= ROT_DIM) & (lane < HEAD_DIM), 1.0, 0.0)
    s2 = pltpu.roll(sin_a, half, 1)
    both = lambda u: u + pltpu.roll(u, HEAD_DIM, 1)
    return both(c), both(-sin_a), both(s2)


def _rope(t, c, s1, s2):
    return t * c + pltpu.roll(t, BLOCK - 8, 1) * s1 + pltpu.roll(t, 8, 1) * s2


def _rope_t(dt, c, s1, s2):
    return dt * c + pltpu.roll(dt * s1, 8, 1) + pltpu.roll(dt * s2, BLOCK - 8, 1)


def _in_proj_fwd(h, g, w_in_t, rope, tm, name, exch=None):
    lp = h.shape[0]

    def body(h_ref, g_ref, w_ref, rope_ref, a_ref, qkv_ref, bch_ref):
        a = _rms_fwd(h_ref[...], g_ref[...]).astype(BF)
        a_ref[...] = a
        proj = _dot_nt(a, w_ref[...])
        c, s1, s2 = _rope_coeffs(rope_ref[...])
        for j in range(5):
            t = _rope(proj[:, j * 128:(j + 1) * 128], c, s1, s2)
            qkv_ref[:, j * 128:(j + 1) * 128] = (t * SCALE if j < 4 else t).astype(BF)
        qkv_ref[:, 640:768] = proj[:, 640:768].astype(BF)
        bch_ref[...] = proj[:, 768:]

    row = lambda w: pl.BlockSpec((tm, w), lambda i: (i, 0))
    return _call(
        body, exch,
        name=name,
        grid=(lp // tm,),
        in_specs=[row(D_MODEL), _full((1, D_MODEL)), _full((IN_W, D_MODEL)), row(128)],
        out_specs=[row(D_MODEL), row(768), row(3 * CONV_W)],
        out_shape=[
            jax.ShapeDtypeStruct((lp, D_MODEL), BF),
            jax.ShapeDtypeStruct((lp, 768), BF),
            jax.ShapeDtypeStruct((lp, 3 * CONV_W), F32),
        ],
        compiler_params=_params(),
    )(h, g, w_in_t, rope)


def _fold_masks(i):
    r = lax.broadcasted_iota(jnp.int32, (2 * BLOCK, BLOCK), 0) & (BLOCK - 1)
    c = lax.broadcasted_iota(jnp.int32, (2 * BLOCK, BLOCK), 1)
    tri = c > r
    ok = jnp.where(tri, (i - 1) * BLOCK + c, i * BLOCK + c) >= LEAD_PAD
    return tri, ok


def _kv_operand(x, kvh):
    lane = lax.broadcasted_iota(jnp.int32, x.shape, 1)
    zero = jnp.zeros_like(x)
    if kvh == 0:
        lo = jnp.where(lane < HEAD_DIM, x, zero)
        hi = pltpu.roll(lo, HEAD_DIM, 1)
    else:
        hi = jnp.where(lane >= HEAD_DIM, x, zero)
        lo = pltpu.roll(hi, HEAD_DIM, 1)
    return jnp.concatenate([lo, hi], axis=0)


def _split4(t, tri):
    zero = jnp.zeros_like(t[0])
    return jnp.concatenate(
        [jnp.where(tri, t[0], zero), jnp.where(tri, zero, t[0]), jnp.where(tri, t[1], zero), jnp.where(tri, zero, t[1])], axis=1)


def _sink_cols(sink_ref, kvh):
    first = lax.broadcasted_iota(jnp.int32, (2 * BLOCK, 1), 0) < BLOCK
    return [jnp.where(first, sink_ref[0, 4 * kvh + half], sink_ref[0, 4 * kvh + 2 + half]) for half in range(2)]


def _folded_probs(q2, k4, tri, ok, sks):
    s = _dot_nt(q2, k4)
    es, ss = [], []
    for half in range(2):
        s_h = s[:, 2 * half * BLOCK:2 * (half + 1) * BLOCK]
        sf = jnp.where(ok, jnp.where(tri, s_h[:, :BLOCK], s_h[:, BLOCK:]), NEG)
        m = jnp.maximum(jnp.max(sf, axis=-1, keepdims=True), sks[half])
        es.append(jnp.exp(sf - m))
        ss.append(jnp.exp(sks[half] - m))
    sums = _dot(jnp.concatenate(es, axis=0).astype(BF), jnp.ones((BLOCK, BLOCK), BF))
    out = []
    for half in range(2):
        inv = 1.0 / (sums[2 * half * BLOCK:2 * (half + 1) * BLOCK] + ss[half])
        out.append((es[half] * inv, ss[half] * inv[:, 0:1]))
    return out


def _attn_fwd(qkv, sink, name, exch=None):
    lp = qkv.shape[0]
    nb = lp // BLOCK

    def body(sink_ref, q_ref, kvc_ref, kvp_ref, o_ref):
        i = pl.program_id(0)
        tri, ok = _fold_masks(i)
        kvc, kvp = kvc_ref[...], kvp_ref[...]
        kk = jnp.concatenate([kvp[:, :128], kvc[:, :128]], axis=0)
        vv = jnp.concatenate([kvp[:, 128:], kvc[:, 128:]], axis=0)
        for kvh in range(2):
            q2 = jnp.concatenate([q_ref[:, 256 * kvh:256 * kvh + 128], q_ref[:, 256 * kvh + 128:256 * kvh + 256]], axis=0)
            (p_e, _), (p_o, _) = _folded_probs(q2, _kv_operand(kk, kvh), tri, ok, _sink_cols(sink_ref, kvh))
            out = _dot(_split4([p_e.astype(BF), p_o.astype(BF)], tri), _kv_operand(vv, kvh))
            o_ref[:, 256 * kvh:256 * kvh + 128] = out[:BLOCK]
            o_ref[:, 256 * kvh + 128:256 * kvh + 256] = out[BLOCK:]

    return _call(
        body, exch,
        name=name,
        grid=(nb,),
        in_specs=[
            pl.BlockSpec(memory_space=pltpu.SMEM),
            pl.BlockSpec((BLOCK, ATTN_W), lambda i: (i, 0)),
            pl.BlockSpec((BLOCK, 256), lambda i: (i, 2)),
            pl.BlockSpec((BLOCK, 256), lambda i: (jnp.maximum(i - 1, 0), 2)),
        ],
        out_specs=[pl.BlockSpec((BLOCK, ATTN_W), lambda i: (i, 0))],
        out_shape=[jax.ShapeDtypeStruct((lp, ATTN_W), F32)],
        compiler_params=_params(),
    )(sink, qkv, qkv, qkv)


def _mix_out_fwd(bch, y_attn, h, conv_w, g_a, g_c, w_out, g_post, tm, name, exch=None):
    lp = h.shape[0]

    def body(bch_ref, ya_ref, h_ref, cw_ref, ga_ref, gc_ref, w_ref, gp_ref, yc_ref, y_ref, z_ref, h2_ref, ext):
        i = pl.program_id(0)

        @pl.when(i == 0)
        def _():
            ext[0:8, :] = jnp.zeros((8, CONV_W), F32)

        b = bch_ref[:, 0:CONV_W]
        u = bch_ref[:, CONV_W:2 * CONV_W] * bch_ref[:, 2 * CONV_W:3 * CONV_W]
        ext[8:8 + tm, :] = u
        u1 = ext[7:7 + tm, :]
        u2 = ext[6:6 + tm, :]
        yc = cw_ref[0:1, :] * u2 + cw_ref[1:2, :] * u1 + cw_ref[2:3, :] * u
        ext[0:8, :] = u[tm - 8:tm, :]
        yc_ref[...] = yc
        ya = _rms_fwd(ya_ref[...], ga_ref[...]).astype(BF)
        yb = _rms_fwd(b * yc, gc_ref[...]).astype(BF)
        y_ref[:, 0:ATTN_W] = ya
        y_ref[:, ATTN_W:] = yb
        z = _dot(ya, w_ref[0:ATTN_W, :]) + _dot(yb, w_ref[ATTN_W:, :])
        z_ref[...] = z
        h2_ref[...] = h_ref[...] + _rms_fwd(z, gp_ref[...])

    row = lambda w: pl.BlockSpec((tm, w), lambda i: (i, 0))
    return _call(
        body, exch,
        name=name,
        grid=(lp // tm,),
        in_specs=[
            row(3 * CONV_W), row(ATTN_W), row(D_MODEL), _full((8, CONV_W)), _full((1, ATTN_W)), _full((1, CONV_W)),
            _full((D_MODEL, D_MODEL)), _full((1, D_MODEL)),
        ],
        out_specs=[row(CONV_W), row(D_MODEL), row(D_MODEL), row(D_MODEL)],
        out_shape=[
            jax.ShapeDtypeStruct((lp, CONV_W), F32),
            jax.ShapeDtypeStruct((lp, D_MODEL), BF),
            jax.ShapeDtypeStruct((lp, D_MODEL), F32),
            jax.ShapeDtypeStruct((lp, D_MODEL), F32),
        ],
        scratch_shapes=[pltpu.VMEM((tm + 8, CONV_W), F32)],
        compiler_params=_params(),
    )(bch, y_attn, h, conv_w, g_a, g_c, w_out, g_post)


def _mlp_fwd(h2, g_pre, w_up_t, w_down, g_post, tm, name, exch=None):
    lp = h2.shape[0]

    def body(h_ref, gp_ref, wu_ref, wd_ref, gq_ref, a_ref, up_ref, f_ref, h3_ref):
        h = h_ref[...]
        a = _rms_fwd(h, gp_ref[...]).astype(BF)
        a_ref[...] = a
        up = _dot_nt(a, wu_ref[...])
        up_ref[...] = up.astype(BF)
        act = jnp.square(jnp.maximum(up, 0.0)).astype(BF)
        f = _dot(act, wd_ref[...])
        f_ref[...] = f
        h3_ref[...] = h + _rms_fwd(f, gq_ref[...])

    row = lambda w: pl.BlockSpec((tm, w), lambda i: (i, 0))
    return _call(
        body, exch,
        name=name,
        grid=(lp // tm,),
        in_specs=[row(D_MODEL), _full((1, D_MODEL)), _full((D_FF, D_MODEL)), _full((D_FF, D_MODEL)), _full((1, D_MODEL))],
        out_specs=[row(D_MODEL), row(D_FF), row(D_MODEL), row(D_MODEL)],
        out_shape=[
            jax.ShapeDtypeStruct((lp, D_MODEL), BF),
            jax.ShapeDtypeStruct((lp, D_FF), BF),
            jax.ShapeDtypeStruct((lp, D_MODEL), F32),
            jax.ShapeDtypeStruct((lp, D_MODEL), F32),
        ],
        compiler_params=_params(),
    )(h2, g_pre, w_up_t, w_down, g_post)


def _loss_head(h, target, name):
    lp = h.shape[0]
    nb = lp // BLOCK

    def body(h_ref, t_ref, dh_ref, ls_ref):
        i = pl.program_id(0)

        @pl.when(i == 0)
        def _():
            dh_ref[...] = jnp.zeros((BLOCK, D_MODEL), F32)
            ls_ref[...] = jnp.zeros((8, 128), F32)

        @pl.when(i > 0)
        def _():
            d = h_ref[...] - t_ref[...]
            dh_ref[...] = d * (1.0 / D_MODEL)
            ls_ref[...] += jnp.sum(d * d)

    dh, ls = pl.pallas_call(
        body,
        name=name,
        grid=(nb,),
        in_specs=[
            pl.BlockSpec((BLOCK, D_MODEL), lambda i: (i, 0)),
            pl.BlockSpec((BLOCK, D_MODEL), lambda i: (jnp.maximum(i - 1, 0), 0)),
        ],
        out_specs=[pl.BlockSpec((BLOCK, D_MODEL), lambda i: (i, 0)), pl.BlockSpec((8, 128), lambda i: (0, 0))],
        out_shape=[jax.ShapeDtypeStruct((lp, D_MODEL), F32), jax.ShapeDtypeStruct((8, 128), F32)],
        compiler_params=_params(),
    )(h, target)
    return dh, ls[0, 0] * (0.5 / D_MODEL)


def _mlp_bwd_dx(dh3, f, up, h2, w_down, w_up_t, g_post, g_pre, tm, name, exch=None):
    lp = h2.shape[0]

    def body(dh3_ref, f_ref, up_ref, h2_ref, wd_ref, wu_ref, gq_ref, gp_ref, df_ref, dup_ref, dh2_ref, dg_ref):
        i = pl.program_id(0)

        @pl.when(i == 0)
        def _():
            dg_ref[...] = jnp.zeros((8, D_MODEL), F32)

        dh3 = dh3_ref[...]
        df, dgq = _rms_bwd(f_ref[...], gq_ref[...], dh3)
        dg_ref[ROW_MLP_POST:ROW_MLP_POST + 1, :] += dgq
        df = df.astype(BF)
        df_ref[...] = df
        dact = _dot_nt(df, wd_ref[...])
        dup = (dact * (2.0 * jnp.maximum(up_ref[...].astype(F32), 0.0))).astype(BF)
        dup_ref[...] = dup
        da = _dot(dup, wu_ref[...])
        dh, dgp = _rms_bwd(h2_ref[...], gp_ref[...], da)
        dg_ref[ROW_MLP_PRE:ROW_MLP_PRE + 1, :] += dgp
        dh2_ref[...] = dh3 + dh

    row = lambda w: pl.BlockSpec((tm, w), lambda i: (i, 0))
    return _call(
        body, exch,
        name=name,
        grid=(lp // tm,),
        in_specs=[
            row(D_MODEL), row(D_MODEL), row(D_FF), row(D_MODEL), _full((D_FF, D_MODEL)), _full((D_FF, D_MODEL)),
            _full((1, D_MODEL)), _full((1, D_MODEL)),
        ],
        out_specs=[row(D_MODEL), row(D_FF), row(D_MODEL), _full_out((8, D_MODEL))],
        out_shape=[
            jax.ShapeDtypeStruct((lp, D_MODEL), BF),
            jax.ShapeDtypeStruct((lp, D_FF), BF),
            jax.ShapeDtypeStruct((lp, D_MODEL), F32),
            jax.ShapeDtypeStruct((8, D_MODEL), F32),
        ],
        compiler_params=_params(),
    )(dh3, f, up, h2, w_down, w_up_t, g_post, g_pre)


def _mlp_bwd_dw(up, df, dup, a2, tm, name):
    lp = up.shape[0]
    nt = lp // tm
    nj = D_FF // D_MODEL

    def body(up_ref, df_ref, dup_ref, a_ref, dwd_ref, dwu_ref, accd, accu):
        i = pl.program_id(1)

        @pl.when(i == 0)
        def _():
            accd[...] = jnp.zeros_like(accd)
            accu[...] = jnp.zeros_like(accu)

        act = jnp.square(jnp.maximum(up_ref[...].astype(F32), 0.0)).astype(BF)
        accd[...] += _dot_tn(act, df_ref[...])
        accu[...] += _dot_tn(dup_ref[...], a_ref[...])

        @pl.when(i == nt - 1)
        def _():
            dwd_ref[...] = accd[...].astype(BF)
            dwu_ref[...] = accu[...].astype(BF)

    return pl.pallas_call(
        body,
        name=name,
        grid=(nj, nt),
        in_specs=[
            pl.BlockSpec((tm, D_MODEL), lambda j, i: (i, j)),
            pl.BlockSpec((tm, D_MODEL), lambda j, i: (i, 0)),
            pl.BlockSpec((tm, D_MODEL), lambda j, i: (i, j)),
            pl.BlockSpec((tm, D_MODEL), lambda j, i: (i, 0)),
        ],
        out_specs=[pl.BlockSpec((D_MODEL, D_MODEL), lambda j, i: (j, 0)), pl.BlockSpec((D_MODEL, D_MODEL), lambda j, i: (j, 0))],
        out_shape=[jax.ShapeDtypeStruct((D_FF, D_MODEL), BF), jax.ShapeDtypeStruct((D_FF, D_MODEL), BF)],
        scratch_shapes=[pltpu.VMEM((D_MODEL, D_MODEL), F32), pltpu.VMEM((D_MODEL, D_MODEL), F32)],
        compiler_params=_params(("arbitrary", "arbitrary")),
    )(up, df, dup, a2)


def _mix_out_bwd(dh2, z, y_attn, yc, bch, w_out, g_post, g_a, g_c, conv_w, tm, name, exch=None):
    lp = dh2.shape[0]
    nt = lp // tm

    def body(dh2_ref, z_ref, ya_ref, yc_ref, bch_ref, w_ref, gp_ref, ga_ref, gc_ref, cw_ref,
             dz_ref, dya_ref, dbch_ref, dg_ref, ext):
        i = pl.program_id(0)
        dcw_ref = dg_ref.at[ROW_CONV:ROW_CONV + 3, 0:CONV_W]

        @pl.when(i == 0)
        def _():
            ext[tm:tm + 8, :] = jnp.zeros((8, CONV_W), F32)
            dg_ref[...] = jnp.zeros((8, D_MODEL), F32)

        dz, dgp = _rms_bwd(z_ref[...], gp_ref[...], dh2_ref[...])
        dg_ref[ROW_MIX_POST:ROW_MIX_POST + 1, :] += dgp
        dz = dz.astype(BF)
        dz_ref[...] = dz
        dya_n = _dot_nt(dz, w_ref[0:ATTN_W, :])
        dyb_n = _dot_nt(dz, w_ref[ATTN_W:, :])
        dya, dga = _rms_bwd(ya_ref[...], ga_ref[...], dya_n)
        dg_ref[ROW_GROUP_G:ROW_GROUP_G + 1, 0:ATTN_W] += dga
        dya_ref[...] = dya
        b = bch_ref[:, 0:CONV_W]
        c = bch_ref[:, CONV_W:2 * CONV_W]
        hc = bch_ref[:, 2 * CONV_W:3 * CONV_W]
        yc_v = yc_ref[...]
        dyconv, dgc = _rms_bwd(b * yc_v, gc_ref[...], dyb_n)
        dg_ref[ROW_GROUP_G:ROW_GROUP_G + 1, ATTN_W:] += dgc
        dbch_ref[:, 0:CONV_W] = (dyconv * yc_v).astype(BF)
        dyc = dyconv * b
        ext[0:tm, :] = dyc
        d1 = ext[1:1 + tm, :]
        d2 = ext[2:2 + tm, :]
        du = cw_ref[2:3, :] * dyc + cw_ref[1:2, :] * d1 + cw_ref[0:1, :] * d2
        ext[tm:tm + 8, :] = dyc[0:8, :]
        dbch_ref[:, CONV_W:2 * CONV_W] = (du * hc).astype(BF)
        dbch_ref[:, 2 * CONV_W:3 * CONV_W] = (du * c).astype(BF)
        u = c * hc
        dcw_ref[0:1, :] += jnp.sum(u * d2, axis=0, keepdims=True)
        dcw_ref[1:2, :] += jnp.sum(u * d1, axis=0, keepdims=True)
        dcw_ref[2:3, :] += jnp.sum(u * dyc, axis=0, keepdims=True)

    row = lambda w: pl.BlockSpec((tm, w), lambda i: (nt - 1 - i, 0))
    return _call(
        body, exch,
        name=name,
        grid=(nt,),
        in_specs=[
            row(D_MODEL), row(D_MODEL), row(ATTN_W), row(CONV_W), row(3 * CONV_W), _full((D_MODEL, D_MODEL)),
            _full((1, D_MODEL)), _full((1, ATTN_W)), _full((1, CONV_W)), _full((8, CONV_W)),
        ],
        out_specs=[row(D_MODEL), row(ATTN_W), row(3 * CONV_W), _full_out((8, D_MODEL))],
        out_shape=[
            jax.ShapeDtypeStruct((lp, D_MODEL), BF),
            jax.ShapeDtypeStruct((lp, ATTN_W), F32),
            jax.ShapeDtypeStruct((lp, 3 * CONV_W), BF),
            jax.ShapeDtypeStruct((8, D_MODEL), F32),
        ],
        scratch_shapes=[pltpu.VMEM((tm + 8, CONV_W), F32)],
        compiler_params=_params(),
    )(dh2, z, y_attn, yc, bch, w_out, g_post, g_a, g_c, conv_w)


def _attn_bwd(qkv, o, do, sink, rope, name, exch=None):
    lp = qkv.shape[0]
    nb = lp // BLOCK

    def body(sink_ref, q_ref, kvc_ref, kvp_ref, o_ref, do_ref, rq_ref, rk_ref, dq_ref, dkv_ref, dsink_ref, carry):
        i = pl.program_id(0)

        @pl.when(i == 0)
        def _():
            carry[...] = jnp.zeros_like(carry)
            dsink_ref[...] = jnp.zeros((8, 128), F32)

        def finish(tot):
            dk = _rope_t(tot[:, :128], *_rope_coeffs(rk_ref[...]))
            dkv_ref[:, 0:128] = dk.astype(BF)
            dkv_ref[:, 128:256] = tot[:, 128:].astype(BF)

        @pl.when(i < nb)
        def _():
            tri, ok = _fold_masks(i)
            kvc, kvp = kvc_ref[...], kvp_ref[...]
            kk = jnp.concatenate([kvp[:, :128], kvc[:, :128]], axis=0)
            vv = jnp.concatenate([kvp[:, 128:], kvc[:, 128:]], axis=0)
            lane = lax.broadcasted_iota(jnp.int32, (BLOCK, 128), 1)
            lane2 = lax.broadcasted_iota(jnp.int32, (2 * BLOCK, 128), 1)
            first = lax.broadcasted_iota(jnp.int32, (2 * BLOCK, 1), 0) < BLOCK
            row_s = lax.broadcasted_iota(jnp.int32, (8, 128), 0)
            lane_s = jnp.where(row_s == ROW_SINK, lax.broadcasted_iota(jnp.int32, (8, 128), 1), -1)
            rope_q = _rope_coeffs(rq_ref[...])
            dsink = jnp.zeros((8, 128), F32)
            folded = []
            for kvh in range(2):
                c0 = 256 * kvh
                q2 = jnp.concatenate([q_ref[:, c0:c0 + 128], q_ref[:, c0 + 128:c0 + 256]], axis=0)
                do2 = jnp.concatenate([do_ref[:, c0:c0 + 128], do_ref[:, c0 + 128:c0 + 256]], axis=0)
                o2 = jnp.concatenate([o_ref[:, c0:c0 + 128], o_ref[:, c0 + 128:c0 + 256]], axis=0)
                k4, v4 = _kv_operand(kk, kvh), _kv_operand(vv, kvh)
                probs = _folded_probs(q2, k4, tri, ok, _sink_cols(sink_ref, kvh))
                prod = do2 * o2
                dob = do2.astype(BF)
                dp = _dot_nt(dob, v4)
                ds, pb = [], []
                for half in range(2):
                    p, ps = probs[half]
                    sel = (lane2 < HEAD_DIM) if half == 0 else (lane2 >= HEAD_DIM)
                    delta = jnp.sum(jnp.where(sel, prod, 0.0), axis=-1, keepdims=True)
                    dp_h = dp[:, 2 * half * BLOCK:2 * (half + 1) * BLOCK]
                    ds.append((p * (jnp.where(tri, dp_h[:, :BLOCK], dp_h[:, BLOCK:]) - delta)).astype(BF))
                    pb.append(p.astype(BF))
                    t = ps * delta
                    for jj in range(2):
                        part = -jnp.sum(jnp.where(first if jj == 0 else ~first, t, 0.0))
                        dsink = dsink + jnp.where(lane_s == 4 * kvh + 2 * jj + half, part, 0.0)
                ds4, p4 = _split4(ds, tri), _split4(pb, tri)
                dq2 = _dot(ds4, k4) * SCALE
                dq_ref[:, c0:c0 + 128] = _rope_t(dq2[:BLOCK], *rope_q).astype(BF)
                dq_ref[:, c0 + 128:c0 + 256] = _rope_t(dq2[BLOCK:], *rope_q).astype(BF)
                rk, rv = _dot_tn(ds4, q2), _dot_tn(p4, dob)
                own = (lane < HEAD_DIM) if kvh == 0 else (lane >= HEAD_DIM)
                group = []
                for r in (rk, rv):
                    for blk in range(2):
                        t = jnp.where(lane < HEAD_DIM, r[blk * BLOCK:(blk + 1) * BLOCK], r[(2 + blk) * BLOCK:(3 + blk) * BLOCK])
                        group.append(jnp.where(own, t + pltpu.roll(t, HEAD_DIM, 1), 0.0))
                folded.append(group)
            dsink_ref[...] += dsink
            dk_p, dk_c, dv_p, dv_c = [folded[0][t] + folded[1][t] for t in range(4)]
            finish(carry[...] + jnp.concatenate([dk_p, dv_p], axis=1))
            carry[...] = jnp.concatenate([dk_c, dv_c], axis=1)

        @pl.when(i == nb)
        def _():
            finish(carry[...])

    qi = lambda i: jnp.minimum(i, nb - 1)
    ki = lambda i: jnp.maximum(i - 1, 0)
    tab_q = pl.BlockSpec((BLOCK, 128), lambda i: (qi(i), 0))
    tab_k = pl.BlockSpec((BLOCK, 128), lambda i: (ki(i), 0))
    return _call(
        body, exch,
        name=name,
        grid=(nb + 1,),
        in_specs=[
            pl.BlockSpec(memory_space=pltpu.SMEM),
            pl.BlockSpec((BLOCK, ATTN_W), lambda i: (qi(i), 0)),
            pl.BlockSpec((BLOCK, 256), lambda i: (qi(i), 2)),
            pl.BlockSpec((BLOCK, 256), lambda i: (jnp.maximum(qi(i) - 1, 0), 2)),
            pl.BlockSpec((BLOCK, ATTN_W), lambda i: (qi(i), 0)),
            pl.BlockSpec((BLOCK, ATTN_W), lambda i: (qi(i), 0)),
            tab_q, tab_k,
        ],
        out_specs=[
            pl.BlockSpec((BLOCK, ATTN_W), lambda i: (qi(i), 0)),
            pl.BlockSpec((BLOCK, 256), lambda i: (ki(i), 0)),
            pl.BlockSpec((8, 128), lambda i: (0, 0)),
        ],
        out_shape=[
            jax.ShapeDtypeStruct((lp, ATTN_W), BF),
            jax.ShapeDtypeStruct((lp, 256), BF),
            jax.ShapeDtypeStruct((8, 128), F32),
        ],
        scratch_shapes=[pltpu.VMEM((BLOCK, 256), F32)],
        compiler_params=_params(),
    )(sink, qkv, qkv, qkv, o, do, rope, rope)


def _in_proj_bwd_dx(dq, dkv, dbch, w_in_t, h, dh2, g, tm, name, exch=None):
    lp = h.shape[0]

    def body(dq_ref, dkv_ref, dbch_ref, w_ref, h_ref, dh2_ref, g_ref, dh_ref, dg_ref):
        i = pl.program_id(0)

        @pl.when(i == 0)
        def _():
            dg_ref[...] = jnp.zeros((8, D_MODEL), F32)

        da = _dot(dq_ref[...], w_ref[0:512, :]) + _dot(dkv_ref[...], w_ref[512:768, :]) + _dot(dbch_ref[...], w_ref[768:, :])
        dh, dg = _rms_bwd(h_ref[...], g_ref[...], da)
        dg_ref[ROW_MIX_PRE:ROW_MIX_PRE + 1, :] += dg
        dh_ref[...] = dh2_ref[...] + dh

    row = lambda w: pl.BlockSpec((tm, w), lambda i: (i, 0))
    return _call(
        body, exch,
        name=name,
        grid=(lp // tm,),
        in_specs=[row(ATTN_W), row(256), row(3 * CONV_W), _full((IN_W, D_MODEL)), row(D_MODEL), row(D_MODEL), _full((1, D_MODEL))],
        out_specs=[row(D_MODEL), _full_out((8, D_MODEL))],
        out_shape=[jax.ShapeDtypeStruct((lp, D_MODEL), F32), jax.ShapeDtypeStruct((8, D_MODEL), F32)],
        compiler_params=_params(),
    )(dq, dkv, dbch, w_in_t, h, dh2, g)


def _mix_bwd_dw(dq, dkv, dbch, a, y, dz, tm, name):
    lp = a.shape[0]
    nt = lp // tm

    def body(dq_ref, dkv_ref, dbch_ref, a_ref, y_ref, dz_ref, dwi_ref, dwo_ref, acci, acco):
        i = pl.program_id(0)

        @pl.when(i == 0)
        def _():
            acci[...] = jnp.zeros_like(acci)
            acco[...] = jnp.zeros_like(acco)

        a_v = a_ref[...]
        acci[0:512, :] += _dot_tn(dq_ref[...], a_v)
        acci[512:768, :] += _dot_tn(dkv_ref[...], a_v)
        acci[768:, :] += _dot_tn(dbch_ref[...], a_v)
        acco[...] += _dot_tn(y_ref[...], dz_ref[...])

        @pl.when(i == nt - 1)
        def _():
            dwi_ref[...] = acci[...].astype(BF)
            dwo_ref[...] = acco[...].astype(BF)

    row = lambda w: pl.BlockSpec((tm, w), lambda i: (i, 0))
    return pl.pallas_call(
        body,
        name=name,
        grid=(nt,),
        in_specs=[row(ATTN_W), row(256), row(3 * CONV_W), row(D_MODEL), row(D_MODEL), row(D_MODEL)],
        out_specs=[_full_out((IN_W, D_MODEL)), _full_out((D_MODEL, D_MODEL))],
        out_shape=[jax.ShapeDtypeStruct((IN_W, D_MODEL), BF), jax.ShapeDtypeStruct((D_MODEL, D_MODEL), BF)],
        scratch_shapes=[pltpu.VMEM((IN_W, D_MODEL), F32), pltpu.VMEM((D_MODEL, D_MODEL), F32)],
        compiler_params=_params(),
    )(dq, dkv, dbch, a, y, dz)


def _mesh_place():
    x, y, c = lax.axis_index("x"), lax.axis_index("y"), lax.axis_index("c")
    return x, y, c, 4 * x + 2 * y + c


def _peer(x, y, c, k):
    px = 1 - x if k & 4 else x
    py = 1 - y if k & 2 else y
    pc = 1 - c if k & 1 else c
    return (px, py, pc), 4 * px + 2 * py + pc


class _Exchange:
    def __init__(self, pieces):
        self.srcs = [s for s, _ in pieces]
        self.to_all = [g for _, g in pieces]
        self.n = len(pieces)
        self.land_shapes = [
            jax.ShapeDtypeStruct((N_DEV,) + (s.shape if g else s.shape[1:]), s.dtype) for s, g in pieces]
        self.sem_shapes = [pltpu.SemaphoreType.DMA((self.n, N_DEV - 1)), pltpu.SemaphoreType.DMA((self.n, N_DEV - 1)),
                           pltpu.SemaphoreType.DMA((self.n,))]

    def _copies(self, srcs, lands, sems, receiving):
        send_sems, recv_sems, local_sems = sems
        x, y, c, me = _mesh_place()
        own, remote = [], []
        for p in range(self.n):
            block = lambda d: srcs[p] if self.to_all[p] else srcs[p].at[d]
            own.append(pltpu.make_async_copy(block(me), lands[p].at[me], local_sems.at[p]))
            for k in range(1, N_DEV):
                peer, pidx = _peer(x, y, c, k)
                remote.append(pltpu.make_async_remote_copy(
                    src_ref=block(pidx), dst_ref=lands[p].at[pidx if receiving else me], send_sem=send_sems.at[p, k - 1],
                    recv_sem=recv_sems.at[p, k - 1], device_id=peer, device_id_type=MESH))
        return own, remote

    def start(self, srcs, lands, sems):
        own, remote = self._copies(srcs, lands, sems, False)
        for cp in own + remote:
            cp.start()

    def wait(self, srcs, lands, sems):
        own, remote = self._copies(srcs, lands, sems, True)
        for cp in remote:
            cp.wait_recv()
        for cp in remote:
            cp.wait_send()
        for cp in own:
            cp.wait()


def _exchange_call(exch, name):
    def body(*refs):
        srcs, lands, sems = refs[:exch.n], refs[exch.n:2 * exch.n], refs[2 * exch.n:]
        exch.start(srcs, lands, sems)
        exch.wait(srcs, lands, sems)

    hbm = pl.BlockSpec(memory_space=pl.ANY)
    return pl.pallas_call(
        body,
        name=name,
        in_specs=[hbm] * exch.n,
        out_specs=[hbm] * exch.n,
        out_shape=exch.land_shapes,
        scratch_shapes=exch.sem_shapes,
    )(*exch.srcs)


def _call(body, exch, *, name, grid, in_specs, out_specs, out_shape, scratch_shapes=(), compiler_params):
    if exch is None:
        return pl.pallas_call(body, name=name, grid=grid, in_specs=in_specs, out_specs=out_specs, out_shape=out_shape,
                              scratch_shapes=scratch_shapes, compiler_params=compiler_params)
    n_in, n_out, n_scr, n_x = len(in_specs), len(out_shape), len(scratch_shapes), exch.n

    def carrying(*refs):
        a, b, c, d, e = n_in, n_in + n_x, n_in + n_x + n_out, n_in + 2 * n_x + n_out, n_in + 2 * n_x + n_out + n_scr
        ins, srcs, outs, lands, scr, sems = refs[:a], refs[a:b], refs[b:c], refs[c:d], refs[d:e], refs[e:]
        ids = [pl.program_id(t) for t in range(len(grid))]
        first = functools.reduce(jnp.logical_and, [i == 0 for i in ids])
        last = functools.reduce(jnp.logical_and, [i == g - 1 for i, g in zip(ids, grid)])

        @pl.when(first)
        def _():
            exch.start(srcs, lands, sems)

        body(*ins, *outs, *scr)

        @pl.when(last)
        def _():
            exch.wait(srcs, lands, sems)

    hbm = pl.BlockSpec(memory_space=pl.ANY)
    call = pl.pallas_call(
        carrying, name=name, grid=grid, in_specs=list(in_specs) + [hbm] * n_x, out_specs=list(out_specs) + [hbm] * n_x,
        out_shape=list(out_shape) + exch.land_shapes, scratch_shapes=list(scratch_shapes) + exch.sem_shapes,
        compiler_params=compiler_params)

    def run(*args):
        res = call(*args, *exch.srcs)
        return list(res[:n_out]), list(res[n_out:])

    return run


def _sum_small(part):
    def body(part_ref, out_ref, land, send_sems, recv_sems):
        x, y, c, me = _mesh_place()
        land[me] = part_ref[...]
        sent = []
        for k in range(1, N_DEV):
            peer, _ = _peer(x, y, c, k)
            cp = pltpu.make_async_remote_copy(
                src_ref=part_ref, dst_ref=land.at[me], send_sem=send_sems.at[k - 1], recv_sem=recv_sems.at[k - 1],
                device_id=peer, device_id_type=MESH)
            cp.start()
            sent.append(cp)
        for k in range(1, N_DEV):
            peer, pidx = _peer(x, y, c, k)
            pltpu.make_async_remote_copy(
                src_ref=part_ref, dst_ref=land.at[pidx], send_sem=send_sems.at[k - 1], recv_sem=recv_sems.at[k - 1],
                device_id=peer, device_id_type=MESH).wait_recv()
        for cp in sent:
            cp.wait_send()
        acc = land[0]
        for d in range(1, N_DEV):
            acc = acc + land[d]
        out_ref[...] = acc

    vmem = pl.BlockSpec(memory_space=pltpu.VMEM)
    return pl.pallas_call(
        body,
        name="sum_small",
        in_specs=[vmem],
        out_specs=vmem,
        out_shape=jax.ShapeDtypeStruct(part.shape, F32),
        scratch_shapes=[pltpu.VMEM((N_DEV,) + part.shape, F32), pltpu.SemaphoreType.DMA((N_DEV - 1,)),
                        pltpu.SemaphoreType.DMA((N_DEV - 1,))],
    )(part)


def _adamw(w, g, m, v):
    m = ADAM_B1 * m + (1.0 - ADAM_B1) * g
    v = ADAM_B2 * v + (1.0 - ADAM_B2) * jnp.square(g)
    m_hat = m / (1.0 - ADAM_B1 ** ADAM_STEP)
    v_hat = v / (1.0 - ADAM_B2 ** ADAM_STEP)
    delta = -ADAM_LR * (m_hat / (jnp.sqrt(v_hat) + ADAM_EPS) + ADAM_WD * w)
    return delta, m, v


def _landed_specs(tr, wd):
    return [pl.BlockSpec((N_DEV, tr, wd), lambda l, i, ll=ll: (0, jnp.where(l == ll, i, 0), 0)) for ll in range(DEPTH)]


def _device_sum(r_ref):
    acc = r_ref[0].astype(F32)
    for d in range(1, N_DEV):
        acc = acc + r_ref[d].astype(F32)
    return acc


def _sum_parts(recv, tr, name):
    _, r, wd = recv[0].shape

    def body(*refs):
        g_ref = refs[DEPTH]
        for ll in range(DEPTH):
            @pl.when(pl.program_id(0) == ll)
            def _(ll=ll):
                g_ref[0] = _device_sum(refs[ll])

    return pl.pallas_call(
        body,
        name=name,
        grid=(DEPTH, r // tr),
        in_specs=_landed_specs(tr, wd),
        out_specs=pl.BlockSpec((1, tr, wd), lambda l, i: (l, i, 0)),
        out_shape=jax.ShapeDtypeStruct((DEPTH, r, wd), F32),
        compiler_params=_params(("arbitrary", "arbitrary")),
    )(*recv)


def _sum_adamw(recv, w, m, v, tr, name):
    _, r, wd = recv[0].shape

    def body(*refs):
        w_ref, m_ref, v_ref, g_ref, d_ref, mo_ref, vo_ref = refs[DEPTH:]
        for ll in range(DEPTH):
            @pl.when(pl.program_id(0) == ll)
            def _(ll=ll):
                g = _device_sum(refs[ll])
                g_ref[0] = g
                d_ref[0], mo_ref[0], vo_ref[0] = _adamw(w_ref[0], g, m_ref[0], v_ref[0])

    blk = pl.BlockSpec((1, tr, wd), lambda l, i: (l, i, 0))
    shape = jax.ShapeDtypeStruct((DEPTH, r, wd), F32)
    return pl.pallas_call(
        body,
        name=name,
        grid=(DEPTH, r // tr),
        in_specs=_landed_specs(tr, wd) + [blk, blk, blk],
        out_specs=[blk] * 4,
        out_shape=[shape] * 4,
        compiler_params=_params(("arbitrary", "arbitrary")),
    )(*recv, w, m, v)


def _adamw_rows(w, g, m, v, tr, name):
    _, r, wd = w.shape

    def body(w_ref, g_ref, m_ref, v_ref, d_ref, mo_ref, vo_ref):
        d_ref[0], mo_ref[0], vo_ref[0] = _adamw(w_ref[0], g_ref[0], m_ref[0], v_ref[0])

    blk = pl.BlockSpec((1, tr, wd), lambda l, i: (l, i, 0))
    shape = jax.ShapeDtypeStruct(w.shape, F32)
    return pl.pallas_call(
        body,
        name=name,
        grid=(DEPTH, r // tr),
        in_specs=[blk] * 4,
        out_specs=[blk] * 3,
        out_shape=[shape] * 3,
        compiler_params=_params(("arbitrary", "arbitrary")),
    )(w, g, m, v)


def _adamw_small(ws, gs, ms, vs):
    n = len(ws)

    def body(*refs):
        w_r, g_r, m_r, v_r = refs[:n], refs[n:2 * n], refs[2 * n:3 * n], refs[3 * n:4 * n]
        d_o, m_o, v_o = refs[4 * n:5 * n], refs[5 * n:6 * n], refs[6 * n:7 * n]
        for t in range(n):
            d_o[t][...], m_o[t][...], v_o[t][...] = _adamw(w_r[t][...], g_r[t][...], m_r[t][...], v_r[t][...])

    vmem = pl.BlockSpec(memory_space=pltpu.VMEM)
    shapes = [jax.ShapeDtypeStruct(w.shape, F32) for w in ws]
    outs = pl.pallas_call(
        body,
        name="adamw_small",
        in_specs=[vmem] * (4 * n),
        out_specs=[vmem] * (3 * n),
        out_shape=shapes * 3,
    )(*ws, *gs, *ms, *vs)
    return outs[:n], outs[n:2 * n], outs[2 * n:]


def kernel(x, meta_tokens, mix_pre_g, w_in, conv_w, sinks, attn_out_g, conv_out_g, w_out, mix_post_g, mlp_pre_g, w_up, w_down, mlp_post_g, loss_target, m_meta_tokens, m_mix_pre_g, m_w_in, m_conv_w, m_sinks, m_attn_out_g, m_conv_out_g, m_w_out, m_mix_post_g, m_mlp_pre_g, m_w_up, m_w_down, m_mlp_post_g, v_meta_tokens, v_mix_pre_g, v_w_in, v_conv_w, v_sinks, v_attn_out_g, v_conv_out_g, v_w_out, v_mix_post_g, v_mlp_pre_g, v_w_up, v_w_down, v_mlp_post_g):
    seq = x.shape[1]
    lp = BLOCK + seq
    tm = _row_tile(lp)
    tm_mlp = _row_tile(lp, (320, 256, 128))
    me = 4 * lax.axis_index("x") + 2 * lax.axis_index("y") + lax.axis_index("c")
    cshard = CONV_W // N_DEV
    mshard = D_MODEL // N_DEV

    gather_with = {
        ("in_proj_fwd", 0): [("out", 0)], ("attn_fwd", 0): [("up", 0)], ("mix_out_fwd", 0): [("down", 0)],
        ("mlp_fwd", 0): [("in", 1), ("out", 1), ("up", 1)], ("attn_fwd", 1): [("down", 1)],
    }
    scatter_with = {
        ("mix_out_bwd", 1): [("down", 1)], ("attn_bwd", 1): [("up", 1)], ("in_proj_bwd_dx", 1): [("out", 1)],
        ("mlp_bwd_dx", 0): [("in", 1)], ("mix_out_bwd", 0): [("down", 0)], ("attn_bwd", 0): [("up", 0)],
        ("in_proj_bwd_dx", 0): [("in", 0), ("out", 0)],
    }
    shard = {"in": jnp.swapaxes(w_in, 1, 2).astype(BF), "out": w_out.astype(BF),
             "up": jnp.swapaxes(w_up, 1, 2).astype(BF), "down": w_down.astype(BF)}
    weight = {}
    grad = {}
    landed = {}

    def run(fn, kind, l, *args):
        key, name = (kind, l), f"{kind}_{l}"
        if key in gather_with:
            blocks = gather_with[key]
            outs, lands = fn(*args, name, _Exchange([(shard[n][k], True) for n, k in blocks]))
            for b, land in zip(blocks, lands):
                weight[b] = land.reshape(-1, D_MODEL)
            return outs
        if key in scatter_with:
            blocks = scatter_with[key]
            outs, lands = fn(*args, name, _Exchange([(grad[b].reshape(N_DEV, -1, D_MODEL), False) for b in blocks]))
            landed.update(zip(blocks, lands))
            return outs
        return fn(*args, name)

    small = jnp.zeros((24, 128), F32)
    small = small.at[0:N_META, :].set(meta_tokens)
    small = small.at[N_META:N_META + 6, 0:cshard].set(conv_w.reshape(6, cshard))
    first_in, g_small = _exchange_call(_Exchange([(shard["in"][0], True), (small, True)]), "gather_first")
    weight[("in", 0)] = first_in.reshape(-1, D_MODEL)
    meta_full = jnp.swapaxes(g_small[:, 0:N_META, :], 0, 1).reshape(N_META, D_MODEL)
    cw = g_small[:, N_META:N_META + 6, 0:cshard].reshape(N_DEV, DEPTH, 3, cshard)
    cw = jnp.transpose(cw, (1, 2, 0, 3)).reshape(DEPTH, 3, CONV_W)
    conv_full = jnp.concatenate([cw, jnp.zeros((DEPTH, 5, CONV_W), F32)], axis=1)

    rope = _rope_table(lp)
    row1 = lambda a, l: a[l].reshape(1, -1)

    h = jnp.concatenate([jnp.zeros((LEAD_PAD, D_MODEL), F32), meta_full, x[0]], axis=0)
    saved = []
    for l in range(DEPTH):
        a, qkv, bch = run(_in_proj_fwd, "in_proj_fwd", l, h, row1(mix_pre_g, l), weight[("in", l)], rope, tm)
        y_attn, = run(_attn_fwd, "attn_fwd", l, qkv, row1(sinks, l))
        yc, y, z, h2 = run(_mix_out_fwd, "mix_out_fwd", l, bch, y_attn, h, conv_full[l], row1(attn_out_g, l),
                           row1(conv_out_g, l), weight[("out", l)], row1(mix_post_g, l), tm)
        a2, up, f, h3 = run(_mlp_fwd, "mlp_fwd", l, h2, row1(mlp_pre_g, l), weight[("up", l)], weight[("down", l)],
                            row1(mlp_post_g, l), tm_mlp)
        saved.append((h, a, qkv, bch, y_attn, yc, y, z, h2, a2, up, f))
        h = h3
    dh, loss_part = _loss_head(h, loss_target[0], "loss_head")

    gsmall = [None] * DEPTH
    for l in reversed(range(DEPTH)):
        h0, a, qkv, bch, y_attn, yc, y, z, h2, a2, up, f = saved[l]
        df, dup, dh2, dg_mlp = run(_mlp_bwd_dx, "mlp_bwd_dx", l, dh, f, up, h2, weight[("down", l)], weight[("up", l)],
                                   row1(mlp_post_g, l), row1(mlp_pre_g, l), tm_mlp)
        grad[("down", l)], grad[("up", l)] = _mlp_bwd_dw(up, df, dup, a2, tm, f"mlp_bwd_dw_{l}")
        dz, dya, dbch, dg_mix = run(_mix_out_bwd, "mix_out_bwd", l, dh2, z, y_attn, yc, bch, weight[("out", l)],
                                    row1(mix_post_g, l), row1(attn_out_g, l), row1(conv_out_g, l), conv_full[l], tm)
        dq, dkv, dsink = run(_attn_bwd, "attn_bwd", l, qkv, y_attn, dya, row1(sinks, l), rope)
        grad[("in", l)], grad[("out", l)] = _mix_bwd_dw(dq, dkv, dbch, a, y, dz, tm_mlp, f"mix_bwd_dw_{l}")
        dh, dg_in = run(_in_proj_bwd_dx, "in_proj_bwd_dx", l, dq, dkv, dbch, weight[("in", l)], h0, dh2,
                        row1(mix_pre_g, l), tm)
        tile_a = dg_mlp + dg_in + jnp.pad(dsink, ((0, 0), (0, D_MODEL - 128)))
        gsmall[l] = (tile_a, dg_mix)
    grad_x = dh[BLOCK:][None]

    loss_tile = jnp.zeros((8, D_MODEL), F32).at[ROW_LOSS, 0].set(loss_part)
    tot = _sum_small(jnp.concatenate(
        [gsmall[0][0] + loss_tile, gsmall[0][1], gsmall[1][0], gsmall[1][1], dh[LEAD_PAD:BLOCK]], axis=0))
    loss = tot[ROW_LOSS, 0]
    ta = [tot[16 * l:16 * l + 8] for l in range(DEPTH)]
    tb = [tot[16 * l + 8:16 * l + 16] for l in range(DEPTH)]
    pick = lambda tiles, r0, r1, c0, c1: jnp.stack([t[r0:r1, c0:c1] for t in tiles])
    g_mlp_post = pick(ta, ROW_MLP_POST, ROW_MLP_POST + 1, 0, D_MODEL).reshape(DEPTH, D_MODEL)
    g_mlp_pre = pick(ta, ROW_MLP_PRE, ROW_MLP_PRE + 1, 0, D_MODEL).reshape(DEPTH, D_MODEL)
    g_mix_pre = pick(ta, ROW_MIX_PRE, ROW_MIX_PRE + 1, 0, D_MODEL).reshape(DEPTH, D_MODEL)
    g_sinks = pick(ta, ROW_SINK, ROW_SINK + 1, 0, N_Q_HEADS).reshape(DEPTH, N_Q_HEADS)
    g_mix_post = pick(tb, ROW_MIX_POST, ROW_MIX_POST + 1, 0, D_MODEL).reshape(DEPTH, D_MODEL)
    g_attn_out = pick(tb, ROW_GROUP_G, ROW_GROUP_G + 1, 0, ATTN_W).reshape(DEPTH, ATTN_W)
    g_conv_out = pick(tb, ROW_GROUP_G, ROW_GROUP_G + 1, ATTN_W, D_MODEL).reshape(DEPTH, CONV_W)
    g_conv_full = pick(tb, ROW_CONV, ROW_CONV + 3, 0, CONV_W)
    g_conv = lax.dynamic_slice_in_dim(g_conv_full, me * cshard, cshard, axis=2)
    g_meta = lax.dynamic_slice_in_dim(tot[16 * DEPTH:16 * DEPTH + N_META], me * mshard, mshard, axis=1)

    r_in, r_out, r_up, r_down = [[landed[(n, l)] for l in range(DEPTH)] for n in ("in", "out", "up", "down")]
    g_w_in = jnp.swapaxes(_sum_parts(r_in, 96, "sum_w_in"), 1, 2)
    g_w_up = jnp.swapaxes(_sum_parts(r_up, 128, "sum_w_up"), 1, 2)
    d_w_in, nm_w_in, nv_w_in = _adamw_rows(w_in, g_w_in, m_w_in, v_w_in, 256, "adamw_w_in")
    d_w_up, nm_w_up, nv_w_up = _adamw_rows(w_up, g_w_up, m_w_up, v_w_up, 256, "adamw_w_up")
    g_w_out, d_w_out, nm_w_out, nv_w_out = _sum_adamw(r_out, w_out, m_w_out, v_w_out, 128, "adamw_w_out")
    g_w_down, d_w_down, nm_w_down, nv_w_down = _sum_adamw(r_down, w_down, m_w_down, v_w_down, 128, "adamw_w_down")

    ws = [meta_tokens, mix_pre_g, conv_w.reshape(6, cshard), sinks, attn_out_g, conv_out_g, mix_post_g, mlp_pre_g, mlp_post_g]
    gs = [g_meta, g_mix_pre, g_conv.reshape(6, cshard), g_sinks, g_attn_out, g_conv_out, g_mix_post, g_mlp_pre, g_mlp_post]
    ms = [m_meta_tokens, m_mix_pre_g, m_conv_w.reshape(6, cshard), m_sinks, m_attn_out_g, m_conv_out_g, m_mix_post_g,
          m_mlp_pre_g, m_mlp_post_g]
    vs = [v_meta_tokens, v_mix_pre_g, v_conv_w.reshape(6, cshard), v_sinks, v_attn_out_g, v_conv_out_g, v_mix_post_g,
          v_mlp_pre_g, v_mlp_post_g]
    ds, nms, nvs = _adamw_small(ws, gs, ms, vs)

    def order(meta, mix_pre, cv, sk, a_out, c_out, mix_post, mlp_pre, mlp_post, win, wout, wup, wdown):
        return [meta, mix_pre, win, cv.reshape(DEPTH, 3, cshard), sk, a_out, c_out, wout, mix_post, mlp_pre, wup, wdown, mlp_post]

    grads = order(*gs, g_w_in, g_w_out, g_w_up, g_w_down)
    deltas = order(*ds, d_w_in, d_w_out, d_w_up, d_w_down)
    new_m = order(*nms, nm_w_in, nm_w_out, nm_w_up, nm_w_down)
    new_v = order(*nvs, nv_w_in, nv_w_out, nv_w_up, nv_w_down)
    return (loss, grad_x, *grads, *deltas, *new_m, *new_v)
```

```python
import functools
import math

import jax
import jax.numpy as jnp
from jax import lax
from jax.experimental import pallas as pl
from jax.experimental.pallas import tpu as pltpu

F32 = jnp.float32
BF = jnp.bfloat16

D_MODEL = 1024
ATTN_W = 512
CONV_W = 512
KV_W = 128
HEAD_DIM = 64
N_Q_HEADS = 8
ROT_DIM = 16
D_FF = 4096
IN_W = 2304
N_META = 16
BLOCK = 128
LEAD_PAD = BLOCK - N_META
ROPE_THETA = 500000.0
EPS = 1e-6
N_DEV = 8
DEPTH = 2
NEG = -1e30
SCALE = HEAD_DIM ** -0.5

ADAM_LR = 0.001
ADAM_B1 = 0.9
ADAM_B2 = 0.999
ADAM_EPS = 1e-08
ADAM_WD = 0.01
ADAM_STEP = 10

ROW_MLP_POST, ROW_MLP_PRE, ROW_MIX_PRE, ROW_SINK, ROW_LOSS = 0, 1, 2, 3, 4
ROW_MIX_POST, ROW_GROUP_G, ROW_CONV = 0, 1, 2

VMEM_LIMIT = 56 * 1024 * 1024
MESH = pl.DeviceIdType.MESH


def _dot(a, b):
    return jnp.dot(a, b, preferred_element_type=F32)


def _dot_nt(a, b):
    return lax.dot_general(a, b, (((1,), (1,)), ((), ())), preferred_element_type=F32)


def _dot_tn(a, b):
    return lax.dot_general(a, b, (((0,), (0,)), ((), ())), preferred_element_type=F32)


def _rms_fwd(x, g):
    r = lax.rsqrt(jnp.mean(x * x, axis=-1, keepdims=True) + EPS)
    return x * r * g


def _rms_bwd(x, g, dy):
    r = lax.rsqrt(jnp.mean(x * x, axis=-1, keepdims=True) + EPS)
    xh = x * r
    t = dy * g
    dx = r * (t - xh * jnp.mean(t * xh, axis=-1, keepdims=True))
    dg = jnp.sum(dy * xh, axis=0, keepdims=True)
    return dx, dg


def _row_tile(lp, cands=(640, 512, 384, 256, 128)):
    for t in cands:
        if lp % t == 0:
            return t
    raise ValueError(f"row count {lp} is not a multiple of 128")


def _full(shape):
    n = len(shape)
    return pl.BlockSpec(shape, lambda *_: (0,) * n, pipeline_mode=pl.Buffered(1))


def _full_out(shape):
    n = len(shape)
    return pl.BlockSpec(shape, lambda *_: (0,) * n)


def _params(sem=("arbitrary",)):
    return pltpu.CompilerParams(dimension_semantics=sem, vmem_limit_bytes=VMEM_LIMIT)


def _rope_table(lp):
    half = ROT_DIM // 2
    pos = jnp.maximum(jnp.arange(lp) - LEAD_PAD, 0).astype(F32)
    inv_freq = jnp.power(jnp.float32(ROPE_THETA), -jnp.arange(0, ROT_DIM, 2, dtype=F32) / ROT_DIM)
    ang_t = jnp.concatenate([inv_freq, inv_freq])[:, None] * pos[None, :]
    row = lax.broadcasted_iota(jnp.int32, (ROT_DIM, lp), 0)
    cs_t = jnp.where(row < half, jnp.cos(ang_t), jnp.sin(ang_t))
    return jnp.pad(cs_t.T, ((0, 0), (0, 128 - ROT_DIM)))


def _rope_coeffs(t):
    half = ROT_DIM // 2
    lane = lax.broadcasted_iota(jnp.int32, t.shape, 1)
    cos_a = jnp.where(lane < half, t, 0.0)
    sin_a = pltpu.roll(jnp.where((lane >= half) & (lane < ROT_DIM), t, 0.0), 128 - half, 1)
    c = cos_a + pltpu.roll(cos_a, half, 1) + jnp.where((lane >= ROT_DIM) & (lane < HEAD_DIM), 1.0, 0.0)
    s2 = pltpu.roll(sin_a, half, 1)
    both = lambda u: u + pltpu.roll(u, HEAD_DIM, 1)
    return both(c), both(-sin_a), both(s2)


def _rope(t, c, s1, s2):
    return t * c + pltpu.roll(t, BLOCK - 8, 1) * s1 + pltpu.roll(t, 8, 1) * s2


def _rope_t(dt, c, s1, s2):
    return dt * c + pltpu.roll(dt * s1, 8, 1) + pltpu.roll(dt * s2, BLOCK - 8, 1)


def _in_proj_fwd(h, g, w_in_t, rope, tm, name, exch=None):
    lp = h.shape[0]

    def body(h_ref, g_ref, w_ref, rope_ref, a_ref, qkv_ref, bch_ref):
        a = _rms_fwd(h_ref[...], g_ref[...]).astype(BF)
        a_ref[...] = a
        proj = _dot_nt(a, w_ref[...])
        c, s1, s2 = _rope_coeffs(rope_ref[...])
        for j in range(5):
            t = _rope(proj[:, j * 128:(j + 1) * 128], c, s1, s2)
            qkv_ref[:, j * 128:(j + 1) * 128] = (t * SCALE if j < 4 else t).astype(BF)
        qkv_ref[:, 640:768] = proj[:, 640:768].astype(BF)
        bch_ref[...] = proj[:, 768:]

    row = lambda w: pl.BlockSpec((tm, w), lambda i: (i, 0))
    return _call(
        body, exch,
        name=name,
        grid=(lp // tm,),
        in_specs=[row(D_MODEL), _full((1, D_MODEL)), _full((IN_W, D_MODEL)), row(128)],
        out_specs=[row(D_MODEL), row(768), row(3 * CONV_W)],
        out_shape=[
            jax.ShapeDtypeStruct((lp, D_MODEL), BF),
            jax.ShapeDtypeStruct((lp, 768), BF),
            jax.ShapeDtypeStruct((lp, 3 * CONV_W), F32),
        ],
        compiler_params=_params(),
    )(h, g, w_in_t, rope)


def _fold_masks(i):
    r = lax.broadcasted_iota(jnp.int32, (2 * BLOCK, BLOCK), 0) & (BLOCK - 1)
    c = lax.broadcasted_iota(jnp.int32, (2 * BLOCK, BLOCK), 1)
    tri = c > r
    ok = jnp.where(tri, (i - 1) * BLOCK + c, i * BLOCK + c) >= LEAD_PAD
    return tri, ok


def _kv_operand(x, kvh):
    lane = lax.broadcasted_iota(jnp.int32, x.shape, 1)
    zero = jnp.zeros_like(x)
    if kvh == 0:
        lo = jnp.where(lane < HEAD_DIM, x, zero)
        hi = pltpu.roll(lo, HEAD_DIM, 1)
    else:
        hi = jnp.where(lane >= HEAD_DIM, x, zero)
        lo = pltpu.roll(hi, HEAD_DIM, 1)
    return jnp.concatenate([lo, hi], axis=0)


def _split4(t, tri):
    zero = jnp.zeros_like(t[0])
    return jnp.concatenate(
        [jnp.where(tri, t[0], zero), jnp.where(tri, zero, t[0]), jnp.where(tri, t[1], zero), jnp.where(tri, zero, t[1])], axis=1)


def _sink_cols(sink_ref, kvh):
    first = lax.broadcasted_iota(jnp.int32, (2 * BLOCK, 1), 0) < BLOCK
    return [jnp.where(first, sink_ref[0, 4 * kvh + half], sink_ref[0, 4 * kvh + 2 + half]) for half in range(2)]


def _folded_probs(q2, k4, tri, ok, sks):
    s = _dot_nt(q2, k4)
    es, ss = [], []
    for half in range(2):
        s_h = s[:, 2 * half * BLOCK:2 * (half + 1) * BLOCK]
        sf = jnp.where(ok, jnp.where(tri, s_h[:, :BLOCK], s_h[:, BLOCK:]), NEG)
        m = jnp.maximum(jnp.max(sf, axis=-1, keepdims=True), sks[half])
        es.append(jnp.exp(sf - m))
        ss.append(jnp.exp(sks[half] - m))
    sums = _dot(jnp.concatenate(es, axis=0).astype(BF), jnp.ones((BLOCK, BLOCK), BF))
    out = []
    for half in range(2):
        inv = 1.0 / (sums[2 * half * BLOCK:2 * (half + 1) * BLOCK] + ss[half])
        out.append((es[half] * inv, ss[half] * inv[:, 0:1]))
    return out


def _attn_fwd(qkv, sink, name, exch=None):
    lp = qkv.shape[0]
    nb = lp // BLOCK

    def body(sink_ref, q_ref, kvc_ref, kvp_ref, o_ref):
        i = pl.program_id(0)
        tri, ok = _fold_masks(i)
        kvc, kvp = kvc_ref[...], kvp_ref[...]
        kk = jnp.concatenate([kvp[:, :128], kvc[:, :128]], axis=0)
        vv = jnp.concatenate([kvp[:, 128:], kvc[:, 128:]], axis=0)
        for kvh in range(2):
            q2 = jnp.concatenate([q_ref[:, 256 * kvh:256 * kvh + 128], q_ref[:, 256 * kvh + 128:256 * kvh + 256]], axis=0)
            (p_e, _), (p_o, _) = _folded_probs(q2, _kv_operand(kk, kvh), tri, ok, _sink_cols(sink_ref, kvh))
            out = _dot(_split4([p_e.astype(BF), p_o.astype(BF)], tri), _kv_operand(vv, kvh))
            o_ref[:, 256 * kvh:256 * kvh + 128] = out[:BLOCK]
            o_ref[:, 256 * kvh + 128:256 * kvh + 256] = out[BLOCK:]

    return _call(
        body, exch,
        name=name,
        grid=(nb,),
        in_specs=[
            pl.BlockSpec(memory_space=pltpu.SMEM),
            pl.BlockSpec((BLOCK, ATTN_W), lambda i: (i, 0)),
            pl.BlockSpec((BLOCK, 256), lambda i: (i, 2)),
            pl.BlockSpec((BLOCK, 256), lambda i: (jnp.maximum(i - 1, 0), 2)),
        ],
        out_specs=[pl.BlockSpec((BLOCK, ATTN_W), lambda i: (i, 0))],
        out_shape=[jax.ShapeDtypeStruct((lp, ATTN_W), F32)],
        compiler_params=_params(),
    )(sink, qkv, qkv, qkv)


def _mix_out_fwd(bch, y_attn, h, conv_w, g_a, g_c, w_out, g_post, tm, name, exch=None):
    lp = h.shape[0]

    def body(bch_ref, ya_ref, h_ref, cw_ref, ga_ref, gc_ref, w_ref, gp_ref, yc_ref, y_ref, z_ref, h2_ref, ext):
        i = pl.program_id(0)

        @pl.when(i == 0)
        def _():
            ext[0:8, :] = jnp.zeros((8, CONV_W), F32)

        b = bch_ref[:, 0:CONV_W]
        u = bch_ref[:, CONV_W:2 * CONV_W] * bch_ref[:, 2 * CONV_W:3 * CONV_W]
        ext[8:8 + tm, :] = u
        u1 = ext[7:7 + tm, :]
        u2 = ext[6:6 + tm, :]
        yc = cw_ref[0:1, :] * u2 + cw_ref[1:2, :] * u1 + cw_ref[2:3, :] * u
        ext[0:8, :] = u[tm - 8:tm, :]
        yc_ref[...] = yc
        ya = _rms_fwd(ya_ref[...], ga_ref[...]).astype(BF)
        yb = _rms_fwd(b * yc, gc_ref[...]).astype(BF)
        y_ref[:, 0:ATTN_W] = ya
        y_ref[:, ATTN_W:] = yb
        z = _dot(ya, w_ref[0:ATTN_W, :]) + _dot(yb, w_ref[ATTN_W:, :])
        z_ref[...] = z
        h2_ref[...] = h_ref[...] + _rms_fwd(z, gp_ref[...])

    row = lambda w: pl.BlockSpec((tm, w), lambda i: (i, 0))
    return _call(
        body, exch,
        name=name,
        grid=(lp // tm,),
        in_specs=[
            row(3 * CONV_W), row(ATTN_W), row(D_MODEL), _full((8, CONV_W)), _full((1, ATTN_W)), _full((1, CONV_W)),
            _full((D_MODEL, D_MODEL)), _full((1, D_MODEL)),
        ],
        out_specs=[row(CONV_W), row(D_MODEL), row(D_MODEL), row(D_MODEL)],
        out_shape=[
            jax.ShapeDtypeStruct((lp, CONV_W), F32),
            jax.ShapeDtypeStruct((lp, D_MODEL), BF),
            jax.ShapeDtypeStruct((lp, D_MODEL), F32),
            jax.ShapeDtypeStruct((lp, D_MODEL), F32),
        ],
        scratch_shapes=[pltpu.VMEM((tm + 8, CONV_W), F32)],
        compiler_params=_params(),
    )(bch, y_attn, h, conv_w, g_a, g_c, w_out, g_post)


def _mlp_fwd(h2, g_pre, w_up_t, w_down, g_post, tm, name, exch=None, target=None):
    lp = h2.shape[0]
    sub = math.gcd(tm, BLOCK)
    n_sub, lead = tm // sub, BLOCK // sub
    n_t = n_sub if target is not None else 0

    def body(*refs):
        h_ref, gp_ref, wu_ref, wd_ref, gq_ref = refs[:5]
        t_refs = refs[5:5 + n_t]
        a_ref, up_ref, f_ref, last_ref = refs[5 + n_t:9 + n_t]
        h = h_ref[...]
        a = _rms_fwd(h, gp_ref[...]).astype(BF)
        a_ref[...] = a
        up = _dot_nt(a, wu_ref[...])
        up_ref[...] = up.astype(BF)
        act = jnp.square(jnp.maximum(up, 0.0)).astype(BF)
        f = _dot(act, wd_ref[...])
        f_ref[...] = f
        h3 = h + _rms_fwd(f, gq_ref[...])
        if target is None:
            last_ref[...] = h3
            return
        ls_ref = refs[9 + n_t]
        i = pl.program_id(0)

        @pl.when(i == 0)
        def _():
            ls_ref[...] = jnp.zeros((8, 128), F32)

        sq = jnp.zeros((1, 1), F32)
        for j in range(n_sub):
            on_tokens = i * n_sub + j >= lead
            d = jnp.where(on_tokens, h3[j * sub:(j + 1) * sub] - t_refs[j][...], 0.0)
            last_ref[j * sub:(j + 1) * sub, :] = d * (1.0 / D_MODEL)
            sq = sq + jnp.sum(d * d)
        ls_ref[...] += sq

    row = lambda w: pl.BlockSpec((tm, w), lambda i: (i, 0))
    piece = lambda j: pl.BlockSpec((sub, D_MODEL), lambda i: (jnp.maximum(i * n_sub + j - lead, 0), 0))
    out_specs = [row(D_MODEL), row(D_FF), row(D_MODEL), row(D_MODEL)]
    out_shape = [
        jax.ShapeDtypeStruct((lp, D_MODEL), BF),
        jax.ShapeDtypeStruct((lp, D_FF), BF),
        jax.ShapeDtypeStruct((lp, D_MODEL), F32),
        jax.ShapeDtypeStruct((lp, D_MODEL), F32),
    ]
    if target is not None:
        out_specs.append(_full_out((8, 128)))
        out_shape.append(jax.ShapeDtypeStruct((8, 128), F32))
    return _call(
        body, exch,
        name=name,
        grid=(lp // tm,),
        in_specs=[row(D_MODEL), _full((1, D_MODEL)), _full((D_FF, D_MODEL)), _full((D_FF, D_MODEL)), _full((1, D_MODEL))]
        + [piece(j) for j in range(n_t)],
        out_specs=out_specs,
        out_shape=out_shape,
        compiler_params=_params(),
    )(h2, g_pre, w_up_t, w_down, g_post, *([target] * n_t))


def _mlp_bwd_dx(dh3, f, up, h2, w_down, w_up_t, g_post, g_pre, tm, name, exch=None):
    lp = h2.shape[0]

    def body(dh3_ref, f_ref, up_ref, h2_ref, wd_ref, wu_ref, gq_ref, gp_ref, df_ref, dup_ref, dh2_ref, dg_ref):
        i = pl.program_id(0)

        @pl.when(i == 0)
        def _():
            dg_ref[...] = jnp.zeros((8, D_MODEL), F32)

        dh3 = dh3_ref[...]
        df, dgq = _rms_bwd(f_ref[...], gq_ref[...], dh3)
        dg_ref[ROW_MLP_POST:ROW_MLP_POST + 1, :] += dgq
        df = df.astype(BF)
        df_ref[...] = df
        dact = _dot_nt(df, wd_ref[...])
        dup = (dact * (2.0 * jnp.maximum(up_ref[...].astype(F32), 0.0))).astype(BF)
        dup_ref[...] = dup
        da = _dot(dup, wu_ref[...])
        dh, dgp = _rms_bwd(h2_ref[...], gp_ref[...], da)
        dg_ref[ROW_MLP_PRE:ROW_MLP_PRE + 1, :] += dgp
        dh2_ref[...] = dh3 + dh

    row = lambda w: pl.BlockSpec((tm, w), lambda i: (i, 0))
    return _call(
        body, exch,
        name=name,
        grid=(lp // tm,),
        in_specs=[
            row(D_MODEL), row(D_MODEL), row(D_FF), row(D_MODEL), _full((D_FF, D_MODEL)), _full((D_FF, D_MODEL)),
            _full((1, D_MODEL)), _full((1, D_MODEL)),
        ],
        out_specs=[row(D_MODEL), row(D_FF), row(D_MODEL), _full_out((8, D_MODEL))],
        out_shape=[
            jax.ShapeDtypeStruct((lp, D_MODEL), BF),
            jax.ShapeDtypeStruct((lp, D_FF), BF),
            jax.ShapeDtypeStruct((lp, D_MODEL), F32),
            jax.ShapeDtypeStruct((8, D_MODEL), F32),
        ],
        compiler_params=_params(),
    )(dh3, f, up, h2, w_down, w_up_t, g_post, g_pre)


def _mlp_bwd_dw(up, df, dup, a2, tm, name):
    lp = up.shape[0]
    nt = lp // tm
    nj = D_FF // D_MODEL

    def body(up_ref, df_ref, dup_ref, a_ref, dwd_ref, dwu_ref, accd, accu):
        i = pl.program_id(1)

        @pl.when(i == 0)
        def _():
            accd[...] = jnp.zeros_like(accd)
            accu[...] = jnp.zeros_like(accu)

        act = jnp.square(jnp.maximum(up_ref[...].astype(F32), 0.0)).astype(BF)
        accd[...] += _dot_tn(act, df_ref[...])
        accu[...] += _dot_tn(dup_ref[...], a_ref[...])

        @pl.when(i == nt - 1)
        def _():
            dwd_ref[...] = accd[...].astype(BF)
            dwu_ref[...] = accu[...].astype(BF)

    return pl.pallas_call(
        body,
        name=name,
        grid=(nj, nt),
        in_specs=[
            pl.BlockSpec((tm, D_MODEL), lambda j, i: (i, j)),
            pl.BlockSpec((tm, D_MODEL), lambda j, i: (i, 0)),
            pl.BlockSpec((tm, D_MODEL), lambda j, i: (i, j)),
            pl.BlockSpec((tm, D_MODEL), lambda j, i: (i, 0)),
        ],
        out_specs=[pl.BlockSpec((D_MODEL, D_MODEL), lambda j, i: (j, 0)), pl.BlockSpec((D_MODEL, D_MODEL), lambda j, i: (j, 0))],
        out_shape=[jax.ShapeDtypeStruct((D_FF, D_MODEL), BF), jax.ShapeDtypeStruct((D_FF, D_MODEL), BF)],
        scratch_shapes=[pltpu.VMEM((D_MODEL, D_MODEL), F32), pltpu.VMEM((D_MODEL, D_MODEL), F32)],
        compiler_params=_params(("arbitrary", "arbitrary")),
    )(up, df, dup, a2)


def _mix_out_bwd(dh2, z, y_attn, yc, bch, w_out, g_post, g_a, g_c, conv_w, tm, name, exch=None):
    lp = dh2.shape[0]
    nt = lp // tm

    def body(dh2_ref, z_ref, ya_ref, yc_ref, bch_ref, w_ref, gp_ref, ga_ref, gc_ref, cw_ref,
             dz_ref, dya_ref, dbch_ref, dg_ref, ext):
        i = pl.program_id(0)
        dcw_ref = dg_ref.at[ROW_CONV:ROW_CONV + 3, 0:CONV_W]

        @pl.when(i == 0)
        def _():
            ext[tm:tm + 8, :] = jnp.zeros((8, CONV_W), F32)
            dg_ref[...] = jnp.zeros((8, D_MODEL), F32)

        dz, dgp = _rms_bwd(z_ref[...], gp_ref[...], dh2_ref[...])
        dg_ref[ROW_MIX_POST:ROW_MIX_POST + 1, :] += dgp
        dz = dz.astype(BF)
        dz_ref[...] = dz
        dya_n = _dot_nt(dz, w_ref[0:ATTN_W, :])
        dyb_n = _dot_nt(dz, w_ref[ATTN_W:, :])
        dya, dga = _rms_bwd(ya_ref[...], ga_ref[...], dya_n)
        dg_ref[ROW_GROUP_G:ROW_GROUP_G + 1, 0:ATTN_W] += dga
        dya_ref[...] = dya
        b = bch_ref[:, 0:CONV_W]
        c = bch_ref[:, CONV_W:2 * CONV_W]
        hc = bch_ref[:, 2 * CONV_W:3 * CONV_W]
        yc_v = yc_ref[...]
        dyconv, dgc = _rms_bwd(b * yc_v, gc_ref[...], dyb_n)
        dg_ref[ROW_GROUP_G:ROW_GROUP_G + 1, ATTN_W:] += dgc
        dbch_ref[:, 0:CONV_W] = (dyconv * yc_v).astype(BF)
        dyc = dyconv * b
        ext[0:tm, :] = dyc
        d1 = ext[1:1 + tm, :]
        d2 = ext[2:2 + tm, :]
        du = cw_ref[2:3, :] * dyc + cw_ref[1:2, :] * d1 + cw_ref[0:1, :] * d2
        ext[tm:tm + 8, :] = dyc[0:8, :]
        dbch_ref[:, CONV_W:2 * CONV_W] = (du * hc).astype(BF)
        dbch_ref[:, 2 * CONV_W:3 * CONV_W] = (du * c).astype(BF)
        u = c * hc
        dcw_ref[0:1, :] += jnp.sum(u * d2, axis=0, keepdims=True)
        dcw_ref[1:2, :] += jnp.sum(u * d1, axis=0, keepdims=True)
        dcw_ref[2:3, :] += jnp.sum(u * dyc, axis=0, keepdims=True)

    row = lambda w: pl.BlockSpec((tm, w), lambda i: (nt - 1 - i, 0))
    return _call(
        body, exch,
        name=name,
        grid=(nt,),
        in_specs=[
            row(D_MODEL), row(D_MODEL), row(ATTN_W), row(CONV_W), row(3 * CONV_W), _full((D_MODEL, D_MODEL)),
            _full((1, D_MODEL)), _full((1, ATTN_W)), _full((1, CONV_W)), _full((8, CONV_W)),
        ],
        out_specs=[row(D_MODEL), row(ATTN_W), row(3 * CONV_W), _full_out((8, D_MODEL))],
        out_shape=[
            jax.ShapeDtypeStruct((lp, D_MODEL), BF),
            jax.ShapeDtypeStruct((lp, ATTN_W), F32),
            jax.ShapeDtypeStruct((lp, 3 * CONV_W), BF),
            jax.ShapeDtypeStruct((8, D_MODEL), F32),
        ],
        scratch_shapes=[pltpu.VMEM((tm + 8, CONV_W), F32)],
        compiler_params=_params(),
    )(dh2, z, y_attn, yc, bch, w_out, g_post, g_a, g_c, conv_w)


def _attn_bwd(qkv, o, do, sink, rope, name, exch=None):
    lp = qkv.shape[0]
    nb = lp // BLOCK

    def body(sink_ref, q_ref, kvc_ref, kvp_ref, o_ref, do_ref, rq_ref, rk_ref, dq_ref, dkv_ref, dsink_ref, carry):
        i = pl.program_id(0)

        @pl.when(i == 0)
        def _():
            carry[...] = jnp.zeros_like(carry)
            dsink_ref[...] = jnp.zeros((8, 128), F32)

        def finish(tot):
            dk = _rope_t(tot[:, :128], *_rope_coeffs(rk_ref[...]))
            dkv_ref[:, 0:128] = dk.astype(BF)
            dkv_ref[:, 128:256] = tot[:, 128:].astype(BF)

        @pl.when(i < nb)
        def _():
            tri, ok = _fold_masks(i)
            kvc, kvp = kvc_ref[...], kvp_ref[...]
            kk = jnp.concatenate([kvp[:, :128], kvc[:, :128]], axis=0)
            vv = jnp.concatenate([kvp[:, 128:], kvc[:, 128:]], axis=0)
            lane = lax.broadcasted_iota(jnp.int32, (BLOCK, 128), 1)
            lane2 = lax.broadcasted_iota(jnp.int32, (2 * BLOCK, 128), 1)
            first = lax.broadcasted_iota(jnp.int32, (2 * BLOCK, 1), 0) < BLOCK
            row_s = lax.broadcasted_iota(jnp.int32, (8, 128), 0)
            lane_s = jnp.where(row_s == ROW_SINK, lax.broadcasted_iota(jnp.int32, (8, 128), 1), -1)
            rope_q = _rope_coeffs(rq_ref[...])
            dsink = jnp.zeros((8, 128), F32)
            folded = []
            for kvh in range(2):
                c0 = 256 * kvh
                q2 = jnp.concatenate([q_ref[:, c0:c0 + 128], q_ref[:, c0 + 128:c0 + 256]], axis=0)
                do2 = jnp.concatenate([do_ref[:, c0:c0 + 128], do_ref[:, c0 + 128:c0 + 256]], axis=0)
                o2 = jnp.concatenate([o_ref[:, c0:c0 + 128], o_ref[:, c0 + 128:c0 + 256]], axis=0)
                k4, v4 = _kv_operand(kk, kvh), _kv_operand(vv, kvh)
                probs = _folded_probs(q2, k4, tri, ok, _sink_cols(sink_ref, kvh))
                prod = do2 * o2
                dob = do2.astype(BF)
                dp = _dot_nt(dob, v4)
                ds, pb = [], []
                for half in range(2):
                    p, ps = probs[half]
                    sel = (lane2 < HEAD_DIM) if half == 0 else (lane2 >= HEAD_DIM)
                    delta = jnp.sum(jnp.where(sel, prod, 0.0), axis=-1, keepdims=True)
                    dp_h = dp[:, 2 * half * BLOCK:2 * (half + 1) * BLOCK]
                    ds.append((p * (jnp.where(tri, dp_h[:, :BLOCK], dp_h[:, BLOCK:]) - delta)).astype(BF))
                    pb.append(p.astype(BF))
                    t = ps * delta
                    for jj in range(2):
                        part = -jnp.sum(jnp.where(first if jj == 0 else ~first, t, 0.0))
                        dsink = dsink + jnp.where(lane_s == 4 * kvh + 2 * jj + half, part, 0.0)
                ds4, p4 = _split4(ds, tri), _split4(pb, tri)
                dq2 = _dot(ds4, k4) * SCALE
                dq_ref[:, c0:c0 + 128] = _rope_t(dq2[:BLOCK], *rope_q).astype(BF)
                dq_ref[:, c0 + 128:c0 + 256] = _rope_t(dq2[BLOCK:], *rope_q).astype(BF)
                rk, rv = _dot_tn(ds4, q2), _dot_tn(p4, dob)
                own = (lane < HEAD_DIM) if kvh == 0 else (lane >= HEAD_DIM)
                group = []
                for r in (rk, rv):
                    for blk in range(2):
                        t = jnp.where(lane < HEAD_DIM, r[blk * BLOCK:(blk + 1) * BLOCK], r[(2 + blk) * BLOCK:(3 + blk) * BLOCK])
                        group.append(jnp.where(own, t + pltpu.roll(t, HEAD_DIM, 1), 0.0))
                folded.append(group)
            dsink_ref[...] += dsink
            dk_p, dk_c, dv_p, dv_c = [folded[0][t] + folded[1][t] for t in range(4)]
            finish(carry[...] + jnp.concatenate([dk_p, dv_p], axis=1))
            carry[...] = jnp.concatenate([dk_c, dv_c], axis=1)

        @pl.when(i == nb)
        def _():
            finish(carry[...])

    qi = lambda i: jnp.minimum(i, nb - 1)
    ki = lambda i: jnp.maximum(i - 1, 0)
    tab_q = pl.BlockSpec((BLOCK, 128), lambda i: (qi(i), 0))
    tab_k = pl.BlockSpec((BLOCK, 128), lambda i: (ki(i), 0))
    return _call(
        body, exch,
        name=name,
        grid=(nb + 1,),
        in_specs=[
            pl.BlockSpec(memory_space=pltpu.SMEM),
            pl.BlockSpec((BLOCK, ATTN_W), lambda i: (qi(i), 0)),
            pl.BlockSpec((BLOCK, 256), lambda i: (qi(i), 2)),
            pl.BlockSpec((BLOCK, 256), lambda i: (jnp.maximum(qi(i) - 1, 0), 2)),
            pl.BlockSpec((BLOCK, ATTN_W), lambda i: (qi(i), 0)),
            pl.BlockSpec((BLOCK, ATTN_W), lambda i: (qi(i), 0)),
            tab_q, tab_k,
        ],
        out_specs=[
            pl.BlockSpec((BLOCK, ATTN_W), lambda i: (qi(i), 0)),
            pl.BlockSpec((BLOCK, 256), lambda i: (ki(i), 0)),
            pl.BlockSpec((8, 128), lambda i: (0, 0)),
        ],
        out_shape=[
            jax.ShapeDtypeStruct((lp, ATTN_W), BF),
            jax.ShapeDtypeStruct((lp, 256), BF),
            jax.ShapeDtypeStruct((8, 128), F32),
        ],
        scratch_shapes=[pltpu.VMEM((BLOCK, 256), F32)],
        compiler_params=_params(),
    )(sink, qkv, qkv, qkv, o, do, rope, rope)


def _in_proj_bwd_dx(dq, dkv, dbch, w_in_t, h, dh2, g, tm, name, exch=None):
    lp = h.shape[0]

    def body(dq_ref, dkv_ref, dbch_ref, w_ref, h_ref, dh2_ref, g_ref, dh_ref, dg_ref):
        i = pl.program_id(0)

        @pl.when(i == 0)
        def _():
            dg_ref[...] = jnp.zeros((8, D_MODEL), F32)

        da = _dot(dq_ref[...], w_ref[0:512, :]) + _dot(dkv_ref[...], w_ref[512:768, :]) + _dot(dbch_ref[...], w_ref[768:, :])
        dh, dg = _rms_bwd(h_ref[...], g_ref[...], da)
        dg_ref[ROW_MIX_PRE:ROW_MIX_PRE + 1, :] += dg
        dh_ref[...] = dh2_ref[...] + dh

    row = lambda w: pl.BlockSpec((tm, w), lambda i: (i, 0))
    return _call(
        body, exch,
        name=name,
        grid=(lp // tm,),
        in_specs=[row(ATTN_W), row(256), row(3 * CONV_W), _full((IN_W, D_MODEL)), row(D_MODEL), row(D_MODEL), _full((1, D_MODEL))],
        out_specs=[row(D_MODEL), _full_out((8, D_MODEL))],
        out_shape=[jax.ShapeDtypeStruct((lp, D_MODEL), F32), jax.ShapeDtypeStruct((8, D_MODEL), F32)],
        compiler_params=_params(),
    )(dq, dkv, dbch, w_in_t, h, dh2, g)


def _mix_bwd_dw(dq, dkv, dbch, a, y, dz, tm, name, exch=None):
    lp = a.shape[0]
    nt = lp // tm

    def body(dq_ref, dkv_ref, dbch_ref, a_ref, y_ref, dz_ref, dwi_ref, dwo_ref, acci, acco):
        i = pl.program_id(0)

        @pl.when(i == 0)
        def _():
            acci[...] = jnp.zeros_like(acci)
            acco[...] = jnp.zeros_like(acco)

        a_v = a_ref[...]
        acci[0:512, :] += _dot_tn(dq_ref[...], a_v)
        acci[512:768, :] += _dot_tn(dkv_ref[...], a_v)
        acci[768:, :] += _dot_tn(dbch_ref[...], a_v)
        acco[...] += _dot_tn(y_ref[...], dz_ref[...])

        @pl.when(i == nt - 1)
        def _():
            dwi_ref[...] = acci[...].astype(BF)
            dwo_ref[...] = acco[...].astype(BF)

    row = lambda w: pl.BlockSpec((tm, w), lambda i: (i, 0))
    return _call(
        body, exch,
        name=name,
        grid=(nt,),
        in_specs=[row(ATTN_W), row(256), row(3 * CONV_W), row(D_MODEL), row(D_MODEL), row(D_MODEL)],
        out_specs=[_full_out((IN_W, D_MODEL)), _full_out((D_MODEL, D_MODEL))],
        out_shape=[jax.ShapeDtypeStruct((IN_W, D_MODEL), BF), jax.ShapeDtypeStruct((D_MODEL, D_MODEL), BF)],
        scratch_shapes=[pltpu.VMEM((IN_W, D_MODEL), F32), pltpu.VMEM((D_MODEL, D_MODEL), F32)],
        compiler_params=_params(),
    )(dq, dkv, dbch, a, y, dz)


def _mesh_place():
    x, y, c = lax.axis_index("x"), lax.axis_index("y"), lax.axis_index("c")
    return x, y, c, 4 * x + 2 * y + c


def _peer(x, y, c, k):
    px = 1 - x if k & 4 else x
    py = 1 - y if k & 2 else y
    pc = 1 - c if k & 1 else c
    return (px, py, pc), 4 * px + 2 * py + pc


class _Exchange:
    def __init__(self, pieces):
        self.srcs = [s for s, _ in pieces]
        self.to_all = [g for _, g in pieces]
        self.n = len(pieces)
        self.land_shapes = [
            jax.ShapeDtypeStruct((N_DEV,) + (s.shape if g else s.shape[1:]), s.dtype) for s, g in pieces]
        self.sem_shapes = [pltpu.SemaphoreType.DMA((self.n, N_DEV - 1)), pltpu.SemaphoreType.DMA((self.n, N_DEV - 1)),
                           pltpu.SemaphoreType.DMA((self.n,))]

    def _copies(self, srcs, lands, sems, receiving):
        send_sems, recv_sems, local_sems = sems
        x, y, c, me = _mesh_place()
        own, remote = [], []
        for p in range(self.n):
            block = lambda d: srcs[p] if self.to_all[p] else srcs[p].at[d]
            own.append(pltpu.make_async_copy(block(me), lands[p].at[me], local_sems.at[p]))
            for k in range(1, N_DEV):
                peer, pidx = _peer(x, y, c, k)
                remote.append(pltpu.make_async_remote_copy(
                    src_ref=block(pidx), dst_ref=lands[p].at[pidx if receiving else me], send_sem=send_sems.at[p, k - 1],
                    recv_sem=recv_sems.at[p, k - 1], device_id=peer, device_id_type=MESH))
        return own, remote

    def start(self, srcs, lands, sems):
        own, remote = self._copies(srcs, lands, sems, False)
        for cp in own + remote:
            cp.start()

    def wait(self, srcs, lands, sems):
        own, remote = self._copies(srcs, lands, sems, True)
        for cp in remote:
            cp.wait_recv()
        for cp in remote:
            cp.wait_send()
        for cp in own:
            cp.wait()


def _exchange_call(exch, name):
    def body(*refs):
        srcs, lands, sems = refs[:exch.n], refs[exch.n:2 * exch.n], refs[2 * exch.n:]
        exch.start(srcs, lands, sems)
        exch.wait(srcs, lands, sems)

    hbm = pl.BlockSpec(memory_space=pl.ANY)
    return pl.pallas_call(
        body,
        name=name,
        in_specs=[hbm] * exch.n,
        out_specs=[hbm] * exch.n,
        out_shape=exch.land_shapes,
        scratch_shapes=exch.sem_shapes,
    )(*exch.srcs)


def _call(body, exch, *, name, grid, in_specs, out_specs, out_shape, scratch_shapes=(), compiler_params):
    if exch is None:
        return pl.pallas_call(body, name=name, grid=grid, in_specs=in_specs, out_specs=out_specs, out_shape=out_shape,
                              scratch_shapes=scratch_shapes, compiler_params=compiler_params)
    n_in, n_out, n_scr, n_x = len(in_specs), len(out_shape), len(scratch_shapes), exch.n

    def carrying(*refs):
        a, b, c, d, e = n_in, n_in + n_x, n_in + n_x + n_out, n_in + 2 * n_x + n_out, n_in + 2 * n_x + n_out + n_scr
        ins, srcs, outs, lands, scr, sems = refs[:a], refs[a:b], refs[b:c], refs[c:d], refs[d:e], refs[e:]
        ids = [pl.program_id(t) for t in range(len(grid))]
        first = functools.reduce(jnp.logical_and, [i == 0 for i in ids])
        last = functools.reduce(jnp.logical_and, [i == g - 1 for i, g in zip(ids, grid)])

        @pl.when(first)
        def _():
            exch.start(srcs, lands, sems)

        body(*ins, *outs, *scr)

        @pl.when(last)
        def _():
            exch.wait(srcs, lands, sems)

    hbm = pl.BlockSpec(memory_space=pl.ANY)
    call = pl.pallas_call(
        carrying, name=name, grid=grid, in_specs=list(in_specs) + [hbm] * n_x, out_specs=list(out_specs) + [hbm] * n_x,
        out_shape=list(out_shape) + exch.land_shapes, scratch_shapes=list(scratch_shapes) + exch.sem_shapes,
        compiler_params=compiler_params)

    def run(*args):
        res = call(*args, *exch.srcs)
        return list(res[:n_out]), list(res[n_out:])

    return run


def _sum_small(part):
    def body(part_ref, out_ref, land, send_sems, recv_sems):
        x, y, c, me = _mesh_place()
        land[me] = part_ref[...]
        sent = []
        for k in range(1, N_DEV):
            peer, _ = _peer(x, y, c, k)
            cp = pltpu.make_async_remote_copy(
                src_ref=part_ref, dst_ref=land.at[me], send_sem=send_sems.at[k - 1], recv_sem=recv_sems.at[k - 1],
                device_id=peer, device_id_type=MESH)
            cp.start()
            sent.append(cp)
        for k in range(1, N_DEV):
            peer, pidx = _peer(x, y, c, k)
            pltpu.make_async_remote_copy(
                src_ref=part_ref, dst_ref=land.at[pidx], send_sem=send_sems.at[k - 1], recv_sem=recv_sems.at[k - 1],
                device_id=peer, device_id_type=MESH).wait_recv()
        for cp in sent:
            cp.wait_send()
        acc = land[0]
        for d in range(1, N_DEV):
            acc = acc + land[d]
        out_ref[...] = acc

    vmem = pl.BlockSpec(memory_space=pltpu.VMEM)
    return pl.pallas_call(
        body,
        name="sum_small",
        in_specs=[vmem],
        out_specs=vmem,
        out_shape=jax.ShapeDtypeStruct(part.shape, F32),
        scratch_shapes=[pltpu.VMEM((N_DEV,) + part.shape, F32), pltpu.SemaphoreType.DMA((N_DEV - 1,)),
                        pltpu.SemaphoreType.DMA((N_DEV - 1,))],
    )(part)


def _adamw(w, g, m, v):
    m = ADAM_B1 * m + (1.0 - ADAM_B1) * g
    v = ADAM_B2 * v + (1.0 - ADAM_B2) * jnp.square(g)
    m_hat = m / (1.0 - ADAM_B1 ** ADAM_STEP)
    v_hat = v / (1.0 - ADAM_B2 ** ADAM_STEP)
    delta = -ADAM_LR * (m_hat / (jnp.sqrt(v_hat) + ADAM_EPS) + ADAM_WD * w)
    return delta, m, v


def _landed_specs(tr, wd):
    return [pl.BlockSpec((N_DEV, tr, wd), lambda l, i, ll=ll: (0, jnp.where(l == ll, i, 0), 0)) for ll in range(DEPTH)]


def _device_sum(r_ref):
    acc = r_ref[0].astype(F32)
    for d in range(1, N_DEV):
        acc = acc + r_ref[d].astype(F32)
    return acc


def _sum_parts(recv, tr, name):
    _, r, wd = recv[0].shape

    def body(*refs):
        g_ref = refs[DEPTH]
        for ll in range(DEPTH):
            @pl.when(pl.program_id(0) == ll)
            def _(ll=ll):
                g_ref[0] = _device_sum(refs[ll])

    return pl.pallas_call(
        body,
        name=name,
        grid=(DEPTH, r // tr),
        in_specs=_landed_specs(tr, wd),
        out_specs=pl.BlockSpec((1, tr, wd), lambda l, i: (l, i, 0)),
        out_shape=jax.ShapeDtypeStruct((DEPTH, r, wd), F32),
        compiler_params=_params(("arbitrary", "arbitrary")),
    )(*recv)


def _sum_adamw(recv, w, m, v, tr, name):
    _, r, wd = recv[0].shape

    def body(*refs):
        w_ref, m_ref, v_ref, g_ref, d_ref, mo_ref, vo_ref = refs[DEPTH:]
        for ll in range(DEPTH):
            @pl.when(pl.program_id(0) == ll)
            def _(ll=ll):
                g = _device_sum(refs[ll])
                g_ref[0] = g
                d_ref[0], mo_ref[0], vo_ref[0] = _adamw(w_ref[0], g, m_ref[0], v_ref[0])

    blk = pl.BlockSpec((1, tr, wd), lambda l, i: (l, i, 0))
    shape = jax.ShapeDtypeStruct((DEPTH, r, wd), F32)
    return pl.pallas_call(
        body,
        name=name,
        grid=(DEPTH, r // tr),
        in_specs=_landed_specs(tr, wd) + [blk, blk, blk],
        out_specs=[blk] * 4,
        out_shape=[shape] * 4,
        compiler_params=_params(("arbitrary", "arbitrary")),
    )(*recv, w, m, v)


def _adamw_rows(w, g, m, v, tr, name):
    _, r, wd = w.shape

    def body(w_ref, g_ref, m_ref, v_ref, d_ref, mo_ref, vo_ref):
        d_ref[0], mo_ref[0], vo_ref[0] = _adamw(w_ref[0], g_ref[0], m_ref[0], v_ref[0])

    blk = pl.BlockSpec((1, tr, wd), lambda l, i: (l, i, 0))
    shape = jax.ShapeDtypeStruct(w.shape, F32)
    return pl.pallas_call(
        body,
        name=name,
        grid=(DEPTH, r // tr),
        in_specs=[blk] * 4,
        out_specs=[blk] * 3,
        out_shape=[shape] * 3,
        compiler_params=_params(("arbitrary", "arbitrary")),
    )(w, g, m, v)


def _adamw_small(ws, gs, ms, vs):
    n = len(ws)

    def body(*refs):
        w_r, g_r, m_r, v_r = refs[:n], refs[n:2 * n], refs[2 * n:3 * n], refs[3 * n:4 * n]
        d_o, m_o, v_o = refs[4 * n:5 * n], refs[5 * n:6 * n], refs[6 * n:7 * n]
        for t in range(n):
            d_o[t][...], m_o[t][...], v_o[t][...] = _adamw(w_r[t][...], g_r[t][...], m_r[t][...], v_r[t][...])

    vmem = pl.BlockSpec(memory_space=pltpu.VMEM)
    shapes = [jax.ShapeDtypeStruct(w.shape, F32) for w in ws]
    outs = pl.pallas_call(
        body,
        name="adamw_small",
        in_specs=[vmem] * (4 * n),
        out_specs=[vmem] * (3 * n),
        out_shape=shapes * 3,
    )(*ws, *gs, *ms, *vs)
    return outs[:n], outs[n:2 * n], outs[2 * n:]


def kernel(x, meta_tokens, mix_pre_g, w_in, conv_w, sinks, attn_out_g, conv_out_g, w_out, mix_post_g, mlp_pre_g, w_up, w_down, mlp_post_g, loss_target, m_meta_tokens, m_mix_pre_g, m_w_in, m_conv_w, m_sinks, m_attn_out_g, m_conv_out_g, m_w_out, m_mix_post_g, m_mlp_pre_g, m_w_up, m_w_down, m_mlp_post_g, v_meta_tokens, v_mix_pre_g, v_w_in, v_conv_w, v_sinks, v_attn_out_g, v_conv_out_g, v_w_out, v_mix_post_g, v_mlp_pre_g, v_w_up, v_w_down, v_mlp_post_g):
    seq = x.shape[1]
    lp = BLOCK + seq
    tm = _row_tile(lp)
    tm_mlp = _row_tile(lp, (320, 256, 128))
    tm_dw_mlp = _row_tile(lp, (1664, 1040, 640, 384, 256, 128))
    tm_dw_mix = _row_tile(lp, (832, 640, 384, 256, 128))
    me = 4 * lax.axis_index("x") + 2 * lax.axis_index("y") + lax.axis_index("c")
    cshard = CONV_W // N_DEV
    mshard = D_MODEL // N_DEV

    gather_with = {
        ("in_proj_fwd", 0): [("out", 0)], ("attn_fwd", 0): [("up", 0)], ("mix_out_fwd", 0): [("down", 0)],
        ("mlp_fwd", 0): [("in", 1), ("out", 1), ("up", 1)], ("attn_fwd", 1): [("down", 1)],
    }
    scatter_with = {
        ("attn_bwd", 1): [("down", 1)], ("mlp_bwd_dx", 0): [("up", 1), ("in", 1), ("out", 1)],
        ("attn_bwd", 0): [("down", 0)], ("mix_bwd_dw", 0): [("up", 0)], ("in_proj_bwd_dx", 0): [("in", 0), ("out", 0)],
    }
    shard = {"in": jnp.swapaxes(w_in, 1, 2).astype(BF), "out": w_out.astype(BF),
             "up": jnp.swapaxes(w_up, 1, 2).astype(BF), "down": w_down.astype(BF)}
    weight = {}
    grad = {}
    landed = {}

    def run(fn, kind, l, *args):
        key, name = (kind, l), f"{kind}_{l}"
        if key in gather_with:
            blocks = gather_with[key]
            outs, lands = fn(*args, name, _Exchange([(shard[n][k], True) for n, k in blocks]))
            for b, land in zip(blocks, lands):
                weight[b] = land.reshape(-1, D_MODEL)
            return outs
        if key in scatter_with:
            blocks = scatter_with[key]
            outs, lands = fn(*args, name, _Exchange([(grad[b].reshape(N_DEV, -1, D_MODEL), False) for b in blocks]))
            landed.update(zip(blocks, lands))
            return outs
        return fn(*args, name)

    small = jnp.zeros((24, 128), F32)
    small = small.at[0:N_META, :].set(meta_tokens)
    small = small.at[N_META:N_META + 6, 0:cshard].set(conv_w.reshape(6, cshard))
    first_in, g_small = _exchange_call(_Exchange([(shard["in"][0], True), (small, True)]), "gather_first")
    weight[("in", 0)] = first_in.reshape(-1, D_MODEL)
    meta_full = jnp.swapaxes(g_small[:, 0:N_META, :], 0, 1).reshape(N_META, D_MODEL)
    cw = g_small[:, N_META:N_META + 6, 0:cshard].reshape(N_DEV, DEPTH, 3, cshard)
    cw = jnp.transpose(cw, (1, 2, 0, 3)).reshape(DEPTH, 3, CONV_W)
    conv_full = jnp.concatenate([cw, jnp.zeros((DEPTH, 5, CONV_W), F32)], axis=1)

    rope = _rope_table(lp)
    row1 = lambda a, l: a[l].reshape(1, -1)

    h = jnp.concatenate([jnp.zeros((LEAD_PAD, D_MODEL), F32), meta_full, x[0]], axis=0)
    saved = []
    for l in range(DEPTH):
        a, qkv, bch = run(_in_proj_fwd, "in_proj_fwd", l, h, row1(mix_pre_g, l), weight[("in", l)], rope, tm)
        y_attn, = run(_attn_fwd, "attn_fwd", l, qkv, row1(sinks, l))
        yc, y, z, h2 = run(_mix_out_fwd, "mix_out_fwd", l, bch, y_attn, h, conv_full[l], row1(attn_out_g, l),
                           row1(conv_out_g, l), weight[("out", l)], row1(mix_post_g, l), tm)
        mlp = _mlp_fwd if l < DEPTH - 1 else functools.partial(_mlp_fwd, target=loss_target[0])
        a2, up, f, *rest = run(mlp, "mlp_fwd", l, h2, row1(mlp_pre_g, l), weight[("up", l)], weight[("down", l)],
                               row1(mlp_post_g, l), tm_mlp)
        saved.append((h, a, qkv, bch, y_attn, yc, y, z, h2, a2, up, f))
        h = rest[0]
    dh, loss_part = rest[0], rest[1][0, 0] * (0.5 / D_MODEL)

    gsmall = [None] * DEPTH
    for l in reversed(range(DEPTH)):
        h0, a, qkv, bch, y_attn, yc, y, z, h2, a2, up, f = saved[l]
        df, dup, dh2, dg_mlp = run(_mlp_bwd_dx, "mlp_bwd_dx", l, dh, f, up, h2, weight[("down", l)], weight[("up", l)],
                                   row1(mlp_post_g, l), row1(mlp_pre_g, l), tm_mlp)
        grad[("down", l)], grad[("up", l)] = _mlp_bwd_dw(up, df, dup, a2, tm_dw_mlp, f"mlp_bwd_dw_{l}")
        dz, dya, dbch, dg_mix = run(_mix_out_bwd, "mix_out_bwd", l, dh2, z, y_attn, yc, bch, weight[("out", l)],
                                    row1(mix_post_g, l), row1(attn_out_g, l), row1(conv_out_g, l), conv_full[l], tm)
        dq, dkv, dsink = run(_attn_bwd, "attn_bwd", l, qkv, y_attn, dya, row1(sinks, l), rope)
        grad[("in", l)], grad[("out", l)] = run(_mix_bwd_dw, "mix_bwd_dw", l, dq, dkv, dbch, a, y, dz, tm_dw_mix)
        dh, dg_in = run(_in_proj_bwd_dx, "in_proj_bwd_dx", l, dq, dkv, dbch, weight[("in", l)], h0, dh2,
                        row1(mix_pre_g, l), tm)
        tile_a = dg_mlp + dg_in + jnp.pad(dsink, ((0, 0), (0, D_MODEL - 128)))
        gsmall[l] = (tile_a, dg_mix)
    grad_x = dh[BLOCK:][None]

    loss_tile = jnp.zeros((8, D_MODEL), F32).at[ROW_LOSS, 0].set(loss_part)
    tot = _sum_small(jnp.concatenate(
        [gsmall[0][0] + loss_tile, gsmall[0][1], gsmall[1][0], gsmall[1][1], dh[LEAD_PAD:BLOCK]], axis=0))
    loss = tot[ROW_LOSS, 0]
    ta = [tot[16 * l:16 * l + 8] for l in range(DEPTH)]
    tb = [tot[16 * l + 8:16 * l + 16] for l in range(DEPTH)]
    pick = lambda tiles, r0, r1, c0, c1: jnp.stack([t[r0:r1, c0:c1] for t in tiles])
    g_mlp_post = pick(ta, ROW_MLP_POST, ROW_MLP_POST + 1, 0, D_MODEL).reshape(DEPTH, D_MODEL)
    g_mlp_pre = pick(ta, ROW_MLP_PRE, ROW_MLP_PRE + 1, 0, D_MODEL).reshape(DEPTH, D_MODEL)
    g_mix_pre = pick(ta, ROW_MIX_PRE, ROW_MIX_PRE + 1, 0, D_MODEL).reshape(DEPTH, D_MODEL)
    g_sinks = pick(ta, ROW_SINK, ROW_SINK + 1, 0, N_Q_HEADS).reshape(DEPTH, N_Q_HEADS)
    g_mix_post = pick(tb, ROW_MIX_POST, ROW_MIX_POST + 1, 0, D_MODEL).reshape(DEPTH, D_MODEL)
    g_attn_out = pick(tb, ROW_GROUP_G, ROW_GROUP_G + 1, 0, ATTN_W).reshape(DEPTH, ATTN_W)
    g_conv_out = pick(tb, ROW_GROUP_G, ROW_GROUP_G + 1, ATTN_W, D_MODEL).reshape(DEPTH, CONV_W)
    g_conv_full = pick(tb, ROW_CONV, ROW_CONV + 3, 0, CONV_W)
    g_conv = lax.dynamic_slice_in_dim(g_conv_full, me * cshard, cshard, axis=2)
    g_meta = lax.dynamic_slice_in_dim(tot[16 * DEPTH:16 * DEPTH + N_META], me * mshard, mshard, axis=1)

    r_in, r_out, r_up, r_down = [[landed[(n, l)] for l in range(DEPTH)] for n in ("in", "out", "up", "down")]
    g_w_in = jnp.swapaxes(_sum_parts(r_in, 96, "sum_w_in"), 1, 2)
    g_w_up = jnp.swapaxes(_sum_parts(r_up, 128, "sum_w_up"), 1, 2)
    d_w_in, nm_w_in, nv_w_in = _adamw_rows(w_in, g_w_in, m_w_in, v_w_in, 256, "adamw_w_in")
    d_w_up, nm_w_up, nv_w_up = _adamw_rows(w_up, g_w_up, m_w_up, v_w_up, 256, "adamw_w_up")
    g_w_out, d_w_out, nm_w_out, nv_w_out = _sum_adamw(r_out, w_out, m_w_out, v_w_out, 128, "adamw_w_out")
    g_w_down, d_w_down, nm_w_down, nv_w_down = _sum_adamw(r_down, w_down, m_w_down, v_w_down, 128, "adamw_w_down")

    ws = [meta_tokens, mix_pre_g, conv_w.reshape(6, cshard), sinks, attn_out_g, conv_out_g, mix_post_g, mlp_pre_g, mlp_post_g]
    gs = [g_meta, g_mix_pre, g_conv.reshape(6, cshard), g_sinks, g_attn_out, g_conv_out, g_mix_post, g_mlp_pre, g_mlp_post]
    ms = [m_meta_tokens, m_mix_pre_g, m_conv_w.reshape(6, cshard), m_sinks, m_attn_out_g, m_conv_out_g, m_mix_post_g,
          m_mlp_pre_g, m_mlp_post_g]
    vs = [v_meta_tokens, v_mix_pre_g, v_conv_w.reshape(6, cshard), v_sinks, v_attn_out_g, v_conv_out_g, v_mix_post_g,
          v_mlp_pre_g, v_mlp_post_g]
    ds, nms, nvs = _adamw_small(ws, gs, ms, vs)

    def order(meta, mix_pre, cv, sk, a_out, c_out, mix_post, mlp_pre, mlp_post, win, wout, wup, wdown):
        return [meta, mix_pre, win, cv.reshape(DEPTH, 3, cshard), sk, a_out, c_out, wout, mix_post, mlp_pre, wup, wdown, mlp_post]

    grads = order(*gs, g_w_in, g_w_out, g_w_up, g_w_down)
    deltas = order(*ds, d_w_in, d_w_out, d_w_up, d_w_down)
    new_m = order(*nms, nm_w_in, nm_w_out, nm_w_up, nm_w_down)
    new_v = order(*nvs, nv_w_in, nv_w_out, nv_w_up, nv_w_down)
    return (loss, grad_x, *grads, *deltas, *new_m, *new_v)
```

```python
import functools
import math

import jax
import jax.numpy as jnp
from jax import lax
from jax.experimental import pallas as pl
from jax.experimental.pallas import tpu as pltpu

F32 = jnp.float32
BF = jnp.bfloat16

D_MODEL = 1024
ATTN_W = 512
CONV_W = 512
KV_W = 128
HEAD_DIM = 64
N_Q_HEADS = 8
ROT_DIM = 16
D_FF = 4096
IN_W = 2304
N_META = 16
BLOCK = 128
LEAD_PAD = BLOCK - N_META
ROPE_THETA = 500000.0
EPS = 1e-6
N_DEV = 8
DEPTH = 2
NEG = -1e30
SCALE = HEAD_DIM ** -0.5

ADAM_LR = 0.001
ADAM_B1 = 0.9
ADAM_B2 = 0.999
ADAM_EPS = 1e-08
ADAM_WD = 0.01
ADAM_STEP = 10

ROW_MLP_POST, ROW_MLP_PRE, ROW_MIX_PRE, ROW_SINK, ROW_LOSS = 0, 1, 2, 3, 4
ROW_MIX_POST, ROW_GROUP_G, ROW_CONV = 0, 1, 2

VMEM_LIMIT = 56 * 1024 * 1024
MESH = pl.DeviceIdType.MESH


def _dot(a, b):
    return jnp.dot(a, b, preferred_element_type=F32)


def _dot_nt(a, b):
    return lax.dot_general(a, b, (((1,), (1,)), ((), ())), preferred_element_type=F32)


def _dot_tn(a, b):
    return lax.dot_general(a, b, (((0,), (0,)), ((), ())), preferred_element_type=F32)


def _rms_fwd(x, g):
    r = lax.rsqrt(jnp.mean(x * x, axis=-1, keepdims=True) + EPS)
    return x * r * g


def _rms_bwd(x, g, dy):
    r = lax.rsqrt(jnp.mean(x * x, axis=-1, keepdims=True) + EPS)
    xh = x * r
    t = dy * g
    dx = r * (t - xh * jnp.mean(t * xh, axis=-1, keepdims=True))
    dg = jnp.sum(dy * xh, axis=0, keepdims=True)
    return dx, dg


def _row_tile(lp, cands=(640, 512, 384, 256, 128)):
    for t in cands:
        if lp % t == 0:
            return t
    raise ValueError(f"row count {lp} is not a multiple of 128")


def _full(shape):
    n = len(shape)
    return pl.BlockSpec(shape, lambda *_: (0,) * n, pipeline_mode=pl.Buffered(1))


def _full_out(shape):
    n = len(shape)
    return pl.BlockSpec(shape, lambda *_: (0,) * n)


def _params(sem=("arbitrary",)):
    return pltpu.CompilerParams(dimension_semantics=sem, vmem_limit_bytes=VMEM_LIMIT)


def _rope_table(lp):
    half = ROT_DIM // 2
    pos = jnp.maximum(jnp.arange(lp) - LEAD_PAD, 0).astype(F32)
    inv_freq = jnp.power(jnp.float32(ROPE_THETA), -jnp.arange(0, ROT_DIM, 2, dtype=F32) / ROT_DIM)
    ang_t = jnp.concatenate([inv_freq, inv_freq])[:, None] * pos[None, :]
    row = lax.broadcasted_iota(jnp.int32, (ROT_DIM, lp), 0)
    cs_t = jnp.where(row < half, jnp.cos(ang_t), jnp.sin(ang_t))
    return jnp.pad(cs_t.T, ((0, 0), (0, 128 - ROT_DIM)))


def _rope_coeffs(t):
    half = ROT_DIM // 2
    lane = lax.broadcasted_iota(jnp.int32, t.shape, 1)
    cos_a = jnp.where(lane < half, t, 0.0)
    sin_a = pltpu.roll(jnp.where((lane >= half) & (lane < ROT_DIM), t, 0.0), 128 - half, 1)
    c = cos_a + pltpu.roll(cos_a, half, 1) + jnp.where((lane >= ROT_DIM) & (lane < HEAD_DIM), 1.0, 0.0)
    s2 = pltpu.roll(sin_a, half, 1)
    both = lambda u: u + pltpu.roll(u, HEAD_DIM, 1)
    return both(c), both(-sin_a), both(s2)


def _rope(t, c, s1, s2):
    return t * c + pltpu.roll(t, BLOCK - 8, 1) * s1 + pltpu.roll(t, 8, 1) * s2


def _rope_t(dt, c, s1, s2):
    return dt * c + pltpu.roll(dt * s1, 8, 1) + pltpu.roll(dt * s2, BLOCK - 8, 1)


def _in_proj_fwd(h, g, w_in_t, rope, tm, name, exch=None):
    lp = h.shape[0]

    def body(h_ref, g_ref, w_ref, rope_ref, a_ref, qkv_ref, bch_ref):
        a = _rms_fwd(h_ref[...], g_ref[...]).astype(BF)
        a_ref[...] = a
        proj = _dot_nt(a, w_ref[...])
        c, s1, s2 = _rope_coeffs(rope_ref[...])
        for j in range(5):
            t = _rope(proj[:, j * 128:(j + 1) * 128], c, s1, s2)
            qkv_ref[:, j * 128:(j + 1) * 128] = (t * SCALE if j < 4 else t).astype(BF)
        qkv_ref[:, 640:768] = proj[:, 640:768].astype(BF)
        bch_ref[...] = proj[:, 768:]

    row = lambda w: pl.BlockSpec((tm, w), lambda i: (i, 0))
    return _call(
        body, exch,
        name=name,
        grid=(lp // tm,),
        in_specs=[row(D_MODEL), _full((1, D_MODEL)), _full((IN_W, D_MODEL)), row(128)],
        out_specs=[row(D_MODEL), row(768), row(3 * CONV_W)],
        out_shape=[
            jax.ShapeDtypeStruct((lp, D_MODEL), BF),
            jax.ShapeDtypeStruct((lp, 768), BF),
            jax.ShapeDtypeStruct((lp, 3 * CONV_W), F32),
        ],
        compiler_params=_params(),
    )(h, g, w_in_t, rope)


def _fold_masks(i):
    r = lax.broadcasted_iota(jnp.int32, (2 * BLOCK, BLOCK), 0) & (BLOCK - 1)
    c = lax.broadcasted_iota(jnp.int32, (2 * BLOCK, BLOCK), 1)
    tri = c > r
    ok = jnp.where(tri, (i - 1) * BLOCK + c, i * BLOCK + c) >= LEAD_PAD
    return tri, ok


def _kv_operand(x, kvh):
    lane = lax.broadcasted_iota(jnp.int32, x.shape, 1)
    zero = jnp.zeros_like(x)
    if kvh == 0:
        lo = jnp.where(lane < HEAD_DIM, x, zero)
        hi = pltpu.roll(lo, HEAD_DIM, 1)
    else:
        hi = jnp.where(lane >= HEAD_DIM, x, zero)
        lo = pltpu.roll(hi, HEAD_DIM, 1)
    return jnp.concatenate([lo, hi], axis=0)


def _split4(t, tri):
    zero = jnp.zeros_like(t[0])
    return jnp.concatenate(
        [jnp.where(tri, t[0], zero), jnp.where(tri, zero, t[0]), jnp.where(tri, t[1], zero), jnp.where(tri, zero, t[1])], axis=1)


def _sink_cols(sink_ref, kvh):
    first = lax.broadcasted_iota(jnp.int32, (2 * BLOCK, 1), 0) < BLOCK
    return [jnp.where(first, sink_ref[0, 4 * kvh + half], sink_ref[0, 4 * kvh + 2 + half]) for half in range(2)]


def _folded_probs(q2, k4, tri, ok, sks):
    s = _dot_nt(q2, k4)
    es, ss = [], []
    for half in range(2):
        s_h = s[:, 2 * half * BLOCK:2 * (half + 1) * BLOCK]
        sf = jnp.where(ok, jnp.where(tri, s_h[:, :BLOCK], s_h[:, BLOCK:]), NEG)
        m = jnp.maximum(jnp.max(sf, axis=-1, keepdims=True), sks[half])
        es.append(jnp.exp(sf - m))
        ss.append(jnp.exp(sks[half] - m))
    sums = _dot(jnp.concatenate(es, axis=0).astype(BF), jnp.ones((BLOCK, BLOCK), BF))
    out = []
    for half in range(2):
        inv = 1.0 / (sums[2 * half * BLOCK:2 * (half + 1) * BLOCK] + ss[half])
        out.append((es[half] * inv, ss[half] * inv[:, 0:1]))
    return out


def _attn_fwd(qkv, sink, name, exch=None):
    lp = qkv.shape[0]
    nb = lp // BLOCK

    def body(sink_ref, q_ref, kvc_ref, kvp_ref, o_ref):
        i = pl.program_id(0)
        tri, ok = _fold_masks(i)
        kvc, kvp = kvc_ref[...], kvp_ref[...]
        kk = jnp.concatenate([kvp[:, :128], kvc[:, :128]], axis=0)
        vv = jnp.concatenate([kvp[:, 128:], kvc[:, 128:]], axis=0)
        for kvh in range(2):
            q2 = jnp.concatenate([q_ref[:, 256 * kvh:256 * kvh + 128], q_ref[:, 256 * kvh + 128:256 * kvh + 256]], axis=0)
            (p_e, _), (p_o, _) = _folded_probs(q2, _kv_operand(kk, kvh), tri, ok, _sink_cols(sink_ref, kvh))
            out = _dot(_split4([p_e.astype(BF), p_o.astype(BF)], tri), _kv_operand(vv, kvh))
            o_ref[:, 256 * kvh:256 * kvh + 128] = out[:BLOCK]
            o_ref[:, 256 * kvh + 128:256 * kvh + 256] = out[BLOCK:]

    return _call(
        body, exch,
        name=name,
        grid=(nb,),
        in_specs=[
            pl.BlockSpec(memory_space=pltpu.SMEM),
            pl.BlockSpec((BLOCK, ATTN_W), lambda i: (i, 0)),
            pl.BlockSpec((BLOCK, 256), lambda i: (i, 2)),
            pl.BlockSpec((BLOCK, 256), lambda i: (jnp.maximum(i - 1, 0), 2)),
        ],
        out_specs=[pl.BlockSpec((BLOCK, ATTN_W), lambda i: (i, 0))],
        out_shape=[jax.ShapeDtypeStruct((lp, ATTN_W), F32)],
        compiler_params=_params(),
    )(sink, qkv, qkv, qkv)


def _mix_out_fwd(bch, y_attn, h, conv_w, g_a, g_c, w_out, g_post, tm, name, exch=None):
    lp = h.shape[0]

    def body(bch_ref, ya_ref, h_ref, cw_ref, ga_ref, gc_ref, w_ref, gp_ref, yc_ref, y_ref, z_ref, h2_ref, ext):
        i = pl.program_id(0)

        @pl.when(i == 0)
        def _():
            ext[0:8, :] = jnp.zeros((8, CONV_W), F32)

        b = bch_ref[:, 0:CONV_W]
        u = bch_ref[:, CONV_W:2 * CONV_W] * bch_ref[:, 2 * CONV_W:3 * CONV_W]
        ext[8:8 + tm, :] = u
        u1 = ext[7:7 + tm, :]
        u2 = ext[6:6 + tm, :]
        yc = cw_ref[0:1, :] * u2 + cw_ref[1:2, :] * u1 + cw_ref[2:3, :] * u
        ext[0:8, :] = u[tm - 8:tm, :]
        yc_ref[...] = yc
        ya = _rms_fwd(ya_ref[...], ga_ref[...]).astype(BF)
        yb = _rms_fwd(b * yc, gc_ref[...]).astype(BF)
        y_ref[:, 0:ATTN_W] = ya
        y_ref[:, ATTN_W:] = yb
        z = _dot(ya, w_ref[0:ATTN_W, :]) + _dot(yb, w_ref[ATTN_W:, :])
        z_ref[...] = z
        h2_ref[...] = h_ref[...] + _rms_fwd(z, gp_ref[...])

    row = lambda w: pl.BlockSpec((tm, w), lambda i: (i, 0))
    return _call(
        body, exch,
        name=name,
        grid=(lp // tm,),
        in_specs=[
            row(3 * CONV_W), row(ATTN_W), row(D_MODEL), _full((8, CONV_W)), _full((1, ATTN_W)), _full((1, CONV_W)),
            _full((D_MODEL, D_MODEL)), _full((1, D_MODEL)),
        ],
        out_specs=[row(CONV_W), row(D_MODEL), row(D_MODEL), row(D_MODEL)],
        out_shape=[
            jax.ShapeDtypeStruct((lp, CONV_W), F32),
            jax.ShapeDtypeStruct((lp, D_MODEL), BF),
            jax.ShapeDtypeStruct((lp, D_MODEL), F32),
            jax.ShapeDtypeStruct((lp, D_MODEL), F32),
        ],
        scratch_shapes=[pltpu.VMEM((tm + 8, CONV_W), F32)],
        compiler_params=_params(),
    )(bch, y_attn, h, conv_w, g_a, g_c, w_out, g_post)


def _mlp_fwd(h2, g_pre, w_up_t, w_down, g_post, tm, name, exch=None, target=None):
    lp = h2.shape[0]
    sub = math.gcd(tm, BLOCK)
    n_sub, lead = tm // sub, BLOCK // sub
    n_t = n_sub if target is not None else 0

    def body(*refs):
        h_ref, gp_ref, wu_ref, wd_ref, gq_ref = refs[:5]
        t_refs = refs[5:5 + n_t]
        a_ref, up_ref, f_ref, last_ref = refs[5 + n_t:9 + n_t]
        h = h_ref[...]
        a = _rms_fwd(h, gp_ref[...]).astype(BF)
        a_ref[...] = a
        up = _dot_nt(a, wu_ref[...])
        up_ref[...] = up.astype(BF)
        act = jnp.square(jnp.maximum(up, 0.0)).astype(BF)
        f = _dot(act, wd_ref[...])
        f_ref[...] = f
        h3 = h + _rms_fwd(f, gq_ref[...])
        if target is None:
            last_ref[...] = h3
            return
        ls_ref = refs[9 + n_t]
        i = pl.program_id(0)

        @pl.when(i == 0)
        def _():
            ls_ref[...] = jnp.zeros((8, 128), F32)

        sq = jnp.zeros((1, 1), F32)
        for j in range(n_sub):
            on_tokens = i * n_sub + j >= lead
            d = jnp.where(on_tokens, h3[j * sub:(j + 1) * sub] - t_refs[j][...], 0.0)
            last_ref[j * sub:(j + 1) * sub, :] = d * (1.0 / D_MODEL)
            sq = sq + jnp.sum(d * d)
        ls_ref[...] += sq

    row = lambda w: pl.BlockSpec((tm, w), lambda i: (i, 0))
    piece = lambda j: pl.BlockSpec((sub, D_MODEL), lambda i: (jnp.maximum(i * n_sub + j - lead, 0), 0))
    out_specs = [row(D_MODEL), row(D_FF), row(D_MODEL), row(D_MODEL)]
    out_shape = [
        jax.ShapeDtypeStruct((lp, D_MODEL), BF),
        jax.ShapeDtypeStruct((lp, D_FF), BF),
        jax.ShapeDtypeStruct((lp, D_MODEL), F32),
        jax.ShapeDtypeStruct((lp, D_MODEL), F32),
    ]
    if target is not None:
        out_specs.append(_full_out((8, 128)))
        out_shape.append(jax.ShapeDtypeStruct((8, 128), F32))
    return _call(
        body, exch,
        name=name,
        grid=(lp // tm,),
        in_specs=[row(D_MODEL), _full((1, D_MODEL)), _full((D_FF, D_MODEL)), _full((D_FF, D_MODEL)), _full((1, D_MODEL))]
        + [piece(j) for j in range(n_t)],
        out_specs=out_specs,
        out_shape=out_shape,
        compiler_params=_params(),
    )(h2, g_pre, w_up_t, w_down, g_post, *([target] * n_t))


def _mlp_bwd_dx(dh3, f, up, h2, w_down, w_up_t, g_post, g_pre, tm, name, exch=None):
    lp = h2.shape[0]

    def body(dh3_ref, f_ref, up_ref, h2_ref, wd_ref, wu_ref, gq_ref, gp_ref, df_ref, dup_ref, dh2_ref, dg_ref):
        i = pl.program_id(0)

        @pl.when(i == 0)
        def _():
            dg_ref[...] = jnp.zeros((8, D_MODEL), F32)

        dh3 = dh3_ref[...]
        df, dgq = _rms_bwd(f_ref[...], gq_ref[...], dh3)
        dg_ref[ROW_MLP_POST:ROW_MLP_POST + 1, :] += dgq
        df = df.astype(BF)
        df_ref[...] = df
        dact = _dot_nt(df, wd_ref[...])
        dup = (dact * (2.0 * jnp.maximum(up_ref[...].astype(F32), 0.0))).astype(BF)
        dup_ref[...] = dup
        da = _dot(dup, wu_ref[...])
        dh, dgp = _rms_bwd(h2_ref[...], gp_ref[...], da)
        dg_ref[ROW_MLP_PRE:ROW_MLP_PRE + 1, :] += dgp
        dh2_ref[...] = dh3 + dh

    row = lambda w: pl.BlockSpec((tm, w), lambda i: (i, 0))
    return _call(
        body, exch,
        name=name,
        grid=(lp // tm,),
        in_specs=[
            row(D_MODEL), row(D_MODEL), row(D_FF), row(D_MODEL), _full((D_FF, D_MODEL)), _full((D_FF, D_MODEL)),
            _full((1, D_MODEL)), _full((1, D_MODEL)),
        ],
        out_specs=[row(D_MODEL), row(D_FF), row(D_MODEL), _full_out((8, D_MODEL))],
        out_shape=[
            jax.ShapeDtypeStruct((lp, D_MODEL), BF),
            jax.ShapeDtypeStruct((lp, D_FF), BF),
            jax.ShapeDtypeStruct((lp, D_MODEL), F32),
            jax.ShapeDtypeStruct((8, D_MODEL), F32),
        ],
        compiler_params=_params(),
    )(dh3, f, up, h2, w_down, w_up_t, g_post, g_pre)


def _mlp_bwd_dw(up, df, dup, a2, tm, name):
    lp = up.shape[0]
    nt = lp // tm
    nj = D_FF // D_MODEL

    def body(up_ref, df_ref, dup_ref, a_ref, dwd_ref, dwu_ref, accd, accu):
        i = pl.program_id(1)

        @pl.when(i == 0)
        def _():
            accd[...] = jnp.zeros_like(accd)
            accu[...] = jnp.zeros_like(accu)

        act = jnp.square(jnp.maximum(up_ref[...].astype(F32), 0.0)).astype(BF)
        accd[...] += _dot_tn(act, df_ref[...])
        accu[...] += _dot_tn(dup_ref[...], a_ref[...])

        @pl.when(i == nt - 1)
        def _():
            dwd_ref[...] = accd[...].astype(BF)
            dwu_ref[...] = accu[...].astype(BF)

    return pl.pallas_call(
        body,
        name=name,
        grid=(nj, nt),
        in_specs=[
            pl.BlockSpec((tm, D_MODEL), lambda j, i: (i, j)),
            pl.BlockSpec((tm, D_MODEL), lambda j, i: (i, 0)),
            pl.BlockSpec((tm, D_MODEL), lambda j, i: (i, j)),
            pl.BlockSpec((tm, D_MODEL), lambda j, i: (i, 0)),
        ],
        out_specs=[pl.BlockSpec((D_MODEL, D_MODEL), lambda j, i: (j, 0)), pl.BlockSpec((D_MODEL, D_MODEL), lambda j, i: (j, 0))],
        out_shape=[jax.ShapeDtypeStruct((D_FF, D_MODEL), BF), jax.ShapeDtypeStruct((D_FF, D_MODEL), BF)],
        scratch_shapes=[pltpu.VMEM((D_MODEL, D_MODEL), F32), pltpu.VMEM((D_MODEL, D_MODEL), F32)],
        compiler_params=_params(("arbitrary", "arbitrary")),
    )(up, df, dup, a2)


def _mix_out_bwd(dh2, z, y_attn, yc, bch, w_out, g_post, g_a, g_c, conv_w, tm, name, exch=None):
    lp = dh2.shape[0]
    nt = lp // tm

    def body(dh2_ref, z_ref, ya_ref, yc_ref, bch_ref, w_ref, gp_ref, ga_ref, gc_ref, cw_ref,
             dz_ref, dya_ref, dbch_ref, dg_ref, ext):
        i = pl.program_id(0)
        dcw_ref = dg_ref.at[ROW_CONV:ROW_CONV + 3, 0:CONV_W]

        @pl.when(i == 0)
        def _():
            ext[tm:tm + 8, :] = jnp.zeros((8, CONV_W), F32)
            dg_ref[...] = jnp.zeros((8, D_MODEL), F32)

        dz, dgp = _rms_bwd(z_ref[...], gp_ref[...], dh2_ref[...])
        dg_ref[ROW_MIX_POST:ROW_MIX_POST + 1, :] += dgp
        dz = dz.astype(BF)
        dz_ref[...] = dz
        dya_n = _dot_nt(dz, w_ref[0:ATTN_W, :])
        dyb_n = _dot_nt(dz, w_ref[ATTN_W:, :])
        dya, dga = _rms_bwd(ya_ref[...], ga_ref[...], dya_n)
        dg_ref[ROW_GROUP_G:ROW_GROUP_G + 1, 0:ATTN_W] += dga
        dya_ref[...] = dya
        b = bch_ref[:, 0:CONV_W]
        c = bch_ref[:, CONV_W:2 * CONV_W]
        hc = bch_ref[:, 2 * CONV_W:3 * CONV_W]
        yc_v = yc_ref[...]
        dyconv, dgc = _rms_bwd(b * yc_v, gc_ref[...], dyb_n)
        dg_ref[ROW_GROUP_G:ROW_GROUP_G + 1, ATTN_W:] += dgc
        dbch_ref[:, 0:CONV_W] = (dyconv * yc_v).astype(BF)
        dyc = dyconv * b
        ext[0:tm, :] = dyc
        d1 = ext[1:1 + tm, :]
        d2 = ext[2:2 + tm, :]
        du = cw_ref[2:3, :] * dyc + cw_ref[1:2, :] * d1 + cw_ref[0:1, :] * d2
        ext[tm:tm + 8, :] = dyc[0:8, :]
        dbch_ref[:, CONV_W:2 * CONV_W] = (du * hc).astype(BF)
        dbch_ref[:, 2 * CONV_W:3 * CONV_W] = (du * c).astype(BF)
        u = c * hc
        dcw_ref[0:1, :] += jnp.sum(u * d2, axis=0, keepdims=True)
        dcw_ref[1:2, :] += jnp.sum(u * d1, axis=0, keepdims=True)
        dcw_ref[2:3, :] += jnp.sum(u * dyc, axis=0, keepdims=True)

    row = lambda w: pl.BlockSpec((tm, w), lambda i: (nt - 1 - i, 0))
    return _call(
        body, exch,
        name=name,
        grid=(nt,),
        in_specs=[
            row(D_MODEL), row(D_MODEL), row(ATTN_W), row(CONV_W), row(3 * CONV_W), _full((D_MODEL, D_MODEL)),
            _full((1, D_MODEL)), _full((1, ATTN_W)), _full((1, CONV_W)), _full((8, CONV_W)),
        ],
        out_specs=[row(D_MODEL), row(ATTN_W), row(3 * CONV_W), _full_out((8, D_MODEL))],
        out_shape=[
            jax.ShapeDtypeStruct((lp, D_MODEL), BF),
            jax.ShapeDtypeStruct((lp, ATTN_W), F32),
            jax.ShapeDtypeStruct((lp, 3 * CONV_W), BF),
            jax.ShapeDtypeStruct((8, D_MODEL), F32),
        ],
        scratch_shapes=[pltpu.VMEM((tm + 8, CONV_W), F32)],
        compiler_params=_params(),
    )(dh2, z, y_attn, yc, bch, w_out, g_post, g_a, g_c, conv_w)


def _attn_bwd(qkv, o, do, sink, rope, name, exch=None):
    lp = qkv.shape[0]
    nb = lp // BLOCK

    def body(sink_ref, q_ref, kvc_ref, kvp_ref, o_ref, do_ref, rq_ref, rk_ref, dq_ref, dkv_ref, dsink_ref, carry):
        i = pl.program_id(0)

        @pl.when(i == 0)
        def _():
            carry[...] = jnp.zeros_like(carry)
            dsink_ref[...] = jnp.zeros((8, 128), F32)

        def finish(tot):
            dk = _rope_t(tot[:, :128], *_rope_coeffs(rk_ref[...]))
            dkv_ref[:, 0:128] = dk.astype(BF)
            dkv_ref[:, 128:256] = tot[:, 128:].astype(BF)

        @pl.when(i < nb)
        def _():
            tri, ok = _fold_masks(i)
            kvc, kvp = kvc_ref[...], kvp_ref[...]
            kk = jnp.concatenate([kvp[:, :128], kvc[:, :128]], axis=0)
            vv = jnp.concatenate([kvp[:, 128:], kvc[:, 128:]], axis=0)
            lane = lax.broadcasted_iota(jnp.int32, (BLOCK, 128), 1)
            lane2 = lax.broadcasted_iota(jnp.int32, (2 * BLOCK, 128), 1)
            first = lax.broadcasted_iota(jnp.int32, (2 * BLOCK, 1), 0) < BLOCK
            row_s = lax.broadcasted_iota(jnp.int32, (8, 128), 0)
            lane_s = jnp.where(row_s == ROW_SINK, lax.broadcasted_iota(jnp.int32, (8, 128), 1), -1)
            rope_q = _rope_coeffs(rq_ref[...])
            dsink = jnp.zeros((8, 128), F32)
            folded = []
            for kvh in range(2):
                c0 = 256 * kvh
                q2 = jnp.concatenate([q_ref[:, c0:c0 + 128], q_ref[:, c0 + 128:c0 + 256]], axis=0)
                do2 = jnp.concatenate([do_ref[:, c0:c0 + 128], do_ref[:, c0 + 128:c0 + 256]], axis=0)
                o2 = jnp.concatenate([o_ref[:, c0:c0 + 128], o_ref[:, c0 + 128:c0 + 256]], axis=0)
                k4, v4 = _kv_operand(kk, kvh), _kv_operand(vv, kvh)
                probs = _folded_probs(q2, k4, tri, ok, _sink_cols(sink_ref, kvh))
                prod = do2 * o2
                dob = do2.astype(BF)
                dp = _dot_nt(dob, v4)
                ds, pb = [], []
                for half in range(2):
                    p, ps = probs[half]
                    sel = (lane2 < HEAD_DIM) if half == 0 else (lane2 >= HEAD_DIM)
                    delta = jnp.sum(jnp.where(sel, prod, 0.0), axis=-1, keepdims=True)
                    dp_h = dp[:, 2 * half * BLOCK:2 * (half + 1) * BLOCK]
                    ds.append((p * (jnp.where(tri, dp_h[:, :BLOCK], dp_h[:, BLOCK:]) - delta)).astype(BF))
                    pb.append(p.astype(BF))
                    t = ps * delta
                    for jj in range(2):
                        part = -jnp.sum(jnp.where(first if jj == 0 else ~first, t, 0.0))
                        dsink = dsink + jnp.where(lane_s == 4 * kvh + 2 * jj + half, part, 0.0)
                ds4, p4 = _split4(ds, tri), _split4(pb, tri)
                dq2 = _dot(ds4, k4) * SCALE
                dq_ref[:, c0:c0 + 128] = _rope_t(dq2[:BLOCK], *rope_q).astype(BF)
                dq_ref[:, c0 + 128:c0 + 256] = _rope_t(dq2[BLOCK:], *rope_q).astype(BF)
                rk, rv = _dot_tn(ds4, q2), _dot_tn(p4, dob)
                own = (lane < HEAD_DIM) if kvh == 0 else (lane >= HEAD_DIM)
                group = []
                for r in (rk, rv):
                    for blk in range(2):
                        t = jnp.where(lane < HEAD_DIM, r[blk * BLOCK:(blk + 1) * BLOCK], r[(2 + blk) * BLOCK:(3 + blk) * BLOCK])
                        group.append(jnp.where(own, t + pltpu.roll(t, HEAD_DIM, 1), 0.0))
                folded.append(group)
            dsink_ref[...] += dsink
            dk_p, dk_c, dv_p, dv_c = [folded[0][t] + folded[1][t] for t in range(4)]
            finish(carry[...] + jnp.concatenate([dk_p, dv_p], axis=1))
            carry[...] = jnp.concatenate([dk_c, dv_c], axis=1)

        @pl.when(i == nb)
        def _():
            finish(carry[...])

    qi = lambda i: jnp.minimum(i, nb - 1)
    ki = lambda i: jnp.maximum(i - 1, 0)
    tab_q = pl.BlockSpec((BLOCK, 128), lambda i: (qi(i), 0))
    tab_k = pl.BlockSpec((BLOCK, 128), lambda i: (ki(i), 0))
    return _call(
        body, exch,
        name=name,
        grid=(nb + 1,),
        in_specs=[
            pl.BlockSpec(memory_space=pltpu.SMEM),
            pl.BlockSpec((BLOCK, ATTN_W), lambda i: (qi(i), 0)),
            pl.BlockSpec((BLOCK, 256), lambda i: (qi(i), 2)),
            pl.BlockSpec((BLOCK, 256), lambda i: (jnp.maximum(qi(i) - 1, 0), 2)),
            pl.BlockSpec((BLOCK, ATTN_W), lambda i: (qi(i), 0)),
            pl.BlockSpec((BLOCK, ATTN_W), lambda i: (qi(i), 0)),
            tab_q, tab_k,
        ],
        out_specs=[
            pl.BlockSpec((BLOCK, ATTN_W), lambda i: (qi(i), 0)),
            pl.BlockSpec((BLOCK, 256), lambda i: (ki(i), 0)),
            pl.BlockSpec((8, 128), lambda i: (0, 0)),
        ],
        out_shape=[
            jax.ShapeDtypeStruct((lp, ATTN_W), BF),
            jax.ShapeDtypeStruct((lp, 256), BF),
            jax.ShapeDtypeStruct((8, 128), F32),
        ],
        scratch_shapes=[pltpu.VMEM((BLOCK, 256), F32)],
        compiler_params=_params(),
    )(sink, qkv, qkv, qkv, o, do, rope, rope)


def _in_proj_bwd_dx(dq, dkv, dbch, w_in_t, h, dh2, g, tm, name, exch=None):
    lp = h.shape[0]

    def body(dq_ref, dkv_ref, dbch_ref, w_ref, h_ref, dh2_ref, g_ref, dh_ref, dg_ref):
        i = pl.program_id(0)

        @pl.when(i == 0)
        def _():
            dg_ref[...] = jnp.zeros((8, D_MODEL), F32)

        da = _dot(dq_ref[...], w_ref[0:512, :]) + _dot(dkv_ref[...], w_ref[512:768, :]) + _dot(dbch_ref[...], w_ref[768:, :])
        dh, dg = _rms_bwd(h_ref[...], g_ref[...], da)
        dg_ref[ROW_MIX_PRE:ROW_MIX_PRE + 1, :] += dg
        dh_ref[...] = dh2_ref[...] + dh

    row = lambda w: pl.BlockSpec((tm, w), lambda i: (i, 0))
    return _call(
        body, exch,
        name=name,
        grid=(lp // tm,),
        in_specs=[row(ATTN_W), row(256), row(3 * CONV_W), _full((IN_W, D_MODEL)), row(D_MODEL), row(D_MODEL), _full((1, D_MODEL))],
        out_specs=[row(D_MODEL), _full_out((8, D_MODEL))],
        out_shape=[jax.ShapeDtypeStruct((lp, D_MODEL), F32), jax.ShapeDtypeStruct((8, D_MODEL), F32)],
        compiler_params=_params(),
    )(dq, dkv, dbch, w_in_t, h, dh2, g)


def _mix_bwd_dw(dq, dkv, dbch, a, y, dz, tm, name, exch=None):
    lp = a.shape[0]
    nt = lp // tm

    def body(dq_ref, dkv_ref, dbch_ref, a_ref, y_ref, dz_ref, dwi_ref, dwo_ref, acci, acco):
        i = pl.program_id(0)

        @pl.when(i == 0)
        def _():
            acci[...] = jnp.zeros_like(acci)
            acco[...] = jnp.zeros_like(acco)

        a_v = a_ref[...]
        acci[0:512, :] += _dot_tn(dq_ref[...], a_v)
        acci[512:768, :] += _dot_tn(dkv_ref[...], a_v)
        acci[768:, :] += _dot_tn(dbch_ref[...], a_v)
        acco[...] += _dot_tn(y_ref[...], dz_ref[...])

        @pl.when(i == nt - 1)
        def _():
            dwi_ref[...] = acci[...].astype(BF)
            dwo_ref[...] = acco[...].astype(BF)

    row = lambda w: pl.BlockSpec((tm, w), lambda i: (i, 0))
    return _call(
        body, exch,
        name=name,
        grid=(nt,),
        in_specs=[row(ATTN_W), row(256), row(3 * CONV_W), row(D_MODEL), row(D_MODEL), row(D_MODEL)],
        out_specs=[_full_out((IN_W, D_MODEL)), _full_out((D_MODEL, D_MODEL))],
        out_shape=[jax.ShapeDtypeStruct((IN_W, D_MODEL), BF), jax.ShapeDtypeStruct((D_MODEL, D_MODEL), BF)],
        scratch_shapes=[pltpu.VMEM((IN_W, D_MODEL), F32), pltpu.VMEM((D_MODEL, D_MODEL), F32)],
        compiler_params=_params(),
    )(dq, dkv, dbch, a, y, dz)


def _mesh_place():
    x, y, c = lax.axis_index("x"), lax.axis_index("y"), lax.axis_index("c")
    return x, y, c, 4 * x + 2 * y + c


def _peer(x, y, c, k):
    px = 1 - x if k & 4 else x
    py = 1 - y if k & 2 else y
    pc = 1 - c if k & 1 else c
    return (px, py, pc), 4 * px + 2 * py + pc


SIBLING = 1
SAME_CORE = (2, 4, 6)
OTHER_CORE = (3, 5, 7)


class _Exchange:
    def __init__(self, pieces):
        self.srcs = [s for s, _ in pieces]
        self.to_all = [g for _, g in pieces]
        self.n = len(pieces)
        self.land_shapes = [
            jax.ShapeDtypeStruct((N_DEV,) + (s.shape if g else s.shape[1:]), s.dtype) for s, g in pieces]
        self.sem_shapes = [pltpu.SemaphoreType.DMA((self.n, N_DEV - 1)), pltpu.SemaphoreType.DMA((self.n, N_DEV - 1)),
                           pltpu.SemaphoreType.DMA((self.n,))]
        self.forwards = any(self.to_all)

    def _ops(self, srcs, lands, sems):
        send_sems, recv_sems, local_sems = sems
        x, y, c, me = _mesh_place()

        def remote(p, k, src, slot, to):
            return pltpu.make_async_remote_copy(
                src_ref=src, dst_ref=lands[p].at[slot], send_sem=send_sems.at[p, k - 1], recv_sem=recv_sems.at[p, k - 1],
                device_id=to, device_id_type=MESH)

        def own(p):
            return pltpu.make_async_copy(srcs[p] if self.to_all[p] else srcs[p].at[me], lands[p].at[me], local_sems.at[p])

        def direct(p, k):
            peer, pidx = _peer(x, y, c, k)
            return remote(p, k, srcs[p] if self.to_all[p] else srcs[p].at[pidx], me, peer)

        def forward(p, k):
            sibling, _ = _peer(x, y, c, SIBLING)
            _, origin = _peer(x, y, c, k ^ SIBLING)
            return remote(p, k, lands[p].at[origin], origin, sibling)

        def arrival(p, k):
            peer, pidx = _peer(x, y, c, k)
            return remote(p, k, lands[p].at[pidx], pidx, peer)

        return own, direct, forward, arrival

    def start(self, srcs, lands, sems):
        own, direct, _, _ = self._ops(srcs, lands, sems)
        for p in range(self.n):
            own(p).start()
            for k in ((SIBLING,) + SAME_CORE) if self.to_all[p] else range(1, N_DEV):
                direct(p, k).start()

    def forward(self, srcs, lands, sems):
        _, _, forward, arrival = self._ops(srcs, lands, sems)
        for p in range(self.n):
            if self.to_all[p]:
                for k in SAME_CORE:
                    arrival(p, k).wait_recv()
                    forward(p, k ^ SIBLING).start()

    def finish(self, srcs, lands, sems):
        own, direct, forward, arrival = self._ops(srcs, lands, sems)
        for p in range(self.n):
            for k in ((SIBLING,) + OTHER_CORE) if self.to_all[p] else range(1, N_DEV):
                arrival(p, k).wait_recv()
        for p in range(self.n):
            for k in range(1, N_DEV):
                (forward(p, k) if self.to_all[p] and k in OTHER_CORE else direct(p, k)).wait_send()
            own(p).wait()


def _exchange_call(exch, name):
    def body(*refs):
        srcs, lands, sems = refs[:exch.n], refs[exch.n:2 * exch.n], refs[2 * exch.n:]
        exch.start(srcs, lands, sems)
        exch.forward(srcs, lands, sems)
        exch.finish(srcs, lands, sems)

    hbm = pl.BlockSpec(memory_space=pl.ANY)
    return pl.pallas_call(
        body,
        name=name,
        in_specs=[hbm] * exch.n,
        out_specs=[hbm] * exch.n,
        out_shape=exch.land_shapes,
        scratch_shapes=exch.sem_shapes,
    )(*exch.srcs)


def _call(body, exch, *, name, grid, in_specs, out_specs, out_shape, scratch_shapes=(), compiler_params):
    if exch is None:
        return pl.pallas_call(body, name=name, grid=grid, in_specs=in_specs, out_specs=out_specs, out_shape=out_shape,
                              scratch_shapes=scratch_shapes, compiler_params=compiler_params)
    n_in, n_out, n_scr, n_x = len(in_specs), len(out_shape), len(scratch_shapes), exch.n
    steps = math.prod(grid)

    def carrying(*refs):
        a, b, c, d, e = n_in, n_in + n_x, n_in + n_x + n_out, n_in + 2 * n_x + n_out, n_in + 2 * n_x + n_out + n_scr
        ins, srcs, outs, lands, scr, sems = refs[:a], refs[a:b], refs[b:c], refs[c:d], refs[d:e], refs[e:]
        step = functools.reduce(lambda acc, t: acc * grid[t] + pl.program_id(t), range(len(grid)), 0)

        @pl.when(step == 0)
        def _():
            exch.start(srcs, lands, sems)

        body(*ins, *outs, *scr)

        if exch.forwards:
            @pl.when(step == max(0, steps - 1 - (steps + 7) // 8))
            def _():
                exch.forward(srcs, lands, sems)

        @pl.when(step == steps - 1)
        def _():
            exch.finish(srcs, lands, sems)

    hbm = pl.BlockSpec(memory_space=pl.ANY)
    call = pl.pallas_call(
        carrying, name=name, grid=grid, in_specs=list(in_specs) + [hbm] * n_x, out_specs=list(out_specs) + [hbm] * n_x,
        out_shape=list(out_shape) + exch.land_shapes, scratch_shapes=list(scratch_shapes) + exch.sem_shapes,
        compiler_params=compiler_params)

    def run(*args):
        res = call(*args, *exch.srcs)
        return list(res[:n_out]), list(res[n_out:])

    return run


def _sum_small(part):
    def body(part_ref, out_ref, land, send_sems, recv_sems):
        x, y, c, me = _mesh_place()
        land[me] = part_ref[...]
        sent = []
        for k in range(1, N_DEV):
            peer, _ = _peer(x, y, c, k)
            cp = pltpu.make_async_remote_copy(
                src_ref=part_ref, dst_ref=land.at[me], send_sem=send_sems.at[k - 1], recv_sem=recv_sems.at[k - 1],
                device_id=peer, device_id_type=MESH)
            cp.start()
            sent.append(cp)
        for k in range(1, N_DEV):
            peer, pidx = _peer(x, y, c, k)
            pltpu.make_async_remote_copy(
                src_ref=part_ref, dst_ref=land.at[pidx], send_sem=send_sems.at[k - 1], recv_sem=recv_sems.at[k - 1],
                device_id=peer, device_id_type=MESH).wait_recv()
        for cp in sent:
            cp.wait_send()
        acc = land[0]
        for d in range(1, N_DEV):
            acc = acc + land[d]
        out_ref[...] = acc

    vmem = pl.BlockSpec(memory_space=pltpu.VMEM)
    return pl.pallas_call(
        body,
        name="sum_small",
        in_specs=[vmem],
        out_specs=vmem,
        out_shape=jax.ShapeDtypeStruct(part.shape, F32),
        scratch_shapes=[pltpu.VMEM((N_DEV,) + part.shape, F32), pltpu.SemaphoreType.DMA((N_DEV - 1,)),
                        pltpu.SemaphoreType.DMA((N_DEV - 1,))],
    )(part)


def _adamw(w, g, m, v):
    m = ADAM_B1 * m + (1.0 - ADAM_B1) * g
    v = ADAM_B2 * v + (1.0 - ADAM_B2) * jnp.square(g)
    m_hat = m / (1.0 - ADAM_B1 ** ADAM_STEP)
    v_hat = v / (1.0 - ADAM_B2 ** ADAM_STEP)
    delta = -ADAM_LR * (m_hat / (jnp.sqrt(v_hat) + ADAM_EPS) + ADAM_WD * w)
    return delta, m, v


def _landed_specs(tr, wd):
    return [pl.BlockSpec((N_DEV, tr, wd), lambda l, i, ll=ll: (0, jnp.where(l == ll, i, 0), 0)) for ll in range(DEPTH)]


def _device_sum(r_ref):
    acc = r_ref[0].astype(F32)
    for d in range(1, N_DEV):
        acc = acc + r_ref[d].astype(F32)
    return acc


def _sum_parts(recv, tr, name):
    _, r, wd = recv[0].shape

    def body(*refs):
        g_ref = refs[DEPTH]
        for ll in range(DEPTH):
            @pl.when(pl.program_id(0) == ll)
            def _(ll=ll):
                g_ref[0] = _device_sum(refs[ll])

    return pl.pallas_call(
        body,
        name=name,
        grid=(DEPTH, r // tr),
        in_specs=_landed_specs(tr, wd),
        out_specs=pl.BlockSpec((1, tr, wd), lambda l, i: (l, i, 0)),
        out_shape=jax.ShapeDtypeStruct((DEPTH, r, wd), F32),
        compiler_params=_params(("arbitrary", "arbitrary")),
    )(*recv)


def _sum_adamw(recv, w, m, v, tr, name):
    _, r, wd = recv[0].shape

    def body(*refs):
        w_ref, m_ref, v_ref, g_ref, d_ref, mo_ref, vo_ref = refs[DEPTH:]
        for ll in range(DEPTH):
            @pl.when(pl.program_id(0) == ll)
            def _(ll=ll):
                g = _device_sum(refs[ll])
                g_ref[0] = g
                d_ref[0], mo_ref[0], vo_ref[0] = _adamw(w_ref[0], g, m_ref[0], v_ref[0])

    blk = pl.BlockSpec((1, tr, wd), lambda l, i: (l, i, 0))
    shape = jax.ShapeDtypeStruct((DEPTH, r, wd), F32)
    return pl.pallas_call(
        body,
        name=name,
        grid=(DEPTH, r // tr),
        in_specs=_landed_specs(tr, wd) + [blk, blk, blk],
        out_specs=[blk] * 4,
        out_shape=[shape] * 4,
        compiler_params=_params(("arbitrary", "arbitrary")),
    )(*recv, w, m, v)


def _adamw_rows(w, g, m, v, tr, name):
    _, r, wd = w.shape

    def body(w_ref, g_ref, m_ref, v_ref, d_ref, mo_ref, vo_ref):
        d_ref[0], mo_ref[0], vo_ref[0] = _adamw(w_ref[0], g_ref[0], m_ref[0], v_ref[0])

    blk = pl.BlockSpec((1, tr, wd), lambda l, i: (l, i, 0))
    shape = jax.ShapeDtypeStruct(w.shape, F32)
    return pl.pallas_call(
        body,
        name=name,
        grid=(DEPTH, r // tr),
        in_specs=[blk] * 4,
        out_specs=[blk] * 3,
        out_shape=[shape] * 3,
        compiler_params=_params(("arbitrary", "arbitrary")),
    )(w, g, m, v)


def _adamw_small(ws, gs, ms, vs):
    n = len(ws)

    def body(*refs):
        w_r, g_r, m_r, v_r = refs[:n], refs[n:2 * n], refs[2 * n:3 * n], refs[3 * n:4 * n]
        d_o, m_o, v_o = refs[4 * n:5 * n], refs[5 * n:6 * n], refs[6 * n:7 * n]
        for t in range(n):
            d_o[t][...], m_o[t][...], v_o[t][...] = _adamw(w_r[t][...], g_r[t][...], m_r[t][...], v_r[t][...])

    vmem = pl.BlockSpec(memory_space=pltpu.VMEM)
    shapes = [jax.ShapeDtypeStruct(w.shape, F32) for w in ws]
    outs = pl.pallas_call(
        body,
        name="adamw_small",
        in_specs=[vmem] * (4 * n),
        out_specs=[vmem] * (3 * n),
        out_shape=shapes * 3,
    )(*ws, *gs, *ms, *vs)
    return outs[:n], outs[n:2 * n], outs[2 * n:]


def kernel(x, meta_tokens, mix_pre_g, w_in, conv_w, sinks, attn_out_g, conv_out_g, w_out, mix_post_g, mlp_pre_g, w_up, w_down, mlp_post_g, loss_target, m_meta_tokens, m_mix_pre_g, m_w_in, m_conv_w, m_sinks, m_attn_out_g, m_conv_out_g, m_w_out, m_mix_post_g, m_mlp_pre_g, m_w_up, m_w_down, m_mlp_post_g, v_meta_tokens, v_mix_pre_g, v_w_in, v_conv_w, v_sinks, v_attn_out_g, v_conv_out_g, v_w_out, v_mix_post_g, v_mlp_pre_g, v_w_up, v_w_down, v_mlp_post_g):
    seq = x.shape[1]
    lp = BLOCK + seq
    tm = _row_tile(lp)
    tm_mlp = _row_tile(lp, (320, 256, 128))
    tm_dw_mlp = _row_tile(lp, (1664, 1040, 640, 384, 256, 128))
    tm_dw_mix = _row_tile(lp, (832, 640, 384, 256, 128))
    me = 4 * lax.axis_index("x") + 2 * lax.axis_index("y") + lax.axis_index("c")
    cshard = CONV_W // N_DEV
    mshard = D_MODEL // N_DEV

    gather_with = {
        ("in_proj_fwd", 0): [("out", 0), ("in", 1)], ("attn_fwd", 0): [("up", 0)], ("mix_out_fwd", 0): [("down", 0)],
        ("mlp_fwd", 0): [("out", 1), ("up", 1), ("down", 1)],
    }
    scatter_with = {
        ("attn_bwd", 1): [("down", 1)], ("mlp_bwd_dx", 0): [("up", 1), ("in", 1), ("out", 1)],
        ("attn_bwd", 0): [("down", 0)], ("mix_bwd_dw", 0): [("up", 0)], ("in_proj_bwd_dx", 0): [("in", 0), ("out", 0)],
    }
    shard = {"in": jnp.swapaxes(w_in, 1, 2).astype(BF), "out": w_out.astype(BF),
             "up": jnp.swapaxes(w_up, 1, 2).astype(BF), "down": w_down.astype(BF)}
    weight = {}
    grad = {}
    landed = {}

    def run(fn, kind, l, *args):
        key, name = (kind, l), f"{kind}_{l}"
        if key in gather_with:
            blocks = gather_with[key]
            outs, lands = fn(*args, name, _Exchange([(shard[n][k], True) for n, k in blocks]))
            for b, land in zip(blocks, lands):
                weight[b] = land.reshape(-1, D_MODEL)
            return outs
        if key in scatter_with:
            blocks = scatter_with[key]
            outs, lands = fn(*args, name, _Exchange([(grad[b].reshape(N_DEV, -1, D_MODEL), False) for b in blocks]))
            landed.update(zip(blocks, lands))
            return outs
        return fn(*args, name)

    small = jnp.zeros((24, 128), F32)
    small = small.at[0:N_META, :].set(meta_tokens)
    small = small.at[N_META:N_META + 6, 0:cshard].set(conv_w.reshape(6, cshard))
    first_in, g_small = _exchange_call(_Exchange([(shard["in"][0], True), (small, True)]), "gather_first")
    weight[("in", 0)] = first_in.reshape(-1, D_MODEL)
    meta_full = jnp.swapaxes(g_small[:, 0:N_META, :], 0, 1).reshape(N_META, D_MODEL)
    cw = g_small[:, N_META:N_META + 6, 0:cshard].reshape(N_DEV, DEPTH, 3, cshard)
    cw = jnp.transpose(cw, (1, 2, 0, 3)).reshape(DEPTH, 3, CONV_W)
    conv_full = jnp.concatenate([cw, jnp.zeros((DEPTH, 5, CONV_W), F32)], axis=1)

    rope = _rope_table(lp)
    row1 = lambda a, l: a[l].reshape(1, -1)

    h = jnp.concatenate([jnp.zeros((LEAD_PAD, D_MODEL), F32), meta_full, x[0]], axis=0)
    saved = []
    for l in range(DEPTH):
        a, qkv, bch = run(_in_proj_fwd, "in_proj_fwd", l, h, row1(mix_pre_g, l), weight[("in", l)], rope, tm)
        y_attn, = run(_attn_fwd, "attn_fwd", l, qkv, row1(sinks, l))
        yc, y, z, h2 = run(_mix_out_fwd, "mix_out_fwd", l, bch, y_attn, h, conv_full[l], row1(attn_out_g, l),
                           row1(conv_out_g, l), weight[("out", l)], row1(mix_post_g, l), tm)
        mlp = _mlp_fwd if l < DEPTH - 1 else functools.partial(_mlp_fwd, target=loss_target[0])
        a2, up, f, *rest = run(mlp, "mlp_fwd", l, h2, row1(mlp_pre_g, l), weight[("up", l)], weight[("down", l)],
                               row1(mlp_post_g, l), tm_mlp)
        saved.append((h, a, qkv, bch, y_attn, yc, y, z, h2, a2, up, f))
        h = rest[0]
    dh, loss_part = rest[0], rest[1][0, 0] * (0.5 / D_MODEL)

    gsmall = [None] * DEPTH
    for l in reversed(range(DEPTH)):
        h0, a, qkv, bch, y_attn, yc, y, z, h2, a2, up, f = saved[l]
        df, dup, dh2, dg_mlp = run(_mlp_bwd_dx, "mlp_bwd_dx", l, dh, f, up, h2, weight[("down", l)], weight[("up", l)],
                                   row1(mlp_post_g, l), row1(mlp_pre_g, l), tm_mlp)
        grad[("down", l)], grad[("up", l)] = _mlp_bwd_dw(up, df, dup, a2, tm_dw_mlp, f"mlp_bwd_dw_{l}")
        dz, dya, dbch, dg_mix = run(_mix_out_bwd, "mix_out_bwd", l, dh2, z, y_attn, yc, bch, weight[("out", l)],
                                    row1(mix_post_g, l), row1(attn_out_g, l), row1(conv_out_g, l), conv_full[l], tm)
        dq, dkv, dsink = run(_attn_bwd, "attn_bwd", l, qkv, y_attn, dya, row1(sinks, l), rope)
        grad[("in", l)], grad[("out", l)] = run(_mix_bwd_dw, "mix_bwd_dw", l, dq, dkv, dbch, a, y, dz, tm_dw_mix)
        dh, dg_in = run(_in_proj_bwd_dx, "in_proj_bwd_dx", l, dq, dkv, dbch, weight[("in", l)], h0, dh2,
                        row1(mix_pre_g, l), tm)
        tile_a = dg_mlp + dg_in + jnp.pad(dsink, ((0, 0), (0, D_MODEL - 128)))
        gsmall[l] = (tile_a, dg_mix)
    grad_x = dh[BLOCK:][None]

    loss_tile = jnp.zeros((8, D_MODEL), F32).at[ROW_LOSS, 0].set(loss_part)
    tot = _sum_small(jnp.concatenate(
        [gsmall[0][0] + loss_tile, gsmall[0][1], gsmall[1][0], gsmall[1][1], dh[LEAD_PAD:BLOCK]], axis=0))
    loss = tot[ROW_LOSS, 0]
    ta = [tot[16 * l:16 * l + 8] for l in range(DEPTH)]
    tb = [tot[16 * l + 8:16 * l + 16] for l in range(DEPTH)]
    pick = lambda tiles, r0, r1, c0, c1: jnp.stack([t[r0:r1, c0:c1] for t in tiles])
    g_mlp_post = pick(ta, ROW_MLP_POST, ROW_MLP_POST + 1, 0, D_MODEL).reshape(DEPTH, D_MODEL)
    g_mlp_pre = pick(ta, ROW_MLP_PRE, ROW_MLP_PRE + 1, 0, D_MODEL).reshape(DEPTH, D_MODEL)
    g_mix_pre = pick(ta, ROW_MIX_PRE, ROW_MIX_PRE + 1, 0, D_MODEL).reshape(DEPTH, D_MODEL)
    g_sinks = pick(ta, ROW_SINK, ROW_SINK + 1, 0, N_Q_HEADS).reshape(DEPTH, N_Q_HEADS)
    g_mix_post = pick(tb, ROW_MIX_POST, ROW_MIX_POST + 1, 0, D_MODEL).reshape(DEPTH, D_MODEL)
    g_attn_out = pick(tb, ROW_GROUP_G, ROW_GROUP_G + 1, 0, ATTN_W).reshape(DEPTH, ATTN_W)
    g_conv_out = pick(tb, ROW_GROUP_G, ROW_GROUP_G + 1, ATTN_W, D_MODEL).reshape(DEPTH, CONV_W)
    g_conv_full = pick(tb, ROW_CONV, ROW_CONV + 3, 0, CONV_W)
    g_conv = lax.dynamic_slice_in_dim(g_conv_full, me * cshard, cshard, axis=2)
    g_meta = lax.dynamic_slice_in_dim(tot[16 * DEPTH:16 * DEPTH + N_META], me * mshard, mshard, axis=1)

    r_in, r_out, r_up, r_down = [[landed[(n, l)] for l in range(DEPTH)] for n in ("in", "out", "up", "down")]
    g_w_in = jnp.swapaxes(_sum_parts(r_in, 96, "sum_w_in"), 1, 2)
    g_w_up = jnp.swapaxes(_sum_parts(r_up, 128, "sum_w_up"), 1, 2)
    d_w_in, nm_w_in, nv_w_in = _adamw_rows(w_in, g_w_in, m_w_in, v_w_in, 256, "adamw_w_in")
    d_w_up, nm_w_up, nv_w_up = _adamw_rows(w_up, g_w_up, m_w_up, v_w_up, 256, "adamw_w_up")
    g_w_out, d_w_out, nm_w_out, nv_w_out = _sum_adamw(r_out, w_out, m_w_out, v_w_out, 128, "adamw_w_out")
    g_w_down, d_w_down, nm_w_down, nv_w_down = _sum_adamw(r_down, w_down, m_w_down, v_w_down, 128, "adamw_w_down")

    ws = [meta_tokens, mix_pre_g, conv_w.reshape(6, cshard), sinks, attn_out_g, conv_out_g, mix_post_g, mlp_pre_g, mlp_post_g]
    gs = [g_meta, g_mix_pre, g_conv.reshape(6, cshard), g_sinks, g_attn_out, g_conv_out, g_mix_post, g_mlp_pre, g_mlp_post]
    ms = [m_meta_tokens, m_mix_pre_g, m_conv_w.reshape(6, cshard), m_sinks, m_attn_out_g, m_conv_out_g, m_mix_post_g,
          m_mlp_pre_g, m_mlp_post_g]
    vs = [v_meta_tokens, v_mix_pre_g, v_conv_w.reshape(6, cshard), v_sinks, v_attn_out_g, v_conv_out_g, v_mix_post_g,
          v_mlp_pre_g, v_mlp_post_g]
    ds, nms, nvs = _adamw_small(ws, gs, ms, vs)

    def order(meta, mix_pre, cv, sk, a_out, c_out, mix_post, mlp_pre, mlp_post, win, wout, wup, wdown):
        return [meta, mix_pre, win, cv.reshape(DEPTH, 3, cshard), sk, a_out, c_out, wout, mix_post, mlp_pre, wup, wdown, mlp_post]

    grads = order(*gs, g_w_in, g_w_out, g_w_up, g_w_down)
    deltas = order(*ds, d_w_in, d_w_out, d_w_up, d_w_down)
    new_m = order(*nms, nm_w_in, nm_w_out, nm_w_up, nm_w_down)
    new_v = order(*nvs, nv_w_in, nv_w_out, nv_w_up, nv_w_down)
    return (loss, grad_x, *grads, *deltas, *new_m, *new_v)
```

```python
import functools
import math

import jax
import jax.numpy as jnp
from jax import lax
from jax.experimental import pallas as pl
from jax.experimental.pallas import tpu as pltpu

F32 = jnp.float32
BF = jnp.bfloat16

D_MODEL = 1024
ATTN_W = 512
CONV_W = 512
KV_W = 128
HEAD_DIM = 64
N_Q_HEADS = 8
ROT_DIM = 16
D_FF = 4096
IN_W = 2304
N_META = 16
BLOCK = 128
LEAD_PAD = BLOCK - N_META
ROPE_THETA = 500000.0
EPS = 1e-6
N_DEV = 8
DEPTH = 2
NEG = -1e30
SCALE = HEAD_DIM ** -0.5

ADAM_LR = 0.001
ADAM_B1 = 0.9
ADAM_B2 = 0.999
ADAM_EPS = 1e-08
ADAM_WD = 0.01
ADAM_STEP = 10

ROW_MLP_POST, ROW_MLP_PRE, ROW_MIX_PRE, ROW_SINK, ROW_LOSS = 0, 1, 2, 3, 4
ROW_MIX_POST, ROW_GROUP_G, ROW_CONV = 0, 1, 2

VMEM_LIMIT = 56 * 1024 * 1024
MESH = pl.DeviceIdType.MESH


def _dot(a, b):
    return jnp.dot(a, b, preferred_element_type=F32)


def _dot_nt(a, b):
    return lax.dot_general(a, b, (((1,), (1,)), ((), ())), preferred_element_type=F32)


def _dot_tn(a, b):
    return lax.dot_general(a, b, (((0,), (0,)), ((), ())), preferred_element_type=F32)


def _rms_fwd(x, g):
    r = lax.rsqrt(jnp.mean(x * x, axis=-1, keepdims=True) + EPS)
    return x * r * g


def _rms_bwd(x, g, dy):
    r = lax.rsqrt(jnp.mean(x * x, axis=-1, keepdims=True) + EPS)
    xh = x * r
    t = dy * g
    dx = r * (t - xh * jnp.mean(t * xh, axis=-1, keepdims=True))
    dg = jnp.sum(dy * xh, axis=0, keepdims=True)
    return dx, dg


def _row_tile(lp, cands=(640, 512, 384, 256, 128)):
    for t in cands:
        if lp % t == 0:
            return t
    raise ValueError(f"row count {lp} is not a multiple of 128")


def _full(shape):
    n = len(shape)
    return pl.BlockSpec(shape, lambda *_: (0,) * n, pipeline_mode=pl.Buffered(1))


def _full_out(shape):
    n = len(shape)
    return pl.BlockSpec(shape, lambda *_: (0,) * n)


def _params(sem=("arbitrary",)):
    return pltpu.CompilerParams(dimension_semantics=sem, vmem_limit_bytes=VMEM_LIMIT)


def _rope_table(lp):
    half = ROT_DIM // 2
    pos = jnp.maximum(jnp.arange(lp) - LEAD_PAD, 0).astype(F32)
    inv_freq = jnp.power(jnp.float32(ROPE_THETA), -jnp.arange(0, ROT_DIM, 2, dtype=F32) / ROT_DIM)
    ang_t = jnp.concatenate([inv_freq, inv_freq])[:, None] * pos[None, :]
    row = lax.broadcasted_iota(jnp.int32, (ROT_DIM, lp), 0)
    cs_t = jnp.where(row < half, jnp.cos(ang_t), jnp.sin(ang_t))
    return jnp.pad(cs_t.T, ((0, 0), (0, 128 - ROT_DIM)))


def _rope_coeffs(t):
    half = ROT_DIM // 2
    lane = lax.broadcasted_iota(jnp.int32, t.shape, 1)
    cos_a = jnp.where(lane < half, t, 0.0)
    sin_a = pltpu.roll(jnp.where((lane >= half) & (lane < ROT_DIM), t, 0.0), 128 - half, 1)
    c = cos_a + pltpu.roll(cos_a, half, 1) + jnp.where((lane >= ROT_DIM) & (lane < HEAD_DIM), 1.0, 0.0)
    s2 = pltpu.roll(sin_a, half, 1)
    both = lambda u: u + pltpu.roll(u, HEAD_DIM, 1)
    return both(c), both(-sin_a), both(s2)


def _rope_tables(compact, tm):
    lp = compact.shape[0]

    def body(t_ref, c_ref, s1_ref, s2_ref):
        c_ref[...], s1_ref[...], s2_ref[...] = _rope_coeffs(t_ref[...])

    row = pl.BlockSpec((tm, 128), lambda i: (i, 0))
    return pl.pallas_call(
        body,
        name="rope_tables",
        grid=(lp // tm,),
        in_specs=[row],
        out_specs=[row] * 3,
        out_shape=[jax.ShapeDtypeStruct((lp, 128), F32)] * 3,
        compiler_params=_params(),
    )(compact)


def _rope(t, c, s1, s2):
    return t * c + pltpu.roll(t, BLOCK - 8, 1) * s1 + pltpu.roll(t, 8, 1) * s2


def _rope_t(dt, c, s1, s2):
    return dt * c + pltpu.roll(dt * s1, 8, 1) + pltpu.roll(dt * s2, BLOCK - 8, 1)


def _in_proj_fwd(h, g, w_in_t, rope, tm, name, exch=None):
    lp = h.shape[0]

    def body(h_ref, g_ref, w_ref, c_ref, s1_ref, s2_ref, a_ref, qkv_ref, bch_ref):
        a = _rms_fwd(h_ref[...], g_ref[...]).astype(BF)
        a_ref[...] = a
        proj = _dot_nt(a, w_ref[...])
        c, s1, s2 = c_ref[...], s1_ref[...], s2_ref[...]
        for j in range(5):
            t = _rope(proj[:, j * 128:(j + 1) * 128], c, s1, s2)
            qkv_ref[:, j * 128:(j + 1) * 128] = (t * SCALE if j < 4 else t).astype(BF)
        qkv_ref[:, 640:768] = proj[:, 640:768].astype(BF)
        bch_ref[...] = proj[:, 768:].astype(BF)

    row = lambda w: pl.BlockSpec((tm, w), lambda i: (i, 0))
    return _call(
        body, exch,
        name=name,
        grid=(lp // tm,),
        in_specs=[row(D_MODEL), _full((1, D_MODEL)), _full((IN_W, D_MODEL)), row(128), row(128), row(128)],
        out_specs=[row(D_MODEL), row(768), row(3 * CONV_W)],
        out_shape=[
            jax.ShapeDtypeStruct((lp, D_MODEL), BF),
            jax.ShapeDtypeStruct((lp, 768), BF),
            jax.ShapeDtypeStruct((lp, 3 * CONV_W), BF),
        ],
        compiler_params=_params(),
    )(h, g, w_in_t, *rope)


def _fold_masks(i):
    r = lax.broadcasted_iota(jnp.int32, (2 * BLOCK, BLOCK), 0) & (BLOCK - 1)
    c = lax.broadcasted_iota(jnp.int32, (2 * BLOCK, BLOCK), 1)
    tri = c > r
    ok = jnp.where(tri, (i - 1) * BLOCK + c, i * BLOCK + c) >= LEAD_PAD
    return tri, ok


def _kv_operand(x, kvh):
    lane = lax.broadcasted_iota(jnp.int32, x.shape, 1)
    zero = jnp.zeros_like(x)
    if kvh == 0:
        lo = jnp.where(lane < HEAD_DIM, x, zero)
        hi = pltpu.roll(lo, HEAD_DIM, 1)
    else:
        hi = jnp.where(lane >= HEAD_DIM, x, zero)
        lo = pltpu.roll(hi, HEAD_DIM, 1)
    return jnp.concatenate([lo, hi], axis=0)


def _split4(t, tri):
    zero = jnp.zeros_like(t[0])
    return jnp.concatenate(
        [jnp.where(tri, t[0], zero), jnp.where(tri, zero, t[0]), jnp.where(tri, t[1], zero), jnp.where(tri, zero, t[1])], axis=1)


def _sink_cols(sink_ref, kvh):
    first = lax.broadcasted_iota(jnp.int32, (2 * BLOCK, 1), 0) < BLOCK
    return [jnp.where(first, sink_ref[0, 4 * kvh + half], sink_ref[0, 4 * kvh + 2 + half]) for half in range(2)]


def _folded_probs(q2, k4, tri, ok, sks):
    s = _dot_nt(q2, k4)
    es, ss = [], []
    for half in range(2):
        s_h = s[:, 2 * half * BLOCK:2 * (half + 1) * BLOCK]
        sf = jnp.where(ok, jnp.where(tri, s_h[:, :BLOCK], s_h[:, BLOCK:]), NEG)
        m = jnp.maximum(jnp.max(sf, axis=-1, keepdims=True), sks[half])
        es.append(jnp.exp(sf - m))
        ss.append(jnp.exp(sks[half] - m))
    sums = _dot(jnp.concatenate(es, axis=0).astype(BF), jnp.ones((BLOCK, BLOCK), BF))
    out = []
    for half in range(2):
        inv = 1.0 / (sums[2 * half * BLOCK:2 * (half + 1) * BLOCK] + ss[half])
        out.append((es[half] * inv, ss[half] * inv[:, 0:1]))
    return out


def _attn_fwd(qkv, sink, name, exch=None):
    lp = qkv.shape[0]
    nb = lp // BLOCK

    def body(sink_ref, q_ref, kvc_ref, kvp_ref, o_ref):
        i = pl.program_id(0)
        tri, ok = _fold_masks(i)
        kvc, kvp = kvc_ref[...], kvp_ref[...]
        kk = jnp.concatenate([kvp[:, :128], kvc[:, :128]], axis=0)
        vv = jnp.concatenate([kvp[:, 128:], kvc[:, 128:]], axis=0)
        for kvh in range(2):
            q2 = jnp.concatenate([q_ref[:, 256 * kvh:256 * kvh + 128], q_ref[:, 256 * kvh + 128:256 * kvh + 256]], axis=0)
            (p_e, _), (p_o, _) = _folded_probs(q2, _kv_operand(kk, kvh), tri, ok, _sink_cols(sink_ref, kvh))
            out = _dot(_split4([p_e.astype(BF), p_o.astype(BF)], tri), _kv_operand(vv, kvh))
            o_ref[:, 256 * kvh:256 * kvh + 128] = out[:BLOCK].astype(BF)
            o_ref[:, 256 * kvh + 128:256 * kvh + 256] = out[BLOCK:].astype(BF)

    return _call(
        body, exch,
        name=name,
        grid=(nb,),
        in_specs=[
            pl.BlockSpec(memory_space=pltpu.SMEM),
            pl.BlockSpec((BLOCK, ATTN_W), lambda i: (i, 0)),
            pl.BlockSpec((BLOCK, 256), lambda i: (i, 2)),
            pl.BlockSpec((BLOCK, 256), lambda i: (jnp.maximum(i - 1, 0), 2)),
        ],
        out_specs=[pl.BlockSpec((BLOCK, ATTN_W), lambda i: (i, 0))],
        out_shape=[jax.ShapeDtypeStruct((lp, ATTN_W), BF)],
        compiler_params=_params(),
    )(sink, qkv, qkv, qkv)


def _mix_out_fwd(bch, y_attn, h, conv_w, g_a, g_c, w_out, g_post, tm, name, exch=None):
    lp = h.shape[0]

    def body(bch_ref, ya_ref, h_ref, cw_ref, ga_ref, gc_ref, w_ref, gp_ref, y_ref, z_ref, h2_ref, ext):
        i = pl.program_id(0)

        @pl.when(i == 0)
        def _():
            ext[0:8, :] = jnp.zeros((8, CONV_W), F32)

        b = bch_ref[:, 0:CONV_W].astype(F32)
        u = bch_ref[:, CONV_W:2 * CONV_W].astype(F32) * bch_ref[:, 2 * CONV_W:3 * CONV_W].astype(F32)
        ext[8:8 + tm, :] = u
        yc = cw_ref[0:1, :] * ext[6:6 + tm, :] + cw_ref[1:2, :] * ext[7:7 + tm, :] + cw_ref[2:3, :] * u
        ext[0:8, :] = u[tm - 8:tm, :]
        ya = _rms_fwd(ya_ref[...].astype(F32), ga_ref[...]).astype(BF)
        yb = _rms_fwd(b * yc, gc_ref[...]).astype(BF)
        y_ref[:, 0:ATTN_W] = ya
        y_ref[:, ATTN_W:] = yb
        z = _dot(ya, w_ref[0:ATTN_W, :]) + _dot(yb, w_ref[ATTN_W:, :])
        z_ref[...] = z
        h2_ref[...] = h_ref[...] + _rms_fwd(z, gp_ref[...])

    row = lambda w: pl.BlockSpec((tm, w), lambda i: (i, 0))
    return _call(
        body, exch,
        name=name,
        grid=(lp // tm,),
        in_specs=[
            row(3 * CONV_W), row(ATTN_W), row(D_MODEL), _full((8, CONV_W)), _full((1, ATTN_W)), _full((1, CONV_W)),
            _full((D_MODEL, D_MODEL)), _full((1, D_MODEL)),
        ],
        out_specs=[row(D_MODEL), row(D_MODEL), row(D_MODEL)],
        out_shape=[
            jax.ShapeDtypeStruct((lp, D_MODEL), BF),
            jax.ShapeDtypeStruct((lp, D_MODEL), F32),
            jax.ShapeDtypeStruct((lp, D_MODEL), F32),
        ],
        scratch_shapes=[pltpu.VMEM((tm + 8, CONV_W), F32)],
        compiler_params=_params(),
    )(bch, y_attn, h, conv_w, g_a, g_c, w_out, g_post)


def _mlp_fwd(h2, g_pre, w_up_t, w_down, g_post, tm, name, exch=None, target=None):
    lp = h2.shape[0]
    sub = math.gcd(tm, BLOCK)
    n_sub, lead = tm // sub, BLOCK // sub
    n_t = n_sub if target is not None else 0

    def body(*refs):
        h_ref, gp_ref, wu_ref, wd_ref, gq_ref = refs[:5]
        t_refs = refs[5:5 + n_t]
        a_ref, up_ref, f_ref, last_ref = refs[5 + n_t:9 + n_t]
        h = h_ref[...]
        a = _rms_fwd(h, gp_ref[...]).astype(BF)
        a_ref[...] = a
        up = _dot_nt(a, wu_ref[...])
        up_ref[...] = up.astype(BF)
        act = jnp.square(jnp.maximum(up, 0.0)).astype(BF)
        f = _dot(act, wd_ref[...])
        f_ref[...] = f
        h3 = h + _rms_fwd(f, gq_ref[...])
        if target is None:
            last_ref[...] = h3
            return
        ls_ref = refs[9 + n_t]
        i = pl.program_id(0)

        @pl.when(i == 0)
        def _():
            ls_ref[...] = jnp.zeros((8, 128), F32)

        sq = jnp.zeros((1, 1), F32)
        for j in range(n_sub):
            on_tokens = i * n_sub + j >= lead
            d = jnp.where(on_tokens, h3[j * sub:(j + 1) * sub] - t_refs[j][...], 0.0)
            last_ref[j * sub:(j + 1) * sub, :] = d * (1.0 / D_MODEL)
            sq = sq + jnp.sum(d * d)
        ls_ref[...] += sq

    row = lambda w: pl.BlockSpec((tm, w), lambda i: (i, 0))
    piece = lambda j: pl.BlockSpec((sub, D_MODEL), lambda i: (jnp.maximum(i * n_sub + j - lead, 0), 0))
    out_specs = [row(D_MODEL), row(D_FF), row(D_MODEL), row(D_MODEL)]
    out_shape = [
        jax.ShapeDtypeStruct((lp, D_MODEL), BF),
        jax.ShapeDtypeStruct((lp, D_FF), BF),
        jax.ShapeDtypeStruct((lp, D_MODEL), F32),
        jax.ShapeDtypeStruct((lp, D_MODEL), F32),
    ]
    if target is not None:
        out_specs.append(_full_out((8, 128)))
        out_shape.append(jax.ShapeDtypeStruct((8, 128), F32))
    return _call(
        body, exch,
        name=name,
        grid=(lp // tm,),
        in_specs=[row(D_MODEL), _full((1, D_MODEL)), _full((D_FF, D_MODEL)), _full((D_FF, D_MODEL)), _full((1, D_MODEL))]
        + [piece(j) for j in range(n_t)],
        out_specs=out_specs,
        out_shape=out_shape,
        compiler_params=_params(),
    )(h2, g_pre, w_up_t, w_down, g_post, *([target] * n_t))


def _mlp_bwd_dx(dh3, f, up, h2, w_down, w_up_t, g_post, g_pre, tm, name, exch=None):
    lp = h2.shape[0]

    def body(dh3_ref, f_ref, up_ref, h2_ref, wd_ref, wu_ref, gq_ref, gp_ref, df_ref, dup_ref, dh2_ref, dg_ref):
        i = pl.program_id(0)

        @pl.when(i == 0)
        def _():
            dg_ref[...] = jnp.zeros((8, D_MODEL), F32)

        dh3 = dh3_ref[...]
        df, dgq = _rms_bwd(f_ref[...], gq_ref[...], dh3)
        dg_ref[ROW_MLP_POST:ROW_MLP_POST + 1, :] += dgq
        df = df.astype(BF)
        df_ref[...] = df
        dact = _dot_nt(df, wd_ref[...])
        dup = (dact * (2.0 * jnp.maximum(up_ref[...].astype(F32), 0.0))).astype(BF)
        dup_ref[...] = dup
        da = _dot(dup, wu_ref[...])
        dh, dgp = _rms_bwd(h2_ref[...], gp_ref[...], da)
        dg_ref[ROW_MLP_PRE:ROW_MLP_PRE + 1, :] += dgp
        dh2_ref[...] = dh3 + dh

    row = lambda w: pl.BlockSpec((tm, w), lambda i: (i, 0))
    return _call(
        body, exch,
        name=name,
        grid=(lp // tm,),
        in_specs=[
            row(D_MODEL), row(D_MODEL), row(D_FF), row(D_MODEL), _full((D_FF, D_MODEL)), _full((D_FF, D_MODEL)),
            _full((1, D_MODEL)), _full((1, D_MODEL)),
        ],
        out_specs=[row(D_MODEL), row(D_FF), row(D_MODEL), _full_out((8, D_MODEL))],
        out_shape=[
            jax.ShapeDtypeStruct((lp, D_MODEL), BF),
            jax.ShapeDtypeStruct((lp, D_FF), BF),
            jax.ShapeDtypeStruct((lp, D_MODEL), F32),
            jax.ShapeDtypeStruct((8, D_MODEL), F32),
        ],
        compiler_params=_params(),
    )(dh3, f, up, h2, w_down, w_up_t, g_post, g_pre)


def _mlp_bwd_dw(up, df, dup, a2, tm, name):
    lp = up.shape[0]
    nt = lp // tm
    nj = D_FF // D_MODEL

    def body(up_ref, df_ref, dup_ref, a_ref, dwd_ref, dwu_ref, accd, accu):
        i = pl.program_id(1)

        @pl.when(i == 0)
        def _():
            accd[...] = jnp.zeros_like(accd)
            accu[...] = jnp.zeros_like(accu)

        act = jnp.square(jnp.maximum(up_ref[...].astype(F32), 0.0)).astype(BF)
        accd[...] += _dot_tn(act, df_ref[...])
        accu[...] += _dot_tn(dup_ref[...], a_ref[...])

        @pl.when(i == nt - 1)
        def _():
            dwd_ref[...] = accd[...].astype(BF)
            dwu_ref[...] = accu[...].astype(BF)

    return pl.pallas_call(
        body,
        name=name,
        grid=(nj, nt),
        in_specs=[
            pl.BlockSpec((tm, D_MODEL), lambda j, i: (i, j)),
            pl.BlockSpec((tm, D_MODEL), lambda j, i: (i, 0)),
            pl.BlockSpec((tm, D_MODEL), lambda j, i: (i, j)),
            pl.BlockSpec((tm, D_MODEL), lambda j, i: (i, 0)),
        ],
        out_specs=[pl.BlockSpec((D_MODEL, D_MODEL), lambda j, i: (j, 0)), pl.BlockSpec((D_MODEL, D_MODEL), lambda j, i: (j, 0))],
        out_shape=[jax.ShapeDtypeStruct((D_FF, D_MODEL), BF), jax.ShapeDtypeStruct((D_FF, D_MODEL), BF)],
        scratch_shapes=[pltpu.VMEM((D_MODEL, D_MODEL), F32), pltpu.VMEM((D_MODEL, D_MODEL), F32)],
        compiler_params=_params(("arbitrary", "arbitrary")),
    )(up, df, dup, a2)


def _mix_out_bwd(dh2, z, y_attn, bch, w_out, g_post, g_a, g_c, conv_w, tm, name, exch=None):
    lp = dh2.shape[0]
    nt = lp // tm
    halo = 16

    def body(dh2_ref, z_ref, ya_ref, bch_ref, halo_ref, w_ref, gp_ref, ga_ref, gc_ref, cw_ref,
             dz_ref, dya_ref, dbch_ref, dg_ref, ext, ext_u):
        i = pl.program_id(0)
        dcw_ref = dg_ref.at[ROW_CONV:ROW_CONV + 3, 0:CONV_W]

        @pl.when(i == 0)
        def _():
            ext[tm:tm + 8, :] = jnp.zeros((8, CONV_W), F32)
            dg_ref[...] = jnp.zeros((8, D_MODEL), F32)

        dz, dgp = _rms_bwd(z_ref[...], gp_ref[...], dh2_ref[...])
        dg_ref[ROW_MIX_POST:ROW_MIX_POST + 1, :] += dgp
        dz = dz.astype(BF)
        dz_ref[...] = dz
        dya_n = _dot_nt(dz, w_ref[0:ATTN_W, :])
        dyb_n = _dot_nt(dz, w_ref[ATTN_W:, :])
        dya, dga = _rms_bwd(ya_ref[...].astype(F32), ga_ref[...], dya_n)
        dg_ref[ROW_GROUP_G:ROW_GROUP_G + 1, 0:ATTN_W] += dga
        dya_ref[...] = dya
        b = bch_ref[:, 0:CONV_W].astype(F32)
        c = bch_ref[:, CONV_W:2 * CONV_W].astype(F32)
        hc = bch_ref[:, 2 * CONV_W:3 * CONV_W].astype(F32)
        u = c * hc
        u_before = halo_ref[:, CONV_W:2 * CONV_W].astype(F32) * halo_ref[:, 2 * CONV_W:3 * CONV_W].astype(F32)
        ext_u[0:halo, :] = jnp.where(i < nt - 1, u_before, 0.0)
        ext_u[halo:halo + tm, :] = u
        yc_v = (cw_ref[0:1, :] * ext_u[halo - 2:halo - 2 + tm, :] + cw_ref[1:2, :] * ext_u[halo - 1:halo - 1 + tm, :]
                + cw_ref[2:3, :] * u)
        dyconv, dgc = _rms_bwd(b * yc_v, gc_ref[...], dyb_n)
        dg_ref[ROW_GROUP_G:ROW_GROUP_G + 1, ATTN_W:] += dgc
        dbch_ref[:, 0:CONV_W] = (dyconv * yc_v).astype(BF)
        dyc = dyconv * b
        ext[0:tm, :] = dyc
        d1 = ext[1:1 + tm, :]
        d2 = ext[2:2 + tm, :]
        du = cw_ref[2:3, :] * dyc + cw_ref[1:2, :] * d1 + cw_ref[0:1, :] * d2
        ext[tm:tm + 8, :] = dyc[0:8, :]
        dbch_ref[:, CONV_W:2 * CONV_W] = (du * hc).astype(BF)
        dbch_ref[:, 2 * CONV_W:3 * CONV_W] = (du * c).astype(BF)
        dcw_ref[0:1, :] += jnp.sum(u * d2, axis=0, keepdims=True)
        dcw_ref[1:2, :] += jnp.sum(u * d1, axis=0, keepdims=True)
        dcw_ref[2:3, :] += jnp.sum(u * dyc, axis=0, keepdims=True)

    row = lambda w: pl.BlockSpec((tm, w), lambda i: (nt - 1 - i, 0))
    before = pl.BlockSpec((halo, 3 * CONV_W), lambda i: (jnp.maximum((nt - 1 - i) * (tm // halo) - 1, 0), 0))
    return _call(
        body, exch,
        name=name,
        grid=(nt,),
        in_specs=[
            row(D_MODEL), row(D_MODEL), row(ATTN_W), row(3 * CONV_W), before, _full((D_MODEL, D_MODEL)),
            _full((1, D_MODEL)), _full((1, ATTN_W)), _full((1, CONV_W)), _full((8, CONV_W)),
        ],
        out_specs=[row(D_MODEL), row(ATTN_W), row(3 * CONV_W), _full_out((8, D_MODEL))],
        out_shape=[
            jax.ShapeDtypeStruct((lp, D_MODEL), BF),
            jax.ShapeDtypeStruct((lp, ATTN_W), F32),
            jax.ShapeDtypeStruct((lp, 3 * CONV_W), BF),
            jax.ShapeDtypeStruct((8, D_MODEL), F32),
        ],
        scratch_shapes=[pltpu.VMEM((tm + 8, CONV_W), F32), pltpu.VMEM((tm + halo, CONV_W), F32)],
        compiler_params=_params(),
    )(dh2, z, y_attn, bch, bch, w_out, g_post, g_a, g_c, conv_w)


def _attn_bwd(qkv, o, do, sink, rope, name, exch=None):
    lp = qkv.shape[0]
    nb = lp // BLOCK

    def body(sink_ref, q_ref, kvc_ref, kvp_ref, o_ref, do_ref, cq_ref, s1q_ref, s2q_ref, ck_ref, s1k_ref, s2k_ref,
             dq_ref, dkv_ref, dsink_ref, carry):
        i = pl.program_id(0)

        @pl.when(i == 0)
        def _():
            carry[...] = jnp.zeros_like(carry)
            dsink_ref[...] = jnp.zeros((8, 128), F32)

        def finish(tot):
            dk = _rope_t(tot[:, :128], ck_ref[...], s1k_ref[...], s2k_ref[...])
            dkv_ref[:, 0:128] = dk.astype(BF)
            dkv_ref[:, 128:256] = tot[:, 128:].astype(BF)

        @pl.when(i < nb)
        def _():
            tri, ok = _fold_masks(i)
            kvc, kvp = kvc_ref[...], kvp_ref[...]
            kk = jnp.concatenate([kvp[:, :128], kvc[:, :128]], axis=0)
            vv = jnp.concatenate([kvp[:, 128:], kvc[:, 128:]], axis=0)
            lane = lax.broadcasted_iota(jnp.int32, (BLOCK, 128), 1)
            lane2 = lax.broadcasted_iota(jnp.int32, (2 * BLOCK, 128), 1)
            first = lax.broadcasted_iota(jnp.int32, (2 * BLOCK, 1), 0) < BLOCK
            row_s = lax.broadcasted_iota(jnp.int32, (8, 128), 0)
            lane_s = jnp.where(row_s == ROW_SINK, lax.broadcasted_iota(jnp.int32, (8, 128), 1), -1)
            rope_q = (cq_ref[...], s1q_ref[...], s2q_ref[...])
            dsink = jnp.zeros((8, 128), F32)
            folded = []
            for kvh in range(2):
                c0 = 256 * kvh
                q2 = jnp.concatenate([q_ref[:, c0:c0 + 128], q_ref[:, c0 + 128:c0 + 256]], axis=0)
                do2 = jnp.concatenate([do_ref[:, c0:c0 + 128], do_ref[:, c0 + 128:c0 + 256]], axis=0)
                o2 = jnp.concatenate([o_ref[:, c0:c0 + 128], o_ref[:, c0 + 128:c0 + 256]], axis=0).astype(F32)
                k4, v4 = _kv_operand(kk, kvh), _kv_operand(vv, kvh)
                probs = _folded_probs(q2, k4, tri, ok, _sink_cols(sink_ref, kvh))
                prod = do2 * o2
                dob = do2.astype(BF)
                dp = _dot_nt(dob, v4)
                ds, pb = [], []
                for half in range(2):
                    p, ps = probs[half]
                    sel = (lane2 < HEAD_DIM) if half == 0 else (lane2 >= HEAD_DIM)
                    delta = jnp.sum(jnp.where(sel, prod, 0.0), axis=-1, keepdims=True)
                    dp_h = dp[:, 2 * half * BLOCK:2 * (half + 1) * BLOCK]
                    ds.append((p * (jnp.where(tri, dp_h[:, :BLOCK], dp_h[:, BLOCK:]) - delta)).astype(BF))
                    pb.append(p.astype(BF))
                    t = ps * delta
                    for jj in range(2):
                        part = -jnp.sum(jnp.where(first if jj == 0 else ~first, t, 0.0))
                        dsink = dsink + jnp.where(lane_s == 4 * kvh + 2 * jj + half, part, 0.0)
                ds4, p4 = _split4(ds, tri), _split4(pb, tri)
                dq2 = _dot(ds4, k4) * SCALE
                dq_ref[:, c0:c0 + 128] = _rope_t(dq2[:BLOCK], *rope_q).astype(BF)
                dq_ref[:, c0 + 128:c0 + 256] = _rope_t(dq2[BLOCK:], *rope_q).astype(BF)
                rk, rv = _dot_tn(ds4, q2), _dot_tn(p4, dob)
                own = (lane < HEAD_DIM) if kvh == 0 else (lane >= HEAD_DIM)
                group = []
                for r in (rk, rv):
                    for blk in range(2):
                        t = jnp.where(lane < HEAD_DIM, r[blk * BLOCK:(blk + 1) * BLOCK], r[(2 + blk) * BLOCK:(3 + blk) * BLOCK])
                        group.append(jnp.where(own, t + pltpu.roll(t, HEAD_DIM, 1), 0.0))
                folded.append(group)
            dsink_ref[...] += dsink
            dk_p, dk_c, dv_p, dv_c = [folded[0][t] + folded[1][t] for t in range(4)]
            finish(carry[...] + jnp.concatenate([dk_p, dv_p], axis=1))
            carry[...] = jnp.concatenate([dk_c, dv_c], axis=1)

        @pl.when(i == nb)
        def _():
            finish(carry[...])

    qi = lambda i: jnp.minimum(i, nb - 1)
    ki = lambda i: jnp.maximum(i - 1, 0)
    tab_q = pl.BlockSpec((BLOCK, 128), lambda i: (qi(i), 0))
    tab_k = pl.BlockSpec((BLOCK, 128), lambda i: (ki(i), 0))
    return _call(
        body, exch,
        name=name,
        grid=(nb + 1,),
        in_specs=[
            pl.BlockSpec(memory_space=pltpu.SMEM),
            pl.BlockSpec((BLOCK, ATTN_W), lambda i: (qi(i), 0)),
            pl.BlockSpec((BLOCK, 256), lambda i: (qi(i), 2)),
            pl.BlockSpec((BLOCK, 256), lambda i: (jnp.maximum(qi(i) - 1, 0), 2)),
            pl.BlockSpec((BLOCK, ATTN_W), lambda i: (qi(i), 0)),
            pl.BlockSpec((BLOCK, ATTN_W), lambda i: (qi(i), 0)),
            tab_q, tab_q, tab_q, tab_k, tab_k, tab_k,
        ],
        out_specs=[
            pl.BlockSpec((BLOCK, ATTN_W), lambda i: (qi(i), 0)),
            pl.BlockSpec((BLOCK, 256), lambda i: (ki(i), 0)),
            pl.BlockSpec((8, 128), lambda i: (0, 0)),
        ],
        out_shape=[
            jax.ShapeDtypeStruct((lp, ATTN_W), BF),
            jax.ShapeDtypeStruct((lp, 256), BF),
            jax.ShapeDtypeStruct((8, 128), F32),
        ],
        scratch_shapes=[pltpu.VMEM((BLOCK, 256), F32)],
        compiler_params=_params(),
    )(sink, qkv, qkv, qkv, o, do, *rope, *rope)


def _in_proj_bwd_dx(dq, dkv, dbch, w_in_t, h, dh2, g, tm, name, exch=None):
    lp = h.shape[0]

    def body(dq_ref, dkv_ref, dbch_ref, w_ref, h_ref, dh2_ref, g_ref, dh_ref, dg_ref):
        i = pl.program_id(0)

        @pl.when(i == 0)
        def _():
            dg_ref[...] = jnp.zeros((8, D_MODEL), F32)

        da = _dot(dq_ref[...], w_ref[0:512, :]) + _dot(dkv_ref[...], w_ref[512:768, :]) + _dot(dbch_ref[...], w_ref[768:, :])
        dh, dg = _rms_bwd(h_ref[...], g_ref[...], da)
        dg_ref[ROW_MIX_PRE:ROW_MIX_PRE + 1, :] += dg
        dh_ref[...] = dh2_ref[...] + dh

    row = lambda w: pl.BlockSpec((tm, w), lambda i: (i, 0))
    return _call(
        body, exch,
        name=name,
        grid=(lp // tm,),
        in_specs=[row(ATTN_W), row(256), row(3 * CONV_W), _full((IN_W, D_MODEL)), row(D_MODEL), row(D_MODEL), _full((1, D_MODEL))],
        out_specs=[row(D_MODEL), _full_out((8, D_MODEL))],
        out_shape=[jax.ShapeDtypeStruct((lp, D_MODEL), F32), jax.ShapeDtypeStruct((8, D_MODEL), F32)],
        compiler_params=_params(),
    )(dq, dkv, dbch, w_in_t, h, dh2, g)


def _mix_bwd_dw(dq, dkv, dbch, a, y, dz, tm, name, exch=None):
    lp = a.shape[0]
    nt = lp // tm

    def body(dq_ref, dkv_ref, dbch_ref, a_ref, y_ref, dz_ref, dwi_ref, dwo_ref, acci, acco):
        i = pl.program_id(0)

        @pl.when(i == 0)
        def _():
            acci[...] = jnp.zeros_like(acci)
            acco[...] = jnp.zeros_like(acco)

        a_v = a_ref[...]
        acci[0:512, :] += _dot_tn(dq_ref[...], a_v)
        acci[512:768, :] += _dot_tn(dkv_ref[...], a_v)
        acci[768:, :] += _dot_tn(dbch_ref[...], a_v)
        acco[...] += _dot_tn(y_ref[...], dz_ref[...])

        @pl.when(i == nt - 1)
        def _():
            dwi_ref[...] = acci[...].astype(BF)
            dwo_ref[...] = acco[...].astype(BF)

    row = lambda w: pl.BlockSpec((tm, w), lambda i: (i, 0))
    return _call(
        body, exch,
        name=name,
        grid=(nt,),
        in_specs=[row(ATTN_W), row(256), row(3 * CONV_W), row(D_MODEL), row(D_MODEL), row(D_MODEL)],
        out_specs=[_full_out((IN_W, D_MODEL)), _full_out((D_MODEL, D_MODEL))],
        out_shape=[jax.ShapeDtypeStruct((IN_W, D_MODEL), BF), jax.ShapeDtypeStruct((D_MODEL, D_MODEL), BF)],
        scratch_shapes=[pltpu.VMEM((IN_W, D_MODEL), F32), pltpu.VMEM((D_MODEL, D_MODEL), F32)],
        compiler_params=_params(),
    )(dq, dkv, dbch, a, y, dz)


def _mesh_place():
    x, y, c = lax.axis_index("x"), lax.axis_index("y"), lax.axis_index("c")
    return x, y, c, 4 * x + 2 * y + c


def _peer(x, y, c, k):
    px = 1 - x if k & 4 else x
    py = 1 - y if k & 2 else y
    pc = 1 - c if k & 1 else c
    return (px, py, pc), 4 * px + 2 * py + pc


SIBLING = 1
SAME_CORE = (2, 4, 6)
OTHER_CORE = (3, 5, 7)


class _Exchange:
    def __init__(self, pieces):
        self.srcs = [s for s, _ in pieces]
        self.to_all = [g for _, g in pieces]
        self.n = len(pieces)
        self.land_shapes = [
            jax.ShapeDtypeStruct((N_DEV,) + (s.shape if g else s.shape[1:]), s.dtype) for s, g in pieces]
        self.sem_shapes = [pltpu.SemaphoreType.DMA((self.n, N_DEV - 1)), pltpu.SemaphoreType.DMA((self.n, N_DEV - 1)),
                           pltpu.SemaphoreType.DMA((self.n,))]
        self.forwards = any(self.to_all)

    def _ops(self, srcs, lands, sems):
        send_sems, recv_sems, local_sems = sems
        x, y, c, me = _mesh_place()

        def remote(p, k, src, slot, to):
            return pltpu.make_async_remote_copy(
                src_ref=src, dst_ref=lands[p].at[slot], send_sem=send_sems.at[p, k - 1], recv_sem=recv_sems.at[p, k - 1],
                device_id=to, device_id_type=MESH)

        def own(p):
            return pltpu.make_async_copy(srcs[p] if self.to_all[p] else srcs[p].at[me], lands[p].at[me], local_sems.at[p])

        def direct(p, k):
            peer, pidx = _peer(x, y, c, k)
            return remote(p, k, srcs[p] if self.to_all[p] else srcs[p].at[pidx], me, peer)

        def forward(p, k):
            sibling, _ = _peer(x, y, c, SIBLING)
            _, origin = _peer(x, y, c, k ^ SIBLING)
            return remote(p, k, lands[p].at[origin], origin, sibling)

        def arrival(p, k):
            peer, pidx = _peer(x, y, c, k)
            return remote(p, k, lands[p].at[pidx], pidx, peer)

        return own, direct, forward, arrival

    def start(self, srcs, lands, sems):
        own, direct, _, _ = self._ops(srcs, lands, sems)
        for p in range(self.n):
            own(p).start()
            for k in ((SIBLING,) + SAME_CORE) if self.to_all[p] else range(1, N_DEV):
                direct(p, k).start()

    def forward(self, srcs, lands, sems):
        _, _, forward, arrival = self._ops(srcs, lands, sems)
        for p in range(self.n):
            if self.to_all[p]:
                for k in SAME_CORE:
                    arrival(p, k).wait_recv()
                    forward(p, k ^ SIBLING).start()

    def finish(self, srcs, lands, sems):
        own, direct, forward, arrival = self._ops(srcs, lands, sems)
        for p in range(self.n):
            for k in ((SIBLING,) + OTHER_CORE) if self.to_all[p] else range(1, N_DEV):
                arrival(p, k).wait_recv()
        for p in range(self.n):
            for k in range(1, N_DEV):
                (forward(p, k) if self.to_all[p] and k in OTHER_CORE else direct(p, k)).wait_send()
            own(p).wait()


def _exchange_call(exch, name):
    def body(*refs):
        srcs, lands, sems = refs[:exch.n], refs[exch.n:2 * exch.n], refs[2 * exch.n:]
        exch.start(srcs, lands, sems)
        exch.forward(srcs, lands, sems)
        exch.finish(srcs, lands, sems)

    hbm = pl.BlockSpec(memory_space=pl.ANY)
    return pl.pallas_call(
        body,
        name=name,
        in_specs=[hbm] * exch.n,
        out_specs=[hbm] * exch.n,
        out_shape=exch.land_shapes,
        scratch_shapes=exch.sem_shapes,
    )(*exch.srcs)


def _call(body, exch, *, name, grid, in_specs, out_specs, out_shape, scratch_shapes=(), compiler_params):
    if exch is None:
        return pl.pallas_call(body, name=name, grid=grid, in_specs=in_specs, out_specs=out_specs, out_shape=out_shape,
                              scratch_shapes=scratch_shapes, compiler_params=compiler_params)
    n_in, n_out, n_scr, n_x = len(in_specs), len(out_shape), len(scratch_shapes), exch.n
    steps = math.prod(grid)

    def carrying(*refs):
        a, b, c, d, e = n_in, n_in + n_x, n_in + n_x + n_out, n_in + 2 * n_x + n_out, n_in + 2 * n_x + n_out + n_scr
        ins, srcs, outs, lands, scr, sems = refs[:a], refs[a:b], refs[b:c], refs[c:d], refs[d:e], refs[e:]
        step = functools.reduce(lambda acc, t: acc * grid[t] + pl.program_id(t), range(len(grid)), 0)

        @pl.when(step == 0)
        def _():
            exch.start(srcs, lands, sems)

        body(*ins, *outs, *scr)

        if exch.forwards:
            @pl.when(step == max(0, steps - 1 - (steps + 7) // 8))
            def _():
                exch.forward(srcs, lands, sems)

        @pl.when(step == steps - 1)
        def _():
            exch.finish(srcs, lands, sems)

    hbm = pl.BlockSpec(memory_space=pl.ANY)
    call = pl.pallas_call(
        carrying, name=name, grid=grid, in_specs=list(in_specs) + [hbm] * n_x, out_specs=list(out_specs) + [hbm] * n_x,
        out_shape=list(out_shape) + exch.land_shapes, scratch_shapes=list(scratch_shapes) + exch.sem_shapes,
        compiler_params=compiler_params)

    def run(*args):
        res = call(*args, *exch.srcs)
        return list(res[:n_out]), list(res[n_out:])

    return run


def _sum_small(part):
    def body(part_ref, out_ref, land, send_sems, recv_sems):
        x, y, c, me = _mesh_place()
        land[me] = part_ref[...]
        sent = []
        for k in range(1, N_DEV):
            peer, _ = _peer(x, y, c, k)
            cp = pltpu.make_async_remote_copy(
                src_ref=part_ref, dst_ref=land.at[me], send_sem=send_sems.at[k - 1], recv_sem=recv_sems.at[k - 1],
                device_id=peer, device_id_type=MESH)
            cp.start()
            sent.append(cp)
        for k in range(1, N_DEV):
            peer, pidx = _peer(x, y, c, k)
            pltpu.make_async_remote_copy(
                src_ref=part_ref, dst_ref=land.at[pidx], send_sem=send_sems.at[k - 1], recv_sem=recv_sems.at[k - 1],
                device_id=peer, device_id_type=MESH).wait_recv()
        for cp in sent:
            cp.wait_send()
        acc = land[0]
        for d in range(1, N_DEV):
            acc = acc + land[d]
        out_ref[...] = acc

    vmem = pl.BlockSpec(memory_space=pltpu.VMEM)
    return pl.pallas_call(
        body,
        name="sum_small",
        in_specs=[vmem],
        out_specs=vmem,
        out_shape=jax.ShapeDtypeStruct(part.shape, F32),
        scratch_shapes=[pltpu.VMEM((N_DEV,) + part.shape, F32), pltpu.SemaphoreType.DMA((N_DEV - 1,)),
                        pltpu.SemaphoreType.DMA((N_DEV - 1,))],
    )(part)


def _adamw(w, g, m, v):
    m = ADAM_B1 * m + (1.0 - ADAM_B1) * g
    v = ADAM_B2 * v + (1.0 - ADAM_B2) * jnp.square(g)
    m_hat = m / (1.0 - ADAM_B1 ** ADAM_STEP)
    v_hat = v / (1.0 - ADAM_B2 ** ADAM_STEP)
    delta = -ADAM_LR * (m_hat / (jnp.sqrt(v_hat) + ADAM_EPS) + ADAM_WD * w)
    return delta, m, v


def _landed_specs(tr, wd):
    return [pl.BlockSpec((N_DEV, tr, wd), lambda l, i, ll=ll: (0, jnp.where(l == ll, i, 0), 0)) for ll in range(DEPTH)]


def _device_sum(r_ref):
    acc = r_ref[0].astype(F32)
    for d in range(1, N_DEV):
        acc = acc + r_ref[d].astype(F32)
    return acc


def _sum_parts(recv, tr, name):
    _, r, wd = recv[0].shape

    def body(*refs):
        g_ref = refs[DEPTH]
        for ll in range(DEPTH):
            @pl.when(pl.program_id(0) == ll)
            def _(ll=ll):
                g_ref[0] = _device_sum(refs[ll])

    return pl.pallas_call(
        body,
        name=name,
        grid=(DEPTH, r // tr),
        in_specs=_landed_specs(tr, wd),
        out_specs=pl.BlockSpec((1, tr, wd), lambda l, i: (l, i, 0)),
        out_shape=jax.ShapeDtypeStruct((DEPTH, r, wd), F32),
        compiler_params=_params(("arbitrary", "arbitrary")),
    )(*recv)


def _sum_adamw(recv, w, m, v, tr, name):
    _, r, wd = recv[0].shape

    def body(*refs):
        w_ref, m_ref, v_ref, g_ref, d_ref, mo_ref, vo_ref = refs[DEPTH:]
        for ll in range(DEPTH):
            @pl.when(pl.program_id(0) == ll)
            def _(ll=ll):
                g = _device_sum(refs[ll])
                g_ref[0] = g
                d_ref[0], mo_ref[0], vo_ref[0] = _adamw(w_ref[0], g, m_ref[0], v_ref[0])

    blk = pl.BlockSpec((1, tr, wd), lambda l, i: (l, i, 0))
    shape = jax.ShapeDtypeStruct((DEPTH, r, wd), F32)
    return pl.pallas_call(
        body,
        name=name,
        grid=(DEPTH, r // tr),
        in_specs=_landed_specs(tr, wd) + [blk, blk, blk],
        out_specs=[blk] * 4,
        out_shape=[shape] * 4,
        compiler_params=_params(("arbitrary", "arbitrary")),
    )(*recv, w, m, v)


def _adamw_rows(w, g, m, v, tr, name):
    _, r, wd = w.shape

    def body(w_ref, g_ref, m_ref, v_ref, d_ref, mo_ref, vo_ref):
        d_ref[0], mo_ref[0], vo_ref[0] = _adamw(w_ref[0], g_ref[0], m_ref[0], v_ref[0])

    blk = pl.BlockSpec((1, tr, wd), lambda l, i: (l, i, 0))
    shape = jax.ShapeDtypeStruct(w.shape, F32)
    return pl.pallas_call(
        body,
        name=name,
        grid=(DEPTH, r // tr),
        in_specs=[blk] * 4,
        out_specs=[blk] * 3,
        out_shape=[shape] * 3,
        compiler_params=_params(("arbitrary", "arbitrary")),
    )(w, g, m, v)


def _adamw_small(ws, gs, ms, vs):
    n = len(ws)

    def body(*refs):
        w_r, g_r, m_r, v_r = refs[:n], refs[n:2 * n], refs[2 * n:3 * n], refs[3 * n:4 * n]
        d_o, m_o, v_o = refs[4 * n:5 * n], refs[5 * n:6 * n], refs[6 * n:7 * n]
        for t in range(n):
            d_o[t][...], m_o[t][...], v_o[t][...] = _adamw(w_r[t][...], g_r[t][...], m_r[t][...], v_r[t][...])

    vmem = pl.BlockSpec(memory_space=pltpu.VMEM)
    shapes = [jax.ShapeDtypeStruct(w.shape, F32) for w in ws]
    outs = pl.pallas_call(
        body,
        name="adamw_small",
        in_specs=[vmem] * (4 * n),
        out_specs=[vmem] * (3 * n),
        out_shape=shapes * 3,
    )(*ws, *gs, *ms, *vs)
    return outs[:n], outs[n:2 * n], outs[2 * n:]


def kernel(x, meta_tokens, mix_pre_g, w_in, conv_w, sinks, attn_out_g, conv_out_g, w_out, mix_post_g, mlp_pre_g, w_up, w_down, mlp_post_g, loss_target, m_meta_tokens, m_mix_pre_g, m_w_in, m_conv_w, m_sinks, m_attn_out_g, m_conv_out_g, m_w_out, m_mix_post_g, m_mlp_pre_g, m_w_up, m_w_down, m_mlp_post_g, v_meta_tokens, v_mix_pre_g, v_w_in, v_conv_w, v_sinks, v_attn_out_g, v_conv_out_g, v_w_out, v_mix_post_g, v_mlp_pre_g, v_w_up, v_w_down, v_mlp_post_g):
    seq = x.shape[1]
    lp = BLOCK + seq
    tm = _row_tile(lp)
    tm_mlp = _row_tile(lp, (320, 256, 128))
    tm_dw_mlp = _row_tile(lp, (1664, 1040, 640, 384, 256, 128))
    tm_dw_mix = _row_tile(lp, (832, 640, 384, 256, 128))
    me = 4 * lax.axis_index("x") + 2 * lax.axis_index("y") + lax.axis_index("c")
    cshard = CONV_W // N_DEV
    mshard = D_MODEL // N_DEV

    gather_with = {
        ("in_proj_fwd", 0): [("out", 0), ("in", 1)], ("attn_fwd", 0): [("up", 0)], ("mix_out_fwd", 0): [("down", 0)],
        ("mlp_fwd", 0): [("out", 1), ("up", 1), ("down", 1)],
    }
    scatter_with = {
        ("attn_bwd", 1): [("down", 1)], ("mlp_bwd_dx", 0): [("up", 1), ("in", 1), ("out", 1)],
        ("attn_bwd", 0): [("down", 0)], ("mix_bwd_dw", 0): [("up", 0)], ("in_proj_bwd_dx", 0): [("in", 0), ("out", 0)],
    }
    shard = {"in": jnp.swapaxes(w_in, 1, 2).astype(BF), "out": w_out.astype(BF),
             "up": jnp.swapaxes(w_up, 1, 2).astype(BF), "down": w_down.astype(BF)}
    weight = {}
    grad = {}
    landed = {}

    def run(fn, kind, l, *args):
        key, name = (kind, l), f"{kind}_{l}"
        if key in gather_with:
            blocks = gather_with[key]
            outs, lands = fn(*args, name, _Exchange([(shard[n][k], True) for n, k in blocks]))
            for b, land in zip(blocks, lands):
                weight[b] = land.reshape(-1, D_MODEL)
            return outs
        if key in scatter_with:
            blocks = scatter_with[key]
            outs, lands = fn(*args, name, _Exchange([(grad[b].reshape(N_DEV, -1, D_MODEL), False) for b in blocks]))
            landed.update(zip(blocks, lands))
            return outs
        return fn(*args, name)

    small = jnp.zeros((24, 128), F32)
    small = small.at[0:N_META, :].set(meta_tokens)
    small = small.at[N_META:N_META + 6, 0:cshard].set(conv_w.reshape(6, cshard))
    first_in, g_small = _exchange_call(_Exchange([(shard["in"][0], True), (small, True)]), "gather_first")
    weight[("in", 0)] = first_in.reshape(-1, D_MODEL)
    meta_full = jnp.swapaxes(g_small[:, 0:N_META, :], 0, 1).reshape(N_META, D_MODEL)
    cw = g_small[:, N_META:N_META + 6, 0:cshard].reshape(N_DEV, DEPTH, 3, cshard)
    cw = jnp.transpose(cw, (1, 2, 0, 3)).reshape(DEPTH, 3, CONV_W)
    conv_full = jnp.concatenate([cw, jnp.zeros((DEPTH, 5, CONV_W), F32)], axis=1)

    rope = _rope_tables(_rope_table(lp), tm)
    row1 = lambda a, l: a[l].reshape(1, -1)

    h = jnp.concatenate([jnp.zeros((LEAD_PAD, D_MODEL), F32), meta_full, x[0]], axis=0)
    saved = []
    for l in range(DEPTH):
        a, qkv, bch = run(_in_proj_fwd, "in_proj_fwd", l, h, row1(mix_pre_g, l), weight[("in", l)], rope, tm)
        y_attn, = run(_attn_fwd, "attn_fwd", l, qkv, row1(sinks, l))
        y, z, h2 = run(_mix_out_fwd, "mix_out_fwd", l, bch, y_attn, h, conv_full[l], row1(attn_out_g, l),
                           row1(conv_out_g, l), weight[("out", l)], row1(mix_post_g, l), tm)
        mlp = _mlp_fwd if l < DEPTH - 1 else functools.partial(_mlp_fwd, target=loss_target[0])
        a2, up, f, *rest = run(mlp, "mlp_fwd", l, h2, row1(mlp_pre_g, l), weight[("up", l)], weight[("down", l)],
                               row1(mlp_post_g, l), tm_mlp)
        saved.append((h, a, qkv, bch, y_attn, y, z, h2, a2, up, f))
        h = rest[0]
    dh, loss_part = rest[0], rest[1][0, 0] * (0.5 / D_MODEL)

    gsmall = [None] * DEPTH
    for l in reversed(range(DEPTH)):
        h0, a, qkv, bch, y_attn, y, z, h2, a2, up, f = saved[l]
        df, dup, dh2, dg_mlp = run(_mlp_bwd_dx, "mlp_bwd_dx", l, dh, f, up, h2, weight[("down", l)], weight[("up", l)],
                                   row1(mlp_post_g, l), row1(mlp_pre_g, l), tm_mlp)
        grad[("down", l)], grad[("up", l)] = _mlp_bwd_dw(up, df, dup, a2, tm_dw_mlp, f"mlp_bwd_dw_{l}")
        dz, dya, dbch, dg_mix = run(_mix_out_bwd, "mix_out_bwd", l, dh2, z, y_attn, bch, weight[("out", l)],
                                    row1(mix_post_g, l), row1(attn_out_g, l), row1(conv_out_g, l), conv_full[l], tm)
        dq, dkv, dsink = run(_attn_bwd, "attn_bwd", l, qkv, y_attn, dya, row1(sinks, l), rope)
        grad[("in", l)], grad[("out", l)] = run(_mix_bwd_dw, "mix_bwd_dw", l, dq, dkv, dbch, a, y, dz, tm_dw_mix)
        dh, dg_in = run(_in_proj_bwd_dx, "in_proj_bwd_dx", l, dq, dkv, dbch, weight[("in", l)], h0, dh2,
                        row1(mix_pre_g, l), tm)
        tile_a = dg_mlp + dg_in + jnp.pad(dsink, ((0, 0), (0, D_MODEL - 128)))
        gsmall[l] = (tile_a, dg_mix)
    grad_x = dh[BLOCK:][None]

    loss_tile = jnp.zeros((8, D_MODEL), F32).at[ROW_LOSS, 0].set(loss_part)
    tot = _sum_small(jnp.concatenate(
        [gsmall[0][0] + loss_tile, gsmall[0][1], gsmall[1][0], gsmall[1][1], dh[LEAD_PAD:BLOCK]], axis=0))
    loss = tot[ROW_LOSS, 0]
    ta = [tot[16 * l:16 * l + 8] for l in range(DEPTH)]
    tb = [tot[16 * l + 8:16 * l + 16] for l in range(DEPTH)]
    pick = lambda tiles, r0, r1, c0, c1: jnp.stack([t[r0:r1, c0:c1] for t in tiles])
    g_mlp_post = pick(ta, ROW_MLP_POST, ROW_MLP_POST + 1, 0, D_MODEL).reshape(DEPTH, D_MODEL)
    g_mlp_pre = pick(ta, ROW_MLP_PRE, ROW_MLP_PRE + 1, 0, D_MODEL).reshape(DEPTH, D_MODEL)
    g_mix_pre = pick(ta, ROW_MIX_PRE, ROW_MIX_PRE + 1, 0, D_MODEL).reshape(DEPTH, D_MODEL)
    g_sinks = pick(ta, ROW_SINK, ROW_SINK + 1, 0, N_Q_HEADS).reshape(DEPTH, N_Q_HEADS)
    g_mix_post = pick(tb, ROW_MIX_POST, ROW_MIX_POST + 1, 0, D_MODEL).reshape(DEPTH, D_MODEL)
    g_attn_out = pick(tb, ROW_GROUP_G, ROW_GROUP_G + 1, 0, ATTN_W).reshape(DEPTH, ATTN_W)
    g_conv_out = pick(tb, ROW_GROUP_G, ROW_GROUP_G + 1, ATTN_W, D_MODEL).reshape(DEPTH, CONV_W)
    g_conv_full = pick(tb, ROW_CONV, ROW_CONV + 3, 0, CONV_W)
    g_conv = lax.dynamic_slice_in_dim(g_conv_full, me * cshard, cshard, axis=2)
    g_meta = lax.dynamic_slice_in_dim(tot[16 * DEPTH:16 * DEPTH + N_META], me * mshard, mshard, axis=1)

    r_in, r_out, r_up, r_down = [[landed[(n, l)] for l in range(DEPTH)] for n in ("in", "out", "up", "down")]
    g_w_in = jnp.swapaxes(_sum_parts(r_in, 96, "sum_w_in"), 1, 2)
    g_w_up = jnp.swapaxes(_sum_parts(r_up, 128, "sum_w_up"), 1, 2)
    d_w_in, nm_w_in, nv_w_in = _adamw_rows(w_in, g_w_in, m_w_in, v_w_in, 256, "adamw_w_in")
    d_w_up, nm_w_up, nv_w_up = _adamw_rows(w_up, g_w_up, m_w_up, v_w_up, 256, "adamw_w_up")
    g_w_out, d_w_out, nm_w_out, nv_w_out = _sum_adamw(r_out, w_out, m_w_out, v_w_out, 128, "adamw_w_out")
    g_w_down, d_w_down, nm_w_down, nv_w_down = _sum_adamw(r_down, w_down, m_w_down, v_w_down, 128, "adamw_w_down")

    ws = [meta_tokens, mix_pre_g, conv_w.reshape(6, cshard), sinks, attn_out_g, conv_out_g, mix_post_g, mlp_pre_g, mlp_post_g]
    gs = [g_meta, g_mix_pre, g_conv.reshape(6, cshard), g_sinks, g_attn_out, g_conv_out, g_mix_post, g_mlp_pre, g_mlp_post]
    ms = [m_meta_tokens, m_mix_pre_g, m_conv_w.reshape(6, cshard), m_sinks, m_attn_out_g, m_conv_out_g, m_mix_post_g,
          m_mlp_pre_g, m_mlp_post_g]
    vs = [v_meta_tokens, v_mix_pre_g, v_conv_w.reshape(6, cshard), v_sinks, v_attn_out_g, v_conv_out_g, v_mix_post_g,
          v_mlp_pre_g, v_mlp_post_g]
    ds, nms, nvs = _adamw_small(ws, gs, ms, vs)

    def order(meta, mix_pre, cv, sk, a_out, c_out, mix_post, mlp_pre, mlp_post, win, wout, wup, wdown):
        return [meta, mix_pre, win, cv.reshape(DEPTH, 3, cshard), sk, a_out, c_out, wout, mix_post, mlp_pre, wup, wdown, mlp_post]

    grads = order(*gs, g_w_in, g_w_out, g_w_up, g_w_down)
    deltas = order(*ds, d_w_in, d_w_out, d_w_up, d_w_down)
    new_m = order(*nms, nm_w_in, nm_w_out, nm_w_up, nm_w_down)
    new_v = order(*nvs, nv_w_in, nv_w_out, nv_w_up, nv_w_down)
    return (loss, grad_x, *grads, *deltas, *new_m, *new_v)
```

```python
import functools
import math

import jax
import jax.numpy as jnp
from jax import lax
from jax.experimental import pallas as pl
from jax.experimental.pallas import tpu as pltpu

F32 = jnp.float32
BF = jnp.bfloat16

D_MODEL = 1024
ATTN_W = 512
CONV_W = 512
KV_W = 128
HEAD_DIM = 64
N_Q_HEADS = 8
ROT_DIM = 16
D_FF = 4096
IN_W = 2304
N_META = 16
BLOCK = 128
LEAD_PAD = BLOCK - N_META
ROPE_THETA = 500000.0
EPS = 1e-6
N_DEV = 8
DEPTH = 2
NEG = -1e30
SCALE = HEAD_DIM ** -0.5

ADAM_LR = 0.001
ADAM_B1 = 0.9
ADAM_B2 = 0.999
ADAM_EPS = 1e-08
ADAM_WD = 0.01
ADAM_STEP = 10

ROW_MLP_POST, ROW_MLP_PRE, ROW_MIX_PRE, ROW_SINK, ROW_LOSS = 0, 1, 2, 3, 4
ROW_MIX_POST, ROW_GROUP_G, ROW_CONV = 0, 1, 2

VMEM_LIMIT = 56 * 1024 * 1024
MESH = pl.DeviceIdType.MESH


def _dot(a, b):
    return jnp.dot(a, b, preferred_element_type=F32)


def _dot_nt(a, b):
    return lax.dot_general(a, b, (((1,), (1,)), ((), ())), preferred_element_type=F32)


def _dot_tn(a, b):
    return lax.dot_general(a, b, (((0,), (0,)), ((), ())), preferred_element_type=F32)


def _rms_fwd(x, g):
    r = lax.rsqrt(jnp.mean(x * x, axis=-1, keepdims=True) + EPS)
    return x * r * g


def _rms_bwd(x, g, dy):
    r = lax.rsqrt(jnp.mean(x * x, axis=-1, keepdims=True) + EPS)
    xh = x * r
    t = dy * g
    dx = r * (t - xh * jnp.mean(t * xh, axis=-1, keepdims=True))
    dg = jnp.sum(dy * xh, axis=0, keepdims=True)
    return dx, dg


def _row_tile(lp, cands=(640, 512, 384, 256, 128)):
    for t in cands:
        if lp % t == 0:
            return t
    raise ValueError(f"row count {lp} is not a multiple of 128")


def _full(shape):
    n = len(shape)
    return pl.BlockSpec(shape, lambda *_: (0,) * n, pipeline_mode=pl.Buffered(1))


def _full_out(shape):
    n = len(shape)
    return pl.BlockSpec(shape, lambda *_: (0,) * n)


def _params(sem=("arbitrary",)):
    return pltpu.CompilerParams(dimension_semantics=sem, vmem_limit_bytes=VMEM_LIMIT)


def _rope_table(lp):
    half = ROT_DIM // 2
    pos = jnp.maximum(jnp.arange(lp) - LEAD_PAD, 0).astype(F32)
    inv_freq = jnp.power(jnp.float32(ROPE_THETA), -jnp.arange(0, ROT_DIM, 2, dtype=F32) / ROT_DIM)
    ang_t = jnp.concatenate([inv_freq, inv_freq])[:, None] * pos[None, :]
    row = lax.broadcasted_iota(jnp.int32, (ROT_DIM, lp), 0)
    cs_t = jnp.where(row < half, jnp.cos(ang_t), jnp.sin(ang_t))
    return jnp.pad(cs_t.T, ((0, 0), (0, 128 - ROT_DIM)))


def _rope_coeffs(t):
    half = ROT_DIM // 2
    lane = lax.broadcasted_iota(jnp.int32, t.shape, 1)
    cos_a = jnp.where(lane < half, t, 0.0)
    sin_a = pltpu.roll(jnp.where((lane >= half) & (lane < ROT_DIM), t, 0.0), 128 - half, 1)
    c = cos_a + pltpu.roll(cos_a, half, 1) + jnp.where((lane >= ROT_DIM) & (lane < HEAD_DIM), 1.0, 0.0)
    s2 = pltpu.roll(sin_a, half, 1)
    both = lambda u: u + pltpu.roll(u, HEAD_DIM, 1)
    return both(c), both(-sin_a), both(s2)


def _rope_tables(compact, tm):
    lp = compact.shape[0]

    def body(t_ref, c_ref, s1_ref, s2_ref):
        c_ref[...], s1_ref[...], s2_ref[...] = _rope_coeffs(t_ref[...])

    row = pl.BlockSpec((tm, 128), lambda i: (i, 0))
    return pl.pallas_call(
        body,
        name="rope_tables",
        grid=(lp // tm,),
        in_specs=[row],
        out_specs=[row] * 3,
        out_shape=[jax.ShapeDtypeStruct((lp, 128), F32)] * 3,
        compiler_params=_params(),
    )(compact)


def _rope(t, c, s1, s2):
    return t * c + pltpu.roll(t, BLOCK - 8, 1) * s1 + pltpu.roll(t, 8, 1) * s2


def _rope_t(dt, c, s1, s2):
    return dt * c + pltpu.roll(dt * s1, 8, 1) + pltpu.roll(dt * s2, BLOCK - 8, 1)


def _build_h(x, tm, exch, small_piece, name):
    seq = x.shape[0]
    lp = BLOCK + seq
    nt = lp // tm
    n_sub = tm // BLOCK
    small_shape = exch.land_shapes[small_piece].shape

    def body(*refs):
        for j in range(n_sub):
            refs[n_sub][j * BLOCK:(j + 1) * BLOCK, :] = refs[j][...]

    def after(lands, *refs):
        h_ref, buf = refs[n_sub], refs[n_sub + 1]
        pltpu.sync_copy(lands[small_piece], buf)
        h_ref[0:LEAD_PAD, :] = jnp.zeros((LEAD_PAD, D_MODEL), F32)
        for d in range(N_DEV):
            h_ref[LEAD_PAD:BLOCK, d * 128:(d + 1) * 128] = buf[d, 0:N_META, :]

    tile = lambda i: (i + 1) % nt
    piece = lambda j: pl.BlockSpec((BLOCK, D_MODEL), lambda i: (jnp.maximum(tile(i) * n_sub + j - 1, 0), 0))
    (h,), lands = _call(
        body, exch,
        name=name,
        grid=(nt,),
        in_specs=[piece(j) for j in range(n_sub)],
        out_specs=[pl.BlockSpec((tm, D_MODEL), lambda i: (tile(i), 0))],
        out_shape=[jax.ShapeDtypeStruct((lp, D_MODEL), F32)],
        scratch_shapes=[pltpu.VMEM(small_shape, F32)],
        compiler_params=_params(),
        after=after,
    )(*([x] * n_sub))
    return h, lands


def _in_proj_fwd(h, g, w_in_t, rope, tm, name, exch=None):
    lp = h.shape[0]

    def body(h_ref, g_ref, w_ref, c_ref, s1_ref, s2_ref, a_ref, qkv_ref, bch_ref):
        a = _rms_fwd(h_ref[...], g_ref[...]).astype(BF)
        a_ref[...] = a
        proj = _dot_nt(a, w_ref[...])
        c, s1, s2 = c_ref[...], s1_ref[...], s2_ref[...]
        for j in range(5):
            t = _rope(proj[:, j * 128:(j + 1) * 128], c, s1, s2)
            qkv_ref[:, j * 128:(j + 1) * 128] = (t * SCALE if j < 4 else t).astype(BF)
        qkv_ref[:, 640:768] = proj[:, 640:768].astype(BF)
        bch_ref[...] = proj[:, 768:].astype(BF)

    row = lambda w: pl.BlockSpec((tm, w), lambda i: (i, 0))
    return _call(
        body, exch,
        name=name,
        grid=(lp // tm,),
        in_specs=[row(D_MODEL), _full((1, D_MODEL)), _full((IN_W, D_MODEL)), row(128), row(128), row(128)],
        out_specs=[row(D_MODEL), row(768), row(3 * CONV_W)],
        out_shape=[
            jax.ShapeDtypeStruct((lp, D_MODEL), BF),
            jax.ShapeDtypeStruct((lp, 768), BF),
            jax.ShapeDtypeStruct((lp, 3 * CONV_W), BF),
        ],
        compiler_params=_params(),
    )(h, g, w_in_t, *rope)


def _fold_masks(i):
    r = lax.broadcasted_iota(jnp.int32, (2 * BLOCK, BLOCK), 0) & (BLOCK - 1)
    c = lax.broadcasted_iota(jnp.int32, (2 * BLOCK, BLOCK), 1)
    tri = c > r
    ok = jnp.where(tri, (i - 1) * BLOCK + c, i * BLOCK + c) >= LEAD_PAD
    return tri, ok


def _kv_operand(x, kvh):
    lane = lax.broadcasted_iota(jnp.int32, x.shape, 1)
    zero = jnp.zeros_like(x)
    if kvh == 0:
        lo = jnp.where(lane < HEAD_DIM, x, zero)
        hi = pltpu.roll(lo, HEAD_DIM, 1)
    else:
        hi = jnp.where(lane >= HEAD_DIM, x, zero)
        lo = pltpu.roll(hi, HEAD_DIM, 1)
    return jnp.concatenate([lo, hi], axis=0)


def _split4(t, tri):
    zero = jnp.zeros_like(t[0])
    return jnp.concatenate(
        [jnp.where(tri, t[0], zero), jnp.where(tri, zero, t[0]), jnp.where(tri, t[1], zero), jnp.where(tri, zero, t[1])], axis=1)


def _sink_cols(sink_ref, kvh):
    first = lax.broadcasted_iota(jnp.int32, (2 * BLOCK, 1), 0) < BLOCK
    return [jnp.where(first, sink_ref[0, 4 * kvh + half], sink_ref[0, 4 * kvh + 2 + half]) for half in range(2)]


def _folded_exp(q2, k4, tri, ok, sks):
    s = _dot_nt(q2, k4)
    es, ss = [], []
    for half in range(2):
        s_h = s[:, 2 * half * BLOCK:2 * (half + 1) * BLOCK]
        sf = jnp.where(ok, jnp.where(tri, s_h[:, :BLOCK], s_h[:, BLOCK:]), NEG)
        m = jnp.maximum(jnp.max(sf, axis=-1, keepdims=True), sks[half])
        es.append(jnp.exp(sf - m))
        ss.append(jnp.exp(sks[half] - m))
    sums = _dot(jnp.concatenate(es, axis=0).astype(BF), jnp.ones((BLOCK, BLOCK), BF))
    invs = [1.0 / (sums[2 * half * BLOCK:2 * (half + 1) * BLOCK] + ss[half]) for half in range(2)]
    return es, ss, invs


def _attn_fwd(qkv, sink, name, exch=None):
    lp = qkv.shape[0]
    nb = lp // BLOCK

    def body(sink_ref, q_ref, kvc_ref, kvp_ref, o_ref):
        i = pl.program_id(0)
        tri, ok = _fold_masks(i)
        kvc, kvp = kvc_ref[...], kvp_ref[...]
        kk = jnp.concatenate([kvp[:, :128], kvc[:, :128]], axis=0)
        vv = jnp.concatenate([kvp[:, 128:], kvc[:, 128:]], axis=0)
        even = lax.broadcasted_iota(jnp.int32, (2 * BLOCK, 128), 1) < HEAD_DIM
        for kvh in range(2):
            q2 = jnp.concatenate([q_ref[:, 256 * kvh:256 * kvh + 128], q_ref[:, 256 * kvh + 128:256 * kvh + 256]], axis=0)
            es, _, invs = _folded_exp(q2, _kv_operand(kk, kvh), tri, ok, _sink_cols(sink_ref, kvh))
            out = _dot(_split4([e.astype(BF) for e in es], tri), _kv_operand(vv, kvh)) * jnp.where(even, invs[0], invs[1])
            o_ref[:, 256 * kvh:256 * kvh + 128] = out[:BLOCK].astype(BF)
            o_ref[:, 256 * kvh + 128:256 * kvh + 256] = out[BLOCK:].astype(BF)

    return _call(
        body, exch,
        name=name,
        grid=(nb,),
        in_specs=[
            pl.BlockSpec(memory_space=pltpu.SMEM),
            pl.BlockSpec((BLOCK, ATTN_W), lambda i: (i, 0)),
            pl.BlockSpec((BLOCK, 256), lambda i: (i, 2)),
            pl.BlockSpec((BLOCK, 256), lambda i: (jnp.maximum(i - 1, 0), 2)),
        ],
        out_specs=[pl.BlockSpec((BLOCK, ATTN_W), lambda i: (i, 0))],
        out_shape=[jax.ShapeDtypeStruct((lp, ATTN_W), BF)],
        compiler_params=_params(),
    )(sink, qkv, qkv, qkv)


def _mix_out_fwd(bch, y_attn, h, conv_w, g_a, g_c, w_out, g_post, tm, name, exch=None):
    lp = h.shape[0]

    def body(bch_ref, ya_ref, h_ref, cw_ref, ga_ref, gc_ref, w_ref, gp_ref, y_ref, z_ref, h2_ref, ext):
        i = pl.program_id(0)

        @pl.when(i == 0)
        def _():
            ext[0:8, :] = jnp.zeros((8, CONV_W), F32)

        b = bch_ref[:, 0:CONV_W].astype(F32)
        u = bch_ref[:, CONV_W:2 * CONV_W].astype(F32) * bch_ref[:, 2 * CONV_W:3 * CONV_W].astype(F32)
        ext[8:8 + tm, :] = u
        yc = cw_ref[0:1, :] * ext[6:6 + tm, :] + cw_ref[1:2, :] * ext[7:7 + tm, :] + cw_ref[2:3, :] * u
        ext[0:8, :] = u[tm - 8:tm, :]
        ya = _rms_fwd(ya_ref[...].astype(F32), ga_ref[...]).astype(BF)
        yb = _rms_fwd(b * yc, gc_ref[...]).astype(BF)
        y_ref[:, 0:ATTN_W] = ya
        y_ref[:, ATTN_W:] = yb
        z = _dot(ya, w_ref[0:ATTN_W, :]) + _dot(yb, w_ref[ATTN_W:, :])
        z_ref[...] = z
        h2_ref[...] = h_ref[...] + _rms_fwd(z, gp_ref[...])

    row = lambda w: pl.BlockSpec((tm, w), lambda i: (i, 0))
    return _call(
        body, exch,
        name=name,
        grid=(lp // tm,),
        in_specs=[
            row(3 * CONV_W), row(ATTN_W), row(D_MODEL), _full((8, CONV_W)), _full((1, ATTN_W)), _full((1, CONV_W)),
            _full((D_MODEL, D_MODEL)), _full((1, D_MODEL)),
        ],
        out_specs=[row(D_MODEL), row(D_MODEL), row(D_MODEL)],
        out_shape=[
            jax.ShapeDtypeStruct((lp, D_MODEL), BF),
            jax.ShapeDtypeStruct((lp, D_MODEL), F32),
            jax.ShapeDtypeStruct((lp, D_MODEL), F32),
        ],
        scratch_shapes=[pltpu.VMEM((tm + 8, CONV_W), F32)],
        compiler_params=_params(),
    )(bch, y_attn, h, conv_w, g_a, g_c, w_out, g_post)


def _mlp_fwd(h2, g_pre, w_up_t, w_down, g_post, tm, name, exch=None, target=None):
    lp = h2.shape[0]
    sub = math.gcd(tm, BLOCK)
    n_sub, lead = tm // sub, BLOCK // sub
    n_t = n_sub if target is not None else 0

    def body(*refs):
        h_ref, gp_ref, wu_ref, wd_ref, gq_ref = refs[:5]
        t_refs = refs[5:5 + n_t]
        a_ref, up_ref, f_ref, last_ref = refs[5 + n_t:9 + n_t]
        h = h_ref[...]
        a = _rms_fwd(h, gp_ref[...]).astype(BF)
        a_ref[...] = a
        up = _dot_nt(a, wu_ref[...])
        up_ref[...] = up.astype(BF)
        act = jnp.square(jnp.maximum(up, 0.0)).astype(BF)
        f = _dot(act, wd_ref[...])
        f_ref[...] = f
        h3 = h + _rms_fwd(f, gq_ref[...])
        if target is None:
            last_ref[...] = h3
            return
        ls_ref = refs[9 + n_t]
        i = pl.program_id(0)

        @pl.when(i == 0)
        def _():
            ls_ref[...] = jnp.zeros((8, 128), F32)

        sq = jnp.zeros((1, 1), F32)
        for j in range(n_sub):
            on_tokens = i * n_sub + j >= lead
            d = jnp.where(on_tokens, h3[j * sub:(j + 1) * sub] - t_refs[j][...], 0.0)
            last_ref[j * sub:(j + 1) * sub, :] = d * (1.0 / D_MODEL)
            sq = sq + jnp.sum(d * d)
        ls_ref[...] += sq

    row = lambda w: pl.BlockSpec((tm, w), lambda i: (i, 0))
    piece = lambda j: pl.BlockSpec((sub, D_MODEL), lambda i: (jnp.maximum(i * n_sub + j - lead, 0), 0))
    out_specs = [row(D_MODEL), row(D_FF), row(D_MODEL), row(D_MODEL)]
    out_shape = [
        jax.ShapeDtypeStruct((lp, D_MODEL), BF),
        jax.ShapeDtypeStruct((lp, D_FF), BF),
        jax.ShapeDtypeStruct((lp, D_MODEL), F32),
        jax.ShapeDtypeStruct((lp, D_MODEL), F32),
    ]
    if target is not None:
        out_specs.append(_full_out((8, 128)))
        out_shape.append(jax.ShapeDtypeStruct((8, 128), F32))
    return _call(
        body, exch,
        name=name,
        grid=(lp // tm,),
        in_specs=[row(D_MODEL), _full((1, D_MODEL)), _full((D_FF, D_MODEL)), _full((D_FF, D_MODEL)), _full((1, D_MODEL))]
        + [piece(j) for j in range(n_t)],
        out_specs=out_specs,
        out_shape=out_shape,
        compiler_params=_params(),
    )(h2, g_pre, w_up_t, w_down, g_post, *([target] * n_t))


def _mlp_bwd_dx(dh3, f, up, h2, w_down, w_up_t, g_post, g_pre, tm, name, exch=None):
    lp = h2.shape[0]

    def body(dh3_ref, f_ref, up_ref, h2_ref, wd_ref, wu_ref, gq_ref, gp_ref, df_ref, dup_ref, dh2_ref, dg_ref):
        i = pl.program_id(0)

        @pl.when(i == 0)
        def _():
            dg_ref[...] = jnp.zeros((8, D_MODEL), F32)

        dh3 = dh3_ref[...]
        df, dgq = _rms_bwd(f_ref[...], gq_ref[...], dh3)
        dg_ref[ROW_MLP_POST:ROW_MLP_POST + 1, :] += dgq
        df = df.astype(BF)
        df_ref[...] = df
        dact = _dot_nt(df, wd_ref[...])
        dup = (dact * (2.0 * jnp.maximum(up_ref[...].astype(F32), 0.0))).astype(BF)
        dup_ref[...] = dup
        da = _dot(dup, wu_ref[...])
        dh, dgp = _rms_bwd(h2_ref[...], gp_ref[...], da)
        dg_ref[ROW_MLP_PRE:ROW_MLP_PRE + 1, :] += dgp
        dh2_ref[...] = dh3 + dh

    row = lambda w: pl.BlockSpec((tm, w), lambda i: (i, 0))
    return _call(
        body, exch,
        name=name,
        grid=(lp // tm,),
        in_specs=[
            row(D_MODEL), row(D_MODEL), row(D_FF), row(D_MODEL), _full((D_FF, D_MODEL)), _full((D_FF, D_MODEL)),
            _full((1, D_MODEL)), _full((1, D_MODEL)),
        ],
        out_specs=[row(D_MODEL), row(D_FF), row(D_MODEL), _full_out((8, D_MODEL))],
        out_shape=[
            jax.ShapeDtypeStruct((lp, D_MODEL), BF),
            jax.ShapeDtypeStruct((lp, D_FF), BF),
            jax.ShapeDtypeStruct((lp, D_MODEL), F32),
            jax.ShapeDtypeStruct((8, D_MODEL), F32),
        ],
        compiler_params=_params(),
    )(dh3, f, up, h2, w_down, w_up_t, g_post, g_pre)


def _mlp_bwd_dw(up, df, dup, a2, tm, name):
    lp = up.shape[0]
    nt = lp // tm
    nj = D_FF // D_MODEL

    def body(up_ref, df_ref, dup_ref, a_ref, dwd_ref, dwu_ref, accd, accu):
        i = pl.program_id(1)

        @pl.when(i == 0)
        def _():
            accd[...] = jnp.zeros_like(accd)
            accu[...] = jnp.zeros_like(accu)

        act = jnp.square(jnp.maximum(up_ref[...].astype(F32), 0.0)).astype(BF)
        accd[...] += _dot_tn(act, df_ref[...])
        accu[...] += _dot_tn(dup_ref[...], a_ref[...])

        @pl.when(i == nt - 1)
        def _():
            dwd_ref[...] = accd[...].astype(BF)
            dwu_ref[...] = accu[...].astype(BF)

    return pl.pallas_call(
        body,
        name=name,
        grid=(nj, nt),
        in_specs=[
            pl.BlockSpec((tm, D_MODEL), lambda j, i: (i, j)),
            pl.BlockSpec((tm, D_MODEL), lambda j, i: (i, 0)),
            pl.BlockSpec((tm, D_MODEL), lambda j, i: (i, j)),
            pl.BlockSpec((tm, D_MODEL), lambda j, i: (i, 0)),
        ],
        out_specs=[pl.BlockSpec((D_MODEL, D_MODEL), lambda j, i: (j, 0)), pl.BlockSpec((D_MODEL, D_MODEL), lambda j, i: (j, 0))],
        out_shape=[jax.ShapeDtypeStruct((D_FF, D_MODEL), BF), jax.ShapeDtypeStruct((D_FF, D_MODEL), BF)],
        scratch_shapes=[pltpu.VMEM((D_MODEL, D_MODEL), F32), pltpu.VMEM((D_MODEL, D_MODEL), F32)],
        compiler_params=_params(("arbitrary", "arbitrary")),
    )(up, df, dup, a2)


def _mix_out_bwd(dh2, z, y_attn, bch, w_out, g_post, g_a, g_c, conv_w, tm, name, exch=None):
    lp = dh2.shape[0]
    nt = lp // tm
    halo = 16

    def body(dh2_ref, z_ref, ya_ref, bch_ref, halo_ref, w_ref, gp_ref, ga_ref, gc_ref, cw_ref,
             dz_ref, dya_ref, dbch_ref, dg_ref, ext, ext_u):
        i = pl.program_id(0)
        dcw_ref = dg_ref.at[ROW_CONV:ROW_CONV + 3, 0:CONV_W]

        @pl.when(i == 0)
        def _():
            ext[tm:tm + 8, :] = jnp.zeros((8, CONV_W), F32)
            dg_ref[...] = jnp.zeros((8, D_MODEL), F32)

        dz, dgp = _rms_bwd(z_ref[...], gp_ref[...], dh2_ref[...])
        dg_ref[ROW_MIX_POST:ROW_MIX_POST + 1, :] += dgp
        dz = dz.astype(BF)
        dz_ref[...] = dz
        dya_n = _dot_nt(dz, w_ref[0:ATTN_W, :])
        dyb_n = _dot_nt(dz, w_ref[ATTN_W:, :])
        dya, dga = _rms_bwd(ya_ref[...].astype(F32), ga_ref[...], dya_n)
        dg_ref[ROW_GROUP_G:ROW_GROUP_G + 1, 0:ATTN_W] += dga
        dya_ref[...] = dya
        b = bch_ref[:, 0:CONV_W].astype(F32)
        c = bch_ref[:, CONV_W:2 * CONV_W].astype(F32)
        hc = bch_ref[:, 2 * CONV_W:3 * CONV_W].astype(F32)
        u = c * hc
        u_before = halo_ref[:, CONV_W:2 * CONV_W].astype(F32) * halo_ref[:, 2 * CONV_W:3 * CONV_W].astype(F32)
        ext_u[0:halo, :] = jnp.where(i < nt - 1, u_before, 0.0)
        ext_u[halo:halo + tm, :] = u
        yc_v = (cw_ref[0:1, :] * ext_u[halo - 2:halo - 2 + tm, :] + cw_ref[1:2, :] * ext_u[halo - 1:halo - 1 + tm, :]
                + cw_ref[2:3, :] * u)
        dyconv, dgc = _rms_bwd(b * yc_v, gc_ref[...], dyb_n)
        dg_ref[ROW_GROUP_G:ROW_GROUP_G + 1, ATTN_W:] += dgc
        dbch_ref[:, 0:CONV_W] = (dyconv * yc_v).astype(BF)
        dyc = dyconv * b
        ext[0:tm, :] = dyc
        d1 = ext[1:1 + tm, :]
        d2 = ext[2:2 + tm, :]
        du = cw_ref[2:3, :] * dyc + cw_ref[1:2, :] * d1 + cw_ref[0:1, :] * d2
        ext[tm:tm + 8, :] = dyc[0:8, :]
        dbch_ref[:, CONV_W:2 * CONV_W] = (du * hc).astype(BF)
        dbch_ref[:, 2 * CONV_W:3 * CONV_W] = (du * c).astype(BF)
        dcw_ref[0:1, :] += jnp.sum(u * d2, axis=0, keepdims=True)
        dcw_ref[1:2, :] += jnp.sum(u * d1, axis=0, keepdims=True)
        dcw_ref[2:3, :] += jnp.sum(u * dyc, axis=0, keepdims=True)

    row = lambda w: pl.BlockSpec((tm, w), lambda i: (nt - 1 - i, 0))
    before = pl.BlockSpec((halo, 3 * CONV_W), lambda i: (jnp.maximum((nt - 1 - i) * (tm // halo) - 1, 0), 0))
    return _call(
        body, exch,
        name=name,
        grid=(nt,),
        in_specs=[
            row(D_MODEL), row(D_MODEL), row(ATTN_W), row(3 * CONV_W), before, _full((D_MODEL, D_MODEL)),
            _full((1, D_MODEL)), _full((1, ATTN_W)), _full((1, CONV_W)), _full((8, CONV_W)),
        ],
        out_specs=[row(D_MODEL), row(ATTN_W), row(3 * CONV_W), _full_out((8, D_MODEL))],
        out_shape=[
            jax.ShapeDtypeStruct((lp, D_MODEL), BF),
            jax.ShapeDtypeStruct((lp, ATTN_W), F32),
            jax.ShapeDtypeStruct((lp, 3 * CONV_W), BF),
            jax.ShapeDtypeStruct((8, D_MODEL), F32),
        ],
        scratch_shapes=[pltpu.VMEM((tm + 8, CONV_W), F32), pltpu.VMEM((tm + halo, CONV_W), F32)],
        compiler_params=_params(),
    )(dh2, z, y_attn, bch, bch, w_out, g_post, g_a, g_c, conv_w)


def _attn_bwd(qkv, o, do, sink, rope, name, exch=None):
    lp = qkv.shape[0]
    nb = lp // BLOCK

    def body(sink_ref, q_ref, kvc_ref, kvp_ref, o_ref, do_ref, cq_ref, s1q_ref, s2q_ref, ck_ref, s1k_ref, s2k_ref,
             dq_ref, dkv_ref, dsink_ref, carry):
        i = pl.program_id(0)

        @pl.when(i == 0)
        def _():
            carry[...] = jnp.zeros_like(carry)
            dsink_ref[...] = jnp.zeros((8, 128), F32)

        def finish(tot):
            dk = _rope_t(tot[:, :128], ck_ref[...], s1k_ref[...], s2k_ref[...])
            dkv_ref[:, 0:128] = dk.astype(BF)
            dkv_ref[:, 128:256] = tot[:, 128:].astype(BF)

        @pl.when(i < nb)
        def _():
            tri, ok = _fold_masks(i)
            kvc, kvp = kvc_ref[...], kvp_ref[...]
            kk = jnp.concatenate([kvp[:, :128], kvc[:, :128]], axis=0)
            vv = jnp.concatenate([kvp[:, 128:], kvc[:, 128:]], axis=0)
            lane = lax.broadcasted_iota(jnp.int32, (BLOCK, 128), 1)
            lane2 = lax.broadcasted_iota(jnp.int32, (2 * BLOCK, 128), 1)
            first = lax.broadcasted_iota(jnp.int32, (2 * BLOCK, 1), 0) < BLOCK
            row_s = lax.broadcasted_iota(jnp.int32, (8, 128), 0)
            lane_s = jnp.where(row_s == ROW_SINK, lax.broadcasted_iota(jnp.int32, (8, 128), 1), -1)
            rope_q = (cq_ref[...], s1q_ref[...], s2q_ref[...])
            dsink = jnp.zeros((8, 128), F32)
            folded = []
            for kvh in range(2):
                c0 = 256 * kvh
                q2 = jnp.concatenate([q_ref[:, c0:c0 + 128], q_ref[:, c0 + 128:c0 + 256]], axis=0)
                do2 = jnp.concatenate([do_ref[:, c0:c0 + 128], do_ref[:, c0 + 128:c0 + 256]], axis=0)
                o2 = jnp.concatenate([o_ref[:, c0:c0 + 128], o_ref[:, c0 + 128:c0 + 256]], axis=0).astype(F32)
                k4, v4 = _kv_operand(kk, kvh), _kv_operand(vv, kvh)
                es, ss, invs = _folded_exp(q2, k4, tri, ok, _sink_cols(sink_ref, kvh))
                prod = do2 * o2
                dob = do2.astype(BF)
                dp = _dot_nt(dob, v4)
                ds, pb = [], []
                for half in range(2):
                    p, ps = es[half] * invs[half], ss[half] * invs[half][:, 0:1]
                    sel = (lane2 < HEAD_DIM) if half == 0 else (lane2 >= HEAD_DIM)
                    delta = jnp.sum(jnp.where(sel, prod, 0.0), axis=-1, keepdims=True)
                    dp_h = dp[:, 2 * half * BLOCK:2 * (half + 1) * BLOCK]
                    ds.append((p * (jnp.where(tri, dp_h[:, :BLOCK], dp_h[:, BLOCK:]) - delta)).astype(BF))
                    pb.append(p.astype(BF))
                    t = ps * delta
                    for jj in range(2):
                        part = -jnp.sum(jnp.where(first if jj == 0 else ~first, t, 0.0))
                        dsink = dsink + jnp.where(lane_s == 4 * kvh + 2 * jj + half, part, 0.0)
                ds4, p4 = _split4(ds, tri), _split4(pb, tri)
                dq2 = _dot(ds4, k4) * SCALE
                dq_ref[:, c0:c0 + 128] = _rope_t(dq2[:BLOCK], *rope_q).astype(BF)
                dq_ref[:, c0 + 128:c0 + 256] = _rope_t(dq2[BLOCK:], *rope_q).astype(BF)
                rk, rv = _dot_tn(ds4, q2), _dot_tn(p4, dob)
                own = (lane < HEAD_DIM) if kvh == 0 else (lane >= HEAD_DIM)
                group = []
                for r in (rk, rv):
                    for blk in range(2):
                        t = jnp.where(lane < HEAD_DIM, r[blk * BLOCK:(blk + 1) * BLOCK], r[(2 + blk) * BLOCK:(3 + blk) * BLOCK])
                        group.append(jnp.where(own, t + pltpu.roll(t, HEAD_DIM, 1), 0.0))
                folded.append(group)
            dsink_ref[...] += dsink
            dk_p, dk_c, dv_p, dv_c = [folded[0][t] + folded[1][t] for t in range(4)]
            finish(carry[...] + jnp.concatenate([dk_p, dv_p], axis=1))
            carry[...] = jnp.concatenate([dk_c, dv_c], axis=1)

        @pl.when(i == nb)
        def _():
            finish(carry[...])

    qi = lambda i: jnp.minimum(i, nb - 1)
    ki = lambda i: jnp.maximum(i - 1, 0)
    tab_q = pl.BlockSpec((BLOCK, 128), lambda i: (qi(i), 0))
    tab_k = pl.BlockSpec((BLOCK, 128), lambda i: (ki(i), 0))
    return _call(
        body, exch,
        name=name,
        grid=(nb + 1,),
        in_specs=[
            pl.BlockSpec(memory_space=pltpu.SMEM),
            pl.BlockSpec((BLOCK, ATTN_W), lambda i: (qi(i), 0)),
            pl.BlockSpec((BLOCK, 256), lambda i: (qi(i), 2)),
            pl.BlockSpec((BLOCK, 256), lambda i: (jnp.maximum(qi(i) - 1, 0), 2)),
            pl.BlockSpec((BLOCK, ATTN_W), lambda i: (qi(i), 0)),
            pl.BlockSpec((BLOCK, ATTN_W), lambda i: (qi(i), 0)),
            tab_q, tab_q, tab_q, tab_k, tab_k, tab_k,
        ],
        out_specs=[
            pl.BlockSpec((BLOCK, ATTN_W), lambda i: (qi(i), 0)),
            pl.BlockSpec((BLOCK, 256), lambda i: (ki(i), 0)),
            pl.BlockSpec((8, 128), lambda i: (0, 0)),
        ],
        out_shape=[
            jax.ShapeDtypeStruct((lp, ATTN_W), BF),
            jax.ShapeDtypeStruct((lp, 256), BF),
            jax.ShapeDtypeStruct((8, 128), F32),
        ],
        scratch_shapes=[pltpu.VMEM((BLOCK, 256), F32)],
        compiler_params=_params(),
    )(sink, qkv, qkv, qkv, o, do, *rope, *rope)


def _in_proj_bwd_dx(dq, dkv, dbch, w_in_t, h, dh2, g, tm, name, exch=None):
    lp = h.shape[0]

    def body(dq_ref, dkv_ref, dbch_ref, w_ref, h_ref, dh2_ref, g_ref, dh_ref, dg_ref):
        i = pl.program_id(0)

        @pl.when(i == 0)
        def _():
            dg_ref[...] = jnp.zeros((8, D_MODEL), F32)

        da = _dot(dq_ref[...], w_ref[0:512, :]) + _dot(dkv_ref[...], w_ref[512:768, :]) + _dot(dbch_ref[...], w_ref[768:, :])
        dh, dg = _rms_bwd(h_ref[...], g_ref[...], da)
        dg_ref[ROW_MIX_PRE:ROW_MIX_PRE + 1, :] += dg
        dh_ref[...] = dh2_ref[...] + dh

    row = lambda w: pl.BlockSpec((tm, w), lambda i: (i, 0))
    return _call(
        body, exch,
        name=name,
        grid=(lp // tm,),
        in_specs=[row(ATTN_W), row(256), row(3 * CONV_W), _full((IN_W, D_MODEL)), row(D_MODEL), row(D_MODEL), _full((1, D_MODEL))],
        out_specs=[row(D_MODEL), _full_out((8, D_MODEL))],
        out_shape=[jax.ShapeDtypeStruct((lp, D_MODEL), F32), jax.ShapeDtypeStruct((8, D_MODEL), F32)],
        compiler_params=_params(),
    )(dq, dkv, dbch, w_in_t, h, dh2, g)


def _mix_bwd_dw(dq, dkv, dbch, a, y, dz, tm, name, exch=None):
    lp = a.shape[0]
    nt = lp // tm

    def body(dq_ref, dkv_ref, dbch_ref, a_ref, y_ref, dz_ref, dwi_ref, dwo_ref, acci, acco):
        i = pl.program_id(0)

        @pl.when(i == 0)
        def _():
            acci[...] = jnp.zeros_like(acci)
            acco[...] = jnp.zeros_like(acco)

        a_v = a_ref[...]
        acci[0:512, :] += _dot_tn(dq_ref[...], a_v)
        acci[512:768, :] += _dot_tn(dkv_ref[...], a_v)
        acci[768:, :] += _dot_tn(dbch_ref[...], a_v)
        acco[...] += _dot_tn(y_ref[...], dz_ref[...])

        @pl.when(i == nt - 1)
        def _():
            dwi_ref[...] = acci[...].astype(BF)
            dwo_ref[...] = acco[...].astype(BF)

    row = lambda w: pl.BlockSpec((tm, w), lambda i: (i, 0))
    return _call(
        body, exch,
        name=name,
        grid=(nt,),
        in_specs=[row(ATTN_W), row(256), row(3 * CONV_W), row(D_MODEL), row(D_MODEL), row(D_MODEL)],
        out_specs=[_full_out((IN_W, D_MODEL)), _full_out((D_MODEL, D_MODEL))],
        out_shape=[jax.ShapeDtypeStruct((IN_W, D_MODEL), BF), jax.ShapeDtypeStruct((D_MODEL, D_MODEL), BF)],
        scratch_shapes=[pltpu.VMEM((IN_W, D_MODEL), F32), pltpu.VMEM((D_MODEL, D_MODEL), F32)],
        compiler_params=_params(),
    )(dq, dkv, dbch, a, y, dz)


def _mesh_place():
    x, y, c = lax.axis_index("x"), lax.axis_index("y"), lax.axis_index("c")
    return x, y, c, 4 * x + 2 * y + c


def _peer(x, y, c, k):
    px = 1 - x if k & 4 else x
    py = 1 - y if k & 2 else y
    pc = 1 - c if k & 1 else c
    return (px, py, pc), 4 * px + 2 * py + pc


SIBLING = 1
SAME_CORE = (2, 4, 6)
OTHER_CORE = (3, 5, 7)


class _Exchange:
    def __init__(self, pieces):
        self.srcs = [s for s, _ in pieces]
        self.to_all = [g for _, g in pieces]
        self.n = len(pieces)
        self.land_shapes = [
            jax.ShapeDtypeStruct((N_DEV,) + (s.shape if g else s.shape[1:]), s.dtype) for s, g in pieces]
        self.sem_shapes = [pltpu.SemaphoreType.DMA((self.n, N_DEV - 1)), pltpu.SemaphoreType.DMA((self.n, N_DEV - 1)),
                           pltpu.SemaphoreType.DMA((self.n,))]
        self.forwards = any(self.to_all)

    def _ops(self, srcs, lands, sems):
        send_sems, recv_sems, local_sems = sems
        x, y, c, me = _mesh_place()

        def remote(p, k, src, slot, to):
            return pltpu.make_async_remote_copy(
                src_ref=src, dst_ref=lands[p].at[slot], send_sem=send_sems.at[p, k - 1], recv_sem=recv_sems.at[p, k - 1],
                device_id=to, device_id_type=MESH)

        def own(p):
            return pltpu.make_async_copy(srcs[p] if self.to_all[p] else srcs[p].at[me], lands[p].at[me], local_sems.at[p])

        def direct(p, k):
            peer, pidx = _peer(x, y, c, k)
            return remote(p, k, srcs[p] if self.to_all[p] else srcs[p].at[pidx], me, peer)

        def forward(p, k):
            sibling, _ = _peer(x, y, c, SIBLING)
            _, origin = _peer(x, y, c, k ^ SIBLING)
            return remote(p, k, lands[p].at[origin], origin, sibling)

        def arrival(p, k):
            peer, pidx = _peer(x, y, c, k)
            return remote(p, k, lands[p].at[pidx], pidx, peer)

        return own, direct, forward, arrival

    def start(self, srcs, lands, sems):
        own, direct, _, _ = self._ops(srcs, lands, sems)
        for p in range(self.n):
            own(p).start()
            for k in ((SIBLING,) + SAME_CORE) if self.to_all[p] else range(1, N_DEV):
                direct(p, k).start()

    def forward(self, srcs, lands, sems):
        _, _, forward, arrival = self._ops(srcs, lands, sems)
        for p in range(self.n):
            if self.to_all[p]:
                for k in SAME_CORE:
                    arrival(p, k).wait_recv()
                    forward(p, k ^ SIBLING).start()

    def finish(self, srcs, lands, sems):
        own, direct, forward, arrival = self._ops(srcs, lands, sems)
        for p in range(self.n):
            for k in ((SIBLING,) + OTHER_CORE) if self.to_all[p] else range(1, N_DEV):
                arrival(p, k).wait_recv()
        for p in range(self.n):
            for k in range(1, N_DEV):
                (forward(p, k) if self.to_all[p] and k in OTHER_CORE else direct(p, k)).wait_send()
            own(p).wait()


def _call(body, exch, *, name, grid, in_specs, out_specs, out_shape, scratch_shapes=(), compiler_params, after=None):
    if exch is None:
        return pl.pallas_call(body, name=name, grid=grid, in_specs=in_specs, out_specs=out_specs, out_shape=out_shape,
                              scratch_shapes=scratch_shapes, compiler_params=compiler_params)
    n_in, n_out, n_scr, n_x = len(in_specs), len(out_shape), len(scratch_shapes), exch.n
    steps = math.prod(grid)

    def carrying(*refs):
        a, b, c, d, e = n_in, n_in + n_x, n_in + n_x + n_out, n_in + 2 * n_x + n_out, n_in + 2 * n_x + n_out + n_scr
        ins, srcs, outs, lands, scr, sems = refs[:a], refs[a:b], refs[b:c], refs[c:d], refs[d:e], refs[e:]
        step = functools.reduce(lambda acc, t: acc * grid[t] + pl.program_id(t), range(len(grid)), 0)

        @pl.when(step == 0)
        def _():
            exch.start(srcs, lands, sems)

        body(*ins, *outs, *scr)

        if exch.forwards:
            @pl.when(step == max(0, steps - 1 - (steps + 7) // 8))
            def _():
                exch.forward(srcs, lands, sems)

        @pl.when(step == steps - 1)
        def _():
            exch.finish(srcs, lands, sems)
            if after is not None:
                after(lands, *ins, *outs, *scr)

    hbm = pl.BlockSpec(memory_space=pl.ANY)
    call = pl.pallas_call(
        carrying, name=name, grid=grid, in_specs=list(in_specs) + [hbm] * n_x, out_specs=list(out_specs) + [hbm] * n_x,
        out_shape=list(out_shape) + exch.land_shapes, scratch_shapes=list(scratch_shapes) + exch.sem_shapes,
        compiler_params=compiler_params)

    def run(*args):
        res = call(*args, *exch.srcs)
        return list(res[:n_out]), list(res[n_out:])

    return run


def _sum_small(part):
    def body(part_ref, out_ref, land, send_sems, recv_sems):
        x, y, c, me = _mesh_place()
        land[me] = part_ref[...]
        sent = []
        for k in range(1, N_DEV):
            peer, _ = _peer(x, y, c, k)
            cp = pltpu.make_async_remote_copy(
                src_ref=part_ref, dst_ref=land.at[me], send_sem=send_sems.at[k - 1], recv_sem=recv_sems.at[k - 1],
                device_id=peer, device_id_type=MESH)
            cp.start()
            sent.append(cp)
        for k in range(1, N_DEV):
            peer, pidx = _peer(x, y, c, k)
            pltpu.make_async_remote_copy(
                src_ref=part_ref, dst_ref=land.at[pidx], send_sem=send_sems.at[k - 1], recv_sem=recv_sems.at[k - 1],
                device_id=peer, device_id_type=MESH).wait_recv()
        for cp in sent:
            cp.wait_send()
        acc = land[0]
        for d in range(1, N_DEV):
            acc = acc + land[d]
        out_ref[...] = acc

    vmem = pl.BlockSpec(memory_space=pltpu.VMEM)
    return pl.pallas_call(
        body,
        name="sum_small",
        in_specs=[vmem],
        out_specs=vmem,
        out_shape=jax.ShapeDtypeStruct(part.shape, F32),
        scratch_shapes=[pltpu.VMEM((N_DEV,) + part.shape, F32), pltpu.SemaphoreType.DMA((N_DEV - 1,)),
                        pltpu.SemaphoreType.DMA((N_DEV - 1,))],
    )(part)


def _adamw(w, g, m, v):
    m = ADAM_B1 * m + (1.0 - ADAM_B1) * g
    v = ADAM_B2 * v + (1.0 - ADAM_B2) * jnp.square(g)
    m_hat = m / (1.0 - ADAM_B1 ** ADAM_STEP)
    v_hat = v / (1.0 - ADAM_B2 ** ADAM_STEP)
    delta = -ADAM_LR * (m_hat / (jnp.sqrt(v_hat) + ADAM_EPS) + ADAM_WD * w)
    return delta, m, v


def _landed_specs(tr, wd):
    return [pl.BlockSpec((N_DEV, tr, wd), lambda l, i, ll=ll: (0, jnp.where(l == ll, i, 0), 0)) for ll in range(DEPTH)]


def _device_sum(r_ref):
    acc = r_ref[0].astype(F32)
    for d in range(1, N_DEV):
        acc = acc + r_ref[d].astype(F32)
    return acc


def _sum_parts(recv, tr, name):
    _, r, wd = recv[0].shape

    def body(*refs):
        g_ref = refs[DEPTH]
        for ll in range(DEPTH):
            @pl.when(pl.program_id(0) == ll)
            def _(ll=ll):
                g_ref[0] = _device_sum(refs[ll])

    return pl.pallas_call(
        body,
        name=name,
        grid=(DEPTH, r // tr),
        in_specs=_landed_specs(tr, wd),
        out_specs=pl.BlockSpec((1, tr, wd), lambda l, i: (l, i, 0)),
        out_shape=jax.ShapeDtypeStruct((DEPTH, r, wd), F32),
        compiler_params=_params(("arbitrary", "arbitrary")),
    )(*recv)


def _sum_adamw(recv, w, m, v, tr, name):
    _, r, wd = recv[0].shape

    def body(*refs):
        w_ref, m_ref, v_ref, g_ref, d_ref, mo_ref, vo_ref = refs[DEPTH:]
        for ll in range(DEPTH):
            @pl.when(pl.program_id(0) == ll)
            def _(ll=ll):
                g = _device_sum(refs[ll])
                g_ref[0] = g
                d_ref[0], mo_ref[0], vo_ref[0] = _adamw(w_ref[0], g, m_ref[0], v_ref[0])

    blk = pl.BlockSpec((1, tr, wd), lambda l, i: (l, i, 0))
    shape = jax.ShapeDtypeStruct((DEPTH, r, wd), F32)
    return pl.pallas_call(
        body,
        name=name,
        grid=(DEPTH, r // tr),
        in_specs=_landed_specs(tr, wd) + [blk, blk, blk],
        out_specs=[blk] * 4,
        out_shape=[shape] * 4,
        compiler_params=_params(("arbitrary", "arbitrary")),
    )(*recv, w, m, v)


def _adamw_rows(w, g, m, v, tr, name):
    _, r, wd = w.shape

    def body(w_ref, g_ref, m_ref, v_ref, d_ref, mo_ref, vo_ref):
        d_ref[0], mo_ref[0], vo_ref[0] = _adamw(w_ref[0], g_ref[0], m_ref[0], v_ref[0])

    blk = pl.BlockSpec((1, tr, wd), lambda l, i: (l, i, 0))
    shape = jax.ShapeDtypeStruct(w.shape, F32)
    return pl.pallas_call(
        body,
        name=name,
        grid=(DEPTH, r // tr),
        in_specs=[blk] * 4,
        out_specs=[blk] * 3,
        out_shape=[shape] * 3,
        compiler_params=_params(("arbitrary", "arbitrary")),
    )(w, g, m, v)


def _adamw_small(ws, gs, ms, vs):
    n = len(ws)

    def body(*refs):
        w_r, g_r, m_r, v_r = refs[:n], refs[n:2 * n], refs[2 * n:3 * n], refs[3 * n:4 * n]
        d_o, m_o, v_o = refs[4 * n:5 * n], refs[5 * n:6 * n], refs[6 * n:7 * n]
        for t in range(n):
            d_o[t][...], m_o[t][...], v_o[t][...] = _adamw(w_r[t][...], g_r[t][...], m_r[t][...], v_r[t][...])

    vmem = pl.BlockSpec(memory_space=pltpu.VMEM)
    shapes = [jax.ShapeDtypeStruct(w.shape, F32) for w in ws]
    outs = pl.pallas_call(
        body,
        name="adamw_small",
        in_specs=[vmem] * (4 * n),
        out_specs=[vmem] * (3 * n),
        out_shape=shapes * 3,
    )(*ws, *gs, *ms, *vs)
    return outs[:n], outs[n:2 * n], outs[2 * n:]


def kernel(x, meta_tokens, mix_pre_g, w_in, conv_w, sinks, attn_out_g, conv_out_g, w_out, mix_post_g, mlp_pre_g, w_up, w_down, mlp_post_g, loss_target, m_meta_tokens, m_mix_pre_g, m_w_in, m_conv_w, m_sinks, m_attn_out_g, m_conv_out_g, m_w_out, m_mix_post_g, m_mlp_pre_g, m_w_up, m_w_down, m_mlp_post_g, v_meta_tokens, v_mix_pre_g, v_w_in, v_conv_w, v_sinks, v_attn_out_g, v_conv_out_g, v_w_out, v_mix_post_g, v_mlp_pre_g, v_w_up, v_w_down, v_mlp_post_g):
    seq = x.shape[1]
    lp = BLOCK + seq
    tm = _row_tile(lp)
    tm_mlp = _row_tile(lp, (320, 256, 128))
    tm_dw_mlp = _row_tile(lp, (1664, 1040, 640, 384, 256, 128))
    tm_dw_mix = _row_tile(lp, (832, 640, 384, 256, 128))
    me = 4 * lax.axis_index("x") + 2 * lax.axis_index("y") + lax.axis_index("c")
    cshard = CONV_W // N_DEV
    mshard = D_MODEL // N_DEV

    gather_with = {
        ("in_proj_fwd", 0): [("out", 0), ("in", 1)], ("attn_fwd", 0): [("up", 0)], ("mix_out_fwd", 0): [("down", 0)],
        ("mlp_fwd", 0): [("out", 1), ("up", 1), ("down", 1)],
    }
    scatter_with = {
        ("attn_bwd", 1): [("down", 1)], ("mlp_bwd_dx", 0): [("up", 1), ("in", 1), ("out", 1)],
        ("attn_bwd", 0): [("down", 0)], ("mix_bwd_dw", 0): [("up", 0)], ("in_proj_bwd_dx", 0): [("in", 0), ("out", 0)],
    }
    shard = {"in": jnp.swapaxes(w_in, 1, 2).astype(BF), "out": w_out.astype(BF),
             "up": jnp.swapaxes(w_up, 1, 2).astype(BF), "down": w_down.astype(BF)}
    weight = {}
    grad = {}
    landed = {}

    def run(fn, kind, l, *args):
        key, name = (kind, l), f"{kind}_{l}"
        if key in gather_with:
            blocks = gather_with[key]
            outs, lands = fn(*args, name, _Exchange([(shard[n][k], True) for n, k in blocks]))
            for b, land in zip(blocks, lands):
                weight[b] = land.reshape(-1, D_MODEL)
            return outs
        if key in scatter_with:
            blocks = scatter_with[key]
            outs, lands = fn(*args, name, _Exchange([(grad[b].reshape(N_DEV, -1, D_MODEL), False) for b in blocks]))
            landed.update(zip(blocks, lands))
            return outs
        return fn(*args, name)

    small = jnp.zeros((24, 128), F32)
    small = small.at[0:N_META, :].set(meta_tokens)
    small = small.at[N_META:N_META + 6, 0:cshard].set(conv_w.reshape(6, cshard))
    h, (first_in, g_small) = _build_h(x[0], tm, _Exchange([(shard["in"][0], True), (small, True)]), 1, "build_h")
    weight[("in", 0)] = first_in.reshape(-1, D_MODEL)
    cw = g_small[:, N_META:N_META + 6, 0:cshard].reshape(N_DEV, DEPTH, 3, cshard)
    cw = jnp.transpose(cw, (1, 2, 0, 3)).reshape(DEPTH, 3, CONV_W)
    conv_full = jnp.concatenate([cw, jnp.zeros((DEPTH, 5, CONV_W), F32)], axis=1)

    rope = _rope_tables(_rope_table(lp), tm)
    row1 = lambda a, l: a[l].reshape(1, -1)

    saved = []
    for l in range(DEPTH):
        a, qkv, bch = run(_in_proj_fwd, "in_proj_fwd", l, h, row1(mix_pre_g, l), weight[("in", l)], rope, tm)
        y_attn, = run(_attn_fwd, "attn_fwd", l, qkv, row1(sinks, l))
        y, z, h2 = run(_mix_out_fwd, "mix_out_fwd", l, bch, y_attn, h, conv_full[l], row1(attn_out_g, l),
                           row1(conv_out_g, l), weight[("out", l)], row1(mix_post_g, l), tm)
        mlp = _mlp_fwd if l < DEPTH - 1 else functools.partial(_mlp_fwd, target=loss_target[0])
        a2, up, f, *rest = run(mlp, "mlp_fwd", l, h2, row1(mlp_pre_g, l), weight[("up", l)], weight[("down", l)],
                               row1(mlp_post_g, l), tm_mlp)
        saved.append((h, a, qkv, bch, y_attn, y, z, h2, a2, up, f))
        h = rest[0]
    dh, loss_part = rest[0], rest[1][0, 0] * (0.5 / D_MODEL)

    gsmall = [None] * DEPTH
    for l in reversed(range(DEPTH)):
        h0, a, qkv, bch, y_attn, y, z, h2, a2, up, f = saved[l]
        df, dup, dh2, dg_mlp = run(_mlp_bwd_dx, "mlp_bwd_dx", l, dh, f, up, h2, weight[("down", l)], weight[("up", l)],
                                   row1(mlp_post_g, l), row1(mlp_pre_g, l), tm_mlp)
        grad[("down", l)], grad[("up", l)] = _mlp_bwd_dw(up, df, dup, a2, tm_dw_mlp, f"mlp_bwd_dw_{l}")
        dz, dya, dbch, dg_mix = run(_mix_out_bwd, "mix_out_bwd", l, dh2, z, y_attn, bch, weight[("out", l)],
                                    row1(mix_post_g, l), row1(attn_out_g, l), row1(conv_out_g, l), conv_full[l], tm)
        dq, dkv, dsink = run(_attn_bwd, "attn_bwd", l, qkv, y_attn, dya, row1(sinks, l), rope)
        grad[("in", l)], grad[("out", l)] = run(_mix_bwd_dw, "mix_bwd_dw", l, dq, dkv, dbch, a, y, dz, tm_dw_mix)
        dh, dg_in = run(_in_proj_bwd_dx, "in_proj_bwd_dx", l, dq, dkv, dbch, weight[("in", l)], h0, dh2,
                        row1(mix_pre_g, l), tm)
        tile_a = dg_mlp + dg_in + jnp.pad(dsink, ((0, 0), (0, D_MODEL - 128)))
        gsmall[l] = (tile_a, dg_mix)
    grad_x = dh[BLOCK:][None]

    loss_tile = jnp.zeros((8, D_MODEL), F32).at[ROW_LOSS, 0].set(loss_part)
    tot = _sum_small(jnp.concatenate(
        [gsmall[0][0] + loss_tile, gsmall[0][1], gsmall[1][0], gsmall[1][1], dh[LEAD_PAD:BLOCK]], axis=0))
    loss = tot[ROW_LOSS, 0]
    ta = [tot[16 * l:16 * l + 8] for l in range(DEPTH)]
    tb = [tot[16 * l + 8:16 * l + 16] for l in range(DEPTH)]
    pick = lambda tiles, r0, r1, c0, c1: jnp.stack([t[r0:r1, c0:c1] for t in tiles])
    g_mlp_post = pick(ta, ROW_MLP_POST, ROW_MLP_POST + 1, 0, D_MODEL).reshape(DEPTH, D_MODEL)
    g_mlp_pre = pick(ta, ROW_MLP_PRE, ROW_MLP_PRE + 1, 0, D_MODEL).reshape(DEPTH, D_MODEL)
    g_mix_pre = pick(ta, ROW_MIX_PRE, ROW_MIX_PRE + 1, 0, D_MODEL).reshape(DEPTH, D_MODEL)
    g_sinks = pick(ta, ROW_SINK, ROW_SINK + 1, 0, N_Q_HEADS).reshape(DEPTH, N_Q_HEADS)
    g_mix_post = pick(tb, ROW_MIX_POST, ROW_MIX_POST + 1, 0, D_MODEL).reshape(DEPTH, D_MODEL)
    g_attn_out = pick(tb, ROW_GROUP_G, ROW_GROUP_G + 1, 0, ATTN_W).reshape(DEPTH, ATTN_W)
    g_conv_out = pick(tb, ROW_GROUP_G, ROW_GROUP_G + 1, ATTN_W, D_MODEL).reshape(DEPTH, CONV_W)
    g_conv_full = pick(tb, ROW_CONV, ROW_CONV + 3, 0, CONV_W)
    g_conv = lax.dynamic_slice_in_dim(g_conv_full, me * cshard, cshard, axis=2)
    g_meta = lax.dynamic_slice_in_dim(tot[16 * DEPTH:16 * DEPTH + N_META], me * mshard, mshard, axis=1)

    r_in, r_out, r_up, r_down = [[landed[(n, l)] for l in range(DEPTH)] for n in ("in", "out", "up", "down")]
    g_w_in = jnp.swapaxes(_sum_parts(r_in, 96, "sum_w_in"), 1, 2)
    g_w_up = jnp.swapaxes(_sum_parts(r_up, 128, "sum_w_up"), 1, 2)
    d_w_in, nm_w_in, nv_w_in = _adamw_rows(w_in, g_w_in, m_w_in, v_w_in, 256, "adamw_w_in")
    d_w_up, nm_w_up, nv_w_up = _adamw_rows(w_up, g_w_up, m_w_up, v_w_up, 256, "adamw_w_up")
    g_w_out, d_w_out, nm_w_out, nv_w_out = _sum_adamw(r_out, w_out, m_w_out, v_w_out, 128, "adamw_w_out")
    g_w_down, d_w_down, nm_w_down, nv_w_down = _sum_adamw(r_down, w_down, m_w_down, v_w_down, 128, "adamw_w_down")

    ws = [meta_tokens, mix_pre_g, conv_w.reshape(6, cshard), sinks, attn_out_g, conv_out_g, mix_post_g, mlp_pre_g, mlp_post_g]
    gs = [g_meta, g_mix_pre, g_conv.reshape(6, cshard), g_sinks, g_attn_out, g_conv_out, g_mix_post, g_mlp_pre, g_mlp_post]
    ms = [m_meta_tokens, m_mix_pre_g, m_conv_w.reshape(6, cshard), m_sinks, m_attn_out_g, m_conv_out_g, m_mix_post_g,
          m_mlp_pre_g, m_mlp_post_g]
    vs = [v_meta_tokens, v_mix_pre_g, v_conv_w.reshape(6, cshard), v_sinks, v_attn_out_g, v_conv_out_g, v_mix_post_g,
          v_mlp_pre_g, v_mlp_post_g]
    ds, nms, nvs = _adamw_small(ws, gs, ms, vs)

    def order(meta, mix_pre, cv, sk, a_out, c_out, mix_post, mlp_pre, mlp_post, win, wout, wup, wdown):
        return [meta, mix_pre, win, cv.reshape(DEPTH, 3, cshard), sk, a_out, c_out, wout, mix_post, mlp_pre, wup, wdown, mlp_post]

    grads = order(*gs, g_w_in, g_w_out, g_w_up, g_w_down)
    deltas = order(*ds, d_w_in, d_w_out, d_w_up, d_w_down)
    new_m = order(*nms, nm_w_in, nm_w_out, nm_w_up, nm_w_down)
    new_v = order(*nvs, nv_w_in, nv_w_out, nv_w_up, nv_w_down)
    return (loss, grad_x, *grads, *deltas, *new_m, *new_v)
```

```python
import functools
import math

import jax
import jax.numpy as jnp
from jax import lax
from jax.experimental import pallas as pl
from jax.experimental.pallas import tpu as pltpu

F32 = jnp.float32
BF = jnp.bfloat16

D_MODEL = 1024
ATTN_W = 512
CONV_W = 512
KV_W = 128
HEAD_DIM = 64
N_Q_HEADS = 8
ROT_DIM = 16
D_FF = 4096
IN_W = 2304
N_META = 16
BLOCK = 128
LEAD_PAD = BLOCK - N_META
ROPE_THETA = 500000.0
EPS = 1e-6
N_DEV = 8
DEPTH = 2
NEG = -1e30
SCALE = HEAD_DIM ** -0.5

ADAM_LR = 0.001
ADAM_B1 = 0.9
ADAM_B2 = 0.999
ADAM_EPS = 1e-08
ADAM_WD = 0.01
ADAM_STEP = 10

ROW_MLP_POST, ROW_MLP_PRE, ROW_MIX_PRE, ROW_SINK, ROW_LOSS = 0, 1, 2, 3, 4
ROW_MIX_POST, ROW_GROUP_G, ROW_CONV = 0, 1, 2

VMEM_LIMIT = 56 * 1024 * 1024
MESH = pl.DeviceIdType.MESH


def _dot(a, b):
    return jnp.dot(a, b, preferred_element_type=F32)


def _dot_nt(a, b):
    return lax.dot_general(a, b, (((1,), (1,)), ((), ())), preferred_element_type=F32)


def _dot_tn(a, b):
    return lax.dot_general(a, b, (((0,), (0,)), ((), ())), preferred_element_type=F32)


def _rms_fwd(x, g):
    r = lax.rsqrt(jnp.mean(x * x, axis=-1, keepdims=True) + EPS)
    return x * r * g


def _rms_bwd(x, g, dy):
    r = lax.rsqrt(jnp.mean(x * x, axis=-1, keepdims=True) + EPS)
    xh = x * r
    t = dy * g
    dx = r * (t - xh * jnp.mean(t * xh, axis=-1, keepdims=True))
    dg = jnp.sum(dy * xh, axis=0, keepdims=True)
    return dx, dg


def _row_tile(lp, cands=(640, 512, 384, 256, 128)):
    for t in cands:
        if lp % t == 0:
            return t
    raise ValueError(f"row count {lp} is not a multiple of 128")


def _full(shape):
    n = len(shape)
    return pl.BlockSpec(shape, lambda *_: (0,) * n, pipeline_mode=pl.Buffered(1))


def _full_out(shape):
    n = len(shape)
    return pl.BlockSpec(shape, lambda *_: (0,) * n)


def _params(sem=("arbitrary",)):
    return pltpu.CompilerParams(dimension_semantics=sem, vmem_limit_bytes=VMEM_LIMIT)


def _rope_table(lp):
    half = ROT_DIM // 2
    pos = jnp.maximum(jnp.arange(lp) - LEAD_PAD, 0).astype(F32)
    inv_freq = jnp.power(jnp.float32(ROPE_THETA), -jnp.arange(0, ROT_DIM, 2, dtype=F32) / ROT_DIM)
    ang_t = jnp.concatenate([inv_freq, inv_freq])[:, None] * pos[None, :]
    row = lax.broadcasted_iota(jnp.int32, (ROT_DIM, lp), 0)
    cs_t = jnp.where(row < half, jnp.cos(ang_t), jnp.sin(ang_t))
    return jnp.pad(cs_t.T, ((0, 0), (0, 128 - ROT_DIM)))


def _rope_coeffs(t):
    half = ROT_DIM // 2
    lane = lax.broadcasted_iota(jnp.int32, t.shape, 1)
    cos_a = jnp.where(lane < half, t, 0.0)
    sin_a = pltpu.roll(jnp.where((lane >= half) & (lane < ROT_DIM), t, 0.0), 128 - half, 1)
    c = cos_a + pltpu.roll(cos_a, half, 1) + jnp.where((lane >= ROT_DIM) & (lane < HEAD_DIM), 1.0, 0.0)
    s2 = pltpu.roll(sin_a, half, 1)
    both = lambda u: u + pltpu.roll(u, HEAD_DIM, 1)
    return both(c), both(-sin_a), both(s2)


def _rope_tables(compact, tm):
    lp = compact.shape[0]

    def body(t_ref, c_ref, s1_ref, s2_ref):
        c_ref[...], s1_ref[...], s2_ref[...] = _rope_coeffs(t_ref[...])

    row = pl.BlockSpec((tm, 128), lambda i: (i, 0))
    return pl.pallas_call(
        body,
        name="rope_tables",
        grid=(lp // tm,),
        in_specs=[row],
        out_specs=[row] * 3,
        out_shape=[jax.ShapeDtypeStruct((lp, 128), F32)] * 3,
        compiler_params=_params(),
    )(compact)


def _rope(t, c, s1, s2):
    return t * c + pltpu.roll(t, BLOCK - 8, 1) * s1 + pltpu.roll(t, 8, 1) * s2


def _rope_t(dt, c, s1, s2):
    return dt * c + pltpu.roll(dt * s1, 8, 1) + pltpu.roll(dt * s2, BLOCK - 8, 1)


def _build_h(x, tm, exch, small_piece, name):
    seq = x.shape[0]
    lp = BLOCK + seq
    nt = lp // tm
    n_sub = tm // BLOCK
    small_shape = exch.land_shapes[small_piece].shape

    def body(*refs):
        for j in range(n_sub):
            refs[n_sub][j * BLOCK:(j + 1) * BLOCK, :] = refs[j][...]

    def after(lands, *refs):
        h_ref, buf = refs[n_sub], refs[n_sub + 1]
        pltpu.sync_copy(lands[small_piece], buf)
        h_ref[0:LEAD_PAD, :] = jnp.zeros((LEAD_PAD, D_MODEL), F32)
        for d in range(N_DEV):
            h_ref[LEAD_PAD:BLOCK, d * 128:(d + 1) * 128] = buf[d, 0:N_META, :]

    tile = lambda i: (i + 1) % nt
    piece = lambda j: pl.BlockSpec((BLOCK, D_MODEL), lambda i: (jnp.maximum(tile(i) * n_sub + j - 1, 0), 0))
    (h,), lands = _call(
        body, exch,
        name=name,
        grid=(nt,),
        in_specs=[piece(j) for j in range(n_sub)],
        out_specs=[pl.BlockSpec((tm, D_MODEL), lambda i: (tile(i), 0))],
        out_shape=[jax.ShapeDtypeStruct((lp, D_MODEL), F32)],
        scratch_shapes=[pltpu.VMEM(small_shape, F32)],
        compiler_params=_params(),
        after=after,
    )(*([x] * n_sub))
    return h, lands


def _in_proj_fwd(h, g, w_in_t, rope, tm, name, exch=None):
    lp = h.shape[0]

    def body(h_ref, g_ref, w_ref, c_ref, s1_ref, s2_ref, a_ref, qkv_ref, bch_ref):
        a = _rms_fwd(h_ref[...], g_ref[...]).astype(BF)
        a_ref[...] = a
        proj = _dot_nt(a, w_ref[...])
        c, s1, s2 = c_ref[...], s1_ref[...], s2_ref[...]
        for j in range(5):
            t = _rope(proj[:, j * 128:(j + 1) * 128], c, s1, s2)
            qkv_ref[:, j * 128:(j + 1) * 128] = (t * SCALE if j < 4 else t).astype(BF)
        qkv_ref[:, 640:768] = proj[:, 640:768].astype(BF)
        bch_ref[...] = proj[:, 768:].astype(BF)

    row = lambda w: pl.BlockSpec((tm, w), lambda i: (i, 0))
    return _call(
        body, exch,
        name=name,
        grid=(lp // tm,),
        in_specs=[row(D_MODEL), _full((1, D_MODEL)), _full((IN_W, D_MODEL)), row(128), row(128), row(128)],
        out_specs=[row(D_MODEL), row(768), row(3 * CONV_W)],
        out_shape=[
            jax.ShapeDtypeStruct((lp, D_MODEL), BF),
            jax.ShapeDtypeStruct((lp, 768), BF),
            jax.ShapeDtypeStruct((lp, 3 * CONV_W), BF),
        ],
        compiler_params=_params(),
    )(h, g, w_in_t, *rope)


def _fold_masks(i):
    r = lax.broadcasted_iota(jnp.int32, (2 * BLOCK, BLOCK), 0) & (BLOCK - 1)
    c = lax.broadcasted_iota(jnp.int32, (2 * BLOCK, BLOCK), 1)
    tri = c > r
    ok = jnp.where(tri, (i - 1) * BLOCK + c, i * BLOCK + c) >= LEAD_PAD
    return tri, ok


def _kv_operand(x, kvh):
    lane = lax.broadcasted_iota(jnp.int32, x.shape, 1)
    zero = jnp.zeros_like(x)
    if kvh == 0:
        lo = jnp.where(lane < HEAD_DIM, x, zero)
        hi = pltpu.roll(lo, HEAD_DIM, 1)
    else:
        hi = jnp.where(lane >= HEAD_DIM, x, zero)
        lo = pltpu.roll(hi, HEAD_DIM, 1)
    return jnp.concatenate([lo, hi], axis=0)


def _split4(t, tri):
    zero = jnp.zeros_like(t[0])
    return jnp.concatenate(
        [jnp.where(tri, t[0], zero), jnp.where(tri, zero, t[0]), jnp.where(tri, t[1], zero), jnp.where(tri, zero, t[1])], axis=1)


def _sink_cols(sink_ref, kvh):
    first = lax.broadcasted_iota(jnp.int32, (2 * BLOCK, 1), 0) < BLOCK
    return [jnp.where(first, sink_ref[0, 4 * kvh + half], sink_ref[0, 4 * kvh + 2 + half]) for half in range(2)]


def _folded_exp(q2, k4, tri, ok, sks):
    s = _dot_nt(q2, k4)
    es, ss = [], []
    for half in range(2):
        s_h = s[:, 2 * half * BLOCK:2 * (half + 1) * BLOCK]
        sf = jnp.where(ok, jnp.where(tri, s_h[:, :BLOCK], s_h[:, BLOCK:]), NEG)
        m = jnp.maximum(jnp.max(sf, axis=-1, keepdims=True), sks[half])
        es.append(jnp.exp(sf - m))
        ss.append(jnp.exp(sks[half] - m))
    sums = _dot(jnp.concatenate(es, axis=0).astype(BF), jnp.ones((BLOCK, BLOCK), BF))
    invs = [1.0 / (sums[2 * half * BLOCK:2 * (half + 1) * BLOCK] + ss[half]) for half in range(2)]
    return es, ss, invs


def _attn_fwd(qkv, sink, name, exch=None):
    lp = qkv.shape[0]
    nb = lp // BLOCK

    def body(sink_ref, q_ref, kvc_ref, kvp_ref, o_ref, p_ref, ps_ref):
        i = pl.program_id(0)
        tri, ok = _fold_masks(i)
        kvc, kvp = kvc_ref[...], kvp_ref[...]
        kk = jnp.concatenate([kvp[:, :128], kvc[:, :128]], axis=0)
        vv = jnp.concatenate([kvp[:, 128:], kvc[:, 128:]], axis=0)
        lane = lax.broadcasted_iota(jnp.int32, (BLOCK, 128), 1)
        p_sink = jnp.zeros((BLOCK, 128), F32)
        for kvh in range(2):
            q2 = jnp.concatenate([q_ref[:, 256 * kvh:256 * kvh + 128], q_ref[:, 256 * kvh + 128:256 * kvh + 256]], axis=0)
            es, ss, invs = _folded_exp(q2, _kv_operand(kk, kvh), tri, ok, _sink_cols(sink_ref, kvh))
            pb = [(es[half] * invs[half]).astype(BF) for half in range(2)]
            out = _dot(_split4(pb, tri), _kv_operand(vv, kvh))
            for pair in range(2):
                rows = slice(pair * BLOCK, (pair + 1) * BLOCK)
                o_ref[:, 256 * kvh + 128 * pair:256 * kvh + 128 * (pair + 1)] = out[rows].astype(BF)
                for half in range(2):
                    head = 4 * kvh + 2 * pair + half
                    p_ref[:, 128 * head:128 * (head + 1)] = pb[half][rows]
                    p_sink = jnp.where(lane == head, (ss[half] * invs[half][:, 0:1])[rows], p_sink)
        ps_ref[...] = p_sink

    return _call(
        body, exch,
        name=name,
        grid=(nb,),
        in_specs=[
            pl.BlockSpec(memory_space=pltpu.SMEM),
            pl.BlockSpec((BLOCK, ATTN_W), lambda i: (i, 0)),
            pl.BlockSpec((BLOCK, 256), lambda i: (i, 2)),
            pl.BlockSpec((BLOCK, 256), lambda i: (jnp.maximum(i - 1, 0), 2)),
        ],
        out_specs=[pl.BlockSpec((BLOCK, ATTN_W), lambda i: (i, 0)), pl.BlockSpec((BLOCK, N_Q_HEADS * BLOCK), lambda i: (i, 0)),
                   pl.BlockSpec((BLOCK, 128), lambda i: (i, 0))],
        out_shape=[jax.ShapeDtypeStruct((lp, ATTN_W), BF), jax.ShapeDtypeStruct((lp, N_Q_HEADS * BLOCK), BF),
                   jax.ShapeDtypeStruct((lp, 128), F32)],
        compiler_params=_params(),
    )(sink, qkv, qkv, qkv)


def _mix_out_fwd(bch, y_attn, h, conv_w, g_a, g_c, w_out, g_post, tm, name, exch=None):
    lp = h.shape[0]

    def body(bch_ref, ya_ref, h_ref, cw_ref, ga_ref, gc_ref, w_ref, gp_ref, y_ref, z_ref, h2_ref, ext):
        i = pl.program_id(0)

        @pl.when(i == 0)
        def _():
            ext[0:8, :] = jnp.zeros((8, CONV_W), F32)

        b = bch_ref[:, 0:CONV_W].astype(F32)
        u = bch_ref[:, CONV_W:2 * CONV_W].astype(F32) * bch_ref[:, 2 * CONV_W:3 * CONV_W].astype(F32)
        ext[8:8 + tm, :] = u
        yc = cw_ref[0:1, :] * ext[6:6 + tm, :] + cw_ref[1:2, :] * ext[7:7 + tm, :] + cw_ref[2:3, :] * u
        ext[0:8, :] = u[tm - 8:tm, :]
        ya = _rms_fwd(ya_ref[...].astype(F32), ga_ref[...]).astype(BF)
        yb = _rms_fwd(b * yc, gc_ref[...]).astype(BF)
        y_ref[:, 0:ATTN_W] = ya
        y_ref[:, ATTN_W:] = yb
        z = _dot(ya, w_ref[0:ATTN_W, :]) + _dot(yb, w_ref[ATTN_W:, :])
        z_ref[...] = z
        h2_ref[...] = h_ref[...] + _rms_fwd(z, gp_ref[...])

    row = lambda w: pl.BlockSpec((tm, w), lambda i: (i, 0))
    return _call(
        body, exch,
        name=name,
        grid=(lp // tm,),
        in_specs=[
            row(3 * CONV_W), row(ATTN_W), row(D_MODEL), _full((8, CONV_W)), _full((1, ATTN_W)), _full((1, CONV_W)),
            _full((D_MODEL, D_MODEL)), _full((1, D_MODEL)),
        ],
        out_specs=[row(D_MODEL), row(D_MODEL), row(D_MODEL)],
        out_shape=[
            jax.ShapeDtypeStruct((lp, D_MODEL), BF),
            jax.ShapeDtypeStruct((lp, D_MODEL), F32),
            jax.ShapeDtypeStruct((lp, D_MODEL), F32),
        ],
        scratch_shapes=[pltpu.VMEM((tm + 8, CONV_W), F32)],
        compiler_params=_params(),
    )(bch, y_attn, h, conv_w, g_a, g_c, w_out, g_post)


def _mlp_fwd(h2, g_pre, w_up_t, w_down, g_post, tm, name, exch=None, target=None):
    lp = h2.shape[0]
    sub = math.gcd(tm, BLOCK)
    n_sub, lead = tm // sub, BLOCK // sub
    n_t = n_sub if target is not None else 0

    def body(*refs):
        h_ref, gp_ref, wu_ref, wd_ref, gq_ref = refs[:5]
        t_refs = refs[5:5 + n_t]
        a_ref, up_ref, f_ref, last_ref = refs[5 + n_t:9 + n_t]
        h = h_ref[...]
        a = _rms_fwd(h, gp_ref[...]).astype(BF)
        a_ref[...] = a
        up = _dot_nt(a, wu_ref[...])
        up_ref[...] = up.astype(BF)
        act = jnp.square(jnp.maximum(up, 0.0)).astype(BF)
        f = _dot(act, wd_ref[...])
        f_ref[...] = f
        h3 = h + _rms_fwd(f, gq_ref[...])
        if target is None:
            last_ref[...] = h3
            return
        ls_ref = refs[9 + n_t]
        i = pl.program_id(0)

        @pl.when(i == 0)
        def _():
            ls_ref[...] = jnp.zeros((8, 128), F32)

        sq = jnp.zeros((1, 1), F32)
        for j in range(n_sub):
            on_tokens = i * n_sub + j >= lead
            d = jnp.where(on_tokens, h3[j * sub:(j + 1) * sub] - t_refs[j][...], 0.0)
            last_ref[j * sub:(j + 1) * sub, :] = d * (1.0 / D_MODEL)
            sq = sq + jnp.sum(d * d)
        ls_ref[...] += sq

    row = lambda w: pl.BlockSpec((tm, w), lambda i: (i, 0))
    piece = lambda j: pl.BlockSpec((sub, D_MODEL), lambda i: (jnp.maximum(i * n_sub + j - lead, 0), 0))
    out_specs = [row(D_MODEL), row(D_FF), row(D_MODEL), row(D_MODEL)]
    out_shape = [
        jax.ShapeDtypeStruct((lp, D_MODEL), BF),
        jax.ShapeDtypeStruct((lp, D_FF), BF),
        jax.ShapeDtypeStruct((lp, D_MODEL), F32),
        jax.ShapeDtypeStruct((lp, D_MODEL), F32),
    ]
    if target is not None:
        out_specs.append(_full_out((8, 128)))
        out_shape.append(jax.ShapeDtypeStruct((8, 128), F32))
    return _call(
        body, exch,
        name=name,
        grid=(lp // tm,),
        in_specs=[row(D_MODEL), _full((1, D_MODEL)), _full((D_FF, D_MODEL)), _full((D_FF, D_MODEL)), _full((1, D_MODEL))]
        + [piece(j) for j in range(n_t)],
        out_specs=out_specs,
        out_shape=out_shape,
        compiler_params=_params(),
    )(h2, g_pre, w_up_t, w_down, g_post, *([target] * n_t))


def _mlp_bwd_dx(dh3, f, up, h2, w_down, w_up_t, g_post, g_pre, tm, name, exch=None):
    lp = h2.shape[0]

    def body(dh3_ref, f_ref, up_ref, h2_ref, wd_ref, wu_ref, gq_ref, gp_ref, df_ref, dup_ref, dh2_ref, dg_ref):
        i = pl.program_id(0)

        @pl.when(i == 0)
        def _():
            dg_ref[...] = jnp.zeros((8, D_MODEL), F32)

        dh3 = dh3_ref[...]
        df, dgq = _rms_bwd(f_ref[...], gq_ref[...], dh3)
        dg_ref[ROW_MLP_POST:ROW_MLP_POST + 1, :] += dgq
        df = df.astype(BF)
        df_ref[...] = df
        dact = _dot_nt(df, wd_ref[...])
        dup = (dact * (2.0 * jnp.maximum(up_ref[...].astype(F32), 0.0))).astype(BF)
        dup_ref[...] = dup
        da = _dot(dup, wu_ref[...])
        dh, dgp = _rms_bwd(h2_ref[...], gp_ref[...], da)
        dg_ref[ROW_MLP_PRE:ROW_MLP_PRE + 1, :] += dgp
        dh2_ref[...] = dh3 + dh

    row = lambda w: pl.BlockSpec((tm, w), lambda i: (i, 0))
    return _call(
        body, exch,
        name=name,
        grid=(lp // tm,),
        in_specs=[
            row(D_MODEL), row(D_MODEL), row(D_FF), row(D_MODEL), _full((D_FF, D_MODEL)), _full((D_FF, D_MODEL)),
            _full((1, D_MODEL)), _full((1, D_MODEL)),
        ],
        out_specs=[row(D_MODEL), row(D_FF), row(D_MODEL), _full_out((8, D_MODEL))],
        out_shape=[
            jax.ShapeDtypeStruct((lp, D_MODEL), BF),
            jax.ShapeDtypeStruct((lp, D_FF), BF),
            jax.ShapeDtypeStruct((lp, D_MODEL), F32),
            jax.ShapeDtypeStruct((8, D_MODEL), F32),
        ],
        compiler_params=_params(),
    )(dh3, f, up, h2, w_down, w_up_t, g_post, g_pre)


def _mlp_bwd_dw(up, df, dup, a2, tm, name):
    lp = up.shape[0]
    nt = lp // tm
    nj = D_FF // D_MODEL

    def body(up_ref, df_ref, dup_ref, a_ref, dwd_ref, dwu_ref, accd, accu):
        i = pl.program_id(1)

        @pl.when(i == 0)
        def _():
            accd[...] = jnp.zeros_like(accd)
            accu[...] = jnp.zeros_like(accu)

        act = jnp.square(jnp.maximum(up_ref[...].astype(F32), 0.0)).astype(BF)
        accd[...] += _dot_tn(act, df_ref[...])
        accu[...] += _dot_tn(dup_ref[...], a_ref[...])

        @pl.when(i == nt - 1)
        def _():
            dwd_ref[...] = accd[...].astype(BF)
            dwu_ref[...] = accu[...].astype(BF)

    return pl.pallas_call(
        body,
        name=name,
        grid=(nj, nt),
        in_specs=[
            pl.BlockSpec((tm, D_MODEL), lambda j, i: (i, j)),
            pl.BlockSpec((tm, D_MODEL), lambda j, i: (i, 0)),
            pl.BlockSpec((tm, D_MODEL), lambda j, i: (i, j)),
            pl.BlockSpec((tm, D_MODEL), lambda j, i: (i, 0)),
        ],
        out_specs=[pl.BlockSpec((D_MODEL, D_MODEL), lambda j, i: (j, 0)), pl.BlockSpec((D_MODEL, D_MODEL), lambda j, i: (j, 0))],
        out_shape=[jax.ShapeDtypeStruct((D_FF, D_MODEL), BF), jax.ShapeDtypeStruct((D_FF, D_MODEL), BF)],
        scratch_shapes=[pltpu.VMEM((D_MODEL, D_MODEL), F32), pltpu.VMEM((D_MODEL, D_MODEL), F32)],
        compiler_params=_params(("arbitrary", "arbitrary")),
    )(up, df, dup, a2)


def _mix_out_bwd(dh2, z, y_attn, bch, w_out, g_post, g_a, g_c, conv_w, tm, name, exch=None):
    lp = dh2.shape[0]
    nt = lp // tm
    halo = 16

    def body(dh2_ref, z_ref, ya_ref, bch_ref, halo_ref, w_ref, gp_ref, ga_ref, gc_ref, cw_ref,
             dz_ref, dya_ref, dbch_ref, dg_ref, ext, ext_u):
        i = pl.program_id(0)
        dcw_ref = dg_ref.at[ROW_CONV:ROW_CONV + 3, 0:CONV_W]

        @pl.when(i == 0)
        def _():
            ext[tm:tm + 8, :] = jnp.zeros((8, CONV_W), F32)
            dg_ref[...] = jnp.zeros((8, D_MODEL), F32)

        dz, dgp = _rms_bwd(z_ref[...], gp_ref[...], dh2_ref[...])
        dg_ref[ROW_MIX_POST:ROW_MIX_POST + 1, :] += dgp
        dz = dz.astype(BF)
        dz_ref[...] = dz
        dya_n = _dot_nt(dz, w_ref[0:ATTN_W, :])
        dyb_n = _dot_nt(dz, w_ref[ATTN_W:, :])
        dya, dga = _rms_bwd(ya_ref[...].astype(F32), ga_ref[...], dya_n)
        dg_ref[ROW_GROUP_G:ROW_GROUP_G + 1, 0:ATTN_W] += dga
        dya_ref[...] = dya
        b = bch_ref[:, 0:CONV_W].astype(F32)
        c = bch_ref[:, CONV_W:2 * CONV_W].astype(F32)
        hc = bch_ref[:, 2 * CONV_W:3 * CONV_W].astype(F32)
        u = c * hc
        u_before = halo_ref[:, CONV_W:2 * CONV_W].astype(F32) * halo_ref[:, 2 * CONV_W:3 * CONV_W].astype(F32)
        ext_u[0:halo, :] = jnp.where(i < nt - 1, u_before, 0.0)
        ext_u[halo:halo + tm, :] = u
        yc_v = (cw_ref[0:1, :] * ext_u[halo - 2:halo - 2 + tm, :] + cw_ref[1:2, :] * ext_u[halo - 1:halo - 1 + tm, :]
                + cw_ref[2:3, :] * u)
        dyconv, dgc = _rms_bwd(b * yc_v, gc_ref[...], dyb_n)
        dg_ref[ROW_GROUP_G:ROW_GROUP_G + 1, ATTN_W:] += dgc
        dbch_ref[:, 0:CONV_W] = (dyconv * yc_v).astype(BF)
        dyc = dyconv * b
        ext[0:tm, :] = dyc
        d1 = ext[1:1 + tm, :]
        d2 = ext[2:2 + tm, :]
        du = cw_ref[2:3, :] * dyc + cw_ref[1:2, :] * d1 + cw_ref[0:1, :] * d2
        ext[tm:tm + 8, :] = dyc[0:8, :]
        dbch_ref[:, CONV_W:2 * CONV_W] = (du * hc).astype(BF)
        dbch_ref[:, 2 * CONV_W:3 * CONV_W] = (du * c).astype(BF)
        dcw_ref[0:1, :] += jnp.sum(u * d2, axis=0, keepdims=True)
        dcw_ref[1:2, :] += jnp.sum(u * d1, axis=0, keepdims=True)
        dcw_ref[2:3, :] += jnp.sum(u * dyc, axis=0, keepdims=True)

    row = lambda w: pl.BlockSpec((tm, w), lambda i: (nt - 1 - i, 0))
    before = pl.BlockSpec((halo, 3 * CONV_W), lambda i: (jnp.maximum((nt - 1 - i) * (tm // halo) - 1, 0), 0))
    return _call(
        body, exch,
        name=name,
        grid=(nt,),
        in_specs=[
            row(D_MODEL), row(D_MODEL), row(ATTN_W), row(3 * CONV_W), before, _full((D_MODEL, D_MODEL)),
            _full((1, D_MODEL)), _full((1, ATTN_W)), _full((1, CONV_W)), _full((8, CONV_W)),
        ],
        out_specs=[row(D_MODEL), row(ATTN_W), row(3 * CONV_W), _full_out((8, D_MODEL))],
        out_shape=[
            jax.ShapeDtypeStruct((lp, D_MODEL), BF),
            jax.ShapeDtypeStruct((lp, ATTN_W), F32),
            jax.ShapeDtypeStruct((lp, 3 * CONV_W), BF),
            jax.ShapeDtypeStruct((8, D_MODEL), F32),
        ],
        scratch_shapes=[pltpu.VMEM((tm + 8, CONV_W), F32), pltpu.VMEM((tm + halo, CONV_W), F32)],
        compiler_params=_params(),
    )(dh2, z, y_attn, bch, bch, w_out, g_post, g_a, g_c, conv_w)


def _attn_bwd(qkv, o, do, probs, p_sink, rope, name, exch=None):
    lp = qkv.shape[0]
    nb = lp // BLOCK

    def body(q_ref, kvc_ref, kvp_ref, o_ref, do_ref, p_ref, ps_ref, cq_ref, s1q_ref, s2q_ref, ck_ref, s1k_ref, s2k_ref,
             dq_ref, dkv_ref, dsink_ref, carry):
        i = pl.program_id(0)

        @pl.when(i == 0)
        def _():
            carry[...] = jnp.zeros_like(carry)
            dsink_ref[...] = jnp.zeros((8, 128), F32)

        def finish(tot):
            dk = _rope_t(tot[:, :128], ck_ref[...], s1k_ref[...], s2k_ref[...])
            dkv_ref[:, 0:128] = dk.astype(BF)
            dkv_ref[:, 128:256] = tot[:, 128:].astype(BF)

        @pl.when(i < nb)
        def _():
            tri, _ = _fold_masks(i)
            kvc, kvp = kvc_ref[...], kvp_ref[...]
            kk = jnp.concatenate([kvp[:, :128], kvc[:, :128]], axis=0)
            vv = jnp.concatenate([kvp[:, 128:], kvc[:, 128:]], axis=0)
            lane = lax.broadcasted_iota(jnp.int32, (BLOCK, 128), 1)
            lane2 = lax.broadcasted_iota(jnp.int32, (2 * BLOCK, 128), 1)
            rope_q = (cq_ref[...], s1q_ref[...], s2q_ref[...])
            deltas = jnp.zeros((BLOCK, 128), F32)
            folded = []
            for kvh in range(2):
                c0 = 256 * kvh
                q2 = jnp.concatenate([q_ref[:, c0:c0 + 128], q_ref[:, c0 + 128:c0 + 256]], axis=0)
                do2 = jnp.concatenate([do_ref[:, c0:c0 + 128], do_ref[:, c0 + 128:c0 + 256]], axis=0)
                o2 = jnp.concatenate([o_ref[:, c0:c0 + 128], o_ref[:, c0 + 128:c0 + 256]], axis=0).astype(F32)
                k4, v4 = _kv_operand(kk, kvh), _kv_operand(vv, kvh)
                prod = do2 * o2
                dob = do2.astype(BF)
                dp = _dot_nt(dob, v4)
                ds, pb = [], []
                for half in range(2):
                    heads = [4 * kvh + 2 * pair + half for pair in range(2)]
                    p = jnp.concatenate([p_ref[:, 128 * h:128 * (h + 1)] for h in heads], axis=0)
                    sel = (lane2 < HEAD_DIM) if half == 0 else (lane2 >= HEAD_DIM)
                    delta = jnp.sum(jnp.where(sel, prod, 0.0), axis=-1, keepdims=True)
                    dp_h = dp[:, 2 * half * BLOCK:2 * (half + 1) * BLOCK]
                    ds.append((p.astype(F32) * (jnp.where(tri, dp_h[:, :BLOCK], dp_h[:, BLOCK:]) - delta)).astype(BF))
                    pb.append(p)
                    for pair in range(2):
                        deltas = jnp.where(lane == heads[pair], delta[pair * BLOCK:(pair + 1) * BLOCK], deltas)
                ds4, p4 = _split4(ds, tri), _split4(pb, tri)
                dq2 = _dot(ds4, k4) * SCALE
                dq_ref[:, c0:c0 + 128] = _rope_t(dq2[:BLOCK], *rope_q).astype(BF)
                dq_ref[:, c0 + 128:c0 + 256] = _rope_t(dq2[BLOCK:], *rope_q).astype(BF)
                rk, rv = _dot_tn(ds4, q2), _dot_tn(p4, dob)
                own = (lane < HEAD_DIM) if kvh == 0 else (lane >= HEAD_DIM)
                group = []
                for r in (rk, rv):
                    for blk in range(2):
                        t = jnp.where(lane < HEAD_DIM, r[blk * BLOCK:(blk + 1) * BLOCK], r[(2 + blk) * BLOCK:(3 + blk) * BLOCK])
                        group.append(jnp.where(own, t + pltpu.roll(t, HEAD_DIM, 1), 0.0))
                folded.append(group)
            dsink_ref[ROW_SINK:ROW_SINK + 1, :] -= jnp.sum(ps_ref[...] * deltas, axis=0, keepdims=True)
            dk_p, dk_c, dv_p, dv_c = [folded[0][t] + folded[1][t] for t in range(4)]
            finish(carry[...] + jnp.concatenate([dk_p, dv_p], axis=1))
            carry[...] = jnp.concatenate([dk_c, dv_c], axis=1)

        @pl.when(i == nb)
        def _():
            finish(carry[...])

    qi = lambda i: jnp.minimum(i, nb - 1)
    ki = lambda i: jnp.maximum(i - 1, 0)
    tab_q = pl.BlockSpec((BLOCK, 128), lambda i: (qi(i), 0))
    tab_k = pl.BlockSpec((BLOCK, 128), lambda i: (ki(i), 0))
    return _call(
        body, exch,
        name=name,
        grid=(nb + 1,),
        in_specs=[
            pl.BlockSpec((BLOCK, ATTN_W), lambda i: (qi(i), 0)),
            pl.BlockSpec((BLOCK, 256), lambda i: (qi(i), 2)),
            pl.BlockSpec((BLOCK, 256), lambda i: (jnp.maximum(qi(i) - 1, 0), 2)),
            pl.BlockSpec((BLOCK, ATTN_W), lambda i: (qi(i), 0)),
            pl.BlockSpec((BLOCK, ATTN_W), lambda i: (qi(i), 0)),
            pl.BlockSpec((BLOCK, N_Q_HEADS * BLOCK), lambda i: (qi(i), 0)),
            tab_q, tab_q, tab_q, tab_q, tab_k, tab_k, tab_k,
        ],
        out_specs=[
            pl.BlockSpec((BLOCK, ATTN_W), lambda i: (qi(i), 0)),
            pl.BlockSpec((BLOCK, 256), lambda i: (ki(i), 0)),
            pl.BlockSpec((8, 128), lambda i: (0, 0)),
        ],
        out_shape=[
            jax.ShapeDtypeStruct((lp, ATTN_W), BF),
            jax.ShapeDtypeStruct((lp, 256), BF),
            jax.ShapeDtypeStruct((8, 128), F32),
        ],
        scratch_shapes=[pltpu.VMEM((BLOCK, 256), F32)],
        compiler_params=_params(),
    )(qkv, qkv, qkv, o, do, probs, p_sink, *rope, *rope)


def _in_proj_bwd_dx(dq, dkv, dbch, w_in_t, h, dh2, g, tm, name, exch=None):
    lp = h.shape[0]

    def body(dq_ref, dkv_ref, dbch_ref, w_ref, h_ref, dh2_ref, g_ref, dh_ref, dg_ref):
        i = pl.program_id(0)

        @pl.when(i == 0)
        def _():
            dg_ref[...] = jnp.zeros((8, D_MODEL), F32)

        da = _dot(dq_ref[...], w_ref[0:512, :]) + _dot(dkv_ref[...], w_ref[512:768, :]) + _dot(dbch_ref[...], w_ref[768:, :])
        dh, dg = _rms_bwd(h_ref[...], g_ref[...], da)
        dg_ref[ROW_MIX_PRE:ROW_MIX_PRE + 1, :] += dg
        dh_ref[...] = dh2_ref[...] + dh

    row = lambda w: pl.BlockSpec((tm, w), lambda i: (i, 0))
    return _call(
        body, exch,
        name=name,
        grid=(lp // tm,),
        in_specs=[row(ATTN_W), row(256), row(3 * CONV_W), _full((IN_W, D_MODEL)), row(D_MODEL), row(D_MODEL), _full((1, D_MODEL))],
        out_specs=[row(D_MODEL), _full_out((8, D_MODEL))],
        out_shape=[jax.ShapeDtypeStruct((lp, D_MODEL), F32), jax.ShapeDtypeStruct((8, D_MODEL), F32)],
        compiler_params=_params(),
    )(dq, dkv, dbch, w_in_t, h, dh2, g)


def _mix_bwd_dw(dq, dkv, dbch, a, y, dz, tm, name, exch=None):
    lp = a.shape[0]
    nt = lp // tm

    def body(dq_ref, dkv_ref, dbch_ref, a_ref, y_ref, dz_ref, dwi_ref, dwo_ref, acci, acco):
        i = pl.program_id(0)

        @pl.when(i == 0)
        def _():
            acci[...] = jnp.zeros_like(acci)
            acco[...] = jnp.zeros_like(acco)

        a_v = a_ref[...]
        acci[0:512, :] += _dot_tn(dq_ref[...], a_v)
        acci[512:768, :] += _dot_tn(dkv_ref[...], a_v)
        acci[768:, :] += _dot_tn(dbch_ref[...], a_v)
        acco[...] += _dot_tn(y_ref[...], dz_ref[...])

        @pl.when(i == nt - 1)
        def _():
            dwi_ref[...] = acci[...].astype(BF)
            dwo_ref[...] = acco[...].astype(BF)

    row = lambda w: pl.BlockSpec((tm, w), lambda i: (i, 0))
    return _call(
        body, exch,
        name=name,
        grid=(nt,),
        in_specs=[row(ATTN_W), row(256), row(3 * CONV_W), row(D_MODEL), row(D_MODEL), row(D_MODEL)],
        out_specs=[_full_out((IN_W, D_MODEL)), _full_out((D_MODEL, D_MODEL))],
        out_shape=[jax.ShapeDtypeStruct((IN_W, D_MODEL), BF), jax.ShapeDtypeStruct((D_MODEL, D_MODEL), BF)],
        scratch_shapes=[pltpu.VMEM((IN_W, D_MODEL), F32), pltpu.VMEM((D_MODEL, D_MODEL), F32)],
        compiler_params=_params(),
    )(dq, dkv, dbch, a, y, dz)


def _mesh_place():
    x, y, c = lax.axis_index("x"), lax.axis_index("y"), lax.axis_index("c")
    return x, y, c, 4 * x + 2 * y + c


def _peer(x, y, c, k):
    px = 1 - x if k & 4 else x
    py = 1 - y if k & 2 else y
    pc = 1 - c if k & 1 else c
    return (px, py, pc), 4 * px + 2 * py + pc


SIBLING = 1
SAME_CORE = (2, 4, 6)
OTHER_CORE = (3, 5, 7)


class _Exchange:
    def __init__(self, pieces):
        self.srcs = [s for s, _ in pieces]
        self.to_all = [g for _, g in pieces]
        self.n = len(pieces)
        self.land_shapes = [
            jax.ShapeDtypeStruct((N_DEV,) + (s.shape if g else s.shape[1:]), s.dtype) for s, g in pieces]
        self.sem_shapes = [pltpu.SemaphoreType.DMA((self.n, N_DEV - 1)), pltpu.SemaphoreType.DMA((self.n, N_DEV - 1)),
                           pltpu.SemaphoreType.DMA((self.n,))]
        self.forwards = any(self.to_all)

    def _ops(self, srcs, lands, sems):
        send_sems, recv_sems, local_sems = sems
        x, y, c, me = _mesh_place()

        def remote(p, k, src, slot, to):
            return pltpu.make_async_remote_copy(
                src_ref=src, dst_ref=lands[p].at[slot], send_sem=send_sems.at[p, k - 1], recv_sem=recv_sems.at[p, k - 1],
                device_id=to, device_id_type=MESH)

        def own(p):
            return pltpu.make_async_copy(srcs[p] if self.to_all[p] else srcs[p].at[me], lands[p].at[me], local_sems.at[p])

        def direct(p, k):
            peer, pidx = _peer(x, y, c, k)
            return remote(p, k, srcs[p] if self.to_all[p] else srcs[p].at[pidx], me, peer)

        def forward(p, k):
            sibling, _ = _peer(x, y, c, SIBLING)
            _, origin = _peer(x, y, c, k ^ SIBLING)
            return remote(p, k, lands[p].at[origin], origin, sibling)

        def arrival(p, k):
            peer, pidx = _peer(x, y, c, k)
            return remote(p, k, lands[p].at[pidx], pidx, peer)

        return own, direct, forward, arrival

    def start(self, srcs, lands, sems):
        own, direct, _, _ = self._ops(srcs, lands, sems)
        for p in range(self.n):
            own(p).start()
            for k in ((SIBLING,) + SAME_CORE) if self.to_all[p] else range(1, N_DEV):
                direct(p, k).start()

    def forward(self, srcs, lands, sems):
        _, _, forward, arrival = self._ops(srcs, lands, sems)
        for p in range(self.n):
            if self.to_all[p]:
                for k in SAME_CORE:
                    arrival(p, k).wait_recv()
                    forward(p, k ^ SIBLING).start()

    def finish(self, srcs, lands, sems):
        own, direct, forward, arrival = self._ops(srcs, lands, sems)
        for p in range(self.n):
            for k in ((SIBLING,) + OTHER_CORE) if self.to_all[p] else range(1, N_DEV):
                arrival(p, k).wait_recv()
        for p in range(self.n):
            for k in range(1, N_DEV):
                (forward(p, k) if self.to_all[p] and k in OTHER_CORE else direct(p, k)).wait_send()
            own(p).wait()


def _call(body, exch, *, name, grid, in_specs, out_specs, out_shape, scratch_shapes=(), compiler_params, after=None):
    if exch is None:
        return pl.pallas_call(body, name=name, grid=grid, in_specs=in_specs, out_specs=out_specs, out_shape=out_shape,
                              scratch_shapes=scratch_shapes, compiler_params=compiler_params)
    n_in, n_out, n_scr, n_x = len(in_specs), len(out_shape), len(scratch_shapes), exch.n
    steps = math.prod(grid)

    def carrying(*refs):
        a, b, c, d, e = n_in, n_in + n_x, n_in + n_x + n_out, n_in + 2 * n_x + n_out, n_in + 2 * n_x + n_out + n_scr
        ins, srcs, outs, lands, scr, sems = refs[:a], refs[a:b], refs[b:c], refs[c:d], refs[d:e], refs[e:]
        step = functools.reduce(lambda acc, t: acc * grid[t] + pl.program_id(t), range(len(grid)), 0)

        @pl.when(step == 0)
        def _():
            exch.start(srcs, lands, sems)

        body(*ins, *outs, *scr)

        if exch.forwards:
            @pl.when(step == max(0, steps - 1 - (steps + 7) // 8))
            def _():
                exch.forward(srcs, lands, sems)

        @pl.when(step == steps - 1)
        def _():
            exch.finish(srcs, lands, sems)
            if after is not None:
                after(lands, *ins, *outs, *scr)

    hbm = pl.BlockSpec(memory_space=pl.ANY)
    call = pl.pallas_call(
        carrying, name=name, grid=grid, in_specs=list(in_specs) + [hbm] * n_x, out_specs=list(out_specs) + [hbm] * n_x,
        out_shape=list(out_shape) + exch.land_shapes, scratch_shapes=list(scratch_shapes) + exch.sem_shapes,
        compiler_params=compiler_params)

    def run(*args):
        res = call(*args, *exch.srcs)
        return list(res[:n_out]), list(res[n_out:])

    return run


def _sum_small(part):
    def body(part_ref, out_ref, land, send_sems, recv_sems):
        x, y, c, me = _mesh_place()
        land[me] = part_ref[...]
        sent = []
        for k in range(1, N_DEV):
            peer, _ = _peer(x, y, c, k)
            cp = pltpu.make_async_remote_copy(
                src_ref=part_ref, dst_ref=land.at[me], send_sem=send_sems.at[k - 1], recv_sem=recv_sems.at[k - 1],
                device_id=peer, device_id_type=MESH)
            cp.start()
            sent.append(cp)
        for k in range(1, N_DEV):
            peer, pidx = _peer(x, y, c, k)
            pltpu.make_async_remote_copy(
                src_ref=part_ref, dst_ref=land.at[pidx], send_sem=send_sems.at[k - 1], recv_sem=recv_sems.at[k - 1],
                device_id=peer, device_id_type=MESH).wait_recv()
        for cp in sent:
            cp.wait_send()
        acc = land[0]
        for d in range(1, N_DEV):
            acc = acc + land[d]
        out_ref[...] = acc

    vmem = pl.BlockSpec(memory_space=pltpu.VMEM)
    return pl.pallas_call(
        body,
        name="sum_small",
        in_specs=[vmem],
        out_specs=vmem,
        out_shape=jax.ShapeDtypeStruct(part.shape, F32),
        scratch_shapes=[pltpu.VMEM((N_DEV,) + part.shape, F32), pltpu.SemaphoreType.DMA((N_DEV - 1,)),
                        pltpu.SemaphoreType.DMA((N_DEV - 1,))],
    )(part)


def _adamw(w, g, m, v):
    m = ADAM_B1 * m + (1.0 - ADAM_B1) * g
    v = ADAM_B2 * v + (1.0 - ADAM_B2) * jnp.square(g)
    m_hat = m / (1.0 - ADAM_B1 ** ADAM_STEP)
    v_hat = v / (1.0 - ADAM_B2 ** ADAM_STEP)
    delta = -ADAM_LR * (m_hat / (jnp.sqrt(v_hat) + ADAM_EPS) + ADAM_WD * w)
    return delta, m, v


def _landed_specs(tr, wd):
    return [pl.BlockSpec((N_DEV, tr, wd), lambda l, i, ll=ll: (0, jnp.where(l == ll, i, 0), 0)) for ll in range(DEPTH)]


def _device_sum(r_ref):
    acc = r_ref[0].astype(F32)
    for d in range(1, N_DEV):
        acc = acc + r_ref[d].astype(F32)
    return acc


def _sum_parts(recv, tr, name):
    _, r, wd = recv[0].shape

    def body(*refs):
        g_ref = refs[DEPTH]
        for ll in range(DEPTH):
            @pl.when(pl.program_id(0) == ll)
            def _(ll=ll):
                g_ref[0] = _device_sum(refs[ll])

    return pl.pallas_call(
        body,
        name=name,
        grid=(DEPTH, r // tr),
        in_specs=_landed_specs(tr, wd),
        out_specs=pl.BlockSpec((1, tr, wd), lambda l, i: (l, i, 0)),
        out_shape=jax.ShapeDtypeStruct((DEPTH, r, wd), F32),
        compiler_params=_params(("arbitrary", "arbitrary")),
    )(*recv)


def _sum_adamw(recv, w, m, v, tr, name):
    _, r, wd = recv[0].shape

    def body(*refs):
        w_ref, m_ref, v_ref, g_ref, d_ref, mo_ref, vo_ref = refs[DEPTH:]
        for ll in range(DEPTH):
            @pl.when(pl.program_id(0) == ll)
            def _(ll=ll):
                g = _device_sum(refs[ll])
                g_ref[0] = g
                d_ref[0], mo_ref[0], vo_ref[0] = _adamw(w_ref[0], g, m_ref[0], v_ref[0])

    blk = pl.BlockSpec((1, tr, wd), lambda l, i: (l, i, 0))
    shape = jax.ShapeDtypeStruct((DEPTH, r, wd), F32)
    return pl.pallas_call(
        body,
        name=name,
        grid=(DEPTH, r // tr),
        in_specs=_landed_specs(tr, wd) + [blk, blk, blk],
        out_specs=[blk] * 4,
        out_shape=[shape] * 4,
        compiler_params=_params(("arbitrary", "arbitrary")),
    )(*recv, w, m, v)


def _adamw_rows(w, g, m, v, tr, name):
    _, r, wd = w.shape

    def body(w_ref, g_ref, m_ref, v_ref, d_ref, mo_ref, vo_ref):
        d_ref[0], mo_ref[0], vo_ref[0] = _adamw(w_ref[0], g_ref[0], m_ref[0], v_ref[0])

    blk = pl.BlockSpec((1, tr, wd), lambda l, i: (l, i, 0))
    shape = jax.ShapeDtypeStruct(w.shape, F32)
    return pl.pallas_call(
        body,
        name=name,
        grid=(DEPTH, r // tr),
        in_specs=[blk] * 4,
        out_specs=[blk] * 3,
        out_shape=[shape] * 3,
        compiler_params=_params(("arbitrary", "arbitrary")),
    )(w, g, m, v)


def _adamw_small(ws, gs, ms, vs):
    n = len(ws)

    def body(*refs):
        w_r, g_r, m_r, v_r = refs[:n], refs[n:2 * n], refs[2 * n:3 * n], refs[3 * n:4 * n]
        d_o, m_o, v_o = refs[4 * n:5 * n], refs[5 * n:6 * n], refs[6 * n:7 * n]
        for t in range(n):
            d_o[t][...], m_o[t][...], v_o[t][...] = _adamw(w_r[t][...], g_r[t][...], m_r[t][...], v_r[t][...])

    vmem = pl.BlockSpec(memory_space=pltpu.VMEM)
    shapes = [jax.ShapeDtypeStruct(w.shape, F32) for w in ws]
    outs = pl.pallas_call(
        body,
        name="adamw_small",
        in_specs=[vmem] * (4 * n),
        out_specs=[vmem] * (3 * n),
        out_shape=shapes * 3,
    )(*ws, *gs, *ms, *vs)
    return outs[:n], outs[n:2 * n], outs[2 * n:]


def kernel(x, meta_tokens, mix_pre_g, w_in, conv_w, sinks, attn_out_g, conv_out_g, w_out, mix_post_g, mlp_pre_g, w_up, w_down, mlp_post_g, loss_target, m_meta_tokens, m_mix_pre_g, m_w_in, m_conv_w, m_sinks, m_attn_out_g, m_conv_out_g, m_w_out, m_mix_post_g, m_mlp_pre_g, m_w_up, m_w_down, m_mlp_post_g, v_meta_tokens, v_mix_pre_g, v_w_in, v_conv_w, v_sinks, v_attn_out_g, v_conv_out_g, v_w_out, v_mix_post_g, v_mlp_pre_g, v_w_up, v_w_down, v_mlp_post_g):
    seq = x.shape[1]
    lp = BLOCK + seq
    tm = _row_tile(lp)
    tm_mlp = _row_tile(lp, (320, 256, 128))
    tm_dw_mlp = _row_tile(lp, (1664, 1040, 640, 384, 256, 128))
    tm_dw_mix = _row_tile(lp, (832, 640, 384, 256, 128))
    me = 4 * lax.axis_index("x") + 2 * lax.axis_index("y") + lax.axis_index("c")
    cshard = CONV_W // N_DEV
    mshard = D_MODEL // N_DEV

    gather_with = {
        ("in_proj_fwd", 0): [("out", 0), ("in", 1)], ("attn_fwd", 0): [("up", 0)], ("mix_out_fwd", 0): [("down", 0)],
        ("mlp_fwd", 0): [("out", 1), ("up", 1), ("down", 1)],
    }
    scatter_with = {
        ("attn_bwd", 1): [("down", 1)], ("mlp_bwd_dx", 0): [("up", 1), ("in", 1), ("out", 1)],
        ("attn_bwd", 0): [("down", 0)], ("mix_bwd_dw", 0): [("up", 0)], ("in_proj_bwd_dx", 0): [("in", 0), ("out", 0)],
    }
    shard = {"in": jnp.swapaxes(w_in, 1, 2).astype(BF), "out": w_out.astype(BF),
             "up": jnp.swapaxes(w_up, 1, 2).astype(BF), "down": w_down.astype(BF)}
    weight = {}
    grad = {}
    landed = {}

    def run(fn, kind, l, *args):
        key, name = (kind, l), f"{kind}_{l}"
        if key in gather_with:
            blocks = gather_with[key]
            outs, lands = fn(*args, name, _Exchange([(shard[n][k], True) for n, k in blocks]))
            for b, land in zip(blocks, lands):
                weight[b] = land.reshape(-1, D_MODEL)
            return outs
        if key in scatter_with:
            blocks = scatter_with[key]
            outs, lands = fn(*args, name, _Exchange([(grad[b].reshape(N_DEV, -1, D_MODEL), False) for b in blocks]))
            landed.update(zip(blocks, lands))
            return outs
        return fn(*args, name)

    small = jnp.zeros((24, 128), F32)
    small = small.at[0:N_META, :].set(meta_tokens)
    small = small.at[N_META:N_META + 6, 0:cshard].set(conv_w.reshape(6, cshard))
    h, (first_in, g_small) = _build_h(x[0], tm, _Exchange([(shard["in"][0], True), (small, True)]), 1, "build_h")
    weight[("in", 0)] = first_in.reshape(-1, D_MODEL)
    cw = g_small[:, N_META:N_META + 6, 0:cshard].reshape(N_DEV, DEPTH, 3, cshard)
    cw = jnp.transpose(cw, (1, 2, 0, 3)).reshape(DEPTH, 3, CONV_W)
    conv_full = jnp.concatenate([cw, jnp.zeros((DEPTH, 5, CONV_W), F32)], axis=1)

    rope = _rope_tables(_rope_table(lp), tm)
    row1 = lambda a, l: a[l].reshape(1, -1)

    saved = []
    for l in range(DEPTH):
        a, qkv, bch = run(_in_proj_fwd, "in_proj_fwd", l, h, row1(mix_pre_g, l), weight[("in", l)], rope, tm)
        y_attn, probs, p_sink = run(_attn_fwd, "attn_fwd", l, qkv, row1(sinks, l))
        y, z, h2 = run(_mix_out_fwd, "mix_out_fwd", l, bch, y_attn, h, conv_full[l], row1(attn_out_g, l),
                           row1(conv_out_g, l), weight[("out", l)], row1(mix_post_g, l), tm)
        mlp = _mlp_fwd if l < DEPTH - 1 else functools.partial(_mlp_fwd, target=loss_target[0])
        a2, up, f, *rest = run(mlp, "mlp_fwd", l, h2, row1(mlp_pre_g, l), weight[("up", l)], weight[("down", l)],
                               row1(mlp_post_g, l), tm_mlp)
        saved.append((h, a, qkv, bch, y_attn, probs, p_sink, y, z, h2, a2, up, f))
        h = rest[0]
    dh, loss_part = rest[0], rest[1][0, 0] * (0.5 / D_MODEL)

    gsmall = [None] * DEPTH
    for l in reversed(range(DEPTH)):
        h0, a, qkv, bch, y_attn, probs, p_sink, y, z, h2, a2, up, f = saved[l]
        df, dup, dh2, dg_mlp = run(_mlp_bwd_dx, "mlp_bwd_dx", l, dh, f, up, h2, weight[("down", l)], weight[("up", l)],
                                   row1(mlp_post_g, l), row1(mlp_pre_g, l), tm_mlp)
        grad[("down", l)], grad[("up", l)] = _mlp_bwd_dw(up, df, dup, a2, tm_dw_mlp, f"mlp_bwd_dw_{l}")
        dz, dya, dbch, dg_mix = run(_mix_out_bwd, "mix_out_bwd", l, dh2, z, y_attn, bch, weight[("out", l)],
                                    row1(mix_post_g, l), row1(attn_out_g, l), row1(conv_out_g, l), conv_full[l], tm)
        dq, dkv, dsink = run(_attn_bwd, "attn_bwd", l, qkv, y_attn, dya, probs, p_sink, rope)
        grad[("in", l)], grad[("out", l)] = run(_mix_bwd_dw, "mix_bwd_dw", l, dq, dkv, dbch, a, y, dz, tm_dw_mix)
        dh, dg_in = run(_in_proj_bwd_dx, "in_proj_bwd_dx", l, dq, dkv, dbch, weight[("in", l)], h0, dh2,
                        row1(mix_pre_g, l), tm)
        tile_a = dg_mlp + dg_in + jnp.pad(dsink, ((0, 0), (0, D_MODEL - 128)))
        gsmall[l] = (tile_a, dg_mix)
    grad_x = dh[BLOCK:][None]

    loss_tile = jnp.zeros((8, D_MODEL), F32).at[ROW_LOSS, 0].set(loss_part)
    tot = _sum_small(jnp.concatenate(
        [gsmall[0][0] + loss_tile, gsmall[0][1], gsmall[1][0], gsmall[1][1], dh[LEAD_PAD:BLOCK]], axis=0))
    loss = tot[ROW_LOSS, 0]
    ta = [tot[16 * l:16 * l + 8] for l in range(DEPTH)]
    tb = [tot[16 * l + 8:16 * l + 16] for l in range(DEPTH)]
    pick = lambda tiles, r0, r1, c0, c1: jnp.stack([t[r0:r1, c0:c1] for t in tiles])
    g_mlp_post = pick(ta, ROW_MLP_POST, ROW_MLP_POST + 1, 0, D_MODEL).reshape(DEPTH, D_MODEL)
    g_mlp_pre = pick(ta, ROW_MLP_PRE, ROW_MLP_PRE + 1, 0, D_MODEL).reshape(DEPTH, D_MODEL)
    g_mix_pre = pick(ta, ROW_MIX_PRE, ROW_MIX_PRE + 1, 0, D_MODEL).reshape(DEPTH, D_MODEL)
    g_sinks = pick(ta, ROW_SINK, ROW_SINK + 1, 0, N_Q_HEADS).reshape(DEPTH, N_Q_HEADS)
    g_mix_post = pick(tb, ROW_MIX_POST, ROW_MIX_POST + 1, 0, D_MODEL).reshape(DEPTH, D_MODEL)
    g_attn_out = pick(tb, ROW_GROUP_G, ROW_GROUP_G + 1, 0, ATTN_W).reshape(DEPTH, ATTN_W)
    g_conv_out = pick(tb, ROW_GROUP_G, ROW_GROUP_G + 1, ATTN_W, D_MODEL).reshape(DEPTH, CONV_W)
    g_conv_full = pick(tb, ROW_CONV, ROW_CONV + 3, 0, CONV_W)
    g_conv = lax.dynamic_slice_in_dim(g_conv_full, me * cshard, cshard, axis=2)
    g_meta = lax.dynamic_slice_in_dim(tot[16 * DEPTH:16 * DEPTH + N_META], me * mshard, mshard, axis=1)

    r_in, r_out, r_up, r_down = [[landed[(n, l)] for l in range(DEPTH)] for n in ("in", "out", "up", "down")]
    g_w_in = jnp.swapaxes(_sum_parts(r_in, 96, "sum_w_in"), 1, 2)
    g_w_up = jnp.swapaxes(_sum_parts(r_up, 128, "sum_w_up"), 1, 2)
    d_w_in, nm_w_in, nv_w_in = _adamw_rows(w_in, g_w_in, m_w_in, v_w_in, 256, "adamw_w_in")
    d_w_up, nm_w_up, nv_w_up = _adamw_rows(w_up, g_w_up, m_w_up, v_w_up, 256, "adamw_w_up")
    g_w_out, d_w_out, nm_w_out, nv_w_out = _sum_adamw(r_out, w_out, m_w_out, v_w_out, 128, "adamw_w_out")
    g_w_down, d_w_down, nm_w_down, nv_w_down = _sum_adamw(r_down, w_down, m_w_down, v_w_down, 128, "adamw_w_down")

    ws = [meta_tokens, mix_pre_g, conv_w.reshape(6, cshard), sinks, attn_out_g, conv_out_g, mix_post_g, mlp_pre_g, mlp_post_g]
    gs = [g_meta, g_mix_pre, g_conv.reshape(6, cshard), g_sinks, g_attn_out, g_conv_out, g_mix_post, g_mlp_pre, g_mlp_post]
    ms = [m_meta_tokens, m_mix_pre_g, m_conv_w.reshape(6, cshard), m_sinks, m_attn_out_g, m_conv_out_g, m_mix_post_g,
          m_mlp_pre_g, m_mlp_post_g]
    vs = [v_meta_tokens, v_mix_pre_g, v_conv_w.reshape(6, cshard), v_sinks, v_attn_out_g, v_conv_out_g, v_mix_post_g,
          v_mlp_pre_g, v_mlp_post_g]
    ds, nms, nvs = _adamw_small(ws, gs, ms, vs)

    def order(meta, mix_pre, cv, sk, a_out, c_out, mix_post, mlp_pre, mlp_post, win, wout, wup, wdown):
        return [meta, mix_pre, win, cv.reshape(DEPTH, 3, cshard), sk, a_out, c_out, wout, mix_post, mlp_pre, wup, wdown, mlp_post]

    grads = order(*gs, g_w_in, g_w_out, g_w_up, g_w_down)
    deltas = order(*ds, d_w_in, d_w_out, d_w_up, d_w_down)
    new_m = order(*nms, nm_w_in, nm_w_out, nm_w_up, nm_w_down)
    new_v = order(*nvs, nv_w_in, nv_w_out, nv_w_up, nv_w_down)
    return (loss, grad_x, *grads, *deltas, *new_m, *new_v)
```

```python
import functools
import math

import jax
import jax.numpy as jnp
from jax import lax
from jax.experimental import pallas as pl
from jax.experimental.pallas import tpu as pltpu

F32 = jnp.float32
BF = jnp.bfloat16

D_MODEL = 1024
ATTN_W = 512
CONV_W = 512
KV_W = 128
HEAD_DIM = 64
N_Q_HEADS = 8
ROT_DIM = 16
D_FF = 4096
IN_W = 2304
N_META = 16
BLOCK = 128
LEAD_PAD = BLOCK - N_META
ROPE_THETA = 500000.0
EPS = 1e-6
N_DEV = 8
DEPTH = 2
NEG = -1e30
SCALE = HEAD_DIM ** -0.5

ADAM_LR = 0.001
ADAM_B1 = 0.9
ADAM_B2 = 0.999
ADAM_EPS = 1e-08
ADAM_WD = 0.01
ADAM_STEP = 10

ROW_MLP_POST, ROW_MLP_PRE, ROW_MIX_PRE, ROW_SINK, ROW_LOSS = 0, 1, 2, 3, 4
ROW_MIX_POST, ROW_GROUP_G, ROW_CONV = 0, 1, 2

VMEM_LIMIT = 56 * 1024 * 1024
MESH = pl.DeviceIdType.MESH


def _dot(a, b):
    return jnp.dot(a, b, preferred_element_type=F32)


def _dot_nt(a, b):
    return lax.dot_general(a, b, (((1,), (1,)), ((), ())), preferred_element_type=F32)


def _dot_tn(a, b):
    return lax.dot_general(a, b, (((0,), (0,)), ((), ())), preferred_element_type=F32)


def _rms_fwd(x, g):
    r = lax.rsqrt(jnp.mean(x * x, axis=-1, keepdims=True) + EPS)
    return x * r * g


def _rms_bwd(x, g, dy):
    r = lax.rsqrt(jnp.mean(x * x, axis=-1, keepdims=True) + EPS)
    xh = x * r
    t = dy * g
    dx = r * (t - xh * jnp.mean(t * xh, axis=-1, keepdims=True))
    dg = jnp.sum(dy * xh, axis=0, keepdims=True)
    return dx, dg


def _row_tile(lp, cands=(640, 512, 384, 256, 128)):
    for t in cands:
        if lp % t == 0:
            return t
    raise ValueError(f"row count {lp} is not a multiple of 128")


def _full(shape):
    n = len(shape)
    return pl.BlockSpec(shape, lambda *_: (0,) * n, pipeline_mode=pl.Buffered(1))


def _full_out(shape):
    n = len(shape)
    return pl.BlockSpec(shape, lambda *_: (0,) * n)


def _params(sem=("arbitrary",)):
    return pltpu.CompilerParams(dimension_semantics=sem, vmem_limit_bytes=VMEM_LIMIT)


def _rope_table(lp):
    half = ROT_DIM // 2
    pos = jnp.maximum(jnp.arange(lp) - LEAD_PAD, 0).astype(F32)
    inv_freq = jnp.power(jnp.float32(ROPE_THETA), -jnp.arange(0, ROT_DIM, 2, dtype=F32) / ROT_DIM)
    ang_t = jnp.concatenate([inv_freq, inv_freq])[:, None] * pos[None, :]
    row = lax.broadcasted_iota(jnp.int32, (ROT_DIM, lp), 0)
    cs_t = jnp.where(row < half, jnp.cos(ang_t), jnp.sin(ang_t))
    return jnp.pad(cs_t.T, ((0, 0), (0, 128 - ROT_DIM)))


def _rope_coeffs(t):
    half = ROT_DIM // 2
    lane = lax.broadcasted_iota(jnp.int32, t.shape, 1)
    cos_a = jnp.where(lane < half, t, 0.0)
    sin_a = pltpu.roll(jnp.where((lane >= half) & (lane < ROT_DIM), t, 0.0), 128 - half, 1)
    c = cos_a + pltpu.roll(cos_a, half, 1) + jnp.where((lane >= ROT_DIM) & (lane < HEAD_DIM), 1.0, 0.0)
    s2 = pltpu.roll(sin_a, half, 1)
    both = lambda u: u + pltpu.roll(u, HEAD_DIM, 1)
    return both(c), both(-sin_a), both(s2)


def _rope(t, c, s1, s2):
    return t * c + pltpu.roll(t, BLOCK - 8, 1) * s1 + pltpu.roll(t, 8, 1) * s2


def _rope_t(dt, c, s1, s2):
    return dt * c + pltpu.roll(dt * s1, 8, 1) + pltpu.roll(dt * s2, BLOCK - 8, 1)


def _build_h(x, rope_compact, tm, exch, small_piece, name):
    seq = x.shape[0]
    lp = BLOCK + seq
    nt = lp // tm
    n_sub = tm // BLOCK
    small_shape = exch.land_shapes[small_piece].shape

    def body(*refs):
        h_ref, c_ref, s1_ref, s2_ref = refs[n_sub + 1:n_sub + 5]
        for j in range(n_sub):
            h_ref[j * BLOCK:(j + 1) * BLOCK, :] = refs[j][...]
        c_ref[...], s1_ref[...], s2_ref[...] = _rope_coeffs(refs[n_sub][...])

    def after(lands, *refs):
        h_ref, buf = refs[n_sub + 1], refs[n_sub + 5]
        pltpu.sync_copy(lands[small_piece], buf)
        h_ref[0:LEAD_PAD, :] = jnp.zeros((LEAD_PAD, D_MODEL), F32)
        for d in range(N_DEV):
            h_ref[LEAD_PAD:BLOCK, d * 128:(d + 1) * 128] = buf[d, 0:N_META, :]

    tile = lambda i: (i + 1) % nt
    piece = lambda j: pl.BlockSpec((BLOCK, D_MODEL), lambda i: (jnp.maximum(tile(i) * n_sub + j - 1, 0), 0))
    rows = lambda w: pl.BlockSpec((tm, w), lambda i: (tile(i), 0))
    (h, *rope), lands = _call(
        body, exch,
        name=name,
        grid=(nt,),
        in_specs=[piece(j) for j in range(n_sub)] + [rows(128)],
        out_specs=[rows(D_MODEL)] + [rows(128)] * 3,
        out_shape=[jax.ShapeDtypeStruct((lp, D_MODEL), F32)] + [jax.ShapeDtypeStruct((lp, 128), F32)] * 3,
        scratch_shapes=[pltpu.VMEM(small_shape, F32)],
        compiler_params=_params(),
        after=after,
    )(*([x] * n_sub), rope_compact)
    return h, rope, lands


def _in_proj_fwd(h, g, w_in_t, rope, tm, name, exch=None):
    lp = h.shape[0]

    def body(h_ref, g_ref, w_ref, c_ref, s1_ref, s2_ref, a_ref, qkv_ref, bch_ref):
        a = _rms_fwd(h_ref[...], g_ref[...]).astype(BF)
        a_ref[...] = a
        proj = _dot_nt(a, w_ref[...])
        c, s1, s2 = c_ref[...], s1_ref[...], s2_ref[...]
        for j in range(5):
            t = _rope(proj[:, j * 128:(j + 1) * 128], c, s1, s2)
            qkv_ref[:, j * 128:(j + 1) * 128] = (t * SCALE if j < 4 else t).astype(BF)
        qkv_ref[:, 640:768] = proj[:, 640:768].astype(BF)
        bch_ref[...] = proj[:, 768:].astype(BF)

    row = lambda w: pl.BlockSpec((tm, w), lambda i: (i, 0))
    return _call(
        body, exch,
        name=name,
        grid=(lp // tm,),
        in_specs=[row(D_MODEL), _full((1, D_MODEL)), _full((IN_W, D_MODEL)), row(128), row(128), row(128)],
        out_specs=[row(D_MODEL), row(768), row(3 * CONV_W)],
        out_shape=[
            jax.ShapeDtypeStruct((lp, D_MODEL), BF),
            jax.ShapeDtypeStruct((lp, 768), BF),
            jax.ShapeDtypeStruct((lp, 3 * CONV_W), BF),
        ],
        compiler_params=_params(),
    )(h, g, w_in_t, *rope)


def _fold_masks(i):
    r = lax.broadcasted_iota(jnp.int32, (2 * BLOCK, BLOCK), 0) & (BLOCK - 1)
    c = lax.broadcasted_iota(jnp.int32, (2 * BLOCK, BLOCK), 1)
    tri = c > r
    ok = jnp.where(tri, (i - 1) * BLOCK + c, i * BLOCK + c) >= LEAD_PAD
    return tri, ok


def _kv_operand(x, kvh):
    lane = lax.broadcasted_iota(jnp.int32, x.shape, 1)
    zero = jnp.zeros_like(x)
    if kvh == 0:
        lo = jnp.where(lane < HEAD_DIM, x, zero)
        hi = pltpu.roll(lo, HEAD_DIM, 1)
    else:
        hi = jnp.where(lane >= HEAD_DIM, x, zero)
        lo = pltpu.roll(hi, HEAD_DIM, 1)
    return jnp.concatenate([lo, hi], axis=0)


def _split4(t, tri):
    zero = jnp.zeros_like(t[0])
    return jnp.concatenate(
        [jnp.where(tri, t[0], zero), jnp.where(tri, zero, t[0]), jnp.where(tri, t[1], zero), jnp.where(tri, zero, t[1])], axis=1)


def _sink_cols(sink_ref, kvh):
    first = lax.broadcasted_iota(jnp.int32, (2 * BLOCK, 1), 0) < BLOCK
    return [jnp.where(first, sink_ref[0, 4 * kvh + half], sink_ref[0, 4 * kvh + 2 + half]) for half in range(2)]


def _folded_exp(q2, k4, tri, ok, sks):
    s = _dot_nt(q2, k4)
    es, ss = [], []
    for half in range(2):
        s_h = s[:, 2 * half * BLOCK:2 * (half + 1) * BLOCK]
        sf = jnp.where(ok, jnp.where(tri, s_h[:, :BLOCK], s_h[:, BLOCK:]), NEG)
        m = jnp.maximum(jnp.max(sf, axis=-1, keepdims=True), sks[half])
        es.append(jnp.exp(sf - m))
        ss.append(jnp.exp(sks[half] - m))
    sums = _dot(jnp.concatenate(es, axis=0).astype(BF), jnp.ones((BLOCK, BLOCK), BF))
    invs = [1.0 / (sums[2 * half * BLOCK:2 * (half + 1) * BLOCK] + ss[half]) for half in range(2)]
    return es, ss, invs


def _attn_fwd(qkv, sink, name, exch=None):
    lp = qkv.shape[0]
    nb = lp // BLOCK

    def body(sink_ref, q_ref, kvc_ref, kvp_ref, o_ref, p_ref, ps_ref):
        i = pl.program_id(0)
        tri, ok = _fold_masks(i)
        kvc, kvp = kvc_ref[...], kvp_ref[...]
        kk = jnp.concatenate([kvp[:, :128], kvc[:, :128]], axis=0)
        vv = jnp.concatenate([kvp[:, 128:], kvc[:, 128:]], axis=0)
        lane = lax.broadcasted_iota(jnp.int32, (BLOCK, 128), 1)
        p_sink = jnp.zeros((BLOCK, 128), F32)
        for kvh in range(2):
            q2 = jnp.concatenate([q_ref[:, 256 * kvh:256 * kvh + 128], q_ref[:, 256 * kvh + 128:256 * kvh + 256]], axis=0)
            es, ss, invs = _folded_exp(q2, _kv_operand(kk, kvh), tri, ok, _sink_cols(sink_ref, kvh))
            pb = [(es[half] * invs[half]).astype(BF) for half in range(2)]
            out = _dot(_split4(pb, tri), _kv_operand(vv, kvh))
            for pair in range(2):
                rows = slice(pair * BLOCK, (pair + 1) * BLOCK)
                o_ref[:, 256 * kvh + 128 * pair:256 * kvh + 128 * (pair + 1)] = out[rows].astype(BF)
                for half in range(2):
                    head = 4 * kvh + 2 * pair + half
                    p_ref[:, 128 * head:128 * (head + 1)] = pb[half][rows]
                    p_sink = jnp.where(lane == head, (ss[half] * invs[half][:, 0:1])[rows], p_sink)
        ps_ref[...] = p_sink

    return _call(
        body, exch,
        name=name,
        grid=(nb,),
        in_specs=[
            pl.BlockSpec(memory_space=pltpu.SMEM),
            pl.BlockSpec((BLOCK, ATTN_W), lambda i: (i, 0)),
            pl.BlockSpec((BLOCK, 256), lambda i: (i, 2)),
            pl.BlockSpec((BLOCK, 256), lambda i: (jnp.maximum(i - 1, 0), 2)),
        ],
        out_specs=[pl.BlockSpec((BLOCK, ATTN_W), lambda i: (i, 0)), pl.BlockSpec((BLOCK, N_Q_HEADS * BLOCK), lambda i: (i, 0)),
                   pl.BlockSpec((BLOCK, 128), lambda i: (i, 0))],
        out_shape=[jax.ShapeDtypeStruct((lp, ATTN_W), BF), jax.ShapeDtypeStruct((lp, N_Q_HEADS * BLOCK), BF),
                   jax.ShapeDtypeStruct((lp, 128), F32)],
        compiler_params=_params(),
    )(sink, qkv, qkv, qkv)


def _mix_out_fwd(bch, y_attn, h, conv_w, g_a, g_c, w_out, g_post, tm, name, exch=None):
    lp = h.shape[0]

    def body(bch_ref, ya_ref, h_ref, cw_ref, ga_ref, gc_ref, w_ref, gp_ref, y_ref, z_ref, h2_ref, ext):
        i = pl.program_id(0)

        @pl.when(i == 0)
        def _():
            ext[0:8, :] = jnp.zeros((8, CONV_W), F32)

        b = bch_ref[:, 0:CONV_W].astype(F32)
        u = bch_ref[:, CONV_W:2 * CONV_W].astype(F32) * bch_ref[:, 2 * CONV_W:3 * CONV_W].astype(F32)
        ext[8:8 + tm, :] = u
        yc = cw_ref[0:1, :] * ext[6:6 + tm, :] + cw_ref[1:2, :] * ext[7:7 + tm, :] + cw_ref[2:3, :] * u
        ext[0:8, :] = u[tm - 8:tm, :]
        ya = _rms_fwd(ya_ref[...].astype(F32), ga_ref[...]).astype(BF)
        yb = _rms_fwd(b * yc, gc_ref[...]).astype(BF)
        y_ref[:, 0:ATTN_W] = ya
        y_ref[:, ATTN_W:] = yb
        z = _dot(ya, w_ref[0:ATTN_W, :]) + _dot(yb, w_ref[ATTN_W:, :])
        z_ref[...] = z
        h2_ref[...] = h_ref[...] + _rms_fwd(z, gp_ref[...])

    row = lambda w: pl.BlockSpec((tm, w), lambda i: (i, 0))
    return _call(
        body, exch,
        name=name,
        grid=(lp // tm,),
        in_specs=[
            row(3 * CONV_W), row(ATTN_W), row(D_MODEL), _full((8, CONV_W)), _full((1, ATTN_W)), _full((1, CONV_W)),
            _full((D_MODEL, D_MODEL)), _full((1, D_MODEL)),
        ],
        out_specs=[row(D_MODEL), row(D_MODEL), row(D_MODEL)],
        out_shape=[
            jax.ShapeDtypeStruct((lp, D_MODEL), BF),
            jax.ShapeDtypeStruct((lp, D_MODEL), F32),
            jax.ShapeDtypeStruct((lp, D_MODEL), F32),
        ],
        scratch_shapes=[pltpu.VMEM((tm + 8, CONV_W), F32)],
        compiler_params=_params(),
    )(bch, y_attn, h, conv_w, g_a, g_c, w_out, g_post)


def _mlp_fwd(h2, g_pre, w_up_t, w_down, g_post, tm, name, exch=None, target=None):
    lp = h2.shape[0]
    sub = math.gcd(tm, BLOCK)
    n_sub, lead = tm // sub, BLOCK // sub
    n_t = n_sub if target is not None else 0

    def body(*refs):
        h_ref, gp_ref, wu_ref, wd_ref, gq_ref = refs[:5]
        t_refs = refs[5:5 + n_t]
        a_ref, up_ref, f_ref, last_ref = refs[5 + n_t:9 + n_t]
        h = h_ref[...]
        a = _rms_fwd(h, gp_ref[...]).astype(BF)
        a_ref[...] = a
        up = _dot_nt(a, wu_ref[...])
        up_ref[...] = up.astype(BF)
        act = jnp.square(jnp.maximum(up, 0.0)).astype(BF)
        f = _dot(act, wd_ref[...])
        f_ref[...] = f
        h3 = h + _rms_fwd(f, gq_ref[...])
        if target is None:
            last_ref[...] = h3
            return
        ls_ref = refs[9 + n_t]
        i = pl.program_id(0)

        @pl.when(i == 0)
        def _():
            ls_ref[...] = jnp.zeros((8, 128), F32)

        sq = jnp.zeros((1, 1), F32)
        for j in range(n_sub):
            on_tokens = i * n_sub + j >= lead
            d = jnp.where(on_tokens, h3[j * sub:(j + 1) * sub] - t_refs[j][...], 0.0)
            last_ref[j * sub:(j + 1) * sub, :] = d * (1.0 / D_MODEL)
            sq = sq + jnp.sum(d * d)
        ls_ref[...] += sq

    row = lambda w: pl.BlockSpec((tm, w), lambda i: (i, 0))
    piece = lambda j: pl.BlockSpec((sub, D_MODEL), lambda i: (jnp.maximum(i * n_sub + j - lead, 0), 0))
    out_specs = [row(D_MODEL), row(D_FF), row(D_MODEL), row(D_MODEL)]
    out_shape = [
        jax.ShapeDtypeStruct((lp, D_MODEL), BF),
        jax.ShapeDtypeStruct((lp, D_FF), BF),
        jax.ShapeDtypeStruct((lp, D_MODEL), F32),
        jax.ShapeDtypeStruct((lp, D_MODEL), F32),
    ]
    if target is not None:
        out_specs.append(_full_out((8, 128)))
        out_shape.append(jax.ShapeDtypeStruct((8, 128), F32))
    return _call(
        body, exch,
        name=name,
        grid=(lp // tm,),
        in_specs=[row(D_MODEL), _full((1, D_MODEL)), _full((D_FF, D_MODEL)), _full((D_FF, D_MODEL)), _full((1, D_MODEL))]
        + [piece(j) for j in range(n_t)],
        out_specs=out_specs,
        out_shape=out_shape,
        compiler_params=_params(),
    )(h2, g_pre, w_up_t, w_down, g_post, *([target] * n_t))


def _mlp_bwd_dx(dh3, f, up, h2, w_down, w_up_t, g_post, g_pre, tm, name, exch=None):
    lp = h2.shape[0]

    def body(dh3_ref, f_ref, up_ref, h2_ref, wd_ref, wu_ref, gq_ref, gp_ref, df_ref, dup_ref, dh2_ref, dg_ref):
        i = pl.program_id(0)

        @pl.when(i == 0)
        def _():
            dg_ref[...] = jnp.zeros((8, D_MODEL), F32)

        dh3 = dh3_ref[...]
        df, dgq = _rms_bwd(f_ref[...], gq_ref[...], dh3)
        dg_ref[ROW_MLP_POST:ROW_MLP_POST + 1, :] += dgq
        df = df.astype(BF)
        df_ref[...] = df
        dact = _dot_nt(df, wd_ref[...])
        dup = (dact * (2.0 * jnp.maximum(up_ref[...].astype(F32), 0.0))).astype(BF)
        dup_ref[...] = dup
        da = _dot(dup, wu_ref[...])
        dh, dgp = _rms_bwd(h2_ref[...], gp_ref[...], da)
        dg_ref[ROW_MLP_PRE:ROW_MLP_PRE + 1, :] += dgp
        dh2_ref[...] = dh3 + dh

    row = lambda w: pl.BlockSpec((tm, w), lambda i: (i, 0))
    return _call(
        body, exch,
        name=name,
        grid=(lp // tm,),
        in_specs=[
            row(D_MODEL), row(D_MODEL), row(D_FF), row(D_MODEL), _full((D_FF, D_MODEL)), _full((D_FF, D_MODEL)),
            _full((1, D_MODEL)), _full((1, D_MODEL)),
        ],
        out_specs=[row(D_MODEL), row(D_FF), row(D_MODEL), _full_out((8, D_MODEL))],
        out_shape=[
            jax.ShapeDtypeStruct((lp, D_MODEL), BF),
            jax.ShapeDtypeStruct((lp, D_FF), BF),
            jax.ShapeDtypeStruct((lp, D_MODEL), F32),
            jax.ShapeDtypeStruct((8, D_MODEL), F32),
        ],
        compiler_params=_params(),
    )(dh3, f, up, h2, w_down, w_up_t, g_post, g_pre)


def _mlp_bwd_dw(up, df, dup, a2, tm, name):
    lp = up.shape[0]
    nt = lp // tm
    nj = D_FF // D_MODEL

    def body(up_ref, df_ref, dup_ref, a_ref, dwd_ref, dwu_ref, accd, accu):
        i = pl.program_id(1)

        @pl.when(i == 0)
        def _():
            accd[...] = jnp.zeros_like(accd)
            accu[...] = jnp.zeros_like(accu)

        act = jnp.square(jnp.maximum(up_ref[...].astype(F32), 0.0)).astype(BF)
        accd[...] += _dot_tn(act, df_ref[...])
        accu[...] += _dot_tn(dup_ref[...], a_ref[...])

        @pl.when(i == nt - 1)
        def _():
            dwd_ref[...] = accd[...].astype(BF)
            dwu_ref[...] = accu[...].astype(BF)

    return pl.pallas_call(
        body,
        name=name,
        grid=(nj, nt),
        in_specs=[
            pl.BlockSpec((tm, D_MODEL), lambda j, i: (i, j)),
            pl.BlockSpec((tm, D_MODEL), lambda j, i: (i, 0)),
            pl.BlockSpec((tm, D_MODEL), lambda j, i: (i, j)),
            pl.BlockSpec((tm, D_MODEL), lambda j, i: (i, 0)),
        ],
        out_specs=[pl.BlockSpec((D_MODEL, D_MODEL), lambda j, i: (j, 0)), pl.BlockSpec((D_MODEL, D_MODEL), lambda j, i: (j, 0))],
        out_shape=[jax.ShapeDtypeStruct((D_FF, D_MODEL), BF), jax.ShapeDtypeStruct((D_FF, D_MODEL), BF)],
        scratch_shapes=[pltpu.VMEM((D_MODEL, D_MODEL), F32), pltpu.VMEM((D_MODEL, D_MODEL), F32)],
        compiler_params=_params(("arbitrary", "arbitrary")),
    )(up, df, dup, a2)


def _mix_out_bwd(dh2, z, y_attn, bch, w_out, g_post, g_a, g_c, conv_w, tm, name, exch=None):
    lp = dh2.shape[0]
    nt = lp // tm
    halo = 16

    def body(dh2_ref, z_ref, ya_ref, bch_ref, halo_ref, w_ref, gp_ref, ga_ref, gc_ref, cw_ref,
             dz_ref, dya_ref, dbch_ref, dg_ref, ext, ext_u):
        i = pl.program_id(0)
        dcw_ref = dg_ref.at[ROW_CONV:ROW_CONV + 3, 0:CONV_W]

        @pl.when(i == 0)
        def _():
            ext[tm:tm + 8, :] = jnp.zeros((8, CONV_W), F32)
            dg_ref[...] = jnp.zeros((8, D_MODEL), F32)

        dz, dgp = _rms_bwd(z_ref[...], gp_ref[...], dh2_ref[...])
        dg_ref[ROW_MIX_POST:ROW_MIX_POST + 1, :] += dgp
        dz = dz.astype(BF)
        dz_ref[...] = dz
        dya_n = _dot_nt(dz, w_ref[0:ATTN_W, :])
        dyb_n = _dot_nt(dz, w_ref[ATTN_W:, :])
        dya, dga = _rms_bwd(ya_ref[...].astype(F32), ga_ref[...], dya_n)
        dg_ref[ROW_GROUP_G:ROW_GROUP_G + 1, 0:ATTN_W] += dga
        dya_ref[...] = dya
        b = bch_ref[:, 0:CONV_W].astype(F32)
        c = bch_ref[:, CONV_W:2 * CONV_W].astype(F32)
        hc = bch_ref[:, 2 * CONV_W:3 * CONV_W].astype(F32)
        u = c * hc
        u_before = halo_ref[:, CONV_W:2 * CONV_W].astype(F32) * halo_ref[:, 2 * CONV_W:3 * CONV_W].astype(F32)
        ext_u[0:halo, :] = jnp.where(i < nt - 1, u_before, 0.0)
        ext_u[halo:halo + tm, :] = u
        yc_v = (cw_ref[0:1, :] * ext_u[halo - 2:halo - 2 + tm, :] + cw_ref[1:2, :] * ext_u[halo - 1:halo - 1 + tm, :]
                + cw_ref[2:3, :] * u)
        dyconv, dgc = _rms_bwd(b * yc_v, gc_ref[...], dyb_n)
        dg_ref[ROW_GROUP_G:ROW_GROUP_G + 1, ATTN_W:] += dgc
        dbch_ref[:, 0:CONV_W] = (dyconv * yc_v).astype(BF)
        dyc = dyconv * b
        ext[0:tm, :] = dyc
        d1 = ext[1:1 + tm, :]
        d2 = ext[2:2 + tm, :]
        du = cw_ref[2:3, :] * dyc + cw_ref[1:2, :] * d1 + cw_ref[0:1, :] * d2
        ext[tm:tm + 8, :] = dyc[0:8, :]
        dbch_ref[:, CONV_W:2 * CONV_W] = (du * hc).astype(BF)
        dbch_ref[:, 2 * CONV_W:3 * CONV_W] = (du * c).astype(BF)
        dcw_ref[0:1, :] += jnp.sum(u * d2, axis=0, keepdims=True)
        dcw_ref[1:2, :] += jnp.sum(u * d1, axis=0, keepdims=True)
        dcw_ref[2:3, :] += jnp.sum(u * dyc, axis=0, keepdims=True)

    row = lambda w: pl.BlockSpec((tm, w), lambda i: (nt - 1 - i, 0))
    before = pl.BlockSpec((halo, 3 * CONV_W), lambda i: (jnp.maximum((nt - 1 - i) * (tm // halo) - 1, 0), 0))
    return _call(
        body, exch,
        name=name,
        grid=(nt,),
        in_specs=[
            row(D_MODEL), row(D_MODEL), row(ATTN_W), row(3 * CONV_W), before, _full((D_MODEL, D_MODEL)),
            _full((1, D_MODEL)), _full((1, ATTN_W)), _full((1, CONV_W)), _full((8, CONV_W)),
        ],
        out_specs=[row(D_MODEL), row(ATTN_W), row(3 * CONV_W), _full_out((8, D_MODEL))],
        out_shape=[
            jax.ShapeDtypeStruct((lp, D_MODEL), BF),
            jax.ShapeDtypeStruct((lp, ATTN_W), F32),
            jax.ShapeDtypeStruct((lp, 3 * CONV_W), BF),
            jax.ShapeDtypeStruct((8, D_MODEL), F32),
        ],
        scratch_shapes=[pltpu.VMEM((tm + 8, CONV_W), F32), pltpu.VMEM((tm + halo, CONV_W), F32)],
        compiler_params=_params(),
    )(dh2, z, y_attn, bch, bch, w_out, g_post, g_a, g_c, conv_w)


def _attn_bwd(qkv, o, do, probs, p_sink, rope, name, exch=None):
    lp = qkv.shape[0]
    nb = lp // BLOCK

    def body(q_ref, kvc_ref, kvp_ref, o_ref, do_ref, p_ref, ps_ref, cq_ref, s1q_ref, s2q_ref, ck_ref, s1k_ref, s2k_ref,
             dq_ref, dkv_ref, dsink_ref, carry):
        i = pl.program_id(0)

        @pl.when(i == 0)
        def _():
            carry[...] = jnp.zeros_like(carry)
            dsink_ref[...] = jnp.zeros((8, 128), F32)

        def finish(tot):
            dk = _rope_t(tot[:, :128], ck_ref[...], s1k_ref[...], s2k_ref[...])
            dkv_ref[:, 0:128] = dk.astype(BF)
            dkv_ref[:, 128:256] = tot[:, 128:].astype(BF)

        @pl.when(i < nb)
        def _():
            tri, _ = _fold_masks(i)
            kvc, kvp = kvc_ref[...], kvp_ref[...]
            kk = jnp.concatenate([kvp[:, :128], kvc[:, :128]], axis=0)
            vv = jnp.concatenate([kvp[:, 128:], kvc[:, 128:]], axis=0)
            lane = lax.broadcasted_iota(jnp.int32, (BLOCK, 128), 1)
            lane2 = lax.broadcasted_iota(jnp.int32, (2 * BLOCK, 128), 1)
            rope_q = (cq_ref[...], s1q_ref[...], s2q_ref[...])
            deltas = jnp.zeros((BLOCK, 128), F32)
            folded = []
            for kvh in range(2):
                c0 = 256 * kvh
                q2 = jnp.concatenate([q_ref[:, c0:c0 + 128], q_ref[:, c0 + 128:c0 + 256]], axis=0)
                do2 = jnp.concatenate([do_ref[:, c0:c0 + 128], do_ref[:, c0 + 128:c0 + 256]], axis=0)
                o2 = jnp.concatenate([o_ref[:, c0:c0 + 128], o_ref[:, c0 + 128:c0 + 256]], axis=0).astype(F32)
                k4, v4 = _kv_operand(kk, kvh), _kv_operand(vv, kvh)
                prod = do2 * o2
                dob = do2.astype(BF)
                dp = _dot_nt(dob, v4)
                ds, pb = [], []
                for half in range(2):
                    heads = [4 * kvh + 2 * pair + half for pair in range(2)]
                    p = jnp.concatenate([p_ref[:, 128 * h:128 * (h + 1)] for h in heads], axis=0)
                    sel = (lane2 < HEAD_DIM) if half == 0 else (lane2 >= HEAD_DIM)
                    delta = jnp.sum(jnp.where(sel, prod, 0.0), axis=-1, keepdims=True)
                    dp_h = dp[:, 2 * half * BLOCK:2 * (half + 1) * BLOCK]
                    ds.append((p.astype(F32) * (jnp.where(tri, dp_h[:, :BLOCK], dp_h[:, BLOCK:]) - delta)).astype(BF))
                    pb.append(p)
                    for pair in range(2):
                        deltas = jnp.where(lane == heads[pair], delta[pair * BLOCK:(pair + 1) * BLOCK], deltas)
                ds4, p4 = _split4(ds, tri), _split4(pb, tri)
                dq2 = _dot(ds4, k4) * SCALE
                dq_ref[:, c0:c0 + 128] = _rope_t(dq2[:BLOCK], *rope_q).astype(BF)
                dq_ref[:, c0 + 128:c0 + 256] = _rope_t(dq2[BLOCK:], *rope_q).astype(BF)
                rk, rv = _dot_tn(ds4, q2), _dot_tn(p4, dob)
                own = (lane < HEAD_DIM) if kvh == 0 else (lane >= HEAD_DIM)
                group = []
                for r in (rk, rv):
                    for blk in range(2):
                        t = jnp.where(lane < HEAD_DIM, r[blk * BLOCK:(blk + 1) * BLOCK], r[(2 + blk) * BLOCK:(3 + blk) * BLOCK])
                        group.append(jnp.where(own, t + pltpu.roll(t, HEAD_DIM, 1), 0.0))
                folded.append(group)
            dsink_ref[ROW_SINK:ROW_SINK + 1, :] -= jnp.sum(ps_ref[...] * deltas, axis=0, keepdims=True)
            dk_p, dk_c, dv_p, dv_c = [folded[0][t] + folded[1][t] for t in range(4)]
            finish(carry[...] + jnp.concatenate([dk_p, dv_p], axis=1))
            carry[...] = jnp.concatenate([dk_c, dv_c], axis=1)

        @pl.when(i == nb)
        def _():
            finish(carry[...])

    qi = lambda i: jnp.minimum(i, nb - 1)
    ki = lambda i: jnp.maximum(i - 1, 0)
    tab_q = pl.BlockSpec((BLOCK, 128), lambda i: (qi(i), 0))
    tab_k = pl.BlockSpec((BLOCK, 128), lambda i: (ki(i), 0))
    return _call(
        body, exch,
        name=name,
        grid=(nb + 1,),
        in_specs=[
            pl.BlockSpec((BLOCK, ATTN_W), lambda i: (qi(i), 0)),
            pl.BlockSpec((BLOCK, 256), lambda i: (qi(i), 2)),
            pl.BlockSpec((BLOCK, 256), lambda i: (jnp.maximum(qi(i) - 1, 0), 2)),
            pl.BlockSpec((BLOCK, ATTN_W), lambda i: (qi(i), 0)),
            pl.BlockSpec((BLOCK, ATTN_W), lambda i: (qi(i), 0)),
            pl.BlockSpec((BLOCK, N_Q_HEADS * BLOCK), lambda i: (qi(i), 0)),
            tab_q, tab_q, tab_q, tab_q, tab_k, tab_k, tab_k,
        ],
        out_specs=[
            pl.BlockSpec((BLOCK, ATTN_W), lambda i: (qi(i), 0)),
            pl.BlockSpec((BLOCK, 256), lambda i: (ki(i), 0)),
            pl.BlockSpec((8, 128), lambda i: (0, 0)),
        ],
        out_shape=[
            jax.ShapeDtypeStruct((lp, ATTN_W), BF),
            jax.ShapeDtypeStruct((lp, 256), BF),
            jax.ShapeDtypeStruct((8, 128), F32),
        ],
        scratch_shapes=[pltpu.VMEM((BLOCK, 256), F32)],
        compiler_params=_params(),
    )(qkv, qkv, qkv, o, do, probs, p_sink, *rope, *rope)


def _in_proj_bwd_dx(dq, dkv, dbch, w_in_t, h, dh2, g, tm, name, exch=None):
    lp = h.shape[0]

    def body(dq_ref, dkv_ref, dbch_ref, w_ref, h_ref, dh2_ref, g_ref, dh_ref, dg_ref):
        i = pl.program_id(0)

        @pl.when(i == 0)
        def _():
            dg_ref[...] = jnp.zeros((8, D_MODEL), F32)

        da = _dot(dq_ref[...], w_ref[0:512, :]) + _dot(dkv_ref[...], w_ref[512:768, :]) + _dot(dbch_ref[...], w_ref[768:, :])
        dh, dg = _rms_bwd(h_ref[...], g_ref[...], da)
        dg_ref[ROW_MIX_PRE:ROW_MIX_PRE + 1, :] += dg
        dh_ref[...] = dh2_ref[...] + dh

    row = lambda w: pl.BlockSpec((tm, w), lambda i: (i, 0))
    return _call(
        body, exch,
        name=name,
        grid=(lp // tm,),
        in_specs=[row(ATTN_W), row(256), row(3 * CONV_W), _full((IN_W, D_MODEL)), row(D_MODEL), row(D_MODEL), _full((1, D_MODEL))],
        out_specs=[row(D_MODEL), _full_out((8, D_MODEL))],
        out_shape=[jax.ShapeDtypeStruct((lp, D_MODEL), F32), jax.ShapeDtypeStruct((8, D_MODEL), F32)],
        compiler_params=_params(),
    )(dq, dkv, dbch, w_in_t, h, dh2, g)


def _mix_bwd_dw(dq, dkv, dbch, a, y, dz, tm, name, exch=None):
    lp = a.shape[0]
    nt = lp // tm

    def body(dq_ref, dkv_ref, dbch_ref, a_ref, y_ref, dz_ref, dwi_ref, dwo_ref, acci, acco):
        i = pl.program_id(0)

        @pl.when(i == 0)
        def _():
            acci[...] = jnp.zeros_like(acci)
            acco[...] = jnp.zeros_like(acco)

        a_v = a_ref[...]
        acci[0:512, :] += _dot_tn(dq_ref[...], a_v)
        acci[512:768, :] += _dot_tn(dkv_ref[...], a_v)
        acci[768:, :] += _dot_tn(dbch_ref[...], a_v)
        acco[...] += _dot_tn(y_ref[...], dz_ref[...])

        @pl.when(i == nt - 1)
        def _():
            dwi_ref[...] = acci[...].astype(BF)
            dwo_ref[...] = acco[...].astype(BF)

    row = lambda w: pl.BlockSpec((tm, w), lambda i: (i, 0))
    return _call(
        body, exch,
        name=name,
        grid=(nt,),
        in_specs=[row(ATTN_W), row(256), row(3 * CONV_W), row(D_MODEL), row(D_MODEL), row(D_MODEL)],
        out_specs=[_full_out((IN_W, D_MODEL)), _full_out((D_MODEL, D_MODEL))],
        out_shape=[jax.ShapeDtypeStruct((IN_W, D_MODEL), BF), jax.ShapeDtypeStruct((D_MODEL, D_MODEL), BF)],
        scratch_shapes=[pltpu.VMEM((IN_W, D_MODEL), F32), pltpu.VMEM((D_MODEL, D_MODEL), F32)],
        compiler_params=_params(),
    )(dq, dkv, dbch, a, y, dz)


def _mesh_place():
    x, y, c = lax.axis_index("x"), lax.axis_index("y"), lax.axis_index("c")
    return x, y, c, 4 * x + 2 * y + c


def _peer(x, y, c, k):
    px = 1 - x if k & 4 else x
    py = 1 - y if k & 2 else y
    pc = 1 - c if k & 1 else c
    return (px, py, pc), 4 * px + 2 * py + pc


SIBLING = 1
SAME_CORE = (2, 4, 6)
OTHER_CORE = (3, 5, 7)


class _Exchange:
    def __init__(self, pieces):
        self.srcs = [s for s, _ in pieces]
        self.to_all = [g for _, g in pieces]
        self.n = len(pieces)
        self.land_shapes = [
            jax.ShapeDtypeStruct((N_DEV,) + (s.shape if g else s.shape[1:]), s.dtype) for s, g in pieces]
        self.sem_shapes = [pltpu.SemaphoreType.DMA((self.n, N_DEV - 1)), pltpu.SemaphoreType.DMA((self.n, N_DEV - 1)),
                           pltpu.SemaphoreType.DMA((self.n,))]
        self.forwards = any(self.to_all)

    def _ops(self, srcs, lands, sems):
        send_sems, recv_sems, local_sems = sems
        x, y, c, me = _mesh_place()

        def remote(p, k, src, slot, to):
            return pltpu.make_async_remote_copy(
                src_ref=src, dst_ref=lands[p].at[slot], send_sem=send_sems.at[p, k - 1], recv_sem=recv_sems.at[p, k - 1],
                device_id=to, device_id_type=MESH)

        def own(p):
            return pltpu.make_async_copy(srcs[p] if self.to_all[p] else srcs[p].at[me], lands[p].at[me], local_sems.at[p])

        def direct(p, k):
            peer, pidx = _peer(x, y, c, k)
            return remote(p, k, srcs[p] if self.to_all[p] else srcs[p].at[pidx], me, peer)

        def forward(p, k):
            sibling, _ = _peer(x, y, c, SIBLING)
            _, origin = _peer(x, y, c, k ^ SIBLING)
            return remote(p, k, lands[p].at[origin], origin, sibling)

        def arrival(p, k):
            peer, pidx = _peer(x, y, c, k)
            return remote(p, k, lands[p].at[pidx], pidx, peer)

        return own, direct, forward, arrival

    def start(self, srcs, lands, sems):
        own, direct, _, _ = self._ops(srcs, lands, sems)
        for p in range(self.n):
            own(p).start()
            for k in ((SIBLING,) + SAME_CORE) if self.to_all[p] else range(1, N_DEV):
                direct(p, k).start()

    def forward(self, srcs, lands, sems):
        _, _, forward, arrival = self._ops(srcs, lands, sems)
        for p in range(self.n):
            if self.to_all[p]:
                for k in SAME_CORE:
                    arrival(p, k).wait_recv()
                    forward(p, k ^ SIBLING).start()

    def finish(self, srcs, lands, sems):
        own, direct, forward, arrival = self._ops(srcs, lands, sems)
        for p in range(self.n):
            for k in ((SIBLING,) + OTHER_CORE) if self.to_all[p] else range(1, N_DEV):
                arrival(p, k).wait_recv()
        for p in range(self.n):
            for k in range(1, N_DEV):
                (forward(p, k) if self.to_all[p] and k in OTHER_CORE else direct(p, k)).wait_send()
            own(p).wait()


def _call(body, exch, *, name, grid, in_specs, out_specs, out_shape, scratch_shapes=(), compiler_params, after=None):
    if exch is None:
        return pl.pallas_call(body, name=name, grid=grid, in_specs=in_specs, out_specs=out_specs, out_shape=out_shape,
                              scratch_shapes=scratch_shapes, compiler_params=compiler_params)
    n_in, n_out, n_scr, n_x = len(in_specs), len(out_shape), len(scratch_shapes), exch.n
    steps = math.prod(grid)

    def carrying(*refs):
        a, b, c, d, e = n_in, n_in + n_x, n_in + n_x + n_out, n_in + 2 * n_x + n_out, n_in + 2 * n_x + n_out + n_scr
        ins, srcs, outs, lands, scr, sems = refs[:a], refs[a:b], refs[b:c], refs[c:d], refs[d:e], refs[e:]
        step = functools.reduce(lambda acc, t: acc * grid[t] + pl.program_id(t), range(len(grid)), 0)

        @pl.when(step == 0)
        def _():
            exch.start(srcs, lands, sems)

        body(*ins, *outs, *scr)

        if exch.forwards:
            @pl.when(step == max(0, steps - 1 - (steps + 7) // 8))
            def _():
                exch.forward(srcs, lands, sems)

        @pl.when(step == steps - 1)
        def _():
            exch.finish(srcs, lands, sems)
            if after is not None:
                after(lands, *ins, *outs, *scr)

    hbm = pl.BlockSpec(memory_space=pl.ANY)
    call = pl.pallas_call(
        carrying, name=name, grid=grid, in_specs=list(in_specs) + [hbm] * n_x, out_specs=list(out_specs) + [hbm] * n_x,
        out_shape=list(out_shape) + exch.land_shapes, scratch_shapes=list(scratch_shapes) + exch.sem_shapes,
        compiler_params=compiler_params)

    def run(*args):
        res = call(*args, *exch.srcs)
        return list(res[:n_out]), list(res[n_out:])

    return run


def _sum_small(part):
    def body(part_ref, out_ref, land, send_sems, recv_sems):
        x, y, c, me = _mesh_place()
        land[me] = part_ref[...]
        sent = []
        for k in range(1, N_DEV):
            peer, _ = _peer(x, y, c, k)
            cp = pltpu.make_async_remote_copy(
                src_ref=part_ref, dst_ref=land.at[me], send_sem=send_sems.at[k - 1], recv_sem=recv_sems.at[k - 1],
                device_id=peer, device_id_type=MESH)
            cp.start()
            sent.append(cp)
        for k in range(1, N_DEV):
            peer, pidx = _peer(x, y, c, k)
            pltpu.make_async_remote_copy(
                src_ref=part_ref, dst_ref=land.at[pidx], send_sem=send_sems.at[k - 1], recv_sem=recv_sems.at[k - 1],
                device_id=peer, device_id_type=MESH).wait_recv()
        for cp in sent:
            cp.wait_send()
        acc = land[0]
        for d in range(1, N_DEV):
            acc = acc + land[d]
        out_ref[...] = acc

    vmem = pl.BlockSpec(memory_space=pltpu.VMEM)
    return pl.pallas_call(
        body,
        name="sum_small",
        in_specs=[vmem],
        out_specs=vmem,
        out_shape=jax.ShapeDtypeStruct(part.shape, F32),
        scratch_shapes=[pltpu.VMEM((N_DEV,) + part.shape, F32), pltpu.SemaphoreType.DMA((N_DEV - 1,)),
                        pltpu.SemaphoreType.DMA((N_DEV - 1,))],
    )(part)


def _adamw(w, g, m, v):
    m = ADAM_B1 * m + (1.0 - ADAM_B1) * g
    v = ADAM_B2 * v + (1.0 - ADAM_B2) * jnp.square(g)
    m_hat = m / (1.0 - ADAM_B1 ** ADAM_STEP)
    v_hat = v / (1.0 - ADAM_B2 ** ADAM_STEP)
    delta = -ADAM_LR * (m_hat / (jnp.sqrt(v_hat) + ADAM_EPS) + ADAM_WD * w)
    return delta, m, v


def _landed_specs(tr, wd):
    return [pl.BlockSpec((N_DEV, tr, wd), lambda l, i, ll=ll: (0, jnp.where(l == ll, i, 0), 0)) for ll in range(DEPTH)]


def _device_sum(r_ref):
    acc = r_ref[0].astype(F32)
    for d in range(1, N_DEV):
        acc = acc + r_ref[d].astype(F32)
    return acc


def _sum_parts(recv, tr, name):
    _, r, wd = recv[0].shape

    def body(*refs):
        g_ref = refs[DEPTH]
        for ll in range(DEPTH):
            @pl.when(pl.program_id(0) == ll)
            def _(ll=ll):
                g_ref[0] = _device_sum(refs[ll])

    return pl.pallas_call(
        body,
        name=name,
        grid=(DEPTH, r // tr),
        in_specs=_landed_specs(tr, wd),
        out_specs=pl.BlockSpec((1, tr, wd), lambda l, i: (l, i, 0)),
        out_shape=jax.ShapeDtypeStruct((DEPTH, r, wd), F32),
        compiler_params=_params(("arbitrary", "arbitrary")),
    )(*recv)


def _sum_adamw(recv, w, m, v, tr, name):
    _, r, wd = recv[0].shape

    def body(*refs):
        w_ref, m_ref, v_ref, g_ref, d_ref, mo_ref, vo_ref = refs[DEPTH:]
        for ll in range(DEPTH):
            @pl.when(pl.program_id(0) == ll)
            def _(ll=ll):
                g = _device_sum(refs[ll])
                g_ref[0] = g
                d_ref[0], mo_ref[0], vo_ref[0] = _adamw(w_ref[0], g, m_ref[0], v_ref[0])

    blk = pl.BlockSpec((1, tr, wd), lambda l, i: (l, i, 0))
    shape = jax.ShapeDtypeStruct((DEPTH, r, wd), F32)
    return pl.pallas_call(
        body,
        name=name,
        grid=(DEPTH, r // tr),
        in_specs=_landed_specs(tr, wd) + [blk, blk, blk],
        out_specs=[blk] * 4,
        out_shape=[shape] * 4,
        compiler_params=_params(("arbitrary", "arbitrary")),
    )(*recv, w, m, v)


def _adamw_rows(w, g, m, v, tr, name):
    _, r, wd = w.shape

    def body(w_ref, g_ref, m_ref, v_ref, d_ref, mo_ref, vo_ref):
        d_ref[0], mo_ref[0], vo_ref[0] = _adamw(w_ref[0], g_ref[0], m_ref[0], v_ref[0])

    blk = pl.BlockSpec((1, tr, wd), lambda l, i: (l, i, 0))
    shape = jax.ShapeDtypeStruct(w.shape, F32)
    return pl.pallas_call(
        body,
        name=name,
        grid=(DEPTH, r // tr),
        in_specs=[blk] * 4,
        out_specs=[blk] * 3,
        out_shape=[shape] * 3,
        compiler_params=_params(("arbitrary", "arbitrary")),
    )(w, g, m, v)


def _adamw_small(ws, gs, ms, vs):
    n = len(ws)

    def body(*refs):
        w_r, g_r, m_r, v_r = refs[:n], refs[n:2 * n], refs[2 * n:3 * n], refs[3 * n:4 * n]
        d_o, m_o, v_o = refs[4 * n:5 * n], refs[5 * n:6 * n], refs[6 * n:7 * n]
        for t in range(n):
            d_o[t][...], m_o[t][...], v_o[t][...] = _adamw(w_r[t][...], g_r[t][...], m_r[t][...], v_r[t][...])

    vmem = pl.BlockSpec(memory_space=pltpu.VMEM)
    shapes = [jax.ShapeDtypeStruct(w.shape, F32) for w in ws]
    outs = pl.pallas_call(
        body,
        name="adamw_small",
        in_specs=[vmem] * (4 * n),
        out_specs=[vmem] * (3 * n),
        out_shape=shapes * 3,
    )(*ws, *gs, *ms, *vs)
    return outs[:n], outs[n:2 * n], outs[2 * n:]


def kernel(x, meta_tokens, mix_pre_g, w_in, conv_w, sinks, attn_out_g, conv_out_g, w_out, mix_post_g, mlp_pre_g, w_up, w_down, mlp_post_g, loss_target, m_meta_tokens, m_mix_pre_g, m_w_in, m_conv_w, m_sinks, m_attn_out_g, m_conv_out_g, m_w_out, m_mix_post_g, m_mlp_pre_g, m_w_up, m_w_down, m_mlp_post_g, v_meta_tokens, v_mix_pre_g, v_w_in, v_conv_w, v_sinks, v_attn_out_g, v_conv_out_g, v_w_out, v_mix_post_g, v_mlp_pre_g, v_w_up, v_w_down, v_mlp_post_g):
    seq = x.shape[1]
    lp = BLOCK + seq
    tm = _row_tile(lp)
    tm_mlp = _row_tile(lp, (320, 256, 128))
    tm_dw_mlp = _row_tile(lp, (1664, 1040, 640, 384, 256, 128))
    tm_dw_mix = _row_tile(lp, (832, 640, 384, 256, 128))
    me = 4 * lax.axis_index("x") + 2 * lax.axis_index("y") + lax.axis_index("c")
    cshard = CONV_W // N_DEV
    mshard = D_MODEL // N_DEV

    gather_with = {
        ("in_proj_fwd", 0): [("up", 0)], ("attn_fwd", 0): [("out", 0), ("down", 0)], ("mix_out_fwd", 0): [("in", 1)],
        ("mlp_fwd", 0): [("out", 1), ("up", 1), ("down", 1)],
    }
    scatter_with = {
        ("attn_bwd", 1): [("down", 1)], ("mlp_bwd_dx", 0): [("up", 1), ("in", 1), ("out", 1)],
        ("mix_out_bwd", 0): [("up", 0)], ("attn_bwd", 0): [("down", 0)], ("in_proj_bwd_dx", 0): [("in", 0), ("out", 0)],
    }
    shard = {"in": jnp.swapaxes(w_in, 1, 2).astype(BF), "out": w_out.astype(BF),
             "up": jnp.swapaxes(w_up, 1, 2).astype(BF), "down": w_down.astype(BF)}
    weight = {}
    grad = {}
    landed = {}

    def run(fn, kind, l, *args):
        key, name = (kind, l), f"{kind}_{l}"
        if key in gather_with:
            blocks = gather_with[key]
            outs, lands = fn(*args, name, _Exchange([(shard[n][k], True) for n, k in blocks]))
            for b, land in zip(blocks, lands):
                weight[b] = land.reshape(-1, D_MODEL)
            return outs
        if key in scatter_with:
            blocks = scatter_with[key]
            outs, lands = fn(*args, name, _Exchange([(grad[b].reshape(N_DEV, -1, D_MODEL), False) for b in blocks]))
            landed.update(zip(blocks, lands))
            return outs
        return fn(*args, name)

    small = jnp.zeros((24, 128), F32)
    small = small.at[0:N_META, :].set(meta_tokens)
    small = small.at[N_META:N_META + 6, 0:cshard].set(conv_w.reshape(6, cshard))
    h, rope, (first_in, g_small) = _build_h(
        x[0], _rope_table(lp), tm, _Exchange([(shard["in"][0], True), (small, True)]), 1, "build_h")
    weight[("in", 0)] = first_in.reshape(-1, D_MODEL)
    cw = g_small[:, N_META:N_META + 6, 0:cshard].reshape(N_DEV, DEPTH, 3, cshard)
    cw = jnp.transpose(cw, (1, 2, 0, 3)).reshape(DEPTH, 3, CONV_W)
    conv_full = jnp.concatenate([cw, jnp.zeros((DEPTH, 5, CONV_W), F32)], axis=1)

    row1 = lambda a, l: a[l].reshape(1, -1)

    saved = []
    for l in range(DEPTH):
        a, qkv, bch = run(_in_proj_fwd, "in_proj_fwd", l, h, row1(mix_pre_g, l), weight[("in", l)], rope, tm)
        y_attn, probs, p_sink = run(_attn_fwd, "attn_fwd", l, qkv, row1(sinks, l))
        y, z, h2 = run(_mix_out_fwd, "mix_out_fwd", l, bch, y_attn, h, conv_full[l], row1(attn_out_g, l),
                           row1(conv_out_g, l), weight[("out", l)], row1(mix_post_g, l), tm)
        mlp = _mlp_fwd if l < DEPTH - 1 else functools.partial(_mlp_fwd, target=loss_target[0])
        a2, up, f, *rest = run(mlp, "mlp_fwd", l, h2, row1(mlp_pre_g, l), weight[("up", l)], weight[("down", l)],
                               row1(mlp_post_g, l), tm_mlp)
        saved.append((h, a, qkv, bch, y_attn, probs, p_sink, y, z, h2, a2, up, f))
        h = rest[0]
    dh, loss_part = rest[0], rest[1][0, 0] * (0.5 / D_MODEL)

    gsmall = [None] * DEPTH
    for l in reversed(range(DEPTH)):
        h0, a, qkv, bch, y_attn, probs, p_sink, y, z, h2, a2, up, f = saved[l]
        df, dup, dh2, dg_mlp = run(_mlp_bwd_dx, "mlp_bwd_dx", l, dh, f, up, h2, weight[("down", l)], weight[("up", l)],
                                   row1(mlp_post_g, l), row1(mlp_pre_g, l), tm_mlp)
        grad[("down", l)], grad[("up", l)] = _mlp_bwd_dw(up, df, dup, a2, tm_dw_mlp, f"mlp_bwd_dw_{l}")
        dz, dya, dbch, dg_mix = run(_mix_out_bwd, "mix_out_bwd", l, dh2, z, y_attn, bch, weight[("out", l)],
                                    row1(mix_post_g, l), row1(attn_out_g, l), row1(conv_out_g, l), conv_full[l], tm)
        dq, dkv, dsink = run(_attn_bwd, "attn_bwd", l, qkv, y_attn, dya, probs, p_sink, rope)
        grad[("in", l)], grad[("out", l)] = run(_mix_bwd_dw, "mix_bwd_dw", l, dq, dkv, dbch, a, y, dz, tm_dw_mix)
        dh, dg_in = run(_in_proj_bwd_dx, "in_proj_bwd_dx", l, dq, dkv, dbch, weight[("in", l)], h0, dh2,
                        row1(mix_pre_g, l), tm)
        tile_a = dg_mlp + dg_in + jnp.pad(dsink, ((0, 0), (0, D_MODEL - 128)))
        gsmall[l] = (tile_a, dg_mix)
    grad_x = dh[BLOCK:][None]

    loss_tile = jnp.zeros((8, D_MODEL), F32).at[ROW_LOSS, 0].set(loss_part)
    tot = _sum_small(jnp.concatenate(
        [gsmall[0][0] + loss_tile, gsmall[0][1], gsmall[1][0], gsmall[1][1], dh[LEAD_PAD:BLOCK]], axis=0))
    loss = tot[ROW_LOSS, 0]
    ta = [tot[16 * l:16 * l + 8] for l in range(DEPTH)]
    tb = [tot[16 * l + 8:16 * l + 16] for l in range(DEPTH)]
    pick = lambda tiles, r0, r1, c0, c1: jnp.stack([t[r0:r1, c0:c1] for t in tiles])
    g_mlp_post = pick(ta, ROW_MLP_POST, ROW_MLP_POST + 1, 0, D_MODEL).reshape(DEPTH, D_MODEL)
    g_mlp_pre = pick(ta, ROW_MLP_PRE, ROW_MLP_PRE + 1, 0, D_MODEL).reshape(DEPTH, D_MODEL)
    g_mix_pre = pick(ta, ROW_MIX_PRE, ROW_MIX_PRE + 1, 0, D_MODEL).reshape(DEPTH, D_MODEL)
    g_sinks = pick(ta, ROW_SINK, ROW_SINK + 1, 0, N_Q_HEADS).reshape(DEPTH, N_Q_HEADS)
    g_mix_post = pick(tb, ROW_MIX_POST, ROW_MIX_POST + 1, 0, D_MODEL).reshape(DEPTH, D_MODEL)
    g_attn_out = pick(tb, ROW_GROUP_G, ROW_GROUP_G + 1, 0, ATTN_W).reshape(DEPTH, ATTN_W)
    g_conv_out = pick(tb, ROW_GROUP_G, ROW_GROUP_G + 1, ATTN_W, D_MODEL).reshape(DEPTH, CONV_W)
    g_conv_full = pick(tb, ROW_CONV, ROW_CONV + 3, 0, CONV_W)
    g_conv = lax.dynamic_slice_in_dim(g_conv_full, me * cshard, cshard, axis=2)
    g_meta = lax.dynamic_slice_in_dim(tot[16 * DEPTH:16 * DEPTH + N_META], me * mshard, mshard, axis=1)

    r_in, r_out, r_up, r_down = [[landed[(n, l)] for l in range(DEPTH)] for n in ("in", "out", "up", "down")]
    g_w_in = jnp.swapaxes(_sum_parts(r_in, 96, "sum_w_in"), 1, 2)
    g_w_up = jnp.swapaxes(_sum_parts(r_up, 128, "sum_w_up"), 1, 2)
    d_w_in, nm_w_in, nv_w_in = _adamw_rows(w_in, g_w_in, m_w_in, v_w_in, 256, "adamw_w_in")
    d_w_up, nm_w_up, nv_w_up = _adamw_rows(w_up, g_w_up, m_w_up, v_w_up, 256, "adamw_w_up")
    g_w_out, d_w_out, nm_w_out, nv_w_out = _sum_adamw(r_out, w_out, m_w_out, v_w_out, 128, "adamw_w_out")
    g_w_down, d_w_down, nm_w_down, nv_w_down = _sum_adamw(r_down, w_down, m_w_down, v_w_down, 128, "adamw_w_down")

    ws = [meta_tokens, mix_pre_g, conv_w.reshape(6, cshard), sinks, attn_out_g, conv_out_g, mix_post_g, mlp_pre_g, mlp_post_g]
    gs = [g_meta, g_mix_pre, g_conv.reshape(6, cshard), g_sinks, g_attn_out, g_conv_out, g_mix_post, g_mlp_pre, g_mlp_post]
    ms = [m_meta_tokens, m_mix_pre_g, m_conv_w.reshape(6, cshard), m_sinks, m_attn_out_g, m_conv_out_g, m_mix_post_g,
          m_mlp_pre_g, m_mlp_post_g]
    vs = [v_meta_tokens, v_mix_pre_g, v_conv_w.reshape(6, cshard), v_sinks, v_attn_out_g, v_conv_out_g, v_mix_post_g,
          v_mlp_pre_g, v_mlp_post_g]
    ds, nms, nvs = _adamw_small(ws, gs, ms, vs)

    def order(meta, mix_pre, cv, sk, a_out, c_out, mix_post, mlp_pre, mlp_post, win, wout, wup, wdown):
        return [meta, mix_pre, win, cv.reshape(DEPTH, 3, cshard), sk, a_out, c_out, wout, mix_post, mlp_pre, wup, wdown, mlp_post]

    grads = order(*gs, g_w_in, g_w_out, g_w_up, g_w_down)
    deltas = order(*ds, d_w_in, d_w_out, d_w_up, d_w_down)
    new_m = order(*nms, nm_w_in, nm_w_out, nm_w_up, nm_w_down)
    new_v = order(*nvs, nv_w_in, nv_w_out, nv_w_up, nv_w_down)
    return (loss, grad_x, *grads, *deltas, *new_m, *new_v)
```

```python
import functools
import math

import jax
import jax.numpy as jnp
from jax import lax
from jax.experimental import pallas as pl
from jax.experimental.pallas import tpu as pltpu

F32 = jnp.float32
BF = jnp.bfloat16

D_MODEL = 1024
ATTN_W = 512
CONV_W = 512
KV_W = 128
HEAD_DIM = 64
N_Q_HEADS = 8
ROT_DIM = 16
D_FF = 4096
IN_W = 2304
N_META = 16
BLOCK = 128
LEAD_PAD = BLOCK - N_META
ROPE_THETA = 500000.0
EPS = 1e-6
N_DEV = 8
DEPTH = 2
NEG = -1e30
SCALE = HEAD_DIM ** -0.5

ADAM_LR = 0.001
ADAM_B1 = 0.9
ADAM_B2 = 0.999
ADAM_EPS = 1e-08
ADAM_WD = 0.01
ADAM_STEP = 10

ROW_MLP_POST, ROW_MLP_PRE, ROW_MIX_PRE, ROW_SINK, ROW_LOSS = 0, 1, 2, 3, 4
ROW_MIX_POST, ROW_GROUP_G, ROW_CONV = 0, 1, 2

VMEM_LIMIT = 56 * 1024 * 1024
MESH = pl.DeviceIdType.MESH


def _dot(a, b):
    return jnp.dot(a, b, preferred_element_type=F32)


def _dot_nt(a, b):
    return lax.dot_general(a, b, (((1,), (1,)), ((), ())), preferred_element_type=F32)


def _dot_tn(a, b):
    return lax.dot_general(a, b, (((0,), (0,)), ((), ())), preferred_element_type=F32)


def _rms_fwd(x, g):
    r = lax.rsqrt(jnp.mean(x * x, axis=-1, keepdims=True) + EPS)
    return x * r * g


def _rms_bwd(x, g, dy):
    r = lax.rsqrt(jnp.mean(x * x, axis=-1, keepdims=True) + EPS)
    xh = x * r
    t = dy * g
    dx = r * (t - xh * jnp.mean(t * xh, axis=-1, keepdims=True))
    dg = jnp.sum(dy * xh, axis=0, keepdims=True)
    return dx, dg


def _row_tile(lp, cands=(640, 512, 384, 256, 128)):
    for t in cands:
        if lp % t == 0:
            return t
    raise ValueError(f"row count {lp} is not a multiple of 128")


def _full(shape):
    n = len(shape)
    return pl.BlockSpec(shape, lambda *_: (0,) * n, pipeline_mode=pl.Buffered(1))


def _full_out(shape):
    n = len(shape)
    return pl.BlockSpec(shape, lambda *_: (0,) * n)


def _params(sem=("arbitrary",)):
    return pltpu.CompilerParams(dimension_semantics=sem, vmem_limit_bytes=VMEM_LIMIT)


def _rope_table(lp):
    half = ROT_DIM // 2
    pos = jnp.maximum(jnp.arange(lp) - LEAD_PAD, 0).astype(F32)
    inv_freq = jnp.power(jnp.float32(ROPE_THETA), -jnp.arange(0, ROT_DIM, 2, dtype=F32) / ROT_DIM)
    ang_t = jnp.concatenate([inv_freq, inv_freq])[:, None] * pos[None, :]
    row = lax.broadcasted_iota(jnp.int32, (ROT_DIM, lp), 0)
    cs_t = jnp.where(row < half, jnp.cos(ang_t), jnp.sin(ang_t))
    return jnp.pad(cs_t.T, ((0, 0), (0, 128 - ROT_DIM)))


def _rope_coeffs(t):
    half = ROT_DIM // 2
    lane = lax.broadcasted_iota(jnp.int32, t.shape, 1)
    cos_a = jnp.where(lane < half, t, 0.0)
    sin_a = pltpu.roll(jnp.where((lane >= half) & (lane < ROT_DIM), t, 0.0), 128 - half, 1)
    c = cos_a + pltpu.roll(cos_a, half, 1) + jnp.where((lane >= ROT_DIM) & (lane < HEAD_DIM), 1.0, 0.0)
    s2 = pltpu.roll(sin_a, half, 1)
    both = lambda u: u + pltpu.roll(u, HEAD_DIM, 1)
    return both(c), both(-sin_a), both(s2)


def _rope(t, c, s1, s2):
    return t * c + pltpu.roll(t, BLOCK - 8, 1) * s1 + pltpu.roll(t, 8, 1) * s2


def _rope_t(dt, c, s1, s2):
    return dt * c + pltpu.roll(dt * s1, 8, 1) + pltpu.roll(dt * s2, BLOCK - 8, 1)


def _build_h(x, rope_compact, tm, exch, small_piece, name):
    seq = x.shape[0]
    lp = BLOCK + seq
    nt = lp // tm
    n_sub = tm // BLOCK
    small_shape = exch.land_shapes[small_piece].shape

    def body(*refs):
        h_ref, c_ref, s1_ref, s2_ref = refs[n_sub + 1:n_sub + 5]
        for j in range(n_sub):
            h_ref[j * BLOCK:(j + 1) * BLOCK, :] = refs[j][...]
        c_ref[...], s1_ref[...], s2_ref[...] = _rope_coeffs(refs[n_sub][...])

    def after(lands, *refs):
        h_ref, buf = refs[n_sub + 1], refs[n_sub + 5]
        pltpu.sync_copy(lands[small_piece], buf)
        h_ref[0:LEAD_PAD, :] = jnp.zeros((LEAD_PAD, D_MODEL), F32)
        for d in range(N_DEV):
            h_ref[LEAD_PAD:BLOCK, d * 128:(d + 1) * 128] = buf[d, 0:N_META, :]

    tile = lambda i: (i + 1) % nt
    piece = lambda j: pl.BlockSpec((BLOCK, D_MODEL), lambda i: (jnp.maximum(tile(i) * n_sub + j - 1, 0), 0))
    rows = lambda w: pl.BlockSpec((tm, w), lambda i: (tile(i), 0))
    (h, *rope), lands = _call(
        body, exch,
        name=name,
        grid=(nt,),
        in_specs=[piece(j) for j in range(n_sub)] + [rows(128)],
        out_specs=[rows(D_MODEL)] + [rows(128)] * 3,
        out_shape=[jax.ShapeDtypeStruct((lp, D_MODEL), F32)] + [jax.ShapeDtypeStruct((lp, 128), F32)] * 3,
        scratch_shapes=[pltpu.VMEM(small_shape, F32)],
        compiler_params=_params(),
        after=after,
    )(*([x] * n_sub), rope_compact)
    return h, rope, lands


def _in_proj_fwd(h, g, w_in_t, rope, tm, name, exch=None):
    lp = h.shape[0]

    def body(h_ref, g_ref, w_ref, c_ref, s1_ref, s2_ref, a_ref, qkv_ref, bch_ref):
        a = _rms_fwd(h_ref[...], g_ref[...]).astype(BF)
        a_ref[...] = a
        proj = _dot_nt(a, w_ref[...])
        c, s1, s2 = c_ref[...], s1_ref[...], s2_ref[...]
        for j in range(5):
            t = _rope(proj[:, j * 128:(j + 1) * 128], c, s1, s2)
            qkv_ref[:, j * 128:(j + 1) * 128] = (t * SCALE if j < 4 else t).astype(BF)
        qkv_ref[:, 640:768] = proj[:, 640:768].astype(BF)
        bch_ref[...] = proj[:, 768:].astype(BF)

    row = lambda w: pl.BlockSpec((tm, w), lambda i: (i, 0))
    return _call(
        body, exch,
        name=name,
        grid=(lp // tm,),
        in_specs=[row(D_MODEL), _full((1, D_MODEL)), _full((IN_W, D_MODEL)), row(128), row(128), row(128)],
        out_specs=[row(D_MODEL), row(768), row(3 * CONV_W)],
        out_shape=[
            jax.ShapeDtypeStruct((lp, D_MODEL), BF),
            jax.ShapeDtypeStruct((lp, 768), BF),
            jax.ShapeDtypeStruct((lp, 3 * CONV_W), BF),
        ],
        compiler_params=_params(),
    )(h, g, w_in_t, *rope)


def _fold_masks(i):
    r = lax.broadcasted_iota(jnp.int32, (2 * BLOCK, BLOCK), 0) & (BLOCK - 1)
    c = lax.broadcasted_iota(jnp.int32, (2 * BLOCK, BLOCK), 1)
    tri = c > r
    ok = jnp.where(tri, (i - 1) * BLOCK + c, i * BLOCK + c) >= LEAD_PAD
    return tri, ok


def _kv_operand(x, kvh):
    lane = lax.broadcasted_iota(jnp.int32, x.shape, 1)
    zero = jnp.zeros_like(x)
    if kvh == 0:
        lo = jnp.where(lane < HEAD_DIM, x, zero)
        hi = pltpu.roll(lo, HEAD_DIM, 1)
    else:
        hi = jnp.where(lane >= HEAD_DIM, x, zero)
        lo = pltpu.roll(hi, HEAD_DIM, 1)
    return jnp.concatenate([lo, hi], axis=0)


def _split4(t, tri):
    zero = jnp.zeros_like(t[0])
    return jnp.concatenate(
        [jnp.where(tri, t[0], zero), jnp.where(tri, zero, t[0]), jnp.where(tri, t[1], zero), jnp.where(tri, zero, t[1])], axis=1)


def _sink_cols(sink_ref, kvh):
    first = lax.broadcasted_iota(jnp.int32, (2 * BLOCK, 1), 0) < BLOCK
    return [jnp.where(first, sink_ref[0, 4 * kvh + half], sink_ref[0, 4 * kvh + 2 + half]) for half in range(2)]


def _folded_exp(q2, k4, tri, ok, sks):
    s = _dot_nt(q2, k4)
    es, ss = [], []
    for half in range(2):
        s_h = s[:, 2 * half * BLOCK:2 * (half + 1) * BLOCK]
        sf = jnp.where(ok, jnp.where(tri, s_h[:, :BLOCK], s_h[:, BLOCK:]), NEG)
        m = jnp.maximum(jnp.max(sf, axis=-1, keepdims=True), sks[half])
        es.append(jnp.exp(sf - m))
        ss.append(jnp.exp(sks[half] - m))
    sums = _dot(jnp.concatenate(es, axis=0).astype(BF), jnp.ones((BLOCK, BLOCK), BF))
    invs = [1.0 / (sums[2 * half * BLOCK:2 * (half + 1) * BLOCK] + ss[half]) for half in range(2)]
    return es, ss, invs


def _attn_fwd(qkv, sink, name, exch=None):
    lp = qkv.shape[0]
    nb = lp // BLOCK

    def body(sink_ref, q_ref, kvc_ref, kvp_ref, o_ref, p_ref, ps_ref):
        i = pl.program_id(0)
        tri, ok = _fold_masks(i)
        kvc, kvp = kvc_ref[...], kvp_ref[...]
        kk = jnp.concatenate([kvp[:, :128], kvc[:, :128]], axis=0)
        vv = jnp.concatenate([kvp[:, 128:], kvc[:, 128:]], axis=0)
        lane = lax.broadcasted_iota(jnp.int32, (BLOCK, 128), 1)
        p_sink = jnp.zeros((BLOCK, 128), F32)
        for kvh in range(2):
            q2 = jnp.concatenate([q_ref[:, 256 * kvh:256 * kvh + 128], q_ref[:, 256 * kvh + 128:256 * kvh + 256]], axis=0)
            es, ss, invs = _folded_exp(q2, _kv_operand(kk, kvh), tri, ok, _sink_cols(sink_ref, kvh))
            pb = [(es[half] * invs[half]).astype(BF) for half in range(2)]
            out = _dot(_split4(pb, tri), _kv_operand(vv, kvh))
            for pair in range(2):
                rows = slice(pair * BLOCK, (pair + 1) * BLOCK)
                o_ref[:, 256 * kvh + 128 * pair:256 * kvh + 128 * (pair + 1)] = out[rows].astype(BF)
                for half in range(2):
                    head = 4 * kvh + 2 * pair + half
                    p_ref[:, 128 * head:128 * (head + 1)] = pb[half][rows]
                    p_sink = jnp.where(lane == head, (ss[half] * invs[half][:, 0:1])[rows], p_sink)
        ps_ref[...] = p_sink

    return _call(
        body, exch,
        name=name,
        grid=(nb,),
        in_specs=[
            pl.BlockSpec(memory_space=pltpu.SMEM),
            pl.BlockSpec((BLOCK, ATTN_W), lambda i: (i, 0)),
            pl.BlockSpec((BLOCK, 256), lambda i: (i, 2)),
            pl.BlockSpec((BLOCK, 256), lambda i: (jnp.maximum(i - 1, 0), 2)),
        ],
        out_specs=[pl.BlockSpec((BLOCK, ATTN_W), lambda i: (i, 0)), pl.BlockSpec((BLOCK, N_Q_HEADS * BLOCK), lambda i: (i, 0)),
                   pl.BlockSpec((BLOCK, 128), lambda i: (i, 0))],
        out_shape=[jax.ShapeDtypeStruct((lp, ATTN_W), BF), jax.ShapeDtypeStruct((lp, N_Q_HEADS * BLOCK), BF),
                   jax.ShapeDtypeStruct((lp, 128), F32)],
        compiler_params=_params(),
    )(sink, qkv, qkv, qkv)


def _mix_out_fwd(bch, y_attn, h, conv_w, g_a, g_c, w_out, g_post, tm, name, exch=None):
    lp = h.shape[0]

    def body(bch_ref, ya_ref, h_ref, cw_ref, ga_ref, gc_ref, w_ref, gp_ref, y_ref, z_ref, h2_ref, ext):
        i = pl.program_id(0)

        @pl.when(i == 0)
        def _():
            ext[0:8, :] = jnp.zeros((8, CONV_W), F32)

        b = bch_ref[:, 0:CONV_W].astype(F32)
        u = bch_ref[:, CONV_W:2 * CONV_W].astype(F32) * bch_ref[:, 2 * CONV_W:3 * CONV_W].astype(F32)
        ext[8:8 + tm, :] = u
        yc = cw_ref[0:1, :] * ext[6:6 + tm, :] + cw_ref[1:2, :] * ext[7:7 + tm, :] + cw_ref[2:3, :] * u
        ext[0:8, :] = u[tm - 8:tm, :]
        ya = _rms_fwd(ya_ref[...].astype(F32), ga_ref[...]).astype(BF)
        yb = _rms_fwd(b * yc, gc_ref[...]).astype(BF)
        y_ref[:, 0:ATTN_W] = ya
        y_ref[:, ATTN_W:] = yb
        z = _dot(ya, w_ref[0:ATTN_W, :]) + _dot(yb, w_ref[ATTN_W:, :])
        z_ref[...] = z
        h2_ref[...] = h_ref[...] + _rms_fwd(z, gp_ref[...])

    row = lambda w: pl.BlockSpec((tm, w), lambda i: (i, 0))
    return _call(
        body, exch,
        name=name,
        grid=(lp // tm,),
        in_specs=[
            row(3 * CONV_W), row(ATTN_W), row(D_MODEL), _full((8, CONV_W)), _full((1, ATTN_W)), _full((1, CONV_W)),
            _full((D_MODEL, D_MODEL)), _full((1, D_MODEL)),
        ],
        out_specs=[row(D_MODEL), row(D_MODEL), row(D_MODEL)],
        out_shape=[
            jax.ShapeDtypeStruct((lp, D_MODEL), BF),
            jax.ShapeDtypeStruct((lp, D_MODEL), F32),
            jax.ShapeDtypeStruct((lp, D_MODEL), F32),
        ],
        scratch_shapes=[pltpu.VMEM((tm + 8, CONV_W), F32)],
        compiler_params=_params(),
    )(bch, y_attn, h, conv_w, g_a, g_c, w_out, g_post)


def _mlp_fwd(h2, g_pre, w_up_t, w_down, g_post, tm, name, exch=None, target=None):
    lp = h2.shape[0]
    sub = math.gcd(tm, BLOCK)
    n_sub, lead = tm // sub, BLOCK // sub
    n_t = n_sub if target is not None else 0

    def body(*refs):
        h_ref, gp_ref, wu_ref, wd_ref, gq_ref = refs[:5]
        t_refs = refs[5:5 + n_t]
        a_ref, up_ref, f_ref, last_ref = refs[5 + n_t:9 + n_t]
        h = h_ref[...]
        a = _rms_fwd(h, gp_ref[...]).astype(BF)
        a_ref[...] = a
        up = _dot_nt(a, wu_ref[...])
        up_ref[...] = up.astype(BF)
        act = jnp.square(jnp.maximum(up, 0.0)).astype(BF)
        f = _dot(act, wd_ref[...])
        f_ref[...] = f
        h3 = h + _rms_fwd(f, gq_ref[...])
        if target is None:
            last_ref[...] = h3
            return
        ls_ref = refs[9 + n_t]
        i = pl.program_id(0)

        @pl.when(i == 0)
        def _():
            ls_ref[...] = jnp.zeros((8, 128), F32)

        sq = jnp.zeros((1, 1), F32)
        for j in range(n_sub):
            on_tokens = i * n_sub + j >= lead
            d = jnp.where(on_tokens, h3[j * sub:(j + 1) * sub] - t_refs[j][...], 0.0)
            last_ref[j * sub:(j + 1) * sub, :] = d * (1.0 / D_MODEL)
            sq = sq + jnp.sum(d * d)
        ls_ref[...] += sq

    row = lambda w: pl.BlockSpec((tm, w), lambda i: (i, 0))
    piece = lambda j: pl.BlockSpec((sub, D_MODEL), lambda i: (jnp.maximum(i * n_sub + j - lead, 0), 0))
    out_specs = [row(D_MODEL), row(D_FF), row(D_MODEL), row(D_MODEL)]
    out_shape = [
        jax.ShapeDtypeStruct((lp, D_MODEL), BF),
        jax.ShapeDtypeStruct((lp, D_FF), BF),
        jax.ShapeDtypeStruct((lp, D_MODEL), F32),
        jax.ShapeDtypeStruct((lp, D_MODEL), F32),
    ]
    if target is not None:
        out_specs.append(_full_out((8, 128)))
        out_shape.append(jax.ShapeDtypeStruct((8, 128), F32))
    return _call(
        body, exch,
        name=name,
        grid=(lp // tm,),
        in_specs=[row(D_MODEL), _full((1, D_MODEL)), _full((D_FF, D_MODEL)), _full((D_FF, D_MODEL)), _full((1, D_MODEL))]
        + [piece(j) for j in range(n_t)],
        out_specs=out_specs,
        out_shape=out_shape,
        compiler_params=_params(),
    )(h2, g_pre, w_up_t, w_down, g_post, *([target] * n_t))


def _mlp_bwd_dx(dh3, f, up, h2, w_down, w_up_t, g_post, g_pre, tm, name, exch=None):
    lp = h2.shape[0]

    def body(dh3_ref, f_ref, up_ref, h2_ref, wd_ref, wu_ref, gq_ref, gp_ref, df_ref, dup_ref, dh2_ref, dg_ref):
        i = pl.program_id(0)

        @pl.when(i == 0)
        def _():
            dg_ref[...] = jnp.zeros((8, D_MODEL), F32)

        dh3 = dh3_ref[...]
        df, dgq = _rms_bwd(f_ref[...], gq_ref[...], dh3)
        dg_ref[ROW_MLP_POST:ROW_MLP_POST + 1, :] += dgq
        df = df.astype(BF)
        df_ref[...] = df
        dact = _dot_nt(df, wd_ref[...])
        dup = (dact * (2.0 * jnp.maximum(up_ref[...].astype(F32), 0.0))).astype(BF)
        dup_ref[...] = dup
        da = _dot(dup, wu_ref[...])
        dh, dgp = _rms_bwd(h2_ref[...], gp_ref[...], da)
        dg_ref[ROW_MLP_PRE:ROW_MLP_PRE + 1, :] += dgp
        dh2_ref[...] = dh3 + dh

    row = lambda w: pl.BlockSpec((tm, w), lambda i: (i, 0))
    return _call(
        body, exch,
        name=name,
        grid=(lp // tm,),
        in_specs=[
            row(D_MODEL), row(D_MODEL), row(D_FF), row(D_MODEL), _full((D_FF, D_MODEL)), _full((D_FF, D_MODEL)),
            _full((1, D_MODEL)), _full((1, D_MODEL)),
        ],
        out_specs=[row(D_MODEL), row(D_FF), row(D_MODEL), _full_out((8, D_MODEL))],
        out_shape=[
            jax.ShapeDtypeStruct((lp, D_MODEL), BF),
            jax.ShapeDtypeStruct((lp, D_FF), BF),
            jax.ShapeDtypeStruct((lp, D_MODEL), F32),
            jax.ShapeDtypeStruct((8, D_MODEL), F32),
        ],
        compiler_params=_params(),
    )(dh3, f, up, h2, w_down, w_up_t, g_post, g_pre)


def _mlp_bwd_dw(up, df, dup, a2, tm, name):
    lp = up.shape[0]
    nt = lp // tm
    nj = D_FF // D_MODEL

    def body(up_ref, df_ref, dup_ref, a_ref, dwd_ref, dwu_ref, accd, accu):
        i = pl.program_id(1)

        @pl.when(i == 0)
        def _():
            accd[...] = jnp.zeros_like(accd)
            accu[...] = jnp.zeros_like(accu)

        act = jnp.square(jnp.maximum(up_ref[...].astype(F32), 0.0)).astype(BF)
        accd[...] += _dot_tn(act, df_ref[...])
        accu[...] += _dot_tn(dup_ref[...], a_ref[...])

        @pl.when(i == nt - 1)
        def _():
            dwd_ref[...] = accd[...].astype(BF)
            dwu_ref[...] = accu[...].astype(BF)

    return pl.pallas_call(
        body,
        name=name,
        grid=(nj, nt),
        in_specs=[
            pl.BlockSpec((tm, D_MODEL), lambda j, i: (i, j)),
            pl.BlockSpec((tm, D_MODEL), lambda j, i: (i, 0)),
            pl.BlockSpec((tm, D_MODEL), lambda j, i: (i, j)),
            pl.BlockSpec((tm, D_MODEL), lambda j, i: (i, 0)),
        ],
        out_specs=[pl.BlockSpec((D_MODEL, D_MODEL), lambda j, i: (j, 0)), pl.BlockSpec((D_MODEL, D_MODEL), lambda j, i: (j, 0))],
        out_shape=[jax.ShapeDtypeStruct((D_FF, D_MODEL), BF), jax.ShapeDtypeStruct((D_FF, D_MODEL), BF)],
        scratch_shapes=[pltpu.VMEM((D_MODEL, D_MODEL), F32), pltpu.VMEM((D_MODEL, D_MODEL), F32)],
        compiler_params=_params(("arbitrary", "arbitrary")),
    )(up, df, dup, a2)


def _mix_out_bwd(dh2, z, y_attn, bch, w_out, g_post, g_a, g_c, conv_w, tm, name, exch=None):
    lp = dh2.shape[0]
    nt = lp // tm
    halo = 16

    def body(dh2_ref, z_ref, ya_ref, bch_ref, halo_ref, w_ref, gp_ref, ga_ref, gc_ref, cw_ref,
             dz_ref, dya_ref, dbch_ref, dg_ref, ext, ext_u):
        i = pl.program_id(0)
        dcw_ref = dg_ref.at[ROW_CONV:ROW_CONV + 3, 0:CONV_W]

        @pl.when(i == 0)
        def _():
            ext[tm:tm + 8, :] = jnp.zeros((8, CONV_W), F32)
            dg_ref[...] = jnp.zeros((8, D_MODEL), F32)

        dz, dgp = _rms_bwd(z_ref[...], gp_ref[...], dh2_ref[...])
        dg_ref[ROW_MIX_POST:ROW_MIX_POST + 1, :] += dgp
        dz = dz.astype(BF)
        dz_ref[...] = dz
        dya_n = _dot_nt(dz, w_ref[0:ATTN_W, :])
        dyb_n = _dot_nt(dz, w_ref[ATTN_W:, :])
        dya, dga = _rms_bwd(ya_ref[...].astype(F32), ga_ref[...], dya_n)
        dg_ref[ROW_GROUP_G:ROW_GROUP_G + 1, 0:ATTN_W] += dga
        dya_ref[...] = dya
        b = bch_ref[:, 0:CONV_W].astype(F32)
        c = bch_ref[:, CONV_W:2 * CONV_W].astype(F32)
        hc = bch_ref[:, 2 * CONV_W:3 * CONV_W].astype(F32)
        u = c * hc
        u_before = halo_ref[:, CONV_W:2 * CONV_W].astype(F32) * halo_ref[:, 2 * CONV_W:3 * CONV_W].astype(F32)
        ext_u[0:halo, :] = jnp.where(i < nt - 1, u_before, 0.0)
        ext_u[halo:halo + tm, :] = u
        yc_v = (cw_ref[0:1, :] * ext_u[halo - 2:halo - 2 + tm, :] + cw_ref[1:2, :] * ext_u[halo - 1:halo - 1 + tm, :]
                + cw_ref[2:3, :] * u)
        dyconv, dgc = _rms_bwd(b * yc_v, gc_ref[...], dyb_n)
        dg_ref[ROW_GROUP_G:ROW_GROUP_G + 1, ATTN_W:] += dgc
        dbch_ref[:, 0:CONV_W] = (dyconv * yc_v).astype(BF)
        dyc = dyconv * b
        ext[0:tm, :] = dyc
        d1 = ext[1:1 + tm, :]
        d2 = ext[2:2 + tm, :]
        du = cw_ref[2:3, :] * dyc + cw_ref[1:2, :] * d1 + cw_ref[0:1, :] * d2
        ext[tm:tm + 8, :] = dyc[0:8, :]
        dbch_ref[:, CONV_W:2 * CONV_W] = (du * hc).astype(BF)
        dbch_ref[:, 2 * CONV_W:3 * CONV_W] = (du * c).astype(BF)
        dcw_ref[0:1, :] += jnp.sum(u * d2, axis=0, keepdims=True)
        dcw_ref[1:2, :] += jnp.sum(u * d1, axis=0, keepdims=True)
        dcw_ref[2:3, :] += jnp.sum(u * dyc, axis=0, keepdims=True)

    row = lambda w: pl.BlockSpec((tm, w), lambda i: (nt - 1 - i, 0))
    before = pl.BlockSpec((halo, 3 * CONV_W), lambda i: (jnp.maximum((nt - 1 - i) * (tm // halo) - 1, 0), 0))
    return _call(
        body, exch,
        name=name,
        grid=(nt,),
        in_specs=[
            row(D_MODEL), row(D_MODEL), row(ATTN_W), row(3 * CONV_W), before, _full((D_MODEL, D_MODEL)),
            _full((1, D_MODEL)), _full((1, ATTN_W)), _full((1, CONV_W)), _full((8, CONV_W)),
        ],
        out_specs=[row(D_MODEL), row(ATTN_W), row(3 * CONV_W), _full_out((8, D_MODEL))],
        out_shape=[
            jax.ShapeDtypeStruct((lp, D_MODEL), BF),
            jax.ShapeDtypeStruct((lp, ATTN_W), F32),
            jax.ShapeDtypeStruct((lp, 3 * CONV_W), BF),
            jax.ShapeDtypeStruct((8, D_MODEL), F32),
        ],
        scratch_shapes=[pltpu.VMEM((tm + 8, CONV_W), F32), pltpu.VMEM((tm + halo, CONV_W), F32)],
        compiler_params=_params(),
    )(dh2, z, y_attn, bch, bch, w_out, g_post, g_a, g_c, conv_w)


def _attn_bwd(qkv, o, do, probs, p_sink, rope, name, exch=None):
    lp = qkv.shape[0]
    nb = lp // BLOCK

    def body(q_ref, kvc_ref, kvp_ref, o_ref, do_ref, p_ref, ps_ref, cq_ref, s1q_ref, s2q_ref, ck_ref, s1k_ref, s2k_ref,
             dq_ref, dkv_ref, dsink_ref, carry):
        i = pl.program_id(0)

        @pl.when(i == 0)
        def _():
            carry[...] = jnp.zeros_like(carry)
            dsink_ref[...] = jnp.zeros((8, 128), F32)

        def finish(tot):
            dk = _rope_t(tot[:, :128], ck_ref[...], s1k_ref[...], s2k_ref[...])
            dkv_ref[:, 0:128] = dk.astype(BF)
            dkv_ref[:, 128:256] = tot[:, 128:].astype(BF)

        @pl.when(i < nb)
        def _():
            tri, _ = _fold_masks(i)
            kvc, kvp = kvc_ref[...], kvp_ref[...]
            kk = jnp.concatenate([kvp[:, :128], kvc[:, :128]], axis=0)
            vv = jnp.concatenate([kvp[:, 128:], kvc[:, 128:]], axis=0)
            lane = lax.broadcasted_iota(jnp.int32, (BLOCK, 128), 1)
            lane2 = lax.broadcasted_iota(jnp.int32, (2 * BLOCK, 128), 1)
            rope_q = (cq_ref[...], s1q_ref[...], s2q_ref[...])
            deltas = jnp.zeros((BLOCK, 128), F32)
            folded = []
            for kvh in range(2):
                c0 = 256 * kvh
                q2 = jnp.concatenate([q_ref[:, c0:c0 + 128], q_ref[:, c0 + 128:c0 + 256]], axis=0)
                do2 = jnp.concatenate([do_ref[:, c0:c0 + 128], do_ref[:, c0 + 128:c0 + 256]], axis=0)
                o2 = jnp.concatenate([o_ref[:, c0:c0 + 128], o_ref[:, c0 + 128:c0 + 256]], axis=0).astype(F32)
                k4, v4 = _kv_operand(kk, kvh), _kv_operand(vv, kvh)
                prod = do2 * o2
                dob = do2.astype(BF)
                dp = _dot_nt(dob, v4)
                ds, pb = [], []
                for half in range(2):
                    heads = [4 * kvh + 2 * pair + half for pair in range(2)]
                    p = jnp.concatenate([p_ref[:, 128 * h:128 * (h + 1)] for h in heads], axis=0)
                    sel = (lane2 < HEAD_DIM) if half == 0 else (lane2 >= HEAD_DIM)
                    delta = jnp.sum(jnp.where(sel, prod, 0.0), axis=-1, keepdims=True)
                    dp_h = dp[:, 2 * half * BLOCK:2 * (half + 1) * BLOCK]
                    ds.append((p.astype(F32) * (jnp.where(tri, dp_h[:, :BLOCK], dp_h[:, BLOCK:]) - delta)).astype(BF))
                    pb.append(p)
                    for pair in range(2):
                        deltas = jnp.where(lane == heads[pair], delta[pair * BLOCK:(pair + 1) * BLOCK], deltas)
                ds4, p4 = _split4(ds, tri), _split4(pb, tri)
                dq2 = _dot(ds4, k4) * SCALE
                dq_ref[:, c0:c0 + 128] = _rope_t(dq2[:BLOCK], *rope_q).astype(BF)
                dq_ref[:, c0 + 128:c0 + 256] = _rope_t(dq2[BLOCK:], *rope_q).astype(BF)
                rk, rv = _dot_tn(ds4, q2), _dot_tn(p4, dob)
                own = (lane < HEAD_DIM) if kvh == 0 else (lane >= HEAD_DIM)
                group = []
                for r in (rk, rv):
                    for blk in range(2):
                        t = jnp.where(lane < HEAD_DIM, r[blk * BLOCK:(blk + 1) * BLOCK], r[(2 + blk) * BLOCK:(3 + blk) * BLOCK])
                        group.append(jnp.where(own, t + pltpu.roll(t, HEAD_DIM, 1), 0.0))
                folded.append(group)
            dsink_ref[ROW_SINK:ROW_SINK + 1, :] -= jnp.sum(ps_ref[...] * deltas, axis=0, keepdims=True)
            dk_p, dk_c, dv_p, dv_c = [folded[0][t] + folded[1][t] for t in range(4)]
            finish(carry[...] + jnp.concatenate([dk_p, dv_p], axis=1))
            carry[...] = jnp.concatenate([dk_c, dv_c], axis=1)

        @pl.when(i == nb)
        def _():
            finish(carry[...])

    qi = lambda i: jnp.minimum(i, nb - 1)
    ki = lambda i: jnp.maximum(i - 1, 0)
    tab_q = pl.BlockSpec((BLOCK, 128), lambda i: (qi(i), 0))
    tab_k = pl.BlockSpec((BLOCK, 128), lambda i: (ki(i), 0))
    return _call(
        body, exch,
        name=name,
        grid=(nb + 1,),
        in_specs=[
            pl.BlockSpec((BLOCK, ATTN_W), lambda i: (qi(i), 0)),
            pl.BlockSpec((BLOCK, 256), lambda i: (qi(i), 2)),
            pl.BlockSpec((BLOCK, 256), lambda i: (jnp.maximum(qi(i) - 1, 0), 2)),
            pl.BlockSpec((BLOCK, ATTN_W), lambda i: (qi(i), 0)),
            pl.BlockSpec((BLOCK, ATTN_W), lambda i: (qi(i), 0)),
            pl.BlockSpec((BLOCK, N_Q_HEADS * BLOCK), lambda i: (qi(i), 0)),
            tab_q, tab_q, tab_q, tab_q, tab_k, tab_k, tab_k,
        ],
        out_specs=[
            pl.BlockSpec((BLOCK, ATTN_W), lambda i: (qi(i), 0)),
            pl.BlockSpec((BLOCK, 256), lambda i: (ki(i), 0)),
            pl.BlockSpec((8, 128), lambda i: (0, 0)),
        ],
        out_shape=[
            jax.ShapeDtypeStruct((lp, ATTN_W), BF),
            jax.ShapeDtypeStruct((lp, 256), BF),
            jax.ShapeDtypeStruct((8, 128), F32),
        ],
        scratch_shapes=[pltpu.VMEM((BLOCK, 256), F32)],
        compiler_params=_params(),
    )(qkv, qkv, qkv, o, do, probs, p_sink, *rope, *rope)


def _in_proj_bwd_dx(dq, dkv, dbch, w_in_t, h, dh2, g, tm, name, exch=None):
    lp = h.shape[0]

    def body(dq_ref, dkv_ref, dbch_ref, w_ref, h_ref, dh2_ref, g_ref, dh_ref, dg_ref):
        i = pl.program_id(0)

        @pl.when(i == 0)
        def _():
            dg_ref[...] = jnp.zeros((8, D_MODEL), F32)

        da = _dot(dq_ref[...], w_ref[0:512, :]) + _dot(dkv_ref[...], w_ref[512:768, :]) + _dot(dbch_ref[...], w_ref[768:, :])
        dh, dg = _rms_bwd(h_ref[...], g_ref[...], da)
        dg_ref[ROW_MIX_PRE:ROW_MIX_PRE + 1, :] += dg
        dh_ref[...] = dh2_ref[...] + dh

    row = lambda w: pl.BlockSpec((tm, w), lambda i: (i, 0))
    return _call(
        body, exch,
        name=name,
        grid=(lp // tm,),
        in_specs=[row(ATTN_W), row(256), row(3 * CONV_W), _full((IN_W, D_MODEL)), row(D_MODEL), row(D_MODEL), _full((1, D_MODEL))],
        out_specs=[row(D_MODEL), _full_out((8, D_MODEL))],
        out_shape=[jax.ShapeDtypeStruct((lp, D_MODEL), F32), jax.ShapeDtypeStruct((8, D_MODEL), F32)],
        compiler_params=_params(),
    )(dq, dkv, dbch, w_in_t, h, dh2, g)


def _mix_bwd_dw(dq, dkv, dbch, a, y, dz, tm, name, exch=None):
    lp = a.shape[0]
    nt = lp // tm

    def body(dq_ref, dkv_ref, dbch_ref, a_ref, y_ref, dz_ref, dwi_ref, dwo_ref, acci, acco):
        i = pl.program_id(0)

        @pl.when(i == 0)
        def _():
            acci[...] = jnp.zeros_like(acci)
            acco[...] = jnp.zeros_like(acco)

        a_v = a_ref[...]
        acci[0:512, :] += _dot_tn(dq_ref[...], a_v)
        acci[512:768, :] += _dot_tn(dkv_ref[...], a_v)
        acci[768:, :] += _dot_tn(dbch_ref[...], a_v)
        acco[...] += _dot_tn(y_ref[...], dz_ref[...])

        @pl.when(i == nt - 1)
        def _():
            dwi_ref[...] = acci[...].astype(BF)
            dwo_ref[...] = acco[...].astype(BF)

    row = lambda w: pl.BlockSpec((tm, w), lambda i: (i, 0))
    return _call(
        body, exch,
        name=name,
        grid=(nt,),
        in_specs=[row(ATTN_W), row(256), row(3 * CONV_W), row(D_MODEL), row(D_MODEL), row(D_MODEL)],
        out_specs=[_full_out((IN_W, D_MODEL)), _full_out((D_MODEL, D_MODEL))],
        out_shape=[jax.ShapeDtypeStruct((IN_W, D_MODEL), BF), jax.ShapeDtypeStruct((D_MODEL, D_MODEL), BF)],
        scratch_shapes=[pltpu.VMEM((IN_W, D_MODEL), F32), pltpu.VMEM((D_MODEL, D_MODEL), F32)],
        compiler_params=_params(),
    )(dq, dkv, dbch, a, y, dz)


def _mesh_place():
    x, y, c = lax.axis_index("x"), lax.axis_index("y"), lax.axis_index("c")
    return x, y, c, 4 * x + 2 * y + c


def _peer(x, y, c, k):
    px = 1 - x if k & 4 else x
    py = 1 - y if k & 2 else y
    pc = 1 - c if k & 1 else c
    return (px, py, pc), 4 * px + 2 * py + pc


SIBLING = 1
SAME_CORE = (2, 4, 6)
OTHER_CORE = (3, 5, 7)


class _Exchange:
    def __init__(self, pieces):
        self.srcs = [s for s, _ in pieces]
        self.to_all = [g for _, g in pieces]
        self.n = len(pieces)
        self.land_shapes = [
            jax.ShapeDtypeStruct((N_DEV,) + (s.shape if g else s.shape[1:]), s.dtype) for s, g in pieces]
        self.sem_shapes = [pltpu.SemaphoreType.DMA((self.n, N_DEV - 1)), pltpu.SemaphoreType.DMA((self.n, N_DEV - 1)),
                           pltpu.SemaphoreType.DMA((self.n,))]
        self.forwards = any(self.to_all)

    def _ops(self, srcs, lands, sems):
        send_sems, recv_sems, local_sems = sems
        x, y, c, me = _mesh_place()

        def remote(p, k, src, slot, to):
            return pltpu.make_async_remote_copy(
                src_ref=src, dst_ref=lands[p].at[slot], send_sem=send_sems.at[p, k - 1], recv_sem=recv_sems.at[p, k - 1],
                device_id=to, device_id_type=MESH)

        def own(p):
            return pltpu.make_async_copy(srcs[p] if self.to_all[p] else srcs[p].at[me], lands[p].at[me], local_sems.at[p])

        def direct(p, k):
            peer, pidx = _peer(x, y, c, k)
            return remote(p, k, srcs[p] if self.to_all[p] else srcs[p].at[pidx], me, peer)

        def forward(p, k):
            sibling, _ = _peer(x, y, c, SIBLING)
            _, origin = _peer(x, y, c, k ^ SIBLING)
            return remote(p, k, lands[p].at[origin], origin, sibling)

        def arrival(p, k):
            peer, pidx = _peer(x, y, c, k)
            return remote(p, k, lands[p].at[pidx], pidx, peer)

        return own, direct, forward, arrival

    def start(self, srcs, lands, sems):
        own, direct, _, _ = self._ops(srcs, lands, sems)
        for p in range(self.n):
            own(p).start()
            for k in ((SIBLING,) + SAME_CORE) if self.to_all[p] else range(1, N_DEV):
                direct(p, k).start()

    def forward(self, srcs, lands, sems):
        _, _, forward, arrival = self._ops(srcs, lands, sems)
        for p in range(self.n):
            if self.to_all[p]:
                for k in SAME_CORE:
                    arrival(p, k).wait_recv()
                    forward(p, k ^ SIBLING).start()

    def finish(self, srcs, lands, sems):
        own, direct, forward, arrival = self._ops(srcs, lands, sems)
        for p in range(self.n):
            for k in ((SIBLING,) + OTHER_CORE) if self.to_all[p] else range(1, N_DEV):
                arrival(p, k).wait_recv()
        for p in range(self.n):
            for k in range(1, N_DEV):
                (forward(p, k) if self.to_all[p] and k in OTHER_CORE else direct(p, k)).wait_send()
            own(p).wait()


def _call(body, exch, *, name, grid, in_specs, out_specs, out_shape, scratch_shapes=(), compiler_params, after=None):
    if exch is None:
        return pl.pallas_call(body, name=name, grid=grid, in_specs=in_specs, out_specs=out_specs, out_shape=out_shape,
                              scratch_shapes=scratch_shapes, compiler_params=compiler_params)
    n_in, n_out, n_scr, n_x = len(in_specs), len(out_shape), len(scratch_shapes), exch.n
    steps = math.prod(grid)

    def carrying(*refs):
        a, b, c, d, e = n_in, n_in + n_x, n_in + n_x + n_out, n_in + 2 * n_x + n_out, n_in + 2 * n_x + n_out + n_scr
        ins, srcs, outs, lands, scr, sems = refs[:a], refs[a:b], refs[b:c], refs[c:d], refs[d:e], refs[e:]
        step = functools.reduce(lambda acc, t: acc * grid[t] + pl.program_id(t), range(len(grid)), 0)

        @pl.when(step == 0)
        def _():
            exch.start(srcs, lands, sems)

        body(*ins, *outs, *scr)

        if exch.forwards:
            @pl.when(step == max(0, steps - 1 - (steps + 7) // 8))
            def _():
                exch.forward(srcs, lands, sems)

        @pl.when(step == steps - 1)
        def _():
            exch.finish(srcs, lands, sems)
            if after is not None:
                after(lands, *ins, *outs, *scr)

    hbm = pl.BlockSpec(memory_space=pl.ANY)
    call = pl.pallas_call(
        carrying, name=name, grid=grid, in_specs=list(in_specs) + [hbm] * n_x, out_specs=list(out_specs) + [hbm] * n_x,
        out_shape=list(out_shape) + exch.land_shapes, scratch_shapes=list(scratch_shapes) + exch.sem_shapes,
        compiler_params=compiler_params)

    def run(*args):
        res = call(*args, *exch.srcs)
        return list(res[:n_out]), list(res[n_out:])

    return run


def _sum_small(part):
    def body(part_ref, out_ref, land, send_sems, recv_sems):
        x, y, c, me = _mesh_place()
        land[me] = part_ref[...]
        sent = []
        for k in range(1, N_DEV):
            peer, _ = _peer(x, y, c, k)
            cp = pltpu.make_async_remote_copy(
                src_ref=part_ref, dst_ref=land.at[me], send_sem=send_sems.at[k - 1], recv_sem=recv_sems.at[k - 1],
                device_id=peer, device_id_type=MESH)
            cp.start()
            sent.append(cp)
        for k in range(1, N_DEV):
            peer, pidx = _peer(x, y, c, k)
            pltpu.make_async_remote_copy(
                src_ref=part_ref, dst_ref=land.at[pidx], send_sem=send_sems.at[k - 1], recv_sem=recv_sems.at[k - 1],
                device_id=peer, device_id_type=MESH).wait_recv()
        for cp in sent:
            cp.wait_send()
        acc = land[0]
        for d in range(1, N_DEV):
            acc = acc + land[d]
        out_ref[...] = acc

    vmem = pl.BlockSpec(memory_space=pltpu.VMEM)
    return pl.pallas_call(
        body,
        name="sum_small",
        in_specs=[vmem],
        out_specs=vmem,
        out_shape=jax.ShapeDtypeStruct(part.shape, F32),
        scratch_shapes=[pltpu.VMEM((N_DEV,) + part.shape, F32), pltpu.SemaphoreType.DMA((N_DEV - 1,)),
                        pltpu.SemaphoreType.DMA((N_DEV - 1,))],
    )(part)


def _adamw(w, g, m, v):
    m = ADAM_B1 * m + (1.0 - ADAM_B1) * g
    v = ADAM_B2 * v + (1.0 - ADAM_B2) * jnp.square(g)
    m_hat = m / (1.0 - ADAM_B1 ** ADAM_STEP)
    v_hat = v / (1.0 - ADAM_B2 ** ADAM_STEP)
    delta = -ADAM_LR * (m_hat / (jnp.sqrt(v_hat) + ADAM_EPS) + ADAM_WD * w)
    return delta, m, v


def _landed_specs(tr, wd):
    return [pl.BlockSpec((N_DEV, tr, wd), lambda l, i, ll=ll: (0, jnp.where(l == ll, i, 0), 0)) for ll in range(DEPTH)]


def _device_sum(r_ref):
    acc = r_ref[0].astype(F32)
    for d in range(1, N_DEV):
        acc = acc + r_ref[d].astype(F32)
    return acc


def _sum_parts(recv, tr, name):
    _, r, wd = recv[0].shape

    def body(*refs):
        g_ref = refs[DEPTH]
        for ll in range(DEPTH):
            @pl.when(pl.program_id(0) == ll)
            def _(ll=ll):
                g_ref[0] = _device_sum(refs[ll])

    return pl.pallas_call(
        body,
        name=name,
        grid=(DEPTH, r // tr),
        in_specs=_landed_specs(tr, wd),
        out_specs=pl.BlockSpec((1, tr, wd), lambda l, i: (l, i, 0)),
        out_shape=jax.ShapeDtypeStruct((DEPTH, r, wd), F32),
        compiler_params=_params(("arbitrary", "arbitrary")),
    )(*recv)


def _sum_adamw(recv, w, m, v, tr, name):
    _, r, wd = recv[0].shape

    def body(*refs):
        w_ref, m_ref, v_ref, g_ref, d_ref, mo_ref, vo_ref = refs[DEPTH:]
        for ll in range(DEPTH):
            @pl.when(pl.program_id(0) == ll)
            def _(ll=ll):
                g = _device_sum(refs[ll])
                g_ref[0] = g
                d_ref[0], mo_ref[0], vo_ref[0] = _adamw(w_ref[0], g, m_ref[0], v_ref[0])

    blk = pl.BlockSpec((1, tr, wd), lambda l, i: (l, i, 0))
    shape = jax.ShapeDtypeStruct((DEPTH, r, wd), F32)
    return pl.pallas_call(
        body,
        name=name,
        grid=(DEPTH, r // tr),
        in_specs=_landed_specs(tr, wd) + [blk, blk, blk],
        out_specs=[blk] * 4,
        out_shape=[shape] * 4,
        compiler_params=_params(("arbitrary", "arbitrary")),
    )(*recv, w, m, v)


def _adamw_rows(w, g, m, v, tr, name):
    _, r, wd = w.shape

    def body(w_ref, g_ref, m_ref, v_ref, d_ref, mo_ref, vo_ref):
        d_ref[0], mo_ref[0], vo_ref[0] = _adamw(w_ref[0], g_ref[0], m_ref[0], v_ref[0])

    blk = pl.BlockSpec((1, tr, wd), lambda l, i: (l, i, 0))
    shape = jax.ShapeDtypeStruct(w.shape, F32)
    return pl.pallas_call(
        body,
        name=name,
        grid=(DEPTH, r // tr),
        in_specs=[blk] * 4,
        out_specs=[blk] * 3,
        out_shape=[shape] * 3,
        compiler_params=_params(("arbitrary", "arbitrary")),
    )(w, g, m, v)


def _adamw_small(ws, gs, ms, vs):
    n = len(ws)

    def body(*refs):
        w_r, g_r, m_r, v_r = refs[:n], refs[n:2 * n], refs[2 * n:3 * n], refs[3 * n:4 * n]
        d_o, m_o, v_o = refs[4 * n:5 * n], refs[5 * n:6 * n], refs[6 * n:7 * n]
        for t in range(n):
            d_o[t][...], m_o[t][...], v_o[t][...] = _adamw(w_r[t][...], g_r[t][...], m_r[t][...], v_r[t][...])

    vmem = pl.BlockSpec(memory_space=pltpu.VMEM)
    shapes = [jax.ShapeDtypeStruct(w.shape, F32) for w in ws]
    outs = pl.pallas_call(
        body,
        name="adamw_small",
        in_specs=[vmem] * (4 * n),
        out_specs=[vmem] * (3 * n),
        out_shape=shapes * 3,
    )(*ws, *gs, *ms, *vs)
    return outs[:n], outs[n:2 * n], outs[2 * n:]


def kernel(x, meta_tokens, mix_pre_g, w_in, conv_w, sinks, attn_out_g, conv_out_g, w_out, mix_post_g, mlp_pre_g, w_up, w_down, mlp_post_g, loss_target, m_meta_tokens, m_mix_pre_g, m_w_in, m_conv_w, m_sinks, m_attn_out_g, m_conv_out_g, m_w_out, m_mix_post_g, m_mlp_pre_g, m_w_up, m_w_down, m_mlp_post_g, v_meta_tokens, v_mix_pre_g, v_w_in, v_conv_w, v_sinks, v_attn_out_g, v_conv_out_g, v_w_out, v_mix_post_g, v_mlp_pre_g, v_w_up, v_w_down, v_mlp_post_g):
    seq = x.shape[1]
    lp = BLOCK + seq
    tm = _row_tile(lp)
    tm_mlp = _row_tile(lp, (320, 256, 128))
    tm_dw_mlp = _row_tile(lp, (1664, 1040, 640, 384, 256, 128))
    tm_dw_mix = _row_tile(lp, (832, 640, 384, 256, 128))
    me = 4 * lax.axis_index("x") + 2 * lax.axis_index("y") + lax.axis_index("c")
    cshard = CONV_W // N_DEV
    mshard = D_MODEL // N_DEV

    gather_with = {
        ("in_proj_fwd", 0): [("in", 1)], ("attn_fwd", 0): [("up", 0)], ("mix_out_fwd", 0): [("down", 0)],
        ("mlp_fwd", 0): [("out", 1), ("up", 1), ("down", 1)],
    }
    scatter_with = {
        ("attn_bwd", 1): [("down", 1)], ("mlp_bwd_dx", 0): [("up", 1), ("in", 1), ("out", 1)],
        ("attn_bwd", 0): [("down", 0)], ("mix_bwd_dw", 0): [("up", 0)], ("in_proj_bwd_dx", 0): [("in", 0), ("out", 0)],
    }
    shard = {"in": jnp.swapaxes(w_in, 1, 2).astype(BF), "out": w_out.astype(BF),
             "up": jnp.swapaxes(w_up, 1, 2).astype(BF), "down": w_down.astype(BF)}
    weight = {}
    grad = {}
    landed = {}

    def run(fn, kind, l, *args):
        key, name = (kind, l), f"{kind}_{l}"
        if key in gather_with:
            blocks = gather_with[key]
            outs, lands = fn(*args, name, _Exchange([(shard[n][k], True) for n, k in blocks]))
            for b, land in zip(blocks, lands):
                weight[b] = land.reshape(-1, D_MODEL)
            return outs
        if key in scatter_with:
            blocks = scatter_with[key]
            outs, lands = fn(*args, name, _Exchange([(grad[b].reshape(N_DEV, -1, D_MODEL), False) for b in blocks]))
            landed.update(zip(blocks, lands))
            return outs
        return fn(*args, name)

    small = jnp.zeros((24, 128), F32)
    small = small.at[0:N_META, :].set(meta_tokens)
    small = small.at[N_META:N_META + 6, 0:cshard].set(conv_w.reshape(6, cshard))
    first = _Exchange([(shard["in"][0], True), (small, True), (shard["out"][0], True)])
    h, rope, (first_in, g_small, first_out) = _build_h(x[0], _rope_table(lp), tm, first, 1, "build_h")
    weight[("in", 0)] = first_in.reshape(-1, D_MODEL)
    weight[("out", 0)] = first_out.reshape(-1, D_MODEL)
    cw = g_small[:, N_META:N_META + 6, 0:cshard].reshape(N_DEV, DEPTH, 3, cshard)
    cw = jnp.transpose(cw, (1, 2, 0, 3)).reshape(DEPTH, 3, CONV_W)
    conv_full = jnp.concatenate([cw, jnp.zeros((DEPTH, 5, CONV_W), F32)], axis=1)

    row1 = lambda a, l: a[l].reshape(1, -1)

    saved = []
    for l in range(DEPTH):
        a, qkv, bch = run(_in_proj_fwd, "in_proj_fwd", l, h, row1(mix_pre_g, l), weight[("in", l)], rope, tm)
        y_attn, probs, p_sink = run(_attn_fwd, "attn_fwd", l, qkv, row1(sinks, l))
        y, z, h2 = run(_mix_out_fwd, "mix_out_fwd", l, bch, y_attn, h, conv_full[l], row1(attn_out_g, l),
                       row1(conv_out_g, l), weight[("out", l)], row1(mix_post_g, l), tm)
        mlp = _mlp_fwd if l < DEPTH - 1 else functools.partial(_mlp_fwd, target=loss_target[0])
        a2, up, f, *rest = run(mlp, "mlp_fwd", l, h2, row1(mlp_pre_g, l), weight[("up", l)], weight[("down", l)],
                               row1(mlp_post_g, l), tm_mlp)
        saved.append((h, a, qkv, bch, y_attn, probs, p_sink, y, z, h2, a2, up, f))
        h = rest[0]
    dh, loss_part = rest[0], rest[1][0, 0] * (0.5 / D_MODEL)

    gsmall = [None] * DEPTH
    for l in reversed(range(DEPTH)):
        h0, a, qkv, bch, y_attn, probs, p_sink, y, z, h2, a2, up, f = saved[l]
        df, dup, dh2, dg_mlp = run(_mlp_bwd_dx, "mlp_bwd_dx", l, dh, f, up, h2, weight[("down", l)], weight[("up", l)],
                                   row1(mlp_post_g, l), row1(mlp_pre_g, l), tm_mlp)
        grad[("down", l)], grad[("up", l)] = _mlp_bwd_dw(up, df, dup, a2, tm_dw_mlp, f"mlp_bwd_dw_{l}")
        dz, dya, dbch, dg_mix = run(_mix_out_bwd, "mix_out_bwd", l, dh2, z, y_attn, bch, weight[("out", l)],
                                    row1(mix_post_g, l), row1(attn_out_g, l), row1(conv_out_g, l), conv_full[l], tm)
        dq, dkv, dsink = run(_attn_bwd, "attn_bwd", l, qkv, y_attn, dya, probs, p_sink, rope)
        grad[("in", l)], grad[("out", l)] = run(_mix_bwd_dw, "mix_bwd_dw", l, dq, dkv, dbch, a, y, dz, tm_dw_mix)
        dh, dg_in = run(_in_proj_bwd_dx, "in_proj_bwd_dx", l, dq, dkv, dbch, weight[("in", l)], h0, dh2,
                        row1(mix_pre_g, l), tm)
        tile_a = dg_mlp + dg_in + jnp.pad(dsink, ((0, 0), (0, D_MODEL - 128)))
        gsmall[l] = (tile_a, dg_mix)
    grad_x = dh[BLOCK:][None]

    loss_tile = jnp.zeros((8, D_MODEL), F32).at[ROW_LOSS, 0].set(loss_part)
    tot = _sum_small(jnp.concatenate(
        [gsmall[0][0] + loss_tile, gsmall[0][1], gsmall[1][0], gsmall[1][1], dh[LEAD_PAD:BLOCK]], axis=0))
    loss = tot[ROW_LOSS, 0]
    ta = [tot[16 * l:16 * l + 8] for l in range(DEPTH)]
    tb = [tot[16 * l + 8:16 * l + 16] for l in range(DEPTH)]
    pick = lambda tiles, r0, r1, c0, c1: jnp.stack([t[r0:r1, c0:c1] for t in tiles])
    g_mlp_post = pick(ta, ROW_MLP_POST, ROW_MLP_POST + 1, 0, D_MODEL).reshape(DEPTH, D_MODEL)
    g_mlp_pre = pick(ta, ROW_MLP_PRE, ROW_MLP_PRE + 1, 0, D_MODEL).reshape(DEPTH, D_MODEL)
    g_mix_pre = pick(ta, ROW_MIX_PRE, ROW_MIX_PRE + 1, 0, D_MODEL).reshape(DEPTH, D_MODEL)
    g_sinks = pick(ta, ROW_SINK, ROW_SINK + 1, 0, N_Q_HEADS).reshape(DEPTH, N_Q_HEADS)
    g_mix_post = pick(tb, ROW_MIX_POST, ROW_MIX_POST + 1, 0, D_MODEL).reshape(DEPTH, D_MODEL)
    g_attn_out = pick(tb, ROW_GROUP_G, ROW_GROUP_G + 1, 0, ATTN_W).reshape(DEPTH, ATTN_W)
    g_conv_out = pick(tb, ROW_GROUP_G, ROW_GROUP_G + 1, ATTN_W, D_MODEL).reshape(DEPTH, CONV_W)
    g_conv_full = pick(tb, ROW_CONV, ROW_CONV + 3, 0, CONV_W)
    g_conv = lax.dynamic_slice_in_dim(g_conv_full, me * cshard, cshard, axis=2)
    g_meta = lax.dynamic_slice_in_dim(tot[16 * DEPTH:16 * DEPTH + N_META], me * mshard, mshard, axis=1)

    r_in, r_out, r_up, r_down = [[landed[(n, l)] for l in range(DEPTH)] for n in ("in", "out", "up", "down")]
    g_w_in = jnp.swapaxes(_sum_parts(r_in, 96, "sum_w_in"), 1, 2)
    g_w_up = jnp.swapaxes(_sum_parts(r_up, 128, "sum_w_up"), 1, 2)
    d_w_in, nm_w_in, nv_w_in = _adamw_rows(w_in, g_w_in, m_w_in, v_w_in, 256, "adamw_w_in")
    d_w_up, nm_w_up, nv_w_up = _adamw_rows(w_up, g_w_up, m_w_up, v_w_up, 256, "adamw_w_up")
    g_w_out, d_w_out, nm_w_out, nv_w_out = _sum_adamw(r_out, w_out, m_w_out, v_w_out, 128, "adamw_w_out")
    g_w_down, d_w_down, nm_w_down, nv_w_down = _sum_adamw(r_down, w_down, m_w_down, v_w_down, 128, "adamw_w_down")

    ws = [meta_tokens, mix_pre_g, conv_w.reshape(6, cshard), sinks, attn_out_g, conv_out_g, mix_post_g, mlp_pre_g, mlp_post_g]
    gs = [g_meta, g_mix_pre, g_conv.reshape(6, cshard), g_sinks, g_attn_out, g_conv_out, g_mix_post, g_mlp_pre, g_mlp_post]
    ms = [m_meta_tokens, m_mix_pre_g, m_conv_w.reshape(6, cshard), m_sinks, m_attn_out_g, m_conv_out_g, m_mix_post_g,
          m_mlp_pre_g, m_mlp_post_g]
    vs = [v_meta_tokens, v_mix_pre_g, v_conv_w.reshape(6, cshard), v_sinks, v_attn_out_g, v_conv_out_g, v_mix_post_g,
          v_mlp_pre_g, v_mlp_post_g]
    ds, nms, nvs = _adamw_small(ws, gs, ms, vs)

    def order(meta, mix_pre, cv, sk, a_out, c_out, mix_post, mlp_pre, mlp_post, win, wout, wup, wdown):
        return [meta, mix_pre, win, cv.reshape(DEPTH, 3, cshard), sk, a_out, c_out, wout, mix_post, mlp_pre, wup, wdown, mlp_post]

    grads = order(*gs, g_w_in, g_w_out, g_w_up, g_w_down)
    deltas = order(*ds, d_w_in, d_w_out, d_w_up, d_w_down)
    new_m = order(*nms, nm_w_in, nm_w_out, nm_w_up, nm_w_down)
    new_v = order(*nvs, nv_w_in, nv_w_out, nv_w_up, nv_w_down)
    return (loss, grad_x, *grads, *deltas, *new_m, *new_v)
```

```python
import functools
import math

import jax
import jax.numpy as jnp
from jax import lax
from jax.experimental import pallas as pl
from jax.experimental.pallas import tpu as pltpu

F32 = jnp.float32
BF = jnp.bfloat16

D_MODEL = 1024
ATTN_W = 512
CONV_W = 512
KV_W = 128
HEAD_DIM = 64
N_Q_HEADS = 8
ROT_DIM = 16
D_FF = 4096
IN_W = 2304
N_META = 16
BLOCK = 128
LEAD_PAD = BLOCK - N_META
ROPE_THETA = 500000.0
EPS = 1e-6
N_DEV = 8
DEPTH = 2
NEG = -1e30
SCALE = HEAD_DIM ** -0.5

ADAM_LR = 0.001
ADAM_B1 = 0.9
ADAM_B2 = 0.999
ADAM_EPS = 1e-08
ADAM_WD = 0.01
ADAM_STEP = 10

ROW_MLP_POST, ROW_MLP_PRE, ROW_MIX_PRE, ROW_SINK, ROW_LOSS = 0, 1, 2, 3, 4
ROW_MIX_POST, ROW_GROUP_G, ROW_CONV = 0, 1, 2

VMEM_LIMIT = 56 * 1024 * 1024
MESH = pl.DeviceIdType.MESH


def _dot(a, b):
    return jnp.dot(a, b, preferred_element_type=F32)


def _dot_nt(a, b):
    return lax.dot_general(a, b, (((1,), (1,)), ((), ())), preferred_element_type=F32)


def _dot_tn(a, b):
    return lax.dot_general(a, b, (((0,), (0,)), ((), ())), preferred_element_type=F32)


def _rms_fwd(x, g):
    r = lax.rsqrt(jnp.mean(x * x, axis=-1, keepdims=True) + EPS)
    return x * r * g


def _rms_bwd(x, g, dy):
    r = lax.rsqrt(jnp.mean(x * x, axis=-1, keepdims=True) + EPS)
    xh = x * r
    t = dy * g
    dx = r * (t - xh * jnp.mean(t * xh, axis=-1, keepdims=True))
    dg = jnp.sum(dy * xh, axis=0, keepdims=True)
    return dx, dg


def _row_tile(lp, cands=(640, 512, 384, 256, 128)):
    for t in cands:
        if lp % t == 0:
            return t
    raise ValueError(f"row count {lp} is not a multiple of 128")


def _full(shape):
    n = len(shape)
    return pl.BlockSpec(shape, lambda *_: (0,) * n, pipeline_mode=pl.Buffered(1))


def _full_out(shape):
    n = len(shape)
    return pl.BlockSpec(shape, lambda *_: (0,) * n)


def _params(sem=("arbitrary",)):
    return pltpu.CompilerParams(dimension_semantics=sem, vmem_limit_bytes=VMEM_LIMIT)


def _rope_table(lp):
    half = ROT_DIM // 2
    pos = jnp.maximum(jnp.arange(lp) - LEAD_PAD, 0).astype(F32)
    inv_freq = jnp.power(jnp.float32(ROPE_THETA), -jnp.arange(0, ROT_DIM, 2, dtype=F32) / ROT_DIM)
    ang_t = jnp.concatenate([inv_freq, inv_freq])[:, None] * pos[None, :]
    row = lax.broadcasted_iota(jnp.int32, (ROT_DIM, lp), 0)
    cs_t = jnp.where(row < half, jnp.cos(ang_t), jnp.sin(ang_t))
    return jnp.pad(cs_t.T, ((0, 0), (0, 128 - ROT_DIM)))


def _rope_coeffs(t):
    half = ROT_DIM // 2
    lane = lax.broadcasted_iota(jnp.int32, t.shape, 1)
    cos_a = jnp.where(lane < half, t, 0.0)
    sin_a = pltpu.roll(jnp.where((lane >= half) & (lane < ROT_DIM), t, 0.0), 128 - half, 1)
    c = cos_a + pltpu.roll(cos_a, half, 1) + jnp.where((lane >= ROT_DIM) & (lane < HEAD_DIM), 1.0, 0.0)
    s2 = pltpu.roll(sin_a, half, 1)
    both = lambda u: u + pltpu.roll(u, HEAD_DIM, 1)
    return both(c), both(-sin_a), both(s2)


def _rope(t, c, s1, s2):
    return t * c + pltpu.roll(t, BLOCK - 8, 1) * s1 + pltpu.roll(t, 8, 1) * s2


def _rope_t(dt, c, s1, s2):
    return dt * c + pltpu.roll(dt * s1, 8, 1) + pltpu.roll(dt * s2, BLOCK - 8, 1)


def _build_h(x, rope_compact, tm, exch, small_piece, name):
    seq = x.shape[0]
    lp = BLOCK + seq
    nt = lp // tm
    n_sub = tm // BLOCK
    small_shape = exch.land_shapes[small_piece].shape

    def body(*refs):
        h_ref, c_ref, s1_ref, s2_ref = refs[n_sub + 1:n_sub + 5]
        for j in range(n_sub):
            h_ref[j * BLOCK:(j + 1) * BLOCK, :] = refs[j][...]
        c_ref[...], s1_ref[...], s2_ref[...] = _rope_coeffs(refs[n_sub][...])

    def after(lands, *refs):
        h_ref, buf = refs[n_sub + 1], refs[n_sub + 5]
        pltpu.sync_copy(lands[small_piece], buf)
        h_ref[0:LEAD_PAD, :] = jnp.zeros((LEAD_PAD, D_MODEL), F32)
        for d in range(N_DEV):
            h_ref[LEAD_PAD:BLOCK, d * 128:(d + 1) * 128] = buf[d, 0:N_META, :]

    tile = lambda i: (i + 1) % nt
    piece = lambda j: pl.BlockSpec((BLOCK, D_MODEL), lambda i: (jnp.maximum(tile(i) * n_sub + j - 1, 0), 0))
    rows = lambda w: pl.BlockSpec((tm, w), lambda i: (tile(i), 0))
    (h, *rope), lands = _call(
        body, exch,
        name=name,
        grid=(nt,),
        in_specs=[piece(j) for j in range(n_sub)] + [rows(128)],
        out_specs=[rows(D_MODEL)] + [rows(128)] * 3,
        out_shape=[jax.ShapeDtypeStruct((lp, D_MODEL), F32)] + [jax.ShapeDtypeStruct((lp, 128), F32)] * 3,
        scratch_shapes=[pltpu.VMEM(small_shape, F32)],
        compiler_params=_params(),
        after=after,
    )(*([x] * n_sub), rope_compact)
    return h, rope, lands


def _in_proj_fwd(h, g, w_in_t, rope, tm, name, exch=None):
    lp = h.shape[0]

    def body(h_ref, g_ref, w_ref, c_ref, s1_ref, s2_ref, a_ref, qkv_ref, bch_ref):
        a = _rms_fwd(h_ref[...], g_ref[...]).astype(BF)
        a_ref[...] = a
        proj = _dot_nt(a, w_ref[...])
        c, s1, s2 = c_ref[...], s1_ref[...], s2_ref[...]
        for j in range(5):
            t = _rope(proj[:, j * 128:(j + 1) * 128], c, s1, s2)
            qkv_ref[:, j * 128:(j + 1) * 128] = (t * SCALE if j < 4 else t).astype(BF)
        qkv_ref[:, 640:768] = proj[:, 640:768].astype(BF)
        bch_ref[...] = proj[:, 768:].astype(BF)

    row = lambda w: pl.BlockSpec((tm, w), lambda i: (i, 0))
    return _call(
        body, exch,
        name=name,
        grid=(lp // tm,),
        in_specs=[row(D_MODEL), _full((1, D_MODEL)), _full((IN_W, D_MODEL)), row(128), row(128), row(128)],
        out_specs=[row(D_MODEL), row(768), row(3 * CONV_W)],
        out_shape=[
            jax.ShapeDtypeStruct((lp, D_MODEL), BF),
            jax.ShapeDtypeStruct((lp, 768), BF),
            jax.ShapeDtypeStruct((lp, 3 * CONV_W), BF),
        ],
        compiler_params=_params(),
    )(h, g, w_in_t, *rope)


def _fold_masks(i):
    r = lax.broadcasted_iota(jnp.int32, (2 * BLOCK, BLOCK), 0) & (BLOCK - 1)
    c = lax.broadcasted_iota(jnp.int32, (2 * BLOCK, BLOCK), 1)
    tri = c > r
    ok = jnp.where(tri, (i - 1) * BLOCK + c, i * BLOCK + c) >= LEAD_PAD
    return tri, ok


def _kv_operand(x, kvh):
    lane = lax.broadcasted_iota(jnp.int32, x.shape, 1)
    zero = jnp.zeros_like(x)
    if kvh == 0:
        lo = jnp.where(lane < HEAD_DIM, x, zero)
        hi = pltpu.roll(lo, HEAD_DIM, 1)
    else:
        hi = jnp.where(lane >= HEAD_DIM, x, zero)
        lo = pltpu.roll(hi, HEAD_DIM, 1)
    return jnp.concatenate([lo, hi], axis=0)


def _split4(t, tri):
    zero = jnp.zeros_like(t[0])
    return jnp.concatenate(
        [jnp.where(tri, t[0], zero), jnp.where(tri, zero, t[0]), jnp.where(tri, t[1], zero), jnp.where(tri, zero, t[1])], axis=1)


def _sink_cols(sink_ref, kvh):
    first = lax.broadcasted_iota(jnp.int32, (2 * BLOCK, 1), 0) < BLOCK
    return [jnp.where(first, sink_ref[0, 4 * kvh + half], sink_ref[0, 4 * kvh + 2 + half]) for half in range(2)]


def _folded_exp(q2, k4, tri, ok, sks):
    s = _dot_nt(q2, k4)
    es, ss = [], []
    for half in range(2):
        s_h = s[:, 2 * half * BLOCK:2 * (half + 1) * BLOCK]
        sf = jnp.where(ok, jnp.where(tri, s_h[:, :BLOCK], s_h[:, BLOCK:]), NEG)
        m = jnp.maximum(jnp.max(sf, axis=-1, keepdims=True), sks[half])
        es.append(jnp.exp(sf - m))
        ss.append(jnp.exp(sks[half] - m))
    sums = _dot(jnp.concatenate(es, axis=0).astype(BF), jnp.ones((BLOCK, BLOCK), BF))
    invs = [1.0 / (sums[2 * half * BLOCK:2 * (half + 1) * BLOCK] + ss[half]) for half in range(2)]
    return es, ss, invs


def _attn_fwd(qkv, sink, name, exch=None):
    lp = qkv.shape[0]
    nb = lp // BLOCK

    def body(sink_ref, q_ref, kvc_ref, kvp_ref, o_ref, p_ref, ps_ref):
        i = pl.program_id(0)
        tri, ok = _fold_masks(i)
        kvc, kvp = kvc_ref[...], kvp_ref[...]
        kk = jnp.concatenate([kvp[:, :128], kvc[:, :128]], axis=0)
        vv = jnp.concatenate([kvp[:, 128:], kvc[:, 128:]], axis=0)
        lane = lax.broadcasted_iota(jnp.int32, (BLOCK, 128), 1)
        p_sink = jnp.zeros((BLOCK, 128), F32)
        for kvh in range(2):
            q2 = jnp.concatenate([q_ref[:, 256 * kvh:256 * kvh + 128], q_ref[:, 256 * kvh + 128:256 * kvh + 256]], axis=0)
            es, ss, invs = _folded_exp(q2, _kv_operand(kk, kvh), tri, ok, _sink_cols(sink_ref, kvh))
            pb = [(es[half] * invs[half]).astype(BF) for half in range(2)]
            out = _dot(_split4(pb, tri), _kv_operand(vv, kvh))
            for pair in range(2):
                rows = slice(pair * BLOCK, (pair + 1) * BLOCK)
                o_ref[:, 256 * kvh + 128 * pair:256 * kvh + 128 * (pair + 1)] = out[rows].astype(BF)
                for half in range(2):
                    head = 4 * kvh + 2 * pair + half
                    p_ref[:, 128 * head:128 * (head + 1)] = pb[half][rows]
                    p_sink = jnp.where(lane == head, (ss[half] * invs[half][:, 0:1])[rows], p_sink)
        ps_ref[...] = p_sink

    return _call(
        body, exch,
        name=name,
        grid=(nb,),
        in_specs=[
            pl.BlockSpec(memory_space=pltpu.SMEM),
            pl.BlockSpec((BLOCK, ATTN_W), lambda i: (i, 0)),
            pl.BlockSpec((BLOCK, 256), lambda i: (i, 2)),
            pl.BlockSpec((BLOCK, 256), lambda i: (jnp.maximum(i - 1, 0), 2)),
        ],
        out_specs=[pl.BlockSpec((BLOCK, ATTN_W), lambda i: (i, 0)), pl.BlockSpec((BLOCK, N_Q_HEADS * BLOCK), lambda i: (i, 0)),
                   pl.BlockSpec((BLOCK, 128), lambda i: (i, 0))],
        out_shape=[jax.ShapeDtypeStruct((lp, ATTN_W), BF), jax.ShapeDtypeStruct((lp, N_Q_HEADS * BLOCK), BF),
                   jax.ShapeDtypeStruct((lp, 128), F32)],
        compiler_params=_params(),
    )(sink, qkv, qkv, qkv)


def _mix_out_fwd(bch, y_attn, h, conv_w, g_a, g_c, w_out, g_post, tm, name, exch=None):
    lp = h.shape[0]

    def body(bch_ref, ya_ref, h_ref, cw_ref, ga_ref, gc_ref, w_ref, gp_ref, yc_ref, y_ref, z_ref, h2_ref, ext):
        i = pl.program_id(0)

        @pl.when(i == 0)
        def _():
            ext[0:8, :] = jnp.zeros((8, CONV_W), F32)

        b = bch_ref[:, 0:CONV_W].astype(F32)
        u = bch_ref[:, CONV_W:2 * CONV_W].astype(F32) * bch_ref[:, 2 * CONV_W:3 * CONV_W].astype(F32)
        ext[8:8 + tm, :] = u
        yc = cw_ref[0:1, :] * ext[6:6 + tm, :] + cw_ref[1:2, :] * ext[7:7 + tm, :] + cw_ref[2:3, :] * u
        ext[0:8, :] = u[tm - 8:tm, :]
        yc_ref[...] = yc.astype(BF)
        ya = _rms_fwd(ya_ref[...].astype(F32), ga_ref[...]).astype(BF)
        yb = _rms_fwd(b * yc, gc_ref[...]).astype(BF)
        y_ref[:, 0:ATTN_W] = ya
        y_ref[:, ATTN_W:] = yb
        z = _dot(ya, w_ref[0:ATTN_W, :]) + _dot(yb, w_ref[ATTN_W:, :])
        z_ref[...] = z
        h2_ref[...] = h_ref[...] + _rms_fwd(z, gp_ref[...])

    row = lambda w: pl.BlockSpec((tm, w), lambda i: (i, 0))
    return _call(
        body, exch,
        name=name,
        grid=(lp // tm,),
        in_specs=[
            row(3 * CONV_W), row(ATTN_W), row(D_MODEL), _full((8, CONV_W)), _full((1, ATTN_W)), _full((1, CONV_W)),
            _full((D_MODEL, D_MODEL)), _full((1, D_MODEL)),
        ],
        out_specs=[row(CONV_W), row(D_MODEL), row(D_MODEL), row(D_MODEL)],
        out_shape=[
            jax.ShapeDtypeStruct((lp, CONV_W), BF),
            jax.ShapeDtypeStruct((lp, D_MODEL), BF),
            jax.ShapeDtypeStruct((lp, D_MODEL), F32),
            jax.ShapeDtypeStruct((lp, D_MODEL), F32),
        ],
        scratch_shapes=[pltpu.VMEM((tm + 8, CONV_W), F32)],
        compiler_params=_params(),
    )(bch, y_attn, h, conv_w, g_a, g_c, w_out, g_post)


def _mlp_fwd(h2, g_pre, w_up_t, w_down, g_post, tm, name, exch=None, target=None):
    lp = h2.shape[0]
    sub = math.gcd(tm, BLOCK)
    n_sub, lead = tm // sub, BLOCK // sub
    n_t = n_sub if target is not None else 0

    def body(*refs):
        h_ref, gp_ref, wu_ref, wd_ref, gq_ref = refs[:5]
        t_refs = refs[5:5 + n_t]
        a_ref, up_ref, f_ref, last_ref = refs[5 + n_t:9 + n_t]
        h = h_ref[...]
        a = _rms_fwd(h, gp_ref[...]).astype(BF)
        a_ref[...] = a
        up = _dot_nt(a, wu_ref[...])
        up_ref[...] = up.astype(BF)
        act = jnp.square(jnp.maximum(up, 0.0)).astype(BF)
        f = _dot(act, wd_ref[...])
        f_ref[...] = f
        h3 = h + _rms_fwd(f, gq_ref[...])
        if target is None:
            last_ref[...] = h3
            return
        ls_ref = refs[9 + n_t]
        i = pl.program_id(0)

        @pl.when(i == 0)
        def _():
            ls_ref[...] = jnp.zeros((8, 128), F32)

        sq = jnp.zeros((1, 1), F32)
        for j in range(n_sub):
            on_tokens = i * n_sub + j >= lead
            d = jnp.where(on_tokens, h3[j * sub:(j + 1) * sub] - t_refs[j][...], 0.0)
            last_ref[j * sub:(j + 1) * sub, :] = d * (1.0 / D_MODEL)
            sq = sq + jnp.sum(d * d)
        ls_ref[...] += sq

    row = lambda w: pl.BlockSpec((tm, w), lambda i: (i, 0))
    piece = lambda j: pl.BlockSpec((sub, D_MODEL), lambda i: (jnp.maximum(i * n_sub + j - lead, 0), 0))
    out_specs = [row(D_MODEL), row(D_FF), row(D_MODEL), row(D_MODEL)]
    out_shape = [
        jax.ShapeDtypeStruct((lp, D_MODEL), BF),
        jax.ShapeDtypeStruct((lp, D_FF), BF),
        jax.ShapeDtypeStruct((lp, D_MODEL), F32),
        jax.ShapeDtypeStruct((lp, D_MODEL), F32),
    ]
    if target is not None:
        out_specs.append(_full_out((8, 128)))
        out_shape.append(jax.ShapeDtypeStruct((8, 128), F32))
    return _call(
        body, exch,
        name=name,
        grid=(lp // tm,),
        in_specs=[row(D_MODEL), _full((1, D_MODEL)), _full((D_FF, D_MODEL)), _full((D_FF, D_MODEL)), _full((1, D_MODEL))]
        + [piece(j) for j in range(n_t)],
        out_specs=out_specs,
        out_shape=out_shape,
        compiler_params=_params(),
    )(h2, g_pre, w_up_t, w_down, g_post, *([target] * n_t))


def _mlp_bwd_dx(dh3, f, up, h2, w_down, w_up_t, g_post, g_pre, tm, name, exch=None):
    lp = h2.shape[0]

    def body(dh3_ref, f_ref, up_ref, h2_ref, wd_ref, wu_ref, gq_ref, gp_ref, df_ref, dup_ref, dh2_ref, dg_ref):
        i = pl.program_id(0)

        @pl.when(i == 0)
        def _():
            dg_ref[...] = jnp.zeros((8, D_MODEL), F32)

        dh3 = dh3_ref[...]
        df, dgq = _rms_bwd(f_ref[...], gq_ref[...], dh3)
        dg_ref[ROW_MLP_POST:ROW_MLP_POST + 1, :] += dgq
        df = df.astype(BF)
        df_ref[...] = df
        dact = _dot_nt(df, wd_ref[...])
        dup = (dact * (2.0 * jnp.maximum(up_ref[...].astype(F32), 0.0))).astype(BF)
        dup_ref[...] = dup
        da = _dot(dup, wu_ref[...])
        dh, dgp = _rms_bwd(h2_ref[...], gp_ref[...], da)
        dg_ref[ROW_MLP_PRE:ROW_MLP_PRE + 1, :] += dgp
        dh2_ref[...] = dh3 + dh

    row = lambda w: pl.BlockSpec((tm, w), lambda i: (i, 0))
    return _call(
        body, exch,
        name=name,
        grid=(lp // tm,),
        in_specs=[
            row(D_MODEL), row(D_MODEL), row(D_FF), row(D_MODEL), _full((D_FF, D_MODEL)), _full((D_FF, D_MODEL)),
            _full((1, D_MODEL)), _full((1, D_MODEL)),
        ],
        out_specs=[row(D_MODEL), row(D_FF), row(D_MODEL), _full_out((8, D_MODEL))],
        out_shape=[
            jax.ShapeDtypeStruct((lp, D_MODEL), BF),
            jax.ShapeDtypeStruct((lp, D_FF), BF),
            jax.ShapeDtypeStruct((lp, D_MODEL), F32),
            jax.ShapeDtypeStruct((8, D_MODEL), F32),
        ],
        compiler_params=_params(),
    )(dh3, f, up, h2, w_down, w_up_t, g_post, g_pre)


def _mlp_bwd_dw(up, df, dup, a2, tm, name):
    lp = up.shape[0]
    nt = lp // tm
    nj = D_FF // D_MODEL

    def body(up_ref, df_ref, dup_ref, a_ref, dwd_ref, dwu_ref, accd, accu):
        i = pl.program_id(1)

        @pl.when(i == 0)
        def _():
            accd[...] = jnp.zeros_like(accd)
            accu[...] = jnp.zeros_like(accu)

        act = jnp.square(jnp.maximum(up_ref[...].astype(F32), 0.0)).astype(BF)
        accd[...] += _dot_tn(act, df_ref[...])
        accu[...] += _dot_tn(dup_ref[...], a_ref[...])

        @pl.when(i == nt - 1)
        def _():
            dwd_ref[...] = accd[...].astype(BF)
            dwu_ref[...] = accu[...].astype(BF)

    return pl.pallas_call(
        body,
        name=name,
        grid=(nj, nt),
        in_specs=[
            pl.BlockSpec((tm, D_MODEL), lambda j, i: (i, j)),
            pl.BlockSpec((tm, D_MODEL), lambda j, i: (i, 0)),
            pl.BlockSpec((tm, D_MODEL), lambda j, i: (i, j)),
            pl.BlockSpec((tm, D_MODEL), lambda j, i: (i, 0)),
        ],
        out_specs=[pl.BlockSpec((D_MODEL, D_MODEL), lambda j, i: (j, 0)), pl.BlockSpec((D_MODEL, D_MODEL), lambda j, i: (j, 0))],
        out_shape=[jax.ShapeDtypeStruct((D_FF, D_MODEL), BF), jax.ShapeDtypeStruct((D_FF, D_MODEL), BF)],
        scratch_shapes=[pltpu.VMEM((D_MODEL, D_MODEL), F32), pltpu.VMEM((D_MODEL, D_MODEL), F32)],
        compiler_params=_params(("arbitrary", "arbitrary")),
    )(up, df, dup, a2)


def _mix_out_bwd(dh2, z, y_attn, yc, bch, w_out, g_post, g_a, g_c, conv_w, tm, name, exch=None):
    lp = dh2.shape[0]
    nt = lp // tm

    def body(dh2_ref, z_ref, ya_ref, yc_ref, bch_ref, w_ref, gp_ref, ga_ref, gc_ref, cw_ref,
             dz_ref, dya_ref, dbch_ref, dg_ref, ext):
        i = pl.program_id(0)
        dcw_ref = dg_ref.at[ROW_CONV:ROW_CONV + 3, 0:CONV_W]

        @pl.when(i == 0)
        def _():
            ext[tm:tm + 8, :] = jnp.zeros((8, CONV_W), F32)
            dg_ref[...] = jnp.zeros((8, D_MODEL), F32)

        dz, dgp = _rms_bwd(z_ref[...], gp_ref[...], dh2_ref[...])
        dg_ref[ROW_MIX_POST:ROW_MIX_POST + 1, :] += dgp
        dz = dz.astype(BF)
        dz_ref[...] = dz
        dya_n = _dot_nt(dz, w_ref[0:ATTN_W, :])
        dyb_n = _dot_nt(dz, w_ref[ATTN_W:, :])
        dya, dga = _rms_bwd(ya_ref[...].astype(F32), ga_ref[...], dya_n)
        dg_ref[ROW_GROUP_G:ROW_GROUP_G + 1, 0:ATTN_W] += dga
        dya_ref[...] = dya
        b = bch_ref[:, 0:CONV_W].astype(F32)
        c = bch_ref[:, CONV_W:2 * CONV_W].astype(F32)
        hc = bch_ref[:, 2 * CONV_W:3 * CONV_W].astype(F32)
        u = c * hc
        yc_v = yc_ref[...].astype(F32)
        dyconv, dgc = _rms_bwd(b * yc_v, gc_ref[...], dyb_n)
        dg_ref[ROW_GROUP_G:ROW_GROUP_G + 1, ATTN_W:] += dgc
        dbch_ref[:, 0:CONV_W] = (dyconv * yc_v).astype(BF)
        dyc = dyconv * b
        ext[0:tm, :] = dyc
        d1 = ext[1:1 + tm, :]
        d2 = ext[2:2 + tm, :]
        du = cw_ref[2:3, :] * dyc + cw_ref[1:2, :] * d1 + cw_ref[0:1, :] * d2
        ext[tm:tm + 8, :] = dyc[0:8, :]
        dbch_ref[:, CONV_W:2 * CONV_W] = (du * hc).astype(BF)
        dbch_ref[:, 2 * CONV_W:3 * CONV_W] = (du * c).astype(BF)
        dcw_ref[0:1, :] += jnp.sum(u * d2, axis=0, keepdims=True)
        dcw_ref[1:2, :] += jnp.sum(u * d1, axis=0, keepdims=True)
        dcw_ref[2:3, :] += jnp.sum(u * dyc, axis=0, keepdims=True)

    row = lambda w: pl.BlockSpec((tm, w), lambda i: (nt - 1 - i, 0))
    return _call(
        body, exch,
        name=name,
        grid=(nt,),
        in_specs=[
            row(D_MODEL), row(D_MODEL), row(ATTN_W), row(CONV_W), row(3 * CONV_W), _full((D_MODEL, D_MODEL)),
            _full((1, D_MODEL)), _full((1, ATTN_W)), _full((1, CONV_W)), _full((8, CONV_W)),
        ],
        out_specs=[row(D_MODEL), row(ATTN_W), row(3 * CONV_W), _full_out((8, D_MODEL))],
        out_shape=[
            jax.ShapeDtypeStruct((lp, D_MODEL), BF),
            jax.ShapeDtypeStruct((lp, ATTN_W), F32),
            jax.ShapeDtypeStruct((lp, 3 * CONV_W), BF),
            jax.ShapeDtypeStruct((8, D_MODEL), F32),
        ],
        scratch_shapes=[pltpu.VMEM((tm + 8, CONV_W), F32)],
        compiler_params=_params(),
    )(dh2, z, y_attn, yc, bch, w_out, g_post, g_a, g_c, conv_w)


def _attn_bwd(qkv, o, do, probs, p_sink, rope, name, exch=None):
    lp = qkv.shape[0]
    nb = lp // BLOCK

    def body(q_ref, kvc_ref, kvp_ref, o_ref, do_ref, p_ref, ps_ref, cq_ref, s1q_ref, s2q_ref, ck_ref, s1k_ref, s2k_ref,
             dq_ref, dkv_ref, dsink_ref, carry):
        i = pl.program_id(0)

        @pl.when(i == 0)
        def _():
            carry[...] = jnp.zeros_like(carry)
            dsink_ref[...] = jnp.zeros((8, 128), F32)

        def finish(tot):
            dk = _rope_t(tot[:, :128], ck_ref[...], s1k_ref[...], s2k_ref[...])
            dkv_ref[:, 0:128] = dk.astype(BF)
            dkv_ref[:, 128:256] = tot[:, 128:].astype(BF)

        @pl.when(i < nb)
        def _():
            tri, _ = _fold_masks(i)
            kvc, kvp = kvc_ref[...], kvp_ref[...]
            kk = jnp.concatenate([kvp[:, :128], kvc[:, :128]], axis=0)
            vv = jnp.concatenate([kvp[:, 128:], kvc[:, 128:]], axis=0)
            lane = lax.broadcasted_iota(jnp.int32, (BLOCK, 128), 1)
            lane2 = lax.broadcasted_iota(jnp.int32, (2 * BLOCK, 128), 1)
            rope_q = (cq_ref[...], s1q_ref[...], s2q_ref[...])
            deltas = jnp.zeros((BLOCK, 128), F32)
            folded = []
            for kvh in range(2):
                c0 = 256 * kvh
                q2 = jnp.concatenate([q_ref[:, c0:c0 + 128], q_ref[:, c0 + 128:c0 + 256]], axis=0)
                do2 = jnp.concatenate([do_ref[:, c0:c0 + 128], do_ref[:, c0 + 128:c0 + 256]], axis=0)
                o2 = jnp.concatenate([o_ref[:, c0:c0 + 128], o_ref[:, c0 + 128:c0 + 256]], axis=0).astype(F32)
                k4, v4 = _kv_operand(kk, kvh), _kv_operand(vv, kvh)
                prod = do2 * o2
                dob = do2.astype(BF)
                dp = _dot_nt(dob, v4)
                ds, pb = [], []
                for half in range(2):
                    heads = [4 * kvh + 2 * pair + half for pair in range(2)]
                    p = jnp.concatenate([p_ref[:, 128 * h:128 * (h + 1)] for h in heads], axis=0)
                    sel = (lane2 < HEAD_DIM) if half == 0 else (lane2 >= HEAD_DIM)
                    delta = jnp.sum(jnp.where(sel, prod, 0.0), axis=-1, keepdims=True)
                    dp_h = dp[:, 2 * half * BLOCK:2 * (half + 1) * BLOCK]
                    ds.append((p.astype(F32) * (jnp.where(tri, dp_h[:, :BLOCK], dp_h[:, BLOCK:]) - delta)).astype(BF))
                    pb.append(p)
                    for pair in range(2):
                        deltas = jnp.where(lane == heads[pair], delta[pair * BLOCK:(pair + 1) * BLOCK], deltas)
                ds4, p4 = _split4(ds, tri), _split4(pb, tri)
                dq2 = _dot(ds4, k4) * SCALE
                dq_ref[:, c0:c0 + 128] = _rope_t(dq2[:BLOCK], *rope_q).astype(BF)
                dq_ref[:, c0 + 128:c0 + 256] = _rope_t(dq2[BLOCK:], *rope_q).astype(BF)
                rk, rv = _dot_tn(ds4, q2), _dot_tn(p4, dob)
                own = (lane < HEAD_DIM) if kvh == 0 else (lane >= HEAD_DIM)
                group = []
                for r in (rk, rv):
                    for blk in range(2):
                        t = jnp.where(lane < HEAD_DIM, r[blk * BLOCK:(blk + 1) * BLOCK], r[(2 + blk) * BLOCK:(3 + blk) * BLOCK])
                        group.append(jnp.where(own, t + pltpu.roll(t, HEAD_DIM, 1), 0.0))
                folded.append(group)
            dsink_ref[ROW_SINK:ROW_SINK + 1, :] -= jnp.sum(ps_ref[...] * deltas, axis=0, keepdims=True)
            dk_p, dk_c, dv_p, dv_c = [folded[0][t] + folded[1][t] for t in range(4)]
            finish(carry[...] + jnp.concatenate([dk_p, dv_p], axis=1))
            carry[...] = jnp.concatenate([dk_c, dv_c], axis=1)

        @pl.when(i == nb)
        def _():
            finish(carry[...])

    qi = lambda i: jnp.minimum(i, nb - 1)
    ki = lambda i: jnp.maximum(i - 1, 0)
    tab_q = pl.BlockSpec((BLOCK, 128), lambda i: (qi(i), 0))
    tab_k = pl.BlockSpec((BLOCK, 128), lambda i: (ki(i), 0))
    return _call(
        body, exch,
        name=name,
        grid=(nb + 1,),
        in_specs=[
            pl.BlockSpec((BLOCK, ATTN_W), lambda i: (qi(i), 0)),
            pl.BlockSpec((BLOCK, 256), lambda i: (qi(i), 2)),
            pl.BlockSpec((BLOCK, 256), lambda i: (jnp.maximum(qi(i) - 1, 0), 2)),
            pl.BlockSpec((BLOCK, ATTN_W), lambda i: (qi(i), 0)),
            pl.BlockSpec((BLOCK, ATTN_W), lambda i: (qi(i), 0)),
            pl.BlockSpec((BLOCK, N_Q_HEADS * BLOCK), lambda i: (qi(i), 0)),
            tab_q, tab_q, tab_q, tab_q, tab_k, tab_k, tab_k,
        ],
        out_specs=[
            pl.BlockSpec((BLOCK, ATTN_W), lambda i: (qi(i), 0)),
            pl.BlockSpec((BLOCK, 256), lambda i: (ki(i), 0)),
            pl.BlockSpec((8, 128), lambda i: (0, 0)),
        ],
        out_shape=[
            jax.ShapeDtypeStruct((lp, ATTN_W), BF),
            jax.ShapeDtypeStruct((lp, 256), BF),
            jax.ShapeDtypeStruct((8, 128), F32),
        ],
        scratch_shapes=[pltpu.VMEM((BLOCK, 256), F32)],
        compiler_params=_params(),
    )(qkv, qkv, qkv, o, do, probs, p_sink, *rope, *rope)


def _in_proj_bwd_dx(dq, dkv, dbch, w_in_t, h, dh2, g, tm, name, exch=None):
    lp = h.shape[0]

    def body(dq_ref, dkv_ref, dbch_ref, w_ref, h_ref, dh2_ref, g_ref, dh_ref, dg_ref):
        i = pl.program_id(0)

        @pl.when(i == 0)
        def _():
            dg_ref[...] = jnp.zeros((8, D_MODEL), F32)

        da = _dot(dq_ref[...], w_ref[0:512, :]) + _dot(dkv_ref[...], w_ref[512:768, :]) + _dot(dbch_ref[...], w_ref[768:, :])
        dh, dg = _rms_bwd(h_ref[...], g_ref[...], da)
        dg_ref[ROW_MIX_PRE:ROW_MIX_PRE + 1, :] += dg
        dh_ref[...] = dh2_ref[...] + dh

    row = lambda w: pl.BlockSpec((tm, w), lambda i: (i, 0))
    return _call(
        body, exch,
        name=name,
        grid=(lp // tm,),
        in_specs=[row(ATTN_W), row(256), row(3 * CONV_W), _full((IN_W, D_MODEL)), row(D_MODEL), row(D_MODEL), _full((1, D_MODEL))],
        out_specs=[row(D_MODEL), _full_out((8, D_MODEL))],
        out_shape=[jax.ShapeDtypeStruct((lp, D_MODEL), F32), jax.ShapeDtypeStruct((8, D_MODEL), F32)],
        compiler_params=_params(),
    )(dq, dkv, dbch, w_in_t, h, dh2, g)


def _mix_bwd_dw(dq, dkv, dbch, a, y, dz, tm, name, exch=None):
    lp = a.shape[0]
    nt = lp // tm

    def body(dq_ref, dkv_ref, dbch_ref, a_ref, y_ref, dz_ref, dwi_ref, dwo_ref, acci, acco):
        i = pl.program_id(0)

        @pl.when(i == 0)
        def _():
            acci[...] = jnp.zeros_like(acci)
            acco[...] = jnp.zeros_like(acco)

        a_v = a_ref[...]
        acci[0:512, :] += _dot_tn(dq_ref[...], a_v)
        acci[512:768, :] += _dot_tn(dkv_ref[...], a_v)
        acci[768:, :] += _dot_tn(dbch_ref[...], a_v)
        acco[...] += _dot_tn(y_ref[...], dz_ref[...])

        @pl.when(i == nt - 1)
        def _():
            dwi_ref[...] = acci[...].astype(BF)
            dwo_ref[...] = acco[...].astype(BF)

    row = lambda w: pl.BlockSpec((tm, w), lambda i: (i, 0))
    return _call(
        body, exch,
        name=name,
        grid=(nt,),
        in_specs=[row(ATTN_W), row(256), row(3 * CONV_W), row(D_MODEL), row(D_MODEL), row(D_MODEL)],
        out_specs=[_full_out((IN_W, D_MODEL)), _full_out((D_MODEL, D_MODEL))],
        out_shape=[jax.ShapeDtypeStruct((IN_W, D_MODEL), BF), jax.ShapeDtypeStruct((D_MODEL, D_MODEL), BF)],
        scratch_shapes=[pltpu.VMEM((IN_W, D_MODEL), F32), pltpu.VMEM((D_MODEL, D_MODEL), F32)],
        compiler_params=_params(),
    )(dq, dkv, dbch, a, y, dz)


def _mesh_place():
    x, y, c = lax.axis_index("x"), lax.axis_index("y"), lax.axis_index("c")
    return x, y, c, 4 * x + 2 * y + c


def _peer(x, y, c, k):
    px = 1 - x if k & 4 else x
    py = 1 - y if k & 2 else y
    pc = 1 - c if k & 1 else c
    return (px, py, pc), 4 * px + 2 * py + pc


SIBLING = 1
SAME_CORE = (2, 4, 6)
OTHER_CORE = (3, 5, 7)


class _Exchange:
    def __init__(self, pieces):
        self.srcs = [s for s, _ in pieces]
        self.to_all = [g for _, g in pieces]
        self.n = len(pieces)
        self.land_shapes = [
            jax.ShapeDtypeStruct((N_DEV,) + (s.shape if g else s.shape[1:]), s.dtype) for s, g in pieces]
        self.sem_shapes = [pltpu.SemaphoreType.DMA((self.n, N_DEV - 1)), pltpu.SemaphoreType.DMA((self.n, N_DEV - 1)),
                           pltpu.SemaphoreType.DMA((self.n,))]
        self.forwards = any(self.to_all)

    def _ops(self, srcs, lands, sems):
        send_sems, recv_sems, local_sems = sems
        x, y, c, me = _mesh_place()

        def remote(p, k, src, slot, to):
            return pltpu.make_async_remote_copy(
                src_ref=src, dst_ref=lands[p].at[slot], send_sem=send_sems.at[p, k - 1], recv_sem=recv_sems.at[p, k - 1],
                device_id=to, device_id_type=MESH)

        def own(p):
            return pltpu.make_async_copy(srcs[p] if self.to_all[p] else srcs[p].at[me], lands[p].at[me], local_sems.at[p])

        def direct(p, k):
            peer, pidx = _peer(x, y, c, k)
            return remote(p, k, srcs[p] if self.to_all[p] else srcs[p].at[pidx], me, peer)

        def forward(p, k):
            sibling, _ = _peer(x, y, c, SIBLING)
            _, origin = _peer(x, y, c, k ^ SIBLING)
            return remote(p, k, lands[p].at[origin], origin, sibling)

        def arrival(p, k):
            peer, pidx = _peer(x, y, c, k)
            return remote(p, k, lands[p].at[pidx], pidx, peer)

        return own, direct, forward, arrival

    def start(self, srcs, lands, sems):
        own, direct, _, _ = self._ops(srcs, lands, sems)
        for p in range(self.n):
            own(p).start()
            for k in ((SIBLING,) + SAME_CORE) if self.to_all[p] else range(1, N_DEV):
                direct(p, k).start()

    def forward(self, srcs, lands, sems):
        _, _, forward, arrival = self._ops(srcs, lands, sems)
        for p in range(self.n):
            if self.to_all[p]:
                for k in SAME_CORE:
                    arrival(p, k).wait_recv()
                    forward(p, k ^ SIBLING).start()

    def finish(self, srcs, lands, sems):
        own, direct, forward, arrival = self._ops(srcs, lands, sems)
        for p in range(self.n):
            for k in ((SIBLING,) + OTHER_CORE) if self.to_all[p] else range(1, N_DEV):
                arrival(p, k).wait_recv()
        for p in range(self.n):
            for k in range(1, N_DEV):
                (forward(p, k) if self.to_all[p] and k in OTHER_CORE else direct(p, k)).wait_send()
            own(p).wait()


def _call(body, exch, *, name, grid, in_specs, out_specs, out_shape, scratch_shapes=(), compiler_params, after=None):
    if exch is None:
        return pl.pallas_call(body, name=name, grid=grid, in_specs=in_specs, out_specs=out_specs, out_shape=out_shape,
                              scratch_shapes=scratch_shapes, compiler_params=compiler_params)
    n_in, n_out, n_scr, n_x = len(in_specs), len(out_shape), len(scratch_shapes), exch.n
    steps = math.prod(grid)

    def carrying(*refs):
        a, b, c, d, e = n_in, n_in + n_x, n_in + n_x + n_out, n_in + 2 * n_x + n_out, n_in + 2 * n_x + n_out + n_scr
        ins, srcs, outs, lands, scr, sems = refs[:a], refs[a:b], refs[b:c], refs[c:d], refs[d:e], refs[e:]
        step = functools.reduce(lambda acc, t: acc * grid[t] + pl.program_id(t), range(len(grid)), 0)

        @pl.when(step == 0)
        def _():
            exch.start(srcs, lands, sems)

        body(*ins, *outs, *scr)

        if exch.forwards:
            @pl.when(step == max(0, steps - 1 - (steps + 7) // 8))
            def _():
                exch.forward(srcs, lands, sems)

        @pl.when(step == steps - 1)
        def _():
            exch.finish(srcs, lands, sems)
            if after is not None:
                after(lands, *ins, *outs, *scr)

    hbm = pl.BlockSpec(memory_space=pl.ANY)
    call = pl.pallas_call(
        carrying, name=name, grid=grid, in_specs=list(in_specs) + [hbm] * n_x, out_specs=list(out_specs) + [hbm] * n_x,
        out_shape=list(out_shape) + exch.land_shapes, scratch_shapes=list(scratch_shapes) + exch.sem_shapes,
        compiler_params=compiler_params)

    def run(*args):
        res = call(*args, *exch.srcs)
        return list(res[:n_out]), list(res[n_out:])

    return run


def _sum_small(part):
    def body(part_ref, out_ref, land, send_sems, recv_sems):
        x, y, c, me = _mesh_place()
        land[me] = part_ref[...]
        sent = []
        for k in range(1, N_DEV):
            peer, _ = _peer(x, y, c, k)
            cp = pltpu.make_async_remote_copy(
                src_ref=part_ref, dst_ref=land.at[me], send_sem=send_sems.at[k - 1], recv_sem=recv_sems.at[k - 1],
                device_id=peer, device_id_type=MESH)
            cp.start()
            sent.append(cp)
        for k in range(1, N_DEV):
            peer, pidx = _peer(x, y, c, k)
            pltpu.make_async_remote_copy(
                src_ref=part_ref, dst_ref=land.at[pidx], send_sem=send_sems.at[k - 1], recv_sem=recv_sems.at[k - 1],
                device_id=peer, device_id_type=MESH).wait_recv()
        for cp in sent:
            cp.wait_send()
        acc = land[0]
        for d in range(1, N_DEV):
            acc = acc + land[d]
        out_ref[...] = acc

    vmem = pl.BlockSpec(memory_space=pltpu.VMEM)
    return pl.pallas_call(
        body,
        name="sum_small",
        in_specs=[vmem],
        out_specs=vmem,
        out_shape=jax.ShapeDtypeStruct(part.shape, F32),
        scratch_shapes=[pltpu.VMEM((N_DEV,) + part.shape, F32), pltpu.SemaphoreType.DMA((N_DEV - 1,)),
                        pltpu.SemaphoreType.DMA((N_DEV - 1,))],
    )(part)


def _adamw(w, g, m, v):
    m = ADAM_B1 * m + (1.0 - ADAM_B1) * g
    v = ADAM_B2 * v + (1.0 - ADAM_B2) * jnp.square(g)
    m_hat = m / (1.0 - ADAM_B1 ** ADAM_STEP)
    v_hat = v / (1.0 - ADAM_B2 ** ADAM_STEP)
    delta = -ADAM_LR * (m_hat / (jnp.sqrt(v_hat) + ADAM_EPS) + ADAM_WD * w)
    return delta, m, v


def _landed_specs(tr, wd):
    return [pl.BlockSpec((N_DEV, tr, wd), lambda l, i, ll=ll: (0, jnp.where(l == ll, i, 0), 0)) for ll in range(DEPTH)]


def _device_sum(r_ref):
    acc = r_ref[0].astype(F32)
    for d in range(1, N_DEV):
        acc = acc + r_ref[d].astype(F32)
    return acc


def _sum_adamw(recv, w, m, v, tr, name, transposed=False):
    _, r, wd = recv[0].shape

    def body(*refs):
        w_ref, m_ref, v_ref, g_ref, d_ref, mo_ref, vo_ref = refs[DEPTH:]
        for ll in range(DEPTH):
            @pl.when(pl.program_id(0) == ll)
            def _(ll=ll):
                g = _device_sum(refs[ll])
                g = g.T if transposed else g
                g_ref[0] = g
                d_ref[0], mo_ref[0], vo_ref[0] = _adamw(w_ref[0], g, m_ref[0], v_ref[0])

    if transposed:
        blk = pl.BlockSpec((1, wd, tr), lambda l, i: (l, 0, i))
        shape = jax.ShapeDtypeStruct((DEPTH, wd, r), F32)
    else:
        blk = pl.BlockSpec((1, tr, wd), lambda l, i: (l, i, 0))
        shape = jax.ShapeDtypeStruct((DEPTH, r, wd), F32)
    return pl.pallas_call(
        body,
        name=name,
        grid=(DEPTH, r // tr),
        in_specs=_landed_specs(tr, wd) + [blk, blk, blk],
        out_specs=[blk] * 4,
        out_shape=[shape] * 4,
        compiler_params=_params(("arbitrary", "arbitrary")),
    )(*recv, w, m, v)


def _adamw_small(ws, gs, ms, vs):
    n = len(ws)

    def body(*refs):
        w_r, g_r, m_r, v_r = refs[:n], refs[n:2 * n], refs[2 * n:3 * n], refs[3 * n:4 * n]
        d_o, m_o, v_o = refs[4 * n:5 * n], refs[5 * n:6 * n], refs[6 * n:7 * n]
        for t in range(n):
            d_o[t][...], m_o[t][...], v_o[t][...] = _adamw(w_r[t][...], g_r[t][...], m_r[t][...], v_r[t][...])

    vmem = pl.BlockSpec(memory_space=pltpu.VMEM)
    shapes = [jax.ShapeDtypeStruct(w.shape, F32) for w in ws]
    outs = pl.pallas_call(
        body,
        name="adamw_small",
        in_specs=[vmem] * (4 * n),
        out_specs=[vmem] * (3 * n),
        out_shape=shapes * 3,
    )(*ws, *gs, *ms, *vs)
    return outs[:n], outs[n:2 * n], outs[2 * n:]


def kernel(x, meta_tokens, mix_pre_g, w_in, conv_w, sinks, attn_out_g, conv_out_g, w_out, mix_post_g, mlp_pre_g, w_up, w_down, mlp_post_g, loss_target, m_meta_tokens, m_mix_pre_g, m_w_in, m_conv_w, m_sinks, m_attn_out_g, m_conv_out_g, m_w_out, m_mix_post_g, m_mlp_pre_g, m_w_up, m_w_down, m_mlp_post_g, v_meta_tokens, v_mix_pre_g, v_w_in, v_conv_w, v_sinks, v_attn_out_g, v_conv_out_g, v_w_out, v_mix_post_g, v_mlp_pre_g, v_w_up, v_w_down, v_mlp_post_g):
    seq = x.shape[1]
    lp = BLOCK + seq
    tm = _row_tile(lp)
    tm_mlp = _row_tile(lp, (320, 256, 128))
    tm_dw_mlp = _row_tile(lp, (1664, 1040, 640, 384, 256, 128))
    tm_dw_mix = _row_tile(lp, (832, 640, 384, 256, 128))
    me = 4 * lax.axis_index("x") + 2 * lax.axis_index("y") + lax.axis_index("c")
    cshard = CONV_W // N_DEV
    mshard = D_MODEL // N_DEV

    gather_with = {
        ("in_proj_fwd", 0): [("in", 1)], ("attn_fwd", 0): [("up", 0)], ("mix_out_fwd", 0): [("down", 0)],
        ("mlp_fwd", 0): [("out", 1), ("up", 1), ("down", 1)],
    }
    scatter_with = {
        ("attn_bwd", 1): [("down", 1)], ("mlp_bwd_dx", 0): [("up", 1), ("in", 1), ("out", 1)],
        ("attn_bwd", 0): [("down", 0)], ("mix_bwd_dw", 0): [("up", 0)], ("in_proj_bwd_dx", 0): [("in", 0), ("out", 0)],
    }
    shard = {"in": jnp.swapaxes(w_in, 1, 2).astype(BF), "out": w_out.astype(BF),
             "up": jnp.swapaxes(w_up, 1, 2).astype(BF), "down": w_down.astype(BF)}
    weight = {}
    grad = {}
    landed = {}

    def run(fn, kind, l, *args):
        key, name = (kind, l), f"{kind}_{l}"
        if key in gather_with:
            blocks = gather_with[key]
            outs, lands = fn(*args, name, _Exchange([(shard[n][k], True) for n, k in blocks]))
            for b, land in zip(blocks, lands):
                weight[b] = land.reshape(-1, D_MODEL)
            return outs
        if key in scatter_with:
            blocks = scatter_with[key]
            outs, lands = fn(*args, name, _Exchange([(grad[b].reshape(N_DEV, -1, D_MODEL), False) for b in blocks]))
            landed.update(zip(blocks, lands))
            return outs
        return fn(*args, name)

    small = jnp.zeros((24, 128), F32)
    small = small.at[0:N_META, :].set(meta_tokens)
    small = small.at[N_META:N_META + 6, 0:cshard].set(conv_w.reshape(6, cshard))
    first = _Exchange([(shard["in"][0], True), (small, True), (shard["out"][0], True)])
    h, rope, (first_in, g_small, first_out) = _build_h(x[0], _rope_table(lp), tm, first, 1, "build_h")
    weight[("in", 0)] = first_in.reshape(-1, D_MODEL)
    weight[("out", 0)] = first_out.reshape(-1, D_MODEL)
    cw = g_small[:, N_META:N_META + 6, 0:cshard].reshape(N_DEV, DEPTH, 3, cshard)
    cw = jnp.transpose(cw, (1, 2, 0, 3)).reshape(DEPTH, 3, CONV_W)
    conv_full = jnp.concatenate([cw, jnp.zeros((DEPTH, 5, CONV_W), F32)], axis=1)

    row1 = lambda a, l: a[l].reshape(1, -1)

    saved = []
    for l in range(DEPTH):
        a, qkv, bch = run(_in_proj_fwd, "in_proj_fwd", l, h, row1(mix_pre_g, l), weight[("in", l)], rope, tm)
        y_attn, probs, p_sink = run(_attn_fwd, "attn_fwd", l, qkv, row1(sinks, l))
        yc, y, z, h2 = run(_mix_out_fwd, "mix_out_fwd", l, bch, y_attn, h, conv_full[l], row1(attn_out_g, l),
                       row1(conv_out_g, l), weight[("out", l)], row1(mix_post_g, l), tm)
        mlp = _mlp_fwd if l < DEPTH - 1 else functools.partial(_mlp_fwd, target=loss_target[0])
        a2, up, f, *rest = run(mlp, "mlp_fwd", l, h2, row1(mlp_pre_g, l), weight[("up", l)], weight[("down", l)],
                               row1(mlp_post_g, l), tm_mlp)
        saved.append((h, a, qkv, bch, y_attn, probs, p_sink, yc, y, z, h2, a2, up, f))
        h = rest[0]
    dh, loss_part = rest[0], rest[1][0, 0] * (0.5 / D_MODEL)

    gsmall = [None] * DEPTH
    for l in reversed(range(DEPTH)):
        h0, a, qkv, bch, y_attn, probs, p_sink, yc, y, z, h2, a2, up, f = saved[l]
        df, dup, dh2, dg_mlp = run(_mlp_bwd_dx, "mlp_bwd_dx", l, dh, f, up, h2, weight[("down", l)], weight[("up", l)],
                                   row1(mlp_post_g, l), row1(mlp_pre_g, l), tm_mlp)
        grad[("down", l)], grad[("up", l)] = _mlp_bwd_dw(up, df, dup, a2, tm_dw_mlp, f"mlp_bwd_dw_{l}")
        dz, dya, dbch, dg_mix = run(_mix_out_bwd, "mix_out_bwd", l, dh2, z, y_attn, yc, bch, weight[("out", l)],
                                    row1(mix_post_g, l), row1(attn_out_g, l), row1(conv_out_g, l), conv_full[l], tm)
        dq, dkv, dsink = run(_attn_bwd, "attn_bwd", l, qkv, y_attn, dya, probs, p_sink, rope)
        grad[("in", l)], grad[("out", l)] = run(_mix_bwd_dw, "mix_bwd_dw", l, dq, dkv, dbch, a, y, dz, tm_dw_mix)
        dh, dg_in = run(_in_proj_bwd_dx, "in_proj_bwd_dx", l, dq, dkv, dbch, weight[("in", l)], h0, dh2,
                        row1(mix_pre_g, l), tm)
        tile_a = dg_mlp + dg_in + jnp.pad(dsink, ((0, 0), (0, D_MODEL - 128)))
        gsmall[l] = (tile_a, dg_mix)
    grad_x = dh[BLOCK:][None]

    loss_tile = jnp.zeros((8, D_MODEL), F32).at[ROW_LOSS, 0].set(loss_part)
    tot = _sum_small(jnp.concatenate(
        [gsmall[0][0] + loss_tile, gsmall[0][1], gsmall[1][0], gsmall[1][1], dh[LEAD_PAD:BLOCK]], axis=0))
    loss = tot[ROW_LOSS, 0]
    ta = [tot[16 * l:16 * l + 8] for l in range(DEPTH)]
    tb = [tot[16 * l + 8:16 * l + 16] for l in range(DEPTH)]
    pick = lambda tiles, r0, r1, c0, c1: jnp.stack([t[r0:r1, c0:c1] for t in tiles])
    g_mlp_post = pick(ta, ROW_MLP_POST, ROW_MLP_POST + 1, 0, D_MODEL).reshape(DEPTH, D_MODEL)
    g_mlp_pre = pick(ta, ROW_MLP_PRE, ROW_MLP_PRE + 1, 0, D_MODEL).reshape(DEPTH, D_MODEL)
    g_mix_pre = pick(ta, ROW_MIX_PRE, ROW_MIX_PRE + 1, 0, D_MODEL).reshape(DEPTH, D_MODEL)
    g_sinks = pick(ta, ROW_SINK, ROW_SINK + 1, 0, N_Q_HEADS).reshape(DEPTH, N_Q_HEADS)
    g_mix_post = pick(tb, ROW_MIX_POST, ROW_MIX_POST + 1, 0, D_MODEL).reshape(DEPTH, D_MODEL)
    g_attn_out = pick(tb, ROW_GROUP_G, ROW_GROUP_G + 1, 0, ATTN_W).reshape(DEPTH, ATTN_W)
    g_conv_out = pick(tb, ROW_GROUP_G, ROW_GROUP_G + 1, ATTN_W, D_MODEL).reshape(DEPTH, CONV_W)
    g_conv_full = pick(tb, ROW_CONV, ROW_CONV + 3, 0, CONV_W)
    g_conv = lax.dynamic_slice_in_dim(g_conv_full, me * cshard, cshard, axis=2)
    g_meta = lax.dynamic_slice_in_dim(tot[16 * DEPTH:16 * DEPTH + N_META], me * mshard, mshard, axis=1)

    r_in, r_out, r_up, r_down = [[landed[(n, l)] for l in range(DEPTH)] for n in ("in", "out", "up", "down")]
    g_w_in, d_w_in, nm_w_in, nv_w_in = _sum_adamw(
        r_in, w_in, m_w_in, v_w_in, r_in[0].shape[1], "adamw_w_in", transposed=True)
    g_w_up, d_w_up, nm_w_up, nv_w_up = _sum_adamw(r_up, w_up, m_w_up, v_w_up, 128, "adamw_w_up", transposed=True)
    g_w_out, d_w_out, nm_w_out, nv_w_out = _sum_adamw(r_out, w_out, m_w_out, v_w_out, 128, "adamw_w_out")
    g_w_down, d_w_down, nm_w_down, nv_w_down = _sum_adamw(r_down, w_down, m_w_down, v_w_down, 128, "adamw_w_down")

    ws = [meta_tokens, mix_pre_g, conv_w.reshape(6, cshard), sinks, attn_out_g, conv_out_g, mix_post_g, mlp_pre_g, mlp_post_g]
    gs = [g_meta, g_mix_pre, g_conv.reshape(6, cshard), g_sinks, g_attn_out, g_conv_out, g_mix_post, g_mlp_pre, g_mlp_post]
    ms = [m_meta_tokens, m_mix_pre_g, m_conv_w.reshape(6, cshard), m_sinks, m_attn_out_g, m_conv_out_g, m_mix_post_g,
          m_mlp_pre_g, m_mlp_post_g]
    vs = [v_meta_tokens, v_mix_pre_g, v_conv_w.reshape(6, cshard), v_sinks, v_attn_out_g, v_conv_out_g, v_mix_post_g,
          v_mlp_pre_g, v_mlp_post_g]
    ds, nms, nvs = _adamw_small(ws, gs, ms, vs)

    def order(meta, mix_pre, cv, sk, a_out, c_out, mix_post, mlp_pre, mlp_post, win, wout, wup, wdown):
        return [meta, mix_pre, win, cv.reshape(DEPTH, 3, cshard), sk, a_out, c_out, wout, mix_post, mlp_pre, wup, wdown, mlp_post]

    grads = order(*gs, g_w_in, g_w_out, g_w_up, g_w_down)
    deltas = order(*ds, d_w_in, d_w_out, d_w_up, d_w_down)
    new_m = order(*nms, nm_w_in, nm_w_out, nm_w_up, nm_w_down)
    new_v = order(*nvs, nv_w_in, nv_w_out, nv_w_up, nv_w_down)
    return (loss, grad_x, *grads, *deltas, *new_m, *new_v)
```

```python
import functools
import math

import jax
import jax.numpy as jnp
from jax import lax
from jax.experimental import pallas as pl
from jax.experimental.pallas import tpu as pltpu

F32 = jnp.float32
BF = jnp.bfloat16

D_MODEL = 1024
ATTN_W = 512
CONV_W = 512
KV_W = 128
HEAD_DIM = 64
N_Q_HEADS = 8
ROT_DIM = 16
D_FF = 4096
IN_W = 2304
N_META = 16
BLOCK = 128
LEAD_PAD = BLOCK - N_META
ROPE_THETA = 500000.0
EPS = 1e-6
N_DEV = 8
DEPTH = 2
NEG = -1e30
SCALE = HEAD_DIM ** -0.5

ADAM_LR = 0.001
ADAM_B1 = 0.9
ADAM_B2 = 0.999
ADAM_EPS = 1e-08
ADAM_WD = 0.01
ADAM_STEP = 10

ROW_MLP_POST, ROW_MLP_PRE, ROW_MIX_PRE, ROW_SINK, ROW_LOSS = 0, 1, 2, 3, 4
ROW_MIX_POST, ROW_GROUP_G, ROW_CONV = 0, 1, 2

VMEM_LIMIT = 56 * 1024 * 1024
MESH = pl.DeviceIdType.MESH


def _dot(a, b):
    return jnp.dot(a, b, preferred_element_type=F32)


def _dot_nt(a, b):
    return lax.dot_general(a, b, (((1,), (1,)), ((), ())), preferred_element_type=F32)


def _dot_tn(a, b):
    return lax.dot_general(a, b, (((0,), (0,)), ((), ())), preferred_element_type=F32)


def _rms_fwd(x, g):
    r = lax.rsqrt(jnp.mean(x * x, axis=-1, keepdims=True) + EPS)
    return x * r * g


def _rms_bwd(x, g, dy):
    r = lax.rsqrt(jnp.mean(x * x, axis=-1, keepdims=True) + EPS)
    xh = x * r
    t = dy * g
    dx = r * (t - xh * jnp.mean(t * xh, axis=-1, keepdims=True))
    dg = jnp.sum(dy * xh, axis=0, keepdims=True)
    return dx, dg


def _row_tile(lp, cands=(640, 512, 384, 256, 128)):
    for t in cands:
        if lp % t == 0:
            return t
    raise ValueError(f"row count {lp} is not a multiple of 128")


def _full(shape):
    n = len(shape)
    return pl.BlockSpec(shape, lambda *_: (0,) * n, pipeline_mode=pl.Buffered(1))


def _full_out(shape):
    n = len(shape)
    return pl.BlockSpec(shape, lambda *_: (0,) * n)


def _params(sem=("arbitrary",)):
    return pltpu.CompilerParams(dimension_semantics=sem, vmem_limit_bytes=VMEM_LIMIT)


def _rope_table(lp):
    half = ROT_DIM // 2
    pos = jnp.maximum(jnp.arange(lp) - LEAD_PAD, 0).astype(F32)
    inv_freq = jnp.power(jnp.float32(ROPE_THETA), -jnp.arange(0, ROT_DIM, 2, dtype=F32) / ROT_DIM)
    ang_t = jnp.concatenate([inv_freq, inv_freq])[:, None] * pos[None, :]
    row = lax.broadcasted_iota(jnp.int32, (ROT_DIM, lp), 0)
    cs_t = jnp.where(row < half, jnp.cos(ang_t), jnp.sin(ang_t))
    return jnp.pad(cs_t.T, ((0, 0), (0, 128 - ROT_DIM)))


def _rope_coeffs(t):
    half = ROT_DIM // 2
    lane = lax.broadcasted_iota(jnp.int32, t.shape, 1)
    cos_a = jnp.where(lane < half, t, 0.0)
    sin_a = pltpu.roll(jnp.where((lane >= half) & (lane < ROT_DIM), t, 0.0), 128 - half, 1)
    c = cos_a + pltpu.roll(cos_a, half, 1) + jnp.where((lane >= ROT_DIM) & (lane < HEAD_DIM), 1.0, 0.0)
    s2 = pltpu.roll(sin_a, half, 1)
    both = lambda u: u + pltpu.roll(u, HEAD_DIM, 1)
    return both(c), both(-sin_a), both(s2)


def _rope(t, c, s1, s2):
    return t * c + pltpu.roll(t, BLOCK - 8, 1) * s1 + pltpu.roll(t, 8, 1) * s2


def _rope_t(dt, c, s1, s2):
    return dt * c + pltpu.roll(dt * s1, 8, 1) + pltpu.roll(dt * s2, BLOCK - 8, 1)


def _build_h(x, rope_compact, tm, exch, small_piece, name):
    seq = x.shape[0]
    lp = BLOCK + seq
    nt = lp // tm
    n_sub = tm // BLOCK
    small_shape = exch.land_shapes[small_piece].shape

    def body(*refs):
        h_ref, c_ref, s1_ref, s2_ref = refs[n_sub + 1:n_sub + 5]
        for j in range(n_sub):
            h_ref[j * BLOCK:(j + 1) * BLOCK, :] = refs[j][...]
        c_ref[...], s1_ref[...], s2_ref[...] = _rope_coeffs(refs[n_sub][...])

    def after(lands, *refs):
        h_ref, buf = refs[n_sub + 1], refs[n_sub + 5]
        pltpu.sync_copy(lands[small_piece], buf)
        h_ref[0:LEAD_PAD, :] = jnp.zeros((LEAD_PAD, D_MODEL), F32)
        for d in range(N_DEV):
            h_ref[LEAD_PAD:BLOCK, d * 128:(d + 1) * 128] = buf[d, 0:N_META, :]

    tile = lambda i: (i + 1) % nt
    piece = lambda j: pl.BlockSpec((BLOCK, D_MODEL), lambda i: (jnp.maximum(tile(i) * n_sub + j - 1, 0), 0))
    rows = lambda w: pl.BlockSpec((tm, w), lambda i: (tile(i), 0))
    (h, *rope), lands = _call(
        body, exch,
        name=name,
        grid=(nt,),
        in_specs=[piece(j) for j in range(n_sub)] + [rows(128)],
        out_specs=[rows(D_MODEL)] + [rows(128)] * 3,
        out_shape=[jax.ShapeDtypeStruct((lp, D_MODEL), F32)] + [jax.ShapeDtypeStruct((lp, 128), F32)] * 3,
        scratch_shapes=[pltpu.VMEM(small_shape, F32)],
        compiler_params=_params(),
        after=after,
    )(*([x] * n_sub), rope_compact)
    return h, rope, lands


def _in_proj_fwd(h, g, w_in_t, rope, tm, name, exch=None):
    lp = h.shape[0]

    def body(h_ref, g_ref, w_ref, c_ref, s1_ref, s2_ref, a_ref, qkv_ref, bch_ref):
        a = _rms_fwd(h_ref[...], g_ref[...]).astype(BF)
        a_ref[...] = a
        proj = _dot_nt(a, w_ref[...])
        c, s1, s2 = c_ref[...], s1_ref[...], s2_ref[...]
        for j in range(5):
            t = _rope(proj[:, j * 128:(j + 1) * 128], c, s1, s2)
            qkv_ref[:, j * 128:(j + 1) * 128] = (t * SCALE if j < 4 else t).astype(BF)
        qkv_ref[:, 640:768] = proj[:, 640:768].astype(BF)
        bch_ref[...] = proj[:, 768:].astype(BF)

    row = lambda w: pl.BlockSpec((tm, w), lambda i: (i, 0))
    return _call(
        body, exch,
        name=name,
        grid=(lp // tm,),
        in_specs=[row(D_MODEL), _full((1, D_MODEL)), _full((IN_W, D_MODEL)), row(128), row(128), row(128)],
        out_specs=[row(D_MODEL), row(768), row(3 * CONV_W)],
        out_shape=[
            jax.ShapeDtypeStruct((lp, D_MODEL), BF),
            jax.ShapeDtypeStruct((lp, 768), BF),
            jax.ShapeDtypeStruct((lp, 3 * CONV_W), BF),
        ],
        compiler_params=_params(),
    )(h, g, w_in_t, *rope)


def _fold_masks(i):
    r = lax.broadcasted_iota(jnp.int32, (2 * BLOCK, BLOCK), 0) & (BLOCK - 1)
    c = lax.broadcasted_iota(jnp.int32, (2 * BLOCK, BLOCK), 1)
    tri = c > r
    ok = jnp.where(tri, (i - 1) * BLOCK + c, i * BLOCK + c) >= LEAD_PAD
    return tri, ok


def _kv_operand(x, kvh):
    lane = lax.broadcasted_iota(jnp.int32, x.shape, 1)
    zero = jnp.zeros_like(x)
    if kvh == 0:
        lo = jnp.where(lane < HEAD_DIM, x, zero)
        hi = pltpu.roll(lo, HEAD_DIM, 1)
    else:
        hi = jnp.where(lane >= HEAD_DIM, x, zero)
        lo = pltpu.roll(hi, HEAD_DIM, 1)
    return jnp.concatenate([lo, hi], axis=0)


def _split4(t, tri):
    zero = jnp.zeros_like(t[0])
    return jnp.concatenate(
        [jnp.where(tri, t[0], zero), jnp.where(tri, zero, t[0]), jnp.where(tri, t[1], zero), jnp.where(tri, zero, t[1])], axis=1)


def _sink_cols(sink_ref, kvh):
    first = lax.broadcasted_iota(jnp.int32, (2 * BLOCK, 1), 0) < BLOCK
    return [jnp.where(first, sink_ref[0, 4 * kvh + half], sink_ref[0, 4 * kvh + 2 + half]) for half in range(2)]


def _folded_exp(q2, k4, tri, ok, sks):
    s = _dot_nt(q2, k4)
    es, ss = [], []
    for half in range(2):
        s_h = s[:, 2 * half * BLOCK:2 * (half + 1) * BLOCK]
        sf = jnp.where(ok, jnp.where(tri, s_h[:, :BLOCK], s_h[:, BLOCK:]), NEG)
        m = jnp.maximum(jnp.max(sf, axis=-1, keepdims=True), sks[half])
        es.append(jnp.exp(sf - m))
        ss.append(jnp.exp(sks[half] - m))
    sums = _dot(jnp.concatenate(es, axis=0).astype(BF), jnp.ones((BLOCK, BLOCK), BF))
    invs = [1.0 / (sums[2 * half * BLOCK:2 * (half + 1) * BLOCK] + ss[half]) for half in range(2)]
    return es, ss, invs


def _attn_fwd(qkv, sink, name, exch=None):
    lp = qkv.shape[0]
    nb = lp // BLOCK

    def body(sink_ref, q_ref, kvc_ref, kvp_ref, o_ref, p_ref, ps_ref):
        i = pl.program_id(0)
        tri, ok = _fold_masks(i)
        kvc, kvp = kvc_ref[...], kvp_ref[...]
        kk = jnp.concatenate([kvp[:, :128], kvc[:, :128]], axis=0)
        vv = jnp.concatenate([kvp[:, 128:], kvc[:, 128:]], axis=0)
        lane = lax.broadcasted_iota(jnp.int32, (BLOCK, 128), 1)
        p_sink = jnp.zeros((BLOCK, 128), F32)
        for kvh in range(2):
            q2 = jnp.concatenate([q_ref[:, 256 * kvh:256 * kvh + 128], q_ref[:, 256 * kvh + 128:256 * kvh + 256]], axis=0)
            es, ss, invs = _folded_exp(q2, _kv_operand(kk, kvh), tri, ok, _sink_cols(sink_ref, kvh))
            pb = [(es[half] * invs[half]).astype(BF) for half in range(2)]
            out = _dot(_split4(pb, tri), _kv_operand(vv, kvh))
            for pair in range(2):
                rows = slice(pair * BLOCK, (pair + 1) * BLOCK)
                o_ref[:, 256 * kvh + 128 * pair:256 * kvh + 128 * (pair + 1)] = out[rows].astype(BF)
                for half in range(2):
                    head = 4 * kvh + 2 * pair + half
                    p_ref[:, 128 * head:128 * (head + 1)] = pb[half][rows]
                    p_sink = jnp.where(lane == head, (ss[half] * invs[half][:, 0:1])[rows], p_sink)
        ps_ref[...] = p_sink

    return _call(
        body, exch,
        name=name,
        grid=(nb,),
        in_specs=[
            pl.BlockSpec(memory_space=pltpu.SMEM),
            pl.BlockSpec((BLOCK, ATTN_W), lambda i: (i, 0)),
            pl.BlockSpec((BLOCK, 256), lambda i: (i, 2)),
            pl.BlockSpec((BLOCK, 256), lambda i: (jnp.maximum(i - 1, 0), 2)),
        ],
        out_specs=[pl.BlockSpec((BLOCK, ATTN_W), lambda i: (i, 0)), pl.BlockSpec((BLOCK, N_Q_HEADS * BLOCK), lambda i: (i, 0)),
                   pl.BlockSpec((BLOCK, 128), lambda i: (i, 0))],
        out_shape=[jax.ShapeDtypeStruct((lp, ATTN_W), BF), jax.ShapeDtypeStruct((lp, N_Q_HEADS * BLOCK), BF),
                   jax.ShapeDtypeStruct((lp, 128), F32)],
        compiler_params=_params(),
    )(sink, qkv, qkv, qkv)


def _mix_out_fwd(bch, y_attn, h, conv_w, g_a, g_c, w_out, g_post, tm, name, exch=None):
    lp = h.shape[0]

    def body(bch_ref, ya_ref, h_ref, cw_ref, ga_ref, gc_ref, w_ref, gp_ref, yc_ref, y_ref, z_ref, h2_ref, ext):
        i = pl.program_id(0)

        @pl.when(i == 0)
        def _():
            ext[0:8, :] = jnp.zeros((8, CONV_W), F32)

        b = bch_ref[:, 0:CONV_W].astype(F32)
        u = bch_ref[:, CONV_W:2 * CONV_W].astype(F32) * bch_ref[:, 2 * CONV_W:3 * CONV_W].astype(F32)
        ext[8:8 + tm, :] = u
        yc = cw_ref[0:1, :] * ext[6:6 + tm, :] + cw_ref[1:2, :] * ext[7:7 + tm, :] + cw_ref[2:3, :] * u
        ext[0:8, :] = u[tm - 8:tm, :]
        yc_ref[...] = yc.astype(BF)
        ya = _rms_fwd(ya_ref[...].astype(F32), ga_ref[...]).astype(BF)
        yb = _rms_fwd(b * yc, gc_ref[...]).astype(BF)
        y_ref[:, 0:ATTN_W] = ya
        y_ref[:, ATTN_W:] = yb
        z = _dot(ya, w_ref[0:ATTN_W, :]) + _dot(yb, w_ref[ATTN_W:, :])
        z_ref[...] = z
        h2_ref[...] = h_ref[...] + _rms_fwd(z, gp_ref[...])

    row = lambda w: pl.BlockSpec((tm, w), lambda i: (i, 0))
    return _call(
        body, exch,
        name=name,
        grid=(lp // tm,),
        in_specs=[
            row(3 * CONV_W), row(ATTN_W), row(D_MODEL), _full((8, CONV_W)), _full((1, ATTN_W)), _full((1, CONV_W)),
            _full((D_MODEL, D_MODEL)), _full((1, D_MODEL)),
        ],
        out_specs=[row(CONV_W), row(D_MODEL), row(D_MODEL), row(D_MODEL)],
        out_shape=[
            jax.ShapeDtypeStruct((lp, CONV_W), BF),
            jax.ShapeDtypeStruct((lp, D_MODEL), BF),
            jax.ShapeDtypeStruct((lp, D_MODEL), F32),
            jax.ShapeDtypeStruct((lp, D_MODEL), F32),
        ],
        scratch_shapes=[pltpu.VMEM((tm + 8, CONV_W), F32)],
        compiler_params=_params(),
    )(bch, y_attn, h, conv_w, g_a, g_c, w_out, g_post)


def _mlp_fwd(h2, g_pre, w_up_t, w_down, g_post, tm, name, exch=None, target=None):
    lp = h2.shape[0]
    sub = math.gcd(tm, BLOCK)
    n_sub, lead = tm // sub, BLOCK // sub
    n_t = n_sub if target is not None else 0

    def body(*refs):
        h_ref, gp_ref, wu_ref, wd_ref, gq_ref = refs[:5]
        t_refs = refs[5:5 + n_t]
        a_ref, up_ref, f_ref, last_ref = refs[5 + n_t:9 + n_t]
        h = h_ref[...]
        a = _rms_fwd(h, gp_ref[...]).astype(BF)
        a_ref[...] = a
        up = _dot_nt(a, wu_ref[...])
        up_ref[...] = up.astype(BF)
        act = jnp.square(jnp.maximum(up, 0.0)).astype(BF)
        f = _dot(act, wd_ref[...])
        f_ref[...] = f
        h3 = h + _rms_fwd(f, gq_ref[...])
        if target is None:
            last_ref[...] = h3
            return
        ls_ref = refs[9 + n_t]
        i = pl.program_id(0)

        @pl.when(i == 0)
        def _():
            ls_ref[...] = jnp.zeros((8, 128), F32)

        sq = jnp.zeros((1, 1), F32)
        for j in range(n_sub):
            on_tokens = i * n_sub + j >= lead
            d = jnp.where(on_tokens, h3[j * sub:(j + 1) * sub] - t_refs[j][...], 0.0)
            last_ref[j * sub:(j + 1) * sub, :] = d * (1.0 / D_MODEL)
            sq = sq + jnp.sum(d * d)
        ls_ref[...] += sq

    row = lambda w: pl.BlockSpec((tm, w), lambda i: (i, 0))
    piece = lambda j: pl.BlockSpec((sub, D_MODEL), lambda i: (jnp.maximum(i * n_sub + j - lead, 0), 0))
    out_specs = [row(D_MODEL), row(D_FF), row(D_MODEL), row(D_MODEL)]
    out_shape = [
        jax.ShapeDtypeStruct((lp, D_MODEL), BF),
        jax.ShapeDtypeStruct((lp, D_FF), BF),
        jax.ShapeDtypeStruct((lp, D_MODEL), F32),
        jax.ShapeDtypeStruct((lp, D_MODEL), F32),
    ]
    if target is not None:
        out_specs.append(_full_out((8, 128)))
        out_shape.append(jax.ShapeDtypeStruct((8, 128), F32))
    return _call(
        body, exch,
        name=name,
        grid=(lp // tm,),
        in_specs=[row(D_MODEL), _full((1, D_MODEL)), _full((D_FF, D_MODEL)), _full((D_FF, D_MODEL)), _full((1, D_MODEL))]
        + [piece(j) for j in range(n_t)],
        out_specs=out_specs,
        out_shape=out_shape,
        compiler_params=_params(),
    )(h2, g_pre, w_up_t, w_down, g_post, *([target] * n_t))


def _mlp_bwd_dx(dh3, f, up, h2, w_down, w_up_t, g_post, g_pre, tm, name, exch=None):
    lp = h2.shape[0]

    def body(dh3_ref, f_ref, up_ref, h2_ref, wd_ref, wu_ref, gq_ref, gp_ref, df_ref, dup_ref, dh2_ref, dg_ref):
        i = pl.program_id(0)

        @pl.when(i == 0)
        def _():
            dg_ref[...] = jnp.zeros((8, D_MODEL), F32)

        dh3 = dh3_ref[...]
        df, dgq = _rms_bwd(f_ref[...], gq_ref[...], dh3)
        dg_ref[ROW_MLP_POST:ROW_MLP_POST + 1, :] += dgq
        df = df.astype(BF)
        df_ref[...] = df
        dact = _dot_nt(df, wd_ref[...])
        dup = (dact * (2.0 * jnp.maximum(up_ref[...].astype(F32), 0.0))).astype(BF)
        dup_ref[...] = dup
        da = _dot(dup, wu_ref[...])
        dh, dgp = _rms_bwd(h2_ref[...], gp_ref[...], da)
        dg_ref[ROW_MLP_PRE:ROW_MLP_PRE + 1, :] += dgp
        dh2_ref[...] = dh3 + dh

    row = lambda w: pl.BlockSpec((tm, w), lambda i: (i, 0))
    return _call(
        body, exch,
        name=name,
        grid=(lp // tm,),
        in_specs=[
            row(D_MODEL), row(D_MODEL), row(D_FF), row(D_MODEL), _full((D_FF, D_MODEL)), _full((D_FF, D_MODEL)),
            _full((1, D_MODEL)), _full((1, D_MODEL)),
        ],
        out_specs=[row(D_MODEL), row(D_FF), row(D_MODEL), _full_out((8, D_MODEL))],
        out_shape=[
            jax.ShapeDtypeStruct((lp, D_MODEL), BF),
            jax.ShapeDtypeStruct((lp, D_FF), BF),
            jax.ShapeDtypeStruct((lp, D_MODEL), F32),
            jax.ShapeDtypeStruct((8, D_MODEL), F32),
        ],
        compiler_params=_params(),
    )(dh3, f, up, h2, w_down, w_up_t, g_post, g_pre)


def _mlp_bwd_dw(up, df, dup, a2, tm, name):
    lp = up.shape[0]
    nt = lp // tm
    nj = D_FF // D_MODEL

    def body(up_ref, df_ref, dup_ref, a_ref, dwd_ref, dwu_ref, accd, accu):
        i = pl.program_id(1)

        @pl.when(i == 0)
        def _():
            accd[...] = jnp.zeros_like(accd)
            accu[...] = jnp.zeros_like(accu)

        act = jnp.square(jnp.maximum(up_ref[...].astype(F32), 0.0)).astype(BF)
        accd[...] += _dot_tn(act, df_ref[...])
        accu[...] += _dot_tn(dup_ref[...], a_ref[...])

        @pl.when(i == nt - 1)
        def _():
            dwd_ref[...] = accd[...].astype(BF)
            dwu_ref[...] = accu[...].astype(BF)

    return pl.pallas_call(
        body,
        name=name,
        grid=(nj, nt),
        in_specs=[
            pl.BlockSpec((tm, D_MODEL), lambda j, i: (i, j)),
            pl.BlockSpec((tm, D_MODEL), lambda j, i: (i, 0)),
            pl.BlockSpec((tm, D_MODEL), lambda j, i: (i, j)),
            pl.BlockSpec((tm, D_MODEL), lambda j, i: (i, 0)),
        ],
        out_specs=[pl.BlockSpec((D_MODEL, D_MODEL), lambda j, i: (j, 0)), pl.BlockSpec((D_MODEL, D_MODEL), lambda j, i: (j, 0))],
        out_shape=[jax.ShapeDtypeStruct((D_FF, D_MODEL), BF), jax.ShapeDtypeStruct((D_FF, D_MODEL), BF)],
        scratch_shapes=[pltpu.VMEM((D_MODEL, D_MODEL), F32), pltpu.VMEM((D_MODEL, D_MODEL), F32)],
        compiler_params=_params(("arbitrary", "arbitrary")),
    )(up, df, dup, a2)


def _mix_out_bwd(dh2, z, y_attn, yc, bch, y, w_out, g_post, g_a, g_c, conv_w, tm, name, exch=None):
    lp = dh2.shape[0]
    nt = lp // tm

    def body(dh2_ref, z_ref, ya_ref, yc_ref, bch_ref, y_ref, w_ref, gp_ref, ga_ref, gc_ref, cw_ref,
             dya_ref, dbch_ref, dg_ref, dwo_ref, ext, acco):
        i = pl.program_id(0)
        dcw_ref = dg_ref.at[ROW_CONV:ROW_CONV + 3, 0:CONV_W]

        @pl.when(i == 0)
        def _():
            ext[tm:tm + 8, :] = jnp.zeros((8, CONV_W), F32)
            dg_ref[...] = jnp.zeros((8, D_MODEL), F32)
            acco[...] = jnp.zeros_like(acco)

        dz, dgp = _rms_bwd(z_ref[...], gp_ref[...], dh2_ref[...])
        dg_ref[ROW_MIX_POST:ROW_MIX_POST + 1, :] += dgp
        dz = dz.astype(BF)
        acco[...] += _dot_tn(y_ref[...], dz)

        @pl.when(i == nt - 1)
        def _():
            dwo_ref[...] = acco[...].astype(BF)

        dya_n = _dot_nt(dz, w_ref[0:ATTN_W, :])
        dyb_n = _dot_nt(dz, w_ref[ATTN_W:, :])
        dya, dga = _rms_bwd(ya_ref[...].astype(F32), ga_ref[...], dya_n)
        dg_ref[ROW_GROUP_G:ROW_GROUP_G + 1, 0:ATTN_W] += dga
        dya_ref[...] = dya
        b = bch_ref[:, 0:CONV_W].astype(F32)
        c = bch_ref[:, CONV_W:2 * CONV_W].astype(F32)
        hc = bch_ref[:, 2 * CONV_W:3 * CONV_W].astype(F32)
        u = c * hc
        yc_v = yc_ref[...].astype(F32)
        dyconv, dgc = _rms_bwd(b * yc_v, gc_ref[...], dyb_n)
        dg_ref[ROW_GROUP_G:ROW_GROUP_G + 1, ATTN_W:] += dgc
        dbch_ref[:, 0:CONV_W] = (dyconv * yc_v).astype(BF)
        dyc = dyconv * b
        ext[0:tm, :] = dyc
        d1 = ext[1:1 + tm, :]
        d2 = ext[2:2 + tm, :]
        du = cw_ref[2:3, :] * dyc + cw_ref[1:2, :] * d1 + cw_ref[0:1, :] * d2
        ext[tm:tm + 8, :] = dyc[0:8, :]
        dbch_ref[:, CONV_W:2 * CONV_W] = (du * hc).astype(BF)
        dbch_ref[:, 2 * CONV_W:3 * CONV_W] = (du * c).astype(BF)
        dcw_ref[0:1, :] += jnp.sum(u * d2, axis=0, keepdims=True)
        dcw_ref[1:2, :] += jnp.sum(u * d1, axis=0, keepdims=True)
        dcw_ref[2:3, :] += jnp.sum(u * dyc, axis=0, keepdims=True)

    row = lambda w: pl.BlockSpec((tm, w), lambda i: (nt - 1 - i, 0))
    return _call(
        body, exch,
        name=name,
        grid=(nt,),
        in_specs=[
            row(D_MODEL), row(D_MODEL), row(ATTN_W), row(CONV_W), row(3 * CONV_W), row(D_MODEL), _full((D_MODEL, D_MODEL)),
            _full((1, D_MODEL)), _full((1, ATTN_W)), _full((1, CONV_W)), _full((8, CONV_W)),
        ],
        out_specs=[row(ATTN_W), row(3 * CONV_W), _full_out((8, D_MODEL)), _full_out((D_MODEL, D_MODEL))],
        out_shape=[
            jax.ShapeDtypeStruct((lp, ATTN_W), F32),
            jax.ShapeDtypeStruct((lp, 3 * CONV_W), BF),
            jax.ShapeDtypeStruct((8, D_MODEL), F32),
            jax.ShapeDtypeStruct((D_MODEL, D_MODEL), BF),
        ],
        scratch_shapes=[pltpu.VMEM((tm + 8, CONV_W), F32), pltpu.VMEM((D_MODEL, D_MODEL), F32)],
        compiler_params=_params(),
    )(dh2, z, y_attn, yc, bch, y, w_out, g_post, g_a, g_c, conv_w)


def _attn_bwd(qkv, o, do, probs, p_sink, rope, name, exch=None):
    lp = qkv.shape[0]
    nb = lp // BLOCK

    def body(q_ref, kvc_ref, kvp_ref, o_ref, do_ref, p_ref, ps_ref, cq_ref, s1q_ref, s2q_ref, ck_ref, s1k_ref, s2k_ref,
             dq_ref, dkv_ref, dsink_ref, carry):
        i = pl.program_id(0)

        @pl.when(i == 0)
        def _():
            carry[...] = jnp.zeros_like(carry)
            dsink_ref[...] = jnp.zeros((8, 128), F32)

        def finish(tot):
            dk = _rope_t(tot[:, :128], ck_ref[...], s1k_ref[...], s2k_ref[...])
            dkv_ref[:, 0:128] = dk.astype(BF)
            dkv_ref[:, 128:256] = tot[:, 128:].astype(BF)

        @pl.when(i < nb)
        def _():
            tri, _ = _fold_masks(i)
            kvc, kvp = kvc_ref[...], kvp_ref[...]
            kk = jnp.concatenate([kvp[:, :128], kvc[:, :128]], axis=0)
            vv = jnp.concatenate([kvp[:, 128:], kvc[:, 128:]], axis=0)
            lane = lax.broadcasted_iota(jnp.int32, (BLOCK, 128), 1)
            lane2 = lax.broadcasted_iota(jnp.int32, (2 * BLOCK, 128), 1)
            rope_q = (cq_ref[...], s1q_ref[...], s2q_ref[...])
            deltas = jnp.zeros((BLOCK, 128), F32)
            folded = []
            for kvh in range(2):
                c0 = 256 * kvh
                q2 = jnp.concatenate([q_ref[:, c0:c0 + 128], q_ref[:, c0 + 128:c0 + 256]], axis=0)
                do2 = jnp.concatenate([do_ref[:, c0:c0 + 128], do_ref[:, c0 + 128:c0 + 256]], axis=0)
                o2 = jnp.concatenate([o_ref[:, c0:c0 + 128], o_ref[:, c0 + 128:c0 + 256]], axis=0).astype(F32)
                k4, v4 = _kv_operand(kk, kvh), _kv_operand(vv, kvh)
                prod = do2 * o2
                dob = do2.astype(BF)
                dp = _dot_nt(dob, v4)
                ds, pb = [], []
                for half in range(2):
                    heads = [4 * kvh + 2 * pair + half for pair in range(2)]
                    p = jnp.concatenate([p_ref[:, 128 * h:128 * (h + 1)] for h in heads], axis=0)
                    sel = (lane2 < HEAD_DIM) if half == 0 else (lane2 >= HEAD_DIM)
                    delta = jnp.sum(jnp.where(sel, prod, 0.0), axis=-1, keepdims=True)
                    dp_h = dp[:, 2 * half * BLOCK:2 * (half + 1) * BLOCK]
                    ds.append((p.astype(F32) * (jnp.where(tri, dp_h[:, :BLOCK], dp_h[:, BLOCK:]) - delta)).astype(BF))
                    pb.append(p)
                    for pair in range(2):
                        deltas = jnp.where(lane == heads[pair], delta[pair * BLOCK:(pair + 1) * BLOCK], deltas)
                ds4, p4 = _split4(ds, tri), _split4(pb, tri)
                dq2 = _dot(ds4, k4) * SCALE
                dq_ref[:, c0:c0 + 128] = _rope_t(dq2[:BLOCK], *rope_q).astype(BF)
                dq_ref[:, c0 + 128:c0 + 256] = _rope_t(dq2[BLOCK:], *rope_q).astype(BF)
                rk, rv = _dot_tn(ds4, q2), _dot_tn(p4, dob)
                own = (lane < HEAD_DIM) if kvh == 0 else (lane >= HEAD_DIM)
                group = []
                for r in (rk, rv):
                    for blk in range(2):
                        t = jnp.where(lane < HEAD_DIM, r[blk * BLOCK:(blk + 1) * BLOCK], r[(2 + blk) * BLOCK:(3 + blk) * BLOCK])
                        group.append(jnp.where(own, t + pltpu.roll(t, HEAD_DIM, 1), 0.0))
                folded.append(group)
            dsink_ref[ROW_SINK:ROW_SINK + 1, :] -= jnp.sum(ps_ref[...] * deltas, axis=0, keepdims=True)
            dk_p, dk_c, dv_p, dv_c = [folded[0][t] + folded[1][t] for t in range(4)]
            finish(carry[...] + jnp.concatenate([dk_p, dv_p], axis=1))
            carry[...] = jnp.concatenate([dk_c, dv_c], axis=1)

        @pl.when(i == nb)
        def _():
            finish(carry[...])

    qi = lambda i: jnp.minimum(i, nb - 1)
    ki = lambda i: jnp.maximum(i - 1, 0)
    tab_q = pl.BlockSpec((BLOCK, 128), lambda i: (qi(i), 0))
    tab_k = pl.BlockSpec((BLOCK, 128), lambda i: (ki(i), 0))
    return _call(
        body, exch,
        name=name,
        grid=(nb + 1,),
        in_specs=[
            pl.BlockSpec((BLOCK, ATTN_W), lambda i: (qi(i), 0)),
            pl.BlockSpec((BLOCK, 256), lambda i: (qi(i), 2)),
            pl.BlockSpec((BLOCK, 256), lambda i: (jnp.maximum(qi(i) - 1, 0), 2)),
            pl.BlockSpec((BLOCK, ATTN_W), lambda i: (qi(i), 0)),
            pl.BlockSpec((BLOCK, ATTN_W), lambda i: (qi(i), 0)),
            pl.BlockSpec((BLOCK, N_Q_HEADS * BLOCK), lambda i: (qi(i), 0)),
            tab_q, tab_q, tab_q, tab_q, tab_k, tab_k, tab_k,
        ],
        out_specs=[
            pl.BlockSpec((BLOCK, ATTN_W), lambda i: (qi(i), 0)),
            pl.BlockSpec((BLOCK, 256), lambda i: (ki(i), 0)),
            pl.BlockSpec((8, 128), lambda i: (0, 0)),
        ],
        out_shape=[
            jax.ShapeDtypeStruct((lp, ATTN_W), BF),
            jax.ShapeDtypeStruct((lp, 256), BF),
            jax.ShapeDtypeStruct((8, 128), F32),
        ],
        scratch_shapes=[pltpu.VMEM((BLOCK, 256), F32)],
        compiler_params=_params(),
    )(qkv, qkv, qkv, o, do, probs, p_sink, *rope, *rope)


def _in_proj_bwd_dx(dq, dkv, dbch, w_in_t, h, dh2, g, tm, name, exch=None):
    lp = h.shape[0]

    def body(dq_ref, dkv_ref, dbch_ref, w_ref, h_ref, dh2_ref, g_ref, dh_ref, dg_ref):
        i = pl.program_id(0)

        @pl.when(i == 0)
        def _():
            dg_ref[...] = jnp.zeros((8, D_MODEL), F32)

        da = _dot(dq_ref[...], w_ref[0:512, :]) + _dot(dkv_ref[...], w_ref[512:768, :]) + _dot(dbch_ref[...], w_ref[768:, :])
        dh, dg = _rms_bwd(h_ref[...], g_ref[...], da)
        dg_ref[ROW_MIX_PRE:ROW_MIX_PRE + 1, :] += dg
        dh_ref[...] = dh2_ref[...] + dh

    row = lambda w: pl.BlockSpec((tm, w), lambda i: (i, 0))
    return _call(
        body, exch,
        name=name,
        grid=(lp // tm,),
        in_specs=[row(ATTN_W), row(256), row(3 * CONV_W), _full((IN_W, D_MODEL)), row(D_MODEL), row(D_MODEL), _full((1, D_MODEL))],
        out_specs=[row(D_MODEL), _full_out((8, D_MODEL))],
        out_shape=[jax.ShapeDtypeStruct((lp, D_MODEL), F32), jax.ShapeDtypeStruct((8, D_MODEL), F32)],
        compiler_params=_params(),
    )(dq, dkv, dbch, w_in_t, h, dh2, g)


def _mix_bwd_dw(dq, dkv, dbch, a, tm, name, exch=None):
    lp = a.shape[0]
    nt = lp // tm

    def body(dq_ref, dkv_ref, dbch_ref, a_ref, dwi_ref, acci):
        i = pl.program_id(0)

        @pl.when(i == 0)
        def _():
            acci[...] = jnp.zeros_like(acci)

        a_v = a_ref[...]
        acci[0:512, :] += _dot_tn(dq_ref[...], a_v)
        acci[512:768, :] += _dot_tn(dkv_ref[...], a_v)
        acci[768:, :] += _dot_tn(dbch_ref[...], a_v)

        @pl.when(i == nt - 1)
        def _():
            dwi_ref[...] = acci[...].astype(BF)

    row = lambda w: pl.BlockSpec((tm, w), lambda i: (i, 0))
    return _call(
        body, exch,
        name=name,
        grid=(nt,),
        in_specs=[row(ATTN_W), row(256), row(3 * CONV_W), row(D_MODEL)],
        out_specs=[_full_out((IN_W, D_MODEL))],
        out_shape=[jax.ShapeDtypeStruct((IN_W, D_MODEL), BF)],
        scratch_shapes=[pltpu.VMEM((IN_W, D_MODEL), F32)],
        compiler_params=_params(),
    )(dq, dkv, dbch, a)


def _mesh_place():
    x, y, c = lax.axis_index("x"), lax.axis_index("y"), lax.axis_index("c")
    return x, y, c, 4 * x + 2 * y + c


def _peer(x, y, c, k):
    px = 1 - x if k & 4 else x
    py = 1 - y if k & 2 else y
    pc = 1 - c if k & 1 else c
    return (px, py, pc), 4 * px + 2 * py + pc


SIBLING = 1
SAME_CORE = (2, 4, 6)
OTHER_CORE = (3, 5, 7)


class _Exchange:
    def __init__(self, pieces):
        self.srcs = [s for s, _ in pieces]
        self.to_all = [g for _, g in pieces]
        self.n = len(pieces)
        self.land_shapes = [
            jax.ShapeDtypeStruct((N_DEV,) + (s.shape if g else s.shape[1:]), s.dtype) for s, g in pieces]
        self.sem_shapes = [pltpu.SemaphoreType.DMA((self.n, N_DEV - 1)), pltpu.SemaphoreType.DMA((self.n, N_DEV - 1)),
                           pltpu.SemaphoreType.DMA((self.n,))]
        self.forwards = any(self.to_all)

    def _ops(self, srcs, lands, sems):
        send_sems, recv_sems, local_sems = sems
        x, y, c, me = _mesh_place()

        def remote(p, k, src, slot, to):
            return pltpu.make_async_remote_copy(
                src_ref=src, dst_ref=lands[p].at[slot], send_sem=send_sems.at[p, k - 1], recv_sem=recv_sems.at[p, k - 1],
                device_id=to, device_id_type=MESH)

        def own(p):
            return pltpu.make_async_copy(srcs[p] if self.to_all[p] else srcs[p].at[me], lands[p].at[me], local_sems.at[p])

        def direct(p, k):
            peer, pidx = _peer(x, y, c, k)
            return remote(p, k, srcs[p] if self.to_all[p] else srcs[p].at[pidx], me, peer)

        def forward(p, k):
            sibling, _ = _peer(x, y, c, SIBLING)
            _, origin = _peer(x, y, c, k ^ SIBLING)
            return remote(p, k, lands[p].at[origin], origin, sibling)

        def arrival(p, k):
            peer, pidx = _peer(x, y, c, k)
            return remote(p, k, lands[p].at[pidx], pidx, peer)

        return own, direct, forward, arrival

    def start(self, srcs, lands, sems):
        own, direct, _, _ = self._ops(srcs, lands, sems)
        for p in range(self.n):
            own(p).start()
            for k in ((SIBLING,) + SAME_CORE) if self.to_all[p] else range(1, N_DEV):
                direct(p, k).start()

    def forward(self, srcs, lands, sems):
        _, _, forward, arrival = self._ops(srcs, lands, sems)
        for p in range(self.n):
            if self.to_all[p]:
                for k in SAME_CORE:
                    arrival(p, k).wait_recv()
                    forward(p, k ^ SIBLING).start()

    def finish(self, srcs, lands, sems):
        own, direct, forward, arrival = self._ops(srcs, lands, sems)
        for p in range(self.n):
            for k in ((SIBLING,) + OTHER_CORE) if self.to_all[p] else range(1, N_DEV):
                arrival(p, k).wait_recv()
        for p in range(self.n):
            for k in range(1, N_DEV):
                (forward(p, k) if self.to_all[p] and k in OTHER_CORE else direct(p, k)).wait_send()
            own(p).wait()


def _call(body, exch, *, name, grid, in_specs, out_specs, out_shape, scratch_shapes=(), compiler_params, after=None):
    if exch is None:
        return pl.pallas_call(body, name=name, grid=grid, in_specs=in_specs, out_specs=out_specs, out_shape=out_shape,
                              scratch_shapes=scratch_shapes, compiler_params=compiler_params)
    n_in, n_out, n_scr, n_x = len(in_specs), len(out_shape), len(scratch_shapes), exch.n
    steps = math.prod(grid)

    def carrying(*refs):
        a, b, c, d, e = n_in, n_in + n_x, n_in + n_x + n_out, n_in + 2 * n_x + n_out, n_in + 2 * n_x + n_out + n_scr
        ins, srcs, outs, lands, scr, sems = refs[:a], refs[a:b], refs[b:c], refs[c:d], refs[d:e], refs[e:]
        step = functools.reduce(lambda acc, t: acc * grid[t] + pl.program_id(t), range(len(grid)), 0)

        @pl.when(step == 0)
        def _():
            exch.start(srcs, lands, sems)

        body(*ins, *outs, *scr)

        if exch.forwards:
            @pl.when(step == max(0, steps - 1 - (steps + 7) // 8))
            def _():
                exch.forward(srcs, lands, sems)

        @pl.when(step == steps - 1)
        def _():
            exch.finish(srcs, lands, sems)
            if after is not None:
                after(lands, *ins, *outs, *scr)

    hbm = pl.BlockSpec(memory_space=pl.ANY)
    call = pl.pallas_call(
        carrying, name=name, grid=grid, in_specs=list(in_specs) + [hbm] * n_x, out_specs=list(out_specs) + [hbm] * n_x,
        out_shape=list(out_shape) + exch.land_shapes, scratch_shapes=list(scratch_shapes) + exch.sem_shapes,
        compiler_params=compiler_params)

    def run(*args):
        res = call(*args, *exch.srcs)
        return list(res[:n_out]), list(res[n_out:])

    return run


def _sum_small(part):
    def body(part_ref, out_ref, land, send_sems, recv_sems):
        x, y, c, me = _mesh_place()
        land[me] = part_ref[...]
        sent = []
        for k in range(1, N_DEV):
            peer, _ = _peer(x, y, c, k)
            cp = pltpu.make_async_remote_copy(
                src_ref=part_ref, dst_ref=land.at[me], send_sem=send_sems.at[k - 1], recv_sem=recv_sems.at[k - 1],
                device_id=peer, device_id_type=MESH)
            cp.start()
            sent.append(cp)
        for k in range(1, N_DEV):
            peer, pidx = _peer(x, y, c, k)
            pltpu.make_async_remote_copy(
                src_ref=part_ref, dst_ref=land.at[pidx], send_sem=send_sems.at[k - 1], recv_sem=recv_sems.at[k - 1],
                device_id=peer, device_id_type=MESH).wait_recv()
        for cp in sent:
            cp.wait_send()
        acc = land[0]
        for d in range(1, N_DEV):
            acc = acc + land[d]
        out_ref[...] = acc

    vmem = pl.BlockSpec(memory_space=pltpu.VMEM)
    return pl.pallas_call(
        body,
        name="sum_small",
        in_specs=[vmem],
        out_specs=vmem,
        out_shape=jax.ShapeDtypeStruct(part.shape, F32),
        scratch_shapes=[pltpu.VMEM((N_DEV,) + part.shape, F32), pltpu.SemaphoreType.DMA((N_DEV - 1,)),
                        pltpu.SemaphoreType.DMA((N_DEV - 1,))],
    )(part)


def _adamw(w, g, m, v):
    m = ADAM_B1 * m + (1.0 - ADAM_B1) * g
    v = ADAM_B2 * v + (1.0 - ADAM_B2) * jnp.square(g)
    m_hat = m / (1.0 - ADAM_B1 ** ADAM_STEP)
    v_hat = v / (1.0 - ADAM_B2 ** ADAM_STEP)
    delta = -ADAM_LR * (m_hat / (jnp.sqrt(v_hat) + ADAM_EPS) + ADAM_WD * w)
    return delta, m, v


def _landed_specs(tr, wd):
    return [pl.BlockSpec((N_DEV, tr, wd), lambda l, i, ll=ll: (0, jnp.where(l == ll, i, 0), 0)) for ll in range(DEPTH)]


def _device_sum(r_ref):
    acc = r_ref[0].astype(F32)
    for d in range(1, N_DEV):
        acc = acc + r_ref[d].astype(F32)
    return acc


def _sum_adamw(recv, w, m, v, tr, name, transposed=False):
    _, r, wd = recv[0].shape

    def body(*refs):
        w_ref, m_ref, v_ref, g_ref, d_ref, mo_ref, vo_ref = refs[DEPTH:]
        for ll in range(DEPTH):
            @pl.when(pl.program_id(0) == ll)
            def _(ll=ll):
                g = _device_sum(refs[ll])
                g = g.T if transposed else g
                g_ref[0] = g
                d_ref[0], mo_ref[0], vo_ref[0] = _adamw(w_ref[0], g, m_ref[0], v_ref[0])

    if transposed:
        blk = pl.BlockSpec((1, wd, tr), lambda l, i: (l, 0, i))
        shape = jax.ShapeDtypeStruct((DEPTH, wd, r), F32)
    else:
        blk = pl.BlockSpec((1, tr, wd), lambda l, i: (l, i, 0))
        shape = jax.ShapeDtypeStruct((DEPTH, r, wd), F32)
    return pl.pallas_call(
        body,
        name=name,
        grid=(DEPTH, r // tr),
        in_specs=_landed_specs(tr, wd) + [blk, blk, blk],
        out_specs=[blk] * 4,
        out_shape=[shape] * 4,
        compiler_params=_params(("arbitrary", "arbitrary")),
    )(*recv, w, m, v)


def _adamw_small(ws, gs, ms, vs):
    n = len(ws)

    def body(*refs):
        w_r, g_r, m_r, v_r = refs[:n], refs[n:2 * n], refs[2 * n:3 * n], refs[3 * n:4 * n]
        d_o, m_o, v_o = refs[4 * n:5 * n], refs[5 * n:6 * n], refs[6 * n:7 * n]
        for t in range(n):
            d_o[t][...], m_o[t][...], v_o[t][...] = _adamw(w_r[t][...], g_r[t][...], m_r[t][...], v_r[t][...])

    vmem = pl.BlockSpec(memory_space=pltpu.VMEM)
    shapes = [jax.ShapeDtypeStruct(w.shape, F32) for w in ws]
    outs = pl.pallas_call(
        body,
        name="adamw_small",
        in_specs=[vmem] * (4 * n),
        out_specs=[vmem] * (3 * n),
        out_shape=shapes * 3,
    )(*ws, *gs, *ms, *vs)
    return outs[:n], outs[n:2 * n], outs[2 * n:]


def kernel(x, meta_tokens, mix_pre_g, w_in, conv_w, sinks, attn_out_g, conv_out_g, w_out, mix_post_g, mlp_pre_g, w_up, w_down, mlp_post_g, loss_target, m_meta_tokens, m_mix_pre_g, m_w_in, m_conv_w, m_sinks, m_attn_out_g, m_conv_out_g, m_w_out, m_mix_post_g, m_mlp_pre_g, m_w_up, m_w_down, m_mlp_post_g, v_meta_tokens, v_mix_pre_g, v_w_in, v_conv_w, v_sinks, v_attn_out_g, v_conv_out_g, v_w_out, v_mix_post_g, v_mlp_pre_g, v_w_up, v_w_down, v_mlp_post_g):
    seq = x.shape[1]
    lp = BLOCK + seq
    tm = _row_tile(lp)
    tm_mlp = _row_tile(lp, (320, 256, 128))
    tm_dw_mlp = _row_tile(lp, (1664, 1040, 640, 384, 256, 128))
    tm_dw_mix = _row_tile(lp, (1664, 832, 640, 384, 256, 128))
    me = 4 * lax.axis_index("x") + 2 * lax.axis_index("y") + lax.axis_index("c")
    cshard = CONV_W // N_DEV
    mshard = D_MODEL // N_DEV

    gather_with = {
        ("in_proj_fwd", 0): [("in", 1)], ("attn_fwd", 0): [("out", 0), ("up", 0)], ("mix_out_fwd", 0): [("down", 0)],
        ("mlp_fwd", 0): [("out", 1), ("up", 1), ("down", 1)],
    }
    scatter_with = {
        ("attn_bwd", 1): [("down", 1)], ("mix_bwd_dw", 1): [("out", 1)], ("mlp_bwd_dx", 0): [("up", 1), ("in", 1)],
        ("mix_out_bwd", 0): [("up", 0)], ("attn_bwd", 0): [("down", 0)], ("mix_bwd_dw", 0): [("out", 0)],
        ("in_proj_bwd_dx", 0): [("in", 0)],
    }
    shard = {"in": jnp.swapaxes(w_in, 1, 2).astype(BF), "out": w_out.astype(BF),
             "up": jnp.swapaxes(w_up, 1, 2).astype(BF), "down": w_down.astype(BF)}
    weight = {}
    grad = {}
    landed = {}

    def run(fn, kind, l, *args):
        key, name = (kind, l), f"{kind}_{l}"
        if key in gather_with:
            blocks = gather_with[key]
            outs, lands = fn(*args, name, _Exchange([(shard[n][k], True) for n, k in blocks]))
            for b, land in zip(blocks, lands):
                weight[b] = land.reshape(-1, D_MODEL)
            return outs
        if key in scatter_with:
            blocks = scatter_with[key]
            outs, lands = fn(*args, name, _Exchange([(grad[b].reshape(N_DEV, -1, D_MODEL), False) for b in blocks]))
            landed.update(zip(blocks, lands))
            return outs
        return fn(*args, name)

    small = jnp.zeros((24, 128), F32)
    small = small.at[0:N_META, :].set(meta_tokens)
    small = small.at[N_META:N_META + 6, 0:cshard].set(conv_w.reshape(6, cshard))
    first = _Exchange([(shard["in"][0], True), (small, True)])
    h, rope, (first_in, g_small) = _build_h(x[0], _rope_table(lp), tm, first, 1, "build_h")
    weight[("in", 0)] = first_in.reshape(-1, D_MODEL)
    cw = g_small[:, N_META:N_META + 6, 0:cshard].reshape(N_DEV, DEPTH, 3, cshard)
    cw = jnp.transpose(cw, (1, 2, 0, 3)).reshape(DEPTH, 3, CONV_W)
    conv_full = jnp.concatenate([cw, jnp.zeros((DEPTH, 5, CONV_W), F32)], axis=1)

    row1 = lambda a, l: a[l].reshape(1, -1)

    saved = []
    for l in range(DEPTH):
        a, qkv, bch = run(_in_proj_fwd, "in_proj_fwd", l, h, row1(mix_pre_g, l), weight[("in", l)], rope, tm)
        y_attn, probs, p_sink = run(_attn_fwd, "attn_fwd", l, qkv, row1(sinks, l))
        yc, y, z, h2 = run(_mix_out_fwd, "mix_out_fwd", l, bch, y_attn, h, conv_full[l], row1(attn_out_g, l),
                       row1(conv_out_g, l), weight[("out", l)], row1(mix_post_g, l), tm)
        mlp = _mlp_fwd if l < DEPTH - 1 else functools.partial(_mlp_fwd, target=loss_target[0])
        a2, up, f, *rest = run(mlp, "mlp_fwd", l, h2, row1(mlp_pre_g, l), weight[("up", l)], weight[("down", l)],
                               row1(mlp_post_g, l), tm_mlp)
        saved.append((h, a, qkv, bch, y_attn, probs, p_sink, yc, y, z, h2, a2, up, f))
        h = rest[0]
    dh, loss_part = rest[0], rest[1][0, 0] * (0.5 / D_MODEL)

    gsmall = [None] * DEPTH
    for l in reversed(range(DEPTH)):
        h0, a, qkv, bch, y_attn, probs, p_sink, yc, y, z, h2, a2, up, f = saved[l]
        df, dup, dh2, dg_mlp = run(_mlp_bwd_dx, "mlp_bwd_dx", l, dh, f, up, h2, weight[("down", l)], weight[("up", l)],
                                   row1(mlp_post_g, l), row1(mlp_pre_g, l), tm_mlp)
        grad[("down", l)], grad[("up", l)] = _mlp_bwd_dw(up, df, dup, a2, tm_dw_mlp, f"mlp_bwd_dw_{l}")
        dya, dbch, dg_mix, grad[("out", l)] = run(
            _mix_out_bwd, "mix_out_bwd", l, dh2, z, y_attn, yc, bch, y, weight[("out", l)], row1(mix_post_g, l),
            row1(attn_out_g, l), row1(conv_out_g, l), conv_full[l], tm)
        dq, dkv, dsink = run(_attn_bwd, "attn_bwd", l, qkv, y_attn, dya, probs, p_sink, rope)
        grad[("in", l)], = run(_mix_bwd_dw, "mix_bwd_dw", l, dq, dkv, dbch, a, tm_dw_mix)
        dh, dg_in = run(_in_proj_bwd_dx, "in_proj_bwd_dx", l, dq, dkv, dbch, weight[("in", l)], h0, dh2,
                        row1(mix_pre_g, l), tm)
        tile_a = dg_mlp + dg_in + jnp.pad(dsink, ((0, 0), (0, D_MODEL - 128)))
        gsmall[l] = (tile_a, dg_mix)
    grad_x = dh[BLOCK:][None]

    loss_tile = jnp.zeros((8, D_MODEL), F32).at[ROW_LOSS, 0].set(loss_part)
    tot = _sum_small(jnp.concatenate(
        [gsmall[0][0] + loss_tile, gsmall[0][1], gsmall[1][0], gsmall[1][1], dh[LEAD_PAD:BLOCK]], axis=0))
    loss = tot[ROW_LOSS, 0]
    ta = [tot[16 * l:16 * l + 8] for l in range(DEPTH)]
    tb = [tot[16 * l + 8:16 * l + 16] for l in range(DEPTH)]
    pick = lambda tiles, r0, r1, c0, c1: jnp.stack([t[r0:r1, c0:c1] for t in tiles])
    g_mlp_post = pick(ta, ROW_MLP_POST, ROW_MLP_POST + 1, 0, D_MODEL).reshape(DEPTH, D_MODEL)
    g_mlp_pre = pick(ta, ROW_MLP_PRE, ROW_MLP_PRE + 1, 0, D_MODEL).reshape(DEPTH, D_MODEL)
    g_mix_pre = pick(ta, ROW_MIX_PRE, ROW_MIX_PRE + 1, 0, D_MODEL).reshape(DEPTH, D_MODEL)
    g_sinks = pick(ta, ROW_SINK, ROW_SINK + 1, 0, N_Q_HEADS).reshape(DEPTH, N_Q_HEADS)
    g_mix_post = pick(tb, ROW_MIX_POST, ROW_MIX_POST + 1, 0, D_MODEL).reshape(DEPTH, D_MODEL)
    g_attn_out = pick(tb, ROW_GROUP_G, ROW_GROUP_G + 1, 0, ATTN_W).reshape(DEPTH, ATTN_W)
    g_conv_out = pick(tb, ROW_GROUP_G, ROW_GROUP_G + 1, ATTN_W, D_MODEL).reshape(DEPTH, CONV_W)
    g_conv_full = pick(tb, ROW_CONV, ROW_CONV + 3, 0, CONV_W)
    g_conv = lax.dynamic_slice_in_dim(g_conv_full, me * cshard, cshard, axis=2)
    g_meta = lax.dynamic_slice_in_dim(tot[16 * DEPTH:16 * DEPTH + N_META], me * mshard, mshard, axis=1)

    r_in, r_out, r_up, r_down = [[landed[(n, l)] for l in range(DEPTH)] for n in ("in", "out", "up", "down")]
    t12 = lambda a: jnp.swapaxes(a, 1, 2)
    g_w_in, d_w_in, nm_w_in, nv_w_in = map(t12, _sum_adamw(r_in, t12(w_in), t12(m_w_in), t12(v_w_in), 96, "adamw_w_in"))
    g_w_up, d_w_up, nm_w_up, nv_w_up = _sum_adamw(r_up, w_up, m_w_up, v_w_up, 128, "adamw_w_up", transposed=True)
    g_w_out, d_w_out, nm_w_out, nv_w_out = _sum_adamw(r_out, w_out, m_w_out, v_w_out, 128, "adamw_w_out")
    g_w_down, d_w_down, nm_w_down, nv_w_down = _sum_adamw(r_down, w_down, m_w_down, v_w_down, 128, "adamw_w_down")

    ws = [meta_tokens, mix_pre_g, conv_w.reshape(6, cshard), sinks, attn_out_g, conv_out_g, mix_post_g, mlp_pre_g, mlp_post_g]
    gs = [g_meta, g_mix_pre, g_conv.reshape(6, cshard), g_sinks, g_attn_out, g_conv_out, g_mix_post, g_mlp_pre, g_mlp_post]
    ms = [m_meta_tokens, m_mix_pre_g, m_conv_w.reshape(6, cshard), m_sinks, m_attn_out_g, m_conv_out_g, m_mix_post_g,
          m_mlp_pre_g, m_mlp_post_g]
    vs = [v_meta_tokens, v_mix_pre_g, v_conv_w.reshape(6, cshard), v_sinks, v_attn_out_g, v_conv_out_g, v_mix_post_g,
          v_mlp_pre_g, v_mlp_post_g]
    ds, nms, nvs = _adamw_small(ws, gs, ms, vs)

    def order(meta, mix_pre, cv, sk, a_out, c_out, mix_post, mlp_pre, mlp_post, win, wout, wup, wdown):
        return [meta, mix_pre, win, cv.reshape(DEPTH, 3, cshard), sk, a_out, c_out, wout, mix_post, mlp_pre, wup, wdown, mlp_post]

    grads = order(*gs, g_w_in, g_w_out, g_w_up, g_w_down)
    deltas = order(*ds, d_w_in, d_w_out, d_w_up, d_w_down)
    new_m = order(*nms, nm_w_in, nm_w_out, nm_w_up, nm_w_down)
    new_v = order(*nvs, nv_w_in, nv_w_out, nv_w_up, nv_w_down)
    return (loss, grad_x, *grads, *deltas, *new_m, *new_v)
```

```python
import functools
import math

import jax
import jax.numpy as jnp
from jax import lax
from jax.experimental import pallas as pl
from jax.experimental.pallas import tpu as pltpu

F32 = jnp.float32
BF = jnp.bfloat16

D_MODEL = 1024
ATTN_W = 512
CONV_W = 512
KV_W = 128
HEAD_DIM = 64
N_Q_HEADS = 8
ROT_DIM = 16
D_FF = 4096
IN_W = 2304
N_META = 16
BLOCK = 128
LEAD_PAD = BLOCK - N_META
ROPE_THETA = 500000.0
EPS = 1e-6
N_DEV = 8
DEPTH = 2
NEG = -1e30
SCALE = HEAD_DIM ** -0.5

ADAM_LR = 0.001
ADAM_B1 = 0.9
ADAM_B2 = 0.999
ADAM_EPS = 1e-08
ADAM_WD = 0.01
ADAM_STEP = 10

ROW_MLP_POST, ROW_MLP_PRE, ROW_MIX_PRE, ROW_SINK, ROW_LOSS = 0, 1, 2, 3, 4
ROW_MIX_POST, ROW_GROUP_G, ROW_CONV = 0, 1, 2

VMEM_LIMIT = 56 * 1024 * 1024
MESH = pl.DeviceIdType.MESH


def _dot(a, b):
    return jnp.dot(a, b, preferred_element_type=F32)


def _dot_nt(a, b):
    return lax.dot_general(a, b, (((1,), (1,)), ((), ())), preferred_element_type=F32)


def _dot_tn(a, b):
    return lax.dot_general(a, b, (((0,), (0,)), ((), ())), preferred_element_type=F32)


def _rms_fwd(x, g):
    r = lax.rsqrt(jnp.mean(x * x, axis=-1, keepdims=True) + EPS)
    return x * r * g


def _rms_bwd(x, g, dy):
    r = lax.rsqrt(jnp.mean(x * x, axis=-1, keepdims=True) + EPS)
    xh = x * r
    t = dy * g
    dx = r * (t - xh * jnp.mean(t * xh, axis=-1, keepdims=True))
    dg = jnp.sum(dy * xh, axis=0, keepdims=True)
    return dx, dg


def _row_tile(lp, cands=(640, 512, 384, 256, 128)):
    for t in cands:
        if lp % t == 0:
            return t
    raise ValueError(f"row count {lp} is not a multiple of 128")


def _full(shape):
    n = len(shape)
    return pl.BlockSpec(shape, lambda *_: (0,) * n, pipeline_mode=pl.Buffered(1))


def _full_out(shape):
    n = len(shape)
    return pl.BlockSpec(shape, lambda *_: (0,) * n)


def _params(sem=("arbitrary",)):
    return pltpu.CompilerParams(dimension_semantics=sem, vmem_limit_bytes=VMEM_LIMIT)


def _rope_table(lp):
    half = ROT_DIM // 2
    pos = jnp.maximum(jnp.arange(lp) - LEAD_PAD, 0).astype(F32)
    inv_freq = jnp.power(jnp.float32(ROPE_THETA), -jnp.arange(0, ROT_DIM, 2, dtype=F32) / ROT_DIM)
    ang_t = jnp.concatenate([inv_freq, inv_freq])[:, None] * pos[None, :]
    row = lax.broadcasted_iota(jnp.int32, (ROT_DIM, lp), 0)
    cs_t = jnp.where(row < half, jnp.cos(ang_t), jnp.sin(ang_t))
    return jnp.pad(cs_t.T, ((0, 0), (0, 128 - ROT_DIM)))


def _rope_coeffs(t):
    half = ROT_DIM // 2
    lane = lax.broadcasted_iota(jnp.int32, t.shape, 1)
    cos_a = jnp.where(lane < half, t, 0.0)
    sin_a = pltpu.roll(jnp.where((lane >= half) & (lane < ROT_DIM), t, 0.0), 128 - half, 1)
    c = cos_a + pltpu.roll(cos_a, half, 1) + jnp.where((lane >= ROT_DIM) & (lane < HEAD_DIM), 1.0, 0.0)
    s2 = pltpu.roll(sin_a, half, 1)
    both = lambda u: u + pltpu.roll(u, HEAD_DIM, 1)
    return both(c), both(-sin_a), both(s2)


def _rope(t, c, s1, s2):
    return t * c + pltpu.roll(t, BLOCK - 8, 1) * s1 + pltpu.roll(t, 8, 1) * s2


def _rope_t(dt, c, s1, s2):
    return dt * c + pltpu.roll(dt * s1, 8, 1) + pltpu.roll(dt * s2, BLOCK - 8, 1)


def _build_h(x, rope_compact, tm, exch, small_piece, name):
    seq = x.shape[0]
    lp = BLOCK + seq
    nt = lp // tm
    n_sub = tm // BLOCK
    small_shape = exch.land_shapes[small_piece].shape

    def body(*refs):
        h_ref, c_ref, s1_ref, s2_ref = refs[n_sub + 1:n_sub + 5]
        for j in range(n_sub):
            h_ref[j * BLOCK:(j + 1) * BLOCK, :] = refs[j][...]
        c_ref[...], s1_ref[...], s2_ref[...] = _rope_coeffs(refs[n_sub][...])

    def after(lands, *refs):
        h_ref, buf = refs[n_sub + 1], refs[n_sub + 5]
        pltpu.sync_copy(lands[small_piece], buf)
        h_ref[0:LEAD_PAD, :] = jnp.zeros((LEAD_PAD, D_MODEL), F32)
        for d in range(N_DEV):
            h_ref[LEAD_PAD:BLOCK, d * 128:(d + 1) * 128] = buf[d, 0:N_META, :]

    tile = lambda i: (i + 1) % nt
    piece = lambda j: pl.BlockSpec((BLOCK, D_MODEL), lambda i: (jnp.maximum(tile(i) * n_sub + j - 1, 0), 0))
    rows = lambda w: pl.BlockSpec((tm, w), lambda i: (tile(i), 0))
    (h, *rope), lands = _call(
        body, exch,
        name=name,
        grid=(nt,),
        in_specs=[piece(j) for j in range(n_sub)] + [rows(128)],
        out_specs=[rows(D_MODEL)] + [rows(128)] * 3,
        out_shape=[jax.ShapeDtypeStruct((lp, D_MODEL), F32)] + [jax.ShapeDtypeStruct((lp, 128), F32)] * 3,
        scratch_shapes=[pltpu.VMEM(small_shape, F32)],
        compiler_params=_params(),
        after=after,
    )(*([x] * n_sub), rope_compact)
    return h, rope, lands


def _in_proj_fwd(h, g, w_in_t, rope, tm, name, exch=None):
    lp = h.shape[0]

    def body(h_ref, g_ref, w_ref, c_ref, s1_ref, s2_ref, a_ref, qkv_ref, bch_ref):
        a = _rms_fwd(h_ref[...], g_ref[...]).astype(BF)
        a_ref[...] = a
        proj = _dot_nt(a, w_ref[...])
        c, s1, s2 = c_ref[...], s1_ref[...], s2_ref[...]
        for j in range(5):
            t = _rope(proj[:, j * 128:(j + 1) * 128], c, s1, s2)
            qkv_ref[:, j * 128:(j + 1) * 128] = (t * SCALE if j < 4 else t).astype(BF)
        qkv_ref[:, 640:768] = proj[:, 640:768].astype(BF)
        bch_ref[...] = proj[:, 768:].astype(BF)

    row = lambda w: pl.BlockSpec((tm, w), lambda i: (i, 0))
    return _call(
        body, exch,
        name=name,
        grid=(lp // tm,),
        in_specs=[row(D_MODEL), _full((1, D_MODEL)), _full((IN_W, D_MODEL)), row(128), row(128), row(128)],
        out_specs=[row(D_MODEL), row(768), row(3 * CONV_W)],
        out_shape=[
            jax.ShapeDtypeStruct((lp, D_MODEL), BF),
            jax.ShapeDtypeStruct((lp, 768), BF),
            jax.ShapeDtypeStruct((lp, 3 * CONV_W), BF),
        ],
        compiler_params=_params(),
    )(h, g, w_in_t, *rope)


def _fold_masks(i):
    r = lax.broadcasted_iota(jnp.int32, (2 * BLOCK, BLOCK), 0) & (BLOCK - 1)
    c = lax.broadcasted_iota(jnp.int32, (2 * BLOCK, BLOCK), 1)
    tri = c > r
    ok = jnp.where(tri, (i - 1) * BLOCK + c, i * BLOCK + c) >= LEAD_PAD
    return tri, ok


def _kv_operand(x, kvh):
    lane = lax.broadcasted_iota(jnp.int32, x.shape, 1)
    zero = jnp.zeros_like(x)
    if kvh == 0:
        lo = jnp.where(lane < HEAD_DIM, x, zero)
        hi = pltpu.roll(lo, HEAD_DIM, 1)
    else:
        hi = jnp.where(lane >= HEAD_DIM, x, zero)
        lo = pltpu.roll(hi, HEAD_DIM, 1)
    return jnp.concatenate([lo, hi], axis=0)


def _split4(t, tri):
    zero = jnp.zeros_like(t[0])
    return jnp.concatenate(
        [jnp.where(tri, t[0], zero), jnp.where(tri, zero, t[0]), jnp.where(tri, t[1], zero), jnp.where(tri, zero, t[1])], axis=1)


def _sink_cols(sink_ref, kvh):
    first = lax.broadcasted_iota(jnp.int32, (2 * BLOCK, 1), 0) < BLOCK
    return [jnp.where(first, sink_ref[0, 4 * kvh + half], sink_ref[0, 4 * kvh + 2 + half]) for half in range(2)]


def _folded_exp(q2, k4, tri, ok, sks):
    s = _dot_nt(q2, k4)
    es, ss = [], []
    for half in range(2):
        s_h = s[:, 2 * half * BLOCK:2 * (half + 1) * BLOCK]
        sf = jnp.where(ok, jnp.where(tri, s_h[:, :BLOCK], s_h[:, BLOCK:]), NEG)
        m = jnp.maximum(jnp.max(sf, axis=-1, keepdims=True), sks[half])
        es.append(jnp.exp(sf - m))
        ss.append(jnp.exp(sks[half] - m))
    sums = _dot(jnp.concatenate(es, axis=0).astype(BF), jnp.ones((BLOCK, BLOCK), BF))
    invs = [1.0 / (sums[2 * half * BLOCK:2 * (half + 1) * BLOCK] + ss[half]) for half in range(2)]
    return es, ss, invs


def _attn_fwd(qkv, sink, name, exch=None):
    lp = qkv.shape[0]
    nb = lp // BLOCK
    per_step = 2

    def one_block(i, sink_ref, q_ref, kvc_ref, kvp_ref, o_ref, p_ref, ps_ref):
        tri, ok = _fold_masks(i)
        kvc, kvp = kvc_ref[...], kvp_ref[...]
        kk = jnp.concatenate([kvp[:, :128], kvc[:, :128]], axis=0)
        vv = jnp.concatenate([kvp[:, 128:], kvc[:, 128:]], axis=0)
        lane = lax.broadcasted_iota(jnp.int32, (BLOCK, 128), 1)
        p_sink = jnp.zeros((BLOCK, 128), F32)
        for kvh in range(2):
            q2 = jnp.concatenate([q_ref[:, 256 * kvh:256 * kvh + 128], q_ref[:, 256 * kvh + 128:256 * kvh + 256]], axis=0)
            es, ss, invs = _folded_exp(q2, _kv_operand(kk, kvh), tri, ok, _sink_cols(sink_ref, kvh))
            pb = [(es[half] * invs[half]).astype(BF) for half in range(2)]
            out = _dot(_split4(pb, tri), _kv_operand(vv, kvh))
            for pair in range(2):
                rows = slice(pair * BLOCK, (pair + 1) * BLOCK)
                o_ref[:, 256 * kvh + 128 * pair:256 * kvh + 128 * (pair + 1)] = out[rows].astype(BF)
                for half in range(2):
                    head = 4 * kvh + 2 * pair + half
                    p_ref[:, 128 * head:128 * (head + 1)] = pb[half][rows]
                    p_sink = jnp.where(lane == head, (ss[half] * invs[half][:, 0:1])[rows], p_sink)
        ps_ref[...] = p_sink

    def body(sink_ref, *refs):
        q_refs, kv_refs = refs[:per_step], refs[per_step:2 * per_step + 1]
        o_ref, p_ref, ps_ref = refs[2 * per_step + 1:]
        for j in range(per_step):
            rows = slice(j * BLOCK, (j + 1) * BLOCK)
            one_block(per_step * pl.program_id(0) + j, sink_ref, q_refs[j], kv_refs[j + 1], kv_refs[j],
                      o_ref.at[rows], p_ref.at[rows], ps_ref.at[rows])

    last = nb - 1
    blk = lambda j: (lambda s: jnp.minimum(per_step * s + j, last))
    out_rows = lambda w: pl.BlockSpec((per_step * BLOCK, w), lambda s: (s, 0))
    return _call(
        body, exch,
        name=name,
        grid=(pl.cdiv(nb, per_step),),
        in_specs=[pl.BlockSpec(memory_space=pltpu.SMEM)]
        + [pl.BlockSpec((BLOCK, ATTN_W), lambda s, j=j: (blk(j)(s), 0)) for j in range(per_step)]
        + [pl.BlockSpec((BLOCK, 256), lambda s: (jnp.maximum(per_step * s - 1, 0), 2))]
        + [pl.BlockSpec((BLOCK, 256), lambda s, j=j: (blk(j)(s), 2)) for j in range(per_step)],
        out_specs=[out_rows(ATTN_W), out_rows(N_Q_HEADS * BLOCK), out_rows(128)],
        out_shape=[jax.ShapeDtypeStruct((lp, ATTN_W), BF), jax.ShapeDtypeStruct((lp, N_Q_HEADS * BLOCK), BF),
                   jax.ShapeDtypeStruct((lp, 128), F32)],
        compiler_params=_params(),
    )(sink, *([qkv] * (2 * per_step + 1)))


def _mix_out_fwd(bch, y_attn, h, conv_w, g_a, g_c, w_out, g_post, tm, name, exch=None):
    lp = h.shape[0]

    def body(bch_ref, ya_ref, h_ref, cw_ref, ga_ref, gc_ref, w_ref, gp_ref, yc_ref, y_ref, z_ref, h2_ref, ext):
        i = pl.program_id(0)

        @pl.when(i == 0)
        def _():
            ext[0:8, :] = jnp.zeros((8, CONV_W), F32)

        b = bch_ref[:, 0:CONV_W].astype(F32)
        u = bch_ref[:, CONV_W:2 * CONV_W].astype(F32) * bch_ref[:, 2 * CONV_W:3 * CONV_W].astype(F32)
        ext[8:8 + tm, :] = u
        yc = cw_ref[0:1, :] * ext[6:6 + tm, :] + cw_ref[1:2, :] * ext[7:7 + tm, :] + cw_ref[2:3, :] * u
        ext[0:8, :] = u[tm - 8:tm, :]
        yc_ref[...] = yc.astype(BF)
        ya = _rms_fwd(ya_ref[...].astype(F32), ga_ref[...]).astype(BF)
        yb = _rms_fwd(b * yc, gc_ref[...]).astype(BF)
        y_ref[:, 0:ATTN_W] = ya
        y_ref[:, ATTN_W:] = yb
        z = _dot(ya, w_ref[0:ATTN_W, :]) + _dot(yb, w_ref[ATTN_W:, :])
        z_ref[...] = z
        h2_ref[...] = h_ref[...] + _rms_fwd(z, gp_ref[...])

    row = lambda w: pl.BlockSpec((tm, w), lambda i: (i, 0))
    return _call(
        body, exch,
        name=name,
        grid=(lp // tm,),
        in_specs=[
            row(3 * CONV_W), row(ATTN_W), row(D_MODEL), _full((8, CONV_W)), _full((1, ATTN_W)), _full((1, CONV_W)),
            _full((D_MODEL, D_MODEL)), _full((1, D_MODEL)),
        ],
        out_specs=[row(CONV_W), row(D_MODEL), row(D_MODEL), row(D_MODEL)],
        out_shape=[
            jax.ShapeDtypeStruct((lp, CONV_W), BF),
            jax.ShapeDtypeStruct((lp, D_MODEL), BF),
            jax.ShapeDtypeStruct((lp, D_MODEL), F32),
            jax.ShapeDtypeStruct((lp, D_MODEL), F32),
        ],
        scratch_shapes=[pltpu.VMEM((tm + 8, CONV_W), F32)],
        compiler_params=_params(),
    )(bch, y_attn, h, conv_w, g_a, g_c, w_out, g_post)


def _mlp_fwd(h2, g_pre, w_up_t, w_down, g_post, tm, name, exch=None, target=None):
    lp = h2.shape[0]
    sub = math.gcd(tm, BLOCK)
    n_sub, lead = tm // sub, BLOCK // sub
    n_t = n_sub if target is not None else 0

    def body(*refs):
        h_ref, gp_ref, wu_ref, wd_ref, gq_ref = refs[:5]
        t_refs = refs[5:5 + n_t]
        a_ref, up_ref, f_ref, last_ref = refs[5 + n_t:9 + n_t]
        h = h_ref[...]
        a = _rms_fwd(h, gp_ref[...]).astype(BF)
        a_ref[...] = a
        up = _dot_nt(a, wu_ref[...])
        up_ref[...] = up.astype(BF)
        act = jnp.square(jnp.maximum(up, 0.0)).astype(BF)
        f = _dot(act, wd_ref[...])
        f_ref[...] = f
        h3 = h + _rms_fwd(f, gq_ref[...])
        if target is None:
            last_ref[...] = h3
            return
        ls_ref = refs[9 + n_t]
        i = pl.program_id(0)

        @pl.when(i == 0)
        def _():
            ls_ref[...] = jnp.zeros((8, 128), F32)

        sq = jnp.zeros((1, 1), F32)
        for j in range(n_sub):
            on_tokens = i * n_sub + j >= lead
            d = jnp.where(on_tokens, h3[j * sub:(j + 1) * sub] - t_refs[j][...], 0.0)
            last_ref[j * sub:(j + 1) * sub, :] = d * (1.0 / D_MODEL)
            sq = sq + jnp.sum(d * d)
        ls_ref[...] += sq

    row = lambda w: pl.BlockSpec((tm, w), lambda i: (i, 0))
    piece = lambda j: pl.BlockSpec((sub, D_MODEL), lambda i: (jnp.maximum(i * n_sub + j - lead, 0), 0))
    out_specs = [row(D_MODEL), row(D_FF), row(D_MODEL), row(D_MODEL)]
    out_shape = [
        jax.ShapeDtypeStruct((lp, D_MODEL), BF),
        jax.ShapeDtypeStruct((lp, D_FF), BF),
        jax.ShapeDtypeStruct((lp, D_MODEL), F32),
        jax.ShapeDtypeStruct((lp, D_MODEL), F32),
    ]
    if target is not None:
        out_specs.append(_full_out((8, 128)))
        out_shape.append(jax.ShapeDtypeStruct((8, 128), F32))
    return _call(
        body, exch,
        name=name,
        grid=(lp // tm,),
        in_specs=[row(D_MODEL), _full((1, D_MODEL)), _full((D_FF, D_MODEL)), _full((D_FF, D_MODEL)), _full((1, D_MODEL))]
        + [piece(j) for j in range(n_t)],
        out_specs=out_specs,
        out_shape=out_shape,
        compiler_params=_params(),
    )(h2, g_pre, w_up_t, w_down, g_post, *([target] * n_t))


def _mlp_bwd_dx(dh3, f, up, h2, w_down, w_up_t, g_post, g_pre, tm, name, exch=None):
    lp = h2.shape[0]

    def body(dh3_ref, f_ref, up_ref, h2_ref, wd_ref, wu_ref, gq_ref, gp_ref, df_ref, dup_ref, dh2_ref, dg_ref):
        i = pl.program_id(0)

        @pl.when(i == 0)
        def _():
            dg_ref[...] = jnp.zeros((8, D_MODEL), F32)

        dh3 = dh3_ref[...]
        df, dgq = _rms_bwd(f_ref[...], gq_ref[...], dh3)
        dg_ref[ROW_MLP_POST:ROW_MLP_POST + 1, :] += dgq
        df = df.astype(BF)
        df_ref[...] = df
        dact = _dot_nt(df, wd_ref[...])
        dup = (dact * (2.0 * jnp.maximum(up_ref[...].astype(F32), 0.0))).astype(BF)
        dup_ref[...] = dup
        da = _dot(dup, wu_ref[...])
        dh, dgp = _rms_bwd(h2_ref[...], gp_ref[...], da)
        dg_ref[ROW_MLP_PRE:ROW_MLP_PRE + 1, :] += dgp
        dh2_ref[...] = dh3 + dh

    row = lambda w: pl.BlockSpec((tm, w), lambda i: (i, 0))
    return _call(
        body, exch,
        name=name,
        grid=(lp // tm,),
        in_specs=[
            row(D_MODEL), row(D_MODEL), row(D_FF), row(D_MODEL), _full((D_FF, D_MODEL)), _full((D_FF, D_MODEL)),
            _full((1, D_MODEL)), _full((1, D_MODEL)),
        ],
        out_specs=[row(D_MODEL), row(D_FF), row(D_MODEL), _full_out((8, D_MODEL))],
        out_shape=[
            jax.ShapeDtypeStruct((lp, D_MODEL), BF),
            jax.ShapeDtypeStruct((lp, D_FF), BF),
            jax.ShapeDtypeStruct((lp, D_MODEL), F32),
            jax.ShapeDtypeStruct((8, D_MODEL), F32),
        ],
        compiler_params=_params(),
    )(dh3, f, up, h2, w_down, w_up_t, g_post, g_pre)


def _mlp_bwd_dw(up, df, dup, a2, tm, name):
    lp = up.shape[0]
    nt = lp // tm
    nj = D_FF // D_MODEL

    def body(up_ref, df_ref, dup_ref, a_ref, dwd_ref, dwu_ref, accd, accu):
        i = pl.program_id(1)

        @pl.when(i == 0)
        def _():
            accd[...] = jnp.zeros_like(accd)
            accu[...] = jnp.zeros_like(accu)

        act = jnp.square(jnp.maximum(up_ref[...].astype(F32), 0.0)).astype(BF)
        accd[...] += _dot_tn(act, df_ref[...])
        accu[...] += _dot_tn(dup_ref[...], a_ref[...])

        @pl.when(i == nt - 1)
        def _():
            dwd_ref[...] = accd[...].astype(BF)
            dwu_ref[...] = accu[...].astype(BF)

    return pl.pallas_call(
        body,
        name=name,
        grid=(nj, nt),
        in_specs=[
            pl.BlockSpec((tm, D_MODEL), lambda j, i: (i, j)),
            pl.BlockSpec((tm, D_MODEL), lambda j, i: (i, 0)),
            pl.BlockSpec((tm, D_MODEL), lambda j, i: (i, j)),
            pl.BlockSpec((tm, D_MODEL), lambda j, i: (i, 0)),
        ],
        out_specs=[pl.BlockSpec((D_MODEL, D_MODEL), lambda j, i: (j, 0)), pl.BlockSpec((D_MODEL, D_MODEL), lambda j, i: (j, 0))],
        out_shape=[jax.ShapeDtypeStruct((D_FF, D_MODEL), BF), jax.ShapeDtypeStruct((D_FF, D_MODEL), BF)],
        scratch_shapes=[pltpu.VMEM((D_MODEL, D_MODEL), F32), pltpu.VMEM((D_MODEL, D_MODEL), F32)],
        compiler_params=_params(("arbitrary", "arbitrary")),
    )(up, df, dup, a2)


def _mix_out_bwd(dh2, z, y_attn, yc, bch, y, w_out, g_post, g_a, g_c, conv_w, tm, name, exch=None):
    lp = dh2.shape[0]
    nt = lp // tm

    def body(dh2_ref, z_ref, ya_ref, yc_ref, bch_ref, y_ref, w_ref, gp_ref, ga_ref, gc_ref, cw_ref,
             dya_ref, dbch_ref, dg_ref, dwo_ref, ext, acco):
        i = pl.program_id(0)
        dcw_ref = dg_ref.at[ROW_CONV:ROW_CONV + 3, 0:CONV_W]

        @pl.when(i == 0)
        def _():
            ext[tm:tm + 8, :] = jnp.zeros((8, CONV_W), F32)
            dg_ref[...] = jnp.zeros((8, D_MODEL), F32)
            acco[...] = jnp.zeros_like(acco)

        dz, dgp = _rms_bwd(z_ref[...], gp_ref[...], dh2_ref[...])
        dg_ref[ROW_MIX_POST:ROW_MIX_POST + 1, :] += dgp
        dz = dz.astype(BF)
        acco[...] += _dot_tn(y_ref[...], dz)
        dya_n = _dot_nt(dz, w_ref[0:ATTN_W, :])
        dyb_n = _dot_nt(dz, w_ref[ATTN_W:, :])
        dya, dga = _rms_bwd(ya_ref[...].astype(F32), ga_ref[...], dya_n)
        dg_ref[ROW_GROUP_G:ROW_GROUP_G + 1, 0:ATTN_W] += dga
        dya_ref[...] = dya
        b = bch_ref[:, 0:CONV_W].astype(F32)
        c = bch_ref[:, CONV_W:2 * CONV_W].astype(F32)
        hc = bch_ref[:, 2 * CONV_W:3 * CONV_W].astype(F32)
        u = c * hc
        yc_v = yc_ref[...].astype(F32)
        dyconv, dgc = _rms_bwd(b * yc_v, gc_ref[...], dyb_n)
        dg_ref[ROW_GROUP_G:ROW_GROUP_G + 1, ATTN_W:] += dgc
        dbch_ref[:, 0:CONV_W] = (dyconv * yc_v).astype(BF)
        dyc = dyconv * b
        ext[0:tm, :] = dyc
        d1 = ext[1:1 + tm, :]
        d2 = ext[2:2 + tm, :]
        du = cw_ref[2:3, :] * dyc + cw_ref[1:2, :] * d1 + cw_ref[0:1, :] * d2
        ext[tm:tm + 8, :] = dyc[0:8, :]
        dbch_ref[:, CONV_W:2 * CONV_W] = (du * hc).astype(BF)
        dbch_ref[:, 2 * CONV_W:3 * CONV_W] = (du * c).astype(BF)
        dcw_ref[0:1, :] += jnp.sum(u * d2, axis=0, keepdims=True)
        dcw_ref[1:2, :] += jnp.sum(u * d1, axis=0, keepdims=True)
        dcw_ref[2:3, :] += jnp.sum(u * dyc, axis=0, keepdims=True)

        @pl.when(i == nt - 1)
        def _():
            dwo_ref[...] = acco[...].astype(BF)

    row = lambda w: pl.BlockSpec((tm, w), lambda i: (nt - 1 - i, 0))
    return _call(
        body, exch,
        name=name,
        grid=(nt,),
        in_specs=[
            row(D_MODEL), row(D_MODEL), row(ATTN_W), row(CONV_W), row(3 * CONV_W), row(D_MODEL), _full((D_MODEL, D_MODEL)),
            _full((1, D_MODEL)), _full((1, ATTN_W)), _full((1, CONV_W)), _full((8, CONV_W)),
        ],
        out_specs=[row(ATTN_W), row(3 * CONV_W), _full_out((8, D_MODEL)), _full_out((D_MODEL, D_MODEL))],
        out_shape=[
            jax.ShapeDtypeStruct((lp, ATTN_W), F32),
            jax.ShapeDtypeStruct((lp, 3 * CONV_W), BF),
            jax.ShapeDtypeStruct((8, D_MODEL), F32),
            jax.ShapeDtypeStruct((D_MODEL, D_MODEL), BF),
        ],
        scratch_shapes=[pltpu.VMEM((tm + 8, CONV_W), F32), pltpu.VMEM((D_MODEL, D_MODEL), F32)],
        compiler_params=_params(),
    )(dh2, z, y_attn, yc, bch, y, w_out, g_post, g_a, g_c, conv_w)


def _attn_bwd(qkv, o, do, probs, p_sink, rope, name, exch=None):
    lp = qkv.shape[0]
    nb = lp // BLOCK

    def body(q_ref, kvc_ref, kvp_ref, o_ref, do_ref, p_ref, ps_ref, cq_ref, s1q_ref, s2q_ref, ck_ref, s1k_ref, s2k_ref,
             dq_ref, dkv_ref, dsink_ref, carry):
        i = pl.program_id(0)

        @pl.when(i == 0)
        def _():
            carry[...] = jnp.zeros_like(carry)
            dsink_ref[...] = jnp.zeros((8, 128), F32)

        def finish(tot):
            dk = _rope_t(tot[:, :128], ck_ref[...], s1k_ref[...], s2k_ref[...])
            dkv_ref[:, 0:128] = dk.astype(BF)
            dkv_ref[:, 128:256] = tot[:, 128:].astype(BF)

        @pl.when(i < nb)
        def _():
            tri, _ = _fold_masks(i)
            kvc, kvp = kvc_ref[...], kvp_ref[...]
            kk = jnp.concatenate([kvp[:, :128], kvc[:, :128]], axis=0)
            vv = jnp.concatenate([kvp[:, 128:], kvc[:, 128:]], axis=0)
            lane = lax.broadcasted_iota(jnp.int32, (BLOCK, 128), 1)
            lane2 = lax.broadcasted_iota(jnp.int32, (2 * BLOCK, 128), 1)
            rope_q = (cq_ref[...], s1q_ref[...], s2q_ref[...])
            deltas = jnp.zeros((BLOCK, 128), F32)
            folded = []
            for kvh in range(2):
                c0 = 256 * kvh
                q2 = jnp.concatenate([q_ref[:, c0:c0 + 128], q_ref[:, c0 + 128:c0 + 256]], axis=0)
                do2 = jnp.concatenate([do_ref[:, c0:c0 + 128], do_ref[:, c0 + 128:c0 + 256]], axis=0)
                o2 = jnp.concatenate([o_ref[:, c0:c0 + 128], o_ref[:, c0 + 128:c0 + 256]], axis=0).astype(F32)
                k4, v4 = _kv_operand(kk, kvh), _kv_operand(vv, kvh)
                prod = do2 * o2
                dob = do2.astype(BF)
                dp = _dot_nt(dob, v4)
                ds, pb = [], []
                for half in range(2):
                    heads = [4 * kvh + 2 * pair + half for pair in range(2)]
                    p = jnp.concatenate([p_ref[:, 128 * h:128 * (h + 1)] for h in heads], axis=0)
                    sel = (lane2 < HEAD_DIM) if half == 0 else (lane2 >= HEAD_DIM)
                    delta = jnp.sum(jnp.where(sel, prod, 0.0), axis=-1, keepdims=True)
                    dp_h = dp[:, 2 * half * BLOCK:2 * (half + 1) * BLOCK]
                    ds.append((p.astype(F32) * (jnp.where(tri, dp_h[:, :BLOCK], dp_h[:, BLOCK:]) - delta)).astype(BF))
                    pb.append(p)
                    for pair in range(2):
                        deltas = jnp.where(lane == heads[pair], delta[pair * BLOCK:(pair + 1) * BLOCK], deltas)
                ds4, p4 = _split4(ds, tri), _split4(pb, tri)
                dq2 = _dot(ds4, k4) * SCALE
                dq_ref[:, c0:c0 + 128] = _rope_t(dq2[:BLOCK], *rope_q).astype(BF)
                dq_ref[:, c0 + 128:c0 + 256] = _rope_t(dq2[BLOCK:], *rope_q).astype(BF)
                rk, rv = _dot_tn(ds4, q2), _dot_tn(p4, dob)
                own = (lane < HEAD_DIM) if kvh == 0 else (lane >= HEAD_DIM)
                group = []
                for r in (rk, rv):
                    for blk in range(2):
                        t = jnp.where(lane < HEAD_DIM, r[blk * BLOCK:(blk + 1) * BLOCK], r[(2 + blk) * BLOCK:(3 + blk) * BLOCK])
                        group.append(jnp.where(own, t + pltpu.roll(t, HEAD_DIM, 1), 0.0))
                folded.append(group)
            dsink_ref[ROW_SINK:ROW_SINK + 1, :] -= jnp.sum(ps_ref[...] * deltas, axis=0, keepdims=True)
            dk_p, dk_c, dv_p, dv_c = [folded[0][t] + folded[1][t] for t in range(4)]
            finish(carry[...] + jnp.concatenate([dk_p, dv_p], axis=1))
            carry[...] = jnp.concatenate([dk_c, dv_c], axis=1)

        @pl.when(i == nb)
        def _():
            finish(carry[...])

    qi = lambda i: jnp.minimum(i, nb - 1)
    ki = lambda i: jnp.maximum(i - 1, 0)
    tab_q = pl.BlockSpec((BLOCK, 128), lambda i: (qi(i), 0))
    tab_k = pl.BlockSpec((BLOCK, 128), lambda i: (ki(i), 0))
    return _call(
        body, exch,
        name=name,
        grid=(nb + 1,),
        in_specs=[
            pl.BlockSpec((BLOCK, ATTN_W), lambda i: (qi(i), 0)),
            pl.BlockSpec((BLOCK, 256), lambda i: (qi(i), 2)),
            pl.BlockSpec((BLOCK, 256), lambda i: (jnp.maximum(qi(i) - 1, 0), 2)),
            pl.BlockSpec((BLOCK, ATTN_W), lambda i: (qi(i), 0)),
            pl.BlockSpec((BLOCK, ATTN_W), lambda i: (qi(i), 0)),
            pl.BlockSpec((BLOCK, N_Q_HEADS * BLOCK), lambda i: (qi(i), 0)),
            tab_q, tab_q, tab_q, tab_q, tab_k, tab_k, tab_k,
        ],
        out_specs=[
            pl.BlockSpec((BLOCK, ATTN_W), lambda i: (qi(i), 0)),
            pl.BlockSpec((BLOCK, 256), lambda i: (ki(i), 0)),
            pl.BlockSpec((8, 128), lambda i: (0, 0)),
        ],
        out_shape=[
            jax.ShapeDtypeStruct((lp, ATTN_W), BF),
            jax.ShapeDtypeStruct((lp, 256), BF),
            jax.ShapeDtypeStruct((8, 128), F32),
        ],
        scratch_shapes=[pltpu.VMEM((BLOCK, 256), F32)],
        compiler_params=_params(),
    )(qkv, qkv, qkv, o, do, probs, p_sink, *rope, *rope)


def _in_proj_bwd_dx(dq, dkv, dbch, w_in_t, h, dh2, g, tm, name, exch=None):
    lp = h.shape[0]

    def body(dq_ref, dkv_ref, dbch_ref, w_ref, h_ref, dh2_ref, g_ref, dh_ref, dg_ref):
        i = pl.program_id(0)

        @pl.when(i == 0)
        def _():
            dg_ref[...] = jnp.zeros((8, D_MODEL), F32)

        da = _dot(dq_ref[...], w_ref[0:512, :]) + _dot(dkv_ref[...], w_ref[512:768, :]) + _dot(dbch_ref[...], w_ref[768:, :])
        dh, dg = _rms_bwd(h_ref[...], g_ref[...], da)
        dg_ref[ROW_MIX_PRE:ROW_MIX_PRE + 1, :] += dg
        dh_ref[...] = dh2_ref[...] + dh

    row = lambda w: pl.BlockSpec((tm, w), lambda i: (i, 0))
    return _call(
        body, exch,
        name=name,
        grid=(lp // tm,),
        in_specs=[row(ATTN_W), row(256), row(3 * CONV_W), _full((IN_W, D_MODEL)), row(D_MODEL), row(D_MODEL), _full((1, D_MODEL))],
        out_specs=[row(D_MODEL), _full_out((8, D_MODEL))],
        out_shape=[jax.ShapeDtypeStruct((lp, D_MODEL), F32), jax.ShapeDtypeStruct((8, D_MODEL), F32)],
        compiler_params=_params(),
    )(dq, dkv, dbch, w_in_t, h, dh2, g)


def _mix_bwd_dw(dq, dkv, dbch, a, tm, name, exch=None):
    lp = a.shape[0]
    nt = lp // tm

    def body(dq_ref, dkv_ref, dbch_ref, a_ref, dwi_ref, acci):
        i = pl.program_id(0)

        @pl.when(i == 0)
        def _():
            acci[...] = jnp.zeros_like(acci)

        a_v = a_ref[...]
        acci[0:512, :] += _dot_tn(dq_ref[...], a_v)
        acci[512:768, :] += _dot_tn(dkv_ref[...], a_v)
        acci[768:, :] += _dot_tn(dbch_ref[...], a_v)

        @pl.when(i == nt - 1)
        def _():
            dwi_ref[...] = acci[...].astype(BF)

    row = lambda w: pl.BlockSpec((tm, w), lambda i: (i, 0))
    return _call(
        body, exch,
        name=name,
        grid=(nt,),
        in_specs=[row(ATTN_W), row(256), row(3 * CONV_W), row(D_MODEL)],
        out_specs=[_full_out((IN_W, D_MODEL))],
        out_shape=[jax.ShapeDtypeStruct((IN_W, D_MODEL), BF)],
        scratch_shapes=[pltpu.VMEM((IN_W, D_MODEL), F32)],
        compiler_params=_params(),
    )(dq, dkv, dbch, a)


def _mesh_place():
    x, y, c = lax.axis_index("x"), lax.axis_index("y"), lax.axis_index("c")
    return x, y, c, 4 * x + 2 * y + c


def _peer(x, y, c, k):
    px = 1 - x if k & 4 else x
    py = 1 - y if k & 2 else y
    pc = 1 - c if k & 1 else c
    return (px, py, pc), 4 * px + 2 * py + pc


SIBLING = 1
SAME_CORE = (2, 4, 6)
OTHER_CORE = (3, 5, 7)


class _Exchange:
    def __init__(self, pieces):
        self.srcs = [s for s, _ in pieces]
        self.to_all = [g for _, g in pieces]
        self.n = len(pieces)
        self.land_shapes = [
            jax.ShapeDtypeStruct((N_DEV,) + (s.shape if g else s.shape[1:]), s.dtype) for s, g in pieces]
        self.sem_shapes = [pltpu.SemaphoreType.DMA((self.n, N_DEV - 1)), pltpu.SemaphoreType.DMA((self.n, N_DEV - 1)),
                           pltpu.SemaphoreType.DMA((self.n,))]
        self.forwards = any(self.to_all)

    def _ops(self, srcs, lands, sems):
        send_sems, recv_sems, local_sems = sems
        x, y, c, me = _mesh_place()

        def remote(p, k, src, slot, to):
            return pltpu.make_async_remote_copy(
                src_ref=src, dst_ref=lands[p].at[slot], send_sem=send_sems.at[p, k - 1], recv_sem=recv_sems.at[p, k - 1],
                device_id=to, device_id_type=MESH)

        def own(p):
            return pltpu.make_async_copy(srcs[p] if self.to_all[p] else srcs[p].at[me], lands[p].at[me], local_sems.at[p])

        def direct(p, k):
            peer, pidx = _peer(x, y, c, k)
            return remote(p, k, srcs[p] if self.to_all[p] else srcs[p].at[pidx], me, peer)

        def forward(p, k):
            sibling, _ = _peer(x, y, c, SIBLING)
            _, origin = _peer(x, y, c, k ^ SIBLING)
            return remote(p, k, lands[p].at[origin], origin, sibling)

        def arrival(p, k):
            peer, pidx = _peer(x, y, c, k)
            return remote(p, k, lands[p].at[pidx], pidx, peer)

        return own, direct, forward, arrival

    def start(self, srcs, lands, sems):
        own, direct, _, _ = self._ops(srcs, lands, sems)
        for p in range(self.n):
            own(p).start()
            for k in ((SIBLING,) + SAME_CORE) if self.to_all[p] else range(1, N_DEV):
                direct(p, k).start()

    def forward(self, srcs, lands, sems):
        _, _, forward, arrival = self._ops(srcs, lands, sems)
        for p in range(self.n):
            if self.to_all[p]:
                for k in SAME_CORE:
                    arrival(p, k).wait_recv()
                    forward(p, k ^ SIBLING).start()

    def finish(self, srcs, lands, sems):
        own, direct, forward, arrival = self._ops(srcs, lands, sems)
        for p in range(self.n):
            for k in ((SIBLING,) + OTHER_CORE) if self.to_all[p] else range(1, N_DEV):
                arrival(p, k).wait_recv()
        for p in range(self.n):
            for k in range(1, N_DEV):
                (forward(p, k) if self.to_all[p] and k in OTHER_CORE else direct(p, k)).wait_send()
            own(p).wait()


def _call(body, exch, *, name, grid, in_specs, out_specs, out_shape, scratch_shapes=(), compiler_params, after=None):
    if exch is None:
        return pl.pallas_call(body, name=name, grid=grid, in_specs=in_specs, out_specs=out_specs, out_shape=out_shape,
                              scratch_shapes=scratch_shapes, compiler_params=compiler_params)
    n_in, n_out, n_scr, n_x = len(in_specs), len(out_shape), len(scratch_shapes), exch.n
    steps = math.prod(grid)

    def carrying(*refs):
        a, b, c, d, e = n_in, n_in + n_x, n_in + n_x + n_out, n_in + 2 * n_x + n_out, n_in + 2 * n_x + n_out + n_scr
        ins, srcs, outs, lands, scr, sems = refs[:a], refs[a:b], refs[b:c], refs[c:d], refs[d:e], refs[e:]
        step = functools.reduce(lambda acc, t: acc * grid[t] + pl.program_id(t), range(len(grid)), 0)

        @pl.when(step == 0)
        def _():
            exch.start(srcs, lands, sems)

        body(*ins, *outs, *scr)

        if exch.forwards:
            @pl.when(step == max(0, steps - 1 - (steps + 7) // 8))
            def _():
                exch.forward(srcs, lands, sems)

        @pl.when(step == steps - 1)
        def _():
            exch.finish(srcs, lands, sems)
            if after is not None:
                after(lands, *ins, *outs, *scr)

    hbm = pl.BlockSpec(memory_space=pl.ANY)
    call = pl.pallas_call(
        carrying, name=name, grid=grid, in_specs=list(in_specs) + [hbm] * n_x, out_specs=list(out_specs) + [hbm] * n_x,
        out_shape=list(out_shape) + exch.land_shapes, scratch_shapes=list(scratch_shapes) + exch.sem_shapes,
        compiler_params=compiler_params)

    def run(*args):
        res = call(*args, *exch.srcs)
        return list(res[:n_out]), list(res[n_out:])

    return run


def _sum_small(part):
    def body(part_ref, out_ref, land, send_sems, recv_sems):
        x, y, c, me = _mesh_place()
        land[me] = part_ref[...]
        sent = []
        for k in range(1, N_DEV):
            peer, _ = _peer(x, y, c, k)
            cp = pltpu.make_async_remote_copy(
                src_ref=part_ref, dst_ref=land.at[me], send_sem=send_sems.at[k - 1], recv_sem=recv_sems.at[k - 1],
                device_id=peer, device_id_type=MESH)
            cp.start()
            sent.append(cp)
        for k in range(1, N_DEV):
            peer, pidx = _peer(x, y, c, k)
            pltpu.make_async_remote_copy(
                src_ref=part_ref, dst_ref=land.at[pidx], send_sem=send_sems.at[k - 1], recv_sem=recv_sems.at[k - 1],
                device_id=peer, device_id_type=MESH).wait_recv()
        for cp in sent:
            cp.wait_send()
        acc = land[0]
        for d in range(1, N_DEV):
            acc = acc + land[d]
        out_ref[...] = acc

    vmem = pl.BlockSpec(memory_space=pltpu.VMEM)
    return pl.pallas_call(
        body,
        name="sum_small",
        in_specs=[vmem],
        out_specs=vmem,
        out_shape=jax.ShapeDtypeStruct(part.shape, F32),
        scratch_shapes=[pltpu.VMEM((N_DEV,) + part.shape, F32), pltpu.SemaphoreType.DMA((N_DEV - 1,)),
                        pltpu.SemaphoreType.DMA((N_DEV - 1,))],
    )(part)


def _adamw(w, g, m, v):
    m = ADAM_B1 * m + (1.0 - ADAM_B1) * g
    v = ADAM_B2 * v + (1.0 - ADAM_B2) * jnp.square(g)
    m_hat = m / (1.0 - ADAM_B1 ** ADAM_STEP)
    v_hat = v / (1.0 - ADAM_B2 ** ADAM_STEP)
    delta = -ADAM_LR * (m_hat / (jnp.sqrt(v_hat) + ADAM_EPS) + ADAM_WD * w)
    return delta, m, v


def _landed_specs(tr, wd):
    return [pl.BlockSpec((N_DEV, tr, wd), lambda l, i, ll=ll: (0, jnp.where(l == ll, i, 0), 0)) for ll in range(DEPTH)]


def _device_sum(r_ref):
    acc = r_ref[0].astype(F32)
    for d in range(1, N_DEV):
        acc = acc + r_ref[d].astype(F32)
    return acc


def _sum_adamw(recv, w, m, v, tr, name, transposed=False):
    _, r, wd = recv[0].shape

    def body(*refs):
        w_ref, m_ref, v_ref, g_ref, d_ref, mo_ref, vo_ref = refs[DEPTH:]
        for ll in range(DEPTH):
            @pl.when(pl.program_id(0) == ll)
            def _(ll=ll):
                g = _device_sum(refs[ll])
                g = g.T if transposed else g
                g_ref[0] = g
                d_ref[0], mo_ref[0], vo_ref[0] = _adamw(w_ref[0], g, m_ref[0], v_ref[0])

    if transposed:
        blk = pl.BlockSpec((1, wd, tr), lambda l, i: (l, 0, i))
        shape = jax.ShapeDtypeStruct((DEPTH, wd, r), F32)
    else:
        blk = pl.BlockSpec((1, tr, wd), lambda l, i: (l, i, 0))
        shape = jax.ShapeDtypeStruct((DEPTH, r, wd), F32)
    return pl.pallas_call(
        body,
        name=name,
        grid=(DEPTH, r // tr),
        in_specs=_landed_specs(tr, wd) + [blk, blk, blk],
        out_specs=[blk] * 4,
        out_shape=[shape] * 4,
        compiler_params=_params(("arbitrary", "arbitrary")),
    )(*recv, w, m, v)


def _adamw_small(ws, gs, ms, vs):
    n = len(ws)

    def body(*refs):
        w_r, g_r, m_r, v_r = refs[:n], refs[n:2 * n], refs[2 * n:3 * n], refs[3 * n:4 * n]
        d_o, m_o, v_o = refs[4 * n:5 * n], refs[5 * n:6 * n], refs[6 * n:7 * n]
        for t in range(n):
            d_o[t][...], m_o[t][...], v_o[t][...] = _adamw(w_r[t][...], g_r[t][...], m_r[t][...], v_r[t][...])

    vmem = pl.BlockSpec(memory_space=pltpu.VMEM)
    shapes = [jax.ShapeDtypeStruct(w.shape, F32) for w in ws]
    outs = pl.pallas_call(
        body,
        name="adamw_small",
        in_specs=[vmem] * (4 * n),
        out_specs=[vmem] * (3 * n),
        out_shape=shapes * 3,
    )(*ws, *gs, *ms, *vs)
    return outs[:n], outs[n:2 * n], outs[2 * n:]


def kernel(x, meta_tokens, mix_pre_g, w_in, conv_w, sinks, attn_out_g, conv_out_g, w_out, mix_post_g, mlp_pre_g, w_up, w_down, mlp_post_g, loss_target, m_meta_tokens, m_mix_pre_g, m_w_in, m_conv_w, m_sinks, m_attn_out_g, m_conv_out_g, m_w_out, m_mix_post_g, m_mlp_pre_g, m_w_up, m_w_down, m_mlp_post_g, v_meta_tokens, v_mix_pre_g, v_w_in, v_conv_w, v_sinks, v_attn_out_g, v_conv_out_g, v_w_out, v_mix_post_g, v_mlp_pre_g, v_w_up, v_w_down, v_mlp_post_g):
    seq = x.shape[1]
    lp = BLOCK + seq
    tm = _row_tile(lp)
    tm_mlp = _row_tile(lp, (320, 256, 128))
    tm_dw_mlp = _row_tile(lp, (1664, 1040, 640, 384, 256, 128))
    tm_dw_mix = _row_tile(lp, (1664, 832, 640, 384, 256, 128))
    me = 4 * lax.axis_index("x") + 2 * lax.axis_index("y") + lax.axis_index("c")
    cshard = CONV_W // N_DEV
    mshard = D_MODEL // N_DEV

    gather_with = {
        ("in_proj_fwd", 0): [("in", 1)], ("attn_fwd", 0): [("out", 0), ("up", 0)], ("mix_out_fwd", 0): [("down", 0)],
        ("mlp_fwd", 0): [("out", 1), ("up", 1), ("down", 1)],
    }
    scatter_with = {
        ("attn_bwd", 1): [("down", 1)], ("mix_bwd_dw", 1): [("out", 1)], ("mlp_bwd_dx", 0): [("up", 1), ("in", 1)],
        ("mix_out_bwd", 0): [("up", 0)], ("attn_bwd", 0): [("down", 0)], ("mix_bwd_dw", 0): [("out", 0)],
        ("in_proj_bwd_dx", 0): [("in", 0)],
    }
    shard = {"in": jnp.swapaxes(w_in, 1, 2).astype(BF), "out": w_out.astype(BF),
             "up": jnp.swapaxes(w_up, 1, 2).astype(BF), "down": w_down.astype(BF)}
    weight = {}
    grad = {}
    landed = {}

    def run(fn, kind, l, *args):
        key, name = (kind, l), f"{kind}_{l}"
        if key in gather_with:
            blocks = gather_with[key]
            outs, lands = fn(*args, name, _Exchange([(shard[n][k], True) for n, k in blocks]))
            for b, land in zip(blocks, lands):
                weight[b] = land.reshape(-1, D_MODEL)
            return outs
        if key in scatter_with:
            blocks = scatter_with[key]
            outs, lands = fn(*args, name, _Exchange([(grad[b].reshape(N_DEV, -1, D_MODEL), False) for b in blocks]))
            landed.update(zip(blocks, lands))
            return outs
        return fn(*args, name)

    small = jnp.zeros((24, 128), F32)
    small = small.at[0:N_META, :].set(meta_tokens)
    small = small.at[N_META:N_META + 6, 0:cshard].set(conv_w.reshape(6, cshard))
    first = _Exchange([(shard["in"][0], True), (small, True)])
    h, rope, (first_in, g_small) = _build_h(x[0], _rope_table(lp), tm, first, 1, "build_h")
    weight[("in", 0)] = first_in.reshape(-1, D_MODEL)
    cw = g_small[:, N_META:N_META + 6, 0:cshard].reshape(N_DEV, DEPTH, 3, cshard)
    cw = jnp.transpose(cw, (1, 2, 0, 3)).reshape(DEPTH, 3, CONV_W)
    conv_full = jnp.concatenate([cw, jnp.zeros((DEPTH, 5, CONV_W), F32)], axis=1)

    row1 = lambda a, l: a[l].reshape(1, -1)

    saved = []
    for l in range(DEPTH):
        a, qkv, bch = run(_in_proj_fwd, "in_proj_fwd", l, h, row1(mix_pre_g, l), weight[("in", l)], rope, tm)
        y_attn, probs, p_sink = run(_attn_fwd, "attn_fwd", l, qkv, row1(sinks, l))
        yc, y, z, h2 = run(_mix_out_fwd, "mix_out_fwd", l, bch, y_attn, h, conv_full[l], row1(attn_out_g, l),
                       row1(conv_out_g, l), weight[("out", l)], row1(mix_post_g, l), tm)
        mlp = _mlp_fwd if l < DEPTH - 1 else functools.partial(_mlp_fwd, target=loss_target[0])
        a2, up, f, *rest = run(mlp, "mlp_fwd", l, h2, row1(mlp_pre_g, l), weight[("up", l)], weight[("down", l)],
                               row1(mlp_post_g, l), tm_mlp)
        saved.append((h, a, qkv, bch, y_attn, probs, p_sink, yc, y, z, h2, a2, up, f))
        h = rest[0]
    dh, loss_part = rest[0], rest[1][0, 0] * (0.5 / D_MODEL)

    gsmall = [None] * DEPTH
    for l in reversed(range(DEPTH)):
        h0, a, qkv, bch, y_attn, probs, p_sink, yc, y, z, h2, a2, up, f = saved[l]
        df, dup, dh2, dg_mlp = run(_mlp_bwd_dx, "mlp_bwd_dx", l, dh, f, up, h2, weight[("down", l)], weight[("up", l)],
                                   row1(mlp_post_g, l), row1(mlp_pre_g, l), tm_mlp)
        grad[("down", l)], grad[("up", l)] = _mlp_bwd_dw(up, df, dup, a2, tm_dw_mlp, f"mlp_bwd_dw_{l}")
        dya, dbch, dg_mix, grad[("out", l)] = run(
            _mix_out_bwd, "mix_out_bwd", l, dh2, z, y_attn, yc, bch, y, weight[("out", l)], row1(mix_post_g, l),
            row1(attn_out_g, l), row1(conv_out_g, l), conv_full[l], tm)
        dq, dkv, dsink = run(_attn_bwd, "attn_bwd", l, qkv, y_attn, dya, probs, p_sink, rope)
        grad[("in", l)], = run(_mix_bwd_dw, "mix_bwd_dw", l, dq, dkv, dbch, a, tm_dw_mix)
        dh, dg_in = run(_in_proj_bwd_dx, "in_proj_bwd_dx", l, dq, dkv, dbch, weight[("in", l)], h0, dh2,
                        row1(mix_pre_g, l), tm)
        tile_a = dg_mlp + dg_in + jnp.pad(dsink, ((0, 0), (0, D_MODEL - 128)))
        gsmall[l] = (tile_a, dg_mix)
    grad_x = dh[BLOCK:][None]

    loss_tile = jnp.zeros((8, D_MODEL), F32).at[ROW_LOSS, 0].set(loss_part)
    tot = _sum_small(jnp.concatenate(
        [gsmall[0][0] + loss_tile, gsmall[0][1], gsmall[1][0], gsmall[1][1], dh[LEAD_PAD:BLOCK]], axis=0))
    loss = tot[ROW_LOSS, 0]
    ta = [tot[16 * l:16 * l + 8] for l in range(DEPTH)]
    tb = [tot[16 * l + 8:16 * l + 16] for l in range(DEPTH)]
    pick = lambda tiles, r0, r1, c0, c1: jnp.stack([t[r0:r1, c0:c1] for t in tiles])
    g_mlp_post = pick(ta, ROW_MLP_POST, ROW_MLP_POST + 1, 0, D_MODEL).reshape(DEPTH, D_MODEL)
    g_mlp_pre = pick(ta, ROW_MLP_PRE, ROW_MLP_PRE + 1, 0, D_MODEL).reshape(DEPTH, D_MODEL)
    g_mix_pre = pick(ta, ROW_MIX_PRE, ROW_MIX_PRE + 1, 0, D_MODEL).reshape(DEPTH, D_MODEL)
    g_sinks = pick(ta, ROW_SINK, ROW_SINK + 1, 0, N_Q_HEADS).reshape(DEPTH, N_Q_HEADS)
    g_mix_post = pick(tb, ROW_MIX_POST, ROW_MIX_POST + 1, 0, D_MODEL).reshape(DEPTH, D_MODEL)
    g_attn_out = pick(tb, ROW_GROUP_G, ROW_GROUP_G + 1, 0, ATTN_W).reshape(DEPTH, ATTN_W)
    g_conv_out = pick(tb, ROW_GROUP_G, ROW_GROUP_G + 1, ATTN_W, D_MODEL).reshape(DEPTH, CONV_W)
    g_conv_full = pick(tb, ROW_CONV, ROW_CONV + 3, 0, CONV_W)
    g_conv = lax.dynamic_slice_in_dim(g_conv_full, me * cshard, cshard, axis=2)
    g_meta = lax.dynamic_slice_in_dim(tot[16 * DEPTH:16 * DEPTH + N_META], me * mshard, mshard, axis=1)

    r_in, r_out, r_up, r_down = [[landed[(n, l)] for l in range(DEPTH)] for n in ("in", "out", "up", "down")]
    t12 = lambda a: jnp.swapaxes(a, 1, 2)
    g_w_in, d_w_in, nm_w_in, nv_w_in = map(t12, _sum_adamw(r_in, t12(w_in), t12(m_w_in), t12(v_w_in), 96, "adamw_w_in"))
    g_w_up, d_w_up, nm_w_up, nv_w_up = _sum_adamw(r_up, w_up, m_w_up, v_w_up, 128, "adamw_w_up", transposed=True)
    g_w_out, d_w_out, nm_w_out, nv_w_out = _sum_adamw(r_out, w_out, m_w_out, v_w_out, 128, "adamw_w_out")
    g_w_down, d_w_down, nm_w_down, nv_w_down = _sum_adamw(r_down, w_down, m_w_down, v_w_down, 128, "adamw_w_down")

    ws = [meta_tokens, mix_pre_g, conv_w.reshape(6, cshard), sinks, attn_out_g, conv_out_g, mix_post_g, mlp_pre_g, mlp_post_g]
    gs = [g_meta, g_mix_pre, g_conv.reshape(6, cshard), g_sinks, g_attn_out, g_conv_out, g_mix_post, g_mlp_pre, g_mlp_post]
    ms = [m_meta_tokens, m_mix_pre_g, m_conv_w.reshape(6, cshard), m_sinks, m_attn_out_g, m_conv_out_g, m_mix_post_g,
          m_mlp_pre_g, m_mlp_post_g]
    vs = [v_meta_tokens, v_mix_pre_g, v_conv_w.reshape(6, cshard), v_sinks, v_attn_out_g, v_conv_out_g, v_mix_post_g,
          v_mlp_pre_g, v_mlp_post_g]
    ds, nms, nvs = _adamw_small(ws, gs, ms, vs)

    def order(meta, mix_pre, cv, sk, a_out, c_out, mix_post, mlp_pre, mlp_post, win, wout, wup, wdown):
        return [meta, mix_pre, win, cv.reshape(DEPTH, 3, cshard), sk, a_out, c_out, wout, mix_post, mlp_pre, wup, wdown, mlp_post]

    grads = order(*gs, g_w_in, g_w_out, g_w_up, g_w_down)
    deltas = order(*ds, d_w_in, d_w_out, d_w_up, d_w_down)
    new_m = order(*nms, nm_w_in, nm_w_out, nm_w_up, nm_w_down)
    new_v = order(*nvs, nv_w_in, nv_w_out, nv_w_up, nv_w_down)
    return (loss, grad_x, *grads, *deltas, *new_m, *new_v)
```

```python
import functools
import math

import jax
import jax.numpy as jnp
from jax import lax
from jax.experimental import pallas as pl
from jax.experimental.pallas import tpu as pltpu

F32 = jnp.float32
BF = jnp.bfloat16

D_MODEL = 1024
ATTN_W = 512
CONV_W = 512
HEAD_DIM = 64
N_Q_HEADS = 8
ROT_DIM = 16
D_FF = 4096
IN_W = 2304
N_META = 16
BLOCK = 128
LEAD_PAD = BLOCK - N_META
ROPE_THETA = 500000.0
EPS = 1e-6
N_DEV = 8
DEPTH = 2
NEG = -1e30
SCALE = HEAD_DIM ** -0.5

ADAM_LR = 0.001
ADAM_B1 = 0.9
ADAM_B2 = 0.999
ADAM_EPS = 1e-08
ADAM_WD = 0.01
ADAM_STEP = 10

ROW_MLP_POST, ROW_MLP_PRE, ROW_MIX_PRE, ROW_SINK, ROW_LOSS = 0, 1, 2, 3, 4
ROW_MIX_POST, ROW_GROUP_G, ROW_CONV = 0, 1, 2

VMEM_LIMIT = 56 * 1024 * 1024
MESH = pl.DeviceIdType.MESH


def _dot(a, b):
    return jnp.dot(a, b, preferred_element_type=F32)


def _dot_nt(a, b):
    return lax.dot_general(a, b, (((1,), (1,)), ((), ())), preferred_element_type=F32)


def _dot_tn(a, b):
    return lax.dot_general(a, b, (((0,), (0,)), ((), ())), preferred_element_type=F32)


def _rms_fwd(x, g):
    r = lax.rsqrt(jnp.mean(x * x, axis=-1, keepdims=True) + EPS)
    return x * r * g


def _rms_bwd(x, g, dy):
    r = lax.rsqrt(jnp.mean(x * x, axis=-1, keepdims=True) + EPS)
    xh = x * r
    t = dy * g
    dx = r * (t - xh * jnp.mean(t * xh, axis=-1, keepdims=True))
    dg = jnp.sum(dy * xh, axis=0, keepdims=True)
    return dx, dg


def _row_tile(lp, cands=(640, 512, 384, 256, 128)):
    for t in cands:
        if lp % t == 0:
            return t
    raise ValueError(f"row count {lp} is not a multiple of 128")


def _full(shape):
    n = len(shape)
    return pl.BlockSpec(shape, lambda *_: (0,) * n, pipeline_mode=pl.Buffered(1))


def _full_out(shape):
    n = len(shape)
    return pl.BlockSpec(shape, lambda *_: (0,) * n)


def _params(sem=("arbitrary",)):
    return pltpu.CompilerParams(dimension_semantics=sem, vmem_limit_bytes=VMEM_LIMIT)


def _rope_table(lp):
    half = ROT_DIM // 2
    pos = jnp.maximum(jnp.arange(lp) - LEAD_PAD, 0).astype(F32)
    inv_freq = jnp.power(jnp.float32(ROPE_THETA), -jnp.arange(0, ROT_DIM, 2, dtype=F32) / ROT_DIM)
    ang_t = jnp.concatenate([inv_freq, inv_freq])[:, None] * pos[None, :]
    row = lax.broadcasted_iota(jnp.int32, (ROT_DIM, lp), 0)
    cs_t = jnp.where(row < half, jnp.cos(ang_t), jnp.sin(ang_t))
    return jnp.pad(cs_t.T, ((0, 0), (0, 128 - ROT_DIM)))


def _rope_coeffs(t):
    half = ROT_DIM // 2
    lane = lax.broadcasted_iota(jnp.int32, t.shape, 1)
    cos_a = jnp.where(lane < half, t, 0.0)
    sin_a = pltpu.roll(jnp.where((lane >= half) & (lane < ROT_DIM), t, 0.0), 128 - half, 1)
    c = cos_a + pltpu.roll(cos_a, half, 1) + jnp.where((lane >= ROT_DIM) & (lane < HEAD_DIM), 1.0, 0.0)
    s2 = pltpu.roll(sin_a, half, 1)
    both = lambda u: u + pltpu.roll(u, HEAD_DIM, 1)
    return both(c), both(-sin_a), both(s2)


def _rope(t, c, s1, s2):
    return t * c + pltpu.roll(t, BLOCK - 8, 1) * s1 + pltpu.roll(t, 8, 1) * s2


def _rope_t(dt, c, s1, s2):
    return dt * c + pltpu.roll(dt * s1, 8, 1) + pltpu.roll(dt * s2, BLOCK - 8, 1)


def _build_h(x, rope_compact, tm, exch, small_piece, name):
    seq = x.shape[0]
    lp = BLOCK + seq
    nt = lp // tm
    n_sub = tm // BLOCK
    small_shape = exch.land_shapes[small_piece].shape

    def body(*refs):
        h_ref, c_ref, s1_ref, s2_ref = refs[n_sub + 1:n_sub + 5]
        for j in range(n_sub):
            h_ref[j * BLOCK:(j + 1) * BLOCK, :] = refs[j][...]
        c_ref[...], s1_ref[...], s2_ref[...] = _rope_coeffs(refs[n_sub][...])

    def after(lands, *refs):
        h_ref, buf = refs[n_sub + 1], refs[n_sub + 5]
        pltpu.sync_copy(lands[small_piece], buf)
        h_ref[0:LEAD_PAD, :] = jnp.zeros((LEAD_PAD, D_MODEL), F32)
        for d in range(N_DEV):
            h_ref[LEAD_PAD:BLOCK, d * 128:(d + 1) * 128] = buf[d, 0:N_META, :]

    tile = lambda i: (i + 1) % nt
    piece = lambda j: pl.BlockSpec((BLOCK, D_MODEL), lambda i: (jnp.maximum(tile(i) * n_sub + j - 1, 0), 0))
    rows = lambda w: pl.BlockSpec((tm, w), lambda i: (tile(i), 0))
    (h, *rope), lands = _call(
        body, exch,
        name=name,
        grid=(nt,),
        in_specs=[piece(j) for j in range(n_sub)] + [rows(128)],
        out_specs=[rows(D_MODEL)] + [rows(128)] * 3,
        out_shape=[jax.ShapeDtypeStruct((lp, D_MODEL), F32)] + [jax.ShapeDtypeStruct((lp, 128), F32)] * 3,
        scratch_shapes=[pltpu.VMEM(small_shape, F32)],
        compiler_params=_params(),
        after=after,
    )(*([x] * n_sub), rope_compact)
    return h, rope, lands


def _in_proj_fwd(h, g, w_in_t, rope, tm, name, exch=None):
    lp = h.shape[0]

    def body(h_ref, g_ref, w_ref, c_ref, s1_ref, s2_ref, a_ref, qkv_ref, bch_ref):
        a = _rms_fwd(h_ref[...], g_ref[...]).astype(BF)
        a_ref[...] = a
        proj = _dot_nt(a, w_ref[...])
        c, s1, s2 = c_ref[...], s1_ref[...], s2_ref[...]
        for j in range(5):
            t = _rope(proj[:, j * 128:(j + 1) * 128], c, s1, s2)
            qkv_ref[:, j * 128:(j + 1) * 128] = (t * SCALE if j < 4 else t).astype(BF)
        qkv_ref[:, 640:768] = proj[:, 640:768].astype(BF)
        bch_ref[...] = proj[:, 768:].astype(BF)

    row = lambda w: pl.BlockSpec((tm, w), lambda i: (i, 0))
    return _call(
        body, exch,
        name=name,
        grid=(lp // tm,),
        in_specs=[row(D_MODEL), _full((1, D_MODEL)), _full((IN_W, D_MODEL)), row(128), row(128), row(128)],
        out_specs=[row(D_MODEL), row(768), row(3 * CONV_W)],
        out_shape=[
            jax.ShapeDtypeStruct((lp, D_MODEL), BF),
            jax.ShapeDtypeStruct((lp, 768), BF),
            jax.ShapeDtypeStruct((lp, 3 * CONV_W), BF),
        ],
        compiler_params=_params(),
    )(h, g, w_in_t, *rope)


def _fold_masks(i):
    r = lax.broadcasted_iota(jnp.int32, (2 * BLOCK, BLOCK), 0) & (BLOCK - 1)
    c = lax.broadcasted_iota(jnp.int32, (2 * BLOCK, BLOCK), 1)
    tri = c > r
    ok = jnp.where(tri, (i - 1) * BLOCK + c, i * BLOCK + c) >= LEAD_PAD
    return tri, ok


def _kv_operand(x, kvh):
    lane = lax.broadcasted_iota(jnp.int32, x.shape, 1)
    zero = jnp.zeros_like(x)
    if kvh == 0:
        lo = jnp.where(lane < HEAD_DIM, x, zero)
        hi = pltpu.roll(lo, HEAD_DIM, 1)
    else:
        hi = jnp.where(lane >= HEAD_DIM, x, zero)
        lo = pltpu.roll(hi, HEAD_DIM, 1)
    return jnp.concatenate([lo, hi], axis=0)


def _split4(t, tri):
    zero = jnp.zeros_like(t[0])
    return jnp.concatenate(
        [jnp.where(tri, t[0], zero), jnp.where(tri, zero, t[0]), jnp.where(tri, t[1], zero), jnp.where(tri, zero, t[1])], axis=1)


def _sink_cols(sink_ref, kvh):
    first = lax.broadcasted_iota(jnp.int32, (2 * BLOCK, 1), 0) < BLOCK
    return [jnp.where(first, sink_ref[0, 4 * kvh + half], sink_ref[0, 4 * kvh + 2 + half]) for half in range(2)]


def _folded_exp(q2, k4, tri, ok, sks):
    s = _dot_nt(q2, k4)
    es, ss = [], []
    for half in range(2):
        s_h = s[:, 2 * half * BLOCK:2 * (half + 1) * BLOCK]
        sf = jnp.where(ok, jnp.where(tri, s_h[:, :BLOCK], s_h[:, BLOCK:]), NEG)
        m = jnp.maximum(jnp.max(sf, axis=-1, keepdims=True), sks[half])
        es.append(jnp.exp(sf - m))
        ss.append(jnp.exp(sks[half] - m))
    sums = _dot(jnp.concatenate(es, axis=0).astype(BF), jnp.ones((BLOCK, BLOCK), BF))
    invs = [1.0 / (sums[2 * half * BLOCK:2 * (half + 1) * BLOCK] + ss[half]) for half in range(2)]
    return es, ss, invs


def _attn_fwd(qkv, sink, name, exch=None):
    lp = qkv.shape[0]
    nb = lp // BLOCK
    per_step = 2

    def one_block(i, sink_ref, q_ref, kvc_ref, kvp_ref, o_ref, p_ref, ps_ref):
        tri, ok = _fold_masks(i)
        kvc, kvp = kvc_ref[...], kvp_ref[...]
        kk = jnp.concatenate([kvp[:, :128], kvc[:, :128]], axis=0)
        vv = jnp.concatenate([kvp[:, 128:], kvc[:, 128:]], axis=0)
        lane = lax.broadcasted_iota(jnp.int32, (BLOCK, 128), 1)
        p_sink = jnp.zeros((BLOCK, 128), F32)
        for kvh in range(2):
            q2 = jnp.concatenate([q_ref[:, 256 * kvh:256 * kvh + 128], q_ref[:, 256 * kvh + 128:256 * kvh + 256]], axis=0)
            es, ss, invs = _folded_exp(q2, _kv_operand(kk, kvh), tri, ok, _sink_cols(sink_ref, kvh))
            pb = [(es[half] * invs[half]).astype(BF) for half in range(2)]
            out = _dot(_split4(pb, tri), _kv_operand(vv, kvh))
            for pair in range(2):
                rows = slice(pair * BLOCK, (pair + 1) * BLOCK)
                o_ref[:, 256 * kvh + 128 * pair:256 * kvh + 128 * (pair + 1)] = out[rows].astype(BF)
                for half in range(2):
                    head = 4 * kvh + 2 * pair + half
                    p_ref[:, 128 * head:128 * (head + 1)] = pb[half][rows]
                    p_sink = jnp.where(lane == head, (ss[half] * invs[half][:, 0:1])[rows], p_sink)
        ps_ref[...] = p_sink

    def body(sink_ref, *refs):
        q_refs, kv_refs = refs[:per_step], refs[per_step:2 * per_step + 1]
        o_ref, p_ref, ps_ref = refs[2 * per_step + 1:]
        for j in range(per_step):
            rows = slice(j * BLOCK, (j + 1) * BLOCK)
            one_block(per_step * pl.program_id(0) + j, sink_ref, q_refs[j], kv_refs[j + 1], kv_refs[j],
                      o_ref.at[rows], p_ref.at[rows], ps_ref.at[rows])

    last = nb - 1
    blk = lambda j: (lambda s: jnp.minimum(per_step * s + j, last))
    out_rows = lambda w: pl.BlockSpec((per_step * BLOCK, w), lambda s: (s, 0))
    return _call(
        body, exch,
        name=name,
        grid=(pl.cdiv(nb, per_step),),
        in_specs=[pl.BlockSpec(memory_space=pltpu.SMEM)]
        + [pl.BlockSpec((BLOCK, ATTN_W), lambda s, j=j: (blk(j)(s), 0)) for j in range(per_step)]
        + [pl.BlockSpec((BLOCK, 256), lambda s: (jnp.maximum(per_step * s - 1, 0), 2))]
        + [pl.BlockSpec((BLOCK, 256), lambda s, j=j: (blk(j)(s), 2)) for j in range(per_step)],
        out_specs=[out_rows(ATTN_W), out_rows(N_Q_HEADS * BLOCK), out_rows(128)],
        out_shape=[jax.ShapeDtypeStruct((lp, ATTN_W), BF), jax.ShapeDtypeStruct((lp, N_Q_HEADS * BLOCK), BF),
                   jax.ShapeDtypeStruct((lp, 128), F32)],
        compiler_params=_params(),
    )(sink, *([qkv] * (2 * per_step + 1)))


def _mix_out_fwd(bch, y_attn, h, conv_w, g_a, g_c, w_out, g_post, tm, name, exch=None):
    lp = h.shape[0]

    def body(bch_ref, ya_ref, h_ref, cw_ref, ga_ref, gc_ref, w_ref, gp_ref, yc_ref, y_ref, z_ref, h2_ref, ext):
        i = pl.program_id(0)

        @pl.when(i == 0)
        def _():
            ext[0:8, :] = jnp.zeros((8, CONV_W), F32)

        b = bch_ref[:, 0:CONV_W].astype(F32)
        u = bch_ref[:, CONV_W:2 * CONV_W].astype(F32) * bch_ref[:, 2 * CONV_W:3 * CONV_W].astype(F32)
        ext[8:8 + tm, :] = u
        yc = cw_ref[0:1, :] * ext[6:6 + tm, :] + cw_ref[1:2, :] * ext[7:7 + tm, :] + cw_ref[2:3, :] * u
        ext[0:8, :] = u[tm - 8:tm, :]
        yc_ref[...] = yc.astype(BF)
        ya = _rms_fwd(ya_ref[...].astype(F32), ga_ref[...]).astype(BF)
        yb = _rms_fwd(b * yc, gc_ref[...]).astype(BF)
        y_ref[:, 0:ATTN_W] = ya
        y_ref[:, ATTN_W:] = yb
        z = _dot(ya, w_ref[0:ATTN_W, :]) + _dot(yb, w_ref[ATTN_W:, :])
        z_ref[...] = z
        h2_ref[...] = h_ref[...] + _rms_fwd(z, gp_ref[...])

    row = lambda w: pl.BlockSpec((tm, w), lambda i: (i, 0))
    return _call(
        body, exch,
        name=name,
        grid=(lp // tm,),
        in_specs=[
            row(3 * CONV_W), row(ATTN_W), row(D_MODEL), _full((8, CONV_W)), _full((1, ATTN_W)), _full((1, CONV_W)),
            _full((D_MODEL, D_MODEL)), _full((1, D_MODEL)),
        ],
        out_specs=[row(CONV_W), row(D_MODEL), row(D_MODEL), row(D_MODEL)],
        out_shape=[
            jax.ShapeDtypeStruct((lp, CONV_W), BF),
            jax.ShapeDtypeStruct((lp, D_MODEL), BF),
            jax.ShapeDtypeStruct((lp, D_MODEL), F32),
            jax.ShapeDtypeStruct((lp, D_MODEL), F32),
        ],
        scratch_shapes=[pltpu.VMEM((tm + 8, CONV_W), F32)],
        compiler_params=_params(),
    )(bch, y_attn, h, conv_w, g_a, g_c, w_out, g_post)


def _mlp_fwd(h2, g_pre, w_up_t, w_down, g_post, tm, name, exch=None, target=None):
    lp = h2.shape[0]
    sub = math.gcd(tm, BLOCK)
    n_sub, lead = tm // sub, BLOCK // sub
    n_t = n_sub if target is not None else 0

    def body(*refs):
        h_ref, gp_ref, wu_ref, wd_ref, gq_ref = refs[:5]
        t_refs = refs[5:5 + n_t]
        a_ref, up_ref, f_ref, last_ref = refs[5 + n_t:9 + n_t]
        h = h_ref[...]
        a = _rms_fwd(h, gp_ref[...]).astype(BF)
        a_ref[...] = a
        up = _dot_nt(a, wu_ref[...])
        up_ref[...] = up.astype(BF)
        act = jnp.square(jnp.maximum(up, 0.0)).astype(BF)
        f = _dot(act, wd_ref[...])
        f_ref[...] = f
        h3 = h + _rms_fwd(f, gq_ref[...])
        if target is None:
            last_ref[...] = h3
            return
        ls_ref = refs[9 + n_t]
        i = pl.program_id(0)

        @pl.when(i == 0)
        def _():
            ls_ref[...] = jnp.zeros((8, 128), F32)

        sq = jnp.zeros((1, 1), F32)
        for j in range(n_sub):
            on_tokens = i * n_sub + j >= lead
            d = jnp.where(on_tokens, h3[j * sub:(j + 1) * sub] - t_refs[j][...], 0.0)
            last_ref[j * sub:(j + 1) * sub, :] = d * (1.0 / D_MODEL)
            sq = sq + jnp.sum(d * d)
        ls_ref[...] += sq

    row = lambda w: pl.BlockSpec((tm, w), lambda i: (i, 0))
    piece = lambda j: pl.BlockSpec((sub, D_MODEL), lambda i: (jnp.maximum(i * n_sub + j - lead, 0), 0))
    out_specs = [row(D_MODEL), row(D_FF), row(D_MODEL), row(D_MODEL)]
    out_shape = [
        jax.ShapeDtypeStruct((lp, D_MODEL), BF),
        jax.ShapeDtypeStruct((lp, D_FF), BF),
        jax.ShapeDtypeStruct((lp, D_MODEL), F32),
        jax.ShapeDtypeStruct((lp, D_MODEL), F32),
    ]
    if target is not None:
        out_specs.append(_full_out((8, 128)))
        out_shape.append(jax.ShapeDtypeStruct((8, 128), F32))
    return _call(
        body, exch,
        name=name,
        grid=(lp // tm,),
        in_specs=[row(D_MODEL), _full((1, D_MODEL)), _full((D_FF, D_MODEL)), _full((D_FF, D_MODEL)), _full((1, D_MODEL))]
        + [piece(j) for j in range(n_t)],
        out_specs=out_specs,
        out_shape=out_shape,
        compiler_params=_params(),
    )(h2, g_pre, w_up_t, w_down, g_post, *([target] * n_t))


def _mlp_bwd_dx(dh3, f, up, h2, w_down, w_up_t, g_post, g_pre, tm, name, exch=None):
    lp = h2.shape[0]

    def body(dh3_ref, f_ref, up_ref, h2_ref, wd_ref, wu_ref, gq_ref, gp_ref, df_ref, dup_ref, dh2_ref, dg_ref):
        i = pl.program_id(0)

        @pl.when(i == 0)
        def _():
            dg_ref[...] = jnp.zeros((8, D_MODEL), F32)

        dh3 = dh3_ref[...]
        df, dgq = _rms_bwd(f_ref[...], gq_ref[...], dh3)
        dg_ref[ROW_MLP_POST:ROW_MLP_POST + 1, :] += dgq
        df = df.astype(BF)
        df_ref[...] = df
        dact = _dot_nt(df, wd_ref[...])
        dup = (dact * (2.0 * jnp.maximum(up_ref[...].astype(F32), 0.0))).astype(BF)
        dup_ref[...] = dup
        da = _dot(dup, wu_ref[...])
        dh, dgp = _rms_bwd(h2_ref[...], gp_ref[...], da)
        dg_ref[ROW_MLP_PRE:ROW_MLP_PRE + 1, :] += dgp
        dh2_ref[...] = dh3 + dh

    row = lambda w: pl.BlockSpec((tm, w), lambda i: (i, 0))
    return _call(
        body, exch,
        name=name,
        grid=(lp // tm,),
        in_specs=[
            row(D_MODEL), row(D_MODEL), row(D_FF), row(D_MODEL), _full((D_FF, D_MODEL)), _full((D_FF, D_MODEL)),
            _full((1, D_MODEL)), _full((1, D_MODEL)),
        ],
        out_specs=[row(D_MODEL), row(D_FF), row(D_MODEL), _full_out((8, D_MODEL))],
        out_shape=[
            jax.ShapeDtypeStruct((lp, D_MODEL), BF),
            jax.ShapeDtypeStruct((lp, D_FF), BF),
            jax.ShapeDtypeStruct((lp, D_MODEL), F32),
            jax.ShapeDtypeStruct((8, D_MODEL), F32),
        ],
        compiler_params=_params(),
    )(dh3, f, up, h2, w_down, w_up_t, g_post, g_pre)


def _mlp_bwd_dw(up, df, dup, a2, tm, name):
    lp = up.shape[0]
    nt = lp // tm
    nj = D_FF // D_MODEL

    def body(up_ref, df_ref, dup_ref, a_ref, dwd_ref, dwu_ref, accd, accu):
        i = pl.program_id(1)

        @pl.when(i == 0)
        def _():
            accd[...] = jnp.zeros_like(accd)
            accu[...] = jnp.zeros_like(accu)

        act = jnp.square(jnp.maximum(up_ref[...].astype(F32), 0.0)).astype(BF)
        accd[...] += _dot_tn(act, df_ref[...])
        accu[...] += _dot_tn(dup_ref[...], a_ref[...])

        @pl.when(i == nt - 1)
        def _():
            dwd_ref[...] = accd[...].astype(BF)
            dwu_ref[...] = accu[...].astype(BF)

    return pl.pallas_call(
        body,
        name=name,
        grid=(nj, nt),
        in_specs=[
            pl.BlockSpec((tm, D_MODEL), lambda j, i: (i, j)),
            pl.BlockSpec((tm, D_MODEL), lambda j, i: (i, 0)),
            pl.BlockSpec((tm, D_MODEL), lambda j, i: (i, j)),
            pl.BlockSpec((tm, D_MODEL), lambda j, i: (i, 0)),
        ],
        out_specs=[pl.BlockSpec((D_MODEL, D_MODEL), lambda j, i: (j, 0)), pl.BlockSpec((D_MODEL, D_MODEL), lambda j, i: (j, 0))],
        out_shape=[jax.ShapeDtypeStruct((D_FF, D_MODEL), BF), jax.ShapeDtypeStruct((D_FF, D_MODEL), BF)],
        scratch_shapes=[pltpu.VMEM((D_MODEL, D_MODEL), F32), pltpu.VMEM((D_MODEL, D_MODEL), F32)],
        compiler_params=_params(("arbitrary", "arbitrary")),
    )(up, df, dup, a2)


def _mix_out_bwd(dh2, z, y_attn, yc, bch, y, w_out, g_post, g_a, g_c, conv_w, tm, name, exch=None):
    lp = dh2.shape[0]
    nt = lp // tm

    def body(dh2_ref, z_ref, ya_ref, yc_ref, bch_ref, y_ref, w_ref, gp_ref, ga_ref, gc_ref, cw_ref,
             dya_ref, dbch_ref, dg_ref, dwo_ref, ext, acco):
        i = pl.program_id(0)
        dcw_ref = dg_ref.at[ROW_CONV:ROW_CONV + 3, 0:CONV_W]

        @pl.when(i == 0)
        def _():
            ext[tm:tm + 8, :] = jnp.zeros((8, CONV_W), F32)
            dg_ref[...] = jnp.zeros((8, D_MODEL), F32)
            acco[...] = jnp.zeros_like(acco)

        dz, dgp = _rms_bwd(z_ref[...], gp_ref[...], dh2_ref[...])
        dg_ref[ROW_MIX_POST:ROW_MIX_POST + 1, :] += dgp
        dz = dz.astype(BF)
        acco[...] += _dot_tn(y_ref[...], dz)
        dya_n = _dot_nt(dz, w_ref[0:ATTN_W, :])
        dyb_n = _dot_nt(dz, w_ref[ATTN_W:, :])
        dya, dga = _rms_bwd(ya_ref[...].astype(F32), ga_ref[...], dya_n)
        dg_ref[ROW_GROUP_G:ROW_GROUP_G + 1, 0:ATTN_W] += dga
        dya_ref[...] = dya
        b = bch_ref[:, 0:CONV_W].astype(F32)
        c = bch_ref[:, CONV_W:2 * CONV_W].astype(F32)
        hc = bch_ref[:, 2 * CONV_W:3 * CONV_W].astype(F32)
        u = c * hc
        yc_v = yc_ref[...].astype(F32)
        dyconv, dgc = _rms_bwd(b * yc_v, gc_ref[...], dyb_n)
        dg_ref[ROW_GROUP_G:ROW_GROUP_G + 1, ATTN_W:] += dgc
        dbch_ref[:, 0:CONV_W] = (dyconv * yc_v).astype(BF)
        dyc = dyconv * b
        ext[0:tm, :] = dyc
        d1 = ext[1:1 + tm, :]
        d2 = ext[2:2 + tm, :]
        du = cw_ref[2:3, :] * dyc + cw_ref[1:2, :] * d1 + cw_ref[0:1, :] * d2
        ext[tm:tm + 8, :] = dyc[0:8, :]
        dbch_ref[:, CONV_W:2 * CONV_W] = (du * hc).astype(BF)
        dbch_ref[:, 2 * CONV_W:3 * CONV_W] = (du * c).astype(BF)
        dcw_ref[0:1, :] += jnp.sum(u * d2, axis=0, keepdims=True)
        dcw_ref[1:2, :] += jnp.sum(u * d1, axis=0, keepdims=True)
        dcw_ref[2:3, :] += jnp.sum(u * dyc, axis=0, keepdims=True)

        @pl.when(i == nt - 1)
        def _():
            dwo_ref[...] = acco[...].astype(BF)

    row = lambda w: pl.BlockSpec((tm, w), lambda i: (nt - 1 - i, 0))
    return _call(
        body, exch,
        name=name,
        grid=(nt,),
        in_specs=[
            row(D_MODEL), row(D_MODEL), row(ATTN_W), row(CONV_W), row(3 * CONV_W), row(D_MODEL), _full((D_MODEL, D_MODEL)),
            _full((1, D_MODEL)), _full((1, ATTN_W)), _full((1, CONV_W)), _full((8, CONV_W)),
        ],
        out_specs=[row(ATTN_W), row(3 * CONV_W), _full_out((8, D_MODEL)), _full_out((D_MODEL, D_MODEL))],
        out_shape=[
            jax.ShapeDtypeStruct((lp, ATTN_W), F32),
            jax.ShapeDtypeStruct((lp, 3 * CONV_W), BF),
            jax.ShapeDtypeStruct((8, D_MODEL), F32),
            jax.ShapeDtypeStruct((D_MODEL, D_MODEL), BF),
        ],
        scratch_shapes=[pltpu.VMEM((tm + 8, CONV_W), F32), pltpu.VMEM((D_MODEL, D_MODEL), F32)],
        compiler_params=_params(),
    )(dh2, z, y_attn, yc, bch, y, w_out, g_post, g_a, g_c, conv_w)


def _attn_bwd(qkv, o, do, probs, p_sink, rope, name, exch=None):
    lp = qkv.shape[0]
    nb = lp // BLOCK

    def body(q_ref, kvc_ref, kvp_ref, o_ref, do_ref, p_ref, ps_ref, cq_ref, s1q_ref, s2q_ref, ck_ref, s1k_ref, s2k_ref,
             dq_ref, dkv_ref, dsink_ref, carry):
        i = pl.program_id(0)

        @pl.when(i == 0)
        def _():
            carry[...] = jnp.zeros_like(carry)
            dsink_ref[...] = jnp.zeros((8, 128), F32)

        def finish(tot):
            dk = _rope_t(tot[:, :128], ck_ref[...], s1k_ref[...], s2k_ref[...])
            dkv_ref[:, 0:128] = dk.astype(BF)
            dkv_ref[:, 128:256] = tot[:, 128:].astype(BF)

        @pl.when(i < nb)
        def _():
            tri, _ = _fold_masks(i)
            kvc, kvp = kvc_ref[...], kvp_ref[...]
            kk = jnp.concatenate([kvp[:, :128], kvc[:, :128]], axis=0)
            vv = jnp.concatenate([kvp[:, 128:], kvc[:, 128:]], axis=0)
            lane = lax.broadcasted_iota(jnp.int32, (BLOCK, 128), 1)
            lane2 = lax.broadcasted_iota(jnp.int32, (2 * BLOCK, 128), 1)
            rope_q = (cq_ref[...], s1q_ref[...], s2q_ref[...])
            deltas = jnp.zeros((BLOCK, 128), F32)
            folded = []
            for kvh in range(2):
                c0 = 256 * kvh
                q2 = jnp.concatenate([q_ref[:, c0:c0 + 128], q_ref[:, c0 + 128:c0 + 256]], axis=0)
                do2 = jnp.concatenate([do_ref[:, c0:c0 + 128], do_ref[:, c0 + 128:c0 + 256]], axis=0)
                o2 = jnp.concatenate([o_ref[:, c0:c0 + 128], o_ref[:, c0 + 128:c0 + 256]], axis=0).astype(F32)
                k4, v4 = _kv_operand(kk, kvh), _kv_operand(vv, kvh)
                prod = do2 * o2
                dob = do2.astype(BF)
                dp = _dot_nt(dob, v4)
                ds, pb = [], []
                for half in range(2):
                    heads = [4 * kvh + 2 * pair + half for pair in range(2)]
                    p = jnp.concatenate([p_ref[:, 128 * h:128 * (h + 1)] for h in heads], axis=0)
                    sel = (lane2 < HEAD_DIM) if half == 0 else (lane2 >= HEAD_DIM)
                    delta = jnp.sum(jnp.where(sel, prod, 0.0), axis=-1, keepdims=True)
                    dp_h = dp[:, 2 * half * BLOCK:2 * (half + 1) * BLOCK]
                    ds.append((p.astype(F32) * (jnp.where(tri, dp_h[:, :BLOCK], dp_h[:, BLOCK:]) - delta)).astype(BF))
                    pb.append(p)
                    for pair in range(2):
                        deltas = jnp.where(lane == heads[pair], delta[pair * BLOCK:(pair + 1) * BLOCK], deltas)
                ds4, p4 = _split4(ds, tri), _split4(pb, tri)
                dq2 = _dot(ds4, k4) * SCALE
                dq_ref[:, c0:c0 + 128] = _rope_t(dq2[:BLOCK], *rope_q).astype(BF)
                dq_ref[:, c0 + 128:c0 + 256] = _rope_t(dq2[BLOCK:], *rope_q).astype(BF)
                rk, rv = _dot_tn(ds4, q2), _dot_tn(p4, dob)
                own = (lane < HEAD_DIM) if kvh == 0 else (lane >= HEAD_DIM)
                group = []
                for r in (rk, rv):
                    for blk in range(2):
                        t = jnp.where(lane < HEAD_DIM, r[blk * BLOCK:(blk + 1) * BLOCK], r[(2 + blk) * BLOCK:(3 + blk) * BLOCK])
                        group.append(jnp.where(own, t + pltpu.roll(t, HEAD_DIM, 1), 0.0))
                folded.append(group)
            dsink_ref[ROW_SINK:ROW_SINK + 1, :] -= jnp.sum(ps_ref[...] * deltas, axis=0, keepdims=True)
            dk_p, dk_c, dv_p, dv_c = [folded[0][t] + folded[1][t] for t in range(4)]
            finish(carry[...] + jnp.concatenate([dk_p, dv_p], axis=1))
            carry[...] = jnp.concatenate([dk_c, dv_c], axis=1)

        @pl.when(i == nb)
        def _():
            finish(carry[...])

    qi = lambda i: jnp.minimum(i, nb - 1)
    ki = lambda i: jnp.maximum(i - 1, 0)
    tab_q = pl.BlockSpec((BLOCK, 128), lambda i: (qi(i), 0))
    tab_k = pl.BlockSpec((BLOCK, 128), lambda i: (ki(i), 0))
    return _call(
        body, exch,
        name=name,
        grid=(nb + 1,),
        in_specs=[
            pl.BlockSpec((BLOCK, ATTN_W), lambda i: (qi(i), 0)),
            pl.BlockSpec((BLOCK, 256), lambda i: (qi(i), 2)),
            pl.BlockSpec((BLOCK, 256), lambda i: (jnp.maximum(qi(i) - 1, 0), 2)),
            pl.BlockSpec((BLOCK, ATTN_W), lambda i: (qi(i), 0)),
            pl.BlockSpec((BLOCK, ATTN_W), lambda i: (qi(i), 0)),
            pl.BlockSpec((BLOCK, N_Q_HEADS * BLOCK), lambda i: (qi(i), 0)),
            tab_q, tab_q, tab_q, tab_q, tab_k, tab_k, tab_k,
        ],
        out_specs=[
            pl.BlockSpec((BLOCK, ATTN_W), lambda i: (qi(i), 0)),
            pl.BlockSpec((BLOCK, 256), lambda i: (ki(i), 0)),
            pl.BlockSpec((8, 128), lambda i: (0, 0)),
        ],
        out_shape=[
            jax.ShapeDtypeStruct((lp, ATTN_W), BF),
            jax.ShapeDtypeStruct((lp, 256), BF),
            jax.ShapeDtypeStruct((8, 128), F32),
        ],
        scratch_shapes=[pltpu.VMEM((BLOCK, 256), F32)],
        compiler_params=_params(),
    )(qkv, qkv, qkv, o, do, probs, p_sink, *rope, *rope)


def _in_proj_bwd_dx(dq, dkv, dbch, w_in_t, h, dh2, g, tm, name, exch=None):
    lp = h.shape[0]

    def body(dq_ref, dkv_ref, dbch_ref, w_ref, h_ref, dh2_ref, g_ref, dh_ref, dg_ref):
        i = pl.program_id(0)

        @pl.when(i == 0)
        def _():
            dg_ref[...] = jnp.zeros((8, D_MODEL), F32)

        da = _dot(jnp.concatenate([dq_ref[...], dkv_ref[...], dbch_ref[...]], axis=1), w_ref[...])
        dh, dg = _rms_bwd(h_ref[...], g_ref[...], da)
        dg_ref[ROW_MIX_PRE:ROW_MIX_PRE + 1, :] += dg
        dh_ref[...] = dh2_ref[...] + dh

    row = lambda w: pl.BlockSpec((tm, w), lambda i: (i, 0))
    return _call(
        body, exch,
        name=name,
        grid=(lp // tm,),
        in_specs=[row(ATTN_W), row(256), row(3 * CONV_W), _full((IN_W, D_MODEL)), row(D_MODEL), row(D_MODEL), _full((1, D_MODEL))],
        out_specs=[row(D_MODEL), _full_out((8, D_MODEL))],
        out_shape=[jax.ShapeDtypeStruct((lp, D_MODEL), F32), jax.ShapeDtypeStruct((8, D_MODEL), F32)],
        compiler_params=_params(),
    )(dq, dkv, dbch, w_in_t, h, dh2, g)


def _mix_bwd_dw(dq, dkv, dbch, a, tm, name, exch=None):
    lp = a.shape[0]
    nt = lp // tm

    def body(dq_ref, dkv_ref, dbch_ref, a_ref, dwi_ref, acci):
        i = pl.program_id(0)

        @pl.when(i == 0)
        def _():
            acci[...] = jnp.zeros_like(acci)

        a_v = a_ref[...]
        acci[0:512, :] += _dot_tn(dq_ref[...], a_v)
        acci[512:768, :] += _dot_tn(dkv_ref[...], a_v)
        acci[768:, :] += _dot_tn(dbch_ref[...], a_v)

        @pl.when(i == nt - 1)
        def _():
            dwi_ref[...] = acci[...].astype(BF)

    row = lambda w: pl.BlockSpec((tm, w), lambda i: (i, 0))
    return _call(
        body, exch,
        name=name,
        grid=(nt,),
        in_specs=[row(ATTN_W), row(256), row(3 * CONV_W), row(D_MODEL)],
        out_specs=[_full_out((IN_W, D_MODEL))],
        out_shape=[jax.ShapeDtypeStruct((IN_W, D_MODEL), BF)],
        scratch_shapes=[pltpu.VMEM((IN_W, D_MODEL), F32)],
        compiler_params=_params(),
    )(dq, dkv, dbch, a)


def _mesh_place():
    x, y, c = lax.axis_index("x"), lax.axis_index("y"), lax.axis_index("c")
    return x, y, c, 4 * x + 2 * y + c


def _peer(x, y, c, k):
    px = 1 - x if k & 4 else x
    py = 1 - y if k & 2 else y
    pc = 1 - c if k & 1 else c
    return (px, py, pc), 4 * px + 2 * py + pc


SIBLING = 1
SAME_CORE = (2, 4, 6)
OTHER_CORE = (3, 5, 7)


class _Exchange:
    def __init__(self, pieces):
        self.srcs = [s for s, _ in pieces]
        self.to_all = [g for _, g in pieces]
        self.n = len(pieces)
        self.land_shapes = [
            jax.ShapeDtypeStruct((N_DEV,) + (s.shape if g else s.shape[1:]), s.dtype) for s, g in pieces]
        self.sem_shapes = [pltpu.SemaphoreType.DMA((self.n, N_DEV - 1)), pltpu.SemaphoreType.DMA((self.n, N_DEV - 1)),
                           pltpu.SemaphoreType.DMA((self.n,))]
        self.forwards = any(self.to_all)

    def _ops(self, srcs, lands, sems):
        send_sems, recv_sems, local_sems = sems
        x, y, c, me = _mesh_place()

        def remote(p, k, src, slot, to):
            return pltpu.make_async_remote_copy(
                src_ref=src, dst_ref=lands[p].at[slot], send_sem=send_sems.at[p, k - 1], recv_sem=recv_sems.at[p, k - 1],
                device_id=to, device_id_type=MESH)

        def own(p):
            return pltpu.make_async_copy(srcs[p] if self.to_all[p] else srcs[p].at[me], lands[p].at[me], local_sems.at[p])

        def direct(p, k):
            peer, pidx = _peer(x, y, c, k)
            return remote(p, k, srcs[p] if self.to_all[p] else srcs[p].at[pidx], me, peer)

        def forward(p, k):
            sibling, _ = _peer(x, y, c, SIBLING)
            _, origin = _peer(x, y, c, k ^ SIBLING)
            return remote(p, k, lands[p].at[origin], origin, sibling)

        def arrival(p, k):
            peer, pidx = _peer(x, y, c, k)
            return remote(p, k, lands[p].at[pidx], pidx, peer)

        return own, direct, forward, arrival

    def start(self, srcs, lands, sems):
        own, direct, _, _ = self._ops(srcs, lands, sems)
        for p in range(self.n):
            own(p).start()
            for k in ((SIBLING,) + SAME_CORE) if self.to_all[p] else range(1, N_DEV):
                direct(p, k).start()

    def forward(self, srcs, lands, sems):
        _, _, forward, arrival = self._ops(srcs, lands, sems)
        for p in range(self.n):
            if self.to_all[p]:
                for k in SAME_CORE:
                    arrival(p, k).wait_recv()
                    forward(p, k ^ SIBLING).start()

    def finish(self, srcs, lands, sems):
        own, direct, forward, arrival = self._ops(srcs, lands, sems)
        for p in range(self.n):
            for k in ((SIBLING,) + OTHER_CORE) if self.to_all[p] else range(1, N_DEV):
                arrival(p, k).wait_recv()
        for p in range(self.n):
            for k in range(1, N_DEV):
                (forward(p, k) if self.to_all[p] and k in OTHER_CORE else direct(p, k)).wait_send()
            own(p).wait()


def _call(body, exch, *, name, grid, in_specs, out_specs, out_shape, scratch_shapes=(), compiler_params, after=None):
    if exch is None:
        return pl.pallas_call(body, name=name, grid=grid, in_specs=in_specs, out_specs=out_specs, out_shape=out_shape,
                              scratch_shapes=scratch_shapes, compiler_params=compiler_params)
    n_in, n_out, n_scr, n_x = len(in_specs), len(out_shape), len(scratch_shapes), exch.n
    steps = math.prod(grid)

    def carrying(*refs):
        a, b, c, d, e = n_in, n_in + n_x, n_in + n_x + n_out, n_in + 2 * n_x + n_out, n_in + 2 * n_x + n_out + n_scr
        ins, srcs, outs, lands, scr, sems = refs[:a], refs[a:b], refs[b:c], refs[c:d], refs[d:e], refs[e:]
        step = functools.reduce(lambda acc, t: acc * grid[t] + pl.program_id(t), range(len(grid)), 0)

        @pl.when(step == 0)
        def _():
            exch.start(srcs, lands, sems)

        body(*ins, *outs, *scr)

        if exch.forwards:
            @pl.when(step == max(0, steps - 1 - (steps + 7) // 8))
            def _():
                exch.forward(srcs, lands, sems)

        @pl.when(step == steps - 1)
        def _():
            exch.finish(srcs, lands, sems)
            if after is not None:
                after(lands, *ins, *outs, *scr)

    hbm = pl.BlockSpec(memory_space=pl.ANY)
    call = pl.pallas_call(
        carrying, name=name, grid=grid, in_specs=list(in_specs) + [hbm] * n_x, out_specs=list(out_specs) + [hbm] * n_x,
        out_shape=list(out_shape) + exch.land_shapes, scratch_shapes=list(scratch_shapes) + exch.sem_shapes,
        compiler_params=compiler_params)

    def run(*args):
        res = call(*args, *exch.srcs)
        return list(res[:n_out]), list(res[n_out:])

    return run


def _sum_small(part):
    exch = _Exchange([(part, True)])

    def body(part_ref, out_ref, land, *sems):
        exch.start([part_ref], [land], sems)
        exch.forward([part_ref], [land], sems)
        exch.finish([part_ref], [land], sems)
        acc = land[0]
        for d in range(1, N_DEV):
            acc = acc + land[d]
        out_ref[...] = acc

    vmem = pl.BlockSpec(memory_space=pltpu.VMEM)
    return pl.pallas_call(
        body,
        name="sum_small",
        in_specs=[vmem],
        out_specs=vmem,
        out_shape=jax.ShapeDtypeStruct(part.shape, F32),
        scratch_shapes=[pltpu.VMEM(exch.land_shapes[0].shape, F32)] + exch.sem_shapes,
    )(part)


def _adamw(w, g, m, v):
    m = ADAM_B1 * m + (1.0 - ADAM_B1) * g
    v = ADAM_B2 * v + (1.0 - ADAM_B2) * jnp.square(g)
    m_hat = m / (1.0 - ADAM_B1 ** ADAM_STEP)
    v_hat = v / (1.0 - ADAM_B2 ** ADAM_STEP)
    delta = -ADAM_LR * (m_hat / (jnp.sqrt(v_hat) + ADAM_EPS) + ADAM_WD * w)
    return delta, m, v


def _landed_specs(tr, wd):
    return [pl.BlockSpec((N_DEV, tr, wd), lambda l, i, ll=ll: (0, jnp.where(l == ll, i, 0), 0)) for ll in range(DEPTH)]


def _device_sum(r_ref):
    acc = r_ref[0].astype(F32)
    for d in range(1, N_DEV):
        acc = acc + r_ref[d].astype(F32)
    return acc


def _sum_adamw(recv, w, m, v, tr, name, transposed=False):
    _, r, wd = recv[0].shape

    def body(*refs):
        w_ref, m_ref, v_ref, g_ref, d_ref, mo_ref, vo_ref = refs[DEPTH:]
        for ll in range(DEPTH):
            @pl.when(pl.program_id(0) == ll)
            def _(ll=ll):
                g = _device_sum(refs[ll])
                g = g.T if transposed else g
                g_ref[0] = g
                d_ref[0], mo_ref[0], vo_ref[0] = _adamw(w_ref[0], g, m_ref[0], v_ref[0])

    if transposed:
        blk = pl.BlockSpec((1, wd, tr), lambda l, i: (l, 0, i))
        shape = jax.ShapeDtypeStruct((DEPTH, wd, r), F32)
    else:
        blk = pl.BlockSpec((1, tr, wd), lambda l, i: (l, i, 0))
        shape = jax.ShapeDtypeStruct((DEPTH, r, wd), F32)
    return pl.pallas_call(
        body,
        name=name,
        grid=(DEPTH, r // tr),
        in_specs=_landed_specs(tr, wd) + [blk, blk, blk],
        out_specs=[blk] * 4,
        out_shape=[shape] * 4,
        compiler_params=_params(("arbitrary", "arbitrary")),
    )(*recv, w, m, v)


def _adamw_small(ws, gs, ms, vs):
    n = len(ws)

    def body(*refs):
        w_r, g_r, m_r, v_r = refs[:n], refs[n:2 * n], refs[2 * n:3 * n], refs[3 * n:4 * n]
        d_o, m_o, v_o = refs[4 * n:5 * n], refs[5 * n:6 * n], refs[6 * n:7 * n]
        for t in range(n):
            d_o[t][...], m_o[t][...], v_o[t][...] = _adamw(w_r[t][...], g_r[t][...], m_r[t][...], v_r[t][...])

    vmem = pl.BlockSpec(memory_space=pltpu.VMEM)
    shapes = [jax.ShapeDtypeStruct(w.shape, F32) for w in ws]
    outs = pl.pallas_call(
        body,
        name="adamw_small",
        in_specs=[vmem] * (4 * n),
        out_specs=[vmem] * (3 * n),
        out_shape=shapes * 3,
    )(*ws, *gs, *ms, *vs)
    return outs[:n], outs[n:2 * n], outs[2 * n:]


def kernel(x, meta_tokens, mix_pre_g, w_in, conv_w, sinks, attn_out_g, conv_out_g, w_out, mix_post_g, mlp_pre_g, w_up, w_down, mlp_post_g, loss_target, m_meta_tokens, m_mix_pre_g, m_w_in, m_conv_w, m_sinks, m_attn_out_g, m_conv_out_g, m_w_out, m_mix_post_g, m_mlp_pre_g, m_w_up, m_w_down, m_mlp_post_g, v_meta_tokens, v_mix_pre_g, v_w_in, v_conv_w, v_sinks, v_attn_out_g, v_conv_out_g, v_w_out, v_mix_post_g, v_mlp_pre_g, v_w_up, v_w_down, v_mlp_post_g):
    seq = x.shape[1]
    lp = BLOCK + seq
    tm = _row_tile(lp)
    tm_mlp = _row_tile(lp, (320, 256, 128))
    tm_dw_mlp = _row_tile(lp, (1664, 1040, 640, 384, 256, 128))
    tm_dw_mix = _row_tile(lp, (1664, 832, 640, 384, 256, 128))
    me = 4 * lax.axis_index("x") + 2 * lax.axis_index("y") + lax.axis_index("c")
    cshard = CONV_W // N_DEV
    mshard = D_MODEL // N_DEV

    gather_with = {
        ("in_proj_fwd", 0): [("in", 1)], ("attn_fwd", 0): [("out", 0), ("up", 0)], ("mix_out_fwd", 0): [("down", 0)],
        ("mlp_fwd", 0): [("out", 1), ("up", 1), ("down", 1)],
    }
    scatter_with = {
        ("attn_bwd", 1): [("down", 1)], ("mix_bwd_dw", 1): [("out", 1)], ("mlp_bwd_dx", 0): [("up", 1), ("in", 1)],
        ("mix_out_bwd", 0): [("up", 0)], ("attn_bwd", 0): [("down", 0)], ("mix_bwd_dw", 0): [("out", 0)],
        ("in_proj_bwd_dx", 0): [("in", 0)],
    }
    shard = {"in": jnp.swapaxes(w_in, 1, 2).astype(BF), "out": w_out.astype(BF),
             "up": jnp.swapaxes(w_up, 1, 2).astype(BF), "down": w_down.astype(BF)}
    weight = {}
    grad = {}
    landed = {}

    def run(fn, kind, l, *args):
        key, name = (kind, l), f"{kind}_{l}"
        if key in gather_with:
            blocks = gather_with[key]
            outs, lands = fn(*args, name, _Exchange([(shard[n][k], True) for n, k in blocks]))
            for b, land in zip(blocks, lands):
                weight[b] = land.reshape(-1, D_MODEL)
            return outs
        if key in scatter_with:
            blocks = scatter_with[key]
            outs, lands = fn(*args, name, _Exchange([(grad[b].reshape(N_DEV, -1, D_MODEL), False) for b in blocks]))
            landed.update(zip(blocks, lands))
            return outs
        return fn(*args, name)

    small = jnp.zeros((24, 128), F32)
    small = small.at[0:N_META, :].set(meta_tokens)
    small = small.at[N_META:N_META + 6, 0:cshard].set(conv_w.reshape(6, cshard))
    first = _Exchange([(shard["in"][0], True), (small, True)])
    h, rope, (first_in, g_small) = _build_h(x[0], _rope_table(lp), tm, first, 1, "build_h")
    weight[("in", 0)] = first_in.reshape(-1, D_MODEL)
    cw = g_small[:, N_META:N_META + 6, 0:cshard].reshape(N_DEV, DEPTH, 3, cshard)
    cw = jnp.transpose(cw, (1, 2, 0, 3)).reshape(DEPTH, 3, CONV_W)
    conv_full = jnp.concatenate([cw, jnp.zeros((DEPTH, 5, CONV_W), F32)], axis=1)

    row1 = lambda a, l: a[l].reshape(1, -1)

    saved = []
    for l in range(DEPTH):
        a, qkv, bch = run(_in_proj_fwd, "in_proj_fwd", l, h, row1(mix_pre_g, l), weight[("in", l)], rope, tm)
        y_attn, probs, p_sink = run(_attn_fwd, "attn_fwd", l, qkv, row1(sinks, l))
        yc, y, z, h2 = run(_mix_out_fwd, "mix_out_fwd", l, bch, y_attn, h, conv_full[l], row1(attn_out_g, l),
                       row1(conv_out_g, l), weight[("out", l)], row1(mix_post_g, l), tm)
        mlp = _mlp_fwd if l < DEPTH - 1 else functools.partial(_mlp_fwd, target=loss_target[0])
        a2, up, f, *rest = run(mlp, "mlp_fwd", l, h2, row1(mlp_pre_g, l), weight[("up", l)], weight[("down", l)],
                               row1(mlp_post_g, l), tm_mlp)
        saved.append((h, a, qkv, bch, y_attn, probs, p_sink, yc, y, z, h2, a2, up, f))
        h = rest[0]
    dh, loss_part = rest[0], rest[1][0, 0] * (0.5 / D_MODEL)

    gsmall = [None] * DEPTH
    for l in reversed(range(DEPTH)):
        h0, a, qkv, bch, y_attn, probs, p_sink, yc, y, z, h2, a2, up, f = saved[l]
        df, dup, dh2, dg_mlp = run(_mlp_bwd_dx, "mlp_bwd_dx", l, dh, f, up, h2, weight[("down", l)], weight[("up", l)],
                                   row1(mlp_post_g, l), row1(mlp_pre_g, l), tm_mlp)
        grad[("down", l)], grad[("up", l)] = _mlp_bwd_dw(up, df, dup, a2, tm_dw_mlp, f"mlp_bwd_dw_{l}")
        dya, dbch, dg_mix, grad[("out", l)] = run(
            _mix_out_bwd, "mix_out_bwd", l, dh2, z, y_attn, yc, bch, y, weight[("out", l)], row1(mix_post_g, l),
            row1(attn_out_g, l), row1(conv_out_g, l), conv_full[l], tm)
        dq, dkv, dsink = run(_attn_bwd, "attn_bwd", l, qkv, y_attn, dya, probs, p_sink, rope)
        grad[("in", l)], = run(_mix_bwd_dw, "mix_bwd_dw", l, dq, dkv, dbch, a, tm_dw_mix)
        dh, dg_in = run(_in_proj_bwd_dx, "in_proj_bwd_dx", l, dq, dkv, dbch, weight[("in", l)], h0, dh2,
                        row1(mix_pre_g, l), tm)
        tile_a = dg_mlp + dg_in + jnp.pad(dsink, ((0, 0), (0, D_MODEL - 128)))
        gsmall[l] = (tile_a, dg_mix)
    grad_x = dh[BLOCK:][None]

    loss_tile = jnp.zeros((8, D_MODEL), F32).at[ROW_LOSS, 0].set(loss_part)
    tot = _sum_small(jnp.concatenate(
        [gsmall[0][0] + loss_tile, gsmall[0][1], gsmall[1][0], gsmall[1][1], dh[LEAD_PAD:BLOCK]], axis=0))
    loss = tot[ROW_LOSS, 0]
    ta = [tot[16 * l:16 * l + 8] for l in range(DEPTH)]
    tb = [tot[16 * l + 8:16 * l + 16] for l in range(DEPTH)]
    pick = lambda tiles, r0, r1, c0, c1: jnp.stack([t[r0:r1, c0:c1] for t in tiles])
    g_mlp_post = pick(ta, ROW_MLP_POST, ROW_MLP_POST + 1, 0, D_MODEL).reshape(DEPTH, D_MODEL)
    g_mlp_pre = pick(ta, ROW_MLP_PRE, ROW_MLP_PRE + 1, 0, D_MODEL).reshape(DEPTH, D_MODEL)
    g_mix_pre = pick(ta, ROW_MIX_PRE, ROW_MIX_PRE + 1, 0, D_MODEL).reshape(DEPTH, D_MODEL)
    g_sinks = pick(ta, ROW_SINK, ROW_SINK + 1, 0, N_Q_HEADS).reshape(DEPTH, N_Q_HEADS)
    g_mix_post = pick(tb, ROW_MIX_POST, ROW_MIX_POST + 1, 0, D_MODEL).reshape(DEPTH, D_MODEL)
    g_attn_out = pick(tb, ROW_GROUP_G, ROW_GROUP_G + 1, 0, ATTN_W).reshape(DEPTH, ATTN_W)
    g_conv_out = pick(tb, ROW_GROUP_G, ROW_GROUP_G + 1, ATTN_W, D_MODEL).reshape(DEPTH, CONV_W)
    g_conv_full = pick(tb, ROW_CONV, ROW_CONV + 3, 0, CONV_W)
    g_conv = lax.dynamic_slice_in_dim(g_conv_full, me * cshard, cshard, axis=2)
    g_meta = lax.dynamic_slice_in_dim(tot[16 * DEPTH:16 * DEPTH + N_META], me * mshard, mshard, axis=1)

    r_in, r_out, r_up, r_down = [[landed[(n, l)] for l in range(DEPTH)] for n in ("in", "out", "up", "down")]
    t12 = lambda a: jnp.swapaxes(a, 1, 2)
    g_w_in, d_w_in, nm_w_in, nv_w_in = map(t12, _sum_adamw(r_in, t12(w_in), t12(m_w_in), t12(v_w_in), 96, "adamw_w_in"))
    g_w_up, d_w_up, nm_w_up, nv_w_up = _sum_adamw(r_up, w_up, m_w_up, v_w_up, 128, "adamw_w_up", transposed=True)
    g_w_out, d_w_out, nm_w_out, nv_w_out = _sum_adamw(r_out, w_out, m_w_out, v_w_out, 128, "adamw_w_out")
    g_w_down, d_w_down, nm_w_down, nv_w_down = _sum_adamw(r_down, w_down, m_w_down, v_w_down, 128, "adamw_w_down")

    ws = [meta_tokens, mix_pre_g, conv_w.reshape(6, cshard), sinks, attn_out_g, conv_out_g, mix_post_g, mlp_pre_g, mlp_post_g]
    gs = [g_meta, g_mix_pre, g_conv.reshape(6, cshard), g_sinks, g_attn_out, g_conv_out, g_mix_post, g_mlp_pre, g_mlp_post]
    ms = [m_meta_tokens, m_mix_pre_g, m_conv_w.reshape(6, cshard), m_sinks, m_attn_out_g, m_conv_out_g, m_mix_post_g,
          m_mlp_pre_g, m_mlp_post_g]
    vs = [v_meta_tokens, v_mix_pre_g, v_conv_w.reshape(6, cshard), v_sinks, v_attn_out_g, v_conv_out_g, v_mix_post_g,
          v_mlp_pre_g, v_mlp_post_g]
    ds, nms, nvs = _adamw_small(ws, gs, ms, vs)

    def order(meta, mix_pre, cv, sk, a_out, c_out, mix_post, mlp_pre, mlp_post, win, wout, wup, wdown):
        return [meta, mix_pre, win, cv.reshape(DEPTH, 3, cshard), sk, a_out, c_out, wout, mix_post, mlp_pre, wup, wdown, mlp_post]

    grads = order(*gs, g_w_in, g_w_out, g_w_up, g_w_down)
    deltas = order(*ds, d_w_in, d_w_out, d_w_up, d_w_down)
    new_m = order(*nms, nm_w_in, nm_w_out, nm_w_up, nm_w_down)
    new_v = order(*nvs, nv_w_in, nv_w_out, nv_w_up, nv_w_down)
    return (loss, grad_x, *grads, *deltas, *new_m, *new_v)
```

```python
import functools
import math

import jax
import jax.numpy as jnp
from jax import lax
from jax.experimental import pallas as pl
from jax.experimental.pallas import tpu as pltpu

F32 = jnp.float32
BF = jnp.bfloat16

D_MODEL = 1024
ATTN_W = 512
CONV_W = 512
HEAD_DIM = 64
N_Q_HEADS = 8
ROT_DIM = 16
D_FF = 4096
IN_W = 2304
N_META = 16
BLOCK = 128
LEAD_PAD = BLOCK - N_META
ROPE_THETA = 500000.0
EPS = 1e-6
N_DEV = 8
DEPTH = 2
NEG = -1e30
SCALE = HEAD_DIM ** -0.5

ADAM_LR = 0.001
ADAM_B1 = 0.9
ADAM_B2 = 0.999
ADAM_EPS = 1e-08
ADAM_WD = 0.01
ADAM_STEP = 10

ROW_MLP_POST, ROW_MLP_PRE, ROW_MIX_PRE, ROW_SINK, ROW_LOSS = 0, 1, 2, 3, 4
ROW_MIX_POST, ROW_GROUP_G, ROW_CONV = 0, 1, 2

VMEM_LIMIT = 56 * 1024 * 1024
MESH = pl.DeviceIdType.MESH


def _dot(a, b):
    return jnp.dot(a, b, preferred_element_type=F32)


def _dot_nt(a, b):
    return lax.dot_general(a, b, (((1,), (1,)), ((), ())), preferred_element_type=F32)


def _dot_tn(a, b):
    return lax.dot_general(a, b, (((0,), (0,)), ((), ())), preferred_element_type=F32)


def _rms_fwd(x, g):
    r = lax.rsqrt(jnp.mean(x * x, axis=-1, keepdims=True) + EPS)
    return x * r * g


def _rms_bwd(x, g, dy):
    r = lax.rsqrt(jnp.mean(x * x, axis=-1, keepdims=True) + EPS)
    xh = x * r
    t = dy * g
    dx = r * (t - xh * jnp.mean(t * xh, axis=-1, keepdims=True))
    dg = jnp.sum(dy * xh, axis=0, keepdims=True)
    return dx, dg


def _row_tile(lp, cands=(640, 512, 384, 256, 128)):
    for t in cands:
        if lp % t == 0:
            return t
    raise ValueError(f"row count {lp} is not a multiple of 128")


def _full(shape):
    n = len(shape)
    return pl.BlockSpec(shape, lambda *_: (0,) * n, pipeline_mode=pl.Buffered(1))


def _full_out(shape):
    n = len(shape)
    return pl.BlockSpec(shape, lambda *_: (0,) * n)


def _params(sem=("arbitrary",)):
    return pltpu.CompilerParams(dimension_semantics=sem, vmem_limit_bytes=VMEM_LIMIT)


def _rope_table(lp):
    half = ROT_DIM // 2
    pos = jnp.maximum(jnp.arange(lp) - LEAD_PAD, 0).astype(F32)
    inv_freq = jnp.power(jnp.float32(ROPE_THETA), -jnp.arange(0, ROT_DIM, 2, dtype=F32) / ROT_DIM)
    ang_t = jnp.concatenate([inv_freq, inv_freq])[:, None] * pos[None, :]
    row = lax.broadcasted_iota(jnp.int32, (ROT_DIM, lp), 0)
    cs_t = jnp.where(row < half, jnp.cos(ang_t), jnp.sin(ang_t))
    return jnp.pad(cs_t.T, ((0, 0), (0, 128 - ROT_DIM)))


def _rope_coeffs(t):
    half = ROT_DIM // 2
    lane = lax.broadcasted_iota(jnp.int32, t.shape, 1)
    cos_a = jnp.where(lane < half, t, 0.0)
    sin_a = pltpu.roll(jnp.where((lane >= half) & (lane < ROT_DIM), t, 0.0), 128 - half, 1)
    c = cos_a + pltpu.roll(cos_a, half, 1) + jnp.where((lane >= ROT_DIM) & (lane < HEAD_DIM), 1.0, 0.0)
    s2 = pltpu.roll(sin_a, half, 1)
    both = lambda u: u + pltpu.roll(u, HEAD_DIM, 1)
    return both(c), both(-sin_a), both(s2)


def _rope(t, c, s1, s2):
    return t * c + pltpu.roll(t, BLOCK - 8, 1) * s1 + pltpu.roll(t, 8, 1) * s2


def _rope_t(dt, c, s1, s2):
    return dt * c + pltpu.roll(dt * s1, 8, 1) + pltpu.roll(dt * s2, BLOCK - 8, 1)


def _build_h(x, rope_compact, tm, exch, small_piece, name):
    seq = x.shape[0]
    lp = BLOCK + seq
    nt = lp // tm
    n_sub = tm // BLOCK
    small_shape = exch.land_shapes[small_piece].shape

    def body(*refs):
        h_ref, c_ref, s1_ref, s2_ref = refs[n_sub + 1:n_sub + 5]
        for j in range(n_sub):
            h_ref[j * BLOCK:(j + 1) * BLOCK, :] = refs[j][...]
        c_ref[...], s1_ref[...], s2_ref[...] = _rope_coeffs(refs[n_sub][...])

    def after(lands, *refs):
        h_ref, buf = refs[n_sub + 1], refs[n_sub + 5]
        pltpu.sync_copy(lands[small_piece], buf)
        h_ref[0:LEAD_PAD, :] = jnp.zeros((LEAD_PAD, D_MODEL), F32)
        for d in range(N_DEV):
            h_ref[LEAD_PAD:BLOCK, d * 128:(d + 1) * 128] = buf[d, 0:N_META, :]

    tile = lambda i: (i + 1) % nt
    piece = lambda j: pl.BlockSpec((BLOCK, D_MODEL), lambda i: (jnp.maximum(tile(i) * n_sub + j - 1, 0), 0))
    rows = lambda w: pl.BlockSpec((tm, w), lambda i: (tile(i), 0))
    (h, *rope), lands = _call(
        body, exch,
        name=name,
        grid=(nt,),
        in_specs=[piece(j) for j in range(n_sub)] + [rows(128)],
        out_specs=[rows(D_MODEL)] + [rows(128)] * 3,
        out_shape=[jax.ShapeDtypeStruct((lp, D_MODEL), F32)] + [jax.ShapeDtypeStruct((lp, 128), F32)] * 3,
        scratch_shapes=[pltpu.VMEM(small_shape, F32)],
        compiler_params=_params(),
        after=after,
    )(*([x] * n_sub), rope_compact)
    return h, rope, lands


def _in_proj_fwd(h, g, w_in_t, rope, tm, name, exch=None):
    lp = h.shape[0]

    def body(h_ref, g_ref, w_ref, c_ref, s1_ref, s2_ref, a_ref, qkv_ref, bch_ref):
        a = _rms_fwd(h_ref[...], g_ref[...]).astype(BF)
        a_ref[...] = a
        proj = _dot_nt(a, w_ref[...])
        c, s1, s2 = c_ref[...], s1_ref[...], s2_ref[...]
        for j in range(5):
            t = _rope(proj[:, j * 128:(j + 1) * 128], c, s1, s2)
            qkv_ref[:, j * 128:(j + 1) * 128] = (t * SCALE if j < 4 else t).astype(BF)
        qkv_ref[:, 640:768] = proj[:, 640:768].astype(BF)
        bch_ref[...] = proj[:, 768:].astype(BF)

    row = lambda w: pl.BlockSpec((tm, w), lambda i: (i, 0))
    return _call(
        body, exch,
        name=name,
        grid=(lp // tm,),
        in_specs=[row(D_MODEL), _full((1, D_MODEL)), _full((IN_W, D_MODEL)), row(128), row(128), row(128)],
        out_specs=[row(D_MODEL), row(768), row(3 * CONV_W)],
        out_shape=[
            jax.ShapeDtypeStruct((lp, D_MODEL), BF),
            jax.ShapeDtypeStruct((lp, 768), BF),
            jax.ShapeDtypeStruct((lp, 3 * CONV_W), BF),
        ],
        compiler_params=_params(),
    )(h, g, w_in_t, *rope)


def _fold_masks(i):
    r = lax.broadcasted_iota(jnp.int32, (2 * BLOCK, BLOCK), 0) & (BLOCK - 1)
    c = lax.broadcasted_iota(jnp.int32, (2 * BLOCK, BLOCK), 1)
    tri = c > r
    ok = jnp.where(tri, (i - 1) * BLOCK + c, i * BLOCK + c) >= LEAD_PAD
    return tri, ok


def _kv_operand(x, kvh):
    lane = lax.broadcasted_iota(jnp.int32, x.shape, 1)
    zero = jnp.zeros_like(x)
    if kvh == 0:
        lo = jnp.where(lane < HEAD_DIM, x, zero)
        hi = pltpu.roll(lo, HEAD_DIM, 1)
    else:
        hi = jnp.where(lane >= HEAD_DIM, x, zero)
        lo = pltpu.roll(hi, HEAD_DIM, 1)
    return jnp.concatenate([lo, hi], axis=0)


def _split4(t, tri):
    zero = jnp.zeros_like(t[0])
    return jnp.concatenate(
        [jnp.where(tri, t[0], zero), jnp.where(tri, zero, t[0]), jnp.where(tri, t[1], zero), jnp.where(tri, zero, t[1])], axis=1)


def _sink_cols(sink_ref, kvh):
    first = lax.broadcasted_iota(jnp.int32, (2 * BLOCK, 1), 0) < BLOCK
    return [jnp.where(first, sink_ref[0, 4 * kvh + half], sink_ref[0, 4 * kvh + 2 + half]) for half in range(2)]


def _folded_exp(q2, k4, tri, ok, sks):
    s = _dot_nt(q2, k4)
    es, ss = [], []
    for half in range(2):
        s_h = s[:, 2 * half * BLOCK:2 * (half + 1) * BLOCK]
        sf = jnp.where(ok, jnp.where(tri, s_h[:, :BLOCK], s_h[:, BLOCK:]), NEG)
        m = jnp.maximum(jnp.max(sf, axis=-1, keepdims=True), sks[half])
        es.append(jnp.exp(sf - m))
        ss.append(jnp.exp(sks[half] - m))
    sums = _dot(jnp.concatenate(es, axis=0).astype(BF), jnp.ones((BLOCK, BLOCK), BF))
    invs = [1.0 / (sums[2 * half * BLOCK:2 * (half + 1) * BLOCK] + ss[half]) for half in range(2)]
    return es, ss, invs


def _attn_fwd(qkv, sink, name, exch=None):
    lp = qkv.shape[0]
    nb = lp // BLOCK
    per_step = 2

    def one_block(i, sink_ref, q_ref, kvc_ref, kvp_ref, o_ref, p_ref, ps_ref):
        tri, ok = _fold_masks(i)
        kvc, kvp = kvc_ref[...], kvp_ref[...]
        kk = jnp.concatenate([kvp[:, :128], kvc[:, :128]], axis=0)
        vv = jnp.concatenate([kvp[:, 128:], kvc[:, 128:]], axis=0)
        lane = lax.broadcasted_iota(jnp.int32, (BLOCK, 128), 1)
        p_sink = jnp.zeros((BLOCK, 128), F32)
        for kvh in range(2):
            q2 = jnp.concatenate([q_ref[:, 256 * kvh:256 * kvh + 128], q_ref[:, 256 * kvh + 128:256 * kvh + 256]], axis=0)
            es, ss, invs = _folded_exp(q2, _kv_operand(kk, kvh), tri, ok, _sink_cols(sink_ref, kvh))
            pb = [(es[half] * invs[half]).astype(BF) for half in range(2)]
            out = _dot(_split4(pb, tri), _kv_operand(vv, kvh))
            for pair in range(2):
                rows = slice(pair * BLOCK, (pair + 1) * BLOCK)
                o_ref[:, 256 * kvh + 128 * pair:256 * kvh + 128 * (pair + 1)] = out[rows].astype(BF)
                for half in range(2):
                    head = 4 * kvh + 2 * pair + half
                    p_ref[:, 128 * head:128 * (head + 1)] = pb[half][rows]
                    p_sink = jnp.where(lane == head, (ss[half] * invs[half][:, 0:1])[rows], p_sink)
        ps_ref[...] = p_sink

    def body(sink_ref, *refs):
        q_refs, kv_refs = refs[:per_step], refs[per_step:2 * per_step + 1]
        o_ref, p_ref, ps_ref = refs[2 * per_step + 1:]
        for j in range(per_step):
            rows = slice(j * BLOCK, (j + 1) * BLOCK)
            one_block(per_step * pl.program_id(0) + j, sink_ref, q_refs[j], kv_refs[j + 1], kv_refs[j],
                      o_ref.at[rows], p_ref.at[rows], ps_ref.at[rows])

    last = nb - 1
    blk = lambda j: (lambda s: jnp.minimum(per_step * s + j, last))
    out_rows = lambda w: pl.BlockSpec((per_step * BLOCK, w), lambda s: (s, 0))
    return _call(
        body, exch,
        name=name,
        grid=(pl.cdiv(nb, per_step),),
        in_specs=[pl.BlockSpec(memory_space=pltpu.SMEM)]
        + [pl.BlockSpec((BLOCK, ATTN_W), lambda s, j=j: (blk(j)(s), 0)) for j in range(per_step)]
        + [pl.BlockSpec((BLOCK, 256), lambda s: (jnp.maximum(per_step * s - 1, 0), 2))]
        + [pl.BlockSpec((BLOCK, 256), lambda s, j=j: (blk(j)(s), 2)) for j in range(per_step)],
        out_specs=[out_rows(ATTN_W), out_rows(N_Q_HEADS * BLOCK), out_rows(128)],
        out_shape=[jax.ShapeDtypeStruct((lp, ATTN_W), BF), jax.ShapeDtypeStruct((lp, N_Q_HEADS * BLOCK), BF),
                   jax.ShapeDtypeStruct((lp, 128), F32)],
        compiler_params=_params(),
    )(sink, *([qkv] * (2 * per_step + 1)))


def _mix_out_fwd(bch, y_attn, h, conv_w, g_a, g_c, w_out, g_post, tm, name, exch=None):
    lp = h.shape[0]

    def body(bch_ref, ya_ref, h_ref, cw_ref, ga_ref, gc_ref, w_ref, gp_ref, yc_ref, y_ref, z_ref, h2_ref, ext):
        i = pl.program_id(0)

        @pl.when(i == 0)
        def _():
            ext[0:8, :] = jnp.zeros((8, CONV_W), F32)

        b = bch_ref[:, 0:CONV_W].astype(F32)
        u = bch_ref[:, CONV_W:2 * CONV_W].astype(F32) * bch_ref[:, 2 * CONV_W:3 * CONV_W].astype(F32)
        ext[8:8 + tm, :] = u
        yc = cw_ref[0:1, :] * ext[6:6 + tm, :] + cw_ref[1:2, :] * ext[7:7 + tm, :] + cw_ref[2:3, :] * u
        ext[0:8, :] = u[tm - 8:tm, :]
        yc_ref[...] = yc.astype(BF)
        ya = _rms_fwd(ya_ref[...].astype(F32), ga_ref[...]).astype(BF)
        yb = _rms_fwd(b * yc, gc_ref[...]).astype(BF)
        y_ref[:, 0:ATTN_W] = ya
        y_ref[:, ATTN_W:] = yb
        z = _dot(ya, w_ref[0:ATTN_W, :]) + _dot(yb, w_ref[ATTN_W:, :])
        z_ref[...] = z
        h2_ref[...] = h_ref[...] + _rms_fwd(z, gp_ref[...])

    row = lambda w: pl.BlockSpec((tm, w), lambda i: (i, 0))
    return _call(
        body, exch,
        name=name,
        grid=(lp // tm,),
        in_specs=[
            row(3 * CONV_W), row(ATTN_W), row(D_MODEL), _full((8, CONV_W)), _full((1, ATTN_W)), _full((1, CONV_W)),
            _full((D_MODEL, D_MODEL)), _full((1, D_MODEL)),
        ],
        out_specs=[row(CONV_W), row(D_MODEL), row(D_MODEL), row(D_MODEL)],
        out_shape=[
            jax.ShapeDtypeStruct((lp, CONV_W), BF),
            jax.ShapeDtypeStruct((lp, D_MODEL), BF),
            jax.ShapeDtypeStruct((lp, D_MODEL), F32),
            jax.ShapeDtypeStruct((lp, D_MODEL), F32),
        ],
        scratch_shapes=[pltpu.VMEM((tm + 8, CONV_W), F32)],
        compiler_params=_params(),
    )(bch, y_attn, h, conv_w, g_a, g_c, w_out, g_post)


def _mlp_fwd(h2, g_pre, w_up_t, w_down, g_post, tm, name, exch=None, target=None):
    lp = h2.shape[0]
    sub = math.gcd(tm, BLOCK)
    n_sub, lead = tm // sub, BLOCK // sub
    n_t = n_sub if target is not None else 0

    def body(*refs):
        h_ref, gp_ref, wu_ref, wd_ref, gq_ref = refs[:5]
        t_refs = refs[5:5 + n_t]
        a_ref, up_ref, f_ref, last_ref = refs[5 + n_t:9 + n_t]
        h = h_ref[...]
        a = _rms_fwd(h, gp_ref[...]).astype(BF)
        a_ref[...] = a
        up = _dot_nt(a, wu_ref[...])
        up_ref[...] = up.astype(BF)
        act = jnp.square(jnp.maximum(up, 0.0)).astype(BF)
        f = _dot(act, wd_ref[...])
        f_ref[...] = f
        h3 = h + _rms_fwd(f, gq_ref[...])
        if target is None:
            last_ref[...] = h3
            return
        ls_ref = refs[9 + n_t]
        i = pl.program_id(0)

        @pl.when(i == 0)
        def _():
            ls_ref[...] = jnp.zeros((8, 128), F32)

        sq = jnp.zeros((1, 1), F32)
        for j in range(n_sub):
            on_tokens = i * n_sub + j >= lead
            d = jnp.where(on_tokens, h3[j * sub:(j + 1) * sub] - t_refs[j][...], 0.0)
            last_ref[j * sub:(j + 1) * sub, :] = d * (1.0 / D_MODEL)
            sq = sq + jnp.sum(d * d)
        ls_ref[...] += sq

    row = lambda w: pl.BlockSpec((tm, w), lambda i: (i, 0))
    piece = lambda j: pl.BlockSpec((sub, D_MODEL), lambda i: (jnp.maximum(i * n_sub + j - lead, 0), 0))
    out_specs = [row(D_MODEL), row(D_FF), row(D_MODEL), row(D_MODEL)]
    out_shape = [
        jax.ShapeDtypeStruct((lp, D_MODEL), BF),
        jax.ShapeDtypeStruct((lp, D_FF), BF),
        jax.ShapeDtypeStruct((lp, D_MODEL), F32),
        jax.ShapeDtypeStruct((lp, D_MODEL), F32),
    ]
    if target is not None:
        out_specs.append(_full_out((8, 128)))
        out_shape.append(jax.ShapeDtypeStruct((8, 128), F32))
    return _call(
        body, exch,
        name=name,
        grid=(lp // tm,),
        in_specs=[row(D_MODEL), _full((1, D_MODEL)), _full((D_FF, D_MODEL)), _full((D_FF, D_MODEL)), _full((1, D_MODEL))]
        + [piece(j) for j in range(n_t)],
        out_specs=out_specs,
        out_shape=out_shape,
        compiler_params=_params(),
    )(h2, g_pre, w_up_t, w_down, g_post, *([target] * n_t))


def _mlp_bwd_dx(dh3, f, up, h2, w_down, w_up_t, g_post, g_pre, tm, name, exch=None):
    lp = h2.shape[0]

    def body(dh3_ref, f_ref, up_ref, h2_ref, wd_ref, wu_ref, gq_ref, gp_ref, df_ref, dup_ref, dh2_ref, dg_ref):
        i = pl.program_id(0)

        @pl.when(i == 0)
        def _():
            dg_ref[...] = jnp.zeros((8, D_MODEL), F32)

        dh3 = dh3_ref[...]
        df, dgq = _rms_bwd(f_ref[...], gq_ref[...], dh3)
        dg_ref[ROW_MLP_POST:ROW_MLP_POST + 1, :] += dgq
        df = df.astype(BF)
        df_ref[...] = df
        dact = _dot_nt(df, wd_ref[...])
        dup = (dact * (2.0 * jnp.maximum(up_ref[...].astype(F32), 0.0))).astype(BF)
        dup_ref[...] = dup
        da = _dot(dup, wu_ref[...])
        dh, dgp = _rms_bwd(h2_ref[...], gp_ref[...], da)
        dg_ref[ROW_MLP_PRE:ROW_MLP_PRE + 1, :] += dgp
        dh2_ref[...] = dh3 + dh

    row = lambda w: pl.BlockSpec((tm, w), lambda i: (i, 0))
    return _call(
        body, exch,
        name=name,
        grid=(lp // tm,),
        in_specs=[
            row(D_MODEL), row(D_MODEL), row(D_FF), row(D_MODEL), _full((D_FF, D_MODEL)), _full((D_FF, D_MODEL)),
            _full((1, D_MODEL)), _full((1, D_MODEL)),
        ],
        out_specs=[row(D_MODEL), row(D_FF), row(D_MODEL), _full_out((8, D_MODEL))],
        out_shape=[
            jax.ShapeDtypeStruct((lp, D_MODEL), BF),
            jax.ShapeDtypeStruct((lp, D_FF), BF),
            jax.ShapeDtypeStruct((lp, D_MODEL), F32),
            jax.ShapeDtypeStruct((8, D_MODEL), F32),
        ],
        compiler_params=_params(),
    )(dh3, f, up, h2, w_down, w_up_t, g_post, g_pre)


def _mlp_bwd_dw(up, df, dup, a2, tm, name):
    lp = up.shape[0]
    nt = lp // tm
    nj = D_FF // D_MODEL

    def body(up_ref, df_ref, dup_ref, a_ref, dwd_ref, dwu_ref, accd, accu):
        i = pl.program_id(1)

        @pl.when(i == 0)
        def _():
            accd[...] = jnp.zeros_like(accd)
            accu[...] = jnp.zeros_like(accu)

        act = jnp.square(jnp.maximum(up_ref[...].astype(F32), 0.0)).astype(BF)
        accd[...] += _dot_tn(act, df_ref[...])
        accu[...] += _dot_tn(dup_ref[...], a_ref[...])

        @pl.when(i == nt - 1)
        def _():
            dwd_ref[...] = accd[...].astype(BF)
            dwu_ref[...] = accu[...].astype(BF)

    return pl.pallas_call(
        body,
        name=name,
        grid=(nj, nt),
        in_specs=[
            pl.BlockSpec((tm, D_MODEL), lambda j, i: (i, j)),
            pl.BlockSpec((tm, D_MODEL), lambda j, i: (i, 0)),
            pl.BlockSpec((tm, D_MODEL), lambda j, i: (i, j)),
            pl.BlockSpec((tm, D_MODEL), lambda j, i: (i, 0)),
        ],
        out_specs=[pl.BlockSpec((D_MODEL, D_MODEL), lambda j, i: (j, 0)), pl.BlockSpec((D_MODEL, D_MODEL), lambda j, i: (j, 0))],
        out_shape=[jax.ShapeDtypeStruct((D_FF, D_MODEL), BF), jax.ShapeDtypeStruct((D_FF, D_MODEL), BF)],
        scratch_shapes=[pltpu.VMEM((D_MODEL, D_MODEL), F32), pltpu.VMEM((D_MODEL, D_MODEL), F32)],
        compiler_params=_params(("arbitrary", "arbitrary")),
    )(up, df, dup, a2)


def _mix_out_bwd(dh2, z, y_attn, yc, bch, y, w_out, g_post, g_a, g_c, conv_w, tm, name, exch=None):
    lp = dh2.shape[0]
    nt = lp // tm

    def body(dh2_ref, z_ref, ya_ref, yc_ref, bch_ref, y_ref, w_ref, gp_ref, ga_ref, gc_ref, cw_ref,
             dya_ref, dbch_ref, dg_ref, dwo_ref, ext, acco):
        i = pl.program_id(0)
        dcw_ref = dg_ref.at[ROW_CONV:ROW_CONV + 3, 0:CONV_W]

        @pl.when(i == 0)
        def _():
            ext[tm:tm + 8, :] = jnp.zeros((8, CONV_W), F32)
            dg_ref[...] = jnp.zeros((8, D_MODEL), F32)
            acco[...] = jnp.zeros_like(acco)

        dz, dgp = _rms_bwd(z_ref[...], gp_ref[...], dh2_ref[...])
        dg_ref[ROW_MIX_POST:ROW_MIX_POST + 1, :] += dgp
        dz = dz.astype(BF)
        acco[...] += _dot_tn(y_ref[...], dz)
        dya_n = _dot_nt(dz, w_ref[0:ATTN_W, :])
        dyb_n = _dot_nt(dz, w_ref[ATTN_W:, :])
        dya, dga = _rms_bwd(ya_ref[...].astype(F32), ga_ref[...], dya_n)
        dg_ref[ROW_GROUP_G:ROW_GROUP_G + 1, 0:ATTN_W] += dga
        dya_ref[...] = dya
        b = bch_ref[:, 0:CONV_W].astype(F32)
        c = bch_ref[:, CONV_W:2 * CONV_W].astype(F32)
        hc = bch_ref[:, 2 * CONV_W:3 * CONV_W].astype(F32)
        u = c * hc
        yc_v = yc_ref[...].astype(F32)
        dyconv, dgc = _rms_bwd(b * yc_v, gc_ref[...], dyb_n)
        dg_ref[ROW_GROUP_G:ROW_GROUP_G + 1, ATTN_W:] += dgc
        dbch_ref[:, 0:CONV_W] = (dyconv * yc_v).astype(BF)
        dyc = dyconv * b
        ext[0:tm, :] = dyc
        d1 = ext[1:1 + tm, :]
        d2 = ext[2:2 + tm, :]
        du = cw_ref[2:3, :] * dyc + cw_ref[1:2, :] * d1 + cw_ref[0:1, :] * d2
        ext[tm:tm + 8, :] = dyc[0:8, :]
        dbch_ref[:, CONV_W:2 * CONV_W] = (du * hc).astype(BF)
        dbch_ref[:, 2 * CONV_W:3 * CONV_W] = (du * c).astype(BF)
        dcw_ref[0:1, :] += jnp.sum(u * d2, axis=0, keepdims=True)
        dcw_ref[1:2, :] += jnp.sum(u * d1, axis=0, keepdims=True)
        dcw_ref[2:3, :] += jnp.sum(u * dyc, axis=0, keepdims=True)

        @pl.when(i == nt - 1)
        def _():
            dwo_ref[...] = acco[...].astype(BF)

    row = lambda w: pl.BlockSpec((tm, w), lambda i: (nt - 1 - i, 0))
    return _call(
        body, exch,
        name=name,
        grid=(nt,),
        in_specs=[
            row(D_MODEL), row(D_MODEL), row(ATTN_W), row(CONV_W), row(3 * CONV_W), row(D_MODEL), _full((D_MODEL, D_MODEL)),
            _full((1, D_MODEL)), _full((1, ATTN_W)), _full((1, CONV_W)), _full((8, CONV_W)),
        ],
        out_specs=[row(ATTN_W), row(3 * CONV_W), _full_out((8, D_MODEL)), _full_out((D_MODEL, D_MODEL))],
        out_shape=[
            jax.ShapeDtypeStruct((lp, ATTN_W), F32),
            jax.ShapeDtypeStruct((lp, 3 * CONV_W), BF),
            jax.ShapeDtypeStruct((8, D_MODEL), F32),
            jax.ShapeDtypeStruct((D_MODEL, D_MODEL), BF),
        ],
        scratch_shapes=[pltpu.VMEM((tm + 8, CONV_W), F32), pltpu.VMEM((D_MODEL, D_MODEL), F32)],
        compiler_params=_params(),
    )(dh2, z, y_attn, yc, bch, y, w_out, g_post, g_a, g_c, conv_w)


def _attn_bwd(qkv, o, do, probs, p_sink, rope, name, exch=None):
    lp = qkv.shape[0]
    nb = lp // BLOCK
    assert nb % 2 == 1, "an odd number of 128-row blocks is expected"
    ns = (nb + 1) // 2

    def block_grads(q_ref, kvc_ref, kvp_ref, o_ref, do, p_ref, ps_ref, rope_q, dq_ref):
        tri, _ = _fold_masks(0)
        kvc, kvp = kvc_ref[...], kvp_ref[...]
        kk = jnp.concatenate([kvp[:, :128], kvc[:, :128]], axis=0)
        vv = jnp.concatenate([kvp[:, 128:], kvc[:, 128:]], axis=0)
        lane = lax.broadcasted_iota(jnp.int32, (BLOCK, 128), 1)
        lane2 = lax.broadcasted_iota(jnp.int32, (2 * BLOCK, 128), 1)
        deltas = jnp.zeros((BLOCK, 128), F32)
        folded = []
        for kvh in range(2):
            c0 = 256 * kvh
            q2 = jnp.concatenate([q_ref[:, c0:c0 + 128], q_ref[:, c0 + 128:c0 + 256]], axis=0)
            do2 = jnp.concatenate([do[:, c0:c0 + 128], do[:, c0 + 128:c0 + 256]], axis=0)
            o2 = jnp.concatenate([o_ref[:, c0:c0 + 128], o_ref[:, c0 + 128:c0 + 256]], axis=0).astype(F32)
            k4, v4 = _kv_operand(kk, kvh), _kv_operand(vv, kvh)
            prod = do2 * o2
            dob = do2.astype(BF)
            dp = _dot_nt(dob, v4)
            ds, pb = [], []
            for half in range(2):
                heads = [4 * kvh + 2 * pair + half for pair in range(2)]
                p = jnp.concatenate([p_ref[:, 128 * h:128 * (h + 1)] for h in heads], axis=0)
                sel = (lane2 < HEAD_DIM) if half == 0 else (lane2 >= HEAD_DIM)
                delta = jnp.sum(jnp.where(sel, prod, 0.0), axis=-1, keepdims=True)
                dp_h = dp[:, 2 * half * BLOCK:2 * (half + 1) * BLOCK]
                ds.append((p.astype(F32) * (jnp.where(tri, dp_h[:, :BLOCK], dp_h[:, BLOCK:]) - delta)).astype(BF))
                pb.append(p)
                for pair in range(2):
                    deltas = jnp.where(lane == heads[pair], delta[pair * BLOCK:(pair + 1) * BLOCK], deltas)
            ds4, p4 = _split4(ds, tri), _split4(pb, tri)
            dq2 = _dot(ds4, k4) * SCALE
            dq_ref[:, c0:c0 + 128] = _rope_t(dq2[:BLOCK], *rope_q).astype(BF)
            dq_ref[:, c0 + 128:c0 + 256] = _rope_t(dq2[BLOCK:], *rope_q).astype(BF)
            rk, rv = _dot_tn(ds4, q2), _dot_tn(p4, dob)
            own = (lane < HEAD_DIM) if kvh == 0 else (lane >= HEAD_DIM)
            group = []
            for r in (rk, rv):
                for blk in range(2):
                    t = jnp.where(lane < HEAD_DIM, r[blk * BLOCK:(blk + 1) * BLOCK], r[(2 + blk) * BLOCK:(3 + blk) * BLOCK])
                    group.append(jnp.where(own, t + pltpu.roll(t, HEAD_DIM, 1), 0.0))
            folded.append(group)
        dk_p, dk_c, dv_p, dv_c = [folded[0][t] + folded[1][t] for t in range(4)]
        sink_row = jnp.sum(ps_ref[...] * deltas, axis=0, keepdims=True)
        return jnp.concatenate([dk_p, dv_p], axis=1), jnp.concatenate([dk_c, dv_c], axis=1), sink_row

    def body(*refs):
        q, kv, o_, do_, p_, ps_ = refs[0:2], refs[2:5], refs[5:7], refs[7:9], refs[9:11], refs[11:13]
        rope_q = (refs[13:16], refs[16:19])
        rope_k = (refs[19:22], refs[22:25])
        dq_ref, dkv_ref, dsink_ref, carry = refs[25:29]
        s = pl.program_id(0)

        @pl.when(s == 0)
        def _():
            carry[...] = jnp.zeros_like(carry)
            dsink_ref[...] = jnp.zeros((8, 128), F32)

        parts = []
        for j in range(2):
            half = slice(j * BLOCK, (j + 1) * BLOCK)
            do = jnp.where(2 * s + j < nb, do_[j][...], 0.0)
            parts.append(block_grads(q[j], kv[j + 1], kv[j], o_[j], do, p_[j], ps_[j],
                                     tuple(t[...] for t in rope_q[j]), dq_ref.at[half]))
        (prev_a, own_a, sink_a), (prev_b, own_b, sink_b) = parts
        dsink_ref[ROW_SINK:ROW_SINK + 1, :] -= sink_a + sink_b
        for j, tot in enumerate((carry[...] + prev_a, own_a + prev_b)):
            half = slice(j * BLOCK, (j + 1) * BLOCK)
            dk = _rope_t(tot[:, :128], *(t[...] for t in rope_k[j]))
            dkv_ref[half, 0:128] = dk.astype(BF)
            dkv_ref[half, 128:256] = tot[:, 128:].astype(BF)
        carry[...] = own_b

    blk = lambda j: (lambda s: jnp.clip(2 * s + j, 0, nb - 1))
    rows = lambda w, j, col=0: pl.BlockSpec((BLOCK, w), lambda s: (blk(j)(s), col))
    pair = lambda w: [rows(w, 0), rows(w, 1)]
    tabs = lambda j: [rows(128, j)] * 3
    res = _call(
        body, exch,
        name=name,
        grid=(ns,),
        in_specs=pair(ATTN_W) + [rows(256, j, 2) for j in (-1, 0, 1)] + pair(ATTN_W) + pair(ATTN_W)
        + pair(N_Q_HEADS * BLOCK) + pair(128) + tabs(0) + tabs(1) + tabs(-1) + tabs(0),
        out_specs=[
            pl.BlockSpec((2 * BLOCK, ATTN_W), lambda s: (s, 0)),
            pl.BlockSpec((2 * BLOCK, 256), lambda s: (s, 0)),
            pl.BlockSpec((8, 128), lambda s: (0, 0)),
        ],
        out_shape=[
            jax.ShapeDtypeStruct((lp, ATTN_W), BF),
            jax.ShapeDtypeStruct((ns * 2 * BLOCK, 256), BF),
            jax.ShapeDtypeStruct((8, 128), F32),
        ],
        scratch_shapes=[pltpu.VMEM((BLOCK, 256), F32)],
        compiler_params=_params(),
    )(qkv, qkv, qkv, qkv, qkv, o, o, do, do, probs, probs, p_sink, p_sink, *(tuple(rope) * 4))
    (dq, dkv_late, dsink), lands = res if exch is not None else (res, None)
    outs = [dq, dkv_late[BLOCK:BLOCK + lp], dsink]
    return (outs, lands) if exch is not None else outs


def _in_proj_bwd_dx(dq, dkv, dbch, w_in_t, h, dh2, g, tm, name, exch=None):
    lp = h.shape[0]

    def body(dq_ref, dkv_ref, dbch_ref, w_ref, h_ref, dh2_ref, g_ref, dh_ref, dg_ref):
        i = pl.program_id(0)

        @pl.when(i == 0)
        def _():
            dg_ref[...] = jnp.zeros((8, D_MODEL), F32)

        da = _dot(jnp.concatenate([dq_ref[...], dkv_ref[...], dbch_ref[...]], axis=1), w_ref[...])
        dh, dg = _rms_bwd(h_ref[...], g_ref[...], da)
        dg_ref[ROW_MIX_PRE:ROW_MIX_PRE + 1, :] += dg
        dh_ref[...] = dh2_ref[...] + dh

    row = lambda w: pl.BlockSpec((tm, w), lambda i: (i, 0))
    return _call(
        body, exch,
        name=name,
        grid=(lp // tm,),
        in_specs=[row(ATTN_W), row(256), row(3 * CONV_W), _full((IN_W, D_MODEL)), row(D_MODEL), row(D_MODEL), _full((1, D_MODEL))],
        out_specs=[row(D_MODEL), _full_out((8, D_MODEL))],
        out_shape=[jax.ShapeDtypeStruct((lp, D_MODEL), F32), jax.ShapeDtypeStruct((8, D_MODEL), F32)],
        compiler_params=_params(),
    )(dq, dkv, dbch, w_in_t, h, dh2, g)


def _mix_bwd_dw(dq, dkv, dbch, a, tm, name, exch=None):
    lp = a.shape[0]
    nt = lp // tm

    def body(dq_ref, dkv_ref, dbch_ref, a_ref, dwi_ref, acci):
        i = pl.program_id(0)

        @pl.when(i == 0)
        def _():
            acci[...] = jnp.zeros_like(acci)

        a_v = a_ref[...]
        acci[0:512, :] += _dot_tn(dq_ref[...], a_v)
        acci[512:768, :] += _dot_tn(dkv_ref[...], a_v)
        acci[768:, :] += _dot_tn(dbch_ref[...], a_v)

        @pl.when(i == nt - 1)
        def _():
            dwi_ref[...] = acci[...].astype(BF)

    row = lambda w: pl.BlockSpec((tm, w), lambda i: (i, 0))
    return _call(
        body, exch,
        name=name,
        grid=(nt,),
        in_specs=[row(ATTN_W), row(256), row(3 * CONV_W), row(D_MODEL)],
        out_specs=[_full_out((IN_W, D_MODEL))],
        out_shape=[jax.ShapeDtypeStruct((IN_W, D_MODEL), BF)],
        scratch_shapes=[pltpu.VMEM((IN_W, D_MODEL), F32)],
        compiler_params=_params(),
    )(dq, dkv, dbch, a)


def _mesh_place():
    x, y, c = lax.axis_index("x"), lax.axis_index("y"), lax.axis_index("c")
    return x, y, c, 4 * x + 2 * y + c


def _peer(x, y, c, k):
    px = 1 - x if k & 4 else x
    py = 1 - y if k & 2 else y
    pc = 1 - c if k & 1 else c
    return (px, py, pc), 4 * px + 2 * py + pc


SIBLING = 1
SAME_CORE = (2, 4, 6)
OTHER_CORE = (3, 5, 7)


class _Exchange:
    def __init__(self, pieces):
        self.srcs = [s for s, _ in pieces]
        self.to_all = [g for _, g in pieces]
        self.n = len(pieces)
        self.land_shapes = [
            jax.ShapeDtypeStruct((N_DEV,) + (s.shape if g else s.shape[1:]), s.dtype) for s, g in pieces]
        self.sem_shapes = [pltpu.SemaphoreType.DMA((self.n, N_DEV - 1)), pltpu.SemaphoreType.DMA((self.n, N_DEV - 1)),
                           pltpu.SemaphoreType.DMA((self.n,))]
        self.forwards = any(self.to_all)

    def _ops(self, srcs, lands, sems):
        send_sems, recv_sems, local_sems = sems
        x, y, c, me = _mesh_place()

        def remote(p, k, src, slot, to):
            return pltpu.make_async_remote_copy(
                src_ref=src, dst_ref=lands[p].at[slot], send_sem=send_sems.at[p, k - 1], recv_sem=recv_sems.at[p, k - 1],
                device_id=to, device_id_type=MESH)

        def own(p):
            return pltpu.make_async_copy(srcs[p] if self.to_all[p] else srcs[p].at[me], lands[p].at[me], local_sems.at[p])

        def direct(p, k):
            peer, pidx = _peer(x, y, c, k)
            return remote(p, k, srcs[p] if self.to_all[p] else srcs[p].at[pidx], me, peer)

        def forward(p, k):
            sibling, _ = _peer(x, y, c, SIBLING)
            _, origin = _peer(x, y, c, k ^ SIBLING)
            return remote(p, k, lands[p].at[origin], origin, sibling)

        def arrival(p, k):
            peer, pidx = _peer(x, y, c, k)
            return remote(p, k, lands[p].at[pidx], pidx, peer)

        return own, direct, forward, arrival

    def start(self, srcs, lands, sems):
        own, direct, _, _ = self._ops(srcs, lands, sems)
        for p in range(self.n):
            own(p).start()
            for k in ((SIBLING,) + SAME_CORE) if self.to_all[p] else range(1, N_DEV):
                direct(p, k).start()

    def forward(self, srcs, lands, sems):
        _, _, forward, arrival = self._ops(srcs, lands, sems)
        for p in range(self.n):
            if self.to_all[p]:
                for k in SAME_CORE:
                    arrival(p, k).wait_recv()
                    forward(p, k ^ SIBLING).start()

    def finish(self, srcs, lands, sems):
        own, direct, forward, arrival = self._ops(srcs, lands, sems)
        for p in range(self.n):
            for k in ((SIBLING,) + OTHER_CORE) if self.to_all[p] else range(1, N_DEV):
                arrival(p, k).wait_recv()
        for p in range(self.n):
            for k in range(1, N_DEV):
                (forward(p, k) if self.to_all[p] and k in OTHER_CORE else direct(p, k)).wait_send()
            own(p).wait()


def _call(body, exch, *, name, grid, in_specs, out_specs, out_shape, scratch_shapes=(), compiler_params, after=None):
    if exch is None:
        return pl.pallas_call(body, name=name, grid=grid, in_specs=in_specs, out_specs=out_specs, out_shape=out_shape,
                              scratch_shapes=scratch_shapes, compiler_params=compiler_params)
    n_in, n_out, n_scr, n_x = len(in_specs), len(out_shape), len(scratch_shapes), exch.n
    steps = math.prod(grid)

    def carrying(*refs):
        a, b, c, d, e = n_in, n_in + n_x, n_in + n_x + n_out, n_in + 2 * n_x + n_out, n_in + 2 * n_x + n_out + n_scr
        ins, srcs, outs, lands, scr, sems = refs[:a], refs[a:b], refs[b:c], refs[c:d], refs[d:e], refs[e:]
        step = functools.reduce(lambda acc, t: acc * grid[t] + pl.program_id(t), range(len(grid)), 0)

        @pl.when(step == 0)
        def _():
            exch.start(srcs, lands, sems)

        body(*ins, *outs, *scr)

        if exch.forwards:
            @pl.when(step == max(0, steps - 1 - (steps + 7) // 8))
            def _():
                exch.forward(srcs, lands, sems)

        @pl.when(step == steps - 1)
        def _():
            exch.finish(srcs, lands, sems)
            if after is not None:
                after(lands, *ins, *outs, *scr)

    hbm = pl.BlockSpec(memory_space=pl.ANY)
    call = pl.pallas_call(
        carrying, name=name, grid=grid, in_specs=list(in_specs) + [hbm] * n_x, out_specs=list(out_specs) + [hbm] * n_x,
        out_shape=list(out_shape) + exch.land_shapes, scratch_shapes=list(scratch_shapes) + exch.sem_shapes,
        compiler_params=compiler_params)

    def run(*args):
        res = call(*args, *exch.srcs)
        return list(res[:n_out]), list(res[n_out:])

    return run


def _sum_small(part):
    exch = _Exchange([(part, True)])

    def body(part_ref, out_ref, land, *sems):
        exch.start([part_ref], [land], sems)
        exch.forward([part_ref], [land], sems)
        exch.finish([part_ref], [land], sems)
        acc = land[0]
        for d in range(1, N_DEV):
            acc = acc + land[d]
        out_ref[...] = acc

    vmem = pl.BlockSpec(memory_space=pltpu.VMEM)
    return pl.pallas_call(
        body,
        name="sum_small",
        in_specs=[vmem],
        out_specs=vmem,
        out_shape=jax.ShapeDtypeStruct(part.shape, F32),
        scratch_shapes=[pltpu.VMEM(exch.land_shapes[0].shape, F32)] + exch.sem_shapes,
    )(part)


def _adamw(w, g, m, v):
    m = ADAM_B1 * m + (1.0 - ADAM_B1) * g
    v = ADAM_B2 * v + (1.0 - ADAM_B2) * jnp.square(g)
    m_hat = m / (1.0 - ADAM_B1 ** ADAM_STEP)
    v_hat = v / (1.0 - ADAM_B2 ** ADAM_STEP)
    delta = -ADAM_LR * (m_hat / (jnp.sqrt(v_hat) + ADAM_EPS) + ADAM_WD * w)
    return delta, m, v


def _landed_specs(tr, wd):
    return [pl.BlockSpec((N_DEV, tr, wd), lambda l, i, ll=ll: (0, jnp.where(l == ll, i, 0), 0)) for ll in range(DEPTH)]


def _device_sum(r_ref):
    acc = r_ref[0].astype(F32)
    for d in range(1, N_DEV):
        acc = acc + r_ref[d].astype(F32)
    return acc


def _sum_adamw(recv, w, m, v, tr, name, transposed=False):
    _, r, wd = recv[0].shape

    def body(*refs):
        w_ref, m_ref, v_ref, g_ref, d_ref, mo_ref, vo_ref = refs[DEPTH:]
        for ll in range(DEPTH):
            @pl.when(pl.program_id(0) == ll)
            def _(ll=ll):
                g = _device_sum(refs[ll])
                g = g.T if transposed else g
                g_ref[0] = g
                d_ref[0], mo_ref[0], vo_ref[0] = _adamw(w_ref[0], g, m_ref[0], v_ref[0])

    if transposed:
        blk = pl.BlockSpec((1, wd, tr), lambda l, i: (l, 0, i))
        shape = jax.ShapeDtypeStruct((DEPTH, wd, r), F32)
    else:
        blk = pl.BlockSpec((1, tr, wd), lambda l, i: (l, i, 0))
        shape = jax.ShapeDtypeStruct((DEPTH, r, wd), F32)
    return pl.pallas_call(
        body,
        name=name,
        grid=(DEPTH, r // tr),
        in_specs=_landed_specs(tr, wd) + [blk, blk, blk],
        out_specs=[blk] * 4,
        out_shape=[shape] * 4,
        compiler_params=_params(("arbitrary", "arbitrary")),
    )(*recv, w, m, v)


def _adamw_small(ws, gs, ms, vs):
    n = len(ws)

    def body(*refs):
        w_r, g_r, m_r, v_r = refs[:n], refs[n:2 * n], refs[2 * n:3 * n], refs[3 * n:4 * n]
        d_o, m_o, v_o = refs[4 * n:5 * n], refs[5 * n:6 * n], refs[6 * n:7 * n]
        for t in range(n):
            d_o[t][...], m_o[t][...], v_o[t][...] = _adamw(w_r[t][...], g_r[t][...], m_r[t][...], v_r[t][...])

    vmem = pl.BlockSpec(memory_space=pltpu.VMEM)
    shapes = [jax.ShapeDtypeStruct(w.shape, F32) for w in ws]
    outs = pl.pallas_call(
        body,
        name="adamw_small",
        in_specs=[vmem] * (4 * n),
        out_specs=[vmem] * (3 * n),
        out_shape=shapes * 3,
    )(*ws, *gs, *ms, *vs)
    return outs[:n], outs[n:2 * n], outs[2 * n:]


def kernel(x, meta_tokens, mix_pre_g, w_in, conv_w, sinks, attn_out_g, conv_out_g, w_out, mix_post_g, mlp_pre_g, w_up, w_down, mlp_post_g, loss_target, m_meta_tokens, m_mix_pre_g, m_w_in, m_conv_w, m_sinks, m_attn_out_g, m_conv_out_g, m_w_out, m_mix_post_g, m_mlp_pre_g, m_w_up, m_w_down, m_mlp_post_g, v_meta_tokens, v_mix_pre_g, v_w_in, v_conv_w, v_sinks, v_attn_out_g, v_conv_out_g, v_w_out, v_mix_post_g, v_mlp_pre_g, v_w_up, v_w_down, v_mlp_post_g):
    seq = x.shape[1]
    lp = BLOCK + seq
    tm = _row_tile(lp)
    tm_mlp = _row_tile(lp, (320, 256, 128))
    tm_dw_mlp = _row_tile(lp, (1664, 1040, 640, 384, 256, 128))
    tm_dw_mix = _row_tile(lp, (1664, 832, 640, 384, 256, 128))
    me = 4 * lax.axis_index("x") + 2 * lax.axis_index("y") + lax.axis_index("c")
    cshard = CONV_W // N_DEV
    mshard = D_MODEL // N_DEV

    gather_with = {
        ("in_proj_fwd", 0): [("in", 1)], ("attn_fwd", 0): [("out", 0), ("up", 0)], ("mix_out_fwd", 0): [("down", 0)],
        ("mlp_fwd", 0): [("out", 1), ("up", 1), ("down", 1)],
    }
    scatter_with = {
        ("attn_bwd", 1): [("down", 1)], ("mix_bwd_dw", 1): [("out", 1)], ("mlp_bwd_dx", 0): [("up", 1), ("in", 1)],
        ("mix_out_bwd", 0): [("up", 0)], ("attn_bwd", 0): [("down", 0)], ("mix_bwd_dw", 0): [("out", 0)],
        ("in_proj_bwd_dx", 0): [("in", 0)],
    }
    shard = {"in": jnp.swapaxes(w_in, 1, 2).astype(BF), "out": w_out.astype(BF),
             "up": jnp.swapaxes(w_up, 1, 2).astype(BF), "down": w_down.astype(BF)}
    weight = {}
    grad = {}
    landed = {}

    def run(fn, kind, l, *args):
        key, name = (kind, l), f"{kind}_{l}"
        if key in gather_with:
            blocks = gather_with[key]
            outs, lands = fn(*args, name, _Exchange([(shard[n][k], True) for n, k in blocks]))
            for b, land in zip(blocks, lands):
                weight[b] = land.reshape(-1, D_MODEL)
            return outs
        if key in scatter_with:
            blocks = scatter_with[key]
            outs, lands = fn(*args, name, _Exchange([(grad[b].reshape(N_DEV, -1, D_MODEL), False) for b in blocks]))
            landed.update(zip(blocks, lands))
            return outs
        return fn(*args, name)

    small = jnp.zeros((24, 128), F32)
    small = small.at[0:N_META, :].set(meta_tokens)
    small = small.at[N_META:N_META + 6, 0:cshard].set(conv_w.reshape(6, cshard))
    first = _Exchange([(shard["in"][0], True), (small, True)])
    h, rope, (first_in, g_small) = _build_h(x[0], _rope_table(lp), tm, first, 1, "build_h")
    weight[("in", 0)] = first_in.reshape(-1, D_MODEL)
    cw = g_small[:, N_META:N_META + 6, 0:cshard].reshape(N_DEV, DEPTH, 3, cshard)
    cw = jnp.transpose(cw, (1, 2, 0, 3)).reshape(DEPTH, 3, CONV_W)
    conv_full = jnp.concatenate([cw, jnp.zeros((DEPTH, 5, CONV_W), F32)], axis=1)

    row1 = lambda a, l: a[l].reshape(1, -1)

    saved = []
    for l in range(DEPTH):
        a, qkv, bch = run(_in_proj_fwd, "in_proj_fwd", l, h, row1(mix_pre_g, l), weight[("in", l)], rope, tm)
        y_attn, probs, p_sink = run(_attn_fwd, "attn_fwd", l, qkv, row1(sinks, l))
        yc, y, z, h2 = run(_mix_out_fwd, "mix_out_fwd", l, bch, y_attn, h, conv_full[l], row1(attn_out_g, l),
                       row1(conv_out_g, l), weight[("out", l)], row1(mix_post_g, l), tm)
        mlp = _mlp_fwd if l < DEPTH - 1 else functools.partial(_mlp_fwd, target=loss_target[0])
        a2, up, f, *rest = run(mlp, "mlp_fwd", l, h2, row1(mlp_pre_g, l), weight[("up", l)], weight[("down", l)],
                               row1(mlp_post_g, l), tm_mlp)
        saved.append((h, a, qkv, bch, y_attn, probs, p_sink, yc, y, z, h2, a2, up, f))
        h = rest[0]
    dh, loss_part = rest[0], rest[1][0, 0] * (0.5 / D_MODEL)

    gsmall = [None] * DEPTH
    for l in reversed(range(DEPTH)):
        h0, a, qkv, bch, y_attn, probs, p_sink, yc, y, z, h2, a2, up, f = saved[l]
        df, dup, dh2, dg_mlp = run(_mlp_bwd_dx, "mlp_bwd_dx", l, dh, f, up, h2, weight[("down", l)], weight[("up", l)],
                                   row1(mlp_post_g, l), row1(mlp_pre_g, l), tm_mlp)
        grad[("down", l)], grad[("up", l)] = _mlp_bwd_dw(up, df, dup, a2, tm_dw_mlp, f"mlp_bwd_dw_{l}")
        dya, dbch, dg_mix, grad[("out", l)] = run(
            _mix_out_bwd, "mix_out_bwd", l, dh2, z, y_attn, yc, bch, y, weight[("out", l)], row1(mix_post_g, l),
            row1(attn_out_g, l), row1(conv_out_g, l), conv_full[l], tm)
        dq, dkv, dsink = run(_attn_bwd, "attn_bwd", l, qkv, y_attn, dya, probs, p_sink, rope)
        grad[("in", l)], = run(_mix_bwd_dw, "mix_bwd_dw", l, dq, dkv, dbch, a, tm_dw_mix)
        dh, dg_in = run(_in_proj_bwd_dx, "in_proj_bwd_dx", l, dq, dkv, dbch, weight[("in", l)], h0, dh2,
                        row1(mix_pre_g, l), tm)
        tile_a = dg_mlp + dg_in + jnp.pad(dsink, ((0, 0), (0, D_MODEL - 128)))
        gsmall[l] = (tile_a, dg_mix)
    grad_x = dh[BLOCK:][None]

    loss_tile = jnp.zeros((8, D_MODEL), F32).at[ROW_LOSS, 0].set(loss_part)
    tot = _sum_small(jnp.concatenate(
        [gsmall[0][0] + loss_tile, gsmall[0][1], gsmall[1][0], gsmall[1][1], dh[LEAD_PAD:BLOCK]], axis=0))
    loss = tot[ROW_LOSS, 0]
    ta = [tot[16 * l:16 * l + 8] for l in range(DEPTH)]
    tb = [tot[16 * l + 8:16 * l + 16] for l in range(DEPTH)]
    pick = lambda tiles, r0, r1, c0, c1: jnp.stack([t[r0:r1, c0:c1] for t in tiles])
    g_mlp_post = pick(ta, ROW_MLP_POST, ROW_MLP_POST + 1, 0, D_MODEL).reshape(DEPTH, D_MODEL)
    g_mlp_pre = pick(ta, ROW_MLP_PRE, ROW_MLP_PRE + 1, 0, D_MODEL).reshape(DEPTH, D_MODEL)
    g_mix_pre = pick(ta, ROW_MIX_PRE, ROW_MIX_PRE + 1, 0, D_MODEL).reshape(DEPTH, D_MODEL)
    g_sinks = pick(ta, ROW_SINK, ROW_SINK + 1, 0, N_Q_HEADS).reshape(DEPTH, N_Q_HEADS)
    g_mix_post = pick(tb, ROW_MIX_POST, ROW_MIX_POST + 1, 0, D_MODEL).reshape(DEPTH, D_MODEL)
    g_attn_out = pick(tb, ROW_GROUP_G, ROW_GROUP_G + 1, 0, ATTN_W).reshape(DEPTH, ATTN_W)
    g_conv_out = pick(tb, ROW_GROUP_G, ROW_GROUP_G + 1, ATTN_W, D_MODEL).reshape(DEPTH, CONV_W)
    g_conv_full = pick(tb, ROW_CONV, ROW_CONV + 3, 0, CONV_W)
    g_conv = lax.dynamic_slice_in_dim(g_conv_full, me * cshard, cshard, axis=2)
    g_meta = lax.dynamic_slice_in_dim(tot[16 * DEPTH:16 * DEPTH + N_META], me * mshard, mshard, axis=1)

    r_in, r_out, r_up, r_down = [[landed[(n, l)] for l in range(DEPTH)] for n in ("in", "out", "up", "down")]
    t12 = lambda a: jnp.swapaxes(a, 1, 2)
    g_w_in, d_w_in, nm_w_in, nv_w_in = map(t12, _sum_adamw(r_in, t12(w_in), t12(m_w_in), t12(v_w_in), 96, "adamw_w_in"))
    g_w_up, d_w_up, nm_w_up, nv_w_up = _sum_adamw(r_up, w_up, m_w_up, v_w_up, 128, "adamw_w_up", transposed=True)
    g_w_out, d_w_out, nm_w_out, nv_w_out = _sum_adamw(r_out, w_out, m_w_out, v_w_out, 128, "adamw_w_out")
    g_w_down, d_w_down, nm_w_down, nv_w_down = _sum_adamw(r_down, w_down, m_w_down, v_w_down, 128, "adamw_w_down")

    ws = [meta_tokens, mix_pre_g, conv_w.reshape(6, cshard), sinks, attn_out_g, conv_out_g, mix_post_g, mlp_pre_g, mlp_post_g]
    gs = [g_meta, g_mix_pre, g_conv.reshape(6, cshard), g_sinks, g_attn_out, g_conv_out, g_mix_post, g_mlp_pre, g_mlp_post]
    ms = [m_meta_tokens, m_mix_pre_g, m_conv_w.reshape(6, cshard), m_sinks, m_attn_out_g, m_conv_out_g, m_mix_post_g,
          m_mlp_pre_g, m_mlp_post_g]
    vs = [v_meta_tokens, v_mix_pre_g, v_conv_w.reshape(6, cshard), v_sinks, v_attn_out_g, v_conv_out_g, v_mix_post_g,
          v_mlp_pre_g, v_mlp_post_g]
    ds, nms, nvs = _adamw_small(ws, gs, ms, vs)

    def order(meta, mix_pre, cv, sk, a_out, c_out, mix_post, mlp_pre, mlp_post, win, wout, wup, wdown):
        return [meta, mix_pre, win, cv.reshape(DEPTH, 3, cshard), sk, a_out, c_out, wout, mix_post, mlp_pre, wup, wdown, mlp_post]

    grads = order(*gs, g_w_in, g_w_out, g_w_up, g_w_down)
    deltas = order(*ds, d_w_in, d_w_out, d_w_up, d_w_down)
    new_m = order(*nms, nm_w_in, nm_w_out, nm_w_up, nm_w_down)
    new_v = order(*nvs, nv_w_in, nv_w_out, nv_w_up, nv_w_down)
    return (loss, grad_x, *grads, *deltas, *new_m, *new_v)
```

```python
import functools
import math

import jax
import jax.numpy as jnp
from jax import lax
from jax.experimental import pallas as pl
from jax.experimental.pallas import tpu as pltpu

F32 = jnp.float32
BF = jnp.bfloat16

D_MODEL = 1024
ATTN_W = 512
CONV_W = 512
HEAD_DIM = 64
N_Q_HEADS = 8
ROT_DIM = 16
D_FF = 4096
IN_W = 2304
N_META = 16
BLOCK = 128
LEAD_PAD = BLOCK - N_META
ROPE_THETA = 500000.0
EPS = 1e-6
N_DEV = 8
DEPTH = 2
NEG = -1e30
SCALE = HEAD_DIM ** -0.5

ADAM_LR = 0.001
ADAM_B1 = 0.9
ADAM_B2 = 0.999
ADAM_EPS = 1e-08
ADAM_WD = 0.01
ADAM_STEP = 10

ROW_MLP_POST, ROW_MLP_PRE, ROW_MIX_PRE, ROW_SINK, ROW_LOSS = 0, 1, 2, 3, 4
ROW_MIX_POST, ROW_GROUP_G, ROW_CONV = 0, 1, 2

VMEM_LIMIT = 56 * 1024 * 1024
MESH = pl.DeviceIdType.MESH


def _dot(a, b):
    return jnp.dot(a, b, preferred_element_type=F32)


def _dot_nt(a, b):
    return lax.dot_general(a, b, (((1,), (1,)), ((), ())), preferred_element_type=F32)


def _dot_tn(a, b):
    return lax.dot_general(a, b, (((0,), (0,)), ((), ())), preferred_element_type=F32)


def _rms_fwd(x, g):
    r = lax.rsqrt(jnp.mean(x * x, axis=-1, keepdims=True) + EPS)
    return x * r * g


def _rms_bwd(x, g, dy):
    r = lax.rsqrt(jnp.mean(x * x, axis=-1, keepdims=True) + EPS)
    xh = x * r
    t = dy * g
    dx = r * (t - xh * jnp.mean(t * xh, axis=-1, keepdims=True))
    dg = jnp.sum(dy * xh, axis=0, keepdims=True)
    return dx, dg


def _row_tile(lp, cands=(640, 512, 384, 256, 128)):
    for t in cands:
        if lp % t == 0:
            return t
    raise ValueError(f"row count {lp} is not a multiple of 128")


def _full(shape):
    n = len(shape)
    return pl.BlockSpec(shape, lambda *_: (0,) * n, pipeline_mode=pl.Buffered(1))


def _full_out(shape):
    n = len(shape)
    return pl.BlockSpec(shape, lambda *_: (0,) * n)


def _params(sem=("arbitrary",)):
    return pltpu.CompilerParams(dimension_semantics=sem, vmem_limit_bytes=VMEM_LIMIT)


def _rope_table(lp):
    half = ROT_DIM // 2
    pos = jnp.maximum(jnp.arange(lp) - LEAD_PAD, 0).astype(F32)
    inv_freq = jnp.power(jnp.float32(ROPE_THETA), -jnp.arange(0, ROT_DIM, 2, dtype=F32) / ROT_DIM)
    ang_t = jnp.concatenate([inv_freq, inv_freq])[:, None] * pos[None, :]
    row = lax.broadcasted_iota(jnp.int32, (ROT_DIM, lp), 0)
    cs_t = jnp.where(row < half, jnp.cos(ang_t), jnp.sin(ang_t))
    return jnp.pad(cs_t.T, ((0, 0), (0, 128 - ROT_DIM)))


def _rope_coeffs(t):
    half = ROT_DIM // 2
    lane = lax.broadcasted_iota(jnp.int32, t.shape, 1)
    cos_a = jnp.where(lane < half, t, 0.0)
    sin_a = pltpu.roll(jnp.where((lane >= half) & (lane < ROT_DIM), t, 0.0), 128 - half, 1)
    c = cos_a + pltpu.roll(cos_a, half, 1) + jnp.where((lane >= ROT_DIM) & (lane < HEAD_DIM), 1.0, 0.0)
    s2 = pltpu.roll(sin_a, half, 1)
    both = lambda u: u + pltpu.roll(u, HEAD_DIM, 1)
    return both(c), both(-sin_a), both(s2)


def _rope(t, c, s1, s2):
    return t * c + pltpu.roll(t, BLOCK - 8, 1) * s1 + pltpu.roll(t, 8, 1) * s2


def _rope_t(dt, c, s1, s2):
    return dt * c + pltpu.roll(dt * s1, 8, 1) + pltpu.roll(dt * s2, BLOCK - 8, 1)


def _build_h(x, rope_compact, tm, exch, small_piece, name):
    seq = x.shape[0]
    lp = BLOCK + seq
    nt = lp // tm
    n_sub = tm // BLOCK
    small_shape = exch.land_shapes[small_piece].shape

    def body(*refs):
        h_ref, c_ref, s1_ref, s2_ref = refs[n_sub + 1:n_sub + 5]
        for j in range(n_sub):
            h_ref[j * BLOCK:(j + 1) * BLOCK, :] = refs[j][...]
        c_ref[...], s1_ref[...], s2_ref[...] = _rope_coeffs(refs[n_sub][...])

    def after(lands, *refs):
        h_ref, buf = refs[n_sub + 1], refs[n_sub + 5]
        pltpu.sync_copy(lands[small_piece], buf)
        h_ref[0:LEAD_PAD, :] = jnp.zeros((LEAD_PAD, D_MODEL), F32)
        for d in range(N_DEV):
            h_ref[LEAD_PAD:BLOCK, d * 128:(d + 1) * 128] = buf[d, 0:N_META, :]

    tile = lambda i: (i + 1) % nt
    piece = lambda j: pl.BlockSpec((BLOCK, D_MODEL), lambda i: (jnp.maximum(tile(i) * n_sub + j - 1, 0), 0))
    rows = lambda w: pl.BlockSpec((tm, w), lambda i: (tile(i), 0))
    (h, *rope), lands = _call(
        body, exch,
        name=name,
        grid=(nt,),
        in_specs=[piece(j) for j in range(n_sub)] + [rows(128)],
        out_specs=[rows(D_MODEL)] + [rows(128)] * 3,
        out_shape=[jax.ShapeDtypeStruct((lp, D_MODEL), F32)] + [jax.ShapeDtypeStruct((lp, 128), F32)] * 3,
        scratch_shapes=[pltpu.VMEM(small_shape, F32)],
        compiler_params=_params(),
        after=after,
    )(*([x] * n_sub), rope_compact)
    return h, rope, lands


def _in_proj_fwd(h, g, w_in_t, rope, tm, name, exch=None):
    lp = h.shape[0]

    def body(h_ref, g_ref, w_ref, c_ref, s1_ref, s2_ref, a_ref, qkv_ref, bch_ref):
        a = _rms_fwd(h_ref[...], g_ref[...]).astype(BF)
        a_ref[...] = a
        proj = _dot_nt(a, w_ref[...])
        c, s1, s2 = c_ref[...], s1_ref[...], s2_ref[...]
        for j in range(5):
            t = _rope(proj[:, j * 128:(j + 1) * 128], c, s1, s2)
            qkv_ref[:, j * 128:(j + 1) * 128] = (t * SCALE if j < 4 else t).astype(BF)
        qkv_ref[:, 640:768] = proj[:, 640:768].astype(BF)
        bch_ref[...] = proj[:, 768:].astype(BF)

    row = lambda w: pl.BlockSpec((tm, w), lambda i: (i, 0))
    return _call(
        body, exch,
        name=name,
        grid=(lp // tm,),
        in_specs=[row(D_MODEL), _full((1, D_MODEL)), _full((IN_W, D_MODEL)), row(128), row(128), row(128)],
        out_specs=[row(D_MODEL), row(768), row(3 * CONV_W)],
        out_shape=[
            jax.ShapeDtypeStruct((lp, D_MODEL), BF),
            jax.ShapeDtypeStruct((lp, 768), BF),
            jax.ShapeDtypeStruct((lp, 3 * CONV_W), BF),
        ],
        compiler_params=_params(),
    )(h, g, w_in_t, *rope)


def _fold_masks(i):
    r = lax.broadcasted_iota(jnp.int32, (2 * BLOCK, BLOCK), 0) & (BLOCK - 1)
    c = lax.broadcasted_iota(jnp.int32, (2 * BLOCK, BLOCK), 1)
    tri = c > r
    ok = jnp.where(tri, (i - 1) * BLOCK + c, i * BLOCK + c) >= LEAD_PAD
    return tri, ok


def _kv_operand(x, kvh):
    lane = lax.broadcasted_iota(jnp.int32, x.shape, 1)
    zero = jnp.zeros_like(x)
    if kvh == 0:
        lo = jnp.where(lane < HEAD_DIM, x, zero)
        hi = pltpu.roll(lo, HEAD_DIM, 1)
    else:
        hi = jnp.where(lane >= HEAD_DIM, x, zero)
        lo = pltpu.roll(hi, HEAD_DIM, 1)
    return jnp.concatenate([lo, hi], axis=0)


def _split4(t, tri):
    zero = jnp.zeros_like(t[0])
    return jnp.concatenate(
        [jnp.where(tri, t[0], zero), jnp.where(tri, zero, t[0]), jnp.where(tri, t[1], zero), jnp.where(tri, zero, t[1])], axis=1)


def _sink_cols(sink_ref, kvh):
    first = lax.broadcasted_iota(jnp.int32, (2 * BLOCK, 1), 0) < BLOCK
    return [jnp.where(first, sink_ref[0, 4 * kvh + half], sink_ref[0, 4 * kvh + 2 + half]) for half in range(2)]


def _folded_exp(q2, k4, tri, ok, sks):
    s = _dot_nt(q2, k4)
    es, ss = [], []
    for half in range(2):
        s_h = s[:, 2 * half * BLOCK:2 * (half + 1) * BLOCK]
        sf = jnp.where(ok, jnp.where(tri, s_h[:, :BLOCK], s_h[:, BLOCK:]), NEG)
        m = jnp.maximum(jnp.max(sf, axis=-1, keepdims=True), sks[half])
        es.append(jnp.exp(sf - m))
        ss.append(jnp.exp(sks[half] - m))
    sums = _dot(jnp.concatenate(es, axis=0).astype(BF), jnp.ones((BLOCK, BLOCK), BF))
    invs = [1.0 / (sums[2 * half * BLOCK:2 * (half + 1) * BLOCK] + ss[half]) for half in range(2)]
    return es, ss, invs


def _attn_fwd(qkv, sink, name, exch=None):
    lp = qkv.shape[0]
    nb = lp // BLOCK
    per_step = 2

    def one_block(i, sink_ref, q_ref, kvc_ref, kvp_ref, o_ref, p_ref, ps_ref):
        tri, ok = _fold_masks(i)
        kvc, kvp = kvc_ref[...], kvp_ref[...]
        kk = jnp.concatenate([kvp[:, :128], kvc[:, :128]], axis=0)
        vv = jnp.concatenate([kvp[:, 128:], kvc[:, 128:]], axis=0)
        lane = lax.broadcasted_iota(jnp.int32, (BLOCK, 128), 1)
        p_sink = jnp.zeros((BLOCK, 128), F32)
        for kvh in range(2):
            q2 = jnp.concatenate([q_ref[:, 256 * kvh:256 * kvh + 128], q_ref[:, 256 * kvh + 128:256 * kvh + 256]], axis=0)
            es, ss, invs = _folded_exp(q2, _kv_operand(kk, kvh), tri, ok, _sink_cols(sink_ref, kvh))
            pb = [(es[half] * invs[half]).astype(BF) for half in range(2)]
            out = _dot(_split4(pb, tri), _kv_operand(vv, kvh))
            for pair in range(2):
                rows = slice(pair * BLOCK, (pair + 1) * BLOCK)
                o_ref[:, 256 * kvh + 128 * pair:256 * kvh + 128 * (pair + 1)] = out[rows].astype(BF)
                for half in range(2):
                    head = 4 * kvh + 2 * pair + half
                    p_ref[:, 128 * head:128 * (head + 1)] = pb[half][rows]
                    p_sink = jnp.where(lane == head, (ss[half] * invs[half][:, 0:1])[rows], p_sink)
        ps_ref[...] = p_sink

    def body(sink_ref, *refs):
        q_refs, kv_refs = refs[:per_step], refs[per_step:2 * per_step + 1]
        o_ref, p_ref, ps_ref = refs[2 * per_step + 1:]
        for j in range(per_step):
            rows = slice(j * BLOCK, (j + 1) * BLOCK)
            one_block(per_step * pl.program_id(0) + j, sink_ref, q_refs[j], kv_refs[j + 1], kv_refs[j],
                      o_ref.at[rows], p_ref.at[rows], ps_ref.at[rows])

    last = nb - 1
    blk = lambda j: (lambda s: jnp.minimum(per_step * s + j, last))
    out_rows = lambda w: pl.BlockSpec((per_step * BLOCK, w), lambda s: (s, 0))
    return _call(
        body, exch,
        name=name,
        grid=(pl.cdiv(nb, per_step),),
        in_specs=[pl.BlockSpec(memory_space=pltpu.SMEM)]
        + [pl.BlockSpec((BLOCK, ATTN_W), lambda s, j=j: (blk(j)(s), 0)) for j in range(per_step)]
        + [pl.BlockSpec((BLOCK, 256), lambda s: (jnp.maximum(per_step * s - 1, 0), 2))]
        + [pl.BlockSpec((BLOCK, 256), lambda s, j=j: (blk(j)(s), 2)) for j in range(per_step)],
        out_specs=[out_rows(ATTN_W), out_rows(N_Q_HEADS * BLOCK), out_rows(128)],
        out_shape=[jax.ShapeDtypeStruct((lp, ATTN_W), BF), jax.ShapeDtypeStruct((lp, N_Q_HEADS * BLOCK), BF),
                   jax.ShapeDtypeStruct((lp, 128), F32)],
        compiler_params=_params(),
    )(sink, *([qkv] * (2 * per_step + 1)))


def _mix_out_fwd(bch, y_attn, h, conv_w, g_a, g_c, w_out, g_post, tm, name, exch=None):
    lp = h.shape[0]

    def body(bch_ref, ya_ref, h_ref, cw_ref, ga_ref, gc_ref, w_ref, gp_ref, yc_ref, y_ref, z_ref, h2_ref, ext):
        i = pl.program_id(0)

        @pl.when(i == 0)
        def _():
            ext[0:8, :] = jnp.zeros((8, CONV_W), F32)

        b = bch_ref[:, 0:CONV_W].astype(F32)
        u = bch_ref[:, CONV_W:2 * CONV_W].astype(F32) * bch_ref[:, 2 * CONV_W:3 * CONV_W].astype(F32)
        ext[8:8 + tm, :] = u
        yc = cw_ref[0:1, :] * ext[6:6 + tm, :] + cw_ref[1:2, :] * ext[7:7 + tm, :] + cw_ref[2:3, :] * u
        ext[0:8, :] = u[tm - 8:tm, :]
        yc_ref[...] = yc.astype(BF)
        ya = _rms_fwd(ya_ref[...].astype(F32), ga_ref[...]).astype(BF)
        yb = _rms_fwd(b * yc, gc_ref[...]).astype(BF)
        y_ref[:, 0:ATTN_W] = ya
        y_ref[:, ATTN_W:] = yb
        z = _dot(ya, w_ref[0:ATTN_W, :]) + _dot(yb, w_ref[ATTN_W:, :])
        z_ref[...] = z.astype(BF)
        h2_ref[...] = h_ref[...] + _rms_fwd(z, gp_ref[...])

    row = lambda w: pl.BlockSpec((tm, w), lambda i: (i, 0))
    return _call(
        body, exch,
        name=name,
        grid=(lp // tm,),
        in_specs=[
            row(3 * CONV_W), row(ATTN_W), row(D_MODEL), _full((8, CONV_W)), _full((1, ATTN_W)), _full((1, CONV_W)),
            _full((D_MODEL, D_MODEL)), _full((1, D_MODEL)),
        ],
        out_specs=[row(CONV_W), row(D_MODEL), row(D_MODEL), row(D_MODEL)],
        out_shape=[
            jax.ShapeDtypeStruct((lp, CONV_W), BF),
            jax.ShapeDtypeStruct((lp, D_MODEL), BF),
            jax.ShapeDtypeStruct((lp, D_MODEL), BF),
            jax.ShapeDtypeStruct((lp, D_MODEL), F32),
        ],
        scratch_shapes=[pltpu.VMEM((tm + 8, CONV_W), F32)],
        compiler_params=_params(),
    )(bch, y_attn, h, conv_w, g_a, g_c, w_out, g_post)


def _mlp_fwd(h2, g_pre, w_up_t, w_down, g_post, tm, name, exch=None, target=None):
    lp = h2.shape[0]
    sub = math.gcd(tm, BLOCK)
    n_sub, lead = tm // sub, BLOCK // sub
    n_t = n_sub if target is not None else 0

    def body(*refs):
        h_ref, gp_ref, wu_ref, wd_ref, gq_ref = refs[:5]
        t_refs = refs[5:5 + n_t]
        a_ref, up_ref, f_ref, last_ref = refs[5 + n_t:9 + n_t]
        h = h_ref[...]
        a = _rms_fwd(h, gp_ref[...]).astype(BF)
        a_ref[...] = a
        up = _dot_nt(a, wu_ref[...])
        up_ref[...] = up.astype(BF)
        act = jnp.square(jnp.maximum(up, 0.0)).astype(BF)
        f = _dot(act, wd_ref[...])
        f_ref[...] = f
        h3 = h + _rms_fwd(f, gq_ref[...])
        if target is None:
            last_ref[...] = h3
            return
        ls_ref = refs[9 + n_t]
        i = pl.program_id(0)

        @pl.when(i == 0)
        def _():
            ls_ref[...] = jnp.zeros((8, 128), F32)

        sq = jnp.zeros((8, D_MODEL), F32)
        for j in range(n_sub):
            on_tokens = i * n_sub + j >= lead
            d = jnp.where(on_tokens, h3[j * sub:(j + 1) * sub] - t_refs[j][...], 0.0)
            last_ref[j * sub:(j + 1) * sub, :] = d * (1.0 / D_MODEL)
            sq = sq + jnp.sum((d * d).reshape(sub // 8, 8, D_MODEL), axis=0)
        ls_ref[...] += sum(sq[:, k * 128:(k + 1) * 128] for k in range(D_MODEL // 128))

        @pl.when(i == lp // tm - 1)
        def _():
            ls_ref[...] = jnp.full((8, 128), jnp.sum(ls_ref[...]), F32)

    row = lambda w: pl.BlockSpec((tm, w), lambda i: (i, 0))
    piece = lambda j: pl.BlockSpec((sub, D_MODEL), lambda i: (jnp.maximum(i * n_sub + j - lead, 0), 0))
    out_specs = [row(D_MODEL), row(D_FF), row(D_MODEL), row(D_MODEL)]
    out_shape = [
        jax.ShapeDtypeStruct((lp, D_MODEL), BF),
        jax.ShapeDtypeStruct((lp, D_FF), BF),
        jax.ShapeDtypeStruct((lp, D_MODEL), F32),
        jax.ShapeDtypeStruct((lp, D_MODEL), F32),
    ]
    if target is not None:
        out_specs.append(_full_out((8, 128)))
        out_shape.append(jax.ShapeDtypeStruct((8, 128), F32))
    return _call(
        body, exch,
        name=name,
        grid=(lp // tm,),
        in_specs=[row(D_MODEL), _full((1, D_MODEL)), _full((D_FF, D_MODEL)), _full((D_FF, D_MODEL)), _full((1, D_MODEL))]
        + [piece(j) for j in range(n_t)],
        out_specs=out_specs,
        out_shape=out_shape,
        compiler_params=_params(),
    )(h2, g_pre, w_up_t, w_down, g_post, *([target] * n_t))


def _mlp_bwd_dx(dh3, f, up, h2, w_down, w_up_t, g_post, g_pre, tm, name, exch=None):
    lp = h2.shape[0]

    def body(dh3_ref, f_ref, up_ref, h2_ref, wd_ref, wu_ref, gq_ref, gp_ref, df_ref, dup_ref, dh2_ref, dg_ref):
        i = pl.program_id(0)

        @pl.when(i == 0)
        def _():
            dg_ref[...] = jnp.zeros((8, D_MODEL), F32)

        dh3 = dh3_ref[...]
        df, dgq = _rms_bwd(f_ref[...], gq_ref[...], dh3)
        dg_ref[ROW_MLP_POST:ROW_MLP_POST + 1, :] += dgq
        df = df.astype(BF)
        df_ref[...] = df
        dact = _dot_nt(df, wd_ref[...])
        dup = (dact * (2.0 * jnp.maximum(up_ref[...].astype(F32), 0.0))).astype(BF)
        dup_ref[...] = dup
        da = _dot(dup, wu_ref[...])
        dh, dgp = _rms_bwd(h2_ref[...], gp_ref[...], da)
        dg_ref[ROW_MLP_PRE:ROW_MLP_PRE + 1, :] += dgp
        dh2_ref[...] = dh3 + dh

    row = lambda w: pl.BlockSpec((tm, w), lambda i: (i, 0))
    return _call(
        body, exch,
        name=name,
        grid=(lp // tm,),
        in_specs=[
            row(D_MODEL), row(D_MODEL), row(D_FF), row(D_MODEL), _full((D_FF, D_MODEL)), _full((D_FF, D_MODEL)),
            _full((1, D_MODEL)), _full((1, D_MODEL)),
        ],
        out_specs=[row(D_MODEL), row(D_FF), row(D_MODEL), _full_out((8, D_MODEL))],
        out_shape=[
            jax.ShapeDtypeStruct((lp, D_MODEL), BF),
            jax.ShapeDtypeStruct((lp, D_FF), BF),
            jax.ShapeDtypeStruct((lp, D_MODEL), F32),
            jax.ShapeDtypeStruct((8, D_MODEL), F32),
        ],
        compiler_params=_params(),
    )(dh3, f, up, h2, w_down, w_up_t, g_post, g_pre)


def _mlp_bwd_dw(up, df, dup, a2, tm, name):
    lp = up.shape[0]
    nt = lp // tm
    nj = D_FF // D_MODEL

    def body(up_ref, df_ref, dup_ref, a_ref, dwd_ref, dwu_ref, accd, accu):
        i = pl.program_id(1)

        @pl.when(i == 0)
        def _():
            accd[...] = jnp.zeros_like(accd)
            accu[...] = jnp.zeros_like(accu)

        act = jnp.square(jnp.maximum(up_ref[...].astype(F32), 0.0)).astype(BF)
        accd[...] += _dot_tn(act, df_ref[...])
        accu[...] += _dot_tn(dup_ref[...], a_ref[...])

        @pl.when(i == nt - 1)
        def _():
            dwd_ref[...] = accd[...].astype(BF)
            dwu_ref[...] = accu[...].astype(BF)

    return pl.pallas_call(
        body,
        name=name,
        grid=(nj, nt),
        in_specs=[
            pl.BlockSpec((tm, D_MODEL), lambda j, i: (i, j)),
            pl.BlockSpec((tm, D_MODEL), lambda j, i: (i, 0)),
            pl.BlockSpec((tm, D_MODEL), lambda j, i: (i, j)),
            pl.BlockSpec((tm, D_MODEL), lambda j, i: (i, 0)),
        ],
        out_specs=[pl.BlockSpec((D_MODEL, D_MODEL), lambda j, i: (j, 0)), pl.BlockSpec((D_MODEL, D_MODEL), lambda j, i: (j, 0))],
        out_shape=[jax.ShapeDtypeStruct((D_FF, D_MODEL), BF), jax.ShapeDtypeStruct((D_FF, D_MODEL), BF)],
        scratch_shapes=[pltpu.VMEM((D_MODEL, D_MODEL), F32), pltpu.VMEM((D_MODEL, D_MODEL), F32)],
        compiler_params=_params(("arbitrary", "arbitrary")),
    )(up, df, dup, a2)


def _mix_out_bwd(dh2, z, y_attn, yc, bch, y, w_out, g_post, g_a, g_c, conv_w, tm, name, exch=None):
    lp = dh2.shape[0]
    nt = lp // tm

    def body(dh2_ref, z_ref, ya_ref, yc_ref, bch_ref, y_ref, w_ref, gp_ref, ga_ref, gc_ref, cw_ref,
             dya_ref, dbch_ref, dg_ref, dwo_ref, ext, acco):
        i = pl.program_id(0)
        dcw_ref = dg_ref.at[ROW_CONV:ROW_CONV + 3, 0:CONV_W]

        @pl.when(i == 0)
        def _():
            ext[tm:tm + 8, :] = jnp.zeros((8, CONV_W), F32)
            dg_ref[...] = jnp.zeros((8, D_MODEL), F32)
            acco[...] = jnp.zeros_like(acco)

        dz, dgp = _rms_bwd(z_ref[...].astype(F32), gp_ref[...], dh2_ref[...])
        dg_ref[ROW_MIX_POST:ROW_MIX_POST + 1, :] += dgp
        dz = dz.astype(BF)
        acco[...] += _dot_tn(y_ref[...], dz)
        dya_n = _dot_nt(dz, w_ref[0:ATTN_W, :])
        dyb_n = _dot_nt(dz, w_ref[ATTN_W:, :])
        dya, dga = _rms_bwd(ya_ref[...].astype(F32), ga_ref[...], dya_n)
        dg_ref[ROW_GROUP_G:ROW_GROUP_G + 1, 0:ATTN_W] += dga
        dya_ref[...] = dya
        b = bch_ref[:, 0:CONV_W].astype(F32)
        c = bch_ref[:, CONV_W:2 * CONV_W].astype(F32)
        hc = bch_ref[:, 2 * CONV_W:3 * CONV_W].astype(F32)
        u = c * hc
        yc_v = yc_ref[...].astype(F32)
        dyconv, dgc = _rms_bwd(b * yc_v, gc_ref[...], dyb_n)
        dg_ref[ROW_GROUP_G:ROW_GROUP_G + 1, ATTN_W:] += dgc
        dbch_ref[:, 0:CONV_W] = (dyconv * yc_v).astype(BF)
        dyc = dyconv * b
        ext[0:tm, :] = dyc
        d1 = ext[1:1 + tm, :]
        d2 = ext[2:2 + tm, :]
        du = cw_ref[2:3, :] * dyc + cw_ref[1:2, :] * d1 + cw_ref[0:1, :] * d2
        ext[tm:tm + 8, :] = dyc[0:8, :]
        dbch_ref[:, CONV_W:2 * CONV_W] = (du * hc).astype(BF)
        dbch_ref[:, 2 * CONV_W:3 * CONV_W] = (du * c).astype(BF)
        dcw_ref[0:1, :] += jnp.sum(u * d2, axis=0, keepdims=True)
        dcw_ref[1:2, :] += jnp.sum(u * d1, axis=0, keepdims=True)
        dcw_ref[2:3, :] += jnp.sum(u * dyc, axis=0, keepdims=True)

        @pl.when(i == nt - 1)
        def _():
            dwo_ref[...] = acco[...].astype(BF)

    row = lambda w: pl.BlockSpec((tm, w), lambda i: (nt - 1 - i, 0))
    return _call(
        body, exch,
        name=name,
        grid=(nt,),
        in_specs=[
            row(D_MODEL), row(D_MODEL), row(ATTN_W), row(CONV_W), row(3 * CONV_W), row(D_MODEL), _full((D_MODEL, D_MODEL)),
            _full((1, D_MODEL)), _full((1, ATTN_W)), _full((1, CONV_W)), _full((8, CONV_W)),
        ],
        out_specs=[row(ATTN_W), row(3 * CONV_W), _full_out((8, D_MODEL)), _full_out((D_MODEL, D_MODEL))],
        out_shape=[
            jax.ShapeDtypeStruct((lp, ATTN_W), F32),
            jax.ShapeDtypeStruct((lp, 3 * CONV_W), BF),
            jax.ShapeDtypeStruct((8, D_MODEL), F32),
            jax.ShapeDtypeStruct((D_MODEL, D_MODEL), BF),
        ],
        scratch_shapes=[pltpu.VMEM((tm + 8, CONV_W), F32), pltpu.VMEM((D_MODEL, D_MODEL), F32)],
        compiler_params=_params(),
    )(dh2, z, y_attn, yc, bch, y, w_out, g_post, g_a, g_c, conv_w)


def _attn_bwd(qkv, o, do, probs, p_sink, rope, name, exch=None):
    lp = qkv.shape[0]
    nb = lp // BLOCK

    def body(q_ref, kvc_ref, kvp_ref, o_ref, do_ref, p_ref, ps_ref, cq_ref, s1q_ref, s2q_ref, ck_ref, s1k_ref, s2k_ref,
             dq_ref, dkv_ref, dsink_ref, carry):
        i = pl.program_id(0)

        @pl.when(i == 0)
        def _():
            carry[...] = jnp.zeros_like(carry)
            dsink_ref[...] = jnp.zeros((8, 128), F32)

        def finish(tot):
            dk = _rope_t(tot[:, :128], ck_ref[...], s1k_ref[...], s2k_ref[...])
            dkv_ref[:, 0:128] = dk.astype(BF)
            dkv_ref[:, 128:256] = tot[:, 128:].astype(BF)

        @pl.when(i < nb)
        def _():
            tri, _ = _fold_masks(i)
            kvc, kvp = kvc_ref[...], kvp_ref[...]
            kk = jnp.concatenate([kvp[:, :128], kvc[:, :128]], axis=0)
            vv = jnp.concatenate([kvp[:, 128:], kvc[:, 128:]], axis=0)
            lane = lax.broadcasted_iota(jnp.int32, (BLOCK, 128), 1)
            lane2 = lax.broadcasted_iota(jnp.int32, (2 * BLOCK, 128), 1)
            rope_q = (cq_ref[...], s1q_ref[...], s2q_ref[...])
            deltas = jnp.zeros((BLOCK, 128), F32)
            folded = []
            for kvh in range(2):
                c0 = 256 * kvh
                q2 = jnp.concatenate([q_ref[:, c0:c0 + 128], q_ref[:, c0 + 128:c0 + 256]], axis=0)
                do2 = jnp.concatenate([do_ref[:, c0:c0 + 128], do_ref[:, c0 + 128:c0 + 256]], axis=0)
                o2 = jnp.concatenate([o_ref[:, c0:c0 + 128], o_ref[:, c0 + 128:c0 + 256]], axis=0).astype(F32)
                k4, v4 = _kv_operand(kk, kvh), _kv_operand(vv, kvh)
                prod = do2 * o2
                dob = do2.astype(BF)
                dp = _dot_nt(dob, v4)
                ds, pb = [], []
                for half in range(2):
                    heads = [4 * kvh + 2 * pair + half for pair in range(2)]
                    p = jnp.concatenate([p_ref[:, 128 * h:128 * (h + 1)] for h in heads], axis=0)
                    sel = (lane2 < HEAD_DIM) if half == 0 else (lane2 >= HEAD_DIM)
                    delta = jnp.sum(jnp.where(sel, prod, 0.0), axis=-1, keepdims=True)
                    dp_h = dp[:, 2 * half * BLOCK:2 * (half + 1) * BLOCK]
                    ds.append((p.astype(F32) * (jnp.where(tri, dp_h[:, :BLOCK], dp_h[:, BLOCK:]) - delta)).astype(BF))
                    pb.append(p)
                    for pair in range(2):
                        deltas = jnp.where(lane == heads[pair], delta[pair * BLOCK:(pair + 1) * BLOCK], deltas)
                ds4, p4 = _split4(ds, tri), _split4(pb, tri)
                dq2 = _dot(ds4, k4) * SCALE
                dq_ref[:, c0:c0 + 128] = _rope_t(dq2[:BLOCK], *rope_q).astype(BF)
                dq_ref[:, c0 + 128:c0 + 256] = _rope_t(dq2[BLOCK:], *rope_q).astype(BF)
                rk, rv = _dot_tn(ds4, q2), _dot_tn(p4, dob)
                own = (lane < HEAD_DIM) if kvh == 0 else (lane >= HEAD_DIM)
                group = []
                for r in (rk, rv):
                    for blk in range(2):
                        t = jnp.where(lane < HEAD_DIM, r[blk * BLOCK:(blk + 1) * BLOCK], r[(2 + blk) * BLOCK:(3 + blk) * BLOCK])
                        group.append(jnp.where(own, t + pltpu.roll(t, HEAD_DIM, 1), 0.0))
                folded.append(group)
            dsink_ref[ROW_SINK:ROW_SINK + 1, :] -= jnp.sum(ps_ref[...] * deltas, axis=0, keepdims=True)
            dk_p, dk_c, dv_p, dv_c = [folded[0][t] + folded[1][t] for t in range(4)]
            finish(carry[...] + jnp.concatenate([dk_p, dv_p], axis=1))
            carry[...] = jnp.concatenate([dk_c, dv_c], axis=1)

        @pl.when(i == nb)
        def _():
            finish(carry[...])

    qi = lambda i: jnp.minimum(i, nb - 1)
    ki = lambda i: jnp.maximum(i - 1, 0)
    tab_q = pl.BlockSpec((BLOCK, 128), lambda i: (qi(i), 0))
    tab_k = pl.BlockSpec((BLOCK, 128), lambda i: (ki(i), 0))
    return _call(
        body, exch,
        name=name,
        grid=(nb + 1,),
        in_specs=[
            pl.BlockSpec((BLOCK, ATTN_W), lambda i: (qi(i), 0)),
            pl.BlockSpec((BLOCK, 256), lambda i: (qi(i), 2)),
            pl.BlockSpec((BLOCK, 256), lambda i: (jnp.maximum(qi(i) - 1, 0), 2)),
            pl.BlockSpec((BLOCK, ATTN_W), lambda i: (qi(i), 0)),
            pl.BlockSpec((BLOCK, ATTN_W), lambda i: (qi(i), 0)),
            pl.BlockSpec((BLOCK, N_Q_HEADS * BLOCK), lambda i: (qi(i), 0)),
            tab_q, tab_q, tab_q, tab_q, tab_k, tab_k, tab_k,
        ],
        out_specs=[
            pl.BlockSpec((BLOCK, ATTN_W), lambda i: (qi(i), 0)),
            pl.BlockSpec((BLOCK, 256), lambda i: (ki(i), 0)),
            pl.BlockSpec((8, 128), lambda i: (0, 0)),
        ],
        out_shape=[
            jax.ShapeDtypeStruct((lp, ATTN_W), BF),
            jax.ShapeDtypeStruct((lp, 256), BF),
            jax.ShapeDtypeStruct((8, 128), F32),
        ],
        scratch_shapes=[pltpu.VMEM((BLOCK, 256), F32)],
        compiler_params=_params(),
    )(qkv, qkv, qkv, o, do, probs, p_sink, *rope, *rope)


def _in_proj_bwd_dx(dq, dkv, dbch, w_in_t, h, dh2, g, tm, name, exch=None):
    lp = h.shape[0]

    def body(dq_ref, dkv_ref, dbch_ref, w_ref, h_ref, dh2_ref, g_ref, dh_ref, dg_ref):
        i = pl.program_id(0)

        @pl.when(i == 0)
        def _():
            dg_ref[...] = jnp.zeros((8, D_MODEL), F32)

        da = _dot(jnp.concatenate([dq_ref[...], dkv_ref[...], dbch_ref[...]], axis=1), w_ref[...])
        dh, dg = _rms_bwd(h_ref[...], g_ref[...], da)
        dg_ref[ROW_MIX_PRE:ROW_MIX_PRE + 1, :] += dg
        dh_ref[...] = dh2_ref[...] + dh

    row = lambda w: pl.BlockSpec((tm, w), lambda i: (i, 0))
    return _call(
        body, exch,
        name=name,
        grid=(lp // tm,),
        in_specs=[row(ATTN_W), row(256), row(3 * CONV_W), _full((IN_W, D_MODEL)), row(D_MODEL), row(D_MODEL), _full((1, D_MODEL))],
        out_specs=[row(D_MODEL), _full_out((8, D_MODEL))],
        out_shape=[jax.ShapeDtypeStruct((lp, D_MODEL), F32), jax.ShapeDtypeStruct((8, D_MODEL), F32)],
        compiler_params=_params(),
    )(dq, dkv, dbch, w_in_t, h, dh2, g)


def _mix_bwd_dw(dq, dkv, dbch, a, tm, name, exch=None):
    lp = a.shape[0]
    nt = lp // tm

    def body(dq_ref, dkv_ref, dbch_ref, a_ref, dwi_ref, acci):
        i = pl.program_id(0)

        @pl.when(i == 0)
        def _():
            acci[...] = jnp.zeros_like(acci)

        a_v = a_ref[...]
        acci[0:512, :] += _dot_tn(dq_ref[...], a_v)
        acci[512:768, :] += _dot_tn(dkv_ref[...], a_v)
        acci[768:, :] += _dot_tn(dbch_ref[...], a_v)

        @pl.when(i == nt - 1)
        def _():
            dwi_ref[...] = acci[...].astype(BF)

    row = lambda w: pl.BlockSpec((tm, w), lambda i: (i, 0))
    return _call(
        body, exch,
        name=name,
        grid=(nt,),
        in_specs=[row(ATTN_W), row(256), row(3 * CONV_W), row(D_MODEL)],
        out_specs=[_full_out((IN_W, D_MODEL))],
        out_shape=[jax.ShapeDtypeStruct((IN_W, D_MODEL), BF)],
        scratch_shapes=[pltpu.VMEM((IN_W, D_MODEL), F32)],
        compiler_params=_params(),
    )(dq, dkv, dbch, a)


def _mesh_place():
    x, y, c = lax.axis_index("x"), lax.axis_index("y"), lax.axis_index("c")
    return x, y, c, 4 * x + 2 * y + c


def _peer(x, y, c, k):
    px = 1 - x if k & 4 else x
    py = 1 - y if k & 2 else y
    pc = 1 - c if k & 1 else c
    return (px, py, pc), 4 * px + 2 * py + pc


SIBLING = 1
SAME_CORE = (2, 4, 6)
OTHER_CORE = (3, 5, 7)


class _Exchange:
    def __init__(self, pieces):
        self.srcs = [s for s, _ in pieces]
        self.to_all = [g for _, g in pieces]
        self.n = len(pieces)
        self.land_shapes = [
            jax.ShapeDtypeStruct((N_DEV,) + (s.shape if g else s.shape[1:]), s.dtype) for s, g in pieces]
        self.sem_shapes = [pltpu.SemaphoreType.DMA((self.n, N_DEV - 1)), pltpu.SemaphoreType.DMA((self.n, N_DEV - 1)),
                           pltpu.SemaphoreType.DMA((self.n,))]
        self.forwards = any(self.to_all)

    def _ops(self, srcs, lands, sems):
        send_sems, recv_sems, local_sems = sems
        x, y, c, me = _mesh_place()

        def remote(p, k, src, slot, to):
            return pltpu.make_async_remote_copy(
                src_ref=src, dst_ref=lands[p].at[slot], send_sem=send_sems.at[p, k - 1], recv_sem=recv_sems.at[p, k - 1],
                device_id=to, device_id_type=MESH)

        def own(p):
            return pltpu.make_async_copy(srcs[p] if self.to_all[p] else srcs[p].at[me], lands[p].at[me], local_sems.at[p])

        def direct(p, k):
            peer, pidx = _peer(x, y, c, k)
            return remote(p, k, srcs[p] if self.to_all[p] else srcs[p].at[pidx], me, peer)

        def forward(p, k):
            sibling, _ = _peer(x, y, c, SIBLING)
            _, origin = _peer(x, y, c, k ^ SIBLING)
            return remote(p, k, lands[p].at[origin], origin, sibling)

        def arrival(p, k):
            peer, pidx = _peer(x, y, c, k)
            return remote(p, k, lands[p].at[pidx], pidx, peer)

        return own, direct, forward, arrival

    def start(self, srcs, lands, sems):
        own, direct, _, _ = self._ops(srcs, lands, sems)
        for p in range(self.n):
            own(p).start()
            for k in ((SIBLING,) + SAME_CORE) if self.to_all[p] else range(1, N_DEV):
                direct(p, k).start()

    def forward(self, srcs, lands, sems):
        _, _, forward, arrival = self._ops(srcs, lands, sems)
        for p in range(self.n):
            if self.to_all[p]:
                for k in SAME_CORE:
                    arrival(p, k).wait_recv()
                    forward(p, k ^ SIBLING).start()

    def finish(self, srcs, lands, sems):
        own, direct, forward, arrival = self._ops(srcs, lands, sems)
        for p in range(self.n):
            for k in ((SIBLING,) + OTHER_CORE) if self.to_all[p] else range(1, N_DEV):
                arrival(p, k).wait_recv()
        for p in range(self.n):
            for k in range(1, N_DEV):
                (forward(p, k) if self.to_all[p] and k in OTHER_CORE else direct(p, k)).wait_send()
            own(p).wait()


def _call(body, exch, *, name, grid, in_specs, out_specs, out_shape, scratch_shapes=(), compiler_params, after=None):
    if exch is None:
        return pl.pallas_call(body, name=name, grid=grid, in_specs=in_specs, out_specs=out_specs, out_shape=out_shape,
                              scratch_shapes=scratch_shapes, compiler_params=compiler_params)
    n_in, n_out, n_scr, n_x = len(in_specs), len(out_shape), len(scratch_shapes), exch.n
    steps = math.prod(grid)

    def carrying(*refs):
        a, b, c, d, e = n_in, n_in + n_x, n_in + n_x + n_out, n_in + 2 * n_x + n_out, n_in + 2 * n_x + n_out + n_scr
        ins, srcs, outs, lands, scr, sems = refs[:a], refs[a:b], refs[b:c], refs[c:d], refs[d:e], refs[e:]
        step = functools.reduce(lambda acc, t: acc * grid[t] + pl.program_id(t), range(len(grid)), 0)

        @pl.when(step == 0)
        def _():
            exch.start(srcs, lands, sems)

        body(*ins, *outs, *scr)

        if exch.forwards:
            @pl.when(step == max(0, steps - 1 - (steps + 7) // 8))
            def _():
                exch.forward(srcs, lands, sems)

        @pl.when(step == steps - 1)
        def _():
            exch.finish(srcs, lands, sems)
            if after is not None:
                after(lands, *ins, *outs, *scr)

    hbm = pl.BlockSpec(memory_space=pl.ANY)
    call = pl.pallas_call(
        carrying, name=name, grid=grid, in_specs=list(in_specs) + [hbm] * n_x, out_specs=list(out_specs) + [hbm] * n_x,
        out_shape=list(out_shape) + exch.land_shapes, scratch_shapes=list(scratch_shapes) + exch.sem_shapes,
        compiler_params=compiler_params)

    def run(*args):
        res = call(*args, *exch.srcs)
        return list(res[:n_out]), list(res[n_out:])

    return run


def _sum_small(part):
    exch = _Exchange([(part, True)])

    def body(part_ref, out_ref, land, *sems):
        exch.start([part_ref], [land], sems)
        exch.forward([part_ref], [land], sems)
        exch.finish([part_ref], [land], sems)
        acc = land[0]
        for d in range(1, N_DEV):
            acc = acc + land[d]
        out_ref[...] = acc

    vmem = pl.BlockSpec(memory_space=pltpu.VMEM)
    return pl.pallas_call(
        body,
        name="sum_small",
        in_specs=[vmem],
        out_specs=vmem,
        out_shape=jax.ShapeDtypeStruct(part.shape, F32),
        scratch_shapes=[pltpu.VMEM(exch.land_shapes[0].shape, F32)] + exch.sem_shapes,
    )(part)


def _adamw(w, g, m, v):
    m = ADAM_B1 * m + (1.0 - ADAM_B1) * g
    v = ADAM_B2 * v + (1.0 - ADAM_B2) * jnp.square(g)
    m_hat = m / (1.0 - ADAM_B1 ** ADAM_STEP)
    v_hat = v / (1.0 - ADAM_B2 ** ADAM_STEP)
    delta = -ADAM_LR * (m_hat / (jnp.sqrt(v_hat) + ADAM_EPS) + ADAM_WD * w)
    return delta, m, v


def _landed_specs(tr, wd):
    return [pl.BlockSpec((N_DEV, tr, wd), lambda l, i, ll=ll: (0, jnp.where(l == ll, i, 0), 0)) for ll in range(DEPTH)]


def _device_sum(r_ref):
    acc = r_ref[0].astype(F32)
    for d in range(1, N_DEV):
        acc = acc + r_ref[d].astype(F32)
    return acc


def _sum_adamw(recv, w, m, v, tr, name, transposed=False):
    _, r, wd = recv[0].shape

    def body(*refs):
        w_ref, m_ref, v_ref, g_ref, d_ref, mo_ref, vo_ref = refs[DEPTH:]
        for ll in range(DEPTH):
            @pl.when(pl.program_id(0) == ll)
            def _(ll=ll):
                g = _device_sum(refs[ll])
                g = g.T if transposed else g
                g_ref[0] = g
                d_ref[0], mo_ref[0], vo_ref[0] = _adamw(w_ref[0], g, m_ref[0], v_ref[0])

    if transposed:
        blk = pl.BlockSpec((1, wd, tr), lambda l, i: (l, 0, i))
        shape = jax.ShapeDtypeStruct((DEPTH, wd, r), F32)
    else:
        blk = pl.BlockSpec((1, tr, wd), lambda l, i: (l, i, 0))
        shape = jax.ShapeDtypeStruct((DEPTH, r, wd), F32)
    return pl.pallas_call(
        body,
        name=name,
        grid=(DEPTH, r // tr),
        in_specs=_landed_specs(tr, wd) + [blk, blk, blk],
        out_specs=[blk] * 4,
        out_shape=[shape] * 4,
        compiler_params=_params(("arbitrary", "arbitrary")),
    )(*recv, w, m, v)


def _adamw_small(ws, gs, ms, vs):
    n = len(ws)

    def body(*refs):
        w_r, g_r, m_r, v_r = refs[:n], refs[n:2 * n], refs[2 * n:3 * n], refs[3 * n:4 * n]
        d_o, m_o, v_o = refs[4 * n:5 * n], refs[5 * n:6 * n], refs[6 * n:7 * n]
        for t in range(n):
            d_o[t][...], m_o[t][...], v_o[t][...] = _adamw(w_r[t][...], g_r[t][...], m_r[t][...], v_r[t][...])

    vmem = pl.BlockSpec(memory_space=pltpu.VMEM)
    shapes = [jax.ShapeDtypeStruct(w.shape, F32) for w in ws]
    outs = pl.pallas_call(
        body,
        name="adamw_small",
        in_specs=[vmem] * (4 * n),
        out_specs=[vmem] * (3 * n),
        out_shape=shapes * 3,
    )(*ws, *gs, *ms, *vs)
    return outs[:n], outs[n:2 * n], outs[2 * n:]


def kernel(x, meta_tokens, mix_pre_g, w_in, conv_w, sinks, attn_out_g, conv_out_g, w_out, mix_post_g, mlp_pre_g, w_up, w_down, mlp_post_g, loss_target, m_meta_tokens, m_mix_pre_g, m_w_in, m_conv_w, m_sinks, m_attn_out_g, m_conv_out_g, m_w_out, m_mix_post_g, m_mlp_pre_g, m_w_up, m_w_down, m_mlp_post_g, v_meta_tokens, v_mix_pre_g, v_w_in, v_conv_w, v_sinks, v_attn_out_g, v_conv_out_g, v_w_out, v_mix_post_g, v_mlp_pre_g, v_w_up, v_w_down, v_mlp_post_g):
    seq = x.shape[1]
    lp = BLOCK + seq
    tm = _row_tile(lp)
    tm_mlp = _row_tile(lp, (320, 256, 128))
    tm_dw_mlp = _row_tile(lp, (1664, 1040, 640, 384, 256, 128))
    tm_dw_mix = _row_tile(lp, (1664, 832, 640, 384, 256, 128))
    me = 4 * lax.axis_index("x") + 2 * lax.axis_index("y") + lax.axis_index("c")
    cshard = CONV_W // N_DEV
    mshard = D_MODEL // N_DEV

    gather_with = {
        ("in_proj_fwd", 0): [("in", 1)], ("attn_fwd", 0): [("out", 0), ("up", 0)], ("mix_out_fwd", 0): [("down", 0)],
        ("mlp_fwd", 0): [("out", 1), ("up", 1), ("down", 1)],
    }
    scatter_with = {
        ("attn_bwd", 1): [("down", 1)], ("mix_bwd_dw", 1): [("out", 1)], ("mlp_bwd_dx", 0): [("up", 1), ("in", 1)],
        ("mix_out_bwd", 0): [("up", 0)], ("attn_bwd", 0): [("down", 0)], ("mix_bwd_dw", 0): [("out", 0)],
        ("in_proj_bwd_dx", 0): [("in", 0)],
    }
    shard = {"in": jnp.swapaxes(w_in, 1, 2).astype(BF), "out": w_out.astype(BF),
             "up": jnp.swapaxes(w_up, 1, 2).astype(BF), "down": w_down.astype(BF)}
    weight = {}
    grad = {}
    landed = {}

    def run(fn, kind, l, *args):
        key, name = (kind, l), f"{kind}_{l}"
        if key in gather_with:
            blocks = gather_with[key]
            outs, lands = fn(*args, name, _Exchange([(shard[n][k], True) for n, k in blocks]))
            for b, land in zip(blocks, lands):
                weight[b] = land.reshape(-1, D_MODEL)
            return outs
        if key in scatter_with:
            blocks = scatter_with[key]
            outs, lands = fn(*args, name, _Exchange([(grad[b].reshape(N_DEV, -1, D_MODEL), False) for b in blocks]))
            landed.update(zip(blocks, lands))
            return outs
        return fn(*args, name)

    small = jnp.zeros((24, 128), F32)
    small = small.at[0:N_META, :].set(meta_tokens)
    small = small.at[N_META:N_META + 6, 0:cshard].set(conv_w.reshape(6, cshard))
    first = _Exchange([(shard["in"][0], True), (small, True)])
    h, rope, (first_in, g_small) = _build_h(x[0], _rope_table(lp), tm, first, 1, "build_h")
    weight[("in", 0)] = first_in.reshape(-1, D_MODEL)
    cw = g_small[:, N_META:N_META + 6, 0:cshard].reshape(N_DEV, DEPTH, 3, cshard)
    cw = jnp.transpose(cw, (1, 2, 0, 3)).reshape(DEPTH, 3, CONV_W)
    conv_full = jnp.concatenate([cw, jnp.zeros((DEPTH, 5, CONV_W), F32)], axis=1)

    row1 = lambda a, l: a[l].reshape(1, -1)

    saved = []
    for l in range(DEPTH):
        a, qkv, bch = run(_in_proj_fwd, "in_proj_fwd", l, h, row1(mix_pre_g, l), weight[("in", l)], rope, tm)
        y_attn, probs, p_sink = run(_attn_fwd, "attn_fwd", l, qkv, row1(sinks, l))
        yc, y, z, h2 = run(_mix_out_fwd, "mix_out_fwd", l, bch, y_attn, h, conv_full[l], row1(attn_out_g, l),
                       row1(conv_out_g, l), weight[("out", l)], row1(mix_post_g, l), tm)
        mlp = _mlp_fwd if l < DEPTH - 1 else functools.partial(_mlp_fwd, target=loss_target[0])
        a2, up, f, *rest = run(mlp, "mlp_fwd", l, h2, row1(mlp_pre_g, l), weight[("up", l)], weight[("down", l)],
                               row1(mlp_post_g, l), tm_mlp)
        saved.append((h, a, qkv, bch, y_attn, probs, p_sink, yc, y, z, h2, a2, up, f))
        h = rest[0]
    dh, loss_part = rest[0], rest[1][0, 0] * (0.5 / D_MODEL)

    gsmall = [None] * DEPTH
    for l in reversed(range(DEPTH)):
        h0, a, qkv, bch, y_attn, probs, p_sink, yc, y, z, h2, a2, up, f = saved[l]
        df, dup, dh2, dg_mlp = run(_mlp_bwd_dx, "mlp_bwd_dx", l, dh, f, up, h2, weight[("down", l)], weight[("up", l)],
                                   row1(mlp_post_g, l), row1(mlp_pre_g, l), tm_mlp)
        grad[("down", l)], grad[("up", l)] = _mlp_bwd_dw(up, df, dup, a2, tm_dw_mlp, f"mlp_bwd_dw_{l}")
        dya, dbch, dg_mix, grad[("out", l)] = run(
            _mix_out_bwd, "mix_out_bwd", l, dh2, z, y_attn, yc, bch, y, weight[("out", l)], row1(mix_post_g, l),
            row1(attn_out_g, l), row1(conv_out_g, l), conv_full[l], tm)
        dq, dkv, dsink = run(_attn_bwd, "attn_bwd", l, qkv, y_attn, dya, probs, p_sink, rope)
        grad[("in", l)], = run(_mix_bwd_dw, "mix_bwd_dw", l, dq, dkv, dbch, a, tm_dw_mix)
        dh, dg_in = run(_in_proj_bwd_dx, "in_proj_bwd_dx", l, dq, dkv, dbch, weight[("in", l)], h0, dh2,
                        row1(mix_pre_g, l), tm)
        tile_a = dg_mlp + dg_in + jnp.pad(dsink, ((0, 0), (0, D_MODEL - 128)))
        gsmall[l] = (tile_a, dg_mix)
    grad_x = dh[BLOCK:][None]

    loss_tile = jnp.zeros((8, D_MODEL), F32).at[ROW_LOSS, 0].set(loss_part)
    tot = _sum_small(jnp.concatenate(
        [gsmall[0][0] + loss_tile, gsmall[0][1], gsmall[1][0], gsmall[1][1], dh[LEAD_PAD:BLOCK]], axis=0))
    loss = tot[ROW_LOSS, 0]
    ta = [tot[16 * l:16 * l + 8] for l in range(DEPTH)]
    tb = [tot[16 * l + 8:16 * l + 16] for l in range(DEPTH)]
    pick = lambda tiles, r0, r1, c0, c1: jnp.stack([t[r0:r1, c0:c1] for t in tiles])
    g_mlp_post = pick(ta, ROW_MLP_POST, ROW_MLP_POST + 1, 0, D_MODEL).reshape(DEPTH, D_MODEL)
    g_mlp_pre = pick(ta, ROW_MLP_PRE, ROW_MLP_PRE + 1, 0, D_MODEL).reshape(DEPTH, D_MODEL)
    g_mix_pre = pick(ta, ROW_MIX_PRE, ROW_MIX_PRE + 1, 0, D_MODEL).reshape(DEPTH, D_MODEL)
    g_sinks = pick(ta, ROW_SINK, ROW_SINK + 1, 0, N_Q_HEADS).reshape(DEPTH, N_Q_HEADS)
    g_mix_post = pick(tb, ROW_MIX_POST, ROW_MIX_POST + 1, 0, D_MODEL).reshape(DEPTH, D_MODEL)
    g_attn_out = pick(tb, ROW_GROUP_G, ROW_GROUP_G + 1, 0, ATTN_W).reshape(DEPTH, ATTN_W)
    g_conv_out = pick(tb, ROW_GROUP_G, ROW_GROUP_G + 1, ATTN_W, D_MODEL).reshape(DEPTH, CONV_W)
    g_conv_full = pick(tb, ROW_CONV, ROW_CONV + 3, 0, CONV_W)
    g_conv = lax.dynamic_slice_in_dim(g_conv_full, me * cshard, cshard, axis=2)
    g_meta = lax.dynamic_slice_in_dim(tot[16 * DEPTH:16 * DEPTH + N_META], me * mshard, mshard, axis=1)

    r_in, r_out, r_up, r_down = [[landed[(n, l)] for l in range(DEPTH)] for n in ("in", "out", "up", "down")]
    t12 = lambda a: jnp.swapaxes(a, 1, 2)
    g_w_in, d_w_in, nm_w_in, nv_w_in = map(t12, _sum_adamw(r_in, t12(w_in), t12(m_w_in), t12(v_w_in), 96, "adamw_w_in"))
    g_w_up, d_w_up, nm_w_up, nv_w_up = _sum_adamw(r_up, w_up, m_w_up, v_w_up, 128, "adamw_w_up", transposed=True)
    g_w_out, d_w_out, nm_w_out, nv_w_out = _sum_adamw(r_out, w_out, m_w_out, v_w_out, 128, "adamw_w_out")
    g_w_down, d_w_down, nm_w_down, nv_w_down = _sum_adamw(r_down, w_down, m_w_down, v_w_down, 128, "adamw_w_down")

    ws = [meta_tokens, mix_pre_g, conv_w.reshape(6, cshard), sinks, attn_out_g, conv_out_g, mix_post_g, mlp_pre_g, mlp_post_g]
    gs = [g_meta, g_mix_pre, g_conv.reshape(6, cshard), g_sinks, g_attn_out, g_conv_out, g_mix_post, g_mlp_pre, g_mlp_post]
    ms = [m_meta_tokens, m_mix_pre_g, m_conv_w.reshape(6, cshard), m_sinks, m_attn_out_g, m_conv_out_g, m_mix_post_g,
          m_mlp_pre_g, m_mlp_post_g]
    vs = [v_meta_tokens, v_mix_pre_g, v_conv_w.reshape(6, cshard), v_sinks, v_attn_out_g, v_conv_out_g, v_mix_post_g,
          v_mlp_pre_g, v_mlp_post_g]
    ds, nms, nvs = _adamw_small(ws, gs, ms, vs)

    def order(meta, mix_pre, cv, sk, a_out, c_out, mix_post, mlp_pre, mlp_post, win, wout, wup, wdown):
        return [meta, mix_pre, win, cv.reshape(DEPTH, 3, cshard), sk, a_out, c_out, wout, mix_post, mlp_pre, wup, wdown, mlp_post]

    grads = order(*gs, g_w_in, g_w_out, g_w_up, g_w_down)
    deltas = order(*ds, d_w_in, d_w_out, d_w_up, d_w_down)
    new_m = order(*nms, nm_w_in, nm_w_out, nm_w_up, nm_w_down)
    new_v = order(*nvs, nv_w_in, nv_w_out, nv_w_up, nv_w_down)
    return (loss, grad_x, *grads, *deltas, *new_m, *new_v)
```

```python
import functools
import math

import jax
import jax.numpy as jnp
from jax import lax
from jax.experimental import pallas as pl
from jax.experimental.pallas import tpu as pltpu

F32 = jnp.float32
BF = jnp.bfloat16

D_MODEL = 1024
ATTN_W = 512
CONV_W = 512
HEAD_DIM = 64
N_Q_HEADS = 8
ROT_DIM = 16
D_FF = 4096
IN_W = 2304
N_META = 16
BLOCK = 128
LEAD_PAD = BLOCK - N_META
ROPE_THETA = 500000.0
EPS = 1e-6
N_DEV = 8
DEPTH = 2
NEG = -1e30
SCALE = HEAD_DIM ** -0.5

ADAM_LR = 0.001
ADAM_B1 = 0.9
ADAM_B2 = 0.999
ADAM_EPS = 1e-08
ADAM_WD = 0.01
ADAM_STEP = 10

ROW_MLP_POST, ROW_MLP_PRE, ROW_MIX_PRE, ROW_SINK, ROW_LOSS = 0, 1, 2, 3, 4
ROW_MIX_POST, ROW_GROUP_G, ROW_CONV = 0, 1, 2

VMEM_LIMIT = 56 * 1024 * 1024
MESH = pl.DeviceIdType.MESH


def _dot(a, b):
    return jnp.dot(a, b, preferred_element_type=F32)


def _dot_nt(a, b):
    return lax.dot_general(a, b, (((1,), (1,)), ((), ())), preferred_element_type=F32)


def _dot_tn(a, b):
    return lax.dot_general(a, b, (((0,), (0,)), ((), ())), preferred_element_type=F32)


def _rms_fwd(x, g):
    r = lax.rsqrt(jnp.mean(x * x, axis=-1, keepdims=True) + EPS)
    return x * r * g


def _rms_bwd(x, g, dy):
    r = lax.rsqrt(jnp.mean(x * x, axis=-1, keepdims=True) + EPS)
    xh = x * r
    t = dy * g
    dx = r * (t - xh * jnp.mean(t * xh, axis=-1, keepdims=True))
    dg = jnp.sum(dy * xh, axis=0, keepdims=True)
    return dx, dg


def _row_tile(lp, cands=(640, 512, 384, 256, 128)):
    for t in cands:
        if lp % t == 0:
            return t
    raise ValueError(f"row count {lp} is not a multiple of 128")


def _full(shape):
    n = len(shape)
    return pl.BlockSpec(shape, lambda *_: (0,) * n, pipeline_mode=pl.Buffered(1))


def _full_out(shape):
    n = len(shape)
    return pl.BlockSpec(shape, lambda *_: (0,) * n)


def _params(sem=("arbitrary",)):
    return pltpu.CompilerParams(dimension_semantics=sem, vmem_limit_bytes=VMEM_LIMIT)


def _rope_table(lp):
    half = ROT_DIM // 2
    pos = jnp.maximum(jnp.arange(lp) - LEAD_PAD, 0).astype(F32)
    inv_freq = jnp.power(jnp.float32(ROPE_THETA), -jnp.arange(0, ROT_DIM, 2, dtype=F32) / ROT_DIM)
    ang_t = jnp.concatenate([inv_freq, inv_freq])[:, None] * pos[None, :]
    row = lax.broadcasted_iota(jnp.int32, (ROT_DIM, lp), 0)
    cs_t = jnp.where(row < half, jnp.cos(ang_t), jnp.sin(ang_t))
    return jnp.pad(cs_t.T, ((0, 0), (0, 128 - ROT_DIM)))


def _rope_coeffs(t):
    half = ROT_DIM // 2
    lane = lax.broadcasted_iota(jnp.int32, t.shape, 1)
    cos_a = jnp.where(lane < half, t, 0.0)
    sin_a = pltpu.roll(jnp.where((lane >= half) & (lane < ROT_DIM), t, 0.0), 128 - half, 1)
    c = cos_a + pltpu.roll(cos_a, half, 1) + jnp.where((lane >= ROT_DIM) & (lane < HEAD_DIM), 1.0, 0.0)
    s2 = pltpu.roll(sin_a, half, 1)
    both = lambda u: u + pltpu.roll(u, HEAD_DIM, 1)
    return both(c), both(-sin_a), both(s2)


def _rope(t, c, s1, s2):
    return t * c + pltpu.roll(t, BLOCK - 8, 1) * s1 + pltpu.roll(t, 8, 1) * s2


def _rope_t(dt, c, s1, s2):
    return dt * c + pltpu.roll(dt * s1, 8, 1) + pltpu.roll(dt * s2, BLOCK - 8, 1)


def _build_h(x, rope_compact, tm, exch, small_piece, name):
    seq = x.shape[0]
    lp = BLOCK + seq
    nt = lp // tm
    n_sub = tm // BLOCK
    small_shape = exch.land_shapes[small_piece].shape

    def body(*refs):
        h_ref, c_ref, s1_ref, s2_ref = refs[n_sub + 1:n_sub + 5]
        for j in range(n_sub):
            h_ref[j * BLOCK:(j + 1) * BLOCK, :] = refs[j][...]
        c_ref[...], s1_ref[...], s2_ref[...] = _rope_coeffs(refs[n_sub][...])

    def after(lands, *refs):
        h_ref, buf = refs[n_sub + 1], refs[n_sub + 5]
        pltpu.sync_copy(lands[small_piece], buf)
        h_ref[0:LEAD_PAD, :] = jnp.zeros((LEAD_PAD, D_MODEL), F32)
        for d in range(N_DEV):
            h_ref[LEAD_PAD:BLOCK, d * 128:(d + 1) * 128] = buf[d, 0:N_META, :]

    tile = lambda i: (i + 1) % nt
    piece = lambda j: pl.BlockSpec((BLOCK, D_MODEL), lambda i: (jnp.maximum(tile(i) * n_sub + j - 1, 0), 0))
    rows = lambda w: pl.BlockSpec((tm, w), lambda i: (tile(i), 0))
    (h, *rope), lands = _call(
        body, exch,
        name=name,
        grid=(nt,),
        in_specs=[piece(j) for j in range(n_sub)] + [rows(128)],
        out_specs=[rows(D_MODEL)] + [rows(128)] * 3,
        out_shape=[jax.ShapeDtypeStruct((lp, D_MODEL), F32)] + [jax.ShapeDtypeStruct((lp, 128), F32)] * 3,
        scratch_shapes=[pltpu.VMEM(small_shape, F32)],
        compiler_params=_params(),
        after=after,
    )(*([x] * n_sub), rope_compact)
    return h, rope, lands


def _in_proj_fwd(h, g, w_in_t, rope, tm, name, exch=None):
    lp = h.shape[0]

    def body(h_ref, g_ref, w_ref, c_ref, s1_ref, s2_ref, a_ref, qkv_ref, bch_ref):
        a = _rms_fwd(h_ref[...], g_ref[...]).astype(BF)
        a_ref[...] = a
        proj = _dot_nt(a, w_ref[...])
        c, s1, s2 = c_ref[...], s1_ref[...], s2_ref[...]
        for j in range(5):
            t = _rope(proj[:, j * 128:(j + 1) * 128], c, s1, s2)
            qkv_ref[:, j * 128:(j + 1) * 128] = (t * SCALE if j < 4 else t).astype(BF)
        qkv_ref[:, 640:768] = proj[:, 640:768].astype(BF)
        bch_ref[...] = proj[:, 768:].astype(BF)

    row = lambda w: pl.BlockSpec((tm, w), lambda i: (i, 0))
    return _call(
        body, exch,
        name=name,
        grid=(lp // tm,),
        in_specs=[row(D_MODEL), _full((1, D_MODEL)), _full((IN_W, D_MODEL)), row(128), row(128), row(128)],
        out_specs=[row(D_MODEL), row(768), row(3 * CONV_W)],
        out_shape=[
            jax.ShapeDtypeStruct((lp, D_MODEL), BF),
            jax.ShapeDtypeStruct((lp, 768), BF),
            jax.ShapeDtypeStruct((lp, 3 * CONV_W), BF),
        ],
        compiler_params=_params(),
    )(h, g, w_in_t, *rope)


def _fold_masks(i):
    r = lax.broadcasted_iota(jnp.int32, (2 * BLOCK, BLOCK), 0) & (BLOCK - 1)
    c = lax.broadcasted_iota(jnp.int32, (2 * BLOCK, BLOCK), 1)
    tri = c > r
    ok = jnp.where(tri, (i - 1) * BLOCK + c, i * BLOCK + c) >= LEAD_PAD
    return tri, ok


def _kv_operand(x, kvh):
    lane = lax.broadcasted_iota(jnp.int32, x.shape, 1)
    zero = jnp.zeros_like(x)
    if kvh == 0:
        lo = jnp.where(lane < HEAD_DIM, x, zero)
        hi = pltpu.roll(lo, HEAD_DIM, 1)
    else:
        hi = jnp.where(lane >= HEAD_DIM, x, zero)
        lo = pltpu.roll(hi, HEAD_DIM, 1)
    return jnp.concatenate([lo, hi], axis=0)


def _split4(t, tri):
    zero = jnp.zeros_like(t[0])
    return jnp.concatenate(
        [jnp.where(tri, t[0], zero), jnp.where(tri, zero, t[0]), jnp.where(tri, t[1], zero), jnp.where(tri, zero, t[1])], axis=1)


def _sink_cols(sink_ref, kvh):
    first = lax.broadcasted_iota(jnp.int32, (2 * BLOCK, 1), 0) < BLOCK
    return [jnp.where(first, sink_ref[0, 4 * kvh + half], sink_ref[0, 4 * kvh + 2 + half]) for half in range(2)]


def _folded_exp(q2, k4, tri, ok, sks):
    s = _dot_nt(q2, k4)
    es, ss = [], []
    for half in range(2):
        s_h = s[:, 2 * half * BLOCK:2 * (half + 1) * BLOCK]
        sf = jnp.where(ok, jnp.where(tri, s_h[:, :BLOCK], s_h[:, BLOCK:]), NEG)
        m = jnp.maximum(jnp.max(sf, axis=-1, keepdims=True), sks[half])
        es.append(jnp.exp(sf - m))
        ss.append(jnp.exp(sks[half] - m))
    sums = _dot(jnp.concatenate(es, axis=0).astype(BF), jnp.ones((BLOCK, BLOCK), BF))
    invs = [1.0 / (sums[2 * half * BLOCK:2 * (half + 1) * BLOCK] + ss[half]) for half in range(2)]
    return es, ss, invs


def _attn_fwd(qkv, sink, name, exch=None):
    lp = qkv.shape[0]
    nb = lp // BLOCK
    per_step = 2

    def one_block(i, sink_ref, q_ref, kvc_ref, kvp_ref, o_ref, p_ref, ps_ref):
        tri, ok = _fold_masks(i)
        kvc, kvp = kvc_ref[...], kvp_ref[...]
        kk = jnp.concatenate([kvp[:, :128], kvc[:, :128]], axis=0)
        vv = jnp.concatenate([kvp[:, 128:], kvc[:, 128:]], axis=0)
        lane = lax.broadcasted_iota(jnp.int32, (BLOCK, 128), 1)
        p_sink = jnp.zeros((BLOCK, 128), F32)
        for kvh in range(2):
            q2 = jnp.concatenate([q_ref[:, 256 * kvh:256 * kvh + 128], q_ref[:, 256 * kvh + 128:256 * kvh + 256]], axis=0)
            es, ss, invs = _folded_exp(q2, _kv_operand(kk, kvh), tri, ok, _sink_cols(sink_ref, kvh))
            pb = [(es[half] * invs[half]).astype(BF) for half in range(2)]
            out = _dot(_split4(pb, tri), _kv_operand(vv, kvh))
            for pair in range(2):
                rows = slice(pair * BLOCK, (pair + 1) * BLOCK)
                o_ref[:, 256 * kvh + 128 * pair:256 * kvh + 128 * (pair + 1)] = out[rows].astype(BF)
                for half in range(2):
                    head = 4 * kvh + 2 * pair + half
                    p_ref[:, 128 * head:128 * (head + 1)] = pb[half][rows]
                    p_sink = jnp.where(lane == head, (ss[half] * invs[half][:, 0:1])[rows], p_sink)
        ps_ref[...] = p_sink

    def body(sink_ref, *refs):
        q_refs, kv_refs = refs[:per_step], refs[per_step:2 * per_step + 1]
        o_ref, p_ref, ps_ref = refs[2 * per_step + 1:]
        for j in range(per_step):
            rows = slice(j * BLOCK, (j + 1) * BLOCK)
            one_block(per_step * pl.program_id(0) + j, sink_ref, q_refs[j], kv_refs[j + 1], kv_refs[j],
                      o_ref.at[rows], p_ref.at[rows], ps_ref.at[rows])

    last = nb - 1
    blk = lambda j: (lambda s: jnp.minimum(per_step * s + j, last))
    out_rows = lambda w: pl.BlockSpec((per_step * BLOCK, w), lambda s: (s, 0))
    return _call(
        body, exch,
        name=name,
        grid=(pl.cdiv(nb, per_step),),
        in_specs=[pl.BlockSpec(memory_space=pltpu.SMEM)]
        + [pl.BlockSpec((BLOCK, ATTN_W), lambda s, j=j: (blk(j)(s), 0)) for j in range(per_step)]
        + [pl.BlockSpec((BLOCK, 256), lambda s: (jnp.maximum(per_step * s - 1, 0), 2))]
        + [pl.BlockSpec((BLOCK, 256), lambda s, j=j: (blk(j)(s), 2)) for j in range(per_step)],
        out_specs=[out_rows(ATTN_W), out_rows(N_Q_HEADS * BLOCK), out_rows(128)],
        out_shape=[jax.ShapeDtypeStruct((lp, ATTN_W), BF), jax.ShapeDtypeStruct((lp, N_Q_HEADS * BLOCK), BF),
                   jax.ShapeDtypeStruct((lp, 128), F32)],
        compiler_params=_params(),
    )(sink, *([qkv] * (2 * per_step + 1)))


def _mix_out_fwd(bch, y_attn, h, conv_w, g_a, g_c, w_out, g_post, tm, name, exch=None):
    lp = h.shape[0]

    def body(bch_ref, ya_ref, h_ref, cw_ref, ga_ref, gc_ref, w_ref, gp_ref, yc_ref, y_ref, z_ref, h2_ref, ext):
        i = pl.program_id(0)

        @pl.when(i == 0)
        def _():
            ext[0:8, :] = jnp.zeros((8, CONV_W), F32)

        b = bch_ref[:, 0:CONV_W].astype(F32)
        u = bch_ref[:, CONV_W:2 * CONV_W].astype(F32) * bch_ref[:, 2 * CONV_W:3 * CONV_W].astype(F32)
        ext[8:8 + tm, :] = u
        yc = cw_ref[0:1, :] * ext[6:6 + tm, :] + cw_ref[1:2, :] * ext[7:7 + tm, :] + cw_ref[2:3, :] * u
        ext[0:8, :] = u[tm - 8:tm, :]
        yc_ref[...] = yc.astype(BF)
        ya = _rms_fwd(ya_ref[...].astype(F32), ga_ref[...]).astype(BF)
        yb = _rms_fwd(b * yc, gc_ref[...]).astype(BF)
        y_ref[:, 0:ATTN_W] = ya
        y_ref[:, ATTN_W:] = yb
        z = _dot(ya, w_ref[0:ATTN_W, :]) + _dot(yb, w_ref[ATTN_W:, :])
        z_ref[...] = z.astype(BF)
        h2_ref[...] = h_ref[...] + _rms_fwd(z, gp_ref[...])

    row = lambda w: pl.BlockSpec((tm, w), lambda i: (i, 0))
    return _call(
        body, exch,
        name=name,
        grid=(lp // tm,),
        in_specs=[
            row(3 * CONV_W), row(ATTN_W), row(D_MODEL), _full((8, CONV_W)), _full((1, ATTN_W)), _full((1, CONV_W)),
            _full((D_MODEL, D_MODEL)), _full((1, D_MODEL)),
        ],
        out_specs=[row(CONV_W), row(D_MODEL), row(D_MODEL), row(D_MODEL)],
        out_shape=[
            jax.ShapeDtypeStruct((lp, CONV_W), BF),
            jax.ShapeDtypeStruct((lp, D_MODEL), BF),
            jax.ShapeDtypeStruct((lp, D_MODEL), BF),
            jax.ShapeDtypeStruct((lp, D_MODEL), F32),
        ],
        scratch_shapes=[pltpu.VMEM((tm + 8, CONV_W), F32)],
        compiler_params=_params(),
    )(bch, y_attn, h, conv_w, g_a, g_c, w_out, g_post)


def _mlp_fwd(h2, g_pre, w_up_t, w_down, g_post, tm, name, exch=None, target=None):
    lp = h2.shape[0]
    sub = math.gcd(tm, BLOCK)
    n_sub, lead = tm // sub, BLOCK // sub
    n_t = n_sub if target is not None else 0

    def body(*refs):
        h_ref, gp_ref, wu_ref, wd_ref, gq_ref = refs[:5]
        t_refs = refs[5:5 + n_t]
        a_ref, up_ref, f_ref, last_ref = refs[5 + n_t:9 + n_t]
        h = h_ref[...]
        a = _rms_fwd(h, gp_ref[...]).astype(BF)
        a_ref[...] = a
        up = _dot_nt(a, wu_ref[...])
        up_ref[...] = up.astype(BF)
        act = jnp.square(jnp.maximum(up, 0.0)).astype(BF)
        f = _dot(act, wd_ref[...])
        f_ref[...] = f
        h3 = h + _rms_fwd(f, gq_ref[...])
        if target is None:
            last_ref[...] = h3
            return
        ls_ref = refs[9 + n_t]
        i = pl.program_id(0)

        @pl.when(i == 0)
        def _():
            ls_ref[...] = jnp.zeros((8, 128), F32)

        sq = jnp.zeros((8, D_MODEL), F32)
        for j in range(n_sub):
            on_tokens = i * n_sub + j >= lead
            d = jnp.where(on_tokens, h3[j * sub:(j + 1) * sub] - t_refs[j][...], 0.0)
            last_ref[j * sub:(j + 1) * sub, :] = d * (1.0 / D_MODEL)
            sq = sq + jnp.sum((d * d).reshape(sub // 8, 8, D_MODEL), axis=0)
        ls_ref[...] += sum(sq[:, k * 128:(k + 1) * 128] for k in range(D_MODEL // 128))

        @pl.when(i == lp // tm - 1)
        def _():
            ls_ref[...] = jnp.full((8, 128), jnp.sum(ls_ref[...]), F32)

    row = lambda w: pl.BlockSpec((tm, w), lambda i: (i, 0))
    piece = lambda j: pl.BlockSpec((sub, D_MODEL), lambda i: (jnp.maximum(i * n_sub + j - lead, 0), 0))
    out_specs = [row(D_MODEL), row(D_FF), row(D_MODEL), row(D_MODEL)]
    out_shape = [
        jax.ShapeDtypeStruct((lp, D_MODEL), BF),
        jax.ShapeDtypeStruct((lp, D_FF), BF),
        jax.ShapeDtypeStruct((lp, D_MODEL), F32),
        jax.ShapeDtypeStruct((lp, D_MODEL), F32),
    ]
    if target is not None:
        out_specs.append(_full_out((8, 128)))
        out_shape.append(jax.ShapeDtypeStruct((8, 128), F32))
    return _call(
        body, exch,
        name=name,
        grid=(lp // tm,),
        in_specs=[row(D_MODEL), _full((1, D_MODEL)), _full((D_FF, D_MODEL)), _full((D_FF, D_MODEL)), _full((1, D_MODEL))]
        + [piece(j) for j in range(n_t)],
        out_specs=out_specs,
        out_shape=out_shape,
        compiler_params=_params(),
    )(h2, g_pre, w_up_t, w_down, g_post, *([target] * n_t))


def _mlp_bwd_dx(dh3, f, up, h2, w_down, w_up_t, g_post, g_pre, tm, name, exch=None):
    lp = h2.shape[0]

    def body(dh3_ref, f_ref, up_ref, h2_ref, wd_ref, wu_ref, gq_ref, gp_ref, df_ref, dup_ref, dh2_ref, dg_ref):
        i = pl.program_id(0)

        @pl.when(i == 0)
        def _():
            dg_ref[...] = jnp.zeros((8, D_MODEL), F32)

        dh3 = dh3_ref[...]
        df, dgq = _rms_bwd(f_ref[...], gq_ref[...], dh3)
        dg_ref[ROW_MLP_POST:ROW_MLP_POST + 1, :] += dgq
        df = df.astype(BF)
        df_ref[...] = df
        dact = _dot_nt(df, wd_ref[...])
        dup = (dact * (2.0 * jnp.maximum(up_ref[...].astype(F32), 0.0))).astype(BF)
        dup_ref[...] = dup
        da = _dot(dup, wu_ref[...])
        dh, dgp = _rms_bwd(h2_ref[...], gp_ref[...], da)
        dg_ref[ROW_MLP_PRE:ROW_MLP_PRE + 1, :] += dgp
        dh2_ref[...] = dh3 + dh

    row = lambda w: pl.BlockSpec((tm, w), lambda i: (i, 0))
    return _call(
        body, exch,
        name=name,
        grid=(lp // tm,),
        in_specs=[
            row(D_MODEL), row(D_MODEL), row(D_FF), row(D_MODEL), _full((D_FF, D_MODEL)), _full((D_FF, D_MODEL)),
            _full((1, D_MODEL)), _full((1, D_MODEL)),
        ],
        out_specs=[row(D_MODEL), row(D_FF), row(D_MODEL), _full_out((8, D_MODEL))],
        out_shape=[
            jax.ShapeDtypeStruct((lp, D_MODEL), BF),
            jax.ShapeDtypeStruct((lp, D_FF), BF),
            jax.ShapeDtypeStruct((lp, D_MODEL), F32),
            jax.ShapeDtypeStruct((8, D_MODEL), F32),
        ],
        compiler_params=_params(),
    )(dh3, f, up, h2, w_down, w_up_t, g_post, g_pre)


def _mlp_bwd_dw(up, df, dup, a2, tm, name):
    lp = up.shape[0]
    nt = lp // tm
    nj = D_FF // D_MODEL

    def body(up_ref, df_ref, dup_ref, a_ref, dwd_ref, dwu_ref, accd, accu):
        i = pl.program_id(1)

        @pl.when(i == 0)
        def _():
            accd[...] = jnp.zeros_like(accd)
            accu[...] = jnp.zeros_like(accu)

        act = jnp.square(jnp.maximum(up_ref[...].astype(F32), 0.0)).astype(BF)
        accd[...] += _dot_tn(act, df_ref[...])
        accu[...] += _dot_tn(dup_ref[...], a_ref[...])

        @pl.when(i == nt - 1)
        def _():
            dwd_ref[...] = accd[...].astype(BF)
            dwu_ref[...] = accu[...].astype(BF)

    return pl.pallas_call(
        body,
        name=name,
        grid=(nj, nt),
        in_specs=[
            pl.BlockSpec((tm, D_MODEL), lambda j, i: (i, j)),
            pl.BlockSpec((tm, D_MODEL), lambda j, i: (i, 0)),
            pl.BlockSpec((tm, D_MODEL), lambda j, i: (i, j)),
            pl.BlockSpec((tm, D_MODEL), lambda j, i: (i, 0)),
        ],
        out_specs=[pl.BlockSpec((D_MODEL, D_MODEL), lambda j, i: (j, 0)), pl.BlockSpec((D_MODEL, D_MODEL), lambda j, i: (j, 0))],
        out_shape=[jax.ShapeDtypeStruct((D_FF, D_MODEL), BF), jax.ShapeDtypeStruct((D_FF, D_MODEL), BF)],
        scratch_shapes=[pltpu.VMEM((D_MODEL, D_MODEL), F32), pltpu.VMEM((D_MODEL, D_MODEL), F32)],
        compiler_params=_params(("arbitrary", "arbitrary")),
    )(up, df, dup, a2)


def _mix_out_bwd(dh2, z, y_attn, yc, bch, y, w_out, g_post, g_a, g_c, conv_w, tm, name, exch=None):
    lp = dh2.shape[0]
    nt = lp // tm

    def body(dh2_ref, z_ref, ya_ref, yc_ref, bch_ref, y_ref, w_ref, gp_ref, ga_ref, gc_ref, cw_ref,
             dya_ref, dbch_ref, dg_ref, dwo_ref, ext, acco):
        i = pl.program_id(0)
        dcw_ref = dg_ref.at[ROW_CONV:ROW_CONV + 3, 0:CONV_W]

        @pl.when(i == 0)
        def _():
            ext[tm:tm + 8, :] = jnp.zeros((8, CONV_W), F32)
            dg_ref[...] = jnp.zeros((8, D_MODEL), F32)
            acco[...] = jnp.zeros_like(acco)

        dz, dgp = _rms_bwd(z_ref[...].astype(F32), gp_ref[...], dh2_ref[...])
        dg_ref[ROW_MIX_POST:ROW_MIX_POST + 1, :] += dgp
        dz = dz.astype(BF)
        acco[...] += _dot_tn(y_ref[...], dz)
        dya_n = _dot_nt(dz, w_ref[0:ATTN_W, :])
        dyb_n = _dot_nt(dz, w_ref[ATTN_W:, :])
        dya, dga = _rms_bwd(ya_ref[...].astype(F32), ga_ref[...], dya_n)
        dg_ref[ROW_GROUP_G:ROW_GROUP_G + 1, 0:ATTN_W] += dga
        dya_ref[...] = dya
        b = bch_ref[:, 0:CONV_W].astype(F32)
        c = bch_ref[:, CONV_W:2 * CONV_W].astype(F32)
        hc = bch_ref[:, 2 * CONV_W:3 * CONV_W].astype(F32)
        u = c * hc
        yc_v = yc_ref[...].astype(F32)
        dyconv, dgc = _rms_bwd(b * yc_v, gc_ref[...], dyb_n)
        dg_ref[ROW_GROUP_G:ROW_GROUP_G + 1, ATTN_W:] += dgc
        dbch_ref[:, 0:CONV_W] = (dyconv * yc_v).astype(BF)
        dyc = dyconv * b
        ext[0:tm, :] = dyc
        d1 = ext[1:1 + tm, :]
        d2 = ext[2:2 + tm, :]
        du = cw_ref[2:3, :] * dyc + cw_ref[1:2, :] * d1 + cw_ref[0:1, :] * d2
        ext[tm:tm + 8, :] = dyc[0:8, :]
        dbch_ref[:, CONV_W:2 * CONV_W] = (du * hc).astype(BF)
        dbch_ref[:, 2 * CONV_W:3 * CONV_W] = (du * c).astype(BF)
        dcw_ref[0:1, :] += jnp.sum(u * d2, axis=0, keepdims=True)
        dcw_ref[1:2, :] += jnp.sum(u * d1, axis=0, keepdims=True)
        dcw_ref[2:3, :] += jnp.sum(u * dyc, axis=0, keepdims=True)

        @pl.when(i == nt - 1)
        def _():
            dwo_ref[...] = acco[...].astype(BF)

    row = lambda w: pl.BlockSpec((tm, w), lambda i: (nt - 1 - i, 0))
    return _call(
        body, exch,
        name=name,
        grid=(nt,),
        in_specs=[
            row(D_MODEL), row(D_MODEL), row(ATTN_W), row(CONV_W), row(3 * CONV_W), row(D_MODEL), _full((D_MODEL, D_MODEL)),
            _full((1, D_MODEL)), _full((1, ATTN_W)), _full((1, CONV_W)), _full((8, CONV_W)),
        ],
        out_specs=[row(ATTN_W), row(3 * CONV_W), _full_out((8, D_MODEL)), _full_out((D_MODEL, D_MODEL))],
        out_shape=[
            jax.ShapeDtypeStruct((lp, ATTN_W), F32),
            jax.ShapeDtypeStruct((lp, 3 * CONV_W), BF),
            jax.ShapeDtypeStruct((8, D_MODEL), F32),
            jax.ShapeDtypeStruct((D_MODEL, D_MODEL), BF),
        ],
        scratch_shapes=[pltpu.VMEM((tm + 8, CONV_W), F32), pltpu.VMEM((D_MODEL, D_MODEL), F32)],
        compiler_params=_params(),
    )(dh2, z, y_attn, yc, bch, y, w_out, g_post, g_a, g_c, conv_w)


def _attn_bwd(qkv, o, do, probs, p_sink, rope, name, exch=None):
    lp = qkv.shape[0]
    nb = lp // BLOCK

    def body(q_ref, kvc_ref, kvp_ref, o_ref, do_ref, p_ref, ps_ref, cq_ref, s1q_ref, s2q_ref, ck_ref, s1k_ref, s2k_ref,
             dq_ref, dkv_ref, dsink_ref, carry):
        i = pl.program_id(0)

        @pl.when(i == 0)
        def _():
            carry[...] = jnp.zeros_like(carry)
            dsink_ref[...] = jnp.zeros((8, 128), F32)

        def finish(tot):
            dk = _rope_t(tot[:, :128], ck_ref[...], s1k_ref[...], s2k_ref[...])
            dkv_ref[:, 0:128] = dk.astype(BF)
            dkv_ref[:, 128:256] = tot[:, 128:].astype(BF)

        @pl.when(i < nb)
        def _():
            tri, _ = _fold_masks(i)
            kvc, kvp = kvc_ref[...], kvp_ref[...]
            kk = jnp.concatenate([kvp[:, :128], kvc[:, :128]], axis=0)
            vv = jnp.concatenate([kvp[:, 128:], kvc[:, 128:]], axis=0)
            lane = lax.broadcasted_iota(jnp.int32, (BLOCK, 128), 1)
            lane2 = lax.broadcasted_iota(jnp.int32, (2 * BLOCK, 128), 1)
            rope_q = (cq_ref[...], s1q_ref[...], s2q_ref[...])
            deltas = jnp.zeros((BLOCK, 128), F32)
            folded = []
            for kvh in range(2):
                c0 = 256 * kvh
                q2 = jnp.concatenate([q_ref[:, c0:c0 + 128], q_ref[:, c0 + 128:c0 + 256]], axis=0)
                do2 = jnp.concatenate([do_ref[:, c0:c0 + 128], do_ref[:, c0 + 128:c0 + 256]], axis=0)
                o2 = jnp.concatenate([o_ref[:, c0:c0 + 128], o_ref[:, c0 + 128:c0 + 256]], axis=0).astype(F32)
                k4, v4 = _kv_operand(kk, kvh), _kv_operand(vv, kvh)
                prod = do2 * o2
                dob = do2.astype(BF)
                dp = _dot_nt(dob, v4)
                ds, pb = [], []
                for half in range(2):
                    heads = [4 * kvh + 2 * pair + half for pair in range(2)]
                    p = jnp.concatenate([p_ref[:, 128 * h:128 * (h + 1)] for h in heads], axis=0)
                    sel = (lane2 < HEAD_DIM) if half == 0 else (lane2 >= HEAD_DIM)
                    delta = jnp.sum(jnp.where(sel, prod, 0.0), axis=-1, keepdims=True)
                    dp_h = dp[:, 2 * half * BLOCK:2 * (half + 1) * BLOCK]
                    ds.append((p.astype(F32) * (jnp.where(tri, dp_h[:, :BLOCK], dp_h[:, BLOCK:]) - delta)).astype(BF))
                    pb.append(p)
                    for pair in range(2):
                        deltas = jnp.where(lane == heads[pair], delta[pair * BLOCK:(pair + 1) * BLOCK], deltas)
                ds4, p4 = _split4(ds, tri), _split4(pb, tri)
                dq2 = _dot(ds4, k4) * SCALE
                dq_ref[:, c0:c0 + 128] = _rope_t(dq2[:BLOCK], *rope_q).astype(BF)
                dq_ref[:, c0 + 128:c0 + 256] = _rope_t(dq2[BLOCK:], *rope_q).astype(BF)
                rk, rv = _dot_tn(ds4, q2), _dot_tn(p4, dob)
                own = (lane < HEAD_DIM) if kvh == 0 else (lane >= HEAD_DIM)
                group = []
                for r in (rk, rv):
                    for blk in range(2):
                        t = jnp.where(lane < HEAD_DIM, r[blk * BLOCK:(blk + 1) * BLOCK], r[(2 + blk) * BLOCK:(3 + blk) * BLOCK])
                        group.append(jnp.where(own, t + pltpu.roll(t, HEAD_DIM, 1), 0.0))
                folded.append(group)
            dsink_ref[ROW_SINK:ROW_SINK + 1, :] -= jnp.sum(ps_ref[...] * deltas, axis=0, keepdims=True)
            dk_p, dk_c, dv_p, dv_c = [folded[0][t] + folded[1][t] for t in range(4)]
            finish(carry[...] + jnp.concatenate([dk_p, dv_p], axis=1))
            carry[...] = jnp.concatenate([dk_c, dv_c], axis=1)

        @pl.when(i == nb)
        def _():
            finish(carry[...])

    qi = lambda i: jnp.minimum(i, nb - 1)
    ki = lambda i: jnp.maximum(i - 1, 0)
    tab_q = pl.BlockSpec((BLOCK, 128), lambda i: (qi(i), 0))
    tab_k = pl.BlockSpec((BLOCK, 128), lambda i: (ki(i), 0))
    return _call(
        body, exch,
        name=name,
        grid=(nb + 1,),
        in_specs=[
            pl.BlockSpec((BLOCK, ATTN_W), lambda i: (qi(i), 0)),
            pl.BlockSpec((BLOCK, 256), lambda i: (qi(i), 2)),
            pl.BlockSpec((BLOCK, 256), lambda i: (jnp.maximum(qi(i) - 1, 0), 2)),
            pl.BlockSpec((BLOCK, ATTN_W), lambda i: (qi(i), 0)),
            pl.BlockSpec((BLOCK, ATTN_W), lambda i: (qi(i), 0)),
            pl.BlockSpec((BLOCK, N_Q_HEADS * BLOCK), lambda i: (qi(i), 0)),
            tab_q, tab_q, tab_q, tab_q, tab_k, tab_k, tab_k,
        ],
        out_specs=[
            pl.BlockSpec((BLOCK, ATTN_W), lambda i: (qi(i), 0)),
            pl.BlockSpec((BLOCK, 256), lambda i: (ki(i), 0)),
            pl.BlockSpec((8, 128), lambda i: (0, 0)),
        ],
        out_shape=[
            jax.ShapeDtypeStruct((lp, ATTN_W), BF),
            jax.ShapeDtypeStruct((lp, 256), BF),
            jax.ShapeDtypeStruct((8, 128), F32),
        ],
        scratch_shapes=[pltpu.VMEM((BLOCK, 256), F32)],
        compiler_params=_params(),
    )(qkv, qkv, qkv, o, do, probs, p_sink, *rope, *rope)


def _in_proj_bwd_dx(dq, dkv, dbch, w_in_t, h, dh2, g, tm, name, exch=None):
    lp = h.shape[0]

    def body(dq_ref, dkv_ref, dbch_ref, w_ref, h_ref, dh2_ref, g_ref, dh_ref, dg_ref):
        i = pl.program_id(0)

        @pl.when(i == 0)
        def _():
            dg_ref[...] = jnp.zeros((8, D_MODEL), F32)

        da = _dot(jnp.concatenate([dq_ref[...], dkv_ref[...], dbch_ref[...]], axis=1), w_ref[...])
        dh, dg = _rms_bwd(h_ref[...], g_ref[...], da)
        dg_ref[ROW_MIX_PRE:ROW_MIX_PRE + 1, :] += dg
        dh_ref[...] = dh2_ref[...] + dh

    row = lambda w: pl.BlockSpec((tm, w), lambda i: (i, 0))
    return _call(
        body, exch,
        name=name,
        grid=(lp // tm,),
        in_specs=[row(ATTN_W), row(256), row(3 * CONV_W), _full((IN_W, D_MODEL)), row(D_MODEL), row(D_MODEL), _full((1, D_MODEL))],
        out_specs=[row(D_MODEL), _full_out((8, D_MODEL))],
        out_shape=[jax.ShapeDtypeStruct((lp, D_MODEL), F32), jax.ShapeDtypeStruct((8, D_MODEL), F32)],
        compiler_params=_params(),
    )(dq, dkv, dbch, w_in_t, h, dh2, g)


def _mix_bwd_dw(dq, dkv, dbch, a, tm, name, exch=None):
    lp = a.shape[0]
    nt = lp // tm

    def body(dq_ref, dkv_ref, dbch_ref, a_ref, dwi_ref, acci):
        i = pl.program_id(0)

        @pl.when(i == 0)
        def _():
            acci[...] = jnp.zeros_like(acci)

        a_v = a_ref[...]
        acci[0:512, :] += _dot_tn(dq_ref[...], a_v)
        acci[512:768, :] += _dot_tn(dkv_ref[...], a_v)
        acci[768:, :] += _dot_tn(dbch_ref[...], a_v)

        @pl.when(i == nt - 1)
        def _():
            dwi_ref[...] = acci[...].astype(BF)

    row = lambda w: pl.BlockSpec((tm, w), lambda i: (i, 0))
    return _call(
        body, exch,
        name=name,
        grid=(nt,),
        in_specs=[row(ATTN_W), row(256), row(3 * CONV_W), row(D_MODEL)],
        out_specs=[_full_out((IN_W, D_MODEL))],
        out_shape=[jax.ShapeDtypeStruct((IN_W, D_MODEL), BF)],
        scratch_shapes=[pltpu.VMEM((IN_W, D_MODEL), F32)],
        compiler_params=_params(),
    )(dq, dkv, dbch, a)


def _mesh_place():
    x, y, c = lax.axis_index("x"), lax.axis_index("y"), lax.axis_index("c")
    return x, y, c, 4 * x + 2 * y + c


def _peer(x, y, c, k):
    px = 1 - x if k & 4 else x
    py = 1 - y if k & 2 else y
    pc = 1 - c if k & 1 else c
    return (px, py, pc), 4 * px + 2 * py + pc


SIBLING = 1
SAME_CORE = (2, 4, 6)
OTHER_CORE = (3, 5, 7)


class _Exchange:
    def __init__(self, pieces):
        self.srcs = [s for s, _ in pieces]
        self.to_all = [g for _, g in pieces]
        self.n = len(pieces)
        self.land_shapes = [
            jax.ShapeDtypeStruct((N_DEV,) + (s.shape if g else s.shape[1:]), s.dtype) for s, g in pieces]
        self.sem_shapes = [pltpu.SemaphoreType.DMA((self.n, N_DEV - 1)), pltpu.SemaphoreType.DMA((self.n, N_DEV - 1)),
                           pltpu.SemaphoreType.DMA((self.n,))]
        self.forwards = any(self.to_all)

    def _ops(self, srcs, lands, sems):
        send_sems, recv_sems, local_sems = sems
        x, y, c, me = _mesh_place()

        def remote(p, k, src, slot, to):
            return pltpu.make_async_remote_copy(
                src_ref=src, dst_ref=lands[p].at[slot], send_sem=send_sems.at[p, k - 1], recv_sem=recv_sems.at[p, k - 1],
                device_id=to, device_id_type=MESH)

        def own(p):
            return pltpu.make_async_copy(srcs[p] if self.to_all[p] else srcs[p].at[me], lands[p].at[me], local_sems.at[p])

        def direct(p, k):
            peer, pidx = _peer(x, y, c, k)
            return remote(p, k, srcs[p] if self.to_all[p] else srcs[p].at[pidx], me, peer)

        def forward(p, k):
            sibling, _ = _peer(x, y, c, SIBLING)
            _, origin = _peer(x, y, c, k ^ SIBLING)
            return remote(p, k, lands[p].at[origin], origin, sibling)

        def arrival(p, k):
            peer, pidx = _peer(x, y, c, k)
            return remote(p, k, lands[p].at[pidx], pidx, peer)

        return own, direct, forward, arrival

    def start(self, srcs, lands, sems):
        own, direct, _, _ = self._ops(srcs, lands, sems)
        for p in range(self.n):
            own(p).start()
            for k in ((SIBLING,) + SAME_CORE) if self.to_all[p] else range(1, N_DEV):
                direct(p, k).start()

    def forward(self, srcs, lands, sems):
        _, _, forward, arrival = self._ops(srcs, lands, sems)
        for p in range(self.n):
            if self.to_all[p]:
                for k in SAME_CORE:
                    arrival(p, k).wait_recv()
                    forward(p, k ^ SIBLING).start()

    def finish(self, srcs, lands, sems):
        own, direct, forward, arrival = self._ops(srcs, lands, sems)
        for p in range(self.n):
            for k in ((SIBLING,) + OTHER_CORE) if self.to_all[p] else range(1, N_DEV):
                arrival(p, k).wait_recv()
        for p in range(self.n):
            for k in range(1, N_DEV):
                (forward(p, k) if self.to_all[p] and k in OTHER_CORE else direct(p, k)).wait_send()
            own(p).wait()


def _call(body, exch, *, name, grid, in_specs, out_specs, out_shape, scratch_shapes=(), compiler_params, after=None):
    if exch is None:
        return pl.pallas_call(body, name=name, grid=grid, in_specs=in_specs, out_specs=out_specs, out_shape=out_shape,
                              scratch_shapes=scratch_shapes, compiler_params=compiler_params)
    n_in, n_out, n_scr, n_x = len(in_specs), len(out_shape), len(scratch_shapes), exch.n
    steps = math.prod(grid)

    def carrying(*refs):
        a, b, c, d, e = n_in, n_in + n_x, n_in + n_x + n_out, n_in + 2 * n_x + n_out, n_in + 2 * n_x + n_out + n_scr
        ins, srcs, outs, lands, scr, sems = refs[:a], refs[a:b], refs[b:c], refs[c:d], refs[d:e], refs[e:]
        step = functools.reduce(lambda acc, t: acc * grid[t] + pl.program_id(t), range(len(grid)), 0)

        @pl.when(step == 0)
        def _():
            exch.start(srcs, lands, sems)

        body(*ins, *outs, *scr)

        if exch.forwards:
            @pl.when(step == max(0, steps - 1 - (steps + 7) // 8))
            def _():
                exch.forward(srcs, lands, sems)

        @pl.when(step == steps - 1)
        def _():
            exch.finish(srcs, lands, sems)
            if after is not None:
                after(lands, *ins, *outs, *scr)

    hbm = pl.BlockSpec(memory_space=pl.ANY)
    call = pl.pallas_call(
        carrying, name=name, grid=grid, in_specs=list(in_specs) + [hbm] * n_x, out_specs=list(out_specs) + [hbm] * n_x,
        out_shape=list(out_shape) + exch.land_shapes, scratch_shapes=list(scratch_shapes) + exch.sem_shapes,
        compiler_params=compiler_params)

    def run(*args):
        res = call(*args, *exch.srcs)
        return list(res[:n_out]), list(res[n_out:])

    return run


def _sum_small(part):
    exch = _Exchange([(part, True)])

    def body(part_ref, out_ref, land, *sems):
        exch.start([part_ref], [land], sems)
        exch.forward([part_ref], [land], sems)
        exch.finish([part_ref], [land], sems)
        acc = land[0]
        for d in range(1, N_DEV):
            acc = acc + land[d]
        out_ref[...] = acc

    vmem = pl.BlockSpec(memory_space=pltpu.VMEM)
    return pl.pallas_call(
        body,
        name="sum_small",
        in_specs=[vmem],
        out_specs=vmem,
        out_shape=jax.ShapeDtypeStruct(part.shape, F32),
        scratch_shapes=[pltpu.VMEM(exch.land_shapes[0].shape, F32)] + exch.sem_shapes,
    )(part)


def _adamw(w, g, m, v):
    m = ADAM_B1 * m + (1.0 - ADAM_B1) * g
    v = ADAM_B2 * v + (1.0 - ADAM_B2) * jnp.square(g)
    m_hat = m / (1.0 - ADAM_B1 ** ADAM_STEP)
    v_hat = v / (1.0 - ADAM_B2 ** ADAM_STEP)
    delta = -ADAM_LR * (m_hat / (jnp.sqrt(v_hat) + ADAM_EPS) + ADAM_WD * w)
    return delta, m, v


def _landed_specs(tr, wd):
    return [pl.BlockSpec((N_DEV, tr, wd), lambda l, i, ll=ll: (0, jnp.where(l == ll, i, 0), 0)) for ll in range(DEPTH)]


def _device_sum(r_ref):
    acc = r_ref[0].astype(F32)
    for d in range(1, N_DEV):
        acc = acc + r_ref[d].astype(F32)
    return acc


def _sum_adamw(recv, w, m, v, tr, name, transposed=False):
    _, r, wd = recv[0].shape

    def body(*refs):
        w_ref, m_ref, v_ref, g_ref, d_ref, mo_ref, vo_ref = refs[DEPTH:]
        for ll in range(DEPTH):
            @pl.when(pl.program_id(0) == ll)
            def _(ll=ll):
                g = _device_sum(refs[ll])
                g = g.T if transposed else g
                g_ref[0] = g
                d_ref[0], mo_ref[0], vo_ref[0] = _adamw(w_ref[0], g, m_ref[0], v_ref[0])

    if transposed:
        blk = pl.BlockSpec((1, wd, tr), lambda l, i: (l, 0, i))
        shape = jax.ShapeDtypeStruct((DEPTH, wd, r), F32)
    else:
        blk = pl.BlockSpec((1, tr, wd), lambda l, i: (l, i, 0))
        shape = jax.ShapeDtypeStruct((DEPTH, r, wd), F32)
    return pl.pallas_call(
        body,
        name=name,
        grid=(DEPTH, r // tr),
        in_specs=_landed_specs(tr, wd) + [blk, blk, blk],
        out_specs=[blk] * 4,
        out_shape=[shape] * 4,
        compiler_params=_params(("arbitrary", "arbitrary")),
    )(*recv, w, m, v)


def _adamw_small(ws, gs, ms, vs):
    n = len(ws)

    def body(*refs):
        w_r, g_r, m_r, v_r = refs[:n], refs[n:2 * n], refs[2 * n:3 * n], refs[3 * n:4 * n]
        d_o, m_o, v_o = refs[4 * n:5 * n], refs[5 * n:6 * n], refs[6 * n:7 * n]
        for t in range(n):
            d_o[t][...], m_o[t][...], v_o[t][...] = _adamw(w_r[t][...], g_r[t][...], m_r[t][...], v_r[t][...])

    vmem = pl.BlockSpec(memory_space=pltpu.VMEM)
    shapes = [jax.ShapeDtypeStruct(w.shape, F32) for w in ws]
    outs = pl.pallas_call(
        body,
        name="adamw_small",
        in_specs=[vmem] * (4 * n),
        out_specs=[vmem] * (3 * n),
        out_shape=shapes * 3,
    )(*ws, *gs, *ms, *vs)
    return outs[:n], outs[n:2 * n], outs[2 * n:]


def kernel(x, meta_tokens, mix_pre_g, w_in, conv_w, sinks, attn_out_g, conv_out_g, w_out, mix_post_g, mlp_pre_g, w_up, w_down, mlp_post_g, loss_target, m_meta_tokens, m_mix_pre_g, m_w_in, m_conv_w, m_sinks, m_attn_out_g, m_conv_out_g, m_w_out, m_mix_post_g, m_mlp_pre_g, m_w_up, m_w_down, m_mlp_post_g, v_meta_tokens, v_mix_pre_g, v_w_in, v_conv_w, v_sinks, v_attn_out_g, v_conv_out_g, v_w_out, v_mix_post_g, v_mlp_pre_g, v_w_up, v_w_down, v_mlp_post_g):
    seq = x.shape[1]
    lp = BLOCK + seq
    tm = _row_tile(lp)
    tm_mlp = _row_tile(lp, (320, 256, 128))
    tm_dw_mlp = _row_tile(lp, (1664, 1040, 640, 384, 256, 128))
    tm_dw_mix = _row_tile(lp, (1664, 832, 640, 384, 256, 128))
    me = 4 * lax.axis_index("x") + 2 * lax.axis_index("y") + lax.axis_index("c")
    cshard = CONV_W // N_DEV
    mshard = D_MODEL // N_DEV

    gather_with = {
        ("in_proj_fwd", 0): [("down", 0)], ("attn_fwd", 0): [("out", 0), ("up", 0)],
        ("mlp_fwd", 0): [("in", 1), ("out", 1), ("up", 1), ("down", 1)],
    }
    scatter_with = {
        ("attn_bwd", 1): [("down", 1)], ("mix_bwd_dw", 1): [("out", 1)], ("mlp_bwd_dx", 0): [("up", 1), ("in", 1)],
        ("mix_out_bwd", 0): [("up", 0)], ("attn_bwd", 0): [("down", 0)], ("mix_bwd_dw", 0): [("out", 0)],
        ("in_proj_bwd_dx", 0): [("in", 0)],
    }
    shard = {"in": jnp.swapaxes(w_in, 1, 2).astype(BF), "out": w_out.astype(BF),
             "up": jnp.swapaxes(w_up, 1, 2).astype(BF), "down": w_down.astype(BF)}
    weight = {}
    grad = {}
    landed = {}

    def run(fn, kind, l, *args):
        key, name = (kind, l), f"{kind}_{l}"
        if key in gather_with:
            blocks = gather_with[key]
            outs, lands = fn(*args, name, _Exchange([(shard[n][k], True) for n, k in blocks]))
            for b, land in zip(blocks, lands):
                weight[b] = land.reshape(-1, D_MODEL)
            return outs
        if key in scatter_with:
            blocks = scatter_with[key]
            outs, lands = fn(*args, name, _Exchange([(grad[b].reshape(N_DEV, -1, D_MODEL), False) for b in blocks]))
            landed.update(zip(blocks, lands))
            return outs
        return fn(*args, name)

    small = jnp.zeros((24, 128), F32)
    small = small.at[0:N_META, :].set(meta_tokens)
    small = small.at[N_META:N_META + 6, 0:cshard].set(conv_w.reshape(6, cshard))
    first = _Exchange([(shard["in"][0], True), (small, True)])
    h, rope, (first_in, g_small) = _build_h(x[0], _rope_table(lp), tm, first, 1, "build_h")
    weight[("in", 0)] = first_in.reshape(-1, D_MODEL)
    cw = g_small[:, N_META:N_META + 6, 0:cshard].reshape(N_DEV, DEPTH, 3, cshard)
    cw = jnp.transpose(cw, (1, 2, 0, 3)).reshape(DEPTH, 3, CONV_W)
    conv_full = jnp.concatenate([cw, jnp.zeros((DEPTH, 5, CONV_W), F32)], axis=1)

    row1 = lambda a, l: a[l].reshape(1, -1)

    saved = []
    for l in range(DEPTH):
        a, qkv, bch = run(_in_proj_fwd, "in_proj_fwd", l, h, row1(mix_pre_g, l), weight[("in", l)], rope, tm)
        y_attn, probs, p_sink = run(_attn_fwd, "attn_fwd", l, qkv, row1(sinks, l))
        yc, y, z, h2 = run(_mix_out_fwd, "mix_out_fwd", l, bch, y_attn, h, conv_full[l], row1(attn_out_g, l),
                       row1(conv_out_g, l), weight[("out", l)], row1(mix_post_g, l), tm)
        mlp = _mlp_fwd if l < DEPTH - 1 else functools.partial(_mlp_fwd, target=loss_target[0])
        a2, up, f, *rest = run(mlp, "mlp_fwd", l, h2, row1(mlp_pre_g, l), weight[("up", l)], weight[("down", l)],
                               row1(mlp_post_g, l), tm_mlp)
        saved.append((h, a, qkv, bch, y_attn, probs, p_sink, yc, y, z, h2, a2, up, f))
        h = rest[0]
    dh, loss_part = rest[0], rest[1][0, 0] * (0.5 / D_MODEL)

    gsmall = [None] * DEPTH
    for l in reversed(range(DEPTH)):
        h0, a, qkv, bch, y_attn, probs, p_sink, yc, y, z, h2, a2, up, f = saved[l]
        df, dup, dh2, dg_mlp = run(_mlp_bwd_dx, "mlp_bwd_dx", l, dh, f, up, h2, weight[("down", l)], weight[("up", l)],
                                   row1(mlp_post_g, l), row1(mlp_pre_g, l), tm_mlp)
        grad[("down", l)], grad[("up", l)] = _mlp_bwd_dw(up, df, dup, a2, tm_dw_mlp, f"mlp_bwd_dw_{l}")
        dya, dbch, dg_mix, grad[("out", l)] = run(
            _mix_out_bwd, "mix_out_bwd", l, dh2, z, y_attn, yc, bch, y, weight[("out", l)], row1(mix_post_g, l),
            row1(attn_out_g, l), row1(conv_out_g, l), conv_full[l], tm)
        dq, dkv, dsink = run(_attn_bwd, "attn_bwd", l, qkv, y_attn, dya, probs, p_sink, rope)
        grad[("in", l)], = run(_mix_bwd_dw, "mix_bwd_dw", l, dq, dkv, dbch, a, tm_dw_mix)
        dh, dg_in = run(_in_proj_bwd_dx, "in_proj_bwd_dx", l, dq, dkv, dbch, weight[("in", l)], h0, dh2,
                        row1(mix_pre_g, l), tm)
        tile_a = dg_mlp + dg_in + jnp.pad(dsink, ((0, 0), (0, D_MODEL - 128)))
        gsmall[l] = (tile_a, dg_mix)
    grad_x = dh[BLOCK:][None]

    loss_tile = jnp.zeros((8, D_MODEL), F32).at[ROW_LOSS, 0].set(loss_part)
    tot = _sum_small(jnp.concatenate(
        [gsmall[0][0] + loss_tile, gsmall[0][1], gsmall[1][0], gsmall[1][1], dh[LEAD_PAD:BLOCK]], axis=0))
    loss = tot[ROW_LOSS, 0]
    ta = [tot[16 * l:16 * l + 8] for l in range(DEPTH)]
    tb = [tot[16 * l + 8:16 * l + 16] for l in range(DEPTH)]
    pick = lambda tiles, r0, r1, c0, c1: jnp.stack([t[r0:r1, c0:c1] for t in tiles])
    g_mlp_post = pick(ta, ROW_MLP_POST, ROW_MLP_POST + 1, 0, D_MODEL).reshape(DEPTH, D_MODEL)
    g_mlp_pre = pick(ta, ROW_MLP_PRE, ROW_MLP_PRE + 1, 0, D_MODEL).reshape(DEPTH, D_MODEL)
    g_mix_pre = pick(ta, ROW_MIX_PRE, ROW_MIX_PRE + 1, 0, D_MODEL).reshape(DEPTH, D_MODEL)
    g_sinks = pick(ta, ROW_SINK, ROW_SINK + 1, 0, N_Q_HEADS).reshape(DEPTH, N_Q_HEADS)
    g_mix_post = pick(tb, ROW_MIX_POST, ROW_MIX_POST + 1, 0, D_MODEL).reshape(DEPTH, D_MODEL)
    g_attn_out = pick(tb, ROW_GROUP_G, ROW_GROUP_G + 1, 0, ATTN_W).reshape(DEPTH, ATTN_W)
    g_conv_out = pick(tb, ROW_GROUP_G, ROW_GROUP_G + 1, ATTN_W, D_MODEL).reshape(DEPTH, CONV_W)
    g_conv_full = pick(tb, ROW_CONV, ROW_CONV + 3, 0, CONV_W)
    g_conv = lax.dynamic_slice_in_dim(g_conv_full, me * cshard, cshard, axis=2)
    g_meta = lax.dynamic_slice_in_dim(tot[16 * DEPTH:16 * DEPTH + N_META], me * mshard, mshard, axis=1)

    r_in, r_out, r_up, r_down = [[landed[(n, l)] for l in range(DEPTH)] for n in ("in", "out", "up", "down")]
    t12 = lambda a: jnp.swapaxes(a, 1, 2)
    g_w_in, d_w_in, nm_w_in, nv_w_in = map(t12, _sum_adamw(r_in, t12(w_in), t12(m_w_in), t12(v_w_in), 96, "adamw_w_in"))
    g_w_up, d_w_up, nm_w_up, nv_w_up = _sum_adamw(r_up, w_up, m_w_up, v_w_up, 128, "adamw_w_up", transposed=True)
    g_w_out, d_w_out, nm_w_out, nv_w_out = _sum_adamw(r_out, w_out, m_w_out, v_w_out, 128, "adamw_w_out")
    g_w_down, d_w_down, nm_w_down, nv_w_down = _sum_adamw(r_down, w_down, m_w_down, v_w_down, 128, "adamw_w_down")

    ws = [meta_tokens, mix_pre_g, conv_w.reshape(6, cshard), sinks, attn_out_g, conv_out_g, mix_post_g, mlp_pre_g, mlp_post_g]
    gs = [g_meta, g_mix_pre, g_conv.reshape(6, cshard), g_sinks, g_attn_out, g_conv_out, g_mix_post, g_mlp_pre, g_mlp_post]
    ms = [m_meta_tokens, m_mix_pre_g, m_conv_w.reshape(6, cshard), m_sinks, m_attn_out_g, m_conv_out_g, m_mix_post_g,
          m_mlp_pre_g, m_mlp_post_g]
    vs = [v_meta_tokens, v_mix_pre_g, v_conv_w.reshape(6, cshard), v_sinks, v_attn_out_g, v_conv_out_g, v_mix_post_g,
          v_mlp_pre_g, v_mlp_post_g]
    ds, nms, nvs = _adamw_small(ws, gs, ms, vs)

    def order(meta, mix_pre, cv, sk, a_out, c_out, mix_post, mlp_pre, mlp_post, win, wout, wup, wdown):
        return [meta, mix_pre, win, cv.reshape(DEPTH, 3, cshard), sk, a_out, c_out, wout, mix_post, mlp_pre, wup, wdown, mlp_post]

    grads = order(*gs, g_w_in, g_w_out, g_w_up, g_w_down)
    deltas = order(*ds, d_w_in, d_w_out, d_w_up, d_w_down)
    new_m = order(*nms, nm_w_in, nm_w_out, nm_w_up, nm_w_down)
    new_v = order(*nvs, nv_w_in, nv_w_out, nv_w_up, nv_w_down)
    return (loss, grad_x, *grads, *deltas, *new_m, *new_v)
```

```python
import functools
import math

import jax
import jax.numpy as jnp
from jax import lax
from jax.experimental import pallas as pl
from jax.experimental.pallas import tpu as pltpu

F32 = jnp.float32
BF = jnp.bfloat16

D_MODEL = 1024
ATTN_W = 512
CONV_W = 512
HEAD_DIM = 64
N_Q_HEADS = 8
ROT_DIM = 16
D_FF = 4096
IN_W = 2304
N_META = 16
BLOCK = 128
LEAD_PAD = BLOCK - N_META
ROPE_THETA = 500000.0
EPS = 1e-6
N_DEV = 8
DEPTH = 2
NEG = -1e30
SCALE = HEAD_DIM ** -0.5

ADAM_LR = 0.001
ADAM_B1 = 0.9
ADAM_B2 = 0.999
ADAM_EPS = 1e-08
ADAM_WD = 0.01
ADAM_STEP = 10

ROW_MLP_POST, ROW_MLP_PRE, ROW_MIX_PRE, ROW_SINK, ROW_LOSS = 0, 1, 2, 3, 4
ROW_MIX_POST, ROW_GROUP_G, ROW_CONV = 0, 1, 2

VMEM_LIMIT = 56 * 1024 * 1024
MESH = pl.DeviceIdType.MESH


def _dot(a, b):
    return jnp.dot(a, b, preferred_element_type=F32)


def _dot_nt(a, b):
    return lax.dot_general(a, b, (((1,), (1,)), ((), ())), preferred_element_type=F32)


def _dot_tn(a, b):
    return lax.dot_general(a, b, (((0,), (0,)), ((), ())), preferred_element_type=F32)


def _rms_fwd(x, g):
    r = lax.rsqrt(jnp.mean(x * x, axis=-1, keepdims=True) + EPS)
    return x * r * g


def _rms_bwd(x, g, dy):
    r = lax.rsqrt(jnp.mean(x * x, axis=-1, keepdims=True) + EPS)
    xh = x * r
    t = dy * g
    dx = r * (t - xh * jnp.mean(t * xh, axis=-1, keepdims=True))
    dg = jnp.sum(dy * xh, axis=0, keepdims=True)
    return dx, dg


def _row_tile(lp, cands=(640, 512, 384, 256, 128)):
    for t in cands:
        if lp % t == 0:
            return t
    raise ValueError(f"row count {lp} is not a multiple of 128")


def _full(shape):
    n = len(shape)
    return pl.BlockSpec(shape, lambda *_: (0,) * n, pipeline_mode=pl.Buffered(1))


def _full_out(shape):
    n = len(shape)
    return pl.BlockSpec(shape, lambda *_: (0,) * n)


def _params(sem=("arbitrary",)):
    return pltpu.CompilerParams(dimension_semantics=sem, vmem_limit_bytes=VMEM_LIMIT)


def _rope_table(lp):
    half = ROT_DIM // 2
    pos = jnp.maximum(jnp.arange(lp) - LEAD_PAD, 0).astype(F32)
    inv_freq = jnp.power(jnp.float32(ROPE_THETA), -jnp.arange(0, ROT_DIM, 2, dtype=F32) / ROT_DIM)
    ang_t = jnp.concatenate([inv_freq, inv_freq])[:, None] * pos[None, :]
    row = lax.broadcasted_iota(jnp.int32, (ROT_DIM, lp), 0)
    cs_t = jnp.where(row < half, jnp.cos(ang_t), jnp.sin(ang_t))
    return jnp.pad(cs_t.T, ((0, 0), (0, 128 - ROT_DIM)))


def _rope_coeffs(t):
    half = ROT_DIM // 2
    lane = lax.broadcasted_iota(jnp.int32, t.shape, 1)
    cos_a = jnp.where(lane < half, t, 0.0)
    sin_a = pltpu.roll(jnp.where((lane >= half) & (lane < ROT_DIM), t, 0.0), 128 - half, 1)
    c = cos_a + pltpu.roll(cos_a, half, 1) + jnp.where((lane >= ROT_DIM) & (lane < HEAD_DIM), 1.0, 0.0)
    s2 = pltpu.roll(sin_a, half, 1)
    both = lambda u: u + pltpu.roll(u, HEAD_DIM, 1)
    return both(c), both(-sin_a), both(s2)


def _rope(t, c, s1, s2):
    return t * c + pltpu.roll(t, BLOCK - 8, 1) * s1 + pltpu.roll(t, 8, 1) * s2


def _rope_t(dt, c, s1, s2):
    return dt * c + pltpu.roll(dt * s1, 8, 1) + pltpu.roll(dt * s2, BLOCK - 8, 1)


def _build_h(x, rope_compact, tm, exch, small_piece, name):
    seq = x.shape[0]
    lp = BLOCK + seq
    nt = lp // tm
    n_sub = tm // BLOCK
    small_shape = exch.land_shapes[small_piece].shape

    def body(*refs):
        h_ref, c_ref, s1_ref, s2_ref = refs[n_sub + 1:n_sub + 5]
        for j in range(n_sub):
            h_ref[j * BLOCK:(j + 1) * BLOCK, :] = refs[j][...]
        c_ref[...], s1_ref[...], s2_ref[...] = _rope_coeffs(refs[n_sub][...])

    def after(lands, *refs):
        h_ref, buf = refs[n_sub + 1], refs[n_sub + 5]
        pltpu.sync_copy(lands[small_piece], buf)
        h_ref[0:LEAD_PAD, :] = jnp.zeros((LEAD_PAD, D_MODEL), F32)
        for d in range(N_DEV):
            h_ref[LEAD_PAD:BLOCK, d * 128:(d + 1) * 128] = buf[d, 0:N_META, :]

    tile = lambda i: (i + 1) % nt
    piece = lambda j: pl.BlockSpec((BLOCK, D_MODEL), lambda i: (jnp.maximum(tile(i) * n_sub + j - 1, 0), 0))
    rows = lambda w: pl.BlockSpec((tm, w), lambda i: (tile(i), 0))
    (h, *rope), lands = _call(
        body, exch,
        name=name,
        grid=(nt,),
        in_specs=[piece(j) for j in range(n_sub)] + [rows(128)],
        out_specs=[rows(D_MODEL)] + [rows(128)] * 3,
        out_shape=[jax.ShapeDtypeStruct((lp, D_MODEL), F32)] + [jax.ShapeDtypeStruct((lp, 128), F32)] * 3,
        scratch_shapes=[pltpu.VMEM(small_shape, F32)],
        compiler_params=_params(),
        after=after,
    )(*([x] * n_sub), rope_compact)
    return h, rope, lands


def _in_proj_fwd(h, g, w_in_t, rope, tm, name, exch=None):
    lp = h.shape[0]

    def body(h_ref, g_ref, w_ref, c_ref, s1_ref, s2_ref, a_ref, qkv_ref, bch_ref):
        a = _rms_fwd(h_ref[...], g_ref[...]).astype(BF)
        a_ref[...] = a
        proj = _dot_nt(a, w_ref[...])
        c, s1, s2 = c_ref[...], s1_ref[...], s2_ref[...]
        for j in range(5):
            t = _rope(proj[:, j * 128:(j + 1) * 128], c, s1, s2)
            qkv_ref[:, j * 128:(j + 1) * 128] = (t * SCALE if j < 4 else t).astype(BF)
        qkv_ref[:, 640:768] = proj[:, 640:768].astype(BF)
        bch_ref[...] = proj[:, 768:].astype(BF)

    row = lambda w: pl.BlockSpec((tm, w), lambda i: (i, 0))
    return _call(
        body, exch,
        name=name,
        grid=(lp // tm,),
        in_specs=[row(D_MODEL), _full((1, D_MODEL)), _full((IN_W, D_MODEL)), row(128), row(128), row(128)],
        out_specs=[row(D_MODEL), row(768), row(3 * CONV_W)],
        out_shape=[
            jax.ShapeDtypeStruct((lp, D_MODEL), BF),
            jax.ShapeDtypeStruct((lp, 768), BF),
            jax.ShapeDtypeStruct((lp, 3 * CONV_W), BF),
        ],
        compiler_params=_params(),
    )(h, g, w_in_t, *rope)


def _fold_masks(i):
    r = lax.broadcasted_iota(jnp.int32, (2 * BLOCK, BLOCK), 0) & (BLOCK - 1)
    c = lax.broadcasted_iota(jnp.int32, (2 * BLOCK, BLOCK), 1)
    tri = c > r
    ok = jnp.where(tri, (i - 1) * BLOCK + c, i * BLOCK + c) >= LEAD_PAD
    return tri, ok


def _kv_operand(x, kvh):
    lane = lax.broadcasted_iota(jnp.int32, x.shape, 1)
    zero = jnp.zeros_like(x)
    if kvh == 0:
        lo = jnp.where(lane < HEAD_DIM, x, zero)
        hi = pltpu.roll(lo, HEAD_DIM, 1)
    else:
        hi = jnp.where(lane >= HEAD_DIM, x, zero)
        lo = pltpu.roll(hi, HEAD_DIM, 1)
    return jnp.concatenate([lo, hi], axis=0)


def _split4(t, tri):
    zero = jnp.zeros_like(t[0])
    return jnp.concatenate(
        [jnp.where(tri, t[0], zero), jnp.where(tri, zero, t[0]), jnp.where(tri, t[1], zero), jnp.where(tri, zero, t[1])], axis=1)


def _sink_cols(sink_ref, kvh):
    first = lax.broadcasted_iota(jnp.int32, (2 * BLOCK, 1), 0) < BLOCK
    return [jnp.where(first, sink_ref[0, 4 * kvh + half], sink_ref[0, 4 * kvh + 2 + half]) for half in range(2)]


def _folded_exp(q2, k4, tri, ok, sks):
    s = _dot_nt(q2, k4)
    es, ss = [], []
    for half in range(2):
        s_h = s[:, 2 * half * BLOCK:2 * (half + 1) * BLOCK]
        sf = jnp.where(ok, jnp.where(tri, s_h[:, :BLOCK], s_h[:, BLOCK:]), NEG)
        m = jnp.maximum(jnp.max(sf, axis=-1, keepdims=True), sks[half])
        es.append(jnp.exp(sf - m))
        ss.append(jnp.exp(sks[half] - m))
    sums = _dot(jnp.concatenate(es, axis=0).astype(BF), jnp.ones((BLOCK, BLOCK), BF))
    invs = [1.0 / (sums[2 * half * BLOCK:2 * (half + 1) * BLOCK] + ss[half]) for half in range(2)]
    return es, ss, invs


def _attn_fwd(qkv, sink, name, exch=None):
    lp = qkv.shape[0]
    nb = lp // BLOCK
    per_step = 2

    def one_block(i, sink_ref, q_ref, kvc_ref, kvp_ref, o_ref, p_ref, ps_ref):
        tri, ok = _fold_masks(i)
        kvc, kvp = kvc_ref[...], kvp_ref[...]
        kk = jnp.concatenate([kvp[:, :128], kvc[:, :128]], axis=0)
        vv = jnp.concatenate([kvp[:, 128:], kvc[:, 128:]], axis=0)
        lane = lax.broadcasted_iota(jnp.int32, (BLOCK, 128), 1)
        p_sink = jnp.zeros((BLOCK, 128), F32)
        for kvh in range(2):
            q2 = jnp.concatenate([q_ref[:, 256 * kvh:256 * kvh + 128], q_ref[:, 256 * kvh + 128:256 * kvh + 256]], axis=0)
            es, ss, invs = _folded_exp(q2, _kv_operand(kk, kvh), tri, ok, _sink_cols(sink_ref, kvh))
            pb = [(es[half] * invs[half]).astype(BF) for half in range(2)]
            out = _dot(_split4(pb, tri), _kv_operand(vv, kvh))
            for pair in range(2):
                rows = slice(pair * BLOCK, (pair + 1) * BLOCK)
                o_ref[:, 256 * kvh + 128 * pair:256 * kvh + 128 * (pair + 1)] = out[rows].astype(BF)
                for half in range(2):
                    head = 4 * kvh + 2 * pair + half
                    p_ref[:, 128 * head:128 * (head + 1)] = pb[half][rows]
                    p_sink = jnp.where(lane == head, (ss[half] * invs[half][:, 0:1])[rows], p_sink)
        ps_ref[...] = p_sink

    def body(sink_ref, *refs):
        q_refs, kv_refs = refs[:per_step], refs[per_step:2 * per_step + 1]
        o_ref, p_ref, ps_ref = refs[2 * per_step + 1:]
        for j in range(per_step):
            rows = slice(j * BLOCK, (j + 1) * BLOCK)
            one_block(per_step * pl.program_id(0) + j, sink_ref, q_refs[j], kv_refs[j + 1], kv_refs[j],
                      o_ref.at[rows], p_ref.at[rows], ps_ref.at[rows])

    last = nb - 1
    blk = lambda j: (lambda s: jnp.minimum(per_step * s + j, last))
    out_rows = lambda w: pl.BlockSpec((per_step * BLOCK, w), lambda s: (s, 0))
    return _call(
        body, exch,
        name=name,
        grid=(pl.cdiv(nb, per_step),),
        in_specs=[pl.BlockSpec(memory_space=pltpu.SMEM)]
        + [pl.BlockSpec((BLOCK, ATTN_W), lambda s, j=j: (blk(j)(s), 0)) for j in range(per_step)]
        + [pl.BlockSpec((BLOCK, 256), lambda s: (jnp.maximum(per_step * s - 1, 0), 2))]
        + [pl.BlockSpec((BLOCK, 256), lambda s, j=j: (blk(j)(s), 2)) for j in range(per_step)],
        out_specs=[out_rows(ATTN_W), out_rows(N_Q_HEADS * BLOCK), out_rows(128)],
        out_shape=[jax.ShapeDtypeStruct((lp, ATTN_W), BF), jax.ShapeDtypeStruct((lp, N_Q_HEADS * BLOCK), BF),
                   jax.ShapeDtypeStruct((lp, 128), F32)],
        compiler_params=_params(),
    )(sink, *([qkv] * (2 * per_step + 1)))


def _mix_out_fwd(bch, y_attn, h, conv_w, g_a, g_c, w_out, g_post, tm, name, exch=None):
    lp = h.shape[0]

    def body(bch_ref, ya_ref, h_ref, cw_ref, ga_ref, gc_ref, w_ref, gp_ref, yc_ref, y_ref, z_ref, h2_ref, ext):
        i = pl.program_id(0)

        @pl.when(i == 0)
        def _():
            ext[0:8, :] = jnp.zeros((8, CONV_W), F32)

        b = bch_ref[:, 0:CONV_W].astype(F32)
        u = bch_ref[:, CONV_W:2 * CONV_W].astype(F32) * bch_ref[:, 2 * CONV_W:3 * CONV_W].astype(F32)
        ext[8:8 + tm, :] = u
        yc = cw_ref[0:1, :] * ext[6:6 + tm, :] + cw_ref[1:2, :] * ext[7:7 + tm, :] + cw_ref[2:3, :] * u
        ext[0:8, :] = u[tm - 8:tm, :]
        yc_ref[...] = yc.astype(BF)
        ya = _rms_fwd(ya_ref[...].astype(F32), ga_ref[...]).astype(BF)
        yb = _rms_fwd(b * yc, gc_ref[...]).astype(BF)
        y_ref[:, 0:ATTN_W] = ya
        y_ref[:, ATTN_W:] = yb
        z = _dot(ya, w_ref[0:ATTN_W, :]) + _dot(yb, w_ref[ATTN_W:, :])
        z_ref[...] = z.astype(BF)
        h2_ref[...] = h_ref[...] + _rms_fwd(z, gp_ref[...])

    row = lambda w: pl.BlockSpec((tm, w), lambda i: (i, 0))
    return _call(
        body, exch,
        name=name,
        grid=(lp // tm,),
        in_specs=[
            row(3 * CONV_W), row(ATTN_W), row(D_MODEL), _full((8, CONV_W)), _full((1, ATTN_W)), _full((1, CONV_W)),
            _full((D_MODEL, D_MODEL)), _full((1, D_MODEL)),
        ],
        out_specs=[row(CONV_W), row(D_MODEL), row(D_MODEL), row(D_MODEL)],
        out_shape=[
            jax.ShapeDtypeStruct((lp, CONV_W), BF),
            jax.ShapeDtypeStruct((lp, D_MODEL), BF),
            jax.ShapeDtypeStruct((lp, D_MODEL), BF),
            jax.ShapeDtypeStruct((lp, D_MODEL), F32),
        ],
        scratch_shapes=[pltpu.VMEM((tm + 8, CONV_W), F32)],
        compiler_params=_params(),
    )(bch, y_attn, h, conv_w, g_a, g_c, w_out, g_post)


def _mlp_fwd(h2, g_pre, w_up_t, w_down, g_post, tm, name, exch=None, target=None):
    lp = h2.shape[0]
    sub = math.gcd(tm, BLOCK)
    n_sub, lead = tm // sub, BLOCK // sub
    n_t = n_sub if target is not None else 0

    def body(*refs):
        h_ref, gp_ref, wu_ref, wd_ref, gq_ref = refs[:5]
        t_refs = refs[5:5 + n_t]
        a_ref, up_ref, f_ref, last_ref = refs[5 + n_t:9 + n_t]
        h = h_ref[...]
        a = _rms_fwd(h, gp_ref[...]).astype(BF)
        a_ref[...] = a
        up = _dot_nt(a, wu_ref[...])
        up_ref[...] = up.astype(BF)
        act = jnp.square(jnp.maximum(up, 0.0)).astype(BF)
        f = _dot(act, wd_ref[...])
        f_ref[...] = f
        h3 = h + _rms_fwd(f, gq_ref[...])
        if target is None:
            last_ref[...] = h3
            return
        ls_ref = refs[9 + n_t]
        i = pl.program_id(0)

        @pl.when(i == 0)
        def _():
            ls_ref[...] = jnp.zeros((8, 128), F32)

        sq = jnp.zeros((8, D_MODEL), F32)
        for j in range(n_sub):
            on_tokens = i * n_sub + j >= lead
            d = jnp.where(on_tokens, h3[j * sub:(j + 1) * sub] - t_refs[j][...], 0.0)
            last_ref[j * sub:(j + 1) * sub, :] = d * (1.0 / D_MODEL)
            sq = sq + jnp.sum((d * d).reshape(sub // 8, 8, D_MODEL), axis=0)
        ls_ref[...] += sum(sq[:, k * 128:(k + 1) * 128] for k in range(D_MODEL // 128))

        @pl.when(i == lp // tm - 1)
        def _():
            ls_ref[...] = jnp.full((8, 128), jnp.sum(ls_ref[...]), F32)

    row = lambda w: pl.BlockSpec((tm, w), lambda i: (i, 0))
    piece = lambda j: pl.BlockSpec((sub, D_MODEL), lambda i: (jnp.maximum(i * n_sub + j - lead, 0), 0))
    out_specs = [row(D_MODEL), row(D_FF), row(D_MODEL), row(D_MODEL)]
    out_shape = [
        jax.ShapeDtypeStruct((lp, D_MODEL), BF),
        jax.ShapeDtypeStruct((lp, D_FF), BF),
        jax.ShapeDtypeStruct((lp, D_MODEL), F32),
        jax.ShapeDtypeStruct((lp, D_MODEL), F32),
    ]
    if target is not None:
        out_specs.append(_full_out((8, 128)))
        out_shape.append(jax.ShapeDtypeStruct((8, 128), F32))
    return _call(
        body, exch,
        name=name,
        grid=(lp // tm,),
        in_specs=[row(D_MODEL), _full((1, D_MODEL)), _full((D_FF, D_MODEL)), _full((D_FF, D_MODEL)), _full((1, D_MODEL))]
        + [piece(j) for j in range(n_t)],
        out_specs=out_specs,
        out_shape=out_shape,
        compiler_params=_params(),
    )(h2, g_pre, w_up_t, w_down, g_post, *([target] * n_t))


def _mlp_bwd_dx(dh3, f, up, h2, w_down, w_up_t, g_post, g_pre, tm, name, exch=None):
    lp = h2.shape[0]

    def body(dh3_ref, f_ref, up_ref, h2_ref, wd_ref, wu_ref, gq_ref, gp_ref, df_ref, dup_ref, dh2_ref, dg_ref):
        i = pl.program_id(0)

        @pl.when(i == 0)
        def _():
            dg_ref[...] = jnp.zeros((8, D_MODEL), F32)

        dh3 = dh3_ref[...]
        df, dgq = _rms_bwd(f_ref[...], gq_ref[...], dh3)
        dg_ref[ROW_MLP_POST:ROW_MLP_POST + 1, :] += dgq
        df = df.astype(BF)
        df_ref[...] = df
        dact = _dot_nt(df, wd_ref[...])
        dup = (dact * (2.0 * jnp.maximum(up_ref[...].astype(F32), 0.0))).astype(BF)
        dup_ref[...] = dup
        da = _dot(dup, wu_ref[...])
        dh, dgp = _rms_bwd(h2_ref[...], gp_ref[...], da)
        dg_ref[ROW_MLP_PRE:ROW_MLP_PRE + 1, :] += dgp
        dh2_ref[...] = dh3 + dh

    row = lambda w: pl.BlockSpec((tm, w), lambda i: (i, 0))
    return _call(
        body, exch,
        name=name,
        grid=(lp // tm,),
        in_specs=[
            row(D_MODEL), row(D_MODEL), row(D_FF), row(D_MODEL), _full((D_FF, D_MODEL)), _full((D_FF, D_MODEL)),
            _full((1, D_MODEL)), _full((1, D_MODEL)),
        ],
        out_specs=[row(D_MODEL), row(D_FF), row(D_MODEL), _full_out((8, D_MODEL))],
        out_shape=[
            jax.ShapeDtypeStruct((lp, D_MODEL), BF),
            jax.ShapeDtypeStruct((lp, D_FF), BF),
            jax.ShapeDtypeStruct((lp, D_MODEL), F32),
            jax.ShapeDtypeStruct((8, D_MODEL), F32),
        ],
        compiler_params=_params(),
    )(dh3, f, up, h2, w_down, w_up_t, g_post, g_pre)


def _mlp_bwd_dw(up, df, dup, a2, tm, name):
    lp = up.shape[0]
    nt = lp // tm
    nj = D_FF // D_MODEL

    def body(up_ref, df_ref, dup_ref, a_ref, dwd_ref, dwu_ref, accd, accu):
        i = pl.program_id(1)

        @pl.when(i == 0)
        def _():
            accd[...] = jnp.zeros_like(accd)
            accu[...] = jnp.zeros_like(accu)

        act = jnp.square(jnp.maximum(up_ref[...].astype(F32), 0.0)).astype(BF)
        accd[...] += _dot_tn(act, df_ref[...])
        accu[...] += _dot_tn(dup_ref[...], a_ref[...])

        @pl.when(i == nt - 1)
        def _():
            dwd_ref[...] = accd[...].astype(BF)
            dwu_ref[...] = accu[...].astype(BF)

    return pl.pallas_call(
        body,
        name=name,
        grid=(nj, nt),
        in_specs=[
            pl.BlockSpec((tm, D_MODEL), lambda j, i: (i, j)),
            pl.BlockSpec((tm, D_MODEL), lambda j, i: (i, 0)),
            pl.BlockSpec((tm, D_MODEL), lambda j, i: (i, j)),
            pl.BlockSpec((tm, D_MODEL), lambda j, i: (i, 0)),
        ],
        out_specs=[pl.BlockSpec((D_MODEL, D_MODEL), lambda j, i: (j, 0)), pl.BlockSpec((D_MODEL, D_MODEL), lambda j, i: (j, 0))],
        out_shape=[jax.ShapeDtypeStruct((D_FF, D_MODEL), BF), jax.ShapeDtypeStruct((D_FF, D_MODEL), BF)],
        scratch_shapes=[pltpu.VMEM((D_MODEL, D_MODEL), F32), pltpu.VMEM((D_MODEL, D_MODEL), F32)],
        compiler_params=_params(("arbitrary", "arbitrary")),
    )(up, df, dup, a2)


def _mix_out_bwd(dh2, z, y_attn, yc, bch, y, w_out, g_post, g_a, g_c, conv_w, tm, name, exch=None):
    lp = dh2.shape[0]
    nt = lp // tm

    def body(dh2_ref, z_ref, ya_ref, yc_ref, bch_ref, y_ref, w_ref, gp_ref, ga_ref, gc_ref, cw_ref,
             dya_ref, dbch_ref, dg_ref, dwo_ref, ext, acco):
        i = pl.program_id(0)
        dcw_ref = dg_ref.at[ROW_CONV:ROW_CONV + 3, 0:CONV_W]

        @pl.when(i == 0)
        def _():
            ext[tm:tm + 8, :] = jnp.zeros((8, CONV_W), F32)
            dg_ref[...] = jnp.zeros((8, D_MODEL), F32)
            acco[...] = jnp.zeros_like(acco)

        dz, dgp = _rms_bwd(z_ref[...].astype(F32), gp_ref[...], dh2_ref[...])
        dg_ref[ROW_MIX_POST:ROW_MIX_POST + 1, :] += dgp
        dz = dz.astype(BF)
        acco[...] += _dot_tn(y_ref[...], dz)
        dya_n = _dot_nt(dz, w_ref[0:ATTN_W, :])
        dyb_n = _dot_nt(dz, w_ref[ATTN_W:, :])
        dya, dga = _rms_bwd(ya_ref[...].astype(F32), ga_ref[...], dya_n)
        dg_ref[ROW_GROUP_G:ROW_GROUP_G + 1, 0:ATTN_W] += dga
        dya_ref[...] = dya
        b = bch_ref[:, 0:CONV_W].astype(F32)
        c = bch_ref[:, CONV_W:2 * CONV_W].astype(F32)
        hc = bch_ref[:, 2 * CONV_W:3 * CONV_W].astype(F32)
        u = c * hc
        yc_v = yc_ref[...].astype(F32)
        dyconv, dgc = _rms_bwd(b * yc_v, gc_ref[...], dyb_n)
        dg_ref[ROW_GROUP_G:ROW_GROUP_G + 1, ATTN_W:] += dgc
        dbch_ref[:, 0:CONV_W] = (dyconv * yc_v).astype(BF)
        dyc = dyconv * b
        ext[0:tm, :] = dyc
        d1 = ext[1:1 + tm, :]
        d2 = ext[2:2 + tm, :]
        du = cw_ref[2:3, :] * dyc + cw_ref[1:2, :] * d1 + cw_ref[0:1, :] * d2
        ext[tm:tm + 8, :] = dyc[0:8, :]
        dbch_ref[:, CONV_W:2 * CONV_W] = (du * hc).astype(BF)
        dbch_ref[:, 2 * CONV_W:3 * CONV_W] = (du * c).astype(BF)
        dcw_ref[0:1, :] += jnp.sum(u * d2, axis=0, keepdims=True)
        dcw_ref[1:2, :] += jnp.sum(u * d1, axis=0, keepdims=True)
        dcw_ref[2:3, :] += jnp.sum(u * dyc, axis=0, keepdims=True)

        @pl.when(i == nt - 1)
        def _():
            dwo_ref[...] = acco[...].astype(BF)

    row = lambda w: pl.BlockSpec((tm, w), lambda i: (nt - 1 - i, 0))
    return _call(
        body, exch,
        name=name,
        grid=(nt,),
        in_specs=[
            row(D_MODEL), row(D_MODEL), row(ATTN_W), row(CONV_W), row(3 * CONV_W), row(D_MODEL), _full((D_MODEL, D_MODEL)),
            _full((1, D_MODEL)), _full((1, ATTN_W)), _full((1, CONV_W)), _full((8, CONV_W)),
        ],
        out_specs=[row(ATTN_W), row(3 * CONV_W), _full_out((8, D_MODEL)), _full_out((D_MODEL, D_MODEL))],
        out_shape=[
            jax.ShapeDtypeStruct((lp, ATTN_W), F32),
            jax.ShapeDtypeStruct((lp, 3 * CONV_W), BF),
            jax.ShapeDtypeStruct((8, D_MODEL), F32),
            jax.ShapeDtypeStruct((D_MODEL, D_MODEL), BF),
        ],
        scratch_shapes=[pltpu.VMEM((tm + 8, CONV_W), F32), pltpu.VMEM((D_MODEL, D_MODEL), F32)],
        compiler_params=_params(),
    )(dh2, z, y_attn, yc, bch, y, w_out, g_post, g_a, g_c, conv_w)


def _attn_bwd(qkv, o, do, probs, p_sink, rope, name, exch=None):
    lp = qkv.shape[0]
    nb = lp // BLOCK

    def body(q_ref, kvc_ref, kvp_ref, o_ref, do_ref, p_ref, ps_ref, cq_ref, s1q_ref, s2q_ref, ck_ref, s1k_ref, s2k_ref,
             dq_ref, dkv_ref, dsink_ref, carry):
        i = pl.program_id(0)

        @pl.when(i == 0)
        def _():
            carry[...] = jnp.zeros_like(carry)
            dsink_ref[...] = jnp.zeros((8, 128), F32)

        def finish(tot):
            dk = _rope_t(tot[:, :128], ck_ref[...], s1k_ref[...], s2k_ref[...])
            dkv_ref[:, 0:128] = dk.astype(BF)
            dkv_ref[:, 128:256] = tot[:, 128:].astype(BF)

        @pl.when(i < nb)
        def _():
            tri, _ = _fold_masks(i)
            kvc, kvp = kvc_ref[...], kvp_ref[...]
            kk = jnp.concatenate([kvp[:, :128], kvc[:, :128]], axis=0)
            vv = jnp.concatenate([kvp[:, 128:], kvc[:, 128:]], axis=0)
            lane = lax.broadcasted_iota(jnp.int32, (BLOCK, 128), 1)
            lane2 = lax.broadcasted_iota(jnp.int32, (2 * BLOCK, 128), 1)
            rope_q = (cq_ref[...], s1q_ref[...], s2q_ref[...])
            deltas = jnp.zeros((BLOCK, 128), F32)
            folded = []
            for kvh in range(2):
                c0 = 256 * kvh
                q2 = jnp.concatenate([q_ref[:, c0:c0 + 128], q_ref[:, c0 + 128:c0 + 256]], axis=0)
                do2 = jnp.concatenate([do_ref[:, c0:c0 + 128], do_ref[:, c0 + 128:c0 + 256]], axis=0)
                o2 = jnp.concatenate([o_ref[:, c0:c0 + 128], o_ref[:, c0 + 128:c0 + 256]], axis=0).astype(F32)
                k4, v4 = _kv_operand(kk, kvh), _kv_operand(vv, kvh)
                prod = do2 * o2
                dob = do2.astype(BF)
                dp = _dot_nt(dob, v4)
                ds, pb = [], []
                for half in range(2):
                    heads = [4 * kvh + 2 * pair + half for pair in range(2)]
                    p = jnp.concatenate([p_ref[:, 128 * h:128 * (h + 1)] for h in heads], axis=0)
                    sel = (lane2 < HEAD_DIM) if half == 0 else (lane2 >= HEAD_DIM)
                    delta = jnp.sum(jnp.where(sel, prod, 0.0), axis=-1, keepdims=True)
                    dp_h = dp[:, 2 * half * BLOCK:2 * (half + 1) * BLOCK]
                    ds.append((p.astype(F32) * (jnp.where(tri, dp_h[:, :BLOCK], dp_h[:, BLOCK:]) - delta)).astype(BF))
                    pb.append(p)
                    for pair in range(2):
                        deltas = jnp.where(lane == heads[pair], delta[pair * BLOCK:(pair + 1) * BLOCK], deltas)
                ds4, p4 = _split4(ds, tri), _split4(pb, tri)
                dq2 = _dot(ds4, k4) * SCALE
                dq_ref[:, c0:c0 + 128] = _rope_t(dq2[:BLOCK], *rope_q).astype(BF)
                dq_ref[:, c0 + 128:c0 + 256] = _rope_t(dq2[BLOCK:], *rope_q).astype(BF)
                rk, rv = _dot_tn(ds4, q2), _dot_tn(p4, dob)
                own = (lane < HEAD_DIM) if kvh == 0 else (lane >= HEAD_DIM)
                group = []
                for r in (rk, rv):
                    for blk in range(2):
                        t = jnp.where(lane < HEAD_DIM, r[blk * BLOCK:(blk + 1) * BLOCK], r[(2 + blk) * BLOCK:(3 + blk) * BLOCK])
                        group.append(jnp.where(own, t + pltpu.roll(t, HEAD_DIM, 1), 0.0))
                folded.append(group)
            dsink_ref[ROW_SINK:ROW_SINK + 1, :] -= jnp.sum(ps_ref[...] * deltas, axis=0, keepdims=True)
            dk_p, dk_c, dv_p, dv_c = [folded[0][t] + folded[1][t] for t in range(4)]
            finish(carry[...] + jnp.concatenate([dk_p, dv_p], axis=1))
            carry[...] = jnp.concatenate([dk_c, dv_c], axis=1)

        @pl.when(i == nb)
        def _():
            finish(carry[...])

    qi = lambda i: jnp.minimum(i, nb - 1)
    ki = lambda i: jnp.maximum(i - 1, 0)
    tab_q = pl.BlockSpec((BLOCK, 128), lambda i: (qi(i), 0))
    tab_k = pl.BlockSpec((BLOCK, 128), lambda i: (ki(i), 0))
    return _call(
        body, exch,
        name=name,
        grid=(nb + 1,),
        in_specs=[
            pl.BlockSpec((BLOCK, ATTN_W), lambda i: (qi(i), 0)),
            pl.BlockSpec((BLOCK, 256), lambda i: (qi(i), 2)),
            pl.BlockSpec((BLOCK, 256), lambda i: (jnp.maximum(qi(i) - 1, 0), 2)),
            pl.BlockSpec((BLOCK, ATTN_W), lambda i: (qi(i), 0)),
            pl.BlockSpec((BLOCK, ATTN_W), lambda i: (qi(i), 0)),
            pl.BlockSpec((BLOCK, N_Q_HEADS * BLOCK), lambda i: (qi(i), 0)),
            tab_q, tab_q, tab_q, tab_q, tab_k, tab_k, tab_k,
        ],
        out_specs=[
            pl.BlockSpec((BLOCK, ATTN_W), lambda i: (qi(i), 0)),
            pl.BlockSpec((BLOCK, 256), lambda i: (ki(i), 0)),
            pl.BlockSpec((8, 128), lambda i: (0, 0)),
        ],
        out_shape=[
            jax.ShapeDtypeStruct((lp, ATTN_W), BF),
            jax.ShapeDtypeStruct((lp, 256), BF),
            jax.ShapeDtypeStruct((8, 128), F32),
        ],
        scratch_shapes=[pltpu.VMEM((BLOCK, 256), F32)],
        compiler_params=_params(),
    )(qkv, qkv, qkv, o, do, probs, p_sink, *rope, *rope)


def _in_proj_bwd_dx(dq, dkv, dbch, w_in_t, h, dh2, g, tm, name, exch=None):
    lp = h.shape[0]

    def body(dq_ref, dkv_ref, dbch_ref, w_ref, h_ref, dh2_ref, g_ref, dh_ref, dg_ref):
        i = pl.program_id(0)

        @pl.when(i == 0)
        def _():
            dg_ref[...] = jnp.zeros((8, D_MODEL), F32)

        da = _dot(jnp.concatenate([dq_ref[...], dkv_ref[...], dbch_ref[...]], axis=1), w_ref[...])
        dh, dg = _rms_bwd(h_ref[...], g_ref[...], da)
        dg_ref[ROW_MIX_PRE:ROW_MIX_PRE + 1, :] += dg
        dh_ref[...] = dh2_ref[...] + dh

    row = lambda w: pl.BlockSpec((tm, w), lambda i: (i, 0))
    return _call(
        body, exch,
        name=name,
        grid=(lp // tm,),
        in_specs=[row(ATTN_W), row(256), row(3 * CONV_W), _full((IN_W, D_MODEL)), row(D_MODEL), row(D_MODEL), _full((1, D_MODEL))],
        out_specs=[row(D_MODEL), _full_out((8, D_MODEL))],
        out_shape=[jax.ShapeDtypeStruct((lp, D_MODEL), F32), jax.ShapeDtypeStruct((8, D_MODEL), F32)],
        compiler_params=_params(),
    )(dq, dkv, dbch, w_in_t, h, dh2, g)


def _mix_bwd_dw(dq, dkv, dbch, a, tm, name, exch=None):
    lp = a.shape[0]
    nt = lp // tm

    def body(dq_ref, dkv_ref, dbch_ref, a_ref, dwi_ref, acci):
        i = pl.program_id(0)

        @pl.when(i == 0)
        def _():
            acci[...] = jnp.zeros_like(acci)

        a_v = a_ref[...]
        acci[0:512, :] += _dot_tn(dq_ref[...], a_v)
        acci[512:768, :] += _dot_tn(dkv_ref[...], a_v)
        acci[768:, :] += _dot_tn(dbch_ref[...], a_v)

        @pl.when(i == nt - 1)
        def _():
            dwi_ref[...] = acci[...].astype(BF)

    row = lambda w: pl.BlockSpec((tm, w), lambda i: (i, 0))
    return _call(
        body, exch,
        name=name,
        grid=(nt,),
        in_specs=[row(ATTN_W), row(256), row(3 * CONV_W), row(D_MODEL)],
        out_specs=[_full_out((IN_W, D_MODEL))],
        out_shape=[jax.ShapeDtypeStruct((IN_W, D_MODEL), BF)],
        scratch_shapes=[pltpu.VMEM((IN_W, D_MODEL), F32)],
        compiler_params=_params(),
    )(dq, dkv, dbch, a)


def _mesh_place():
    x, y, c = lax.axis_index("x"), lax.axis_index("y"), lax.axis_index("c")
    return x, y, c, 4 * x + 2 * y + c


def _peer(x, y, c, k):
    px = 1 - x if k & 4 else x
    py = 1 - y if k & 2 else y
    pc = 1 - c if k & 1 else c
    return (px, py, pc), 4 * px + 2 * py + pc


SIBLING = 1
SAME_CORE = (2, 4, 6)
OTHER_CORE = (3, 5, 7)


class _Exchange:
    def __init__(self, pieces):
        self.srcs = [s for s, _ in pieces]
        self.to_all = [g for _, g in pieces]
        self.n = len(pieces)
        self.land_shapes = [
            jax.ShapeDtypeStruct((N_DEV,) + (s.shape if g else s.shape[1:]), s.dtype) for s, g in pieces]
        self.sem_shapes = [pltpu.SemaphoreType.DMA((self.n, N_DEV - 1)), pltpu.SemaphoreType.DMA((self.n, N_DEV - 1)),
                           pltpu.SemaphoreType.DMA((self.n,))]
        self.forwards = any(self.to_all)

    def _ops(self, srcs, lands, sems):
        send_sems, recv_sems, local_sems = sems
        x, y, c, me = _mesh_place()

        def remote(p, k, src, slot, to):
            return pltpu.make_async_remote_copy(
                src_ref=src, dst_ref=lands[p].at[slot], send_sem=send_sems.at[p, k - 1], recv_sem=recv_sems.at[p, k - 1],
                device_id=to, device_id_type=MESH)

        def own(p):
            return pltpu.make_async_copy(srcs[p] if self.to_all[p] else srcs[p].at[me], lands[p].at[me], local_sems.at[p])

        def direct(p, k):
            peer, pidx = _peer(x, y, c, k)
            return remote(p, k, srcs[p] if self.to_all[p] else srcs[p].at[pidx], me, peer)

        def forward(p, k):
            sibling, _ = _peer(x, y, c, SIBLING)
            _, origin = _peer(x, y, c, k ^ SIBLING)
            return remote(p, k, lands[p].at[origin], origin, sibling)

        def arrival(p, k):
            peer, pidx = _peer(x, y, c, k)
            return remote(p, k, lands[p].at[pidx], pidx, peer)

        return own, direct, forward, arrival

    def start(self, srcs, lands, sems):
        own, direct, _, _ = self._ops(srcs, lands, sems)
        for p in range(self.n):
            own(p).start()
            for k in ((SIBLING,) + SAME_CORE) if self.to_all[p] else range(1, N_DEV):
                direct(p, k).start()

    def forward(self, srcs, lands, sems):
        _, _, forward, arrival = self._ops(srcs, lands, sems)
        for p in range(self.n):
            if self.to_all[p]:
                for k in SAME_CORE:
                    arrival(p, k).wait_recv()
                    forward(p, k ^ SIBLING).start()

    def finish(self, srcs, lands, sems):
        own, direct, forward, arrival = self._ops(srcs, lands, sems)
        for p in range(self.n):
            for k in ((SIBLING,) + OTHER_CORE) if self.to_all[p] else range(1, N_DEV):
                arrival(p, k).wait_recv()
        for p in range(self.n):
            for k in range(1, N_DEV):
                (forward(p, k) if self.to_all[p] and k in OTHER_CORE else direct(p, k)).wait_send()
            own(p).wait()


def _call(body, exch, *, name, grid, in_specs, out_specs, out_shape, scratch_shapes=(), compiler_params, after=None):
    if exch is None:
        return pl.pallas_call(body, name=name, grid=grid, in_specs=in_specs, out_specs=out_specs, out_shape=out_shape,
                              scratch_shapes=scratch_shapes, compiler_params=compiler_params)
    n_in, n_out, n_scr, n_x = len(in_specs), len(out_shape), len(scratch_shapes), exch.n
    steps = math.prod(grid)

    def carrying(*refs):
        a, b, c, d, e = n_in, n_in + n_x, n_in + n_x + n_out, n_in + 2 * n_x + n_out, n_in + 2 * n_x + n_out + n_scr
        ins, srcs, outs, lands, scr, sems = refs[:a], refs[a:b], refs[b:c], refs[c:d], refs[d:e], refs[e:]
        step = functools.reduce(lambda acc, t: acc * grid[t] + pl.program_id(t), range(len(grid)), 0)

        @pl.when(step == 0)
        def _():
            exch.start(srcs, lands, sems)

        body(*ins, *outs, *scr)

        if exch.forwards:
            @pl.when(step == max(0, steps - 1 - (steps + 7) // 8))
            def _():
                exch.forward(srcs, lands, sems)

        @pl.when(step == steps - 1)
        def _():
            exch.finish(srcs, lands, sems)
            if after is not None:
                after(lands, *ins, *outs, *scr)

    hbm = pl.BlockSpec(memory_space=pl.ANY)
    call = pl.pallas_call(
        carrying, name=name, grid=grid, in_specs=list(in_specs) + [hbm] * n_x, out_specs=list(out_specs) + [hbm] * n_x,
        out_shape=list(out_shape) + exch.land_shapes, scratch_shapes=list(scratch_shapes) + exch.sem_shapes,
        compiler_params=compiler_params)

    def run(*args):
        res = call(*args, *exch.srcs)
        return list(res[:n_out]), list(res[n_out:])

    return run


def _sum_small(part):
    exch = _Exchange([(part, True)])

    def body(part_ref, out_ref, land, *sems):
        exch.start([part_ref], [land], sems)
        exch.forward([part_ref], [land], sems)
        exch.finish([part_ref], [land], sems)
        acc = land[0]
        for d in range(1, N_DEV):
            acc = acc + land[d]
        out_ref[...] = acc

    vmem = pl.BlockSpec(memory_space=pltpu.VMEM)
    return pl.pallas_call(
        body,
        name="sum_small",
        in_specs=[vmem],
        out_specs=vmem,
        out_shape=jax.ShapeDtypeStruct(part.shape, F32),
        scratch_shapes=[pltpu.VMEM(exch.land_shapes[0].shape, F32)] + exch.sem_shapes,
    )(part)


def _adamw(w, g, m, v):
    m = ADAM_B1 * m + (1.0 - ADAM_B1) * g
    v = ADAM_B2 * v + (1.0 - ADAM_B2) * jnp.square(g)
    m_hat = m / (1.0 - ADAM_B1 ** ADAM_STEP)
    v_hat = v / (1.0 - ADAM_B2 ** ADAM_STEP)
    delta = -ADAM_LR * (m_hat / (jnp.sqrt(v_hat) + ADAM_EPS) + ADAM_WD * w)
    return delta, m, v


def _landed_specs(tr, wd):
    return [pl.BlockSpec((N_DEV, tr, wd), lambda l, i, ll=ll: (0, jnp.where(l == ll, i, 0), 0)) for ll in range(DEPTH)]


def _device_sum(r_ref):
    acc = r_ref[0].astype(F32)
    for d in range(1, N_DEV):
        acc = acc + r_ref[d].astype(F32)
    return acc


def _sum_adamw(recv, w, m, v, tr, name, transposed=False):
    _, r, wd = recv[0].shape

    def body(*refs):
        w_ref, m_ref, v_ref, g_ref, d_ref, mo_ref, vo_ref = refs[DEPTH:]
        for ll in range(DEPTH):
            @pl.when(pl.program_id(0) == ll)
            def _(ll=ll):
                g = _device_sum(refs[ll])
                g = g.T if transposed else g
                g_ref[0] = g
                d_ref[0], mo_ref[0], vo_ref[0] = _adamw(w_ref[0], g, m_ref[0], v_ref[0])

    if transposed:
        blk = pl.BlockSpec((1, wd, tr), lambda l, i: (l, 0, i))
        shape = jax.ShapeDtypeStruct((DEPTH, wd, r), F32)
    else:
        blk = pl.BlockSpec((1, tr, wd), lambda l, i: (l, i, 0))
        shape = jax.ShapeDtypeStruct((DEPTH, r, wd), F32)
    return pl.pallas_call(
        body,
        name=name,
        grid=(DEPTH, r // tr),
        in_specs=_landed_specs(tr, wd) + [blk, blk, blk],
        out_specs=[blk] * 4,
        out_shape=[shape] * 4,
        compiler_params=_params(("arbitrary", "arbitrary")),
    )(*recv, w, m, v)


def _adamw_small(ws, gs, ms, vs):
    n = len(ws)

    def body(*refs):
        w_r, g_r, m_r, v_r = refs[:n], refs[n:2 * n], refs[2 * n:3 * n], refs[3 * n:4 * n]
        d_o, m_o, v_o = refs[4 * n:5 * n], refs[5 * n:6 * n], refs[6 * n:7 * n]
        for t in range(n):
            d_o[t][...], m_o[t][...], v_o[t][...] = _adamw(w_r[t][...], g_r[t][...], m_r[t][...], v_r[t][...])

    vmem = pl.BlockSpec(memory_space=pltpu.VMEM)
    shapes = [jax.ShapeDtypeStruct(w.shape, F32) for w in ws]
    outs = pl.pallas_call(
        body,
        name="adamw_small",
        in_specs=[vmem] * (4 * n),
        out_specs=[vmem] * (3 * n),
        out_shape=shapes * 3,
    )(*ws, *gs, *ms, *vs)
    return outs[:n], outs[n:2 * n], outs[2 * n:]


def kernel(x, meta_tokens, mix_pre_g, w_in, conv_w, sinks, attn_out_g, conv_out_g, w_out, mix_post_g, mlp_pre_g, w_up, w_down, mlp_post_g, loss_target, m_meta_tokens, m_mix_pre_g, m_w_in, m_conv_w, m_sinks, m_attn_out_g, m_conv_out_g, m_w_out, m_mix_post_g, m_mlp_pre_g, m_w_up, m_w_down, m_mlp_post_g, v_meta_tokens, v_mix_pre_g, v_w_in, v_conv_w, v_sinks, v_attn_out_g, v_conv_out_g, v_w_out, v_mix_post_g, v_mlp_pre_g, v_w_up, v_w_down, v_mlp_post_g):
    seq = x.shape[1]
    lp = BLOCK + seq
    tm = _row_tile(lp)
    tm_mlp = _row_tile(lp, (320, 256, 128))
    tm_dw_mlp = _row_tile(lp, (1664, 1040, 640, 384, 256, 128))
    tm_dw_mix = _row_tile(lp, (1664, 832, 640, 384, 256, 128))
    me = 4 * lax.axis_index("x") + 2 * lax.axis_index("y") + lax.axis_index("c")
    cshard = CONV_W // N_DEV
    mshard = D_MODEL // N_DEV

    gather_with = {
        ("in_proj_fwd", 0): [("down", 0)], ("attn_fwd", 0): [("out", 0), ("up", 0)], ("mix_out_fwd", 0): [("in", 1)],
        ("mlp_fwd", 0): [("out", 1), ("up", 1), ("down", 1)],
    }
    scatter_with = {
        ("attn_bwd", 1): [("down", 1)], ("mix_bwd_dw", 1): [("out", 1)], ("mlp_bwd_dx", 0): [("up", 1), ("in", 1)],
        ("mix_out_bwd", 0): [("up", 0)], ("attn_bwd", 0): [("down", 0)], ("mix_bwd_dw", 0): [("out", 0)],
        ("in_proj_bwd_dx", 0): [("in", 0)],
    }
    shard = {"in": jnp.swapaxes(w_in, 1, 2).astype(BF), "out": w_out.astype(BF),
             "up": jnp.swapaxes(w_up, 1, 2).astype(BF), "down": w_down.astype(BF)}
    weight = {}
    grad = {}
    landed = {}

    def run(fn, kind, l, *args):
        key, name = (kind, l), f"{kind}_{l}"
        if key in gather_with:
            blocks = gather_with[key]
            outs, lands = fn(*args, name, _Exchange([(shard[n][k], True) for n, k in blocks]))
            for b, land in zip(blocks, lands):
                weight[b] = land.reshape(-1, D_MODEL)
            return outs
        if key in scatter_with:
            blocks = scatter_with[key]
            outs, lands = fn(*args, name, _Exchange([(grad[b].reshape(N_DEV, -1, D_MODEL), False) for b in blocks]))
            landed.update(zip(blocks, lands))
            return outs
        return fn(*args, name)

    small = jnp.zeros((24, 128), F32)
    small = small.at[0:N_META, :].set(meta_tokens)
    small = small.at[N_META:N_META + 6, 0:cshard].set(conv_w.reshape(6, cshard))
    first = _Exchange([(shard["in"][0], True), (small, True)])
    h, rope, (first_in, g_small) = _build_h(x[0], _rope_table(lp), tm, first, 1, "build_h")
    weight[("in", 0)] = first_in.reshape(-1, D_MODEL)
    cw = g_small[:, N_META:N_META + 6, 0:cshard].reshape(N_DEV, DEPTH, 3, cshard)
    cw = jnp.transpose(cw, (1, 2, 0, 3)).reshape(DEPTH, 3, CONV_W)
    conv_full = jnp.concatenate([cw, jnp.zeros((DEPTH, 5, CONV_W), F32)], axis=1)

    row1 = lambda a, l: a[l].reshape(1, -1)

    saved = []
    for l in range(DEPTH):
        a, qkv, bch = run(_in_proj_fwd, "in_proj_fwd", l, h, row1(mix_pre_g, l), weight[("in", l)], rope, tm)
        y_attn, probs, p_sink = run(_attn_fwd, "attn_fwd", l, qkv, row1(sinks, l))
        yc, y, z, h2 = run(_mix_out_fwd, "mix_out_fwd", l, bch, y_attn, h, conv_full[l], row1(attn_out_g, l),
                       row1(conv_out_g, l), weight[("out", l)], row1(mix_post_g, l), tm)
        mlp = _mlp_fwd if l < DEPTH - 1 else functools.partial(_mlp_fwd, target=loss_target[0])
        a2, up, f, *rest = run(mlp, "mlp_fwd", l, h2, row1(mlp_pre_g, l), weight[("up", l)], weight[("down", l)],
                               row1(mlp_post_g, l), tm_mlp)
        saved.append((h, a, qkv, bch, y_attn, probs, p_sink, yc, y, z, h2, a2, up, f))
        h = rest[0]
    dh, loss_part = rest[0], rest[1][0, 0] * (0.5 / D_MODEL)

    gsmall = [None] * DEPTH
    for l in reversed(range(DEPTH)):
        h0, a, qkv, bch, y_attn, probs, p_sink, yc, y, z, h2, a2, up, f = saved[l]
        df, dup, dh2, dg_mlp = run(_mlp_bwd_dx, "mlp_bwd_dx", l, dh, f, up, h2, weight[("down", l)], weight[("up", l)],
                                   row1(mlp_post_g, l), row1(mlp_pre_g, l), tm_mlp)
        grad[("down", l)], grad[("up", l)] = _mlp_bwd_dw(up, df, dup, a2, tm_dw_mlp, f"mlp_bwd_dw_{l}")
        dya, dbch, dg_mix, grad[("out", l)] = run(
            _mix_out_bwd, "mix_out_bwd", l, dh2, z, y_attn, yc, bch, y, weight[("out", l)], row1(mix_post_g, l),
            row1(attn_out_g, l), row1(conv_out_g, l), conv_full[l], tm)
        dq, dkv, dsink = run(_attn_bwd, "attn_bwd", l, qkv, y_attn, dya, probs, p_sink, rope)
        grad[("in", l)], = run(_mix_bwd_dw, "mix_bwd_dw", l, dq, dkv, dbch, a, tm_dw_mix)
        dh, dg_in = run(_in_proj_bwd_dx, "in_proj_bwd_dx", l, dq, dkv, dbch, weight[("in", l)], h0, dh2,
                        row1(mix_pre_g, l), tm)
        tile_a = dg_mlp + dg_in + jnp.pad(dsink, ((0, 0), (0, D_MODEL - 128)))
        gsmall[l] = (tile_a, dg_mix)
    grad_x = dh[BLOCK:][None]

    loss_tile = jnp.zeros((8, D_MODEL), F32).at[ROW_LOSS, 0].set(loss_part)
    tot = _sum_small(jnp.concatenate(
        [gsmall[0][0] + loss_tile, gsmall[0][1], gsmall[1][0], gsmall[1][1], dh[LEAD_PAD:BLOCK]], axis=0))
    loss = tot[ROW_LOSS, 0]
    ta = [tot[16 * l:16 * l + 8] for l in range(DEPTH)]
    tb = [tot[16 * l + 8:16 * l + 16] for l in range(DEPTH)]
    pick = lambda tiles, r0, r1, c0, c1: jnp.stack([t[r0:r1, c0:c1] for t in tiles])
    g_mlp_post = pick(ta, ROW_MLP_POST, ROW_MLP_POST + 1, 0, D_MODEL).reshape(DEPTH, D_MODEL)
    g_mlp_pre = pick(ta, ROW_MLP_PRE, ROW_MLP_PRE + 1, 0, D_MODEL).reshape(DEPTH, D_MODEL)
    g_mix_pre = pick(ta, ROW_MIX_PRE, ROW_MIX_PRE + 1, 0, D_MODEL).reshape(DEPTH, D_MODEL)
    g_sinks = pick(ta, ROW_SINK, ROW_SINK + 1, 0, N_Q_HEADS).reshape(DEPTH, N_Q_HEADS)
    g_mix_post = pick(tb, ROW_MIX_POST, ROW_MIX_POST + 1, 0, D_MODEL).reshape(DEPTH, D_MODEL)
    g_attn_out = pick(tb, ROW_GROUP_G, ROW_GROUP_G + 1, 0, ATTN_W).reshape(DEPTH, ATTN_W)
    g_conv_out = pick(tb, ROW_GROUP_G, ROW_GROUP_G + 1, ATTN_W, D_MODEL).reshape(DEPTH, CONV_W)
    g_conv_full = pick(tb, ROW_CONV, ROW_CONV + 3, 0, CONV_W)
    g_conv = lax.dynamic_slice_in_dim(g_conv_full, me * cshard, cshard, axis=2)
    g_meta = lax.dynamic_slice_in_dim(tot[16 * DEPTH:16 * DEPTH + N_META], me * mshard, mshard, axis=1)

    r_in, r_out, r_up, r_down = [[landed[(n, l)] for l in range(DEPTH)] for n in ("in", "out", "up", "down")]
    t12 = lambda a: jnp.swapaxes(a, 1, 2)
    g_w_in, d_w_in, nm_w_in, nv_w_in = map(t12, _sum_adamw(r_in, t12(w_in), t12(m_w_in), t12(v_w_in), 96, "adamw_w_in"))
    g_w_up, d_w_up, nm_w_up, nv_w_up = _sum_adamw(r_up, w_up, m_w_up, v_w_up, 128, "adamw_w_up", transposed=True)
    g_w_out, d_w_out, nm_w_out, nv_w_out = _sum_adamw(r_out, w_out, m_w_out, v_w_out, 128, "adamw_w_out")
    g_w_down, d_w_down, nm_w_down, nv_w_down = _sum_adamw(r_down, w_down, m_w_down, v_w_down, 128, "adamw_w_down")

    ws = [meta_tokens, mix_pre_g, conv_w.reshape(6, cshard), sinks, attn_out_g, conv_out_g, mix_post_g, mlp_pre_g, mlp_post_g]
    gs = [g_meta, g_mix_pre, g_conv.reshape(6, cshard), g_sinks, g_attn_out, g_conv_out, g_mix_post, g_mlp_pre, g_mlp_post]
    ms = [m_meta_tokens, m_mix_pre_g, m_conv_w.reshape(6, cshard), m_sinks, m_attn_out_g, m_conv_out_g, m_mix_post_g,
          m_mlp_pre_g, m_mlp_post_g]
    vs = [v_meta_tokens, v_mix_pre_g, v_conv_w.reshape(6, cshard), v_sinks, v_attn_out_g, v_conv_out_g, v_mix_post_g,
          v_mlp_pre_g, v_mlp_post_g]
    ds, nms, nvs = _adamw_small(ws, gs, ms, vs)

    def order(meta, mix_pre, cv, sk, a_out, c_out, mix_post, mlp_pre, mlp_post, win, wout, wup, wdown):
        return [meta, mix_pre, win, cv.reshape(DEPTH, 3, cshard), sk, a_out, c_out, wout, mix_post, mlp_pre, wup, wdown, mlp_post]

    grads = order(*gs, g_w_in, g_w_out, g_w_up, g_w_down)
    deltas = order(*ds, d_w_in, d_w_out, d_w_up, d_w_down)
    new_m = order(*nms, nm_w_in, nm_w_out, nm_w_up, nm_w_down)
    new_v = order(*nvs, nv_w_in, nv_w_out, nv_w_up, nv_w_down)
    return (loss, grad_x, *grads, *deltas, *new_m, *new_v)
```

```python
import functools
import math

import jax
import jax.numpy as jnp
from jax import lax
from jax.experimental import pallas as pl
from jax.experimental.pallas import tpu as pltpu

F32 = jnp.float32
BF = jnp.bfloat16

D_MODEL = 1024
ATTN_W = 512
CONV_W = 512
HEAD_DIM = 64
N_Q_HEADS = 8
ROT_DIM = 16
D_FF = 4096
IN_W = 2304
N_META = 16
BLOCK = 128
LEAD_PAD = BLOCK - N_META
ROPE_THETA = 500000.0
EPS = 1e-6
N_DEV = 8
DEPTH = 2
NEG = -1e30
SCALE = HEAD_DIM ** -0.5

ADAM_LR = 0.001
ADAM_B1 = 0.9
ADAM_B2 = 0.999
ADAM_EPS = 1e-08
ADAM_WD = 0.01
ADAM_STEP = 10

ROW_MLP_POST, ROW_MLP_PRE, ROW_MIX_PRE, ROW_SINK, ROW_LOSS = 0, 1, 2, 3, 4
ROW_MIX_POST, ROW_GROUP_G, ROW_CONV = 0, 1, 2

VMEM_LIMIT = 56 * 1024 * 1024
MESH = pl.DeviceIdType.MESH


def _dot(a, b):
    return jnp.dot(a, b, preferred_element_type=F32)


def _dot_nt(a, b):
    return lax.dot_general(a, b, (((1,), (1,)), ((), ())), preferred_element_type=F32)


def _dot_tn(a, b):
    return lax.dot_general(a, b, (((0,), (0,)), ((), ())), preferred_element_type=F32)


def _rms_fwd(x, g):
    r = lax.rsqrt(jnp.mean(x * x, axis=-1, keepdims=True) + EPS)
    return x * r * g


def _rms_bwd(x, g, dy):
    r = lax.rsqrt(jnp.mean(x * x, axis=-1, keepdims=True) + EPS)
    xh = x * r
    t = dy * g
    dx = r * (t - xh * jnp.mean(t * xh, axis=-1, keepdims=True))
    dg = jnp.sum(dy * xh, axis=0, keepdims=True)
    return dx, dg


def _row_tile(lp, cands=(640, 512, 384, 256, 128)):
    for t in cands:
        if lp % t == 0:
            return t
    raise ValueError(f"row count {lp} is not a multiple of 128")


def _full(shape):
    n = len(shape)
    return pl.BlockSpec(shape, lambda *_: (0,) * n, pipeline_mode=pl.Buffered(1))


def _full_out(shape):
    n = len(shape)
    return pl.BlockSpec(shape, lambda *_: (0,) * n)


def _params(sem=("arbitrary",)):
    return pltpu.CompilerParams(dimension_semantics=sem, vmem_limit_bytes=VMEM_LIMIT)


def _rope_table(lp):
    half = ROT_DIM // 2
    pos = jnp.maximum(jnp.arange(lp) - LEAD_PAD, 0).astype(F32)
    inv_freq = jnp.power(jnp.float32(ROPE_THETA), -jnp.arange(0, ROT_DIM, 2, dtype=F32) / ROT_DIM)
    ang_t = jnp.concatenate([inv_freq, inv_freq])[:, None] * pos[None, :]
    row = lax.broadcasted_iota(jnp.int32, (ROT_DIM, lp), 0)
    cs_t = jnp.where(row < half, jnp.cos(ang_t), jnp.sin(ang_t))
    return jnp.pad(cs_t.T, ((0, 0), (0, 128 - ROT_DIM)))


def _rope_coeffs(t):
    half = ROT_DIM // 2
    lane = lax.broadcasted_iota(jnp.int32, t.shape, 1)
    cos_a = jnp.where(lane < half, t, 0.0)
    sin_a = pltpu.roll(jnp.where((lane >= half) & (lane < ROT_DIM), t, 0.0), 128 - half, 1)
    c = cos_a + pltpu.roll(cos_a, half, 1) + jnp.where((lane >= ROT_DIM) & (lane < HEAD_DIM), 1.0, 0.0)
    s2 = pltpu.roll(sin_a, half, 1)
    both = lambda u: u + pltpu.roll(u, HEAD_DIM, 1)
    return both(c), both(-sin_a), both(s2)


def _rope(t, c, s1, s2):
    return t * c + pltpu.roll(t, BLOCK - 8, 1) * s1 + pltpu.roll(t, 8, 1) * s2


def _rope_t(dt, c, s1, s2):
    return dt * c + pltpu.roll(dt * s1, 8, 1) + pltpu.roll(dt * s2, BLOCK - 8, 1)


def _build_h(x, rope_compact, tm, exch, small_piece, name):
    seq = x.shape[0]
    lp = BLOCK + seq
    nt = lp // tm
    n_sub = tm // BLOCK
    small_shape = exch.land_shapes[small_piece].shape

    def body(*refs):
        h_ref, c_ref, s1_ref, s2_ref = refs[n_sub + 1:n_sub + 5]
        for j in range(n_sub):
            h_ref[j * BLOCK:(j + 1) * BLOCK, :] = refs[j][...]
        c_ref[...], s1_ref[...], s2_ref[...] = _rope_coeffs(refs[n_sub][...])

    def after(lands, *refs):
        h_ref, buf = refs[n_sub + 1], refs[n_sub + 5]
        pltpu.sync_copy(lands[small_piece], buf)
        h_ref[0:LEAD_PAD, :] = jnp.zeros((LEAD_PAD, D_MODEL), F32)
        for d in range(N_DEV):
            h_ref[LEAD_PAD:BLOCK, d * 128:(d + 1) * 128] = buf[d, 0:N_META, :]

    tile = lambda i: (i + 1) % nt
    piece = lambda j: pl.BlockSpec((BLOCK, D_MODEL), lambda i: (jnp.maximum(tile(i) * n_sub + j - 1, 0), 0))
    rows = lambda w: pl.BlockSpec((tm, w), lambda i: (tile(i), 0))
    (h, *rope), lands = _call(
        body, exch,
        name=name,
        grid=(nt,),
        in_specs=[piece(j) for j in range(n_sub)] + [rows(128)],
        out_specs=[rows(D_MODEL)] + [rows(128)] * 3,
        out_shape=[jax.ShapeDtypeStruct((lp, D_MODEL), F32)] + [jax.ShapeDtypeStruct((lp, 128), F32)] * 3,
        scratch_shapes=[pltpu.VMEM(small_shape, F32)],
        compiler_params=_params(),
        after=after,
    )(*([x] * n_sub), rope_compact)
    return h, rope, lands


def _in_proj_fwd(h, g, w_in_t, rope, tm, name, exch=None):
    lp = h.shape[0]

    def body(h_ref, g_ref, w_ref, c_ref, s1_ref, s2_ref, a_ref, qkv_ref, bch_ref):
        a = _rms_fwd(h_ref[...], g_ref[...]).astype(BF)
        a_ref[...] = a
        proj = _dot_nt(a, w_ref[...])
        c, s1, s2 = c_ref[...], s1_ref[...], s2_ref[...]
        for j in range(5):
            t = _rope(proj[:, j * 128:(j + 1) * 128], c, s1, s2)
            qkv_ref[:, j * 128:(j + 1) * 128] = (t * SCALE if j < 4 else t).astype(BF)
        qkv_ref[:, 640:768] = proj[:, 640:768].astype(BF)
        bch_ref[...] = proj[:, 768:].astype(BF)

    row = lambda w: pl.BlockSpec((tm, w), lambda i: (i, 0))
    return _call(
        body, exch,
        name=name,
        grid=(lp // tm,),
        in_specs=[row(D_MODEL), _full((1, D_MODEL)), _full((IN_W, D_MODEL)), row(128), row(128), row(128)],
        out_specs=[row(D_MODEL), row(768), row(3 * CONV_W)],
        out_shape=[
            jax.ShapeDtypeStruct((lp, D_MODEL), BF),
            jax.ShapeDtypeStruct((lp, 768), BF),
            jax.ShapeDtypeStruct((lp, 3 * CONV_W), BF),
        ],
        compiler_params=_params(),
    )(h, g, w_in_t, *rope)


def _fold_masks(i):
    r = lax.broadcasted_iota(jnp.int32, (2 * BLOCK, BLOCK), 0) & (BLOCK - 1)
    c = lax.broadcasted_iota(jnp.int32, (2 * BLOCK, BLOCK), 1)
    tri = c > r
    ok = jnp.where(tri, (i - 1) * BLOCK + c, i * BLOCK + c) >= LEAD_PAD
    return tri, ok


def _kv_operand(x, kvh):
    lane = lax.broadcasted_iota(jnp.int32, x.shape, 1)
    zero = jnp.zeros_like(x)
    if kvh == 0:
        lo = jnp.where(lane < HEAD_DIM, x, zero)
        hi = pltpu.roll(lo, HEAD_DIM, 1)
    else:
        hi = jnp.where(lane >= HEAD_DIM, x, zero)
        lo = pltpu.roll(hi, HEAD_DIM, 1)
    return jnp.concatenate([lo, hi], axis=0)


def _split4(t, tri):
    zero = jnp.zeros_like(t[0])
    return jnp.concatenate(
        [jnp.where(tri, t[0], zero), jnp.where(tri, zero, t[0]), jnp.where(tri, t[1], zero), jnp.where(tri, zero, t[1])], axis=1)


def _sink_cols(sink_ref, kvh):
    first = lax.broadcasted_iota(jnp.int32, (2 * BLOCK, 1), 0) < BLOCK
    return [jnp.where(first, sink_ref[0, 4 * kvh + half], sink_ref[0, 4 * kvh + 2 + half]) for half in range(2)]


def _folded_exp(q2, k4, tri, ok, sks):
    s = _dot_nt(q2, k4)
    es, ss = [], []
    for half in range(2):
        s_h = s[:, 2 * half * BLOCK:2 * (half + 1) * BLOCK]
        sf = jnp.where(ok, jnp.where(tri, s_h[:, :BLOCK], s_h[:, BLOCK:]), NEG)
        m = jnp.maximum(jnp.max(sf, axis=-1, keepdims=True), sks[half])
        es.append(jnp.exp(sf - m))
        ss.append(jnp.exp(sks[half] - m))
    sums = _dot(jnp.concatenate(es, axis=0).astype(BF), jnp.ones((BLOCK, BLOCK), BF))
    invs = [1.0 / (sums[2 * half * BLOCK:2 * (half + 1) * BLOCK] + ss[half]) for half in range(2)]
    return es, ss, invs


def _attn_fwd(qkv, sink, name, exch=None):
    lp = qkv.shape[0]
    nb = lp // BLOCK
    per_step = 2

    def one_block(i, sink_ref, q_ref, kvc_ref, kvp_ref, o_ref, p_ref, ps_ref):
        tri, ok = _fold_masks(i)
        kvc, kvp = kvc_ref[...], kvp_ref[...]
        kk = jnp.concatenate([kvp[:, :128], kvc[:, :128]], axis=0)
        vv = jnp.concatenate([kvp[:, 128:], kvc[:, 128:]], axis=0)
        lane = lax.broadcasted_iota(jnp.int32, (BLOCK, 128), 1)
        p_sink = jnp.zeros((BLOCK, 128), F32)
        for kvh in range(2):
            q2 = jnp.concatenate([q_ref[:, 256 * kvh:256 * kvh + 128], q_ref[:, 256 * kvh + 128:256 * kvh + 256]], axis=0)
            es, ss, invs = _folded_exp(q2, _kv_operand(kk, kvh), tri, ok, _sink_cols(sink_ref, kvh))
            pb = [(es[half] * invs[half]).astype(BF) for half in range(2)]
            out = _dot(_split4(pb, tri), _kv_operand(vv, kvh))
            for pair in range(2):
                rows = slice(pair * BLOCK, (pair + 1) * BLOCK)
                o_ref[:, 256 * kvh + 128 * pair:256 * kvh + 128 * (pair + 1)] = out[rows].astype(BF)
                for half in range(2):
                    head = 4 * kvh + 2 * pair + half
                    p_ref[:, 128 * head:128 * (head + 1)] = pb[half][rows]
                    p_sink = jnp.where(lane == head, (ss[half] * invs[half][:, 0:1])[rows], p_sink)
        ps_ref[...] = p_sink

    def body(sink_ref, *refs):
        q_refs, kv_refs = refs[:per_step], refs[per_step:2 * per_step + 1]
        o_ref, p_ref, ps_ref = refs[2 * per_step + 1:]
        for j in range(per_step):
            rows = slice(j * BLOCK, (j + 1) * BLOCK)
            one_block(per_step * pl.program_id(0) + j, sink_ref, q_refs[j], kv_refs[j + 1], kv_refs[j],
                      o_ref.at[rows], p_ref.at[rows], ps_ref.at[rows])

    last = nb - 1
    blk = lambda j: (lambda s: jnp.minimum(per_step * s + j, last))
    out_rows = lambda w: pl.BlockSpec((per_step * BLOCK, w), lambda s: (s, 0))
    return _call(
        body, exch,
        name=name,
        grid=(pl.cdiv(nb, per_step),),
        in_specs=[pl.BlockSpec(memory_space=pltpu.SMEM)]
        + [pl.BlockSpec((BLOCK, ATTN_W), lambda s, j=j: (blk(j)(s), 0)) for j in range(per_step)]
        + [pl.BlockSpec((BLOCK, 256), lambda s: (jnp.maximum(per_step * s - 1, 0), 2))]
        + [pl.BlockSpec((BLOCK, 256), lambda s, j=j: (blk(j)(s), 2)) for j in range(per_step)],
        out_specs=[out_rows(ATTN_W), out_rows(N_Q_HEADS * BLOCK), out_rows(128)],
        out_shape=[jax.ShapeDtypeStruct((lp, ATTN_W), BF), jax.ShapeDtypeStruct((lp, N_Q_HEADS * BLOCK), BF),
                   jax.ShapeDtypeStruct((lp, 128), F32)],
        compiler_params=_params(),
    )(sink, *([qkv] * (2 * per_step + 1)))


def _mix_out_fwd(bch, y_attn, h, conv_w, g_a, g_c, w_out, g_post, tm, name, exch=None):
    lp = h.shape[0]

    def body(bch_ref, ya_ref, h_ref, cw_ref, ga_ref, gc_ref, w_ref, gp_ref, yc_ref, y_ref, z_ref, h2_ref, ext):
        i = pl.program_id(0)

        @pl.when(i == 0)
        def _():
            ext[0:8, :] = jnp.zeros((8, CONV_W), F32)

        b = bch_ref[:, 0:CONV_W].astype(F32)
        u = bch_ref[:, CONV_W:2 * CONV_W].astype(F32) * bch_ref[:, 2 * CONV_W:3 * CONV_W].astype(F32)
        ext[8:8 + tm, :] = u
        yc = cw_ref[0:1, :] * ext[6:6 + tm, :] + cw_ref[1:2, :] * ext[7:7 + tm, :] + cw_ref[2:3, :] * u
        ext[0:8, :] = u[tm - 8:tm, :]
        yc_ref[...] = yc.astype(BF)
        ya = _rms_fwd(ya_ref[...].astype(F32), ga_ref[...]).astype(BF)
        yb = _rms_fwd(b * yc, gc_ref[...]).astype(BF)
        y_ref[:, 0:ATTN_W] = ya
        y_ref[:, ATTN_W:] = yb
        z = _dot(ya, w_ref[0:ATTN_W, :]) + _dot(yb, w_ref[ATTN_W:, :])
        z_ref[...] = z.astype(BF)
        h2_ref[...] = h_ref[...] + _rms_fwd(z, gp_ref[...])

    row = lambda w: pl.BlockSpec((tm, w), lambda i: (i, 0))
    return _call(
        body, exch,
        name=name,
        grid=(lp // tm,),
        in_specs=[
            row(3 * CONV_W), row(ATTN_W), row(D_MODEL), _full((8, CONV_W)), _full((1, ATTN_W)), _full((1, CONV_W)),
            _full((D_MODEL, D_MODEL)), _full((1, D_MODEL)),
        ],
        out_specs=[row(CONV_W), row(D_MODEL), row(D_MODEL), row(D_MODEL)],
        out_shape=[
            jax.ShapeDtypeStruct((lp, CONV_W), BF),
            jax.ShapeDtypeStruct((lp, D_MODEL), BF),
            jax.ShapeDtypeStruct((lp, D_MODEL), BF),
            jax.ShapeDtypeStruct((lp, D_MODEL), F32),
        ],
        scratch_shapes=[pltpu.VMEM((tm + 8, CONV_W), F32)],
        compiler_params=_params(),
    )(bch, y_attn, h, conv_w, g_a, g_c, w_out, g_post)


def _mlp_fwd(h2, g_pre, w_up_t, w_down, g_post, tm, name, exch=None, target=None):
    lp = h2.shape[0]
    sub = math.gcd(tm, BLOCK)
    n_sub, lead = tm // sub, BLOCK // sub
    n_t = n_sub if target is not None else 0

    def body(*refs):
        h_ref, gp_ref, wu_ref, wd_ref, gq_ref = refs[:5]
        t_refs = refs[5:5 + n_t]
        a_ref, up_ref, f_ref, last_ref = refs[5 + n_t:9 + n_t]
        h = h_ref[...]
        a = _rms_fwd(h, gp_ref[...]).astype(BF)
        a_ref[...] = a
        up = _dot_nt(a, wu_ref[...])
        up_ref[...] = up.astype(BF)
        act = jnp.square(jnp.maximum(up, 0.0)).astype(BF)
        f = _dot(act, wd_ref[...])
        f_ref[...] = f
        h3 = h + _rms_fwd(f, gq_ref[...])
        if target is None:
            last_ref[...] = h3
            return
        ls_ref = refs[9 + n_t]
        i = pl.program_id(0)

        @pl.when(i == 0)
        def _():
            ls_ref[...] = jnp.zeros((8, 128), F32)

        sq = jnp.zeros((8, D_MODEL), F32)
        for j in range(n_sub):
            on_tokens = i * n_sub + j >= lead
            d = jnp.where(on_tokens, h3[j * sub:(j + 1) * sub] - t_refs[j][...], 0.0)
            last_ref[j * sub:(j + 1) * sub, :] = d * (1.0 / D_MODEL)
            sq = sq + jnp.sum((d * d).reshape(sub // 8, 8, D_MODEL), axis=0)
        ls_ref[...] += sum(sq[:, k * 128:(k + 1) * 128] for k in range(D_MODEL // 128))

        @pl.when(i == lp // tm - 1)
        def _():
            ls_ref[...] = jnp.full((8, 128), jnp.sum(ls_ref[...]), F32)

    row = lambda w: pl.BlockSpec((tm, w), lambda i: (i, 0))
    piece = lambda j: pl.BlockSpec((sub, D_MODEL), lambda i: (jnp.maximum(i * n_sub + j - lead, 0), 0))
    out_specs = [row(D_MODEL), row(D_FF), row(D_MODEL), row(D_MODEL)]
    out_shape = [
        jax.ShapeDtypeStruct((lp, D_MODEL), BF),
        jax.ShapeDtypeStruct((lp, D_FF), BF),
        jax.ShapeDtypeStruct((lp, D_MODEL), F32),
        jax.ShapeDtypeStruct((lp, D_MODEL), F32),
    ]
    if target is not None:
        out_specs.append(_full_out((8, 128)))
        out_shape.append(jax.ShapeDtypeStruct((8, 128), F32))
    return _call(
        body, exch,
        name=name,
        grid=(lp // tm,),
        in_specs=[row(D_MODEL), _full((1, D_MODEL)), _full((D_FF, D_MODEL)), _full((D_FF, D_MODEL)), _full((1, D_MODEL))]
        + [piece(j) for j in range(n_t)],
        out_specs=out_specs,
        out_shape=out_shape,
        compiler_params=_params(),
    )(h2, g_pre, w_up_t, w_down, g_post, *([target] * n_t))


def _mlp_bwd_dx(dh3, f, up, h2, w_down, w_up_t, g_post, g_pre, tm, name, exch=None):
    lp = h2.shape[0]

    def body(dh3_ref, f_ref, up_ref, h2_ref, wd_ref, wu_ref, gq_ref, gp_ref, df_ref, dup_ref, dh2_ref, dg_ref):
        i = pl.program_id(0)

        @pl.when(i == 0)
        def _():
            dg_ref[...] = jnp.zeros((8, D_MODEL), F32)

        dh3 = dh3_ref[...]
        df, dgq = _rms_bwd(f_ref[...], gq_ref[...], dh3)
        dg_ref[ROW_MLP_POST:ROW_MLP_POST + 1, :] += dgq
        df = df.astype(BF)
        df_ref[...] = df
        dact = _dot_nt(df, wd_ref[...])
        dup = (dact * (2.0 * jnp.maximum(up_ref[...].astype(F32), 0.0))).astype(BF)
        dup_ref[...] = dup
        da = _dot(dup, wu_ref[...])
        dh, dgp = _rms_bwd(h2_ref[...], gp_ref[...], da)
        dg_ref[ROW_MLP_PRE:ROW_MLP_PRE + 1, :] += dgp
        dh2_ref[...] = dh3 + dh

    row = lambda w: pl.BlockSpec((tm, w), lambda i: (i, 0))
    return _call(
        body, exch,
        name=name,
        grid=(lp // tm,),
        in_specs=[
            row(D_MODEL), row(D_MODEL), row(D_FF), row(D_MODEL), _full((D_FF, D_MODEL)), _full((D_FF, D_MODEL)),
            _full((1, D_MODEL)), _full((1, D_MODEL)),
        ],
        out_specs=[row(D_MODEL), row(D_FF), row(D_MODEL), _full_out((8, D_MODEL))],
        out_shape=[
            jax.ShapeDtypeStruct((lp, D_MODEL), BF),
            jax.ShapeDtypeStruct((lp, D_FF), BF),
            jax.ShapeDtypeStruct((lp, D_MODEL), F32),
            jax.ShapeDtypeStruct((8, D_MODEL), F32),
        ],
        compiler_params=_params(),
    )(dh3, f, up, h2, w_down, w_up_t, g_post, g_pre)


def _mlp_bwd_dw(up, df, dup, a2, tm, name):
    lp = up.shape[0]
    nt = lp // tm
    nj = D_FF // D_MODEL

    def body(up_ref, df_ref, dup_ref, a_ref, dwd_ref, dwu_ref, accd, accu):
        i = pl.program_id(1)

        @pl.when(i == 0)
        def _():
            accd[...] = jnp.zeros_like(accd)
            accu[...] = jnp.zeros_like(accu)

        act = jnp.square(jnp.maximum(up_ref[...].astype(F32), 0.0)).astype(BF)
        accd[...] += _dot_tn(act, df_ref[...])
        accu[...] += _dot_tn(dup_ref[...], a_ref[...])

        @pl.when(i == nt - 1)
        def _():
            dwd_ref[...] = accd[...].astype(BF)
            dwu_ref[...] = accu[...].astype(BF)

    return pl.pallas_call(
        body,
        name=name,
        grid=(nj, nt),
        in_specs=[
            pl.BlockSpec((tm, D_MODEL), lambda j, i: (i, j)),
            pl.BlockSpec((tm, D_MODEL), lambda j, i: (i, 0)),
            pl.BlockSpec((tm, D_MODEL), lambda j, i: (i, j)),
            pl.BlockSpec((tm, D_MODEL), lambda j, i: (i, 0)),
        ],
        out_specs=[pl.BlockSpec((D_MODEL, D_MODEL), lambda j, i: (j, 0)), pl.BlockSpec((D_MODEL, D_MODEL), lambda j, i: (j, 0))],
        out_shape=[jax.ShapeDtypeStruct((D_FF, D_MODEL), BF), jax.ShapeDtypeStruct((D_FF, D_MODEL), BF)],
        scratch_shapes=[pltpu.VMEM((D_MODEL, D_MODEL), F32), pltpu.VMEM((D_MODEL, D_MODEL), F32)],
        compiler_params=_params(("arbitrary", "arbitrary")),
    )(up, df, dup, a2)


def _mix_out_bwd(dh2, z, y_attn, yc, bch, y, w_out, g_post, g_a, g_c, conv_w, tm, name, exch=None):
    lp = dh2.shape[0]
    nt = lp // tm

    def body(dh2_ref, z_ref, ya_ref, yc_ref, bch_ref, y_ref, w_ref, gp_ref, ga_ref, gc_ref, cw_ref,
             dya_ref, dbch_ref, dg_ref, dwo_ref, ext, acco):
        i = pl.program_id(0)
        dcw_ref = dg_ref.at[ROW_CONV:ROW_CONV + 3, 0:CONV_W]

        @pl.when(i == 0)
        def _():
            ext[tm:tm + 8, :] = jnp.zeros((8, CONV_W), F32)
            dg_ref[...] = jnp.zeros((8, D_MODEL), F32)
            acco[...] = jnp.zeros_like(acco)

        dz, dgp = _rms_bwd(z_ref[...].astype(F32), gp_ref[...], dh2_ref[...])
        dg_ref[ROW_MIX_POST:ROW_MIX_POST + 1, :] += dgp
        dz = dz.astype(BF)
        acco[...] += _dot_tn(y_ref[...], dz)
        dya_n = _dot_nt(dz, w_ref[0:ATTN_W, :])
        dyb_n = _dot_nt(dz, w_ref[ATTN_W:, :])
        dya, dga = _rms_bwd(ya_ref[...].astype(F32), ga_ref[...], dya_n)
        dg_ref[ROW_GROUP_G:ROW_GROUP_G + 1, 0:ATTN_W] += dga
        dya_ref[...] = dya
        b = bch_ref[:, 0:CONV_W].astype(F32)
        c = bch_ref[:, CONV_W:2 * CONV_W].astype(F32)
        hc = bch_ref[:, 2 * CONV_W:3 * CONV_W].astype(F32)
        u = c * hc
        yc_v = yc_ref[...].astype(F32)
        dyconv, dgc = _rms_bwd(b * yc_v, gc_ref[...], dyb_n)
        dg_ref[ROW_GROUP_G:ROW_GROUP_G + 1, ATTN_W:] += dgc
        dbch_ref[:, 0:CONV_W] = (dyconv * yc_v).astype(BF)
        dyc = dyconv * b
        ext[0:tm, :] = dyc
        d1 = ext[1:1 + tm, :]
        d2 = ext[2:2 + tm, :]
        du = cw_ref[2:3, :] * dyc + cw_ref[1:2, :] * d1 + cw_ref[0:1, :] * d2
        ext[tm:tm + 8, :] = dyc[0:8, :]
        dbch_ref[:, CONV_W:2 * CONV_W] = (du * hc).astype(BF)
        dbch_ref[:, 2 * CONV_W:3 * CONV_W] = (du * c).astype(BF)
        dcw_ref[0:1, :] += jnp.sum(u * d2, axis=0, keepdims=True)
        dcw_ref[1:2, :] += jnp.sum(u * d1, axis=0, keepdims=True)
        dcw_ref[2:3, :] += jnp.sum(u * dyc, axis=0, keepdims=True)

        @pl.when(i == nt - 1)
        def _():
            dwo_ref[...] = acco[...].astype(BF)

    row = lambda w: pl.BlockSpec((tm, w), lambda i: (nt - 1 - i, 0))
    return _call(
        body, exch,
        name=name,
        grid=(nt,),
        in_specs=[
            row(D_MODEL), row(D_MODEL), row(ATTN_W), row(CONV_W), row(3 * CONV_W), row(D_MODEL), _full((D_MODEL, D_MODEL)),
            _full((1, D_MODEL)), _full((1, ATTN_W)), _full((1, CONV_W)), _full((8, CONV_W)),
        ],
        out_specs=[row(ATTN_W), row(3 * CONV_W), _full_out((8, D_MODEL)), _full_out((D_MODEL, D_MODEL))],
        out_shape=[
            jax.ShapeDtypeStruct((lp, ATTN_W), F32),
            jax.ShapeDtypeStruct((lp, 3 * CONV_W), BF),
            jax.ShapeDtypeStruct((8, D_MODEL), F32),
            jax.ShapeDtypeStruct((D_MODEL, D_MODEL), BF),
        ],
        scratch_shapes=[pltpu.VMEM((tm + 8, CONV_W), F32), pltpu.VMEM((D_MODEL, D_MODEL), F32)],
        compiler_params=_params(),
    )(dh2, z, y_attn, yc, bch, y, w_out, g_post, g_a, g_c, conv_w)


def _attn_bwd(qkv, o, do, probs, p_sink, rope, name, exch=None):
    lp = qkv.shape[0]
    nb = lp // BLOCK

    def body(q_ref, kvc_ref, kvp_ref, o_ref, do_ref, p_ref, ps_ref, cq_ref, s1q_ref, s2q_ref, ck_ref, s1k_ref, s2k_ref,
             dq_ref, dkv_ref, dsink_ref, carry):
        i = pl.program_id(0)

        @pl.when(i == 0)
        def _():
            carry[...] = jnp.zeros_like(carry)
            dsink_ref[...] = jnp.zeros((8, 128), F32)

        def finish(tot):
            dk = _rope_t(tot[:, :128], ck_ref[...], s1k_ref[...], s2k_ref[...])
            dkv_ref[:, 0:128] = dk.astype(BF)
            dkv_ref[:, 128:256] = tot[:, 128:].astype(BF)

        @pl.when(i < nb)
        def _():
            tri, _ = _fold_masks(i)
            kvc, kvp = kvc_ref[...], kvp_ref[...]
            kk = jnp.concatenate([kvp[:, :128], kvc[:, :128]], axis=0)
            vv = jnp.concatenate([kvp[:, 128:], kvc[:, 128:]], axis=0)
            lane = lax.broadcasted_iota(jnp.int32, (BLOCK, 128), 1)
            lane2 = lax.broadcasted_iota(jnp.int32, (2 * BLOCK, 128), 1)
            rope_q = (cq_ref[...], s1q_ref[...], s2q_ref[...])
            deltas = jnp.zeros((BLOCK, 128), F32)
            folded = []
            for kvh in range(2):
                c0 = 256 * kvh
                q2 = jnp.concatenate([q_ref[:, c0:c0 + 128], q_ref[:, c0 + 128:c0 + 256]], axis=0)
                do2 = jnp.concatenate([do_ref[:, c0:c0 + 128], do_ref[:, c0 + 128:c0 + 256]], axis=0)
                o2 = jnp.concatenate([o_ref[:, c0:c0 + 128], o_ref[:, c0 + 128:c0 + 256]], axis=0).astype(F32)
                k4, v4 = _kv_operand(kk, kvh), _kv_operand(vv, kvh)
                prod = do2 * o2
                dob = do2.astype(BF)
                dp = _dot_nt(dob, v4)
                ds, pb = [], []
                for half in range(2):
                    heads = [4 * kvh + 2 * pair + half for pair in range(2)]
                    p = jnp.concatenate([p_ref[:, 128 * h:128 * (h + 1)] for h in heads], axis=0)
                    sel = (lane2 < HEAD_DIM) if half == 0 else (lane2 >= HEAD_DIM)
                    delta = jnp.sum(jnp.where(sel, prod, 0.0), axis=-1, keepdims=True)
                    dp_h = dp[:, 2 * half * BLOCK:2 * (half + 1) * BLOCK]
                    ds.append((p.astype(F32) * (jnp.where(tri, dp_h[:, :BLOCK], dp_h[:, BLOCK:]) - delta)).astype(BF))
                    pb.append(p)
                    for pair in range(2):
                        deltas = jnp.where(lane == heads[pair], delta[pair * BLOCK:(pair + 1) * BLOCK], deltas)
                ds4, p4 = _split4(ds, tri), _split4(pb, tri)
                dq2 = _dot(ds4, k4) * SCALE
                dq_ref[:, c0:c0 + 128] = _rope_t(dq2[:BLOCK], *rope_q).astype(BF)
                dq_ref[:, c0 + 128:c0 + 256] = _rope_t(dq2[BLOCK:], *rope_q).astype(BF)
                rk, rv = _dot_tn(ds4, q2), _dot_tn(p4, dob)
                own = (lane < HEAD_DIM) if kvh == 0 else (lane >= HEAD_DIM)
                group = []
                for r in (rk, rv):
                    for blk in range(2):
                        t = jnp.where(lane < HEAD_DIM, r[blk * BLOCK:(blk + 1) * BLOCK], r[(2 + blk) * BLOCK:(3 + blk) * BLOCK])
                        group.append(jnp.where(own, t + pltpu.roll(t, HEAD_DIM, 1), 0.0))
                folded.append(group)
            dsink_ref[ROW_SINK:ROW_SINK + 1, :] -= jnp.sum(ps_ref[...] * deltas, axis=0, keepdims=True)
            dk_p, dk_c, dv_p, dv_c = [folded[0][t] + folded[1][t] for t in range(4)]
            finish(carry[...] + jnp.concatenate([dk_p, dv_p], axis=1))
            carry[...] = jnp.concatenate([dk_c, dv_c], axis=1)

        @pl.when(i == nb)
        def _():
            finish(carry[...])

    qi = lambda i: jnp.minimum(i, nb - 1)
    ki = lambda i: jnp.maximum(i - 1, 0)
    tab_q = pl.BlockSpec((BLOCK, 128), lambda i: (qi(i), 0))
    tab_k = pl.BlockSpec((BLOCK, 128), lambda i: (ki(i), 0))
    return _call(
        body, exch,
        name=name,
        grid=(nb + 1,),
        in_specs=[
            pl.BlockSpec((BLOCK, ATTN_W), lambda i: (qi(i), 0)),
            pl.BlockSpec((BLOCK, 256), lambda i: (qi(i), 2)),
            pl.BlockSpec((BLOCK, 256), lambda i: (jnp.maximum(qi(i) - 1, 0), 2)),
            pl.BlockSpec((BLOCK, ATTN_W), lambda i: (qi(i), 0)),
            pl.BlockSpec((BLOCK, ATTN_W), lambda i: (qi(i), 0)),
            pl.BlockSpec((BLOCK, N_Q_HEADS * BLOCK), lambda i: (qi(i), 0)),
            tab_q, tab_q, tab_q, tab_q, tab_k, tab_k, tab_k,
        ],
        out_specs=[
            pl.BlockSpec((BLOCK, ATTN_W), lambda i: (qi(i), 0)),
            pl.BlockSpec((BLOCK, 256), lambda i: (ki(i), 0)),
            pl.BlockSpec((8, 128), lambda i: (0, 0)),
        ],
        out_shape=[
            jax.ShapeDtypeStruct((lp, ATTN_W), BF),
            jax.ShapeDtypeStruct((lp, 256), BF),
            jax.ShapeDtypeStruct((8, 128), F32),
        ],
        scratch_shapes=[pltpu.VMEM((BLOCK, 256), F32)],
        compiler_params=_params(),
    )(qkv, qkv, qkv, o, do, probs, p_sink, *rope, *rope)


def _in_proj_bwd_dx(dq, dkv, dbch, w_in_t, h, dh2, g, tm, name, exch=None):
    lp = h.shape[0]

    def body(dq_ref, dkv_ref, dbch_ref, w_ref, h_ref, dh2_ref, g_ref, dh_ref, dg_ref):
        i = pl.program_id(0)

        @pl.when(i == 0)
        def _():
            dg_ref[...] = jnp.zeros((8, D_MODEL), F32)

        da = _dot(jnp.concatenate([dq_ref[...], dkv_ref[...], dbch_ref[...]], axis=1), w_ref[...])
        dh, dg = _rms_bwd(h_ref[...], g_ref[...], da)
        dg_ref[ROW_MIX_PRE:ROW_MIX_PRE + 1, :] += dg
        dh_ref[...] = dh2_ref[...] + dh

    row = lambda w: pl.BlockSpec((tm, w), lambda i: (i, 0))
    return _call(
        body, exch,
        name=name,
        grid=(lp // tm,),
        in_specs=[row(ATTN_W), row(256), row(3 * CONV_W), _full((IN_W, D_MODEL)), row(D_MODEL), row(D_MODEL), _full((1, D_MODEL))],
        out_specs=[row(D_MODEL), _full_out((8, D_MODEL))],
        out_shape=[jax.ShapeDtypeStruct((lp, D_MODEL), F32), jax.ShapeDtypeStruct((8, D_MODEL), F32)],
        compiler_params=_params(),
    )(dq, dkv, dbch, w_in_t, h, dh2, g)


def _mix_bwd_dw(dq, dkv, dbch, a, tm, name, exch=None):
    lp = a.shape[0]
    nt = lp // tm

    def body(dq_ref, dkv_ref, dbch_ref, a_ref, dwi_ref, acci):
        i = pl.program_id(0)

        @pl.when(i == 0)
        def _():
            acci[...] = jnp.zeros_like(acci)

        a_v = a_ref[...]
        acci[0:512, :] += _dot_tn(dq_ref[...], a_v)
        acci[512:768, :] += _dot_tn(dkv_ref[...], a_v)
        acci[768:, :] += _dot_tn(dbch_ref[...], a_v)

        @pl.when(i == nt - 1)
        def _():
            dwi_ref[...] = acci[...].astype(BF)

    row = lambda w: pl.BlockSpec((tm, w), lambda i: (i, 0))
    return _call(
        body, exch,
        name=name,
        grid=(nt,),
        in_specs=[row(ATTN_W), row(256), row(3 * CONV_W), row(D_MODEL)],
        out_specs=[_full_out((IN_W, D_MODEL))],
        out_shape=[jax.ShapeDtypeStruct((IN_W, D_MODEL), BF)],
        scratch_shapes=[pltpu.VMEM((IN_W, D_MODEL), F32)],
        compiler_params=_params(),
    )(dq, dkv, dbch, a)


def _mesh_place():
    x, y, c = lax.axis_index("x"), lax.axis_index("y"), lax.axis_index("c")
    return x, y, c, 4 * x + 2 * y + c


def _peer(x, y, c, k):
    px = 1 - x if k & 4 else x
    py = 1 - y if k & 2 else y
    pc = 1 - c if k & 1 else c
    return (px, py, pc), 4 * px + 2 * py + pc


SIBLING = 1
SAME_CORE = (2, 4, 6)
OTHER_CORE = (3, 5, 7)


class _Exchange:
    def __init__(self, pieces):
        self.srcs = [s for s, _ in pieces]
        self.to_all = [g for _, g in pieces]
        self.n = len(pieces)
        self.land_shapes = [
            jax.ShapeDtypeStruct((N_DEV,) + (s.shape if g else s.shape[1:]), s.dtype) for s, g in pieces]
        self.sem_shapes = [pltpu.SemaphoreType.DMA((self.n, N_DEV - 1)), pltpu.SemaphoreType.DMA((self.n, N_DEV - 1)),
                           pltpu.SemaphoreType.DMA((self.n,))]
        self.forwards = any(self.to_all)

    def _ops(self, srcs, lands, sems):
        send_sems, recv_sems, local_sems = sems
        x, y, c, me = _mesh_place()

        def remote(p, k, src, slot, to):
            return pltpu.make_async_remote_copy(
                src_ref=src, dst_ref=lands[p].at[slot], send_sem=send_sems.at[p, k - 1], recv_sem=recv_sems.at[p, k - 1],
                device_id=to, device_id_type=MESH)

        def own(p):
            return pltpu.make_async_copy(srcs[p] if self.to_all[p] else srcs[p].at[me], lands[p].at[me], local_sems.at[p])

        def direct(p, k):
            peer, pidx = _peer(x, y, c, k)
            return remote(p, k, srcs[p] if self.to_all[p] else srcs[p].at[pidx], me, peer)

        def forward(p, k):
            sibling, _ = _peer(x, y, c, SIBLING)
            _, origin = _peer(x, y, c, k ^ SIBLING)
            return remote(p, k, lands[p].at[origin], origin, sibling)

        def arrival(p, k):
            peer, pidx = _peer(x, y, c, k)
            return remote(p, k, lands[p].at[pidx], pidx, peer)

        return own, direct, forward, arrival

    def start(self, srcs, lands, sems):
        own, direct, _, _ = self._ops(srcs, lands, sems)
        for p in range(self.n):
            own(p).start()
            for k in ((SIBLING,) + SAME_CORE) if self.to_all[p] else range(1, N_DEV):
                direct(p, k).start()

    def forward(self, srcs, lands, sems):
        _, _, forward, arrival = self._ops(srcs, lands, sems)
        for p in range(self.n):
            if self.to_all[p]:
                for k in SAME_CORE:
                    arrival(p, k).wait_recv()
                    forward(p, k ^ SIBLING).start()

    def finish(self, srcs, lands, sems):
        own, direct, forward, arrival = self._ops(srcs, lands, sems)
        for p in range(self.n):
            for k in ((SIBLING,) + OTHER_CORE) if self.to_all[p] else range(1, N_DEV):
                arrival(p, k).wait_recv()
        for p in range(self.n):
            for k in range(1, N_DEV):
                (forward(p, k) if self.to_all[p] and k in OTHER_CORE else direct(p, k)).wait_send()
            own(p).wait()


class _LayerRows:
    def __init__(self, array, layer):
        self.array = array if array.ndim == 3 else array.reshape(DEPTH, 1, -1)
        self.layer = layer

    def spec(self):
        layer = self.layer
        return pl.BlockSpec((None,) + self.array.shape[1:], lambda *_: (layer, 0, 0), pipeline_mode=pl.Buffered(1))


def _call(body, exch, *, name, grid, in_specs, out_specs, out_shape, scratch_shapes=(), compiler_params, after=None):
    def with_layer_rows(args):
        specs = [a.spec() if isinstance(a, _LayerRows) else s for s, a in zip(in_specs, args)]
        return specs, [a.array if isinstance(a, _LayerRows) else a for a in args]

    if exch is None:
        def plain(*args):
            specs, args = with_layer_rows(args)
            return pl.pallas_call(body, name=name, grid=grid, in_specs=specs, out_specs=out_specs, out_shape=out_shape,
                                  scratch_shapes=scratch_shapes, compiler_params=compiler_params)(*args)
        return plain
    n_in, n_out, n_scr, n_x = len(in_specs), len(out_shape), len(scratch_shapes), exch.n
    steps = math.prod(grid)

    def carrying(*refs):
        a, b, c, d, e = n_in, n_in + n_x, n_in + n_x + n_out, n_in + 2 * n_x + n_out, n_in + 2 * n_x + n_out + n_scr
        ins, srcs, outs, lands, scr, sems = refs[:a], refs[a:b], refs[b:c], refs[c:d], refs[d:e], refs[e:]
        step = functools.reduce(lambda acc, t: acc * grid[t] + pl.program_id(t), range(len(grid)), 0)

        @pl.when(step == 0)
        def _():
            exch.start(srcs, lands, sems)

        body(*ins, *outs, *scr)

        if exch.forwards:
            @pl.when(step == max(0, steps - 1 - (steps + 7) // 8))
            def _():
                exch.forward(srcs, lands, sems)

        @pl.when(step == steps - 1)
        def _():
            exch.finish(srcs, lands, sems)
            if after is not None:
                after(lands, *ins, *outs, *scr)

    hbm = pl.BlockSpec(memory_space=pl.ANY)

    def run(*args):
        specs, args = with_layer_rows(args)
        res = pl.pallas_call(
            carrying, name=name, grid=grid, in_specs=specs + [hbm] * n_x, out_specs=list(out_specs) + [hbm] * n_x,
            out_shape=list(out_shape) + exch.land_shapes, scratch_shapes=list(scratch_shapes) + exch.sem_shapes,
            compiler_params=compiler_params)(*args, *exch.srcs)
        return list(res[:n_out]), list(res[n_out:])

    return run


def _sum_small(part):
    exch = _Exchange([(part, True)])

    def body(part_ref, out_ref, land, *sems):
        exch.start([part_ref], [land], sems)
        exch.forward([part_ref], [land], sems)
        exch.finish([part_ref], [land], sems)
        acc = land[0]
        for d in range(1, N_DEV):
            acc = acc + land[d]
        out_ref[...] = acc

    vmem = pl.BlockSpec(memory_space=pltpu.VMEM)
    return pl.pallas_call(
        body,
        name="sum_small",
        in_specs=[vmem],
        out_specs=vmem,
        out_shape=jax.ShapeDtypeStruct(part.shape, F32),
        scratch_shapes=[pltpu.VMEM(exch.land_shapes[0].shape, F32)] + exch.sem_shapes,
    )(part)


def _adamw(w, g, m, v):
    m = ADAM_B1 * m + (1.0 - ADAM_B1) * g
    v = ADAM_B2 * v + (1.0 - ADAM_B2) * jnp.square(g)
    m_hat = m / (1.0 - ADAM_B1 ** ADAM_STEP)
    v_hat = v / (1.0 - ADAM_B2 ** ADAM_STEP)
    delta = -ADAM_LR * (m_hat / (jnp.sqrt(v_hat) + ADAM_EPS) + ADAM_WD * w)
    return delta, m, v


def _landed_specs(tr, wd):
    return [pl.BlockSpec((N_DEV, tr, wd), lambda l, i, ll=ll: (0, jnp.where(l == ll, i, 0), 0)) for ll in range(DEPTH)]


def _device_sum(r_ref):
    acc = r_ref[0].astype(F32)
    for d in range(1, N_DEV):
        acc = acc + r_ref[d].astype(F32)
    return acc


def _sum_adamw(recv, w, m, v, tr, name, transposed=False):
    _, r, wd = recv[0].shape

    def body(*refs):
        w_ref, m_ref, v_ref, g_ref, d_ref, mo_ref, vo_ref = refs[DEPTH:]
        for ll in range(DEPTH):
            @pl.when(pl.program_id(0) == ll)
            def _(ll=ll):
                g = _device_sum(refs[ll])
                g = g.T if transposed else g
                g_ref[0] = g
                d_ref[0], mo_ref[0], vo_ref[0] = _adamw(w_ref[0], g, m_ref[0], v_ref[0])

    if transposed:
        blk = pl.BlockSpec((1, wd, tr), lambda l, i: (l, 0, i))
        shape = jax.ShapeDtypeStruct((DEPTH, wd, r), F32)
    else:
        blk = pl.BlockSpec((1, tr, wd), lambda l, i: (l, i, 0))
        shape = jax.ShapeDtypeStruct((DEPTH, r, wd), F32)
    return pl.pallas_call(
        body,
        name=name,
        grid=(DEPTH, r // tr),
        in_specs=_landed_specs(tr, wd) + [blk, blk, blk],
        out_specs=[blk] * 4,
        out_shape=[shape] * 4,
        compiler_params=_params(("arbitrary", "arbitrary")),
    )(*recv, w, m, v)


def _adamw_small(ws, gs, ms, vs):
    n = len(ws)

    def body(*refs):
        w_r, g_r, m_r, v_r = refs[:n], refs[n:2 * n], refs[2 * n:3 * n], refs[3 * n:4 * n]
        d_o, m_o, v_o = refs[4 * n:5 * n], refs[5 * n:6 * n], refs[6 * n:7 * n]
        for t in range(n):
            d_o[t][...], m_o[t][...], v_o[t][...] = _adamw(w_r[t][...], g_r[t][...], m_r[t][...], v_r[t][...])

    vmem = pl.BlockSpec(memory_space=pltpu.VMEM)
    shapes = [jax.ShapeDtypeStruct(w.shape, F32) for w in ws]
    outs = pl.pallas_call(
        body,
        name="adamw_small",
        in_specs=[vmem] * (4 * n),
        out_specs=[vmem] * (3 * n),
        out_shape=shapes * 3,
    )(*ws, *gs, *ms, *vs)
    return outs[:n], outs[n:2 * n], outs[2 * n:]


def kernel(x, meta_tokens, mix_pre_g, w_in, conv_w, sinks, attn_out_g, conv_out_g, w_out, mix_post_g, mlp_pre_g, w_up, w_down, mlp_post_g, loss_target, m_meta_tokens, m_mix_pre_g, m_w_in, m_conv_w, m_sinks, m_attn_out_g, m_conv_out_g, m_w_out, m_mix_post_g, m_mlp_pre_g, m_w_up, m_w_down, m_mlp_post_g, v_meta_tokens, v_mix_pre_g, v_w_in, v_conv_w, v_sinks, v_attn_out_g, v_conv_out_g, v_w_out, v_mix_post_g, v_mlp_pre_g, v_w_up, v_w_down, v_mlp_post_g):
    seq = x.shape[1]
    lp = BLOCK + seq
    tm = _row_tile(lp)
    tm_mlp = _row_tile(lp, (320, 256, 128))
    tm_dw_mlp = _row_tile(lp, (1664, 1040, 640, 384, 256, 128))
    tm_dw_mix = _row_tile(lp, (1664, 832, 640, 384, 256, 128))
    tm_in = _row_tile(lp, (832, 640, 384, 256, 128))
    me = 4 * lax.axis_index("x") + 2 * lax.axis_index("y") + lax.axis_index("c")
    cshard = CONV_W // N_DEV
    mshard = D_MODEL // N_DEV

    gather_with = {
        ("in_proj_fwd", 0): [("down", 0)], ("attn_fwd", 0): [("out", 0), ("up", 0)],
        ("mlp_fwd", 0): [("in", 1), ("out", 1), ("up", 1), ("down", 1)],
    }
    scatter_with = {
        ("attn_bwd", 1): [("down", 1)], ("mix_bwd_dw", 1): [("out", 1)], ("mlp_bwd_dx", 0): [("up", 1), ("in", 1)],
        ("mix_out_bwd", 0): [("up", 0)], ("attn_bwd", 0): [("down", 0)], ("mix_bwd_dw", 0): [("out", 0)],
        ("in_proj_bwd_dx", 0): [("in", 0)],
    }
    shard = {"in": jnp.swapaxes(w_in, 1, 2).astype(BF), "out": w_out.astype(BF),
             "up": jnp.swapaxes(w_up, 1, 2).astype(BF), "down": w_down.astype(BF)}
    weight = {}
    grad = {}
    landed = {}

    def run(fn, kind, l, *args):
        key, name = (kind, l), f"{kind}_{l}"
        if key in gather_with:
            blocks = gather_with[key]
            outs, lands = fn(*args, name, _Exchange([(shard[n][k], True) for n, k in blocks]))
            for b, land in zip(blocks, lands):
                weight[b] = land.reshape(-1, D_MODEL)
            return outs
        if key in scatter_with:
            blocks = scatter_with[key]
            outs, lands = fn(*args, name, _Exchange([(grad[b].reshape(N_DEV, -1, D_MODEL), False) for b in blocks]))
            landed.update(zip(blocks, lands))
            return outs
        return fn(*args, name)

    small = jnp.zeros((24, 128), F32)
    small = small.at[0:N_META, :].set(meta_tokens)
    small = small.at[N_META:N_META + 6, 0:cshard].set(conv_w.reshape(6, cshard))
    first = _Exchange([(shard["in"][0], True), (small, True)])
    h, rope, (first_in, g_small) = _build_h(x[0], _rope_table(lp), tm, first, 1, "build_h")
    weight[("in", 0)] = first_in.reshape(-1, D_MODEL)
    cw = g_small[:, N_META:N_META + 6, 0:cshard].reshape(N_DEV, DEPTH, 3, cshard)
    cw = jnp.transpose(cw, (1, 2, 0, 3)).reshape(DEPTH, 3, CONV_W)
    conv_full = jnp.concatenate([cw, jnp.zeros((DEPTH, 5, CONV_W), F32)], axis=1)

    row1 = _LayerRows

    saved = []
    for l in range(DEPTH):
        a, qkv, bch = run(_in_proj_fwd, "in_proj_fwd", l, h, row1(mix_pre_g, l), weight[("in", l)], rope, tm_in)
        y_attn, probs, p_sink = run(_attn_fwd, "attn_fwd", l, qkv, sinks[l].reshape(1, -1))
        yc, y, z, h2 = run(_mix_out_fwd, "mix_out_fwd", l, bch, y_attn, h, row1(conv_full, l), row1(attn_out_g, l),
                       row1(conv_out_g, l), weight[("out", l)], row1(mix_post_g, l), tm)
        mlp = _mlp_fwd if l < DEPTH - 1 else functools.partial(_mlp_fwd, target=loss_target[0])
        a2, up, f, *rest = run(mlp, "mlp_fwd", l, h2, row1(mlp_pre_g, l), weight[("up", l)], weight[("down", l)],
                               row1(mlp_post_g, l), tm_mlp)
        saved.append((h, a, qkv, bch, y_attn, probs, p_sink, yc, y, z, h2, a2, up, f))
        h = rest[0]
    dh, loss_part = rest[0], rest[1][0, 0] * (0.5 / D_MODEL)

    gsmall = [None] * DEPTH
    for l in reversed(range(DEPTH)):
        h0, a, qkv, bch, y_attn, probs, p_sink, yc, y, z, h2, a2, up, f = saved[l]
        df, dup, dh2, dg_mlp = run(_mlp_bwd_dx, "mlp_bwd_dx", l, dh, f, up, h2, weight[("down", l)], weight[("up", l)],
                                   row1(mlp_post_g, l), row1(mlp_pre_g, l), tm_mlp)
        grad[("down", l)], grad[("up", l)] = _mlp_bwd_dw(up, df, dup, a2, tm_dw_mlp, f"mlp_bwd_dw_{l}")
        dya, dbch, dg_mix, grad[("out", l)] = run(
            _mix_out_bwd, "mix_out_bwd", l, dh2, z, y_attn, yc, bch, y, weight[("out", l)], row1(mix_post_g, l),
            row1(attn_out_g, l), row1(conv_out_g, l), row1(conv_full, l), tm)
        dq, dkv, dsink = run(_attn_bwd, "attn_bwd", l, qkv, y_attn, dya, probs, p_sink, rope)
        grad[("in", l)], = run(_mix_bwd_dw, "mix_bwd_dw", l, dq, dkv, dbch, a, tm_dw_mix)
        dh, dg_in = run(_in_proj_bwd_dx, "in_proj_bwd_dx", l, dq, dkv, dbch, weight[("in", l)], h0, dh2,
                        row1(mix_pre_g, l), tm_in)
        tile_a =dg_mlp + dg_in + jnp.pad(dsink, ((0, 0), (0, D_MODEL - 128)))
        gsmall[l] = (tile_a, dg_mix)
    grad_x = dh[BLOCK:][None]

    loss_tile = jnp.zeros((8, D_MODEL), F32).at[ROW_LOSS, 0].set(loss_part)
    tot = _sum_small(jnp.concatenate(
        [gsmall[0][0] + loss_tile, gsmall[0][1], gsmall[1][0], gsmall[1][1], dh[LEAD_PAD:BLOCK]], axis=0))
    loss = tot[ROW_LOSS, 0]
    ta = [tot[16 * l:16 * l + 8] for l in range(DEPTH)]
    tb = [tot[16 * l + 8:16 * l + 16] for l in range(DEPTH)]
    pick = lambda tiles, r0, r1, c0, c1: jnp.stack([t[r0:r1, c0:c1] for t in tiles])
    g_mlp_post = pick(ta, ROW_MLP_POST, ROW_MLP_POST + 1, 0, D_MODEL).reshape(DEPTH, D_MODEL)
    g_mlp_pre = pick(ta, ROW_MLP_PRE, ROW_MLP_PRE + 1, 0, D_MODEL).reshape(DEPTH, D_MODEL)
    g_mix_pre = pick(ta, ROW_MIX_PRE, ROW_MIX_PRE + 1, 0, D_MODEL).reshape(DEPTH, D_MODEL)
    g_sinks = pick(ta, ROW_SINK, ROW_SINK + 1, 0, N_Q_HEADS).reshape(DEPTH, N_Q_HEADS)
    g_mix_post = pick(tb, ROW_MIX_POST, ROW_MIX_POST + 1, 0, D_MODEL).reshape(DEPTH, D_MODEL)
    g_attn_out = pick(tb, ROW_GROUP_G, ROW_GROUP_G + 1, 0, ATTN_W).reshape(DEPTH, ATTN_W)
    g_conv_out = pick(tb, ROW_GROUP_G, ROW_GROUP_G + 1, ATTN_W, D_MODEL).reshape(DEPTH, CONV_W)
    g_conv_full = pick(tb, ROW_CONV, ROW_CONV + 3, 0, CONV_W)
    g_conv = lax.dynamic_slice_in_dim(g_conv_full, me * cshard, cshard, axis=2)
    g_meta = lax.dynamic_slice_in_dim(tot[16 * DEPTH:16 * DEPTH + N_META], me * mshard, mshard, axis=1)

    r_in, r_out, r_up, r_down = [[landed[(n, l)] for l in range(DEPTH)] for n in ("in", "out", "up", "down")]
    t12 = lambda a: jnp.swapaxes(a, 1, 2)
    g_w_in, d_w_in, nm_w_in, nv_w_in = map(t12, _sum_adamw(r_in, t12(w_in), t12(m_w_in), t12(v_w_in), 96, "adamw_w_in"))
    g_w_up, d_w_up, nm_w_up, nv_w_up = _sum_adamw(r_up, w_up, m_w_up, v_w_up, 128, "adamw_w_up", transposed=True)
    g_w_out, d_w_out, nm_w_out, nv_w_out = _sum_adamw(r_out, w_out, m_w_out, v_w_out, 128, "adamw_w_out")
    g_w_down, d_w_down, nm_w_down, nv_w_down = _sum_adamw(r_down, w_down, m_w_down, v_w_down, 128, "adamw_w_down")

    ws = [meta_tokens, mix_pre_g, conv_w.reshape(6, cshard), sinks, attn_out_g, conv_out_g, mix_post_g, mlp_pre_g, mlp_post_g]
    gs = [g_meta, g_mix_pre, g_conv.reshape(6, cshard), g_sinks, g_attn_out, g_conv_out, g_mix_post, g_mlp_pre, g_mlp_post]
    ms = [m_meta_tokens, m_mix_pre_g, m_conv_w.reshape(6, cshard), m_sinks, m_attn_out_g, m_conv_out_g, m_mix_post_g,
          m_mlp_pre_g, m_mlp_post_g]
    vs = [v_meta_tokens, v_mix_pre_g, v_conv_w.reshape(6, cshard), v_sinks, v_attn_out_g, v_conv_out_g, v_mix_post_g,
          v_mlp_pre_g, v_mlp_post_g]
    ds, nms, nvs = _adamw_small(ws, gs, ms, vs)

    def order(meta, mix_pre, cv, sk, a_out, c_out, mix_post, mlp_pre, mlp_post, win, wout, wup, wdown):
        return [meta, mix_pre, win, cv.reshape(DEPTH, 3, cshard), sk, a_out, c_out, wout, mix_post, mlp_pre, wup, wdown, mlp_post]

    grads = order(*gs, g_w_in, g_w_out, g_w_up, g_w_down)
    deltas = order(*ds, d_w_in, d_w_out, d_w_up, d_w_down)
    new_m = order(*nms, nm_w_in, nm_w_out, nm_w_up, nm_w_down)
    new_v = order(*nvs, nv_w_in, nv_w_out, nv_w_up, nv_w_down)
    return (loss, grad_x, *grads, *deltas, *new_m, *new_v)
```

```python
import functools
import math

import jax
import jax.numpy as jnp
from jax import lax
from jax.experimental import pallas as pl
from jax.experimental.pallas import tpu as pltpu

F32 = jnp.float32
BF = jnp.bfloat16

D_MODEL = 1024
ATTN_W = 512
CONV_W = 512
HEAD_DIM = 64
N_Q_HEADS = 8
ROT_DIM = 16
D_FF = 4096
IN_W = 2304
N_META = 16
BLOCK = 128
LEAD_PAD = BLOCK - N_META
ROPE_THETA = 500000.0
EPS = 1e-6
N_DEV = 8
DEPTH = 2
NEG = -1e30
SCALE = HEAD_DIM ** -0.5

ADAM_LR = 0.001
ADAM_B1 = 0.9
ADAM_B2 = 0.999
ADAM_EPS = 1e-08
ADAM_WD = 0.01
ADAM_STEP = 10

ROW_MLP_POST, ROW_MLP_PRE, ROW_MIX_PRE, ROW_SINK, ROW_LOSS = 0, 1, 2, 3, 4
ROW_MIX_POST, ROW_GROUP_G, ROW_CONV = 0, 1, 2

VMEM_LIMIT = 56 * 1024 * 1024
MESH = pl.DeviceIdType.MESH


def _dot(a, b):
    return jnp.dot(a, b, preferred_element_type=F32)


def _dot_nt(a, b):
    return lax.dot_general(a, b, (((1,), (1,)), ((), ())), preferred_element_type=F32)


def _dot_tn(a, b):
    return lax.dot_general(a, b, (((0,), (0,)), ((), ())), preferred_element_type=F32)


def _rms_fwd(x, g):
    r = lax.rsqrt(jnp.mean(x * x, axis=-1, keepdims=True) + EPS)
    return x * r * g


def _rms_bwd(x, g, dy):
    r = lax.rsqrt(jnp.mean(x * x, axis=-1, keepdims=True) + EPS)
    xh = x * r
    t = dy * g
    dx = r * (t - xh * jnp.mean(t * xh, axis=-1, keepdims=True))
    dg = jnp.sum(dy * xh, axis=0, keepdims=True)
    return dx, dg


def _row_tile(lp, cands=(640, 512, 384, 256, 128)):
    for t in cands:
        if lp % t == 0:
            return t
    raise ValueError(f"row count {lp} is not a multiple of 128")


def _full(shape):
    n = len(shape)
    return pl.BlockSpec(shape, lambda *_: (0,) * n, pipeline_mode=pl.Buffered(1))


def _full_out(shape):
    n = len(shape)
    return pl.BlockSpec(shape, lambda *_: (0,) * n)


def _params(sem=("arbitrary",)):
    return pltpu.CompilerParams(dimension_semantics=sem, vmem_limit_bytes=VMEM_LIMIT)


def _rope_table(lp):
    half = ROT_DIM // 2
    pos = jnp.maximum(jnp.arange(lp) - LEAD_PAD, 0).astype(F32)
    inv_freq = jnp.power(jnp.float32(ROPE_THETA), -jnp.arange(0, ROT_DIM, 2, dtype=F32) / ROT_DIM)
    ang_t = jnp.concatenate([inv_freq, inv_freq])[:, None] * pos[None, :]
    row = lax.broadcasted_iota(jnp.int32, (ROT_DIM, lp), 0)
    cs_t = jnp.where(row < half, jnp.cos(ang_t), jnp.sin(ang_t))
    return jnp.pad(cs_t.T, ((0, 0), (0, 128 - ROT_DIM)))


def _rope_coeffs(t):
    half = ROT_DIM // 2
    lane = lax.broadcasted_iota(jnp.int32, t.shape, 1)
    cos_a = jnp.where(lane < half, t, 0.0)
    sin_a = pltpu.roll(jnp.where((lane >= half) & (lane < ROT_DIM), t, 0.0), 128 - half, 1)
    c = cos_a + pltpu.roll(cos_a, half, 1) + jnp.where((lane >= ROT_DIM) & (lane < HEAD_DIM), 1.0, 0.0)
    s2 = pltpu.roll(sin_a, half, 1)
    both = lambda u: u + pltpu.roll(u, HEAD_DIM, 1)
    return both(c), both(-sin_a), both(s2)


def _rope(t, c, s1, s2):
    return t * c + pltpu.roll(t, BLOCK - 8, 1) * s1 + pltpu.roll(t, 8, 1) * s2


def _rope_t(dt, c, s1, s2):
    return dt * c + pltpu.roll(dt * s1, 8, 1) + pltpu.roll(dt * s2, BLOCK - 8, 1)


def _build_h(x, rope_compact, tm, exch, small_piece, name):
    seq = x.shape[0]
    lp = BLOCK + seq
    nt = lp // tm
    n_sub = tm // BLOCK
    small_shape = exch.land_shapes[small_piece].shape

    def body(*refs):
        h_ref, c_ref, s1_ref, s2_ref = refs[n_sub + 1:n_sub + 5]
        for j in range(n_sub):
            h_ref[j * BLOCK:(j + 1) * BLOCK, :] = refs[j][...]
        c_ref[...], s1_ref[...], s2_ref[...] = _rope_coeffs(refs[n_sub][...])

    def after(lands, *refs):
        h_ref, buf = refs[n_sub + 1], refs[n_sub + 5]
        pltpu.sync_copy(lands[small_piece], buf)
        h_ref[0:LEAD_PAD, :] = jnp.zeros((LEAD_PAD, D_MODEL), F32)
        for d in range(N_DEV):
            h_ref[LEAD_PAD:BLOCK, d * 128:(d + 1) * 128] = buf[d, 0:N_META, :]

    tile = lambda i: (i + 1) % nt
    piece = lambda j: pl.BlockSpec((BLOCK, D_MODEL), lambda i: (jnp.maximum(tile(i) * n_sub + j - 1, 0), 0))
    rows = lambda w: pl.BlockSpec((tm, w), lambda i: (tile(i), 0))
    (h, *rope), lands = _call(
        body, exch,
        name=name,
        grid=(nt,),
        in_specs=[piece(j) for j in range(n_sub)] + [rows(128)],
        out_specs=[rows(D_MODEL)] + [rows(128)] * 3,
        out_shape=[jax.ShapeDtypeStruct((lp, D_MODEL), F32)] + [jax.ShapeDtypeStruct((lp, 128), F32)] * 3,
        scratch_shapes=[pltpu.VMEM(small_shape, F32)],
        compiler_params=_params(),
        after=after,
    )(*([x] * n_sub), rope_compact)
    return h, rope, lands


def _in_proj_fwd(h, g, w_in_t, rope, tm, name, exch=None):
    lp = h.shape[0]

    def body(h_ref, g_ref, w_ref, c_ref, s1_ref, s2_ref, a_ref, qkv_ref, bch_ref):
        a = _rms_fwd(h_ref[...], g_ref[...]).astype(BF)
        a_ref[...] = a
        proj = _dot_nt(a, w_ref[...])
        c, s1, s2 = c_ref[...], s1_ref[...], s2_ref[...]
        for j in range(5):
            t = _rope(proj[:, j * 128:(j + 1) * 128], c, s1, s2)
            qkv_ref[:, j * 128:(j + 1) * 128] = (t * SCALE if j < 4 else t).astype(BF)
        qkv_ref[:, 640:768] = proj[:, 640:768].astype(BF)
        bch_ref[...] = proj[:, 768:].astype(BF)

    row = lambda w: pl.BlockSpec((tm, w), lambda i: (i, 0))
    return _call(
        body, exch,
        name=name,
        grid=(lp // tm,),
        in_specs=[row(D_MODEL), _full((1, D_MODEL)), _full((IN_W, D_MODEL)), row(128), row(128), row(128)],
        out_specs=[row(D_MODEL), row(768), row(3 * CONV_W)],
        out_shape=[
            jax.ShapeDtypeStruct((lp, D_MODEL), BF),
            jax.ShapeDtypeStruct((lp, 768), BF),
            jax.ShapeDtypeStruct((lp, 3 * CONV_W), BF),
        ],
        compiler_params=_params(),
    )(h, g, w_in_t, *rope)


def _fold_masks(i):
    r = lax.broadcasted_iota(jnp.int32, (2 * BLOCK, BLOCK), 0) & (BLOCK - 1)
    c = lax.broadcasted_iota(jnp.int32, (2 * BLOCK, BLOCK), 1)
    tri = c > r
    ok = jnp.where(tri, (i - 1) * BLOCK + c, i * BLOCK + c) >= LEAD_PAD
    return tri, ok


def _kv_operand(x, kvh):
    lane = lax.broadcasted_iota(jnp.int32, x.shape, 1)
    zero = jnp.zeros_like(x)
    if kvh == 0:
        lo = jnp.where(lane < HEAD_DIM, x, zero)
        hi = pltpu.roll(lo, HEAD_DIM, 1)
    else:
        hi = jnp.where(lane >= HEAD_DIM, x, zero)
        lo = pltpu.roll(hi, HEAD_DIM, 1)
    return jnp.concatenate([lo, hi], axis=0)


def _split4(t, tri):
    zero = jnp.zeros_like(t[0])
    return jnp.concatenate(
        [jnp.where(tri, t[0], zero), jnp.where(tri, zero, t[0]), jnp.where(tri, t[1], zero), jnp.where(tri, zero, t[1])], axis=1)


def _sink_cols(sink_ref, kvh):
    first = lax.broadcasted_iota(jnp.int32, (2 * BLOCK, 1), 0) < BLOCK
    return [jnp.where(first, sink_ref[0, 4 * kvh + half], sink_ref[0, 4 * kvh + 2 + half]) for half in range(2)]


def _folded_exp(q2, k4, tri, ok, sks):
    s = _dot_nt(q2, k4)
    es, ss = [], []
    for half in range(2):
        s_h = s[:, 2 * half * BLOCK:2 * (half + 1) * BLOCK]
        sf = jnp.where(ok, jnp.where(tri, s_h[:, :BLOCK], s_h[:, BLOCK:]), NEG)
        m = jnp.maximum(jnp.max(sf, axis=-1, keepdims=True), sks[half])
        es.append(jnp.exp(sf - m))
        ss.append(jnp.exp(sks[half] - m))
    sums = _dot(jnp.concatenate(es, axis=0).astype(BF), jnp.ones((BLOCK, BLOCK), BF))
    invs = [1.0 / (sums[2 * half * BLOCK:2 * (half + 1) * BLOCK] + ss[half]) for half in range(2)]
    return es, ss, invs


def _attn_fwd(qkv, sink, name, exch=None):
    lp = qkv.shape[0]
    nb = lp // BLOCK
    per_step = 2

    def one_block(i, sink_ref, q_ref, kvc_ref, kvp_ref, o_ref, p_ref, ps_ref):
        tri, ok = _fold_masks(i)
        kvc, kvp = kvc_ref[...], kvp_ref[...]
        kk = jnp.concatenate([kvp[:, :128], kvc[:, :128]], axis=0)
        vv = jnp.concatenate([kvp[:, 128:], kvc[:, 128:]], axis=0)
        lane = lax.broadcasted_iota(jnp.int32, (BLOCK, 128), 1)
        p_sink = jnp.zeros((BLOCK, 128), F32)
        for kvh in range(2):
            q2 = jnp.concatenate([q_ref[:, 256 * kvh:256 * kvh + 128], q_ref[:, 256 * kvh + 128:256 * kvh + 256]], axis=0)
            es, ss, invs = _folded_exp(q2, _kv_operand(kk, kvh), tri, ok, _sink_cols(sink_ref, kvh))
            pb = [(es[half] * invs[half]).astype(BF) for half in range(2)]
            out = _dot(_split4(pb, tri), _kv_operand(vv, kvh))
            for pair in range(2):
                rows = slice(pair * BLOCK, (pair + 1) * BLOCK)
                o_ref[:, 256 * kvh + 128 * pair:256 * kvh + 128 * (pair + 1)] = out[rows].astype(BF)
                for half in range(2):
                    head = 4 * kvh + 2 * pair + half
                    p_ref[:, 128 * head:128 * (head + 1)] = pb[half][rows]
                    p_sink = jnp.where(lane == head, (ss[half] * invs[half][:, 0:1])[rows], p_sink)
        ps_ref[...] = p_sink

    def body(sink_ref, *refs):
        q_refs, kv_refs = refs[:per_step], refs[per_step:2 * per_step + 1]
        o_ref, p_ref, ps_ref = refs[2 * per_step + 1:]
        for j in range(per_step):
            rows = slice(j * BLOCK, (j + 1) * BLOCK)
            one_block(per_step * pl.program_id(0) + j, sink_ref, q_refs[j], kv_refs[j + 1], kv_refs[j],
                      o_ref.at[rows], p_ref.at[rows], ps_ref.at[rows])

    last = nb - 1
    blk = lambda j: (lambda s: jnp.minimum(per_step * s + j, last))
    out_rows = lambda w: pl.BlockSpec((per_step * BLOCK, w), lambda s: (s, 0))
    return _call(
        body, exch,
        name=name,
        grid=(pl.cdiv(nb, per_step),),
        in_specs=[pl.BlockSpec(memory_space=pltpu.SMEM)]
        + [pl.BlockSpec((BLOCK, ATTN_W), lambda s, j=j: (blk(j)(s), 0)) for j in range(per_step)]
        + [pl.BlockSpec((BLOCK, 256), lambda s: (jnp.maximum(per_step * s - 1, 0), 2))]
        + [pl.BlockSpec((BLOCK, 256), lambda s, j=j: (blk(j)(s), 2)) for j in range(per_step)],
        out_specs=[out_rows(ATTN_W), out_rows(N_Q_HEADS * BLOCK), out_rows(128)],
        out_shape=[jax.ShapeDtypeStruct((lp, ATTN_W), BF), jax.ShapeDtypeStruct((lp, N_Q_HEADS * BLOCK), BF),
                   jax.ShapeDtypeStruct((lp, 128), F32)],
        compiler_params=_params(),
    )(sink, *([qkv] * (2 * per_step + 1)))


def _mix_out_fwd(bch, y_attn, h, conv_w, g_a, g_c, w_out, g_post, tm, name, exch=None):
    lp = h.shape[0]

    def body(bch_ref, ya_ref, h_ref, cw_ref, ga_ref, gc_ref, w_ref, gp_ref, yc_ref, y_ref, z_ref, h2_ref, ext):
        i = pl.program_id(0)

        @pl.when(i == 0)
        def _():
            ext[0:8, :] = jnp.zeros((8, CONV_W), F32)

        b = bch_ref[:, 0:CONV_W].astype(F32)
        u = bch_ref[:, CONV_W:2 * CONV_W].astype(F32) * bch_ref[:, 2 * CONV_W:3 * CONV_W].astype(F32)
        ext[8:8 + tm, :] = u
        yc = cw_ref[0:1, :] * ext[6:6 + tm, :] + cw_ref[1:2, :] * ext[7:7 + tm, :] + cw_ref[2:3, :] * u
        ext[0:8, :] = u[tm - 8:tm, :]
        yc_ref[...] = yc.astype(BF)
        ya = _rms_fwd(ya_ref[...].astype(F32), ga_ref[...]).astype(BF)
        yb = _rms_fwd(b * yc, gc_ref[...]).astype(BF)
        y_ref[:, 0:ATTN_W] = ya
        y_ref[:, ATTN_W:] = yb
        z = _dot(ya, w_ref[0:ATTN_W, :]) + _dot(yb, w_ref[ATTN_W:, :])
        z_ref[...] = z.astype(BF)
        h2_ref[...] = h_ref[...] + _rms_fwd(z, gp_ref[...])

    row = lambda w: pl.BlockSpec((tm, w), lambda i: (i, 0))
    return _call(
        body, exch,
        name=name,
        grid=(lp // tm,),
        in_specs=[
            row(3 * CONV_W), row(ATTN_W), row(D_MODEL), _full((8, CONV_W)), _full((1, ATTN_W)), _full((1, CONV_W)),
            _full((D_MODEL, D_MODEL)), _full((1, D_MODEL)),
        ],
        out_specs=[row(CONV_W), row(D_MODEL), row(D_MODEL), row(D_MODEL)],
        out_shape=[
            jax.ShapeDtypeStruct((lp, CONV_W), BF),
            jax.ShapeDtypeStruct((lp, D_MODEL), BF),
            jax.ShapeDtypeStruct((lp, D_MODEL), BF),
            jax.ShapeDtypeStruct((lp, D_MODEL), F32),
        ],
        scratch_shapes=[pltpu.VMEM((tm + 8, CONV_W), F32)],
        compiler_params=_params(),
    )(bch, y_attn, h, conv_w, g_a, g_c, w_out, g_post)


def _mlp_fwd(h2, g_pre, w_up_t, w_down, g_post, tm, name, exch=None, target=None):
    lp = h2.shape[0]
    sub = math.gcd(tm, BLOCK)
    n_sub, lead = tm // sub, BLOCK // sub
    n_t = n_sub if target is not None else 0

    def body(*refs):
        h_ref, gp_ref, wu_ref, wd_ref, gq_ref = refs[:5]
        t_refs = refs[5:5 + n_t]
        a_ref, up_ref, f_ref, last_ref = refs[5 + n_t:9 + n_t]
        h = h_ref[...]
        a = _rms_fwd(h, gp_ref[...]).astype(BF)
        a_ref[...] = a
        up = _dot_nt(a, wu_ref[...])
        up_ref[...] = up.astype(BF)
        act = jnp.square(jnp.maximum(up, 0.0)).astype(BF)
        f = _dot(act, wd_ref[...])
        f_ref[...] = f
        h3 = h + _rms_fwd(f, gq_ref[...])
        if target is None:
            last_ref[...] = h3
            return
        ls_ref = refs[9 + n_t]
        i = pl.program_id(0)

        @pl.when(i == 0)
        def _():
            ls_ref[...] = jnp.zeros((8, 128), F32)

        sq = jnp.zeros((8, D_MODEL), F32)
        for j in range(n_sub):
            on_tokens = i * n_sub + j >= lead
            d = jnp.where(on_tokens, h3[j * sub:(j + 1) * sub] - t_refs[j][...], 0.0)
            last_ref[j * sub:(j + 1) * sub, :] = d * (1.0 / D_MODEL)
            sq = sq + jnp.sum((d * d).reshape(sub // 8, 8, D_MODEL), axis=0)
        ls_ref[...] += sum(sq[:, k * 128:(k + 1) * 128] for k in range(D_MODEL // 128))

        @pl.when(i == lp // tm - 1)
        def _():
            ls_ref[...] = jnp.full((8, 128), jnp.sum(ls_ref[...]), F32)

    row = lambda w: pl.BlockSpec((tm, w), lambda i: (i, 0))
    piece = lambda j: pl.BlockSpec((sub, D_MODEL), lambda i: (jnp.maximum(i * n_sub + j - lead, 0), 0))
    out_specs = [row(D_MODEL), row(D_FF), row(D_MODEL), row(D_MODEL)]
    out_shape = [
        jax.ShapeDtypeStruct((lp, D_MODEL), BF),
        jax.ShapeDtypeStruct((lp, D_FF), BF),
        jax.ShapeDtypeStruct((lp, D_MODEL), F32),
        jax.ShapeDtypeStruct((lp, D_MODEL), F32),
    ]
    if target is not None:
        out_specs.append(_full_out((8, 128)))
        out_shape.append(jax.ShapeDtypeStruct((8, 128), F32))
    return _call(
        body, exch,
        name=name,
        grid=(lp // tm,),
        in_specs=[row(D_MODEL), _full((1, D_MODEL)), _full((D_FF, D_MODEL)), _full((D_FF, D_MODEL)), _full((1, D_MODEL))]
        + [piece(j) for j in range(n_t)],
        out_specs=out_specs,
        out_shape=out_shape,
        compiler_params=_params(),
    )(h2, g_pre, w_up_t, w_down, g_post, *([target] * n_t))


def _mlp_bwd_dx(dh3, f, up, h2, w_down, w_up_t, g_post, g_pre, tm, name, exch=None):
    lp = h2.shape[0]

    def body(dh3_ref, f_ref, up_ref, h2_ref, wd_ref, wu_ref, gq_ref, gp_ref, df_ref, dup_ref, dh2_ref, dg_ref):
        i = pl.program_id(0)

        @pl.when(i == 0)
        def _():
            dg_ref[...] = jnp.zeros((8, D_MODEL), F32)

        dh3 = dh3_ref[...]
        df, dgq = _rms_bwd(f_ref[...], gq_ref[...], dh3)
        dg_ref[ROW_MLP_POST:ROW_MLP_POST + 1, :] += dgq
        df = df.astype(BF)
        df_ref[...] = df
        dact = _dot_nt(df, wd_ref[...])
        dup = (dact * (2.0 * jnp.maximum(up_ref[...].astype(F32), 0.0))).astype(BF)
        dup_ref[...] = dup
        da = _dot(dup, wu_ref[...])
        dh, dgp = _rms_bwd(h2_ref[...], gp_ref[...], da)
        dg_ref[ROW_MLP_PRE:ROW_MLP_PRE + 1, :] += dgp
        dh2_ref[...] = dh3 + dh

    row = lambda w: pl.BlockSpec((tm, w), lambda i: (i, 0))
    return _call(
        body, exch,
        name=name,
        grid=(lp // tm,),
        in_specs=[
            row(D_MODEL), row(D_MODEL), row(D_FF), row(D_MODEL), _full((D_FF, D_MODEL)), _full((D_FF, D_MODEL)),
            _full((1, D_MODEL)), _full((1, D_MODEL)),
        ],
        out_specs=[row(D_MODEL), row(D_FF), row(D_MODEL), _full_out((8, D_MODEL))],
        out_shape=[
            jax.ShapeDtypeStruct((lp, D_MODEL), BF),
            jax.ShapeDtypeStruct((lp, D_FF), BF),
            jax.ShapeDtypeStruct((lp, D_MODEL), F32),
            jax.ShapeDtypeStruct((8, D_MODEL), F32),
        ],
        compiler_params=_params(),
    )(dh3, f, up, h2, w_down, w_up_t, g_post, g_pre)


def _mlp_bwd_dw(up, df, dup, a2, tm, name):
    lp = up.shape[0]
    nt = lp // tm
    nj = D_FF // D_MODEL

    def body(up_ref, df_ref, dup_ref, a_ref, dwd_ref, dwu_ref, accd, accu):
        i = pl.program_id(1)

        @pl.when(i == 0)
        def _():
            accd[...] = jnp.zeros_like(accd)
            accu[...] = jnp.zeros_like(accu)

        act = jnp.square(jnp.maximum(up_ref[...].astype(F32), 0.0)).astype(BF)
        accd[...] += _dot_tn(act, df_ref[...])
        accu[...] += _dot_tn(dup_ref[...], a_ref[...])

        @pl.when(i == nt - 1)
        def _():
            dwd_ref[...] = accd[...].astype(BF)
            dwu_ref[...] = accu[...].astype(BF)

    return pl.pallas_call(
        body,
        name=name,
        grid=(nj, nt),
        in_specs=[
            pl.BlockSpec((tm, D_MODEL), lambda j, i: (i, j)),
            pl.BlockSpec((tm, D_MODEL), lambda j, i: (i, 0)),
            pl.BlockSpec((tm, D_MODEL), lambda j, i: (i, j)),
            pl.BlockSpec((tm, D_MODEL), lambda j, i: (i, 0)),
        ],
        out_specs=[pl.BlockSpec((D_MODEL, D_MODEL), lambda j, i: (j, 0)), pl.BlockSpec((D_MODEL, D_MODEL), lambda j, i: (j, 0))],
        out_shape=[jax.ShapeDtypeStruct((D_FF, D_MODEL), BF), jax.ShapeDtypeStruct((D_FF, D_MODEL), BF)],
        scratch_shapes=[pltpu.VMEM((D_MODEL, D_MODEL), F32), pltpu.VMEM((D_MODEL, D_MODEL), F32)],
        compiler_params=_params(("arbitrary", "arbitrary")),
    )(up, df, dup, a2)


def _mix_out_bwd(dh2, z, y_attn, yc, bch, y, w_out, g_post, g_a, g_c, conv_w, tm, name, exch=None):
    lp = dh2.shape[0]
    nt = lp // tm

    def body(dh2_ref, z_ref, ya_ref, yc_ref, bch_ref, y_ref, w_ref, gp_ref, ga_ref, gc_ref, cw_ref,
             dya_ref, dbch_ref, dg_ref, dwo_ref, ext, acco):
        i = pl.program_id(0)
        dcw_ref = dg_ref.at[ROW_CONV:ROW_CONV + 3, 0:CONV_W]

        @pl.when(i == 0)
        def _():
            ext[tm:tm + 8, :] = jnp.zeros((8, CONV_W), F32)
            dg_ref[...] = jnp.zeros((8, D_MODEL), F32)
            acco[...] = jnp.zeros_like(acco)

        dz, dgp = _rms_bwd(z_ref[...].astype(F32), gp_ref[...], dh2_ref[...])
        dg_ref[ROW_MIX_POST:ROW_MIX_POST + 1, :] += dgp
        dz = dz.astype(BF)
        acco[...] += _dot_tn(y_ref[...], dz)
        dya_n = _dot_nt(dz, w_ref[0:ATTN_W, :])
        dyb_n = _dot_nt(dz, w_ref[ATTN_W:, :])
        dya, dga = _rms_bwd(ya_ref[...].astype(F32), ga_ref[...], dya_n)
        dg_ref[ROW_GROUP_G:ROW_GROUP_G + 1, 0:ATTN_W] += dga
        dya_ref[...] = dya
        b = bch_ref[:, 0:CONV_W].astype(F32)
        c = bch_ref[:, CONV_W:2 * CONV_W].astype(F32)
        hc = bch_ref[:, 2 * CONV_W:3 * CONV_W].astype(F32)
        u = c * hc
        yc_v = yc_ref[...].astype(F32)
        dyconv, dgc = _rms_bwd(b * yc_v, gc_ref[...], dyb_n)
        dg_ref[ROW_GROUP_G:ROW_GROUP_G + 1, ATTN_W:] += dgc
        dbch_ref[:, 0:CONV_W] = (dyconv * yc_v).astype(BF)
        dyc = dyconv * b
        ext[0:tm, :] = dyc
        d1 = ext[1:1 + tm, :]
        d2 = ext[2:2 + tm, :]
        du = cw_ref[2:3, :] * dyc + cw_ref[1:2, :] * d1 + cw_ref[0:1, :] * d2
        ext[tm:tm + 8, :] = dyc[0:8, :]
        dbch_ref[:, CONV_W:2 * CONV_W] = (du * hc).astype(BF)
        dbch_ref[:, 2 * CONV_W:3 * CONV_W] = (du * c).astype(BF)
        dcw_ref[0:1, :] += jnp.sum(u * d2, axis=0, keepdims=True)
        dcw_ref[1:2, :] += jnp.sum(u * d1, axis=0, keepdims=True)
        dcw_ref[2:3, :] += jnp.sum(u * dyc, axis=0, keepdims=True)

        @pl.when(i == nt - 1)
        def _():
            dwo_ref[...] = acco[...].astype(BF)

    row = lambda w: pl.BlockSpec((tm, w), lambda i: (nt - 1 - i, 0))
    return _call(
        body, exch,
        name=name,
        grid=(nt,),
        in_specs=[
            row(D_MODEL), row(D_MODEL), row(ATTN_W), row(CONV_W), row(3 * CONV_W), row(D_MODEL), _full((D_MODEL, D_MODEL)),
            _full((1, D_MODEL)), _full((1, ATTN_W)), _full((1, CONV_W)), _full((8, CONV_W)),
        ],
        out_specs=[row(ATTN_W), row(3 * CONV_W), _full_out((8, D_MODEL)), _full_out((D_MODEL, D_MODEL))],
        out_shape=[
            jax.ShapeDtypeStruct((lp, ATTN_W), F32),
            jax.ShapeDtypeStruct((lp, 3 * CONV_W), BF),
            jax.ShapeDtypeStruct((8, D_MODEL), F32),
            jax.ShapeDtypeStruct((D_MODEL, D_MODEL), BF),
        ],
        scratch_shapes=[pltpu.VMEM((tm + 8, CONV_W), F32), pltpu.VMEM((D_MODEL, D_MODEL), F32)],
        compiler_params=_params(),
    )(dh2, z, y_attn, yc, bch, y, w_out, g_post, g_a, g_c, conv_w)


def _attn_bwd(qkv, o, do, probs, p_sink, rope, name, exch=None):
    lp = qkv.shape[0]
    nb = lp // BLOCK

    def body(q_ref, kvc_ref, kvp_ref, o_ref, do_ref, p_ref, ps_ref, cq_ref, s1q_ref, s2q_ref, ck_ref, s1k_ref, s2k_ref,
             dq_ref, dkv_ref, dsink_ref, carry):
        i = pl.program_id(0)

        @pl.when(i == 0)
        def _():
            carry[...] = jnp.zeros_like(carry)
            dsink_ref[...] = jnp.zeros((8, 128), F32)

        def finish(tot):
            dk = _rope_t(tot[:, :128], ck_ref[...], s1k_ref[...], s2k_ref[...])
            dkv_ref[:, 0:128] = dk.astype(BF)
            dkv_ref[:, 128:256] = tot[:, 128:].astype(BF)

        @pl.when(i < nb)
        def _():
            tri, _ = _fold_masks(i)
            kvc, kvp = kvc_ref[...], kvp_ref[...]
            kk = jnp.concatenate([kvp[:, :128], kvc[:, :128]], axis=0)
            vv = jnp.concatenate([kvp[:, 128:], kvc[:, 128:]], axis=0)
            lane = lax.broadcasted_iota(jnp.int32, (BLOCK, 128), 1)
            lane2 = lax.broadcasted_iota(jnp.int32, (2 * BLOCK, 128), 1)
            rope_q = (cq_ref[...], s1q_ref[...], s2q_ref[...])
            deltas = jnp.zeros((BLOCK, 128), F32)
            folded = []
            for kvh in range(2):
                c0 = 256 * kvh
                q2 = jnp.concatenate([q_ref[:, c0:c0 + 128], q_ref[:, c0 + 128:c0 + 256]], axis=0)
                do2 = jnp.concatenate([do_ref[:, c0:c0 + 128], do_ref[:, c0 + 128:c0 + 256]], axis=0)
                o2 = jnp.concatenate([o_ref[:, c0:c0 + 128], o_ref[:, c0 + 128:c0 + 256]], axis=0).astype(F32)
                k4, v4 = _kv_operand(kk, kvh), _kv_operand(vv, kvh)
                prod = do2 * o2
                dob = do2.astype(BF)
                dp = _dot_nt(dob, v4)
                ds, pb = [], []
                for half in range(2):
                    heads = [4 * kvh + 2 * pair + half for pair in range(2)]
                    p = jnp.concatenate([p_ref[:, 128 * h:128 * (h + 1)] for h in heads], axis=0)
                    sel = (lane2 < HEAD_DIM) if half == 0 else (lane2 >= HEAD_DIM)
                    delta = jnp.sum(jnp.where(sel, prod, 0.0), axis=-1, keepdims=True)
                    dp_h = dp[:, 2 * half * BLOCK:2 * (half + 1) * BLOCK]
                    ds.append((p.astype(F32) * (jnp.where(tri, dp_h[:, :BLOCK], dp_h[:, BLOCK:]) - delta)).astype(BF))
                    pb.append(p)
                    for pair in range(2):
                        deltas = jnp.where(lane == heads[pair], delta[pair * BLOCK:(pair + 1) * BLOCK], deltas)
                ds4, p4 = _split4(ds, tri), _split4(pb, tri)
                dq2 = _dot(ds4, k4) * SCALE
                dq_ref[:, c0:c0 + 128] = _rope_t(dq2[:BLOCK], *rope_q).astype(BF)
                dq_ref[:, c0 + 128:c0 + 256] = _rope_t(dq2[BLOCK:], *rope_q).astype(BF)
                rk, rv = _dot_tn(ds4, q2), _dot_tn(p4, dob)
                own = (lane < HEAD_DIM) if kvh == 0 else (lane >= HEAD_DIM)
                group = []
                for r in (rk, rv):
                    for blk in range(2):
                        t = jnp.where(lane < HEAD_DIM, r[blk * BLOCK:(blk + 1) * BLOCK], r[(2 + blk) * BLOCK:(3 + blk) * BLOCK])
                        group.append(jnp.where(own, t + pltpu.roll(t, HEAD_DIM, 1), 0.0))
                folded.append(group)
            dsink_ref[ROW_SINK:ROW_SINK + 1, :] -= jnp.sum(ps_ref[...] * deltas, axis=0, keepdims=True)
            dk_p, dk_c, dv_p, dv_c = [folded[0][t] + folded[1][t] for t in range(4)]
            finish(carry[...] + jnp.concatenate([dk_p, dv_p], axis=1))
            carry[...] = jnp.concatenate([dk_c, dv_c], axis=1)

        @pl.when(i == nb)
        def _():
            finish(carry[...])

    qi = lambda i: jnp.minimum(i, nb - 1)
    ki = lambda i: jnp.maximum(i - 1, 0)
    tab_q = pl.BlockSpec((BLOCK, 128), lambda i: (qi(i), 0))
    tab_k = pl.BlockSpec((BLOCK, 128), lambda i: (ki(i), 0))
    return _call(
        body, exch,
        name=name,
        grid=(nb + 1,),
        in_specs=[
            pl.BlockSpec((BLOCK, ATTN_W), lambda i: (qi(i), 0)),
            pl.BlockSpec((BLOCK, 256), lambda i: (qi(i), 2)),
            pl.BlockSpec((BLOCK, 256), lambda i: (jnp.maximum(qi(i) - 1, 0), 2)),
            pl.BlockSpec((BLOCK, ATTN_W), lambda i: (qi(i), 0)),
            pl.BlockSpec((BLOCK, ATTN_W), lambda i: (qi(i), 0)),
            pl.BlockSpec((BLOCK, N_Q_HEADS * BLOCK), lambda i: (qi(i), 0)),
            tab_q, tab_q, tab_q, tab_q, tab_k, tab_k, tab_k,
        ],
        out_specs=[
            pl.BlockSpec((BLOCK, ATTN_W), lambda i: (qi(i), 0)),
            pl.BlockSpec((BLOCK, 256), lambda i: (ki(i), 0)),
            pl.BlockSpec((8, 128), lambda i: (0, 0)),
        ],
        out_shape=[
            jax.ShapeDtypeStruct((lp, ATTN_W), BF),
            jax.ShapeDtypeStruct((lp, 256), BF),
            jax.ShapeDtypeStruct((8, 128), F32),
        ],
        scratch_shapes=[pltpu.VMEM((BLOCK, 256), F32)],
        compiler_params=_params(),
    )(qkv, qkv, qkv, o, do, probs, p_sink, *rope, *rope)


def _in_proj_bwd_dx(dq, dkv, dbch, w_in_t, h, dh2, g, tm, name, exch=None):
    lp = h.shape[0]

    def body(dq_ref, dkv_ref, dbch_ref, w_ref, h_ref, dh2_ref, g_ref, dh_ref, dg_ref):
        i = pl.program_id(0)

        @pl.when(i == 0)
        def _():
            dg_ref[...] = jnp.zeros((8, D_MODEL), F32)

        da = _dot(jnp.concatenate([dq_ref[...], dkv_ref[...], dbch_ref[...]], axis=1), w_ref[...])
        dh, dg = _rms_bwd(h_ref[...], g_ref[...], da)
        dg_ref[ROW_MIX_PRE:ROW_MIX_PRE + 1, :] += dg
        dh_ref[...] = dh2_ref[...] + dh

    row = lambda w: pl.BlockSpec((tm, w), lambda i: (i, 0))
    return _call(
        body, exch,
        name=name,
        grid=(lp // tm,),
        in_specs=[row(ATTN_W), row(256), row(3 * CONV_W), _full((IN_W, D_MODEL)), row(D_MODEL), row(D_MODEL), _full((1, D_MODEL))],
        out_specs=[row(D_MODEL), _full_out((8, D_MODEL))],
        out_shape=[jax.ShapeDtypeStruct((lp, D_MODEL), F32), jax.ShapeDtypeStruct((8, D_MODEL), F32)],
        compiler_params=_params(),
    )(dq, dkv, dbch, w_in_t, h, dh2, g)


def _mix_bwd_dw(dq, dkv, dbch, a, tm, name, exch=None):
    lp = a.shape[0]
    nt = lp // tm

    def body(dq_ref, dkv_ref, dbch_ref, a_ref, dwi_ref, acci):
        i = pl.program_id(0)

        @pl.when(i == 0)
        def _():
            acci[...] = jnp.zeros_like(acci)

        a_v = a_ref[...]
        acci[0:512, :] += _dot_tn(dq_ref[...], a_v)
        acci[512:768, :] += _dot_tn(dkv_ref[...], a_v)
        acci[768:, :] += _dot_tn(dbch_ref[...], a_v)

        @pl.when(i == nt - 1)
        def _():
            dwi_ref[...] = acci[...].astype(BF)

    row = lambda w: pl.BlockSpec((tm, w), lambda i: (i, 0))
    return _call(
        body, exch,
        name=name,
        grid=(nt,),
        in_specs=[row(ATTN_W), row(256), row(3 * CONV_W), row(D_MODEL)],
        out_specs=[_full_out((IN_W, D_MODEL))],
        out_shape=[jax.ShapeDtypeStruct((IN_W, D_MODEL), BF)],
        scratch_shapes=[pltpu.VMEM((IN_W, D_MODEL), F32)],
        compiler_params=_params(),
    )(dq, dkv, dbch, a)


def _mesh_place():
    x, y, c = lax.axis_index("x"), lax.axis_index("y"), lax.axis_index("c")
    return x, y, c, 4 * x + 2 * y + c


def _peer(x, y, c, k):
    px = 1 - x if k & 4 else x
    py = 1 - y if k & 2 else y
    pc = 1 - c if k & 1 else c
    return (px, py, pc), 4 * px + 2 * py + pc


SIBLING = 1
SAME_CORE = (2, 4, 6)
OTHER_CORE = (3, 5, 7)


class _Exchange:
    def __init__(self, pieces):
        self.srcs = [s for s, _ in pieces]
        self.to_all = [g for _, g in pieces]
        self.n = len(pieces)
        self.land_shapes = [
            jax.ShapeDtypeStruct((N_DEV,) + (s.shape if g else s.shape[1:]), s.dtype) for s, g in pieces]
        self.sem_shapes = [pltpu.SemaphoreType.DMA((self.n, N_DEV - 1)), pltpu.SemaphoreType.DMA((self.n, N_DEV - 1)),
                           pltpu.SemaphoreType.DMA((self.n,))]
        self.forwards = any(self.to_all)

    def _ops(self, srcs, lands, sems):
        send_sems, recv_sems, local_sems = sems
        x, y, c, me = _mesh_place()

        def remote(p, k, src, slot, to):
            return pltpu.make_async_remote_copy(
                src_ref=src, dst_ref=lands[p].at[slot], send_sem=send_sems.at[p, k - 1], recv_sem=recv_sems.at[p, k - 1],
                device_id=to, device_id_type=MESH)

        def own(p):
            return pltpu.make_async_copy(srcs[p] if self.to_all[p] else srcs[p].at[me], lands[p].at[me], local_sems.at[p])

        def direct(p, k):
            peer, pidx = _peer(x, y, c, k)
            return remote(p, k, srcs[p] if self.to_all[p] else srcs[p].at[pidx], me, peer)

        def forward(p, k):
            sibling, _ = _peer(x, y, c, SIBLING)
            _, origin = _peer(x, y, c, k ^ SIBLING)
            return remote(p, k, lands[p].at[origin], origin, sibling)

        def arrival(p, k):
            peer, pidx = _peer(x, y, c, k)
            return remote(p, k, lands[p].at[pidx], pidx, peer)

        return own, direct, forward, arrival

    def start(self, srcs, lands, sems):
        own, direct, _, _ = self._ops(srcs, lands, sems)
        for p in range(self.n):
            own(p).start()
            for k in ((SIBLING,) + SAME_CORE) if self.to_all[p] else range(1, N_DEV):
                direct(p, k).start()

    def forward(self, srcs, lands, sems):
        _, _, forward, arrival = self._ops(srcs, lands, sems)
        for p in range(self.n):
            if self.to_all[p]:
                for k in SAME_CORE:
                    arrival(p, k).wait_recv()
                    forward(p, k ^ SIBLING).start()

    def finish(self, srcs, lands, sems):
        own, direct, forward, arrival = self._ops(srcs, lands, sems)
        for p in range(self.n):
            for k in ((SIBLING,) + OTHER_CORE) if self.to_all[p] else range(1, N_DEV):
                arrival(p, k).wait_recv()
        for p in range(self.n):
            for k in range(1, N_DEV):
                (forward(p, k) if self.to_all[p] and k in OTHER_CORE else direct(p, k)).wait_send()
            own(p).wait()


class _LayerRows:
    def __init__(self, array, layer):
        self.array = array if array.ndim == 3 else array.reshape(DEPTH, 1, -1)
        self.layer = layer

    def spec(self):
        layer = self.layer
        return pl.BlockSpec((None,) + self.array.shape[1:], lambda *_: (layer, 0, 0), pipeline_mode=pl.Buffered(1))


def _call(body, exch, *, name, grid, in_specs, out_specs, out_shape, scratch_shapes=(), compiler_params, after=None):
    def with_layer_rows(args):
        specs = [a.spec() if isinstance(a, _LayerRows) else s for s, a in zip(in_specs, args)]
        return specs, [a.array if isinstance(a, _LayerRows) else a for a in args]

    if exch is None:
        def plain(*args):
            specs, args = with_layer_rows(args)
            return pl.pallas_call(body, name=name, grid=grid, in_specs=specs, out_specs=out_specs, out_shape=out_shape,
                                  scratch_shapes=scratch_shapes, compiler_params=compiler_params)(*args)
        return plain
    n_in, n_out, n_scr, n_x = len(in_specs), len(out_shape), len(scratch_shapes), exch.n
    steps = math.prod(grid)

    def carrying(*refs):
        a, b, c, d, e = n_in, n_in + n_x, n_in + n_x + n_out, n_in + 2 * n_x + n_out, n_in + 2 * n_x + n_out + n_scr
        ins, srcs, outs, lands, scr, sems = refs[:a], refs[a:b], refs[b:c], refs[c:d], refs[d:e], refs[e:]
        step = functools.reduce(lambda acc, t: acc * grid[t] + pl.program_id(t), range(len(grid)), 0)

        @pl.when(step == 0)
        def _():
            exch.start(srcs, lands, sems)

        body(*ins, *outs, *scr)

        if exch.forwards:
            @pl.when(step == max(0, steps - 1 - (steps + 7) // 8))
            def _():
                exch.forward(srcs, lands, sems)

        @pl.when(step == steps - 1)
        def _():
            exch.finish(srcs, lands, sems)
            if after is not None:
                after(lands, *ins, *outs, *scr)

    hbm = pl.BlockSpec(memory_space=pl.ANY)

    def run(*args):
        specs, args = with_layer_rows(args)
        res = pl.pallas_call(
            carrying, name=name, grid=grid, in_specs=specs + [hbm] * n_x, out_specs=list(out_specs) + [hbm] * n_x,
            out_shape=list(out_shape) + exch.land_shapes, scratch_shapes=list(scratch_shapes) + exch.sem_shapes,
            compiler_params=compiler_params)(*args, *exch.srcs)
        return list(res[:n_out]), list(res[n_out:])

    return run


def _sum_small(part):
    exch = _Exchange([(part, True)])

    def body(part_ref, out_ref, land, *sems):
        exch.start([part_ref], [land], sems)
        exch.forward([part_ref], [land], sems)
        exch.finish([part_ref], [land], sems)
        acc = land[0]
        for d in range(1, N_DEV):
            acc = acc + land[d]
        out_ref[...] = acc

    vmem = pl.BlockSpec(memory_space=pltpu.VMEM)
    return pl.pallas_call(
        body,
        name="sum_small",
        in_specs=[vmem],
        out_specs=vmem,
        out_shape=jax.ShapeDtypeStruct(part.shape, F32),
        scratch_shapes=[pltpu.VMEM(exch.land_shapes[0].shape, F32)] + exch.sem_shapes,
    )(part)


def _adamw(w, g, m, v):
    m = ADAM_B1 * m + (1.0 - ADAM_B1) * g
    v = ADAM_B2 * v + (1.0 - ADAM_B2) * jnp.square(g)
    m_hat = m / (1.0 - ADAM_B1 ** ADAM_STEP)
    v_hat = v / (1.0 - ADAM_B2 ** ADAM_STEP)
    delta = -ADAM_LR * (m_hat / (jnp.sqrt(v_hat) + ADAM_EPS) + ADAM_WD * w)
    return delta, m, v


def _landed_specs(tr, wd):
    return [pl.BlockSpec((N_DEV, tr, wd), lambda l, i, ll=ll: (0, jnp.where(l == ll, i, 0), 0)) for ll in range(DEPTH)]


def _device_sum(r_ref):
    acc = r_ref[0].astype(F32)
    for d in range(1, N_DEV):
        acc = acc + r_ref[d].astype(F32)
    return acc


def _sum_adamw(recv, w, m, v, tr, name, transposed=False):
    _, r, wd = recv[0].shape

    def body(*refs):
        w_ref, m_ref, v_ref, g_ref, d_ref, mo_ref, vo_ref = refs[DEPTH:]
        for ll in range(DEPTH):
            @pl.when(pl.program_id(0) == ll)
            def _(ll=ll):
                g = _device_sum(refs[ll])
                g = g.T if transposed else g
                g_ref[0] = g
                d_ref[0], mo_ref[0], vo_ref[0] = _adamw(w_ref[0], g, m_ref[0], v_ref[0])

    if transposed:
        blk = pl.BlockSpec((1, wd, tr), lambda l, i: (l, 0, i))
        shape = jax.ShapeDtypeStruct((DEPTH, wd, r), F32)
    else:
        blk = pl.BlockSpec((1, tr, wd), lambda l, i: (l, i, 0))
        shape = jax.ShapeDtypeStruct((DEPTH, r, wd), F32)
    return pl.pallas_call(
        body,
        name=name,
        grid=(DEPTH, r // tr),
        in_specs=_landed_specs(tr, wd) + [blk, blk, blk],
        out_specs=[blk] * 4,
        out_shape=[shape] * 4,
        compiler_params=_params(("arbitrary", "arbitrary")),
    )(*recv, w, m, v)


def _adamw_small(ws, gs, ms, vs):
    n = len(ws)

    def body(*refs):
        w_r, g_r, m_r, v_r = refs[:n], refs[n:2 * n], refs[2 * n:3 * n], refs[3 * n:4 * n]
        d_o, m_o, v_o = refs[4 * n:5 * n], refs[5 * n:6 * n], refs[6 * n:7 * n]
        for t in range(n):
            d_o[t][...], m_o[t][...], v_o[t][...] = _adamw(w_r[t][...], g_r[t][...], m_r[t][...], v_r[t][...])

    vmem = pl.BlockSpec(memory_space=pltpu.VMEM)
    shapes = [jax.ShapeDtypeStruct(w.shape, F32) for w in ws]
    outs = pl.pallas_call(
        body,
        name="adamw_small",
        in_specs=[vmem] * (4 * n),
        out_specs=[vmem] * (3 * n),
        out_shape=shapes * 3,
    )(*ws, *gs, *ms, *vs)
    return outs[:n], outs[n:2 * n], outs[2 * n:]


def kernel(x, meta_tokens, mix_pre_g, w_in, conv_w, sinks, attn_out_g, conv_out_g, w_out, mix_post_g, mlp_pre_g, w_up, w_down, mlp_post_g, loss_target, m_meta_tokens, m_mix_pre_g, m_w_in, m_conv_w, m_sinks, m_attn_out_g, m_conv_out_g, m_w_out, m_mix_post_g, m_mlp_pre_g, m_w_up, m_w_down, m_mlp_post_g, v_meta_tokens, v_mix_pre_g, v_w_in, v_conv_w, v_sinks, v_attn_out_g, v_conv_out_g, v_w_out, v_mix_post_g, v_mlp_pre_g, v_w_up, v_w_down, v_mlp_post_g):
    seq = x.shape[1]
    lp = BLOCK + seq
    tm = _row_tile(lp)
    tm_mlp = _row_tile(lp, (320, 256, 128))
    tm_dw_mlp = _row_tile(lp, (1664, 1040, 640, 384, 256, 128))
    tm_dw_mix = _row_tile(lp, (1664, 832, 640, 384, 256, 128))
    me = 4 * lax.axis_index("x") + 2 * lax.axis_index("y") + lax.axis_index("c")
    cshard = CONV_W // N_DEV
    mshard = D_MODEL // N_DEV

    gather_with = {
        ("in_proj_fwd", 0): [("down", 0)], ("attn_fwd", 0): [("out", 0), ("up", 0)],
        ("mlp_fwd", 0): [("in", 1), ("out", 1), ("up", 1), ("down", 1)],
    }
    scatter_with = {
        ("attn_bwd", 1): [("down", 1)], ("mix_bwd_dw", 1): [("out", 1)], ("mlp_bwd_dx", 0): [("up", 1), ("in", 1)],
        ("mix_out_bwd", 0): [("up", 0)], ("attn_bwd", 0): [("down", 0)], ("mix_bwd_dw", 0): [("out", 0)],
        ("in_proj_bwd_dx", 0): [("in", 0)],
    }
    shard = {"in": jnp.swapaxes(w_in, 1, 2).astype(BF), "out": w_out.astype(BF),
             "up": jnp.swapaxes(w_up, 1, 2).astype(BF), "down": w_down.astype(BF)}
    weight = {}
    grad = {}
    landed = {}

    def run(fn, kind, l, *args):
        key, name = (kind, l), f"{kind}_{l}"
        if key in gather_with:
            blocks = gather_with[key]
            outs, lands = fn(*args, name, _Exchange([(shard[n][k], True) for n, k in blocks]))
            for b, land in zip(blocks, lands):
                weight[b] = land.reshape(-1, D_MODEL)
            return outs
        if key in scatter_with:
            blocks = scatter_with[key]
            outs, lands = fn(*args, name, _Exchange([(grad[b].reshape(N_DEV, -1, D_MODEL), False) for b in blocks]))
            landed.update(zip(blocks, lands))
            return outs
        return fn(*args, name)

    small = jnp.zeros((24, 128), F32)
    small = small.at[0:N_META, :].set(meta_tokens)
    small = small.at[N_META:N_META + 6, 0:cshard].set(conv_w.reshape(6, cshard))
    first = _Exchange([(shard["in"][0], True), (small, True)])
    h, rope, (first_in, g_small) = _build_h(x[0], _rope_table(lp), tm, first, 1, "build_h")
    weight[("in", 0)] = first_in.reshape(-1, D_MODEL)
    cw = g_small[:, N_META:N_META + 6, 0:cshard].reshape(N_DEV, DEPTH, 3, cshard)
    cw = jnp.transpose(cw, (1, 2, 0, 3)).reshape(DEPTH, 3, CONV_W)
    conv_full = jnp.concatenate([cw, jnp.zeros((DEPTH, 5, CONV_W), F32)], axis=1)

    row1 = _LayerRows

    saved = []
    for l in range(DEPTH):
        a, qkv, bch = run(_in_proj_fwd, "in_proj_fwd", l, h, row1(mix_pre_g, l), weight[("in", l)], rope, tm)
        y_attn, probs, p_sink = run(_attn_fwd, "attn_fwd", l, qkv, sinks[l].reshape(1, -1))
        yc, y, z, h2 = run(_mix_out_fwd, "mix_out_fwd", l, bch, y_attn, h, row1(conv_full, l), row1(attn_out_g, l),
                       row1(conv_out_g, l), weight[("out", l)], row1(mix_post_g, l), tm)
        mlp = _mlp_fwd if l < DEPTH - 1 else functools.partial(_mlp_fwd, target=loss_target[0])
        a2, up, f, *rest = run(mlp, "mlp_fwd", l, h2, row1(mlp_pre_g, l), weight[("up", l)], weight[("down", l)],
                               row1(mlp_post_g, l), tm_mlp)
        saved.append((h, a, qkv, bch, y_attn, probs, p_sink, yc, y, z, h2, a2, up, f))
        h = rest[0]
    dh, loss_part = rest[0], rest[1][0, 0] * (0.5 / D_MODEL)

    gsmall = [None] * DEPTH
    for l in reversed(range(DEPTH)):
        h0, a, qkv, bch, y_attn, probs, p_sink, yc, y, z, h2, a2, up, f = saved[l]
        df, dup, dh2, dg_mlp = run(_mlp_bwd_dx, "mlp_bwd_dx", l, dh, f, up, h2, weight[("down", l)], weight[("up", l)],
                                   row1(mlp_post_g, l), row1(mlp_pre_g, l), tm_mlp)
        grad[("down", l)], grad[("up", l)] = _mlp_bwd_dw(up, df, dup, a2, tm_dw_mlp, f"mlp_bwd_dw_{l}")
        dya, dbch, dg_mix, grad[("out", l)] = run(
            _mix_out_bwd, "mix_out_bwd", l, dh2, z, y_attn, yc, bch, y, weight[("out", l)], row1(mix_post_g, l),
            row1(attn_out_g, l), row1(conv_out_g, l), row1(conv_full, l), tm)
        dq, dkv, dsink = run(_attn_bwd, "attn_bwd", l, qkv, y_attn, dya, probs, p_sink, rope)
        grad[("in", l)], = run(_mix_bwd_dw, "mix_bwd_dw", l, dq, dkv, dbch, a, tm_dw_mix)
        dh, dg_in = run(_in_proj_bwd_dx, "in_proj_bwd_dx", l, dq, dkv, dbch, weight[("in", l)], h0, dh2,
                        row1(mix_pre_g, l), tm)
        tile_a =dg_mlp + dg_in + jnp.pad(dsink, ((0, 0), (0, D_MODEL - 128)))
        gsmall[l] = (tile_a, dg_mix)
    grad_x = dh[BLOCK:][None]

    loss_tile = jnp.zeros((8, D_MODEL), F32).at[ROW_LOSS, 0].set(loss_part)
    tot = _sum_small(jnp.concatenate(
        [gsmall[0][0] + loss_tile, gsmall[0][1], gsmall[1][0], gsmall[1][1], dh[LEAD_PAD:BLOCK]], axis=0))
    loss = tot[ROW_LOSS, 0]
    ta = [tot[16 * l:16 * l + 8] for l in range(DEPTH)]
    tb = [tot[16 * l + 8:16 * l + 16] for l in range(DEPTH)]
    pick = lambda tiles, r0, r1, c0, c1: jnp.stack([t[r0:r1, c0:c1] for t in tiles])
    g_mlp_post = pick(ta, ROW_MLP_POST, ROW_MLP_POST + 1, 0, D_MODEL).reshape(DEPTH, D_MODEL)
    g_mlp_pre = pick(ta, ROW_MLP_PRE, ROW_MLP_PRE + 1, 0, D_MODEL).reshape(DEPTH, D_MODEL)
    g_mix_pre = pick(ta, ROW_MIX_PRE, ROW_MIX_PRE + 1, 0, D_MODEL).reshape(DEPTH, D_MODEL)
    g_sinks = pick(ta, ROW_SINK, ROW_SINK + 1, 0, N_Q_HEADS).reshape(DEPTH, N_Q_HEADS)
    g_mix_post = pick(tb, ROW_MIX_POST, ROW_MIX_POST + 1, 0, D_MODEL).reshape(DEPTH, D_MODEL)
    g_attn_out = pick(tb, ROW_GROUP_G, ROW_GROUP_G + 1, 0, ATTN_W).reshape(DEPTH, ATTN_W)
    g_conv_out = pick(tb, ROW_GROUP_G, ROW_GROUP_G + 1, ATTN_W, D_MODEL).reshape(DEPTH, CONV_W)
    g_conv_full = pick(tb, ROW_CONV, ROW_CONV + 3, 0, CONV_W)
    g_conv = lax.dynamic_slice_in_dim(g_conv_full, me * cshard, cshard, axis=2)
    g_meta = lax.dynamic_slice_in_dim(tot[16 * DEPTH:16 * DEPTH + N_META], me * mshard, mshard, axis=1)

    r_in, r_out, r_up, r_down = [[landed[(n, l)] for l in range(DEPTH)] for n in ("in", "out", "up", "down")]
    t12 = lambda a: jnp.swapaxes(a, 1, 2)
    g_w_in, d_w_in, nm_w_in, nv_w_in = map(t12, _sum_adamw(r_in, t12(w_in), t12(m_w_in), t12(v_w_in), 96, "adamw_w_in"))
    g_w_up, d_w_up, nm_w_up, nv_w_up = _sum_adamw(r_up, w_up, m_w_up, v_w_up, 128, "adamw_w_up", transposed=True)
    g_w_out, d_w_out, nm_w_out, nv_w_out = _sum_adamw(r_out, w_out, m_w_out, v_w_out, 128, "adamw_w_out")
    g_w_down, d_w_down, nm_w_down, nv_w_down = _sum_adamw(r_down, w_down, m_w_down, v_w_down, 128, "adamw_w_down")

    ws = [meta_tokens, mix_pre_g, conv_w.reshape(6, cshard), sinks, attn_out_g, conv_out_g, mix_post_g, mlp_pre_g, mlp_post_g]
    gs = [g_meta, g_mix_pre, g_conv.reshape(6, cshard), g_sinks, g_attn_out, g_conv_out, g_mix_post, g_mlp_pre, g_mlp_post]
    ms = [m_meta_tokens, m_mix_pre_g, m_conv_w.reshape(6, cshard), m_sinks, m_attn_out_g, m_conv_out_g, m_mix_post_g,
          m_mlp_pre_g, m_mlp_post_g]
    vs = [v_meta_tokens, v_mix_pre_g, v_conv_w.reshape(6, cshard), v_sinks, v_attn_out_g, v_conv_out_g, v_mix_post_g,
          v_mlp_pre_g, v_mlp_post_g]
    ds, nms, nvs = _adamw_small(ws, gs, ms, vs)

    def order(meta, mix_pre, cv, sk, a_out, c_out, mix_post, mlp_pre, mlp_post, win, wout, wup, wdown):
        return [meta, mix_pre, win, cv.reshape(DEPTH, 3, cshard), sk, a_out, c_out, wout, mix_post, mlp_pre, wup, wdown, mlp_post]

    grads = order(*gs, g_w_in, g_w_out, g_w_up, g_w_down)
    deltas = order(*ds, d_w_in, d_w_out, d_w_up, d_w_down)
    new_m = order(*nms, nm_w_in, nm_w_out, nm_w_up, nm_w_down)
    new_v = order(*nvs, nv_w_in, nv_w_out, nv_w_up, nv_w_down)
    return (loss, grad_x, *grads, *deltas, *new_m, *new_v)
```

```python
import functools
import math

import jax
import jax.numpy as jnp
from jax import lax
from jax.experimental import pallas as pl
from jax.experimental.pallas import tpu as pltpu

F32 = jnp.float32
BF = jnp.bfloat16

D_MODEL = 1024
ATTN_W = 512
CONV_W = 512
HEAD_DIM = 64
N_Q_HEADS = 8
ROT_DIM = 16
D_FF = 4096
IN_W = 2304
N_META = 16
BLOCK = 128
LEAD_PAD = BLOCK - N_META
ROPE_THETA = 500000.0
EPS = 1e-6
N_DEV = 8
DEPTH = 2
NEG = -1e30
SCALE = HEAD_DIM ** -0.5

ADAM_LR = 0.001
ADAM_B1 = 0.9
ADAM_B2 = 0.999
ADAM_EPS = 1e-08
ADAM_WD = 0.01
ADAM_STEP = 10

ROW_MLP_POST, ROW_MLP_PRE, ROW_MIX_PRE, ROW_SINK, ROW_LOSS = 0, 1, 2, 3, 4
ROW_MIX_POST, ROW_GROUP_G, ROW_CONV = 0, 1, 2

VMEM_LIMIT = 56 * 1024 * 1024
MESH = pl.DeviceIdType.MESH


def _dot(a, b):
    return jnp.dot(a, b, preferred_element_type=F32)


def _dot_nt(a, b):
    return lax.dot_general(a, b, (((1,), (1,)), ((), ())), preferred_element_type=F32)


def _dot_tn(a, b):
    return lax.dot_general(a, b, (((0,), (0,)), ((), ())), preferred_element_type=F32)


def _rms_fwd(x, g):
    r = lax.rsqrt(jnp.mean(x * x, axis=-1, keepdims=True) + EPS)
    return x * r * g


def _rms_bwd(x, g, dy):
    r = lax.rsqrt(jnp.mean(x * x, axis=-1, keepdims=True) + EPS)
    xh = x * r
    t = dy * g
    dx = r * (t - xh * jnp.mean(t * xh, axis=-1, keepdims=True))
    dg = jnp.sum(dy * xh, axis=0, keepdims=True)
    return dx, dg


def _row_tile(lp, cands=(640, 512, 384, 256, 128)):
    for t in cands:
        if lp % t == 0:
            return t
    raise ValueError(f"row count {lp} is not a multiple of 128")


def _full(shape):
    n = len(shape)
    return pl.BlockSpec(shape, lambda *_: (0,) * n, pipeline_mode=pl.Buffered(1))


def _full_out(shape):
    n = len(shape)
    return pl.BlockSpec(shape, lambda *_: (0,) * n)


def _params(sem=("arbitrary",)):
    return pltpu.CompilerParams(dimension_semantics=sem, vmem_limit_bytes=VMEM_LIMIT)


def _rope_table(lp):
    half = ROT_DIM // 2
    pos = jnp.maximum(jnp.arange(lp) - LEAD_PAD, 0).astype(F32)
    inv_freq = jnp.power(jnp.float32(ROPE_THETA), -jnp.arange(0, ROT_DIM, 2, dtype=F32) / ROT_DIM)
    ang_t = jnp.concatenate([inv_freq, inv_freq])[:, None] * pos[None, :]
    row = lax.broadcasted_iota(jnp.int32, (ROT_DIM, lp), 0)
    cs_t = jnp.where(row < half, jnp.cos(ang_t), jnp.sin(ang_t))
    return jnp.pad(cs_t.T, ((0, 0), (0, 128 - ROT_DIM)))


def _rope_coeffs(t):
    half = ROT_DIM // 2
    lane = lax.broadcasted_iota(jnp.int32, t.shape, 1)
    cos_a = jnp.where(lane < half, t, 0.0)
    sin_a = pltpu.roll(jnp.where((lane >= half) & (lane < ROT_DIM), t, 0.0), 128 - half, 1)
    c = cos_a + pltpu.roll(cos_a, half, 1) + jnp.where((lane >= ROT_DIM) & (lane < HEAD_DIM), 1.0, 0.0)
    s2 = pltpu.roll(sin_a, half, 1)
    both = lambda u: u + pltpu.roll(u, HEAD_DIM, 1)
    return both(c), both(-sin_a), both(s2)


def _rope(t, c, s1, s2):
    return t * c + pltpu.roll(t, BLOCK - 8, 1) * s1 + pltpu.roll(t, 8, 1) * s2


def _rope_t(dt, c, s1, s2):
    return dt * c + pltpu.roll(dt * s1, 8, 1) + pltpu.roll(dt * s2, BLOCK - 8, 1)


def _build_h(x, rope_compact, tm, exch, small_piece, name):
    seq = x.shape[0]
    lp = BLOCK + seq
    nt = lp // tm
    n_sub = tm // BLOCK
    small_shape = exch.land_shapes[small_piece].shape

    def body(*refs):
        h_ref, c_ref, s1_ref, s2_ref = refs[n_sub + 1:n_sub + 5]
        for j in range(n_sub):
            h_ref[j * BLOCK:(j + 1) * BLOCK, :] = refs[j][...]
        c_ref[...], s1_ref[...], s2_ref[...] = _rope_coeffs(refs[n_sub][...])

    def after(lands, *refs):
        h_ref, buf = refs[n_sub + 1], refs[n_sub + 5]
        pltpu.sync_copy(lands[small_piece], buf)
        h_ref[0:LEAD_PAD, :] = jnp.zeros((LEAD_PAD, D_MODEL), F32)
        for d in range(N_DEV):
            h_ref[LEAD_PAD:BLOCK, d * 128:(d + 1) * 128] = buf[d, 0:N_META, :]

    tile = lambda i: (i + 1) % nt
    piece = lambda j: pl.BlockSpec((BLOCK, D_MODEL), lambda i: (jnp.maximum(tile(i) * n_sub + j - 1, 0), 0))
    rows = lambda w: pl.BlockSpec((tm, w), lambda i: (tile(i), 0))
    (h, *rope), lands = _call(
        body, exch,
        name=name,
        grid=(nt,),
        in_specs=[piece(j) for j in range(n_sub)] + [rows(128)],
        out_specs=[rows(D_MODEL)] + [rows(128)] * 3,
        out_shape=[jax.ShapeDtypeStruct((lp, D_MODEL), F32)] + [jax.ShapeDtypeStruct((lp, 128), F32)] * 3,
        scratch_shapes=[pltpu.VMEM(small_shape, F32)],
        compiler_params=_params(),
        after=after,
    )(*([x] * n_sub), rope_compact)
    return h, rope, lands


def _in_proj_fwd(h, g, w_in_t, rope, tm, name, exch=None):
    lp = h.shape[0]

    def body(h_ref, g_ref, w_ref, c_ref, s1_ref, s2_ref, a_ref, qkv_ref, bch_ref):
        a = _rms_fwd(h_ref[...], g_ref[...]).astype(BF)
        a_ref[...] = a
        proj = _dot_nt(a, w_ref[...])
        c, s1, s2 = c_ref[...], s1_ref[...], s2_ref[...]
        for j in range(5):
            t = _rope(proj[:, j * 128:(j + 1) * 128], c, s1, s2)
            qkv_ref[:, j * 128:(j + 1) * 128] = (t * SCALE if j < 4 else t).astype(BF)
        qkv_ref[:, 640:768] = proj[:, 640:768].astype(BF)
        bch_ref[...] = proj[:, 768:].astype(BF)

    row = lambda w: pl.BlockSpec((tm, w), lambda i: (i, 0))
    return _call(
        body, exch,
        name=name,
        grid=(lp // tm,),
        in_specs=[row(D_MODEL), _full((1, D_MODEL)), _full((IN_W, D_MODEL)), row(128), row(128), row(128)],
        out_specs=[row(D_MODEL), row(768), row(3 * CONV_W)],
        out_shape=[
            jax.ShapeDtypeStruct((lp, D_MODEL), BF),
            jax.ShapeDtypeStruct((lp, 768), BF),
            jax.ShapeDtypeStruct((lp, 3 * CONV_W), BF),
        ],
        compiler_params=_params(),
    )(h, g, w_in_t, *rope)


def _fold_masks(i):
    r = lax.broadcasted_iota(jnp.int32, (2 * BLOCK, BLOCK), 0) & (BLOCK - 1)
    c = lax.broadcasted_iota(jnp.int32, (2 * BLOCK, BLOCK), 1)
    tri = c > r
    ok = jnp.where(tri, (i - 1) * BLOCK + c, i * BLOCK + c) >= LEAD_PAD
    return tri, ok


def _kv_operand(x, kvh):
    lane = lax.broadcasted_iota(jnp.int32, x.shape, 1)
    zero = jnp.zeros_like(x)
    if kvh == 0:
        lo = jnp.where(lane < HEAD_DIM, x, zero)
        hi = pltpu.roll(lo, HEAD_DIM, 1)
    else:
        hi = jnp.where(lane >= HEAD_DIM, x, zero)
        lo = pltpu.roll(hi, HEAD_DIM, 1)
    return jnp.concatenate([lo, hi], axis=0)


def _split4(t, tri):
    zero = jnp.zeros_like(t[0])
    return jnp.concatenate(
        [jnp.where(tri, t[0], zero), jnp.where(tri, zero, t[0]), jnp.where(tri, t[1], zero), jnp.where(tri, zero, t[1])], axis=1)


def _sink_cols(sink_ref, kvh):
    first = lax.broadcasted_iota(jnp.int32, (2 * BLOCK, 1), 0) < BLOCK
    return [jnp.where(first, sink_ref[0, 4 * kvh + half], sink_ref[0, 4 * kvh + 2 + half]) for half in range(2)]


def _folded_exp(q2, k4, tri, ok, sks):
    s = _dot_nt(q2, k4)
    es, ss = [], []
    for half in range(2):
        s_h = s[:, 2 * half * BLOCK:2 * (half + 1) * BLOCK]
        sf = jnp.where(ok, jnp.where(tri, s_h[:, :BLOCK], s_h[:, BLOCK:]), NEG)
        m = jnp.maximum(jnp.max(sf, axis=-1, keepdims=True), sks[half])
        es.append(jnp.exp(sf - m))
        ss.append(jnp.exp(sks[half] - m))
    sums = _dot(jnp.concatenate(es, axis=0).astype(BF), jnp.ones((BLOCK, BLOCK), BF))
    invs = [1.0 / (sums[2 * half * BLOCK:2 * (half + 1) * BLOCK] + ss[half]) for half in range(2)]
    return es, ss, invs


def _attn_fwd(qkv, sink, name, exch=None):
    lp = qkv.shape[0]
    nb = lp // BLOCK
    per_step = 2

    def one_block(i, sink_ref, q_ref, kvc_ref, kvp_ref, o_ref, p_ref, ps_ref):
        tri, ok = _fold_masks(i)
        kvc, kvp = kvc_ref[...], kvp_ref[...]
        kk = jnp.concatenate([kvp[:, :128], kvc[:, :128]], axis=0)
        vv = jnp.concatenate([kvp[:, 128:], kvc[:, 128:]], axis=0)
        lane = lax.broadcasted_iota(jnp.int32, (BLOCK, 128), 1)
        p_sink = jnp.zeros((BLOCK, 128), F32)
        for kvh in range(2):
            q2 = jnp.concatenate([q_ref[:, 256 * kvh:256 * kvh + 128], q_ref[:, 256 * kvh + 128:256 * kvh + 256]], axis=0)
            es, ss, invs = _folded_exp(q2, _kv_operand(kk, kvh), tri, ok, _sink_cols(sink_ref, kvh))
            pb = [(es[half] * invs[half]).astype(BF) for half in range(2)]
            out = _dot(_split4(pb, tri), _kv_operand(vv, kvh))
            for pair in range(2):
                rows = slice(pair * BLOCK, (pair + 1) * BLOCK)
                o_ref[:, 256 * kvh + 128 * pair:256 * kvh + 128 * (pair + 1)] = out[rows].astype(BF)
                for half in range(2):
                    head = 4 * kvh + 2 * pair + half
                    p_ref[:, 128 * head:128 * (head + 1)] = pb[half][rows]
                    p_sink = jnp.where(lane == head, (ss[half] * invs[half][:, 0:1])[rows], p_sink)
        ps_ref[...] = p_sink

    def body(sink_ref, *refs):
        q_refs, kv_refs = refs[:per_step], refs[per_step:2 * per_step + 1]
        o_ref, p_ref, ps_ref = refs[2 * per_step + 1:]
        for j in range(per_step):
            rows = slice(j * BLOCK, (j + 1) * BLOCK)
            one_block(per_step * pl.program_id(0) + j, sink_ref, q_refs[j], kv_refs[j + 1], kv_refs[j],
                      o_ref.at[rows], p_ref.at[rows], ps_ref.at[rows])

    last = nb - 1
    blk = lambda j: (lambda s: jnp.minimum(per_step * s + j, last))
    out_rows = lambda w: pl.BlockSpec((per_step * BLOCK, w), lambda s: (s, 0))
    return _call(
        body, exch,
        name=name,
        grid=(pl.cdiv(nb, per_step),),
        in_specs=[pl.BlockSpec(memory_space=pltpu.SMEM)]
        + [pl.BlockSpec((BLOCK, ATTN_W), lambda s, j=j: (blk(j)(s), 0)) for j in range(per_step)]
        + [pl.BlockSpec((BLOCK, 256), lambda s: (jnp.maximum(per_step * s - 1, 0), 2))]
        + [pl.BlockSpec((BLOCK, 256), lambda s, j=j: (blk(j)(s), 2)) for j in range(per_step)],
        out_specs=[out_rows(ATTN_W), out_rows(N_Q_HEADS * BLOCK), out_rows(128)],
        out_shape=[jax.ShapeDtypeStruct((lp, ATTN_W), BF), jax.ShapeDtypeStruct((lp, N_Q_HEADS * BLOCK), BF),
                   jax.ShapeDtypeStruct((lp, 128), F32)],
        compiler_params=_params(),
    )(sink, *([qkv] * (2 * per_step + 1)))


def _mix_out_fwd(bch, y_attn, h, conv_w, g_a, g_c, w_out, g_post, tm, name, exch=None):
    lp = h.shape[0]

    def body(bch_ref, ya_ref, h_ref, cw_ref, ga_ref, gc_ref, w_ref, gp_ref, yc_ref, y_ref, z_ref, h2_ref, ext):
        i = pl.program_id(0)

        @pl.when(i == 0)
        def _():
            ext[0:8, :] = jnp.zeros((8, CONV_W), F32)

        b = bch_ref[:, 0:CONV_W].astype(F32)
        u = bch_ref[:, CONV_W:2 * CONV_W].astype(F32) * bch_ref[:, 2 * CONV_W:3 * CONV_W].astype(F32)
        ext[8:8 + tm, :] = u
        yc = cw_ref[0:1, :] * ext[6:6 + tm, :] + cw_ref[1:2, :] * ext[7:7 + tm, :] + cw_ref[2:3, :] * u
        ext[0:8, :] = u[tm - 8:tm, :]
        yc_ref[...] = yc.astype(BF)
        ya = _rms_fwd(ya_ref[...].astype(F32), ga_ref[...]).astype(BF)
        yb = _rms_fwd(b * yc, gc_ref[...]).astype(BF)
        y_ref[:, 0:ATTN_W] = ya
        y_ref[:, ATTN_W:] = yb
        z = _dot(ya, w_ref[0:ATTN_W, :]) + _dot(yb, w_ref[ATTN_W:, :])
        z_ref[...] = z.astype(BF)
        h2_ref[...] = h_ref[...] + _rms_fwd(z, gp_ref[...])

    row = lambda w: pl.BlockSpec((tm, w), lambda i: (i, 0))
    return _call(
        body, exch,
        name=name,
        grid=(lp // tm,),
        in_specs=[
            row(3 * CONV_W), row(ATTN_W), row(D_MODEL), _full((8, CONV_W)), _full((1, ATTN_W)), _full((1, CONV_W)),
            _full((D_MODEL, D_MODEL)), _full((1, D_MODEL)),
        ],
        out_specs=[row(CONV_W), row(D_MODEL), row(D_MODEL), row(D_MODEL)],
        out_shape=[
            jax.ShapeDtypeStruct((lp, CONV_W), BF),
            jax.ShapeDtypeStruct((lp, D_MODEL), BF),
            jax.ShapeDtypeStruct((lp, D_MODEL), BF),
            jax.ShapeDtypeStruct((lp, D_MODEL), F32),
        ],
        scratch_shapes=[pltpu.VMEM((tm + 8, CONV_W), F32)],
        compiler_params=_params(),
    )(bch, y_attn, h, conv_w, g_a, g_c, w_out, g_post)


def _mlp_fwd(h2, g_pre, w_up_t, w_down, g_post, tm, name, exch=None, target=None):
    lp = h2.shape[0]
    sub = math.gcd(tm, BLOCK)
    n_sub, lead = tm // sub, BLOCK // sub
    n_t = n_sub if target is not None else 0

    def body(*refs):
        h_ref, gp_ref, wu_ref, wd_ref, gq_ref = refs[:5]
        t_refs = refs[5:5 + n_t]
        a_ref, up_ref, f_ref, last_ref = refs[5 + n_t:9 + n_t]
        h = h_ref[...]
        a = _rms_fwd(h, gp_ref[...]).astype(BF)
        a_ref[...] = a
        up = _dot_nt(a, wu_ref[...])
        up_ref[...] = up.astype(BF)
        act = jnp.square(jnp.maximum(up, 0.0)).astype(BF)
        f = _dot(act, wd_ref[...])
        f_ref[...] = f
        h3 = h + _rms_fwd(f, gq_ref[...])
        if target is None:
            last_ref[...] = h3
            return
        ls_ref = refs[9 + n_t]
        i = pl.program_id(0)

        @pl.when(i == 0)
        def _():
            ls_ref[...] = jnp.zeros((8, 128), F32)

        sq = jnp.zeros((8, D_MODEL), F32)
        for j in range(n_sub):
            on_tokens = i * n_sub + j >= lead
            d = jnp.where(on_tokens, h3[j * sub:(j + 1) * sub] - t_refs[j][...], 0.0)
            last_ref[j * sub:(j + 1) * sub, :] = d * (1.0 / D_MODEL)
            sq = sq + jnp.sum((d * d).reshape(sub // 8, 8, D_MODEL), axis=0)
        ls_ref[...] += sum(sq[:, k * 128:(k + 1) * 128] for k in range(D_MODEL // 128))

        @pl.when(i == lp // tm - 1)
        def _():
            ls_ref[...] = jnp.full((8, 128), jnp.sum(ls_ref[...]), F32)

    row = lambda w: pl.BlockSpec((tm, w), lambda i: (i, 0))
    piece = lambda j: pl.BlockSpec((sub, D_MODEL), lambda i: (jnp.maximum(i * n_sub + j - lead, 0), 0))
    out_specs = [row(D_MODEL), row(D_FF), row(D_MODEL), row(D_MODEL)]
    out_shape = [
        jax.ShapeDtypeStruct((lp, D_MODEL), BF),
        jax.ShapeDtypeStruct((lp, D_FF), BF),
        jax.ShapeDtypeStruct((lp, D_MODEL), F32),
        jax.ShapeDtypeStruct((lp, D_MODEL), F32),
    ]
    if target is not None:
        out_specs.append(_full_out((8, 128)))
        out_shape.append(jax.ShapeDtypeStruct((8, 128), F32))
    return _call(
        body, exch,
        name=name,
        grid=(lp // tm,),
        in_specs=[row(D_MODEL), _full((1, D_MODEL)), _full((D_FF, D_MODEL)), _full((D_FF, D_MODEL)), _full((1, D_MODEL))]
        + [piece(j) for j in range(n_t)],
        out_specs=out_specs,
        out_shape=out_shape,
        compiler_params=_params(),
    )(h2, g_pre, w_up_t, w_down, g_post, *([target] * n_t))


def _mlp_bwd_dx(dh3, f, up, h2, w_down, w_up_t, g_post, g_pre, tm, name, exch=None):
    lp = h2.shape[0]

    def body(dh3_ref, f_ref, up_ref, h2_ref, wd_ref, wu_ref, gq_ref, gp_ref, df_ref, dup_ref, dh2_ref, dg_ref):
        i = pl.program_id(0)

        @pl.when(i == 0)
        def _():
            dg_ref[...] = jnp.zeros((8, D_MODEL), F32)

        dh3 = dh3_ref[...]
        df, dgq = _rms_bwd(f_ref[...], gq_ref[...], dh3)
        dg_ref[ROW_MLP_POST:ROW_MLP_POST + 1, :] += dgq
        df = df.astype(BF)
        df_ref[...] = df
        dact = _dot_nt(df, wd_ref[...])
        dup = (dact * (2.0 * jnp.maximum(up_ref[...].astype(F32), 0.0))).astype(BF)
        dup_ref[...] = dup
        da = _dot(dup, wu_ref[...])
        dh, dgp = _rms_bwd(h2_ref[...], gp_ref[...], da)
        dg_ref[ROW_MLP_PRE:ROW_MLP_PRE + 1, :] += dgp
        dh2_ref[...] = dh3 + dh

    row = lambda w: pl.BlockSpec((tm, w), lambda i: (i, 0))
    return _call(
        body, exch,
        name=name,
        grid=(lp // tm,),
        in_specs=[
            row(D_MODEL), row(D_MODEL), row(D_FF), row(D_MODEL), _full((D_FF, D_MODEL)), _full((D_FF, D_MODEL)),
            _full((1, D_MODEL)), _full((1, D_MODEL)),
        ],
        out_specs=[row(D_MODEL), row(D_FF), row(D_MODEL), _full_out((8, D_MODEL))],
        out_shape=[
            jax.ShapeDtypeStruct((lp, D_MODEL), BF),
            jax.ShapeDtypeStruct((lp, D_FF), BF),
            jax.ShapeDtypeStruct((lp, D_MODEL), F32),
            jax.ShapeDtypeStruct((8, D_MODEL), F32),
        ],
        compiler_params=_params(),
    )(dh3, f, up, h2, w_down, w_up_t, g_post, g_pre)


def _mlp_bwd_dw(up, df, dup, a2, tm, name):
    lp = up.shape[0]
    nt = lp // tm
    nj = D_FF // D_MODEL

    def body(up_ref, df_ref, dup_ref, a_ref, dwd_ref, dwu_ref, accd, accu):
        i = pl.program_id(1)

        @pl.when(i == 0)
        def _():
            accd[...] = jnp.zeros_like(accd)
            accu[...] = jnp.zeros_like(accu)

        act = jnp.square(jnp.maximum(up_ref[...].astype(F32), 0.0)).astype(BF)
        accd[...] += _dot_tn(act, df_ref[...])
        accu[...] += _dot_tn(dup_ref[...], a_ref[...])

        @pl.when(i == nt - 1)
        def _():
            dwd_ref[...] = accd[...].astype(BF)
            dwu_ref[...] = accu[...].astype(BF)

    return pl.pallas_call(
        body,
        name=name,
        grid=(nj, nt),
        in_specs=[
            pl.BlockSpec((tm, D_MODEL), lambda j, i: (i, j)),
            pl.BlockSpec((tm, D_MODEL), lambda j, i: (i, 0)),
            pl.BlockSpec((tm, D_MODEL), lambda j, i: (i, j)),
            pl.BlockSpec((tm, D_MODEL), lambda j, i: (i, 0)),
        ],
        out_specs=[pl.BlockSpec((D_MODEL, D_MODEL), lambda j, i: (j, 0)), pl.BlockSpec((D_MODEL, D_MODEL), lambda j, i: (j, 0))],
        out_shape=[jax.ShapeDtypeStruct((D_FF, D_MODEL), BF), jax.ShapeDtypeStruct((D_FF, D_MODEL), BF)],
        scratch_shapes=[pltpu.VMEM((D_MODEL, D_MODEL), F32), pltpu.VMEM((D_MODEL, D_MODEL), F32)],
        compiler_params=_params(("arbitrary", "arbitrary")),
    )(up, df, dup, a2)


def _mix_out_bwd(dh2, z, y_attn, yc, bch, y, w_out, g_post, g_a, g_c, conv_w, tm, name, exch=None):
    lp = dh2.shape[0]
    nt = lp // tm

    def body(dh2_ref, z_ref, ya_ref, yc_ref, bch_ref, y_ref, w_ref, gp_ref, ga_ref, gc_ref, cw_ref,
             dya_ref, dbch_ref, dg_ref, dwo_ref, ext, acco):
        i = pl.program_id(0)
        dcw_ref = dg_ref.at[ROW_CONV:ROW_CONV + 3, 0:CONV_W]

        @pl.when(i == 0)
        def _():
            ext[tm:tm + 8, :] = jnp.zeros((8, CONV_W), F32)
            dg_ref[...] = jnp.zeros((8, D_MODEL), F32)
            acco[...] = jnp.zeros_like(acco)

        dz, dgp = _rms_bwd(z_ref[...].astype(F32), gp_ref[...], dh2_ref[...])
        dg_ref[ROW_MIX_POST:ROW_MIX_POST + 1, :] += dgp
        dz = dz.astype(BF)
        acco[...] += _dot_tn(y_ref[...], dz)
        dya_n = _dot_nt(dz, w_ref[0:ATTN_W, :])
        dyb_n = _dot_nt(dz, w_ref[ATTN_W:, :])
        dya, dga = _rms_bwd(ya_ref[...].astype(F32), ga_ref[...], dya_n)
        dg_ref[ROW_GROUP_G:ROW_GROUP_G + 1, 0:ATTN_W] += dga
        dya_ref[...] = dya
        b = bch_ref[:, 0:CONV_W].astype(F32)
        c = bch_ref[:, CONV_W:2 * CONV_W].astype(F32)
        hc = bch_ref[:, 2 * CONV_W:3 * CONV_W].astype(F32)
        u = c * hc
        yc_v = yc_ref[...].astype(F32)
        dyconv, dgc = _rms_bwd(b * yc_v, gc_ref[...], dyb_n)
        dg_ref[ROW_GROUP_G:ROW_GROUP_G + 1, ATTN_W:] += dgc
        dbch_ref[:, 0:CONV_W] = (dyconv * yc_v).astype(BF)
        dyc = dyconv * b
        ext[0:tm, :] = dyc
        d1 = ext[1:1 + tm, :]
        d2 = ext[2:2 + tm, :]
        du = cw_ref[2:3, :] * dyc + cw_ref[1:2, :] * d1 + cw_ref[0:1, :] * d2
        ext[tm:tm + 8, :] = dyc[0:8, :]
        dbch_ref[:, CONV_W:2 * CONV_W] = (du * hc).astype(BF)
        dbch_ref[:, 2 * CONV_W:3 * CONV_W] = (du * c).astype(BF)
        dcw_ref[0:1, :] += jnp.sum(u * d2, axis=0, keepdims=True)
        dcw_ref[1:2, :] += jnp.sum(u * d1, axis=0, keepdims=True)
        dcw_ref[2:3, :] += jnp.sum(u * dyc, axis=0, keepdims=True)

        @pl.when(i == nt - 1)
        def _():
            dwo_ref[...] = acco[...].astype(BF)

    row = lambda w: pl.BlockSpec((tm, w), lambda i: (nt - 1 - i, 0))
    return _call(
        body, exch,
        name=name,
        grid=(nt,),
        in_specs=[
            row(D_MODEL), row(D_MODEL), row(ATTN_W), row(CONV_W), row(3 * CONV_W), row(D_MODEL), _full((D_MODEL, D_MODEL)),
            _full((1, D_MODEL)), _full((1, ATTN_W)), _full((1, CONV_W)), _full((8, CONV_W)),
        ],
        out_specs=[row(ATTN_W), row(3 * CONV_W), _full_out((8, D_MODEL)), _full_out((D_MODEL, D_MODEL))],
        out_shape=[
            jax.ShapeDtypeStruct((lp, ATTN_W), F32),
            jax.ShapeDtypeStruct((lp, 3 * CONV_W), BF),
            jax.ShapeDtypeStruct((8, D_MODEL), F32),
            jax.ShapeDtypeStruct((D_MODEL, D_MODEL), BF),
        ],
        scratch_shapes=[pltpu.VMEM((tm + 8, CONV_W), F32), pltpu.VMEM((D_MODEL, D_MODEL), F32)],
        compiler_params=_params(),
    )(dh2, z, y_attn, yc, bch, y, w_out, g_post, g_a, g_c, conv_w)


def _attn_bwd(qkv, o, do, probs, p_sink, rope, name, exch=None):
    lp = qkv.shape[0]
    nb = lp // BLOCK

    def body(q_ref, kvc_ref, kvp_ref, o_ref, do_ref, p_ref, ps_ref, cq_ref, s1q_ref, s2q_ref, ck_ref, s1k_ref, s2k_ref,
             dq_ref, dkv_ref, dsink_ref, carry):
        i = pl.program_id(0)

        @pl.when(i == 0)
        def _():
            carry[...] = jnp.zeros_like(carry)
            dsink_ref[...] = jnp.zeros((8, 128), F32)

        def finish(tot):
            dk = _rope_t(tot[:, :128], ck_ref[...], s1k_ref[...], s2k_ref[...])
            dkv_ref[:, 0:128] = dk.astype(BF)
            dkv_ref[:, 128:256] = tot[:, 128:].astype(BF)

        @pl.when(i < nb)
        def _():
            tri, _ = _fold_masks(i)
            kvc, kvp = kvc_ref[...], kvp_ref[...]
            kk = jnp.concatenate([kvp[:, :128], kvc[:, :128]], axis=0)
            vv = jnp.concatenate([kvp[:, 128:], kvc[:, 128:]], axis=0)
            lane = lax.broadcasted_iota(jnp.int32, (BLOCK, 128), 1)
            lane2 = lax.broadcasted_iota(jnp.int32, (2 * BLOCK, 128), 1)
            rope_q = (cq_ref[...], s1q_ref[...], s2q_ref[...])
            deltas = jnp.zeros((BLOCK, 128), F32)
            folded = []
            for kvh in range(2):
                c0 = 256 * kvh
                q2 = jnp.concatenate([q_ref[:, c0:c0 + 128], q_ref[:, c0 + 128:c0 + 256]], axis=0)
                do2 = jnp.concatenate([do_ref[:, c0:c0 + 128], do_ref[:, c0 + 128:c0 + 256]], axis=0)
                o2 = jnp.concatenate([o_ref[:, c0:c0 + 128], o_ref[:, c0 + 128:c0 + 256]], axis=0).astype(F32)
                k4, v4 = _kv_operand(kk, kvh), _kv_operand(vv, kvh)
                prod = do2 * o2
                dob = do2.astype(BF)
                dp = _dot_nt(dob, v4)
                ds, pb = [], []
                for half in range(2):
                    heads = [4 * kvh + 2 * pair + half for pair in range(2)]
                    p = jnp.concatenate([p_ref[:, 128 * h:128 * (h + 1)] for h in heads], axis=0)
                    sel = (lane2 < HEAD_DIM) if half == 0 else (lane2 >= HEAD_DIM)
                    delta = jnp.sum(jnp.where(sel, prod, 0.0), axis=-1, keepdims=True)
                    dp_h = dp[:, 2 * half * BLOCK:2 * (half + 1) * BLOCK]
                    ds.append((p.astype(F32) * (jnp.where(tri, dp_h[:, :BLOCK], dp_h[:, BLOCK:]) - delta)).astype(BF))
                    pb.append(p)
                    for pair in range(2):
                        deltas = jnp.where(lane == heads[pair], delta[pair * BLOCK:(pair + 1) * BLOCK], deltas)
                ds4, p4 = _split4(ds, tri), _split4(pb, tri)
                dq2 = _dot(ds4, k4) * SCALE
                dq_ref[:, c0:c0 + 128] = _rope_t(dq2[:BLOCK], *rope_q).astype(BF)
                dq_ref[:, c0 + 128:c0 + 256] = _rope_t(dq2[BLOCK:], *rope_q).astype(BF)
                rk, rv = _dot_tn(ds4, q2), _dot_tn(p4, dob)
                own = (lane < HEAD_DIM) if kvh == 0 else (lane >= HEAD_DIM)
                group = []
                for r in (rk, rv):
                    for blk in range(2):
                        t = jnp.where(lane < HEAD_DIM, r[blk * BLOCK:(blk + 1) * BLOCK], r[(2 + blk) * BLOCK:(3 + blk) * BLOCK])
                        group.append(jnp.where(own, t + pltpu.roll(t, HEAD_DIM, 1), 0.0))
                folded.append(group)
            dsink_ref[ROW_SINK:ROW_SINK + 1, :] -= jnp.sum(ps_ref[...] * deltas, axis=0, keepdims=True)
            dk_p, dk_c, dv_p, dv_c = [folded[0][t] + folded[1][t] for t in range(4)]
            finish(carry[...] + jnp.concatenate([dk_p, dv_p], axis=1))
            carry[...] = jnp.concatenate([dk_c, dv_c], axis=1)

        @pl.when(i == nb)
        def _():
            finish(carry[...])

    qi = lambda i: jnp.minimum(i, nb - 1)
    ki = lambda i: jnp.maximum(i - 1, 0)
    tab_q = pl.BlockSpec((BLOCK, 128), lambda i: (qi(i), 0))
    tab_k = pl.BlockSpec((BLOCK, 128), lambda i: (ki(i), 0))
    return _call(
        body, exch,
        name=name,
        grid=(nb + 1,),
        in_specs=[
            pl.BlockSpec((BLOCK, ATTN_W), lambda i: (qi(i), 0)),
            pl.BlockSpec((BLOCK, 256), lambda i: (qi(i), 2)),
            pl.BlockSpec((BLOCK, 256), lambda i: (jnp.maximum(qi(i) - 1, 0), 2)),
            pl.BlockSpec((BLOCK, ATTN_W), lambda i: (qi(i), 0)),
            pl.BlockSpec((BLOCK, ATTN_W), lambda i: (qi(i), 0)),
            pl.BlockSpec((BLOCK, N_Q_HEADS * BLOCK), lambda i: (qi(i), 0)),
            tab_q, tab_q, tab_q, tab_q, tab_k, tab_k, tab_k,
        ],
        out_specs=[
            pl.BlockSpec((BLOCK, ATTN_W), lambda i: (qi(i), 0)),
            pl.BlockSpec((BLOCK, 256), lambda i: (ki(i), 0)),
            pl.BlockSpec((8, 128), lambda i: (0, 0)),
        ],
        out_shape=[
            jax.ShapeDtypeStruct((lp, ATTN_W), BF),
            jax.ShapeDtypeStruct((lp, 256), BF),
            jax.ShapeDtypeStruct((8, 128), F32),
        ],
        scratch_shapes=[pltpu.VMEM((BLOCK, 256), F32)],
        compiler_params=_params(),
    )(qkv, qkv, qkv, o, do, probs, p_sink, *rope, *rope)


def _in_proj_bwd_dx(dq, dkv, dbch, w_in_t, h, dh2, g, tm, name, exch=None):
    lp = h.shape[0]

    def body(dq_ref, dkv_ref, dbch_ref, w_ref, h_ref, dh2_ref, g_ref, dh_ref, dg_ref):
        i = pl.program_id(0)

        @pl.when(i == 0)
        def _():
            dg_ref[...] = jnp.zeros((8, D_MODEL), F32)

        da = _dot(jnp.concatenate([dq_ref[...], dkv_ref[...], dbch_ref[...]], axis=1), w_ref[...])
        dh, dg = _rms_bwd(h_ref[...], g_ref[...], da)
        dg_ref[ROW_MIX_PRE:ROW_MIX_PRE + 1, :] += dg
        dh_ref[...] = dh2_ref[...] + dh

    row = lambda w: pl.BlockSpec((tm, w), lambda i: (i, 0))
    return _call(
        body, exch,
        name=name,
        grid=(lp // tm,),
        in_specs=[row(ATTN_W), row(256), row(3 * CONV_W), _full((IN_W, D_MODEL)), row(D_MODEL), row(D_MODEL), _full((1, D_MODEL))],
        out_specs=[row(D_MODEL), _full_out((8, D_MODEL))],
        out_shape=[jax.ShapeDtypeStruct((lp, D_MODEL), F32), jax.ShapeDtypeStruct((8, D_MODEL), F32)],
        compiler_params=_params(),
    )(dq, dkv, dbch, w_in_t, h, dh2, g)


def _mix_bwd_dw(dq, dkv, dbch, a, tm, name, exch=None):
    lp = a.shape[0]
    nt = lp // tm

    def body(dq_ref, dkv_ref, dbch_ref, a_ref, dwi_ref, acci):
        i = pl.program_id(0)

        @pl.when(i == 0)
        def _():
            acci[...] = jnp.zeros_like(acci)

        a_v = a_ref[...]
        acci[0:512, :] += _dot_tn(dq_ref[...], a_v)
        acci[512:768, :] += _dot_tn(dkv_ref[...], a_v)
        acci[768:, :] += _dot_tn(dbch_ref[...], a_v)

        @pl.when(i == nt - 1)
        def _():
            dwi_ref[...] = acci[...].astype(BF)

    row = lambda w: pl.BlockSpec((tm, w), lambda i: (i, 0))
    return _call(
        body, exch,
        name=name,
        grid=(nt,),
        in_specs=[row(ATTN_W), row(256), row(3 * CONV_W), row(D_MODEL)],
        out_specs=[_full_out((IN_W, D_MODEL))],
        out_shape=[jax.ShapeDtypeStruct((IN_W, D_MODEL), BF)],
        scratch_shapes=[pltpu.VMEM((IN_W, D_MODEL), F32)],
        compiler_params=_params(),
    )(dq, dkv, dbch, a)


def _mesh_place():
    x, y, c = lax.axis_index("x"), lax.axis_index("y"), lax.axis_index("c")
    return x, y, c, 4 * x + 2 * y + c


def _peer(x, y, c, k):
    px = 1 - x if k & 4 else x
    py = 1 - y if k & 2 else y
    pc = 1 - c if k & 1 else c
    return (px, py, pc), 4 * px + 2 * py + pc


SIBLING = 1
SAME_CORE = (2, 4, 6)
OTHER_CORE = (3, 5, 7)


class _Exchange:
    def __init__(self, pieces):
        self.srcs = [s for s, _ in pieces]
        self.to_all = [g for _, g in pieces]
        self.n = len(pieces)
        self.land_shapes = [
            jax.ShapeDtypeStruct((N_DEV,) + (s.shape if g else s.shape[1:]), s.dtype) for s, g in pieces]
        self.sem_shapes = [pltpu.SemaphoreType.DMA((self.n, N_DEV - 1)), pltpu.SemaphoreType.DMA((self.n, N_DEV - 1)),
                           pltpu.SemaphoreType.DMA((self.n,))]
        self.forwards = any(self.to_all)

    def _ops(self, srcs, lands, sems):
        send_sems, recv_sems, local_sems = sems
        x, y, c, me = _mesh_place()

        def remote(p, k, src, slot, to):
            return pltpu.make_async_remote_copy(
                src_ref=src, dst_ref=lands[p].at[slot], send_sem=send_sems.at[p, k - 1], recv_sem=recv_sems.at[p, k - 1],
                device_id=to, device_id_type=MESH)

        def own(p):
            return pltpu.make_async_copy(srcs[p] if self.to_all[p] else srcs[p].at[me], lands[p].at[me], local_sems.at[p])

        def direct(p, k):
            peer, pidx = _peer(x, y, c, k)
            return remote(p, k, srcs[p] if self.to_all[p] else srcs[p].at[pidx], me, peer)

        def forward(p, k):
            sibling, _ = _peer(x, y, c, SIBLING)
            _, origin = _peer(x, y, c, k ^ SIBLING)
            return remote(p, k, lands[p].at[origin], origin, sibling)

        def arrival(p, k):
            peer, pidx = _peer(x, y, c, k)
            return remote(p, k, lands[p].at[pidx], pidx, peer)

        return own, direct, forward, arrival

    def start(self, srcs, lands, sems):
        own, direct, _, _ = self._ops(srcs, lands, sems)
        for p in range(self.n):
            own(p).start()
            for k in ((SIBLING,) + SAME_CORE) if self.to_all[p] else range(1, N_DEV):
                direct(p, k).start()

    def forward(self, srcs, lands, sems):
        _, _, forward, arrival = self._ops(srcs, lands, sems)
        for p in range(self.n):
            if self.to_all[p]:
                for k in SAME_CORE:
                    arrival(p, k).wait_recv()
                    forward(p, k ^ SIBLING).start()

    def finish(self, srcs, lands, sems):
        own, direct, forward, arrival = self._ops(srcs, lands, sems)
        for p in range(self.n):
            for k in ((SIBLING,) + OTHER_CORE) if self.to_all[p] else range(1, N_DEV):
                arrival(p, k).wait_recv()
        for p in range(self.n):
            for k in range(1, N_DEV):
                (forward(p, k) if self.to_all[p] and k in OTHER_CORE else direct(p, k)).wait_send()
            own(p).wait()


class _LayerRows:
    def __init__(self, array, layer):
        self.array = array if array.ndim == 3 else array.reshape(DEPTH, 1, -1)
        self.layer = layer

    def spec(self):
        layer = self.layer
        return pl.BlockSpec((None,) + self.array.shape[1:], lambda *_: (layer, 0, 0), pipeline_mode=pl.Buffered(1))


def _call(body, exch, *, name, grid, in_specs, out_specs, out_shape, scratch_shapes=(), compiler_params, after=None):
    def with_layer_rows(args):
        specs = [a.spec() if isinstance(a, _LayerRows) else s for s, a in zip(in_specs, args)]
        return specs, [a.array if isinstance(a, _LayerRows) else a for a in args]

    if exch is None:
        def plain(*args):
            specs, args = with_layer_rows(args)
            return pl.pallas_call(body, name=name, grid=grid, in_specs=specs, out_specs=out_specs, out_shape=out_shape,
                                  scratch_shapes=scratch_shapes, compiler_params=compiler_params)(*args)
        return plain
    n_in, n_out, n_scr, n_x = len(in_specs), len(out_shape), len(scratch_shapes), exch.n
    steps = math.prod(grid)

    def carrying(*refs):
        a, b, c, d, e = n_in, n_in + n_x, n_in + n_x + n_out, n_in + 2 * n_x + n_out, n_in + 2 * n_x + n_out + n_scr
        ins, srcs, outs, lands, scr, sems = refs[:a], refs[a:b], refs[b:c], refs[c:d], refs[d:e], refs[e:]
        step = functools.reduce(lambda acc, t: acc * grid[t] + pl.program_id(t), range(len(grid)), 0)

        @pl.when(step == 0)
        def _():
            exch.start(srcs, lands, sems)

        body(*ins, *outs, *scr)

        if exch.forwards:
            @pl.when(step == max(0, steps - 1 - (steps + 15) // 16))
            def _():
                exch.forward(srcs, lands, sems)

        @pl.when(step == steps - 1)
        def _():
            exch.finish(srcs, lands, sems)
            if after is not None:
                after(lands, *ins, *outs, *scr)

    hbm = pl.BlockSpec(memory_space=pl.ANY)

    def run(*args):
        specs, args = with_layer_rows(args)
        res = pl.pallas_call(
            carrying, name=name, grid=grid, in_specs=specs + [hbm] * n_x, out_specs=list(out_specs) + [hbm] * n_x,
            out_shape=list(out_shape) + exch.land_shapes, scratch_shapes=list(scratch_shapes) + exch.sem_shapes,
            compiler_params=compiler_params)(*args, *exch.srcs)
        return list(res[:n_out]), list(res[n_out:])

    return run


def _sum_small(part):
    exch = _Exchange([(part, True)])

    def body(part_ref, out_ref, land, *sems):
        exch.start([part_ref], [land], sems)
        exch.forward([part_ref], [land], sems)
        exch.finish([part_ref], [land], sems)
        acc = land[0]
        for d in range(1, N_DEV):
            acc = acc + land[d]
        out_ref[...] = acc

    vmem = pl.BlockSpec(memory_space=pltpu.VMEM)
    return pl.pallas_call(
        body,
        name="sum_small",
        in_specs=[vmem],
        out_specs=vmem,
        out_shape=jax.ShapeDtypeStruct(part.shape, F32),
        scratch_shapes=[pltpu.VMEM(exch.land_shapes[0].shape, F32)] + exch.sem_shapes,
    )(part)


def _adamw(w, g, m, v):
    m = ADAM_B1 * m + (1.0 - ADAM_B1) * g
    v = ADAM_B2 * v + (1.0 - ADAM_B2) * jnp.square(g)
    m_hat = m / (1.0 - ADAM_B1 ** ADAM_STEP)
    v_hat = v / (1.0 - ADAM_B2 ** ADAM_STEP)
    delta = -ADAM_LR * (m_hat / (jnp.sqrt(v_hat) + ADAM_EPS) + ADAM_WD * w)
    return delta, m, v


def _landed_specs(tr, wd):
    return [pl.BlockSpec((N_DEV, tr, wd), lambda l, i, ll=ll: (0, jnp.where(l == ll, i, 0), 0)) for ll in range(DEPTH)]


def _device_sum(r_ref):
    acc = r_ref[0].astype(F32)
    for d in range(1, N_DEV):
        acc = acc + r_ref[d].astype(F32)
    return acc


def _sum_adamw(recv, w, m, v, tr, name, transposed=False):
    _, r, wd = recv[0].shape

    def body(*refs):
        w_ref, m_ref, v_ref, g_ref, d_ref, mo_ref, vo_ref = refs[DEPTH:]
        for ll in range(DEPTH):
            @pl.when(pl.program_id(0) == ll)
            def _(ll=ll):
                g = _device_sum(refs[ll])
                g = g.T if transposed else g
                g_ref[0] = g
                d_ref[0], mo_ref[0], vo_ref[0] = _adamw(w_ref[0], g, m_ref[0], v_ref[0])

    if transposed:
        blk = pl.BlockSpec((1, wd, tr), lambda l, i: (l, 0, i))
        shape = jax.ShapeDtypeStruct((DEPTH, wd, r), F32)
    else:
        blk = pl.BlockSpec((1, tr, wd), lambda l, i: (l, i, 0))
        shape = jax.ShapeDtypeStruct((DEPTH, r, wd), F32)
    return pl.pallas_call(
        body,
        name=name,
        grid=(DEPTH, r // tr),
        in_specs=_landed_specs(tr, wd) + [blk, blk, blk],
        out_specs=[blk] * 4,
        out_shape=[shape] * 4,
        compiler_params=_params(("arbitrary", "arbitrary")),
    )(*recv, w, m, v)


def _adamw_small(ws, gs, ms, vs):
    n = len(ws)

    def body(*refs):
        w_r, g_r, m_r, v_r = refs[:n], refs[n:2 * n], refs[2 * n:3 * n], refs[3 * n:4 * n]
        d_o, m_o, v_o = refs[4 * n:5 * n], refs[5 * n:6 * n], refs[6 * n:7 * n]
        for t in range(n):
            d_o[t][...], m_o[t][...], v_o[t][...] = _adamw(w_r[t][...], g_r[t][...], m_r[t][...], v_r[t][...])

    vmem = pl.BlockSpec(memory_space=pltpu.VMEM)
    shapes = [jax.ShapeDtypeStruct(w.shape, F32) for w in ws]
    outs = pl.pallas_call(
        body,
        name="adamw_small",
        in_specs=[vmem] * (4 * n),
        out_specs=[vmem] * (3 * n),
        out_shape=shapes * 3,
    )(*ws, *gs, *ms, *vs)
    return outs[:n], outs[n:2 * n], outs[2 * n:]


def kernel(x, meta_tokens, mix_pre_g, w_in, conv_w, sinks, attn_out_g, conv_out_g, w_out, mix_post_g, mlp_pre_g, w_up, w_down, mlp_post_g, loss_target, m_meta_tokens, m_mix_pre_g, m_w_in, m_conv_w, m_sinks, m_attn_out_g, m_conv_out_g, m_w_out, m_mix_post_g, m_mlp_pre_g, m_w_up, m_w_down, m_mlp_post_g, v_meta_tokens, v_mix_pre_g, v_w_in, v_conv_w, v_sinks, v_attn_out_g, v_conv_out_g, v_w_out, v_mix_post_g, v_mlp_pre_g, v_w_up, v_w_down, v_mlp_post_g):
    seq = x.shape[1]
    lp = BLOCK + seq
    tm = _row_tile(lp)
    tm_mlp = _row_tile(lp, (320, 256, 128))
    tm_dw_mlp = _row_tile(lp, (1664, 1040, 640, 384, 256, 128))
    tm_dw_mix = _row_tile(lp, (1664, 832, 640, 384, 256, 128))
    me = 4 * lax.axis_index("x") + 2 * lax.axis_index("y") + lax.axis_index("c")
    cshard = CONV_W // N_DEV
    mshard = D_MODEL // N_DEV

    gather_with = {
        ("in_proj_fwd", 0): [("down", 0)], ("attn_fwd", 0): [("out", 0), ("up", 0)],
        ("mlp_fwd", 0): [("in", 1), ("out", 1), ("up", 1), ("down", 1)],
    }
    scatter_with = {
        ("attn_bwd", 1): [("down", 1)], ("mix_bwd_dw", 1): [("out", 1)], ("mlp_bwd_dx", 0): [("up", 1), ("in", 1)],
        ("mix_out_bwd", 0): [("up", 0)], ("attn_bwd", 0): [("down", 0)], ("mix_bwd_dw", 0): [("out", 0)],
        ("in_proj_bwd_dx", 0): [("in", 0)],
    }
    shard = {"in": jnp.swapaxes(w_in, 1, 2).astype(BF), "out": w_out.astype(BF),
             "up": jnp.swapaxes(w_up, 1, 2).astype(BF), "down": w_down.astype(BF)}
    weight = {}
    grad = {}
    landed = {}

    def run(fn, kind, l, *args):
        key, name = (kind, l), f"{kind}_{l}"
        if key in gather_with:
            blocks = gather_with[key]
            outs, lands = fn(*args, name, _Exchange([(shard[n][k], True) for n, k in blocks]))
            for b, land in zip(blocks, lands):
                weight[b] = land.reshape(-1, D_MODEL)
            return outs
        if key in scatter_with:
            blocks = scatter_with[key]
            outs, lands = fn(*args, name, _Exchange([(grad[b].reshape(N_DEV, -1, D_MODEL), False) for b in blocks]))
            landed.update(zip(blocks, lands))
            return outs
        return fn(*args, name)

    small = jnp.zeros((24, 128), F32)
    small = small.at[0:N_META, :].set(meta_tokens)
    small = small.at[N_META:N_META + 6, 0:cshard].set(conv_w.reshape(6, cshard))
    first = _Exchange([(shard["in"][0], True), (small, True)])
    h, rope, (first_in, g_small) = _build_h(x[0], _rope_table(lp), tm, first, 1, "build_h")
    weight[("in", 0)] = first_in.reshape(-1, D_MODEL)
    cw = g_small[:, N_META:N_META + 6, 0:cshard].reshape(N_DEV, DEPTH, 3, cshard)
    cw = jnp.transpose(cw, (1, 2, 0, 3)).reshape(DEPTH, 3, CONV_W)
    conv_full = jnp.concatenate([cw, jnp.zeros((DEPTH, 5, CONV_W), F32)], axis=1)

    row1 = _LayerRows

    saved = []
    for l in range(DEPTH):
        a, qkv, bch = run(_in_proj_fwd, "in_proj_fwd", l, h, row1(mix_pre_g, l), weight[("in", l)], rope, tm)
        y_attn, probs, p_sink = run(_attn_fwd, "attn_fwd", l, qkv, sinks[l].reshape(1, -1))
        yc, y, z, h2 = run(_mix_out_fwd, "mix_out_fwd", l, bch, y_attn, h, row1(conv_full, l), row1(attn_out_g, l),
                       row1(conv_out_g, l), weight[("out", l)], row1(mix_post_g, l), tm)
        mlp = _mlp_fwd if l < DEPTH - 1 else functools.partial(_mlp_fwd, target=loss_target[0])
        a2, up, f, *rest = run(mlp, "mlp_fwd", l, h2, row1(mlp_pre_g, l), weight[("up", l)], weight[("down", l)],
                               row1(mlp_post_g, l), tm_mlp)
        saved.append((h, a, qkv, bch, y_attn, probs, p_sink, yc, y, z, h2, a2, up, f))
        h = rest[0]
    dh, loss_part = rest[0], rest[1][0, 0] * (0.5 / D_MODEL)

    gsmall = [None] * DEPTH
    for l in reversed(range(DEPTH)):
        h0, a, qkv, bch, y_attn, probs, p_sink, yc, y, z, h2, a2, up, f = saved[l]
        df, dup, dh2, dg_mlp = run(_mlp_bwd_dx, "mlp_bwd_dx", l, dh, f, up, h2, weight[("down", l)], weight[("up", l)],
                                   row1(mlp_post_g, l), row1(mlp_pre_g, l), tm_mlp)
        grad[("down", l)], grad[("up", l)] = _mlp_bwd_dw(up, df, dup, a2, tm_dw_mlp, f"mlp_bwd_dw_{l}")
        dya, dbch, dg_mix, grad[("out", l)] = run(
            _mix_out_bwd, "mix_out_bwd", l, dh2, z, y_attn, yc, bch, y, weight[("out", l)], row1(mix_post_g, l),
            row1(attn_out_g, l), row1(conv_out_g, l), row1(conv_full, l), tm)
        dq, dkv, dsink = run(_attn_bwd, "attn_bwd", l, qkv, y_attn, dya, probs, p_sink, rope)
        grad[("in", l)], = run(_mix_bwd_dw, "mix_bwd_dw", l, dq, dkv, dbch, a, tm_dw_mix)
        dh, dg_in = run(_in_proj_bwd_dx, "in_proj_bwd_dx", l, dq, dkv, dbch, weight[("in", l)], h0, dh2,
                        row1(mix_pre_g, l), tm)
        tile_a =dg_mlp + dg_in + jnp.pad(dsink, ((0, 0), (0, D_MODEL - 128)))
        gsmall[l] = (tile_a, dg_mix)
    grad_x = dh[BLOCK:][None]

    loss_tile = jnp.zeros((8, D_MODEL), F32).at[ROW_LOSS, 0].set(loss_part)
    tot = _sum_small(jnp.concatenate(
        [gsmall[0][0] + loss_tile, gsmall[0][1], gsmall[1][0], gsmall[1][1], dh[LEAD_PAD:BLOCK]], axis=0))
    loss = tot[ROW_LOSS, 0]
    ta = [tot[16 * l:16 * l + 8] for l in range(DEPTH)]
    tb = [tot[16 * l + 8:16 * l + 16] for l in range(DEPTH)]
    pick = lambda tiles, r0, r1, c0, c1: jnp.stack([t[r0:r1, c0:c1] for t in tiles])
    g_mlp_post = pick(ta, ROW_MLP_POST, ROW_MLP_POST + 1, 0, D_MODEL).reshape(DEPTH, D_MODEL)
    g_mlp_pre = pick(ta, ROW_MLP_PRE, ROW_MLP_PRE + 1, 0, D_MODEL).reshape(DEPTH, D_MODEL)
    g_mix_pre = pick(ta, ROW_MIX_PRE, ROW_MIX_PRE + 1, 0, D_MODEL).reshape(DEPTH, D_MODEL)
    g_sinks = pick(ta, ROW_SINK, ROW_SINK + 1, 0, N_Q_HEADS).reshape(DEPTH, N_Q_HEADS)
    g_mix_post = pick(tb, ROW_MIX_POST, ROW_MIX_POST + 1, 0, D_MODEL).reshape(DEPTH, D_MODEL)
    g_attn_out = pick(tb, ROW_GROUP_G, ROW_GROUP_G + 1, 0, ATTN_W).reshape(DEPTH, ATTN_W)
    g_conv_out = pick(tb, ROW_GROUP_G, ROW_GROUP_G + 1, ATTN_W, D_MODEL).reshape(DEPTH, CONV_W)
    g_conv_full = pick(tb, ROW_CONV, ROW_CONV + 3, 0, CONV_W)
    g_conv = lax.dynamic_slice_in_dim(g_conv_full, me * cshard, cshard, axis=2)
    g_meta = lax.dynamic_slice_in_dim(tot[16 * DEPTH:16 * DEPTH + N_META], me * mshard, mshard, axis=1)

    r_in, r_out, r_up, r_down = [[landed[(n, l)] for l in range(DEPTH)] for n in ("in", "out", "up", "down")]
    t12 = lambda a: jnp.swapaxes(a, 1, 2)
    g_w_in, d_w_in, nm_w_in, nv_w_in = map(t12, _sum_adamw(r_in, t12(w_in), t12(m_w_in), t12(v_w_in), 96, "adamw_w_in"))
    g_w_up, d_w_up, nm_w_up, nv_w_up = _sum_adamw(r_up, w_up, m_w_up, v_w_up, 128, "adamw_w_up", transposed=True)
    g_w_out, d_w_out, nm_w_out, nv_w_out = _sum_adamw(r_out, w_out, m_w_out, v_w_out, 128, "adamw_w_out")
    g_w_down, d_w_down, nm_w_down, nv_w_down = _sum_adamw(r_down, w_down, m_w_down, v_w_down, 128, "adamw_w_down")

    ws = [meta_tokens, mix_pre_g, conv_w.reshape(6, cshard), sinks, attn_out_g, conv_out_g, mix_post_g, mlp_pre_g, mlp_post_g]
    gs = [g_meta, g_mix_pre, g_conv.reshape(6, cshard), g_sinks, g_attn_out, g_conv_out, g_mix_post, g_mlp_pre, g_mlp_post]
    ms = [m_meta_tokens, m_mix_pre_g, m_conv_w.reshape(6, cshard), m_sinks, m_attn_out_g, m_conv_out_g, m_mix_post_g,
          m_mlp_pre_g, m_mlp_post_g]
    vs = [v_meta_tokens, v_mix_pre_g, v_conv_w.reshape(6, cshard), v_sinks, v_attn_out_g, v_conv_out_g, v_mix_post_g,
          v_mlp_pre_g, v_mlp_post_g]
    ds, nms, nvs = _adamw_small(ws, gs, ms, vs)

    def order(meta, mix_pre, cv, sk, a_out, c_out, mix_post, mlp_pre, mlp_post, win, wout, wup, wdown):
        return [meta, mix_pre, win, cv.reshape(DEPTH, 3, cshard), sk, a_out, c_out, wout, mix_post, mlp_pre, wup, wdown, mlp_post]

    grads = order(*gs, g_w_in, g_w_out, g_w_up, g_w_down)
    deltas = order(*ds, d_w_in, d_w_out, d_w_up, d_w_down)
    new_m = order(*nms, nm_w_in, nm_w_out, nm_w_up, nm_w_down)
    new_v = order(*nvs, nv_w_in, nv_w_out, nv_w_up, nv_w_down)
    return (loss, grad_x, *grads, *deltas, *new_m, *new_v)
```

```python
import functools
import math

import jax
import jax.numpy as jnp
from jax import lax
from jax.experimental import pallas as pl
from jax.experimental.pallas import tpu as pltpu

F32 = jnp.float32
BF = jnp.bfloat16

D_MODEL = 1024
ATTN_W = 512
CONV_W = 512
HEAD_DIM = 64
N_Q_HEADS = 8
ROT_DIM = 16
D_FF = 4096
IN_W = 2304
N_META = 16
BLOCK = 128
LEAD_PAD = BLOCK - N_META
ROPE_THETA = 500000.0
EPS = 1e-6
N_DEV = 8
DEPTH = 2
NEG = -1e30
SCALE = HEAD_DIM ** -0.5

ADAM_LR = 0.001
ADAM_B1 = 0.9
ADAM_B2 = 0.999
ADAM_EPS = 1e-08
ADAM_WD = 0.01
ADAM_STEP = 10

ROW_MLP_POST, ROW_MLP_PRE, ROW_MIX_PRE, ROW_SINK, ROW_LOSS = 0, 1, 2, 3, 4
ROW_MIX_POST, ROW_GROUP_G, ROW_CONV = 0, 1, 2

VMEM_LIMIT = 56 * 1024 * 1024
MESH = pl.DeviceIdType.MESH


def _dot(a, b):
    return jnp.dot(a, b, preferred_element_type=F32)


def _dot_nt(a, b):
    return lax.dot_general(a, b, (((1,), (1,)), ((), ())), preferred_element_type=F32)


def _dot_tn(a, b):
    return lax.dot_general(a, b, (((0,), (0,)), ((), ())), preferred_element_type=F32)


def _rms_fwd(x, g):
    r = lax.rsqrt(jnp.mean(x * x, axis=-1, keepdims=True) + EPS)
    return x * r * g


def _rms_bwd(x, g, dy):
    r = lax.rsqrt(jnp.mean(x * x, axis=-1, keepdims=True) + EPS)
    xh = x * r
    t = dy * g
    dx = r * (t - xh * jnp.mean(t * xh, axis=-1, keepdims=True))
    dg = jnp.sum(dy * xh, axis=0, keepdims=True)
    return dx, dg


def _row_tile(lp, cands=(640, 512, 384, 256, 128)):
    for t in cands:
        if lp % t == 0:
            return t
    raise ValueError(f"row count {lp} is not a multiple of 128")


def _full(shape):
    n = len(shape)
    return pl.BlockSpec(shape, lambda *_: (0,) * n, pipeline_mode=pl.Buffered(1))


def _full_out(shape):
    n = len(shape)
    return pl.BlockSpec(shape, lambda *_: (0,) * n)


def _params(sem=("arbitrary",)):
    return pltpu.CompilerParams(dimension_semantics=sem, vmem_limit_bytes=VMEM_LIMIT)


def _rope_table(lp):
    half = ROT_DIM // 2
    pos = jnp.maximum(jnp.arange(lp) - LEAD_PAD, 0).astype(F32)
    inv_freq = jnp.power(jnp.float32(ROPE_THETA), -jnp.arange(0, ROT_DIM, 2, dtype=F32) / ROT_DIM)
    ang_t = jnp.concatenate([inv_freq, inv_freq])[:, None] * pos[None, :]
    row = lax.broadcasted_iota(jnp.int32, (ROT_DIM, lp), 0)
    cs_t = jnp.where(row < half, jnp.cos(ang_t), jnp.sin(ang_t))
    return jnp.pad(cs_t.T, ((0, 0), (0, 128 - ROT_DIM)))


def _rope_coeffs(t):
    half = ROT_DIM // 2
    lane = lax.broadcasted_iota(jnp.int32, t.shape, 1)
    cos_a = jnp.where(lane < half, t, 0.0)
    sin_a = pltpu.roll(jnp.where((lane >= half) & (lane < ROT_DIM), t, 0.0), 128 - half, 1)
    c = cos_a + pltpu.roll(cos_a, half, 1) + jnp.where((lane >= ROT_DIM) & (lane < HEAD_DIM), 1.0, 0.0)
    s2 = pltpu.roll(sin_a, half, 1)
    both = lambda u: u + pltpu.roll(u, HEAD_DIM, 1)
    return both(c), both(-sin_a), both(s2)


def _rope(t, c, s1, s2):
    return t * c + pltpu.roll(t, BLOCK - 8, 1) * s1 + pltpu.roll(t, 8, 1) * s2


def _rope_t(dt, c, s1, s2):
    return dt * c + pltpu.roll(dt * s1, 8, 1) + pltpu.roll(dt * s2, BLOCK - 8, 1)


def _build_h(x, rope_compact, tm, exch, small_piece, name):
    seq = x.shape[0]
    lp = BLOCK + seq
    nt = lp // tm
    n_sub = tm // BLOCK
    small_shape = exch.land_shapes[small_piece].shape

    def body(*refs):
        h_ref, c_ref, s1_ref, s2_ref = refs[n_sub + 1:n_sub + 5]
        for j in range(n_sub):
            h_ref[j * BLOCK:(j + 1) * BLOCK, :] = refs[j][...]
        c_ref[...], s1_ref[...], s2_ref[...] = _rope_coeffs(refs[n_sub][...])

    def after(lands, *refs):
        h_ref, buf = refs[n_sub + 1], refs[n_sub + 5]
        pltpu.sync_copy(lands[small_piece], buf)
        h_ref[0:LEAD_PAD, :] = jnp.zeros((LEAD_PAD, D_MODEL), F32)
        for d in range(N_DEV):
            h_ref[LEAD_PAD:BLOCK, d * 128:(d + 1) * 128] = buf[d, 0:N_META, :]

    tile = lambda i: (i + 1) % nt
    piece = lambda j: pl.BlockSpec((BLOCK, D_MODEL), lambda i: (jnp.maximum(tile(i) * n_sub + j - 1, 0), 0))
    rows = lambda w: pl.BlockSpec((tm, w), lambda i: (tile(i), 0))
    (h, *rope), lands = _call(
        body, exch,
        name=name,
        grid=(nt,),
        in_specs=[piece(j) for j in range(n_sub)] + [rows(128)],
        out_specs=[rows(D_MODEL)] + [rows(128)] * 3,
        out_shape=[jax.ShapeDtypeStruct((lp, D_MODEL), F32)] + [jax.ShapeDtypeStruct((lp, 128), F32)] * 3,
        scratch_shapes=[pltpu.VMEM(small_shape, F32)],
        compiler_params=_params(),
        after=after,
    )(*([x] * n_sub), rope_compact)
    return h, rope, lands


def _in_proj_fwd(h, g, w_in_t, rope, tm, name, exch=None):
    lp = h.shape[0]

    def body(h_ref, g_ref, w_ref, c_ref, s1_ref, s2_ref, a_ref, qkv_ref, bch_ref):
        a = _rms_fwd(h_ref[...], g_ref[...]).astype(BF)
        a_ref[...] = a
        proj = _dot_nt(a, w_ref[...])
        c, s1, s2 = c_ref[...], s1_ref[...], s2_ref[...]
        for j in range(5):
            t = _rope(proj[:, j * 128:(j + 1) * 128], c, s1, s2)
            qkv_ref[:, j * 128:(j + 1) * 128] = (t * SCALE if j < 4 else t).astype(BF)
        qkv_ref[:, 640:768] = proj[:, 640:768].astype(BF)
        bch_ref[...] = proj[:, 768:].astype(BF)

    row = lambda w: pl.BlockSpec((tm, w), lambda i: (i, 0))
    return _call(
        body, exch,
        name=name,
        grid=(lp // tm,),
        in_specs=[row(D_MODEL), _full((1, D_MODEL)), _full((IN_W, D_MODEL)), row(128), row(128), row(128)],
        out_specs=[row(D_MODEL), row(768), row(3 * CONV_W)],
        out_shape=[
            jax.ShapeDtypeStruct((lp, D_MODEL), BF),
            jax.ShapeDtypeStruct((lp, 768), BF),
            jax.ShapeDtypeStruct((lp, 3 * CONV_W), BF),
        ],
        compiler_params=_params(),
    )(h, g, w_in_t, *rope)


def _fold_masks(i):
    r = lax.broadcasted_iota(jnp.int32, (2 * BLOCK, BLOCK), 0) & (BLOCK - 1)
    c = lax.broadcasted_iota(jnp.int32, (2 * BLOCK, BLOCK), 1)
    tri = c > r
    ok = jnp.where(tri, (i - 1) * BLOCK + c, i * BLOCK + c) >= LEAD_PAD
    return tri, ok


def _kv_operand(x, kvh):
    lane = lax.broadcasted_iota(jnp.int32, x.shape, 1)
    zero = jnp.zeros_like(x)
    if kvh == 0:
        lo = jnp.where(lane < HEAD_DIM, x, zero)
        hi = pltpu.roll(lo, HEAD_DIM, 1)
    else:
        hi = jnp.where(lane >= HEAD_DIM, x, zero)
        lo = pltpu.roll(hi, HEAD_DIM, 1)
    return jnp.concatenate([lo, hi], axis=0)


def _split4(t, tri):
    zero = jnp.zeros_like(t[0])
    return jnp.concatenate(
        [jnp.where(tri, t[0], zero), jnp.where(tri, zero, t[0]), jnp.where(tri, t[1], zero), jnp.where(tri, zero, t[1])], axis=1)


def _sink_cols(sink_ref, kvh):
    first = lax.broadcasted_iota(jnp.int32, (2 * BLOCK, 1), 0) < BLOCK
    return [jnp.where(first, sink_ref[0, 4 * kvh + half], sink_ref[0, 4 * kvh + 2 + half]) for half in range(2)]


def _folded_exp(q2, k4, tri, ok, sks):
    s = _dot_nt(q2, k4)
    es, ss = [], []
    for half in range(2):
        s_h = s[:, 2 * half * BLOCK:2 * (half + 1) * BLOCK]
        sf = jnp.where(ok, jnp.where(tri, s_h[:, :BLOCK], s_h[:, BLOCK:]), NEG)
        m = jnp.maximum(jnp.max(sf, axis=-1, keepdims=True), sks[half])
        es.append(jnp.exp(sf - m))
        ss.append(jnp.exp(sks[half] - m))
    sums = _dot(jnp.concatenate(es, axis=0).astype(BF), jnp.ones((BLOCK, BLOCK), BF))
    invs = [1.0 / (sums[2 * half * BLOCK:2 * (half + 1) * BLOCK] + ss[half]) for half in range(2)]
    return es, ss, invs


def _attn_fwd(qkv, sink, name, exch=None):
    lp = qkv.shape[0]
    nb = lp // BLOCK
    per_step = 2

    def one_block(i, sink_ref, q_ref, kvc_ref, kvp_ref, o_ref, p_ref, ps_ref):
        tri, ok = _fold_masks(i)
        kvc, kvp = kvc_ref[...], kvp_ref[...]
        kk = jnp.concatenate([kvp[:, :128], kvc[:, :128]], axis=0)
        vv = jnp.concatenate([kvp[:, 128:], kvc[:, 128:]], axis=0)
        lane = lax.broadcasted_iota(jnp.int32, (BLOCK, 128), 1)
        p_sink = jnp.zeros((BLOCK, 128), F32)
        for kvh in range(2):
            q2 = jnp.concatenate([q_ref[:, 256 * kvh:256 * kvh + 128], q_ref[:, 256 * kvh + 128:256 * kvh + 256]], axis=0)
            es, ss, invs = _folded_exp(q2, _kv_operand(kk, kvh), tri, ok, _sink_cols(sink_ref, kvh))
            pb = [(es[half] * invs[half]).astype(BF) for half in range(2)]
            out = _dot(_split4(pb, tri), _kv_operand(vv, kvh))
            for pair in range(2):
                rows = slice(pair * BLOCK, (pair + 1) * BLOCK)
                o_ref[:, 256 * kvh + 128 * pair:256 * kvh + 128 * (pair + 1)] = out[rows].astype(BF)
                for half in range(2):
                    head = 4 * kvh + 2 * pair + half
                    p_ref[:, 128 * head:128 * (head + 1)] = pb[half][rows]
                    p_sink = jnp.where(lane == head, (ss[half] * invs[half][:, 0:1])[rows], p_sink)
        ps_ref[...] = p_sink

    def body(sink_ref, *refs):
        q_refs, kv_refs = refs[:per_step], refs[per_step:2 * per_step + 1]
        o_ref, p_ref, ps_ref = refs[2 * per_step + 1:]
        for j in range(per_step):
            rows = slice(j * BLOCK, (j + 1) * BLOCK)
            one_block(per_step * pl.program_id(0) + j, sink_ref, q_refs[j], kv_refs[j + 1], kv_refs[j],
                      o_ref.at[rows], p_ref.at[rows], ps_ref.at[rows])

    last = nb - 1
    blk = lambda j: (lambda s: jnp.minimum(per_step * s + j, last))
    out_rows = lambda w: pl.BlockSpec((per_step * BLOCK, w), lambda s: (s, 0))
    return _call(
        body, exch,
        name=name,
        grid=(pl.cdiv(nb, per_step),),
        in_specs=[pl.BlockSpec(memory_space=pltpu.SMEM)]
        + [pl.BlockSpec((BLOCK, ATTN_W), lambda s, j=j: (blk(j)(s), 0)) for j in range(per_step)]
        + [pl.BlockSpec((BLOCK, 256), lambda s: (jnp.maximum(per_step * s - 1, 0), 2))]
        + [pl.BlockSpec((BLOCK, 256), lambda s, j=j: (blk(j)(s), 2)) for j in range(per_step)],
        out_specs=[out_rows(ATTN_W), out_rows(N_Q_HEADS * BLOCK), out_rows(128)],
        out_shape=[jax.ShapeDtypeStruct((lp, ATTN_W), BF), jax.ShapeDtypeStruct((lp, N_Q_HEADS * BLOCK), BF),
                   jax.ShapeDtypeStruct((lp, 128), F32)],
        compiler_params=_params(),
    )(sink, *([qkv] * (2 * per_step + 1)))


def _mix_out_fwd(bch, y_attn, h, conv_w, g_a, g_c, w_out, g_post, tm, name, exch=None):
    lp = h.shape[0]

    def body(bch_ref, ya_ref, h_ref, cw_ref, ga_ref, gc_ref, w_ref, gp_ref, yc_ref, y_ref, z_ref, h2_ref, ext):
        i = pl.program_id(0)

        @pl.when(i == 0)
        def _():
            ext[0:8, :] = jnp.zeros((8, CONV_W), F32)

        b = bch_ref[:, 0:CONV_W].astype(F32)
        u = bch_ref[:, CONV_W:2 * CONV_W].astype(F32) * bch_ref[:, 2 * CONV_W:3 * CONV_W].astype(F32)
        ext[8:8 + tm, :] = u
        yc = cw_ref[0:1, :] * ext[6:6 + tm, :] + cw_ref[1:2, :] * ext[7:7 + tm, :] + cw_ref[2:3, :] * u
        ext[0:8, :] = u[tm - 8:tm, :]
        yc_ref[...] = yc.astype(BF)
        ya = _rms_fwd(ya_ref[...].astype(F32), ga_ref[...]).astype(BF)
        yb = _rms_fwd(b * yc, gc_ref[...]).astype(BF)
        y_ref[:, 0:ATTN_W] = ya
        y_ref[:, ATTN_W:] = yb
        z = _dot(ya, w_ref[0:ATTN_W, :]) + _dot(yb, w_ref[ATTN_W:, :])
        z_ref[...] = z.astype(BF)
        h2_ref[...] = h_ref[...] + _rms_fwd(z, gp_ref[...])

    row = lambda w: pl.BlockSpec((tm, w), lambda i: (i, 0))
    return _call(
        body, exch,
        name=name,
        grid=(lp // tm,),
        in_specs=[
            row(3 * CONV_W), row(ATTN_W), row(D_MODEL), _full((8, CONV_W)), _full((1, ATTN_W)), _full((1, CONV_W)),
            _full((D_MODEL, D_MODEL)), _full((1, D_MODEL)),
        ],
        out_specs=[row(CONV_W), row(D_MODEL), row(D_MODEL), row(D_MODEL)],
        out_shape=[
            jax.ShapeDtypeStruct((lp, CONV_W), BF),
            jax.ShapeDtypeStruct((lp, D_MODEL), BF),
            jax.ShapeDtypeStruct((lp, D_MODEL), BF),
            jax.ShapeDtypeStruct((lp, D_MODEL), F32),
        ],
        scratch_shapes=[pltpu.VMEM((tm + 8, CONV_W), F32)],
        compiler_params=_params(),
    )(bch, y_attn, h, conv_w, g_a, g_c, w_out, g_post)


def _mlp_fwd(h2, g_pre, w_up_t, w_down, g_post, tm, name, exch=None, target=None):
    lp = h2.shape[0]
    sub = math.gcd(tm, BLOCK)
    n_sub, lead = tm // sub, BLOCK // sub
    n_t = n_sub if target is not None else 0

    def body(*refs):
        h_ref, gp_ref, wu_ref, wd_ref, gq_ref = refs[:5]
        t_refs = refs[5:5 + n_t]
        a_ref, up_ref, f_ref, last_ref = refs[5 + n_t:9 + n_t]
        h = h_ref[...]
        a = _rms_fwd(h, gp_ref[...]).astype(BF)
        a_ref[...] = a
        up = _dot_nt(a, wu_ref[...])
        up_ref[...] = up.astype(BF)
        act = jnp.square(jnp.maximum(up, 0.0)).astype(BF)
        f = _dot(act, wd_ref[...])
        f_ref[...] = f
        h3 = h + _rms_fwd(f, gq_ref[...])
        if target is None:
            last_ref[...] = h3
            return
        ls_ref = refs[9 + n_t]
        i = pl.program_id(0)

        @pl.when(i == 0)
        def _():
            ls_ref[...] = jnp.zeros((8, 128), F32)

        sq = jnp.zeros((8, D_MODEL), F32)
        for j in range(n_sub):
            on_tokens = i * n_sub + j >= lead
            d = jnp.where(on_tokens, h3[j * sub:(j + 1) * sub] - t_refs[j][...], 0.0)
            last_ref[j * sub:(j + 1) * sub, :] = d * (1.0 / D_MODEL)
            sq = sq + jnp.sum((d * d).reshape(sub // 8, 8, D_MODEL), axis=0)
        ls_ref[...] += sum(sq[:, k * 128:(k + 1) * 128] for k in range(D_MODEL // 128))

        @pl.when(i == lp // tm - 1)
        def _():
            ls_ref[...] = jnp.full((8, 128), jnp.sum(ls_ref[...]), F32)

    row = lambda w: pl.BlockSpec((tm, w), lambda i: (i, 0))
    piece = lambda j: pl.BlockSpec((sub, D_MODEL), lambda i: (jnp.maximum(i * n_sub + j - lead, 0), 0))
    out_specs = [row(D_MODEL), row(D_FF), row(D_MODEL), row(D_MODEL)]
    out_shape = [
        jax.ShapeDtypeStruct((lp, D_MODEL), BF),
        jax.ShapeDtypeStruct((lp, D_FF), BF),
        jax.ShapeDtypeStruct((lp, D_MODEL), F32),
        jax.ShapeDtypeStruct((lp, D_MODEL), F32),
    ]
    if target is not None:
        out_specs.append(_full_out((8, 128)))
        out_shape.append(jax.ShapeDtypeStruct((8, 128), F32))
    return _call(
        body, exch,
        name=name,
        grid=(lp // tm,),
        in_specs=[row(D_MODEL), _full((1, D_MODEL)), _full((D_FF, D_MODEL)), _full((D_FF, D_MODEL)), _full((1, D_MODEL))]
        + [piece(j) for j in range(n_t)],
        out_specs=out_specs,
        out_shape=out_shape,
        compiler_params=_params(),
    )(h2, g_pre, w_up_t, w_down, g_post, *([target] * n_t))


def _mlp_bwd_dx(dh3, f, up, h2, w_down, w_up_t, g_post, g_pre, tm, name, exch=None):
    lp = h2.shape[0]

    def body(dh3_ref, f_ref, up_ref, h2_ref, wd_ref, wu_ref, gq_ref, gp_ref, df_ref, dup_ref, dh2_ref, dg_ref):
        i = pl.program_id(0)

        @pl.when(i == 0)
        def _():
            dg_ref[...] = jnp.zeros((8, D_MODEL), F32)

        dh3 = dh3_ref[...]
        df, dgq = _rms_bwd(f_ref[...], gq_ref[...], dh3)
        dg_ref[ROW_MLP_POST:ROW_MLP_POST + 1, :] += dgq
        df = df.astype(BF)
        df_ref[...] = df
        dact = _dot_nt(df, wd_ref[...])
        dup = (dact * (2.0 * jnp.maximum(up_ref[...].astype(F32), 0.0))).astype(BF)
        dup_ref[...] = dup
        da = _dot(dup, wu_ref[...])
        dh, dgp = _rms_bwd(h2_ref[...], gp_ref[...], da)
        dg_ref[ROW_MLP_PRE:ROW_MLP_PRE + 1, :] += dgp
        dh2_ref[...] = dh3 + dh

    row = lambda w: pl.BlockSpec((tm, w), lambda i: (i, 0))
    return _call(
        body, exch,
        name=name,
        grid=(lp // tm,),
        in_specs=[
            row(D_MODEL), row(D_MODEL), row(D_FF), row(D_MODEL), _full((D_FF, D_MODEL)), _full((D_FF, D_MODEL)),
            _full((1, D_MODEL)), _full((1, D_MODEL)),
        ],
        out_specs=[row(D_MODEL), row(D_FF), row(D_MODEL), _full_out((8, D_MODEL))],
        out_shape=[
            jax.ShapeDtypeStruct((lp, D_MODEL), BF),
            jax.ShapeDtypeStruct((lp, D_FF), BF),
            jax.ShapeDtypeStruct((lp, D_MODEL), F32),
            jax.ShapeDtypeStruct((8, D_MODEL), F32),
        ],
        compiler_params=_params(),
    )(dh3, f, up, h2, w_down, w_up_t, g_post, g_pre)


def _mlp_bwd_dw(up, df, dup, a2, tm, name):
    lp = up.shape[0]
    nt = lp // tm
    nj = D_FF // D_MODEL

    def body(up_ref, df_ref, dup_ref, a_ref, dwd_ref, dwu_ref, accd, accu):
        i = pl.program_id(1)

        @pl.when(i == 0)
        def _():
            accd[...] = jnp.zeros_like(accd)
            accu[...] = jnp.zeros_like(accu)

        act = jnp.square(jnp.maximum(up_ref[...].astype(F32), 0.0)).astype(BF)
        accd[...] += _dot_tn(act, df_ref[...])
        accu[...] += _dot_tn(dup_ref[...], a_ref[...])

        @pl.when(i == nt - 1)
        def _():
            dwd_ref[...] = accd[...].astype(BF)
            dwu_ref[...] = accu[...].astype(BF)

    return pl.pallas_call(
        body,
        name=name,
        grid=(nj, nt),
        in_specs=[
            pl.BlockSpec((tm, D_MODEL), lambda j, i: (i, j)),
            pl.BlockSpec((tm, D_MODEL), lambda j, i: (i, 0)),
            pl.BlockSpec((tm, D_MODEL), lambda j, i: (i, j)),
            pl.BlockSpec((tm, D_MODEL), lambda j, i: (i, 0)),
        ],
        out_specs=[pl.BlockSpec((D_MODEL, D_MODEL), lambda j, i: (j, 0)), pl.BlockSpec((D_MODEL, D_MODEL), lambda j, i: (j, 0))],
        out_shape=[jax.ShapeDtypeStruct((D_FF, D_MODEL), BF), jax.ShapeDtypeStruct((D_FF, D_MODEL), BF)],
        scratch_shapes=[pltpu.VMEM((D_MODEL, D_MODEL), F32), pltpu.VMEM((D_MODEL, D_MODEL), F32)],
        compiler_params=_params(("arbitrary", "arbitrary")),
    )(up, df, dup, a2)


def _mix_out_bwd(dh2, z, y_attn, yc, bch, y, w_out, g_post, g_a, g_c, conv_w, tm, name, exch=None):
    lp = dh2.shape[0]
    nt = lp // tm

    def body(dh2_ref, z_ref, ya_ref, yc_ref, bch_ref, y_ref, w_ref, gp_ref, ga_ref, gc_ref, cw_ref,
             dya_ref, dbch_ref, dg_ref, dwo_ref, ext, acco):
        i = pl.program_id(0)
        dcw_ref = dg_ref.at[ROW_CONV:ROW_CONV + 3, 0:CONV_W]

        @pl.when(i == 0)
        def _():
            ext[tm:tm + 8, :] = jnp.zeros((8, CONV_W), F32)
            dg_ref[...] = jnp.zeros((8, D_MODEL), F32)
            acco[...] = jnp.zeros_like(acco)

        dz, dgp = _rms_bwd(z_ref[...].astype(F32), gp_ref[...], dh2_ref[...])
        dg_ref[ROW_MIX_POST:ROW_MIX_POST + 1, :] += dgp
        dz = dz.astype(BF)
        acco[...] += _dot_tn(y_ref[...], dz)
        dya_n = _dot_nt(dz, w_ref[0:ATTN_W, :])
        dyb_n = _dot_nt(dz, w_ref[ATTN_W:, :])
        dya, dga = _rms_bwd(ya_ref[...].astype(F32), ga_ref[...], dya_n)
        dg_ref[ROW_GROUP_G:ROW_GROUP_G + 1, 0:ATTN_W] += dga
        dya_ref[...] = dya
        b = bch_ref[:, 0:CONV_W].astype(F32)
        c = bch_ref[:, CONV_W:2 * CONV_W].astype(F32)
        hc = bch_ref[:, 2 * CONV_W:3 * CONV_W].astype(F32)
        u = c * hc
        yc_v = yc_ref[...].astype(F32)
        dyconv, dgc = _rms_bwd(b * yc_v, gc_ref[...], dyb_n)
        dg_ref[ROW_GROUP_G:ROW_GROUP_G + 1, ATTN_W:] += dgc
        dbch_ref[:, 0:CONV_W] = (dyconv * yc_v).astype(BF)
        dyc = dyconv * b
        ext[0:tm, :] = dyc
        d1 = ext[1:1 + tm, :]
        d2 = ext[2:2 + tm, :]
        du = cw_ref[2:3, :] * dyc + cw_ref[1:2, :] * d1 + cw_ref[0:1, :] * d2
        ext[tm:tm + 8, :] = dyc[0:8, :]
        dbch_ref[:, CONV_W:2 * CONV_W] = (du * hc).astype(BF)
        dbch_ref[:, 2 * CONV_W:3 * CONV_W] = (du * c).astype(BF)
        dcw_ref[0:1, :] += jnp.sum(u * d2, axis=0, keepdims=True)
        dcw_ref[1:2, :] += jnp.sum(u * d1, axis=0, keepdims=True)
        dcw_ref[2:3, :] += jnp.sum(u * dyc, axis=0, keepdims=True)

        @pl.when(i == nt - 1)
        def _():
            dwo_ref[...] = acco[...].astype(BF)

    row = lambda w: pl.BlockSpec((tm, w), lambda i: (nt - 1 - i, 0))
    return _call(
        body, exch,
        name=name,
        grid=(nt,),
        in_specs=[
            row(D_MODEL), row(D_MODEL), row(ATTN_W), row(CONV_W), row(3 * CONV_W), row(D_MODEL), _full((D_MODEL, D_MODEL)),
            _full((1, D_MODEL)), _full((1, ATTN_W)), _full((1, CONV_W)), _full((8, CONV_W)),
        ],
        out_specs=[row(ATTN_W), row(3 * CONV_W), _full_out((8, D_MODEL)), _full_out((D_MODEL, D_MODEL))],
        out_shape=[
            jax.ShapeDtypeStruct((lp, ATTN_W), F32),
            jax.ShapeDtypeStruct((lp, 3 * CONV_W), BF),
            jax.ShapeDtypeStruct((8, D_MODEL), F32),
            jax.ShapeDtypeStruct((D_MODEL, D_MODEL), BF),
        ],
        scratch_shapes=[pltpu.VMEM((tm + 8, CONV_W), F32), pltpu.VMEM((D_MODEL, D_MODEL), F32)],
        compiler_params=_params(),
    )(dh2, z, y_attn, yc, bch, y, w_out, g_post, g_a, g_c, conv_w)


def _attn_bwd(qkv, o, do, probs, p_sink, rope, name, exch=None):
    lp = qkv.shape[0]
    nb = lp // BLOCK

    def body(q_ref, kvc_ref, kvp_ref, o_ref, do_ref, p_ref, ps_ref, cq_ref, s1q_ref, s2q_ref, ck_ref, s1k_ref, s2k_ref,
             dq_ref, dkv_ref, dsink_ref, carry):
        i = pl.program_id(0)

        @pl.when(i == 0)
        def _():
            carry[...] = jnp.zeros_like(carry)
            dsink_ref[...] = jnp.zeros((8, 128), F32)

        def finish(tot):
            dk = _rope_t(tot[:, :128], ck_ref[...], s1k_ref[...], s2k_ref[...])
            dkv_ref[:, 0:128] = dk.astype(BF)
            dkv_ref[:, 128:256] = tot[:, 128:].astype(BF)

        @pl.when(i < nb)
        def _():
            tri, _ = _fold_masks(i)
            kvc, kvp = kvc_ref[...], kvp_ref[...]
            kk = jnp.concatenate([kvp[:, :128], kvc[:, :128]], axis=0)
            vv = jnp.concatenate([kvp[:, 128:], kvc[:, 128:]], axis=0)
            lane = lax.broadcasted_iota(jnp.int32, (BLOCK, 128), 1)
            lane2 = lax.broadcasted_iota(jnp.int32, (2 * BLOCK, 128), 1)
            rope_q = (cq_ref[...], s1q_ref[...], s2q_ref[...])
            deltas = jnp.zeros((BLOCK, 128), F32)
            folded = []
            for kvh in range(2):
                c0 = 256 * kvh
                q2 = jnp.concatenate([q_ref[:, c0:c0 + 128], q_ref[:, c0 + 128:c0 + 256]], axis=0)
                do2 = jnp.concatenate([do_ref[:, c0:c0 + 128], do_ref[:, c0 + 128:c0 + 256]], axis=0)
                o2 = jnp.concatenate([o_ref[:, c0:c0 + 128], o_ref[:, c0 + 128:c0 + 256]], axis=0).astype(F32)
                k4, v4 = _kv_operand(kk, kvh), _kv_operand(vv, kvh)
                prod = do2 * o2
                dob = do2.astype(BF)
                dp = _dot_nt(dob, v4)
                ds, pb = [], []
                for half in range(2):
                    heads = [4 * kvh + 2 * pair + half for pair in range(2)]
                    p = jnp.concatenate([p_ref[:, 128 * h:128 * (h + 1)] for h in heads], axis=0)
                    sel = (lane2 < HEAD_DIM) if half == 0 else (lane2 >= HEAD_DIM)
                    delta = jnp.sum(jnp.where(sel, prod, 0.0), axis=-1, keepdims=True)
                    dp_h = dp[:, 2 * half * BLOCK:2 * (half + 1) * BLOCK]
                    ds.append((p.astype(F32) * (jnp.where(tri, dp_h[:, :BLOCK], dp_h[:, BLOCK:]) - delta)).astype(BF))
                    pb.append(p)
                    for pair in range(2):
                        deltas = jnp.where(lane == heads[pair], delta[pair * BLOCK:(pair + 1) * BLOCK], deltas)
                ds4, p4 = _split4(ds, tri), _split4(pb, tri)
                dq2 = _dot(ds4, k4) * SCALE
                dq_ref[:, c0:c0 + 128] = _rope_t(dq2[:BLOCK], *rope_q).astype(BF)
                dq_ref[:, c0 + 128:c0 + 256] = _rope_t(dq2[BLOCK:], *rope_q).astype(BF)
                rk, rv = _dot_tn(ds4, q2), _dot_tn(p4, dob)
                own = (lane < HEAD_DIM) if kvh == 0 else (lane >= HEAD_DIM)
                group = []
                for r in (rk, rv):
                    for blk in range(2):
                        t = jnp.where(lane < HEAD_DIM, r[blk * BLOCK:(blk + 1) * BLOCK], r[(2 + blk) * BLOCK:(3 + blk) * BLOCK])
                        group.append(jnp.where(own, t + pltpu.roll(t, HEAD_DIM, 1), 0.0))
                folded.append(group)
            dsink_ref[ROW_SINK:ROW_SINK + 1, :] -= jnp.sum(ps_ref[...] * deltas, axis=0, keepdims=True)
            dk_p, dk_c, dv_p, dv_c = [folded[0][t] + folded[1][t] for t in range(4)]
            finish(carry[...] + jnp.concatenate([dk_p, dv_p], axis=1))
            carry[...] = jnp.concatenate([dk_c, dv_c], axis=1)

        @pl.when(i == nb)
        def _():
            finish(carry[...])

    qi = lambda i: jnp.minimum(i, nb - 1)
    ki = lambda i: jnp.maximum(i - 1, 0)
    tab_q = pl.BlockSpec((BLOCK, 128), lambda i: (qi(i), 0))
    tab_k = pl.BlockSpec((BLOCK, 128), lambda i: (ki(i), 0))
    return _call(
        body, exch,
        name=name,
        grid=(nb + 1,),
        in_specs=[
            pl.BlockSpec((BLOCK, ATTN_W), lambda i: (qi(i), 0)),
            pl.BlockSpec((BLOCK, 256), lambda i: (qi(i), 2)),
            pl.BlockSpec((BLOCK, 256), lambda i: (jnp.maximum(qi(i) - 1, 0), 2)),
            pl.BlockSpec((BLOCK, ATTN_W), lambda i: (qi(i), 0)),
            pl.BlockSpec((BLOCK, ATTN_W), lambda i: (qi(i), 0)),
            pl.BlockSpec((BLOCK, N_Q_HEADS * BLOCK), lambda i: (qi(i), 0)),
            tab_q, tab_q, tab_q, tab_q, tab_k, tab_k, tab_k,
        ],
        out_specs=[
            pl.BlockSpec((BLOCK, ATTN_W), lambda i: (qi(i), 0)),
            pl.BlockSpec((BLOCK, 256), lambda i: (ki(i), 0)),
            pl.BlockSpec((8, 128), lambda i: (0, 0)),
        ],
        out_shape=[
            jax.ShapeDtypeStruct((lp, ATTN_W), BF),
            jax.ShapeDtypeStruct((lp, 256), BF),
            jax.ShapeDtypeStruct((8, 128), F32),
        ],
        scratch_shapes=[pltpu.VMEM((BLOCK, 256), F32)],
        compiler_params=_params(),
    )(qkv, qkv, qkv, o, do, probs, p_sink, *rope, *rope)


def _in_proj_bwd_dx(dq, dkv, dbch, w_in_t, h, dh2, g, tm, name, exch=None):
    lp = h.shape[0]

    def body(dq_ref, dkv_ref, dbch_ref, w_ref, h_ref, dh2_ref, g_ref, dh_ref, dg_ref):
        i = pl.program_id(0)

        @pl.when(i == 0)
        def _():
            dg_ref[...] = jnp.zeros((8, D_MODEL), F32)

        da = _dot(jnp.concatenate([dq_ref[...], dkv_ref[...], dbch_ref[...]], axis=1), w_ref[...])
        dh, dg = _rms_bwd(h_ref[...], g_ref[...], da)
        dg_ref[ROW_MIX_PRE:ROW_MIX_PRE + 1, :] += dg
        dh_ref[...] = dh2_ref[...] + dh

    row = lambda w: pl.BlockSpec((tm, w), lambda i: (i, 0))
    return _call(
        body, exch,
        name=name,
        grid=(lp // tm,),
        in_specs=[row(ATTN_W), row(256), row(3 * CONV_W), _full((IN_W, D_MODEL)), row(D_MODEL), row(D_MODEL), _full((1, D_MODEL))],
        out_specs=[row(D_MODEL), _full_out((8, D_MODEL))],
        out_shape=[jax.ShapeDtypeStruct((lp, D_MODEL), F32), jax.ShapeDtypeStruct((8, D_MODEL), F32)],
        compiler_params=_params(),
    )(dq, dkv, dbch, w_in_t, h, dh2, g)


def _mix_bwd_dw(dq, dkv, dbch, a, tm, name, exch=None):
    lp = a.shape[0]
    nt = lp // tm

    def body(dq_ref, dkv_ref, dbch_ref, a_ref, dwi_ref, acci):
        i = pl.program_id(0)

        @pl.when(i == 0)
        def _():
            acci[...] = jnp.zeros_like(acci)

        a_v = a_ref[...]
        acci[0:512, :] += _dot_tn(dq_ref[...], a_v)
        acci[512:768, :] += _dot_tn(dkv_ref[...], a_v)
        acci[768:, :] += _dot_tn(dbch_ref[...], a_v)

        @pl.when(i == nt - 1)
        def _():
            dwi_ref[...] = acci[...].astype(BF)

    row = lambda w: pl.BlockSpec((tm, w), lambda i: (i, 0))
    return _call(
        body, exch,
        name=name,
        grid=(nt,),
        in_specs=[row(ATTN_W), row(256), row(3 * CONV_W), row(D_MODEL)],
        out_specs=[_full_out((IN_W, D_MODEL))],
        out_shape=[jax.ShapeDtypeStruct((IN_W, D_MODEL), BF)],
        scratch_shapes=[pltpu.VMEM((IN_W, D_MODEL), F32)],
        compiler_params=_params(),
    )(dq, dkv, dbch, a)


def _mesh_place():
    x, y, c = lax.axis_index("x"), lax.axis_index("y"), lax.axis_index("c")
    return x, y, c, 4 * x + 2 * y + c


def _peer(x, y, c, k):
    px = 1 - x if k & 4 else x
    py = 1 - y if k & 2 else y
    pc = 1 - c if k & 1 else c
    return (px, py, pc), 4 * px + 2 * py + pc


SIBLING = 1
SAME_CORE = (2, 4, 6)
OTHER_CORE = (3, 5, 7)


class _Exchange:
    def __init__(self, pieces, forward_lead=8):
        self.forward_lead = forward_lead
        self.srcs = [s for s, _ in pieces]
        self.to_all = [g for _, g in pieces]
        self.n = len(pieces)
        self.land_shapes = [
            jax.ShapeDtypeStruct((N_DEV,) + (s.shape if g else s.shape[1:]), s.dtype) for s, g in pieces]
        self.sem_shapes = [pltpu.SemaphoreType.DMA((self.n, N_DEV - 1)), pltpu.SemaphoreType.DMA((self.n, N_DEV - 1)),
                           pltpu.SemaphoreType.DMA((self.n,))]
        self.forwards = any(self.to_all)

    def _ops(self, srcs, lands, sems):
        send_sems, recv_sems, local_sems = sems
        x, y, c, me = _mesh_place()

        def remote(p, k, src, slot, to):
            return pltpu.make_async_remote_copy(
                src_ref=src, dst_ref=lands[p].at[slot], send_sem=send_sems.at[p, k - 1], recv_sem=recv_sems.at[p, k - 1],
                device_id=to, device_id_type=MESH)

        def own(p):
            return pltpu.make_async_copy(srcs[p] if self.to_all[p] else srcs[p].at[me], lands[p].at[me], local_sems.at[p])

        def direct(p, k):
            peer, pidx = _peer(x, y, c, k)
            return remote(p, k, srcs[p] if self.to_all[p] else srcs[p].at[pidx], me, peer)

        def forward(p, k):
            sibling, _ = _peer(x, y, c, SIBLING)
            _, origin = _peer(x, y, c, k ^ SIBLING)
            return remote(p, k, lands[p].at[origin], origin, sibling)

        def arrival(p, k):
            peer, pidx = _peer(x, y, c, k)
            return remote(p, k, lands[p].at[pidx], pidx, peer)

        return own, direct, forward, arrival

    def start(self, srcs, lands, sems):
        own, direct, _, _ = self._ops(srcs, lands, sems)
        for p in range(self.n):
            own(p).start()
            for k in ((SIBLING,) + SAME_CORE) if self.to_all[p] else range(1, N_DEV):
                direct(p, k).start()

    def forward(self, srcs, lands, sems):
        _, _, forward, arrival = self._ops(srcs, lands, sems)
        for p in range(self.n):
            if self.to_all[p]:
                for k in SAME_CORE:
                    arrival(p, k).wait_recv()
                    forward(p, k ^ SIBLING).start()

    def finish(self, srcs, lands, sems):
        own, direct, forward, arrival = self._ops(srcs, lands, sems)
        for p in range(self.n):
            for k in ((SIBLING,) + OTHER_CORE) if self.to_all[p] else range(1, N_DEV):
                arrival(p, k).wait_recv()
        for p in range(self.n):
            for k in range(1, N_DEV):
                (forward(p, k) if self.to_all[p] and k in OTHER_CORE else direct(p, k)).wait_send()
            own(p).wait()


class _LayerRows:
    def __init__(self, array, layer):
        self.array = array if array.ndim == 3 else array.reshape(DEPTH, 1, -1)
        self.layer = layer

    def spec(self):
        layer = self.layer
        return pl.BlockSpec((None,) + self.array.shape[1:], lambda *_: (layer, 0, 0), pipeline_mode=pl.Buffered(1))


def _call(body, exch, *, name, grid, in_specs, out_specs, out_shape, scratch_shapes=(), compiler_params, after=None):
    def with_layer_rows(args):
        specs = [a.spec() if isinstance(a, _LayerRows) else s for s, a in zip(in_specs, args)]
        return specs, [a.array if isinstance(a, _LayerRows) else a for a in args]

    if exch is None:
        def plain(*args):
            specs, args = with_layer_rows(args)
            return pl.pallas_call(body, name=name, grid=grid, in_specs=specs, out_specs=out_specs, out_shape=out_shape,
                                  scratch_shapes=scratch_shapes, compiler_params=compiler_params)(*args)
        return plain
    n_in, n_out, n_scr, n_x = len(in_specs), len(out_shape), len(scratch_shapes), exch.n
    steps = math.prod(grid)

    def carrying(*refs):
        a, b, c, d, e = n_in, n_in + n_x, n_in + n_x + n_out, n_in + 2 * n_x + n_out, n_in + 2 * n_x + n_out + n_scr
        ins, srcs, outs, lands, scr, sems = refs[:a], refs[a:b], refs[b:c], refs[c:d], refs[d:e], refs[e:]
        step = functools.reduce(lambda acc, t: acc * grid[t] + pl.program_id(t), range(len(grid)), 0)

        @pl.when(step == 0)
        def _():
            exch.start(srcs, lands, sems)

        body(*ins, *outs, *scr)

        if exch.forwards:
            @pl.when(step == max(0, steps - 1 - pl.cdiv(steps, exch.forward_lead)))
            def _():
                exch.forward(srcs, lands, sems)

        @pl.when(step == steps - 1)
        def _():
            exch.finish(srcs, lands, sems)
            if after is not None:
                after(lands, *ins, *outs, *scr)

    hbm = pl.BlockSpec(memory_space=pl.ANY)

    def run(*args):
        specs, args = with_layer_rows(args)
        res = pl.pallas_call(
            carrying, name=name, grid=grid, in_specs=specs + [hbm] * n_x, out_specs=list(out_specs) + [hbm] * n_x,
            out_shape=list(out_shape) + exch.land_shapes, scratch_shapes=list(scratch_shapes) + exch.sem_shapes,
            compiler_params=compiler_params)(*args, *exch.srcs)
        return list(res[:n_out]), list(res[n_out:])

    return run


def _sum_small(part):
    exch = _Exchange([(part, True)])

    def body(part_ref, out_ref, land, *sems):
        exch.start([part_ref], [land], sems)
        exch.forward([part_ref], [land], sems)
        exch.finish([part_ref], [land], sems)
        acc = land[0]
        for d in range(1, N_DEV):
            acc = acc + land[d]
        out_ref[...] = acc

    vmem = pl.BlockSpec(memory_space=pltpu.VMEM)
    return pl.pallas_call(
        body,
        name="sum_small",
        in_specs=[vmem],
        out_specs=vmem,
        out_shape=jax.ShapeDtypeStruct(part.shape, F32),
        scratch_shapes=[pltpu.VMEM(exch.land_shapes[0].shape, F32)] + exch.sem_shapes,
    )(part)


def _adamw(w, g, m, v):
    m = ADAM_B1 * m + (1.0 - ADAM_B1) * g
    v = ADAM_B2 * v + (1.0 - ADAM_B2) * jnp.square(g)
    m_hat = m / (1.0 - ADAM_B1 ** ADAM_STEP)
    v_hat = v / (1.0 - ADAM_B2 ** ADAM_STEP)
    delta = -ADAM_LR * (m_hat / (jnp.sqrt(v_hat) + ADAM_EPS) + ADAM_WD * w)
    return delta, m, v


def _landed_specs(tr, wd):
    return [pl.BlockSpec((N_DEV, tr, wd), lambda l, i, ll=ll: (0, jnp.where(l == ll, i, 0), 0)) for ll in range(DEPTH)]


def _device_sum(r_ref):
    acc = r_ref[0].astype(F32)
    for d in range(1, N_DEV):
        acc = acc + r_ref[d].astype(F32)
    return acc


def _sum_adamw(recv, w, m, v, tr, name, transposed=False):
    _, r, wd = recv[0].shape

    def body(*refs):
        w_ref, m_ref, v_ref, g_ref, d_ref, mo_ref, vo_ref = refs[DEPTH:]
        for ll in range(DEPTH):
            @pl.when(pl.program_id(0) == ll)
            def _(ll=ll):
                g = _device_sum(refs[ll])
                g = g.T if transposed else g
                g_ref[0] = g
                d_ref[0], mo_ref[0], vo_ref[0] = _adamw(w_ref[0], g, m_ref[0], v_ref[0])

    if transposed:
        blk = pl.BlockSpec((1, wd, tr), lambda l, i: (l, 0, i))
        shape = jax.ShapeDtypeStruct((DEPTH, wd, r), F32)
    else:
        blk = pl.BlockSpec((1, tr, wd), lambda l, i: (l, i, 0))
        shape = jax.ShapeDtypeStruct((DEPTH, r, wd), F32)
    return pl.pallas_call(
        body,
        name=name,
        grid=(DEPTH, r // tr),
        in_specs=_landed_specs(tr, wd) + [blk, blk, blk],
        out_specs=[blk] * 4,
        out_shape=[shape] * 4,
        compiler_params=_params(("arbitrary", "arbitrary")),
    )(*recv, w, m, v)


def _adamw_small(ws, gs, ms, vs):
    n = len(ws)

    def body(*refs):
        w_r, g_r, m_r, v_r = refs[:n], refs[n:2 * n], refs[2 * n:3 * n], refs[3 * n:4 * n]
        d_o, m_o, v_o = refs[4 * n:5 * n], refs[5 * n:6 * n], refs[6 * n:7 * n]
        for t in range(n):
            d_o[t][...], m_o[t][...], v_o[t][...] = _adamw(w_r[t][...], g_r[t][...], m_r[t][...], v_r[t][...])

    vmem = pl.BlockSpec(memory_space=pltpu.VMEM)
    shapes = [jax.ShapeDtypeStruct(w.shape, F32) for w in ws]
    outs = pl.pallas_call(
        body,
        name="adamw_small",
        in_specs=[vmem] * (4 * n),
        out_specs=[vmem] * (3 * n),
        out_shape=shapes * 3,
    )(*ws, *gs, *ms, *vs)
    return outs[:n], outs[n:2 * n], outs[2 * n:]


def kernel(x, meta_tokens, mix_pre_g, w_in, conv_w, sinks, attn_out_g, conv_out_g, w_out, mix_post_g, mlp_pre_g, w_up, w_down, mlp_post_g, loss_target, m_meta_tokens, m_mix_pre_g, m_w_in, m_conv_w, m_sinks, m_attn_out_g, m_conv_out_g, m_w_out, m_mix_post_g, m_mlp_pre_g, m_w_up, m_w_down, m_mlp_post_g, v_meta_tokens, v_mix_pre_g, v_w_in, v_conv_w, v_sinks, v_attn_out_g, v_conv_out_g, v_w_out, v_mix_post_g, v_mlp_pre_g, v_w_up, v_w_down, v_mlp_post_g):
    seq = x.shape[1]
    lp = BLOCK + seq
    tm = _row_tile(lp)
    tm_mlp = _row_tile(lp, (320, 256, 128))
    tm_dw_mlp = _row_tile(lp, (1664, 1040, 640, 384, 256, 128))
    tm_dw_mix = _row_tile(lp, (1664, 832, 640, 384, 256, 128))
    me = 4 * lax.axis_index("x") + 2 * lax.axis_index("y") + lax.axis_index("c")
    cshard = CONV_W // N_DEV
    mshard = D_MODEL // N_DEV

    gather_with = {
        ("in_proj_fwd", 0): [("down", 0)], ("attn_fwd", 0): [("out", 0), ("up", 0)],
        ("mlp_fwd", 0): [("in", 1), ("out", 1), ("up", 1), ("down", 1)],
    }
    tight = {("in_proj_fwd", 0)}
    scatter_with = {
        ("attn_bwd", 1): [("down", 1)], ("mix_bwd_dw", 1): [("out", 1)], ("mlp_bwd_dx", 0): [("up", 1), ("in", 1)],
        ("mix_out_bwd", 0): [("up", 0)], ("attn_bwd", 0): [("down", 0)], ("mix_bwd_dw", 0): [("out", 0)],
        ("in_proj_bwd_dx", 0): [("in", 0)],
    }
    shard = {"in": jnp.swapaxes(w_in, 1, 2).astype(BF), "out": w_out.astype(BF),
             "up": jnp.swapaxes(w_up, 1, 2).astype(BF), "down": w_down.astype(BF)}
    weight = {}
    grad = {}
    landed = {}

    def run(fn, kind, l, *args):
        key, name = (kind, l), f"{kind}_{l}"
        if key in gather_with:
            blocks = gather_with[key]
            lead = 16 if key in tight else 8
            outs, lands = fn(*args, name, _Exchange([(shard[n][k], True) for n, k in blocks], lead))
            for b, land in zip(blocks, lands):
                weight[b] = land.reshape(-1, D_MODEL)
            return outs
        if key in scatter_with:
            blocks = scatter_with[key]
            outs, lands = fn(*args, name, _Exchange([(grad[b].reshape(N_DEV, -1, D_MODEL), False) for b in blocks]))
            landed.update(zip(blocks, lands))
            return outs
        return fn(*args, name)

    small = jnp.zeros((24, 128), F32)
    small = small.at[0:N_META, :].set(meta_tokens)
    small = small.at[N_META:N_META + 6, 0:cshard].set(conv_w.reshape(6, cshard))
    first = _Exchange([(shard["in"][0], True), (small, True)], 16)
    h, rope, (first_in, g_small) = _build_h(x[0], _rope_table(lp), tm, first, 1, "build_h")
    weight[("in", 0)] = first_in.reshape(-1, D_MODEL)
    cw = g_small[:, N_META:N_META + 6, 0:cshard].reshape(N_DEV, DEPTH, 3, cshard)
    cw = jnp.transpose(cw, (1, 2, 0, 3)).reshape(DEPTH, 3, CONV_W)
    conv_full = jnp.concatenate([cw, jnp.zeros((DEPTH, 5, CONV_W), F32)], axis=1)

    row1 = _LayerRows

    saved = []
    for l in range(DEPTH):
        a, qkv, bch = run(_in_proj_fwd, "in_proj_fwd", l, h, row1(mix_pre_g, l), weight[("in", l)], rope, tm)
        y_attn, probs, p_sink = run(_attn_fwd, "attn_fwd", l, qkv, sinks[l].reshape(1, -1))
        yc, y, z, h2 = run(_mix_out_fwd, "mix_out_fwd", l, bch, y_attn, h, row1(conv_full, l), row1(attn_out_g, l),
                       row1(conv_out_g, l), weight[("out", l)], row1(mix_post_g, l), tm)
        mlp = _mlp_fwd if l < DEPTH - 1 else functools.partial(_mlp_fwd, target=loss_target[0])
        a2, up, f, *rest = run(mlp, "mlp_fwd", l, h2, row1(mlp_pre_g, l), weight[("up", l)], weight[("down", l)],
                               row1(mlp_post_g, l), tm_mlp)
        saved.append((h, a, qkv, bch, y_attn, probs, p_sink, yc, y, z, h2, a2, up, f))
        h = rest[0]
    dh, loss_part = rest[0], rest[1][0, 0] * (0.5 / D_MODEL)

    gsmall = [None] * DEPTH
    for l in reversed(range(DEPTH)):
        h0, a, qkv, bch, y_attn, probs, p_sink, yc, y, z, h2, a2, up, f = saved[l]
        df, dup, dh2, dg_mlp = run(_mlp_bwd_dx, "mlp_bwd_dx", l, dh, f, up, h2, weight[("down", l)], weight[("up", l)],
                                   row1(mlp_post_g, l), row1(mlp_pre_g, l), tm_mlp)
        grad[("down", l)], grad[("up", l)] = _mlp_bwd_dw(up, df, dup, a2, tm_dw_mlp, f"mlp_bwd_dw_{l}")
        dya, dbch, dg_mix, grad[("out", l)] = run(
            _mix_out_bwd, "mix_out_bwd", l, dh2, z, y_attn, yc, bch, y, weight[("out", l)], row1(mix_post_g, l),
            row1(attn_out_g, l), row1(conv_out_g, l), row1(conv_full, l), tm)
        dq, dkv, dsink = run(_attn_bwd, "attn_bwd", l, qkv, y_attn, dya, probs, p_sink, rope)
        grad[("in", l)], = run(_mix_bwd_dw, "mix_bwd_dw", l, dq, dkv, dbch, a, tm_dw_mix)
        dh, dg_in = run(_in_proj_bwd_dx, "in_proj_bwd_dx", l, dq, dkv, dbch, weight[("in", l)], h0, dh2,
                        row1(mix_pre_g, l), tm)
        tile_a =dg_mlp + dg_in + jnp.pad(dsink, ((0, 0), (0, D_MODEL - 128)))
        gsmall[l] = (tile_a, dg_mix)
    grad_x = dh[BLOCK:][None]

    loss_tile = jnp.zeros((8, D_MODEL), F32).at[ROW_LOSS, 0].set(loss_part)
    tot = _sum_small(jnp.concatenate(
        [gsmall[0][0] + loss_tile, gsmall[0][1], gsmall[1][0], gsmall[1][1], dh[LEAD_PAD:BLOCK]], axis=0))
    loss = tot[ROW_LOSS, 0]
    ta = [tot[16 * l:16 * l + 8] for l in range(DEPTH)]
    tb = [tot[16 * l + 8:16 * l + 16] for l in range(DEPTH)]
    pick = lambda tiles, r0, r1, c0, c1: jnp.stack([t[r0:r1, c0:c1] for t in tiles])
    g_mlp_post = pick(ta, ROW_MLP_POST, ROW_MLP_POST + 1, 0, D_MODEL).reshape(DEPTH, D_MODEL)
    g_mlp_pre = pick(ta, ROW_MLP_PRE, ROW_MLP_PRE + 1, 0, D_MODEL).reshape(DEPTH, D_MODEL)
    g_mix_pre = pick(ta, ROW_MIX_PRE, ROW_MIX_PRE + 1, 0, D_MODEL).reshape(DEPTH, D_MODEL)
    g_sinks = pick(ta, ROW_SINK, ROW_SINK + 1, 0, N_Q_HEADS).reshape(DEPTH, N_Q_HEADS)
    g_mix_post = pick(tb, ROW_MIX_POST, ROW_MIX_POST + 1, 0, D_MODEL).reshape(DEPTH, D_MODEL)
    g_attn_out = pick(tb, ROW_GROUP_G, ROW_GROUP_G + 1, 0, ATTN_W).reshape(DEPTH, ATTN_W)
    g_conv_out = pick(tb, ROW_GROUP_G, ROW_GROUP_G + 1, ATTN_W, D_MODEL).reshape(DEPTH, CONV_W)
    g_conv_full = pick(tb, ROW_CONV, ROW_CONV + 3, 0, CONV_W)
    g_conv = lax.dynamic_slice_in_dim(g_conv_full, me * cshard, cshard, axis=2)
    g_meta = lax.dynamic_slice_in_dim(tot[16 * DEPTH:16 * DEPTH + N_META], me * mshard, mshard, axis=1)

    r_in, r_out, r_up, r_down = [[landed[(n, l)] for l in range(DEPTH)] for n in ("in", "out", "up", "down")]
    t12 = lambda a: jnp.swapaxes(a, 1, 2)
    g_w_in, d_w_in, nm_w_in, nv_w_in = map(t12, _sum_adamw(r_in, t12(w_in), t12(m_w_in), t12(v_w_in), 96, "adamw_w_in"))
    g_w_up, d_w_up, nm_w_up, nv_w_up = _sum_adamw(r_up, w_up, m_w_up, v_w_up, 128, "adamw_w_up", transposed=True)
    g_w_out, d_w_out, nm_w_out, nv_w_out = _sum_adamw(r_out, w_out, m_w_out, v_w_out, 128, "adamw_w_out")
    g_w_down, d_w_down, nm_w_down, nv_w_down = _sum_adamw(r_down, w_down, m_w_down, v_w_down, 128, "adamw_w_down")

    ws = [meta_tokens, mix_pre_g, conv_w.reshape(6, cshard), sinks, attn_out_g, conv_out_g, mix_post_g, mlp_pre_g, mlp_post_g]
    gs = [g_meta, g_mix_pre, g_conv.reshape(6, cshard), g_sinks, g_attn_out, g_conv_out, g_mix_post, g_mlp_pre, g_mlp_post]
    ms = [m_meta_tokens, m_mix_pre_g, m_conv_w.reshape(6, cshard), m_sinks, m_attn_out_g, m_conv_out_g, m_mix_post_g,
          m_mlp_pre_g, m_mlp_post_g]
    vs = [v_meta_tokens, v_mix_pre_g, v_conv_w.reshape(6, cshard), v_sinks, v_attn_out_g, v_conv_out_g, v_mix_post_g,
          v_mlp_pre_g, v_mlp_post_g]
    ds, nms, nvs = _adamw_small(ws, gs, ms, vs)

    def order(meta, mix_pre, cv, sk, a_out, c_out, mix_post, mlp_pre, mlp_post, win, wout, wup, wdown):
        return [meta, mix_pre, win, cv.reshape(DEPTH, 3, cshard), sk, a_out, c_out, wout, mix_post, mlp_pre, wup, wdown, mlp_post]

    grads = order(*gs, g_w_in, g_w_out, g_w_up, g_w_down)
    deltas = order(*ds, d_w_in, d_w_out, d_w_up, d_w_down)
    new_m = order(*nms, nm_w_in, nm_w_out, nm_w_up, nm_w_down)
    new_v = order(*nvs, nv_w_in, nv_w_out, nv_w_up, nv_w_down)
    return (loss, grad_x, *grads, *deltas, *new_m, *new_v)
```

```python
import functools
import math

import jax
import jax.numpy as jnp
from jax import lax
from jax.experimental import pallas as pl
from jax.experimental.pallas import tpu as pltpu

F32 = jnp.float32
BF = jnp.bfloat16

D_MODEL = 1024
ATTN_W = 512
CONV_W = 512
HEAD_DIM = 64
N_Q_HEADS = 8
ROT_DIM = 16
D_FF = 4096
IN_W = 2304
N_META = 16
BLOCK = 128
LEAD_PAD = BLOCK - N_META
ROPE_THETA = 500000.0
EPS = 1e-6
N_DEV = 8
DEPTH = 2
NEG = -1e30
SCALE = HEAD_DIM ** -0.5

ADAM_LR = 0.001
ADAM_B1 = 0.9
ADAM_B2 = 0.999
ADAM_EPS = 1e-08
ADAM_WD = 0.01
ADAM_STEP = 10

ROW_MLP_POST, ROW_MLP_PRE, ROW_MIX_PRE, ROW_SINK, ROW_LOSS = 0, 1, 2, 3, 4
ROW_MIX_POST, ROW_GROUP_G, ROW_CONV = 0, 1, 2

VMEM_LIMIT = 56 * 1024 * 1024
MESH = pl.DeviceIdType.MESH


def _dot(a, b):
    return jnp.dot(a, b, preferred_element_type=F32)


def _dot_nt(a, b):
    return lax.dot_general(a, b, (((1,), (1,)), ((), ())), preferred_element_type=F32)


def _dot_tn(a, b):
    return lax.dot_general(a, b, (((0,), (0,)), ((), ())), preferred_element_type=F32)


def _rms_fwd(x, g):
    r = lax.rsqrt(jnp.mean(x * x, axis=-1, keepdims=True) + EPS)
    return x * r * g


def _rms_bwd(x, g, dy):
    r = lax.rsqrt(jnp.mean(x * x, axis=-1, keepdims=True) + EPS)
    xh = x * r
    t = dy * g
    dx = r * (t - xh * jnp.mean(t * xh, axis=-1, keepdims=True))
    dg = jnp.sum(dy * xh, axis=0, keepdims=True)
    return dx, dg


def _row_tile(lp, cands=(640, 512, 384, 256, 128)):
    for t in cands:
        if lp % t == 0:
            return t
    raise ValueError(f"row count {lp} is not a multiple of 128")


def _full(shape):
    n = len(shape)
    return pl.BlockSpec(shape, lambda *_: (0,) * n, pipeline_mode=pl.Buffered(1))


def _full_out(shape):
    n = len(shape)
    return pl.BlockSpec(shape, lambda *_: (0,) * n)


def _params(sem=("arbitrary",)):
    return pltpu.CompilerParams(dimension_semantics=sem, vmem_limit_bytes=VMEM_LIMIT)


def _rope_table(lp):
    half = ROT_DIM // 2
    pos = jnp.maximum(jnp.arange(lp) - LEAD_PAD, 0).astype(F32)
    inv_freq = jnp.power(jnp.float32(ROPE_THETA), -jnp.arange(0, ROT_DIM, 2, dtype=F32) / ROT_DIM)
    ang_t = jnp.concatenate([inv_freq, inv_freq])[:, None] * pos[None, :]
    row = lax.broadcasted_iota(jnp.int32, (ROT_DIM, lp), 0)
    cs_t = jnp.where(row < half, jnp.cos(ang_t), jnp.sin(ang_t))
    return jnp.pad(cs_t.T, ((0, 0), (0, 128 - ROT_DIM)))


def _rope_coeffs(t):
    half = ROT_DIM // 2
    lane = lax.broadcasted_iota(jnp.int32, t.shape, 1)
    cos_a = jnp.where(lane < half, t, 0.0)
    sin_a = pltpu.roll(jnp.where((lane >= half) & (lane < ROT_DIM), t, 0.0), 128 - half, 1)
    c = cos_a + pltpu.roll(cos_a, half, 1) + jnp.where((lane >= ROT_DIM) & (lane < HEAD_DIM), 1.0, 0.0)
    s2 = pltpu.roll(sin_a, half, 1)
    both = lambda u: u + pltpu.roll(u, HEAD_DIM, 1)
    return both(c), both(-sin_a), both(s2)


def _rope(t, c, s1, s2):
    return t * c + pltpu.roll(t, BLOCK - 8, 1) * s1 + pltpu.roll(t, 8, 1) * s2


def _rope_t(dt, c, s1, s2):
    return dt * c + pltpu.roll(dt * s1, 8, 1) + pltpu.roll(dt * s2, BLOCK - 8, 1)


def _build_h(x, rope_compact, tm, exch, small_piece, name):
    seq = x.shape[0]
    lp = BLOCK + seq
    nt = lp // tm
    n_sub = tm // BLOCK
    small_shape = exch.land_shapes[small_piece].shape

    def body(*refs):
        h_ref, c_ref, s1_ref, s2_ref = refs[n_sub + 1:n_sub + 5]
        for j in range(n_sub):
            h_ref[j * BLOCK:(j + 1) * BLOCK, :] = refs[j][...]
        c_ref[...], s1_ref[...], s2_ref[...] = _rope_coeffs(refs[n_sub][...])

    def after(lands, *refs):
        h_ref, buf = refs[n_sub + 1], refs[n_sub + 5]
        pltpu.sync_copy(lands[small_piece], buf)
        h_ref[0:LEAD_PAD, :] = jnp.zeros((LEAD_PAD, D_MODEL), F32)
        for d in range(N_DEV):
            h_ref[LEAD_PAD:BLOCK, d * 128:(d + 1) * 128] = buf[d, 0:N_META, :]

    tile = lambda i: (i + 1) % nt
    piece = lambda j: pl.BlockSpec((BLOCK, D_MODEL), lambda i: (jnp.maximum(tile(i) * n_sub + j - 1, 0), 0))
    rows = lambda w: pl.BlockSpec((tm, w), lambda i: (tile(i), 0))
    (h, *rope), lands = _call(
        body, exch,
        name=name,
        grid=(nt,),
        in_specs=[piece(j) for j in range(n_sub)] + [rows(128)],
        out_specs=[rows(D_MODEL)] + [rows(128)] * 3,
        out_shape=[jax.ShapeDtypeStruct((lp, D_MODEL), F32)] + [jax.ShapeDtypeStruct((lp, 128), F32)] * 3,
        scratch_shapes=[pltpu.VMEM(small_shape, F32)],
        compiler_params=_params(),
        after=after,
    )(*([x] * n_sub), rope_compact)
    return h, rope, lands


def _in_proj_fwd(h, g, w_in_t, rope, tm, name, exch=None):
    lp = h.shape[0]

    def body(h_ref, g_ref, w_ref, c_ref, s1_ref, s2_ref, a_ref, qkv_ref, bch_ref):
        a = _rms_fwd(h_ref[...], g_ref[...]).astype(BF)
        a_ref[...] = a
        proj = _dot_nt(a, w_ref[...])
        c, s1, s2 = c_ref[...], s1_ref[...], s2_ref[...]
        for j in range(5):
            t = _rope(proj[:, j * 128:(j + 1) * 128], c, s1, s2)
            qkv_ref[:, j * 128:(j + 1) * 128] = (t * SCALE if j < 4 else t).astype(BF)
        qkv_ref[:, 640:768] = proj[:, 640:768].astype(BF)
        bch_ref[...] = proj[:, 768:].astype(BF)

    row = lambda w: pl.BlockSpec((tm, w), lambda i: (i, 0))
    return _call(
        body, exch,
        name=name,
        grid=(lp // tm,),
        in_specs=[row(D_MODEL), _full((1, D_MODEL)), _full((IN_W, D_MODEL)), row(128), row(128), row(128)],
        out_specs=[row(D_MODEL), row(768), row(3 * CONV_W)],
        out_shape=[
            jax.ShapeDtypeStruct((lp, D_MODEL), BF),
            jax.ShapeDtypeStruct((lp, 768), BF),
            jax.ShapeDtypeStruct((lp, 3 * CONV_W), BF),
        ],
        compiler_params=_params(),
    )(h, g, w_in_t, *rope)


def _fold_masks(i):
    r = lax.broadcasted_iota(jnp.int32, (2 * BLOCK, BLOCK), 0) & (BLOCK - 1)
    c = lax.broadcasted_iota(jnp.int32, (2 * BLOCK, BLOCK), 1)
    tri = c > r
    ok = jnp.where(tri, (i - 1) * BLOCK + c, i * BLOCK + c) >= LEAD_PAD
    return tri, ok


def _kv_operand(x, kvh):
    lane = lax.broadcasted_iota(jnp.int32, x.shape, 1)
    zero = jnp.zeros_like(x)
    if kvh == 0:
        lo = jnp.where(lane < HEAD_DIM, x, zero)
        hi = pltpu.roll(lo, HEAD_DIM, 1)
    else:
        hi = jnp.where(lane >= HEAD_DIM, x, zero)
        lo = pltpu.roll(hi, HEAD_DIM, 1)
    return jnp.concatenate([lo, hi], axis=0)


def _split4(t, tri):
    zero = jnp.zeros_like(t[0])
    return jnp.concatenate(
        [jnp.where(tri, t[0], zero), jnp.where(tri, zero, t[0]), jnp.where(tri, t[1], zero), jnp.where(tri, zero, t[1])], axis=1)


def _sink_cols(sink_ref, kvh):
    first = lax.broadcasted_iota(jnp.int32, (2 * BLOCK, 1), 0) < BLOCK
    return [jnp.where(first, sink_ref[0, 4 * kvh + half], sink_ref[0, 4 * kvh + 2 + half]) for half in range(2)]


def _folded_exp(q2, k4, tri, ok, sks):
    s = _dot_nt(q2, k4)
    es, ss = [], []
    for half in range(2):
        s_h = s[:, 2 * half * BLOCK:2 * (half + 1) * BLOCK]
        sf = jnp.where(ok, jnp.where(tri, s_h[:, :BLOCK], s_h[:, BLOCK:]), NEG)
        m = jnp.maximum(jnp.max(sf, axis=-1, keepdims=True), sks[half])
        es.append(jnp.exp(sf - m))
        ss.append(jnp.exp(sks[half] - m))
    sums = _dot(jnp.concatenate(es, axis=0).astype(BF), jnp.ones((BLOCK, BLOCK), BF))
    invs = [1.0 / (sums[2 * half * BLOCK:2 * (half + 1) * BLOCK] + ss[half]) for half in range(2)]
    return es, ss, invs


def _attn_fwd(qkv, sink, name, exch=None):
    lp = qkv.shape[0]
    nb = lp // BLOCK
    per_step = 4

    def one_block(i, sink_ref, q_ref, kvc_ref, kvp_ref, o_ref, p_ref, ps_ref):
        tri, ok = _fold_masks(i)
        kvc, kvp = kvc_ref[...], kvp_ref[...]
        kk = jnp.concatenate([kvp[:, :128], kvc[:, :128]], axis=0)
        vv = jnp.concatenate([kvp[:, 128:], kvc[:, 128:]], axis=0)
        lane = lax.broadcasted_iota(jnp.int32, (BLOCK, 128), 1)
        p_sink = jnp.zeros((BLOCK, 128), F32)
        for kvh in range(2):
            q2 = jnp.concatenate([q_ref[:, 256 * kvh:256 * kvh + 128], q_ref[:, 256 * kvh + 128:256 * kvh + 256]], axis=0)
            es, ss, invs = _folded_exp(q2, _kv_operand(kk, kvh), tri, ok, _sink_cols(sink_ref, kvh))
            pb = [(es[half] * invs[half]).astype(BF) for half in range(2)]
            out = _dot(_split4(pb, tri), _kv_operand(vv, kvh))
            for pair in range(2):
                rows = slice(pair * BLOCK, (pair + 1) * BLOCK)
                o_ref[:, 256 * kvh + 128 * pair:256 * kvh + 128 * (pair + 1)] = out[rows].astype(BF)
                for half in range(2):
                    head = 4 * kvh + 2 * pair + half
                    p_ref[:, 128 * head:128 * (head + 1)] = pb[half][rows]
                    p_sink = jnp.where(lane == head, (ss[half] * invs[half][:, 0:1])[rows], p_sink)
        ps_ref[...] = p_sink

    def body(sink_ref, *refs):
        q_refs, kv_refs = refs[:per_step], refs[per_step:2 * per_step + 1]
        o_ref, p_ref, ps_ref = refs[2 * per_step + 1:]
        for j in range(per_step):
            rows = slice(j * BLOCK, (j + 1) * BLOCK)
            one_block(per_step * pl.program_id(0) + j, sink_ref, q_refs[j], kv_refs[j + 1], kv_refs[j],
                      o_ref.at[rows], p_ref.at[rows], ps_ref.at[rows])

    last = nb - 1
    blk = lambda j: (lambda s: jnp.minimum(per_step * s + j, last))
    out_rows = lambda w: pl.BlockSpec((per_step * BLOCK, w), lambda s: (s, 0))
    return _call(
        body, exch,
        name=name,
        grid=(pl.cdiv(nb, per_step),),
        in_specs=[pl.BlockSpec(memory_space=pltpu.SMEM)]
        + [pl.BlockSpec((BLOCK, ATTN_W), lambda s, j=j: (blk(j)(s), 0)) for j in range(per_step)]
        + [pl.BlockSpec((BLOCK, 256), lambda s: (jnp.maximum(per_step * s - 1, 0), 2))]
        + [pl.BlockSpec((BLOCK, 256), lambda s, j=j: (blk(j)(s), 2)) for j in range(per_step)],
        out_specs=[out_rows(ATTN_W), out_rows(N_Q_HEADS * BLOCK), out_rows(128)],
        out_shape=[jax.ShapeDtypeStruct((lp, ATTN_W), BF), jax.ShapeDtypeStruct((lp, N_Q_HEADS * BLOCK), BF),
                   jax.ShapeDtypeStruct((lp, 128), F32)],
        compiler_params=_params(),
    )(sink, *([qkv] * (2 * per_step + 1)))


def _mix_out_fwd(bch, y_attn, h, conv_w, g_a, g_c, w_out, g_post, tm, name, exch=None):
    lp = h.shape[0]

    def body(bch_ref, ya_ref, h_ref, cw_ref, ga_ref, gc_ref, w_ref, gp_ref, yc_ref, y_ref, z_ref, h2_ref, ext):
        i = pl.program_id(0)

        @pl.when(i == 0)
        def _():
            ext[0:8, :] = jnp.zeros((8, CONV_W), F32)

        b = bch_ref[:, 0:CONV_W].astype(F32)
        u = bch_ref[:, CONV_W:2 * CONV_W].astype(F32) * bch_ref[:, 2 * CONV_W:3 * CONV_W].astype(F32)
        ext[8:8 + tm, :] = u
        yc = cw_ref[0:1, :] * ext[6:6 + tm, :] + cw_ref[1:2, :] * ext[7:7 + tm, :] + cw_ref[2:3, :] * u
        ext[0:8, :] = u[tm - 8:tm, :]
        yc_ref[...] = yc.astype(BF)
        ya = _rms_fwd(ya_ref[...].astype(F32), ga_ref[...]).astype(BF)
        yb = _rms_fwd(b * yc, gc_ref[...]).astype(BF)
        y_ref[:, 0:ATTN_W] = ya
        y_ref[:, ATTN_W:] = yb
        z = _dot(ya, w_ref[0:ATTN_W, :]) + _dot(yb, w_ref[ATTN_W:, :])
        z_ref[...] = z.astype(BF)
        h2_ref[...] = h_ref[...] + _rms_fwd(z, gp_ref[...])

    row = lambda w: pl.BlockSpec((tm, w), lambda i: (i, 0))
    return _call(
        body, exch,
        name=name,
        grid=(lp // tm,),
        in_specs=[
            row(3 * CONV_W), row(ATTN_W), row(D_MODEL), _full((8, CONV_W)), _full((1, ATTN_W)), _full((1, CONV_W)),
            _full((D_MODEL, D_MODEL)), _full((1, D_MODEL)),
        ],
        out_specs=[row(CONV_W), row(D_MODEL), row(D_MODEL), row(D_MODEL)],
        out_shape=[
            jax.ShapeDtypeStruct((lp, CONV_W), BF),
            jax.ShapeDtypeStruct((lp, D_MODEL), BF),
            jax.ShapeDtypeStruct((lp, D_MODEL), BF),
            jax.ShapeDtypeStruct((lp, D_MODEL), F32),
        ],
        scratch_shapes=[pltpu.VMEM((tm + 8, CONV_W), F32)],
        compiler_params=_params(),
    )(bch, y_attn, h, conv_w, g_a, g_c, w_out, g_post)


def _mlp_fwd(h2, g_pre, w_up_t, w_down, g_post, tm, name, exch=None, target=None):
    lp = h2.shape[0]
    sub = math.gcd(tm, BLOCK)
    n_sub, lead = tm // sub, BLOCK // sub
    n_t = n_sub if target is not None else 0

    def body(*refs):
        h_ref, gp_ref, wu_ref, wd_ref, gq_ref = refs[:5]
        t_refs = refs[5:5 + n_t]
        a_ref, up_ref, f_ref, last_ref = refs[5 + n_t:9 + n_t]
        h = h_ref[...]
        a = _rms_fwd(h, gp_ref[...]).astype(BF)
        a_ref[...] = a
        up = _dot_nt(a, wu_ref[...])
        up_ref[...] = up.astype(BF)
        act = jnp.square(jnp.maximum(up, 0.0)).astype(BF)
        f = _dot(act, wd_ref[...])
        f_ref[...] = f
        h3 = h + _rms_fwd(f, gq_ref[...])
        if target is None:
            last_ref[...] = h3
            return
        ls_ref = refs[9 + n_t]
        i = pl.program_id(0)

        @pl.when(i == 0)
        def _():
            ls_ref[...] = jnp.zeros((8, 128), F32)

        sq = jnp.zeros((8, D_MODEL), F32)
        for j in range(n_sub):
            on_tokens = i * n_sub + j >= lead
            d = jnp.where(on_tokens, h3[j * sub:(j + 1) * sub] - t_refs[j][...], 0.0)
            last_ref[j * sub:(j + 1) * sub, :] = d * (1.0 / D_MODEL)
            sq = sq + jnp.sum((d * d).reshape(sub // 8, 8, D_MODEL), axis=0)
        ls_ref[...] += sum(sq[:, k * 128:(k + 1) * 128] for k in range(D_MODEL // 128))

        @pl.when(i == lp // tm - 1)
        def _():
            ls_ref[...] = jnp.full((8, 128), jnp.sum(ls_ref[...]), F32)

    row = lambda w: pl.BlockSpec((tm, w), lambda i: (i, 0))
    piece = lambda j: pl.BlockSpec((sub, D_MODEL), lambda i: (jnp.maximum(i * n_sub + j - lead, 0), 0))
    out_specs = [row(D_MODEL), row(D_FF), row(D_MODEL), row(D_MODEL)]
    out_shape = [
        jax.ShapeDtypeStruct((lp, D_MODEL), BF),
        jax.ShapeDtypeStruct((lp, D_FF), BF),
        jax.ShapeDtypeStruct((lp, D_MODEL), F32),
        jax.ShapeDtypeStruct((lp, D_MODEL), F32),
    ]
    if target is not None:
        out_specs.append(_full_out((8, 128)))
        out_shape.append(jax.ShapeDtypeStruct((8, 128), F32))
    return _call(
        body, exch,
        name=name,
        grid=(lp // tm,),
        in_specs=[row(D_MODEL), _full((1, D_MODEL)), _full((D_FF, D_MODEL)), _full((D_FF, D_MODEL)), _full((1, D_MODEL))]
        + [piece(j) for j in range(n_t)],
        out_specs=out_specs,
        out_shape=out_shape,
        compiler_params=_params(),
    )(h2, g_pre, w_up_t, w_down, g_post, *([target] * n_t))


def _mlp_bwd_dx(dh3, f, up, h2, w_down, w_up_t, g_post, g_pre, tm, name, exch=None):
    lp = h2.shape[0]

    def body(dh3_ref, f_ref, up_ref, h2_ref, wd_ref, wu_ref, gq_ref, gp_ref, df_ref, dup_ref, dh2_ref, dg_ref):
        i = pl.program_id(0)

        @pl.when(i == 0)
        def _():
            dg_ref[...] = jnp.zeros((8, D_MODEL), F32)

        dh3 = dh3_ref[...]
        df, dgq = _rms_bwd(f_ref[...], gq_ref[...], dh3)
        dg_ref[ROW_MLP_POST:ROW_MLP_POST + 1, :] += dgq
        df = df.astype(BF)
        df_ref[...] = df
        dact = _dot_nt(df, wd_ref[...])
        dup = (dact * (2.0 * jnp.maximum(up_ref[...].astype(F32), 0.0))).astype(BF)
        dup_ref[...] = dup
        da = _dot(dup, wu_ref[...])
        dh, dgp = _rms_bwd(h2_ref[...], gp_ref[...], da)
        dg_ref[ROW_MLP_PRE:ROW_MLP_PRE + 1, :] += dgp
        dh2_ref[...] = dh3 + dh

    row = lambda w: pl.BlockSpec((tm, w), lambda i: (i, 0))
    return _call(
        body, exch,
        name=name,
        grid=(lp // tm,),
        in_specs=[
            row(D_MODEL), row(D_MODEL), row(D_FF), row(D_MODEL), _full((D_FF, D_MODEL)), _full((D_FF, D_MODEL)),
            _full((1, D_MODEL)), _full((1, D_MODEL)),
        ],
        out_specs=[row(D_MODEL), row(D_FF), row(D_MODEL), _full_out((8, D_MODEL))],
        out_shape=[
            jax.ShapeDtypeStruct((lp, D_MODEL), BF),
            jax.ShapeDtypeStruct((lp, D_FF), BF),
            jax.ShapeDtypeStruct((lp, D_MODEL), F32),
            jax.ShapeDtypeStruct((8, D_MODEL), F32),
        ],
        compiler_params=_params(),
    )(dh3, f, up, h2, w_down, w_up_t, g_post, g_pre)


def _mlp_bwd_dw(up, df, dup, a2, tm, name):
    lp = up.shape[0]
    nt = lp // tm
    nj = D_FF // D_MODEL

    def body(up_ref, df_ref, dup_ref, a_ref, dwd_ref, dwu_ref, accd, accu):
        i = pl.program_id(1)

        @pl.when(i == 0)
        def _():
            accd[...] = jnp.zeros_like(accd)
            accu[...] = jnp.zeros_like(accu)

        act = jnp.square(jnp.maximum(up_ref[...].astype(F32), 0.0)).astype(BF)
        accd[...] += _dot_tn(act, df_ref[...])
        accu[...] += _dot_tn(dup_ref[...], a_ref[...])

        @pl.when(i == nt - 1)
        def _():
            dwd_ref[...] = accd[...].astype(BF)
            dwu_ref[...] = accu[...].astype(BF)

    return pl.pallas_call(
        body,
        name=name,
        grid=(nj, nt),
        in_specs=[
            pl.BlockSpec((tm, D_MODEL), lambda j, i: (i, j)),
            pl.BlockSpec((tm, D_MODEL), lambda j, i: (i, 0)),
            pl.BlockSpec((tm, D_MODEL), lambda j, i: (i, j)),
            pl.BlockSpec((tm, D_MODEL), lambda j, i: (i, 0)),
        ],
        out_specs=[pl.BlockSpec((D_MODEL, D_MODEL), lambda j, i: (j, 0)), pl.BlockSpec((D_MODEL, D_MODEL), lambda j, i: (j, 0))],
        out_shape=[jax.ShapeDtypeStruct((D_FF, D_MODEL), BF), jax.ShapeDtypeStruct((D_FF, D_MODEL), BF)],
        scratch_shapes=[pltpu.VMEM((D_MODEL, D_MODEL), F32), pltpu.VMEM((D_MODEL, D_MODEL), F32)],
        compiler_params=_params(("arbitrary", "arbitrary")),
    )(up, df, dup, a2)


def _mix_out_bwd(dh2, z, y_attn, yc, bch, y, w_out, g_post, g_a, g_c, conv_w, tm, name, exch=None):
    lp = dh2.shape[0]
    nt = lp // tm

    def body(dh2_ref, z_ref, ya_ref, yc_ref, bch_ref, y_ref, w_ref, gp_ref, ga_ref, gc_ref, cw_ref,
             dya_ref, dbch_ref, dg_ref, dwo_ref, ext, acco):
        i = pl.program_id(0)
        dcw_ref = dg_ref.at[ROW_CONV:ROW_CONV + 3, 0:CONV_W]

        @pl.when(i == 0)
        def _():
            ext[tm:tm + 8, :] = jnp.zeros((8, CONV_W), F32)
            dg_ref[...] = jnp.zeros((8, D_MODEL), F32)
            acco[...] = jnp.zeros_like(acco)

        dz, dgp = _rms_bwd(z_ref[...].astype(F32), gp_ref[...], dh2_ref[...])
        dg_ref[ROW_MIX_POST:ROW_MIX_POST + 1, :] += dgp
        dz = dz.astype(BF)
        acco[...] += _dot_tn(y_ref[...], dz)
        dya_n = _dot_nt(dz, w_ref[0:ATTN_W, :])
        dyb_n = _dot_nt(dz, w_ref[ATTN_W:, :])
        dya, dga = _rms_bwd(ya_ref[...].astype(F32), ga_ref[...], dya_n)
        dg_ref[ROW_GROUP_G:ROW_GROUP_G + 1, 0:ATTN_W] += dga
        dya_ref[...] = dya
        b = bch_ref[:, 0:CONV_W].astype(F32)
        c = bch_ref[:, CONV_W:2 * CONV_W].astype(F32)
        hc = bch_ref[:, 2 * CONV_W:3 * CONV_W].astype(F32)
        u = c * hc
        yc_v = yc_ref[...].astype(F32)
        dyconv, dgc = _rms_bwd(b * yc_v, gc_ref[...], dyb_n)
        dg_ref[ROW_GROUP_G:ROW_GROUP_G + 1, ATTN_W:] += dgc
        dbch_ref[:, 0:CONV_W] = (dyconv * yc_v).astype(BF)
        dyc = dyconv * b
        ext[0:tm, :] = dyc
        d1 = ext[1:1 + tm, :]
        d2 = ext[2:2 + tm, :]
        du = cw_ref[2:3, :] * dyc + cw_ref[1:2, :] * d1 + cw_ref[0:1, :] * d2
        ext[tm:tm + 8, :] = dyc[0:8, :]
        dbch_ref[:, CONV_W:2 * CONV_W] = (du * hc).astype(BF)
        dbch_ref[:, 2 * CONV_W:3 * CONV_W] = (du * c).astype(BF)
        dcw_ref[0:1, :] += jnp.sum(u * d2, axis=0, keepdims=True)
        dcw_ref[1:2, :] += jnp.sum(u * d1, axis=0, keepdims=True)
        dcw_ref[2:3, :] += jnp.sum(u * dyc, axis=0, keepdims=True)

        @pl.when(i == nt - 1)
        def _():
            dwo_ref[...] = acco[...].astype(BF)

    row = lambda w: pl.BlockSpec((tm, w), lambda i: (nt - 1 - i, 0))
    return _call(
        body, exch,
        name=name,
        grid=(nt,),
        in_specs=[
            row(D_MODEL), row(D_MODEL), row(ATTN_W), row(CONV_W), row(3 * CONV_W), row(D_MODEL), _full((D_MODEL, D_MODEL)),
            _full((1, D_MODEL)), _full((1, ATTN_W)), _full((1, CONV_W)), _full((8, CONV_W)),
        ],
        out_specs=[row(ATTN_W), row(3 * CONV_W), _full_out((8, D_MODEL)), _full_out((D_MODEL, D_MODEL))],
        out_shape=[
            jax.ShapeDtypeStruct((lp, ATTN_W), F32),
            jax.ShapeDtypeStruct((lp, 3 * CONV_W), BF),
            jax.ShapeDtypeStruct((8, D_MODEL), F32),
            jax.ShapeDtypeStruct((D_MODEL, D_MODEL), BF),
        ],
        scratch_shapes=[pltpu.VMEM((tm + 8, CONV_W), F32), pltpu.VMEM((D_MODEL, D_MODEL), F32)],
        compiler_params=_params(),
    )(dh2, z, y_attn, yc, bch, y, w_out, g_post, g_a, g_c, conv_w)


def _attn_bwd(qkv, o, do, probs, p_sink, rope, name, exch=None):
    lp = qkv.shape[0]
    nb = lp // BLOCK

    def body(q_ref, kvc_ref, kvp_ref, o_ref, do_ref, p_ref, ps_ref, cq_ref, s1q_ref, s2q_ref, ck_ref, s1k_ref, s2k_ref,
             dq_ref, dkv_ref, dsink_ref, carry):
        i = pl.program_id(0)

        @pl.when(i == 0)
        def _():
            carry[...] = jnp.zeros_like(carry)
            dsink_ref[...] = jnp.zeros((8, 128), F32)

        def finish(tot):
            dk = _rope_t(tot[:, :128], ck_ref[...], s1k_ref[...], s2k_ref[...])
            dkv_ref[:, 0:128] = dk.astype(BF)
            dkv_ref[:, 128:256] = tot[:, 128:].astype(BF)

        @pl.when(i < nb)
        def _():
            tri, _ = _fold_masks(i)
            kvc, kvp = kvc_ref[...], kvp_ref[...]
            kk = jnp.concatenate([kvp[:, :128], kvc[:, :128]], axis=0)
            vv = jnp.concatenate([kvp[:, 128:], kvc[:, 128:]], axis=0)
            lane = lax.broadcasted_iota(jnp.int32, (BLOCK, 128), 1)
            lane2 = lax.broadcasted_iota(jnp.int32, (2 * BLOCK, 128), 1)
            rope_q = (cq_ref[...], s1q_ref[...], s2q_ref[...])
            deltas = jnp.zeros((BLOCK, 128), F32)
            folded = []
            for kvh in range(2):
                c0 = 256 * kvh
                q2 = jnp.concatenate([q_ref[:, c0:c0 + 128], q_ref[:, c0 + 128:c0 + 256]], axis=0)
                do2 = jnp.concatenate([do_ref[:, c0:c0 + 128], do_ref[:, c0 + 128:c0 + 256]], axis=0)
                o2 = jnp.concatenate([o_ref[:, c0:c0 + 128], o_ref[:, c0 + 128:c0 + 256]], axis=0).astype(F32)
                k4, v4 = _kv_operand(kk, kvh), _kv_operand(vv, kvh)
                prod = do2 * o2
                dob = do2.astype(BF)
                dp = _dot_nt(dob, v4)
                ds, pb = [], []
                for half in range(2):
                    heads = [4 * kvh + 2 * pair + half for pair in range(2)]
                    p = jnp.concatenate([p_ref[:, 128 * h:128 * (h + 1)] for h in heads], axis=0)
                    sel = (lane2 < HEAD_DIM) if half == 0 else (lane2 >= HEAD_DIM)
                    delta = jnp.sum(jnp.where(sel, prod, 0.0), axis=-1, keepdims=True)
                    dp_h = dp[:, 2 * half * BLOCK:2 * (half + 1) * BLOCK]
                    ds.append((p.astype(F32) * (jnp.where(tri, dp_h[:, :BLOCK], dp_h[:, BLOCK:]) - delta)).astype(BF))
                    pb.append(p)
                    for pair in range(2):
                        deltas = jnp.where(lane == heads[pair], delta[pair * BLOCK:(pair + 1) * BLOCK], deltas)
                ds4, p4 = _split4(ds, tri), _split4(pb, tri)
                dq2 = _dot(ds4, k4) * SCALE
                dq_ref[:, c0:c0 + 128] = _rope_t(dq2[:BLOCK], *rope_q).astype(BF)
                dq_ref[:, c0 + 128:c0 + 256] = _rope_t(dq2[BLOCK:], *rope_q).astype(BF)
                rk, rv = _dot_tn(ds4, q2), _dot_tn(p4, dob)
                own = (lane < HEAD_DIM) if kvh == 0 else (lane >= HEAD_DIM)
                group = []
                for r in (rk, rv):
                    for blk in range(2):
                        t = jnp.where(lane < HEAD_DIM, r[blk * BLOCK:(blk + 1) * BLOCK], r[(2 + blk) * BLOCK:(3 + blk) * BLOCK])
                        group.append(jnp.where(own, t + pltpu.roll(t, HEAD_DIM, 1), 0.0))
                folded.append(group)
            dsink_ref[ROW_SINK:ROW_SINK + 1, :] -= jnp.sum(ps_ref[...] * deltas, axis=0, keepdims=True)
            dk_p, dk_c, dv_p, dv_c = [folded[0][t] + folded[1][t] for t in range(4)]
            finish(carry[...] + jnp.concatenate([dk_p, dv_p], axis=1))
            carry[...] = jnp.concatenate([dk_c, dv_c], axis=1)

        @pl.when(i == nb)
        def _():
            finish(carry[...])

    qi = lambda i: jnp.minimum(i, nb - 1)
    ki = lambda i: jnp.maximum(i - 1, 0)
    tab_q = pl.BlockSpec((BLOCK, 128), lambda i: (qi(i), 0))
    tab_k = pl.BlockSpec((BLOCK, 128), lambda i: (ki(i), 0))
    return _call(
        body, exch,
        name=name,
        grid=(nb + 1,),
        in_specs=[
            pl.BlockSpec((BLOCK, ATTN_W), lambda i: (qi(i), 0)),
            pl.BlockSpec((BLOCK, 256), lambda i: (qi(i), 2)),
            pl.BlockSpec((BLOCK, 256), lambda i: (jnp.maximum(qi(i) - 1, 0), 2)),
            pl.BlockSpec((BLOCK, ATTN_W), lambda i: (qi(i), 0)),
            pl.BlockSpec((BLOCK, ATTN_W), lambda i: (qi(i), 0)),
            pl.BlockSpec((BLOCK, N_Q_HEADS * BLOCK), lambda i: (qi(i), 0)),
            tab_q, tab_q, tab_q, tab_q, tab_k, tab_k, tab_k,
        ],
        out_specs=[
            pl.BlockSpec((BLOCK, ATTN_W), lambda i: (qi(i), 0)),
            pl.BlockSpec((BLOCK, 256), lambda i: (ki(i), 0)),
            pl.BlockSpec((8, 128), lambda i: (0, 0)),
        ],
        out_shape=[
            jax.ShapeDtypeStruct((lp, ATTN_W), BF),
            jax.ShapeDtypeStruct((lp, 256), BF),
            jax.ShapeDtypeStruct((8, 128), F32),
        ],
        scratch_shapes=[pltpu.VMEM((BLOCK, 256), F32)],
        compiler_params=_params(),
    )(qkv, qkv, qkv, o, do, probs, p_sink, *rope, *rope)


def _in_proj_bwd_dx(dq, dkv, dbch, w_in_t, h, dh2, g, tm, name, exch=None):
    lp = h.shape[0]

    def body(dq_ref, dkv_ref, dbch_ref, w_ref, h_ref, dh2_ref, g_ref, dh_ref, dg_ref):
        i = pl.program_id(0)

        @pl.when(i == 0)
        def _():
            dg_ref[...] = jnp.zeros((8, D_MODEL), F32)

        da = _dot(jnp.concatenate([dq_ref[...], dkv_ref[...], dbch_ref[...]], axis=1), w_ref[...])
        dh, dg = _rms_bwd(h_ref[...], g_ref[...], da)
        dg_ref[ROW_MIX_PRE:ROW_MIX_PRE + 1, :] += dg
        dh_ref[...] = dh2_ref[...] + dh

    row = lambda w: pl.BlockSpec((tm, w), lambda i: (i, 0))
    return _call(
        body, exch,
        name=name,
        grid=(lp // tm,),
        in_specs=[row(ATTN_W), row(256), row(3 * CONV_W), _full((IN_W, D_MODEL)), row(D_MODEL), row(D_MODEL), _full((1, D_MODEL))],
        out_specs=[row(D_MODEL), _full_out((8, D_MODEL))],
        out_shape=[jax.ShapeDtypeStruct((lp, D_MODEL), F32), jax.ShapeDtypeStruct((8, D_MODEL), F32)],
        compiler_params=_params(),
    )(dq, dkv, dbch, w_in_t, h, dh2, g)


def _mix_bwd_dw(dq, dkv, dbch, a, tm, name, exch=None):
    lp = a.shape[0]
    nt = lp // tm

    def body(dq_ref, dkv_ref, dbch_ref, a_ref, dwi_ref, acci):
        i = pl.program_id(0)

        @pl.when(i == 0)
        def _():
            acci[...] = jnp.zeros_like(acci)

        a_v = a_ref[...]
        acci[0:512, :] += _dot_tn(dq_ref[...], a_v)
        acci[512:768, :] += _dot_tn(dkv_ref[...], a_v)
        acci[768:, :] += _dot_tn(dbch_ref[...], a_v)

        @pl.when(i == nt - 1)
        def _():
            dwi_ref[...] = acci[...].astype(BF)

    row = lambda w: pl.BlockSpec((tm, w), lambda i: (i, 0))
    return _call(
        body, exch,
        name=name,
        grid=(nt,),
        in_specs=[row(ATTN_W), row(256), row(3 * CONV_W), row(D_MODEL)],
        out_specs=[_full_out((IN_W, D_MODEL))],
        out_shape=[jax.ShapeDtypeStruct((IN_W, D_MODEL), BF)],
        scratch_shapes=[pltpu.VMEM((IN_W, D_MODEL), F32)],
        compiler_params=_params(),
    )(dq, dkv, dbch, a)


def _mesh_place():
    x, y, c = lax.axis_index("x"), lax.axis_index("y"), lax.axis_index("c")
    return x, y, c, 4 * x + 2 * y + c


def _peer(x, y, c, k):
    px = 1 - x if k & 4 else x
    py = 1 - y if k & 2 else y
    pc = 1 - c if k & 1 else c
    return (px, py, pc), 4 * px + 2 * py + pc


SIBLING = 1
SAME_CORE = (2, 4, 6)
OTHER_CORE = (3, 5, 7)


class _Exchange:
    def __init__(self, pieces, forward_lead=8):
        self.forward_lead = forward_lead
        self.srcs = [s for s, _ in pieces]
        self.to_all = [g for _, g in pieces]
        self.n = len(pieces)
        self.land_shapes = [
            jax.ShapeDtypeStruct((N_DEV,) + (s.shape if g else s.shape[1:]), s.dtype) for s, g in pieces]
        self.sem_shapes = [pltpu.SemaphoreType.DMA((self.n, N_DEV - 1)), pltpu.SemaphoreType.DMA((self.n, N_DEV - 1)),
                           pltpu.SemaphoreType.DMA((self.n,))]
        self.forwards = any(self.to_all)

    def _ops(self, srcs, lands, sems):
        send_sems, recv_sems, local_sems = sems
        x, y, c, me = _mesh_place()

        def remote(p, k, src, slot, to):
            return pltpu.make_async_remote_copy(
                src_ref=src, dst_ref=lands[p].at[slot], send_sem=send_sems.at[p, k - 1], recv_sem=recv_sems.at[p, k - 1],
                device_id=to, device_id_type=MESH)

        def own(p):
            return pltpu.make_async_copy(srcs[p] if self.to_all[p] else srcs[p].at[me], lands[p].at[me], local_sems.at[p])

        def direct(p, k):
            peer, pidx = _peer(x, y, c, k)
            return remote(p, k, srcs[p] if self.to_all[p] else srcs[p].at[pidx], me, peer)

        def forward(p, k):
            sibling, _ = _peer(x, y, c, SIBLING)
            _, origin = _peer(x, y, c, k ^ SIBLING)
            return remote(p, k, lands[p].at[origin], origin, sibling)

        def arrival(p, k):
            peer, pidx = _peer(x, y, c, k)
            return remote(p, k, lands[p].at[pidx], pidx, peer)

        return own, direct, forward, arrival

    def start(self, srcs, lands, sems):
        own, direct, _, _ = self._ops(srcs, lands, sems)
        for p in range(self.n):
            own(p).start()
            for k in ((SIBLING,) + SAME_CORE) if self.to_all[p] else range(1, N_DEV):
                direct(p, k).start()

    def forward(self, srcs, lands, sems):
        _, _, forward, arrival = self._ops(srcs, lands, sems)
        for p in range(self.n):
            if self.to_all[p]:
                for k in SAME_CORE:
                    arrival(p, k).wait_recv()
                    forward(p, k ^ SIBLING).start()

    def finish(self, srcs, lands, sems):
        own, direct, forward, arrival = self._ops(srcs, lands, sems)
        for p in range(self.n):
            for k in ((SIBLING,) + OTHER_CORE) if self.to_all[p] else range(1, N_DEV):
                arrival(p, k).wait_recv()
        for p in range(self.n):
            for k in range(1, N_DEV):
                (forward(p, k) if self.to_all[p] and k in OTHER_CORE else direct(p, k)).wait_send()
            own(p).wait()


class _LayerRows:
    def __init__(self, array, layer):
        self.array = array if array.ndim == 3 else array.reshape(DEPTH, 1, -1)
        self.layer = layer

    def spec(self):
        layer = self.layer
        return pl.BlockSpec((None,) + self.array.shape[1:], lambda *_: (layer, 0, 0), pipeline_mode=pl.Buffered(1))


def _call(body, exch, *, name, grid, in_specs, out_specs, out_shape, scratch_shapes=(), compiler_params, after=None):
    def with_layer_rows(args):
        specs = [a.spec() if isinstance(a, _LayerRows) else s for s, a in zip(in_specs, args)]
        return specs, [a.array if isinstance(a, _LayerRows) else a for a in args]

    if exch is None:
        def plain(*args):
            specs, args = with_layer_rows(args)
            return pl.pallas_call(body, name=name, grid=grid, in_specs=specs, out_specs=out_specs, out_shape=out_shape,
                                  scratch_shapes=scratch_shapes, compiler_params=compiler_params)(*args)
        return plain
    n_in, n_out, n_scr, n_x = len(in_specs), len(out_shape), len(scratch_shapes), exch.n
    steps = math.prod(grid)

    def carrying(*refs):
        a, b, c, d, e = n_in, n_in + n_x, n_in + n_x + n_out, n_in + 2 * n_x + n_out, n_in + 2 * n_x + n_out + n_scr
        ins, srcs, outs, lands, scr, sems = refs[:a], refs[a:b], refs[b:c], refs[c:d], refs[d:e], refs[e:]
        step = functools.reduce(lambda acc, t: acc * grid[t] + pl.program_id(t), range(len(grid)), 0)

        @pl.when(step == 0)
        def _():
            exch.start(srcs, lands, sems)

        body(*ins, *outs, *scr)

        if exch.forwards:
            @pl.when(step == max(0, steps - 1 - pl.cdiv(steps, exch.forward_lead)))
            def _():
                exch.forward(srcs, lands, sems)

        @pl.when(step == steps - 1)
        def _():
            exch.finish(srcs, lands, sems)
            if after is not None:
                after(lands, *ins, *outs, *scr)

    hbm = pl.BlockSpec(memory_space=pl.ANY)

    def run(*args):
        specs, args = with_layer_rows(args)
        res = pl.pallas_call(
            carrying, name=name, grid=grid, in_specs=specs + [hbm] * n_x, out_specs=list(out_specs) + [hbm] * n_x,
            out_shape=list(out_shape) + exch.land_shapes, scratch_shapes=list(scratch_shapes) + exch.sem_shapes,
            compiler_params=compiler_params)(*args, *exch.srcs)
        return list(res[:n_out]), list(res[n_out:])

    return run


def _sum_small(part):
    exch = _Exchange([(part, True)])

    def body(part_ref, out_ref, land, *sems):
        exch.start([part_ref], [land], sems)
        exch.forward([part_ref], [land], sems)
        exch.finish([part_ref], [land], sems)
        acc = land[0]
        for d in range(1, N_DEV):
            acc = acc + land[d]
        out_ref[...] = acc

    vmem = pl.BlockSpec(memory_space=pltpu.VMEM)
    return pl.pallas_call(
        body,
        name="sum_small",
        in_specs=[vmem],
        out_specs=vmem,
        out_shape=jax.ShapeDtypeStruct(part.shape, F32),
        scratch_shapes=[pltpu.VMEM(exch.land_shapes[0].shape, F32)] + exch.sem_shapes,
    )(part)


def _adamw(w, g, m, v):
    m = ADAM_B1 * m + (1.0 - ADAM_B1) * g
    v = ADAM_B2 * v + (1.0 - ADAM_B2) * jnp.square(g)
    m_hat = m / (1.0 - ADAM_B1 ** ADAM_STEP)
    v_hat = v / (1.0 - ADAM_B2 ** ADAM_STEP)
    delta = -ADAM_LR * (m_hat / (jnp.sqrt(v_hat) + ADAM_EPS) + ADAM_WD * w)
    return delta, m, v


def _landed_specs(tr, wd):
    return [pl.BlockSpec((N_DEV, tr, wd), lambda l, i, ll=ll: (0, jnp.where(l == ll, i, 0), 0)) for ll in range(DEPTH)]


def _device_sum(r_ref):
    acc = r_ref[0].astype(F32)
    for d in range(1, N_DEV):
        acc = acc + r_ref[d].astype(F32)
    return acc


def _sum_adamw(recv, w, m, v, tr, name, transposed=False):
    _, r, wd = recv[0].shape

    def body(*refs):
        w_ref, m_ref, v_ref, g_ref, d_ref, mo_ref, vo_ref = refs[DEPTH:]
        for ll in range(DEPTH):
            @pl.when(pl.program_id(0) == ll)
            def _(ll=ll):
                g = _device_sum(refs[ll])
                g = g.T if transposed else g
                g_ref[0] = g
                d_ref[0], mo_ref[0], vo_ref[0] = _adamw(w_ref[0], g, m_ref[0], v_ref[0])

    if transposed:
        blk = pl.BlockSpec((1, wd, tr), lambda l, i: (l, 0, i))
        shape = jax.ShapeDtypeStruct((DEPTH, wd, r), F32)
    else:
        blk = pl.BlockSpec((1, tr, wd), lambda l, i: (l, i, 0))
        shape = jax.ShapeDtypeStruct((DEPTH, r, wd), F32)
    return pl.pallas_call(
        body,
        name=name,
        grid=(DEPTH, r // tr),
        in_specs=_landed_specs(tr, wd) + [blk, blk, blk],
        out_specs=[blk] * 4,
        out_shape=[shape] * 4,
        compiler_params=_params(("arbitrary", "arbitrary")),
    )(*recv, w, m, v)


def _adamw_small(ws, gs, ms, vs):
    n = len(ws)

    def body(*refs):
        w_r, g_r, m_r, v_r = refs[:n], refs[n:2 * n], refs[2 * n:3 * n], refs[3 * n:4 * n]
        d_o, m_o, v_o = refs[4 * n:5 * n], refs[5 * n:6 * n], refs[6 * n:7 * n]
        for t in range(n):
            d_o[t][...], m_o[t][...], v_o[t][...] = _adamw(w_r[t][...], g_r[t][...], m_r[t][...], v_r[t][...])

    vmem = pl.BlockSpec(memory_space=pltpu.VMEM)
    shapes = [jax.ShapeDtypeStruct(w.shape, F32) for w in ws]
    outs = pl.pallas_call(
        body,
        name="adamw_small",
        in_specs=[vmem] * (4 * n),
        out_specs=[vmem] * (3 * n),
        out_shape=shapes * 3,
    )(*ws, *gs, *ms, *vs)
    return outs[:n], outs[n:2 * n], outs[2 * n:]


def kernel(x, meta_tokens, mix_pre_g, w_in, conv_w, sinks, attn_out_g, conv_out_g, w_out, mix_post_g, mlp_pre_g, w_up, w_down, mlp_post_g, loss_target, m_meta_tokens, m_mix_pre_g, m_w_in, m_conv_w, m_sinks, m_attn_out_g, m_conv_out_g, m_w_out, m_mix_post_g, m_mlp_pre_g, m_w_up, m_w_down, m_mlp_post_g, v_meta_tokens, v_mix_pre_g, v_w_in, v_conv_w, v_sinks, v_attn_out_g, v_conv_out_g, v_w_out, v_mix_post_g, v_mlp_pre_g, v_w_up, v_w_down, v_mlp_post_g):
    seq = x.shape[1]
    lp = BLOCK + seq
    tm = _row_tile(lp)
    tm_mlp = _row_tile(lp, (320, 256, 128))
    tm_dw_mlp = _row_tile(lp, (1664, 1040, 640, 384, 256, 128))
    tm_dw_mix = _row_tile(lp, (1664, 832, 640, 384, 256, 128))
    me = 4 * lax.axis_index("x") + 2 * lax.axis_index("y") + lax.axis_index("c")
    cshard = CONV_W // N_DEV
    mshard = D_MODEL // N_DEV

    gather_with = {
        ("in_proj_fwd", 0): [("down", 0)], ("attn_fwd", 0): [("out", 0), ("up", 0)],
        ("mlp_fwd", 0): [("in", 1), ("out", 1), ("up", 1), ("down", 1)],
    }
    tight = {("in_proj_fwd", 0)}
    scatter_with = {
        ("attn_bwd", 1): [("down", 1)], ("mix_bwd_dw", 1): [("out", 1)], ("mlp_bwd_dx", 0): [("up", 1), ("in", 1)],
        ("mix_out_bwd", 0): [("up", 0)], ("attn_bwd", 0): [("down", 0)], ("mix_bwd_dw", 0): [("out", 0)],
        ("in_proj_bwd_dx", 0): [("in", 0)],
    }
    shard = {"in": jnp.swapaxes(w_in, 1, 2).astype(BF), "out": w_out.astype(BF),
             "up": jnp.swapaxes(w_up, 1, 2).astype(BF), "down": w_down.astype(BF)}
    weight = {}
    grad = {}
    landed = {}

    def run(fn, kind, l, *args):
        key, name = (kind, l), f"{kind}_{l}"
        if key in gather_with:
            blocks = gather_with[key]
            lead = 16 if key in tight else 8
            outs, lands = fn(*args, name, _Exchange([(shard[n][k], True) for n, k in blocks], lead))
            for b, land in zip(blocks, lands):
                weight[b] = land.reshape(-1, D_MODEL)
            return outs
        if key in scatter_with:
            blocks = scatter_with[key]
            outs, lands = fn(*args, name, _Exchange([(grad[b].reshape(N_DEV, -1, D_MODEL), False) for b in blocks]))
            landed.update(zip(blocks, lands))
            return outs
        return fn(*args, name)

    small = jnp.zeros((24, 128), F32)
    small = small.at[0:N_META, :].set(meta_tokens)
    small = small.at[N_META:N_META + 6, 0:cshard].set(conv_w.reshape(6, cshard))
    first = _Exchange([(shard["in"][0], True), (small, True)], 16)
    h, rope, (first_in, g_small) = _build_h(x[0], _rope_table(lp), tm, first, 1, "build_h")
    weight[("in", 0)] = first_in.reshape(-1, D_MODEL)
    cw = g_small[:, N_META:N_META + 6, 0:cshard].reshape(N_DEV, DEPTH, 3, cshard)
    cw = jnp.transpose(cw, (1, 2, 0, 3)).reshape(DEPTH, 3, CONV_W)
    conv_full = jnp.concatenate([cw, jnp.zeros((DEPTH, 5, CONV_W), F32)], axis=1)

    row1 = _LayerRows

    saved = []
    for l in range(DEPTH):
        a, qkv, bch = run(_in_proj_fwd, "in_proj_fwd", l, h, row1(mix_pre_g, l), weight[("in", l)], rope, tm)
        y_attn, probs, p_sink = run(_attn_fwd, "attn_fwd", l, qkv, sinks[l].reshape(1, -1))
        yc, y, z, h2 = run(_mix_out_fwd, "mix_out_fwd", l, bch, y_attn, h, row1(conv_full, l), row1(attn_out_g, l),
                       row1(conv_out_g, l), weight[("out", l)], row1(mix_post_g, l), tm)
        mlp = _mlp_fwd if l < DEPTH - 1 else functools.partial(_mlp_fwd, target=loss_target[0])
        a2, up, f, *rest = run(mlp, "mlp_fwd", l, h2, row1(mlp_pre_g, l), weight[("up", l)], weight[("down", l)],
                               row1(mlp_post_g, l), tm_mlp)
        saved.append((h, a, qkv, bch, y_attn, probs, p_sink, yc, y, z, h2, a2, up, f))
        h = rest[0]
    dh, loss_part = rest[0], rest[1][0, 0] * (0.5 / D_MODEL)

    gsmall = [None] * DEPTH
    for l in reversed(range(DEPTH)):
        h0, a, qkv, bch, y_attn, probs, p_sink, yc, y, z, h2, a2, up, f = saved[l]
        df, dup, dh2, dg_mlp = run(_mlp_bwd_dx, "mlp_bwd_dx", l, dh, f, up, h2, weight[("down", l)], weight[("up", l)],
                                   row1(mlp_post_g, l), row1(mlp_pre_g, l), tm_mlp)
        grad[("down", l)], grad[("up", l)] = _mlp_bwd_dw(up, df, dup, a2, tm_dw_mlp, f"mlp_bwd_dw_{l}")
        dya, dbch, dg_mix, grad[("out", l)] = run(
            _mix_out_bwd, "mix_out_bwd", l, dh2, z, y_attn, yc, bch, y, weight[("out", l)], row1(mix_post_g, l),
            row1(attn_out_g, l), row1(conv_out_g, l), row1(conv_full, l), tm)
        dq, dkv, dsink = run(_attn_bwd, "attn_bwd", l, qkv, y_attn, dya, probs, p_sink, rope)
        grad[("in", l)], = run(_mix_bwd_dw, "mix_bwd_dw", l, dq, dkv, dbch, a, tm_dw_mix)
        dh, dg_in = run(_in_proj_bwd_dx, "in_proj_bwd_dx", l, dq, dkv, dbch, weight[("in", l)], h0, dh2,
                        row1(mix_pre_g, l), tm)
        tile_a =dg_mlp + dg_in + jnp.pad(dsink, ((0, 0), (0, D_MODEL - 128)))
        gsmall[l] = (tile_a, dg_mix)
    grad_x = dh[BLOCK:][None]

    loss_tile = jnp.zeros((8, D_MODEL), F32).at[ROW_LOSS, 0].set(loss_part)
    tot = _sum_small(jnp.concatenate(
        [gsmall[0][0] + loss_tile, gsmall[0][1], gsmall[1][0], gsmall[1][1], dh[LEAD_PAD:BLOCK]], axis=0))
    loss = tot[ROW_LOSS, 0]
    ta = [tot[16 * l:16 * l + 8] for l in range(DEPTH)]
    tb = [tot[16 * l + 8:16 * l + 16] for l in range(DEPTH)]
    pick = lambda tiles, r0, r1, c0, c1: jnp.stack([t[r0:r1, c0:c1] for t in tiles])
    g_mlp_post = pick(ta, ROW_MLP_POST, ROW_MLP_POST + 1, 0, D_MODEL).reshape(DEPTH, D_MODEL)
    g_mlp_pre = pick(ta, ROW_MLP_PRE, ROW_MLP_PRE + 1, 0, D_MODEL).reshape(DEPTH, D_MODEL)
    g_mix_pre = pick(ta, ROW_MIX_PRE, ROW_MIX_PRE + 1, 0, D_MODEL).reshape(DEPTH, D_MODEL)
    g_sinks = pick(ta, ROW_SINK, ROW_SINK + 1, 0, N_Q_HEADS).reshape(DEPTH, N_Q_HEADS)
    g_mix_post = pick(tb, ROW_MIX_POST, ROW_MIX_POST + 1, 0, D_MODEL).reshape(DEPTH, D_MODEL)
    g_attn_out = pick(tb, ROW_GROUP_G, ROW_GROUP_G + 1, 0, ATTN_W).reshape(DEPTH, ATTN_W)
    g_conv_out = pick(tb, ROW_GROUP_G, ROW_GROUP_G + 1, ATTN_W, D_MODEL).reshape(DEPTH, CONV_W)
    g_conv_full = pick(tb, ROW_CONV, ROW_CONV + 3, 0, CONV_W)
    g_conv = lax.dynamic_slice_in_dim(g_conv_full, me * cshard, cshard, axis=2)
    g_meta = lax.dynamic_slice_in_dim(tot[16 * DEPTH:16 * DEPTH + N_META], me * mshard, mshard, axis=1)

    r_in, r_out, r_up, r_down = [[landed[(n, l)] for l in range(DEPTH)] for n in ("in", "out", "up", "down")]
    t12 = lambda a: jnp.swapaxes(a, 1, 2)
    g_w_in, d_w_in, nm_w_in, nv_w_in = map(t12, _sum_adamw(r_in, t12(w_in), t12(m_w_in), t12(v_w_in), 96, "adamw_w_in"))
    g_w_up, d_w_up, nm_w_up, nv_w_up = _sum_adamw(r_up, w_up, m_w_up, v_w_up, 128, "adamw_w_up", transposed=True)
    g_w_out, d_w_out, nm_w_out, nv_w_out = _sum_adamw(r_out, w_out, m_w_out, v_w_out, 128, "adamw_w_out")
    g_w_down, d_w_down, nm_w_down, nv_w_down = _sum_adamw(r_down, w_down, m_w_down, v_w_down, 128, "adamw_w_down")

    ws = [meta_tokens, mix_pre_g, conv_w.reshape(6, cshard), sinks, attn_out_g, conv_out_g, mix_post_g, mlp_pre_g, mlp_post_g]
    gs = [g_meta, g_mix_pre, g_conv.reshape(6, cshard), g_sinks, g_attn_out, g_conv_out, g_mix_post, g_mlp_pre, g_mlp_post]
    ms = [m_meta_tokens, m_mix_pre_g, m_conv_w.reshape(6, cshard), m_sinks, m_attn_out_g, m_conv_out_g, m_mix_post_g,
          m_mlp_pre_g, m_mlp_post_g]
    vs = [v_meta_tokens, v_mix_pre_g, v_conv_w.reshape(6, cshard), v_sinks, v_attn_out_g, v_conv_out_g, v_mix_post_g,
          v_mlp_pre_g, v_mlp_post_g]
    ds, nms, nvs = _adamw_small(ws, gs, ms, vs)

    def order(meta, mix_pre, cv, sk, a_out, c_out, mix_post, mlp_pre, mlp_post, win, wout, wup, wdown):
        return [meta, mix_pre, win, cv.reshape(DEPTH, 3, cshard), sk, a_out, c_out, wout, mix_post, mlp_pre, wup, wdown, mlp_post]

    grads = order(*gs, g_w_in, g_w_out, g_w_up, g_w_down)
    deltas = order(*ds, d_w_in, d_w_out, d_w_up, d_w_down)
    new_m = order(*nms, nm_w_in, nm_w_out, nm_w_up, nm_w_down)
    new_v = order(*nvs, nv_w_in, nv_w_out, nv_w_up, nv_w_down)
    return (loss, grad_x, *grads, *deltas, *new_m, *new_v)
```

```python
import functools
import math

import jax
import jax.numpy as jnp
from jax import lax
from jax.experimental import pallas as pl
from jax.experimental.pallas import tpu as pltpu

F32 = jnp.float32
BF = jnp.bfloat16

D_MODEL = 1024
ATTN_W = 512
CONV_W = 512
HEAD_DIM = 64
N_Q_HEADS = 8
ROT_DIM = 16
D_FF = 4096
IN_W = 2304
N_META = 16
BLOCK = 128
LEAD_PAD = BLOCK - N_META
ROPE_THETA = 500000.0
EPS = 1e-6
N_DEV = 8
DEPTH = 2
NEG = -1e30
SCALE = HEAD_DIM ** -0.5

ADAM_LR = 0.001
ADAM_B1 = 0.9
ADAM_B2 = 0.999
ADAM_EPS = 1e-08
ADAM_WD = 0.01
ADAM_STEP = 10

ROW_MLP_POST, ROW_MLP_PRE, ROW_MIX_PRE, ROW_SINK, ROW_LOSS = 0, 1, 2, 3, 4
ROW_MIX_POST, ROW_GROUP_G, ROW_CONV = 0, 1, 2

VMEM_LIMIT = 56 * 1024 * 1024
MESH = pl.DeviceIdType.MESH


def _dot(a, b):
    return jnp.dot(a, b, preferred_element_type=F32)


def _dot_nt(a, b):
    return lax.dot_general(a, b, (((1,), (1,)), ((), ())), preferred_element_type=F32)


def _dot_tn(a, b):
    return lax.dot_general(a, b, (((0,), (0,)), ((), ())), preferred_element_type=F32)


def _rms_fwd(x, g):
    r = lax.rsqrt(jnp.mean(x * x, axis=-1, keepdims=True) + EPS)
    return x * r * g


def _rms_bwd(x, g, dy):
    r = lax.rsqrt(jnp.mean(x * x, axis=-1, keepdims=True) + EPS)
    xh = x * r
    t = dy * g
    dx = r * (t - xh * jnp.mean(t * xh, axis=-1, keepdims=True))
    dg = jnp.sum(dy * xh, axis=0, keepdims=True)
    return dx, dg


def _row_tile(lp, cands=(640, 512, 384, 256, 128)):
    for t in cands:
        if lp % t == 0:
            return t
    raise ValueError(f"row count {lp} is not a multiple of 128")


def _full(shape):
    n = len(shape)
    return pl.BlockSpec(shape, lambda *_: (0,) * n, pipeline_mode=pl.Buffered(1))


def _full_out(shape):
    n = len(shape)
    return pl.BlockSpec(shape, lambda *_: (0,) * n)


def _params(sem=("arbitrary",)):
    return pltpu.CompilerParams(dimension_semantics=sem, vmem_limit_bytes=VMEM_LIMIT)


def _rope_table(lp):
    half = ROT_DIM // 2
    pos = jnp.maximum(jnp.arange(lp) - LEAD_PAD, 0).astype(F32)
    inv_freq = jnp.power(jnp.float32(ROPE_THETA), -jnp.arange(0, ROT_DIM, 2, dtype=F32) / ROT_DIM)
    ang_t = jnp.concatenate([inv_freq, inv_freq])[:, None] * pos[None, :]
    row = lax.broadcasted_iota(jnp.int32, (ROT_DIM, lp), 0)
    cs_t = jnp.where(row < half, jnp.cos(ang_t), jnp.sin(ang_t))
    return jnp.pad(cs_t.T, ((0, 0), (0, 128 - ROT_DIM)))


def _rope_coeffs(t):
    half = ROT_DIM // 2
    lane = lax.broadcasted_iota(jnp.int32, t.shape, 1)
    cos_a = jnp.where(lane < half, t, 0.0)
    sin_a = pltpu.roll(jnp.where((lane >= half) & (lane < ROT_DIM), t, 0.0), 128 - half, 1)
    c = cos_a + pltpu.roll(cos_a, half, 1) + jnp.where((lane >= ROT_DIM) & (lane < HEAD_DIM), 1.0, 0.0)
    s2 = pltpu.roll(sin_a, half, 1)
    both = lambda u: u + pltpu.roll(u, HEAD_DIM, 1)
    return both(c), both(-sin_a), both(s2)


def _rope(t, c, s1, s2):
    return t * c + pltpu.roll(t, BLOCK - 8, 1) * s1 + pltpu.roll(t, 8, 1) * s2


def _rope_t(dt, c, s1, s2):
    return dt * c + pltpu.roll(dt * s1, 8, 1) + pltpu.roll(dt * s2, BLOCK - 8, 1)


def _build_h(x, rope_compact, tm, exch, small_piece, name):
    seq = x.shape[0]
    lp = BLOCK + seq
    nt = lp // tm
    n_sub = tm // BLOCK
    small_shape = exch.land_shapes[small_piece].shape

    def body(*refs):
        h_ref, c_ref, s1_ref, s2_ref = refs[n_sub + 1:n_sub + 5]
        for j in range(n_sub):
            h_ref[j * BLOCK:(j + 1) * BLOCK, :] = refs[j][...]
        c_ref[...], s1_ref[...], s2_ref[...] = _rope_coeffs(refs[n_sub][...])

    def after(lands, *refs):
        h_ref, buf = refs[n_sub + 1], refs[n_sub + 5]
        pltpu.sync_copy(lands[small_piece], buf)
        h_ref[0:LEAD_PAD, :] = jnp.zeros((LEAD_PAD, D_MODEL), F32)
        for d in range(N_DEV):
            h_ref[LEAD_PAD:BLOCK, d * 128:(d + 1) * 128] = buf[d, 0:N_META, :]

    tile = lambda i: (i + 1) % nt
    piece = lambda j: pl.BlockSpec((BLOCK, D_MODEL), lambda i: (jnp.maximum(tile(i) * n_sub + j - 1, 0), 0))
    rows = lambda w: pl.BlockSpec((tm, w), lambda i: (tile(i), 0))
    (h, *rope), lands = _call(
        body, exch,
        name=name,
        grid=(nt,),
        in_specs=[piece(j) for j in range(n_sub)] + [rows(128)],
        out_specs=[rows(D_MODEL)] + [rows(128)] * 3,
        out_shape=[jax.ShapeDtypeStruct((lp, D_MODEL), F32)] + [jax.ShapeDtypeStruct((lp, 128), F32)] * 3,
        scratch_shapes=[pltpu.VMEM(small_shape, F32)],
        compiler_params=_params(),
        after=after,
    )(*([x] * n_sub), rope_compact)
    return h, rope, lands


def _in_proj_fwd(h, g, w_in_t, rope, tm, name, exch=None):
    lp = h.shape[0]

    def body(h_ref, g_ref, w_ref, c_ref, s1_ref, s2_ref, a_ref, qkv_ref, bch_ref):
        a = _rms_fwd(h_ref[...], g_ref[...]).astype(BF)
        a_ref[...] = a
        proj = _dot_nt(a, w_ref[...])
        c, s1, s2 = c_ref[...], s1_ref[...], s2_ref[...]
        for j in range(5):
            t = _rope(proj[:, j * 128:(j + 1) * 128], c, s1, s2)
            qkv_ref[:, j * 128:(j + 1) * 128] = (t * SCALE if j < 4 else t).astype(BF)
        qkv_ref[:, 640:768] = proj[:, 640:768].astype(BF)
        bch_ref[...] = proj[:, 768:].astype(BF)

    row = lambda w: pl.BlockSpec((tm, w), lambda i: (i, 0))
    return _call(
        body, exch,
        name=name,
        grid=(lp // tm,),
        in_specs=[row(D_MODEL), _full((1, D_MODEL)), _full((IN_W, D_MODEL)), row(128), row(128), row(128)],
        out_specs=[row(D_MODEL), row(768), row(3 * CONV_W)],
        out_shape=[
            jax.ShapeDtypeStruct((lp, D_MODEL), BF),
            jax.ShapeDtypeStruct((lp, 768), BF),
            jax.ShapeDtypeStruct((lp, 3 * CONV_W), BF),
        ],
        compiler_params=_params(),
    )(h, g, w_in_t, *rope)


def _fold_masks(i):
    r = lax.broadcasted_iota(jnp.int32, (2 * BLOCK, BLOCK), 0) & (BLOCK - 1)
    c = lax.broadcasted_iota(jnp.int32, (2 * BLOCK, BLOCK), 1)
    tri = c > r
    ok = jnp.where(tri, (i - 1) * BLOCK + c, i * BLOCK + c) >= LEAD_PAD
    return tri, ok


def _kv_operand(x, kvh):
    lane = lax.broadcasted_iota(jnp.int32, x.shape, 1)
    zero = jnp.zeros_like(x)
    if kvh == 0:
        lo = jnp.where(lane < HEAD_DIM, x, zero)
        hi = pltpu.roll(lo, HEAD_DIM, 1)
    else:
        hi = jnp.where(lane >= HEAD_DIM, x, zero)
        lo = pltpu.roll(hi, HEAD_DIM, 1)
    return jnp.concatenate([lo, hi], axis=0)


def _split4(t, tri):
    zero = jnp.zeros_like(t[0])
    return jnp.concatenate(
        [jnp.where(tri, t[0], zero), jnp.where(tri, zero, t[0]), jnp.where(tri, t[1], zero), jnp.where(tri, zero, t[1])], axis=1)


def _sink_cols(sink_ref, kvh):
    first = lax.broadcasted_iota(jnp.int32, (2 * BLOCK, 1), 0) < BLOCK
    return [jnp.where(first, sink_ref[0, 4 * kvh + half], sink_ref[0, 4 * kvh + 2 + half]) for half in range(2)]


def _folded_exp(q2, k4, tri, ok, sks):
    s = _dot_nt(q2, k4)
    es, ss = [], []
    for half in range(2):
        s_h = s[:, 2 * half * BLOCK:2 * (half + 1) * BLOCK]
        sf = jnp.where(ok, jnp.where(tri, s_h[:, :BLOCK], s_h[:, BLOCK:]), NEG)
        m = jnp.maximum(jnp.max(sf, axis=-1, keepdims=True), sks[half])
        es.append(jnp.exp(sf - m))
        ss.append(jnp.exp(sks[half] - m))
    sums = _dot(jnp.concatenate(es, axis=0).astype(BF), jnp.ones((BLOCK, BLOCK), BF))
    invs = [1.0 / (sums[2 * half * BLOCK:2 * (half + 1) * BLOCK] + ss[half]) for half in range(2)]
    return es, ss, invs


def _attn_fwd(qkv, sink, name, exch=None):
    lp = qkv.shape[0]
    nb = lp // BLOCK
    per_step = 4

    def one_block(i, sink_ref, q_ref, kvc_ref, kvp_ref, o_ref, p_ref, ps_ref):
        tri, ok = _fold_masks(i)
        kvc, kvp = kvc_ref[...], kvp_ref[...]
        kk = jnp.concatenate([kvp[:, :128], kvc[:, :128]], axis=0)
        vv = jnp.concatenate([kvp[:, 128:], kvc[:, 128:]], axis=0)
        lane = lax.broadcasted_iota(jnp.int32, (BLOCK, 128), 1)
        p_sink = jnp.zeros((BLOCK, 128), F32)
        for kvh in range(2):
            q2 = jnp.concatenate([q_ref[:, 256 * kvh:256 * kvh + 128], q_ref[:, 256 * kvh + 128:256 * kvh + 256]], axis=0)
            es, ss, invs = _folded_exp(q2, _kv_operand(kk, kvh), tri, ok, _sink_cols(sink_ref, kvh))
            pb = [(es[half] * invs[half]).astype(BF) for half in range(2)]
            out = _dot(_split4(pb, tri), _kv_operand(vv, kvh))
            for pair in range(2):
                rows = slice(pair * BLOCK, (pair + 1) * BLOCK)
                o_ref[:, 256 * kvh + 128 * pair:256 * kvh + 128 * (pair + 1)] = out[rows].astype(BF)
                for half in range(2):
                    head = 4 * kvh + 2 * pair + half
                    p_ref[:, 128 * head:128 * (head + 1)] = pb[half][rows]
                    p_sink = jnp.where(lane == head, (ss[half] * invs[half][:, 0:1])[rows], p_sink)
        ps_ref[...] = p_sink

    def body(sink_ref, *refs):
        q_refs, kv_refs = refs[:per_step], refs[per_step:2 * per_step + 1]
        o_ref, p_ref, ps_ref = refs[2 * per_step + 1:]
        for j in range(per_step):
            rows = slice(j * BLOCK, (j + 1) * BLOCK)
            one_block(per_step * pl.program_id(0) + j, sink_ref, q_refs[j], kv_refs[j + 1], kv_refs[j],
                      o_ref.at[rows], p_ref.at[rows], ps_ref.at[rows])

    last = nb - 1
    blk = lambda j: (lambda s: jnp.minimum(per_step * s + j, last))
    out_rows = lambda w: pl.BlockSpec((per_step * BLOCK, w), lambda s: (s, 0))
    return _call(
        body, exch,
        name=name,
        grid=(pl.cdiv(nb, per_step),),
        in_specs=[pl.BlockSpec(memory_space=pltpu.SMEM)]
        + [pl.BlockSpec((BLOCK, ATTN_W), lambda s, j=j: (blk(j)(s), 0)) for j in range(per_step)]
        + [pl.BlockSpec((BLOCK, 256), lambda s: (jnp.maximum(per_step * s - 1, 0), 2))]
        + [pl.BlockSpec((BLOCK, 256), lambda s, j=j: (blk(j)(s), 2)) for j in range(per_step)],
        out_specs=[out_rows(ATTN_W), out_rows(N_Q_HEADS * BLOCK), out_rows(128)],
        out_shape=[jax.ShapeDtypeStruct((lp, ATTN_W), BF), jax.ShapeDtypeStruct((lp, N_Q_HEADS * BLOCK), BF),
                   jax.ShapeDtypeStruct((lp, 128), F32)],
        compiler_params=_params(),
    )(sink, *([qkv] * (2 * per_step + 1)))


def _mix_out_fwd(bch, y_attn, h, conv_w, g_a, g_c, w_out, g_post, tm, name, exch=None):
    lp = h.shape[0]

    def body(bch_ref, ya_ref, h_ref, cw_ref, ga_ref, gc_ref, w_ref, gp_ref, yc_ref, y_ref, z_ref, h2_ref, ext):
        i = pl.program_id(0)

        @pl.when(i == 0)
        def _():
            ext[0:8, :] = jnp.zeros((8, CONV_W), F32)

        b = bch_ref[:, 0:CONV_W].astype(F32)
        u = bch_ref[:, CONV_W:2 * CONV_W].astype(F32) * bch_ref[:, 2 * CONV_W:3 * CONV_W].astype(F32)
        ext[8:8 + tm, :] = u
        yc = cw_ref[0:1, :] * ext[6:6 + tm, :] + cw_ref[1:2, :] * ext[7:7 + tm, :] + cw_ref[2:3, :] * u
        ext[0:8, :] = u[tm - 8:tm, :]
        yc_ref[...] = yc.astype(BF)
        ya = _rms_fwd(ya_ref[...].astype(F32), ga_ref[...]).astype(BF)
        yb = _rms_fwd(b * yc, gc_ref[...]).astype(BF)
        y_ref[:, 0:ATTN_W] = ya
        y_ref[:, ATTN_W:] = yb
        z = _dot(ya, w_ref[0:ATTN_W, :]) + _dot(yb, w_ref[ATTN_W:, :])
        z_ref[...] = z.astype(BF)
        h2_ref[...] = h_ref[...] + _rms_fwd(z, gp_ref[...])

    row = lambda w: pl.BlockSpec((tm, w), lambda i: (i, 0))
    return _call(
        body, exch,
        name=name,
        grid=(lp // tm,),
        in_specs=[
            row(3 * CONV_W), row(ATTN_W), row(D_MODEL), _full((8, CONV_W)), _full((1, ATTN_W)), _full((1, CONV_W)),
            _full((D_MODEL, D_MODEL)), _full((1, D_MODEL)),
        ],
        out_specs=[row(CONV_W), row(D_MODEL), row(D_MODEL), row(D_MODEL)],
        out_shape=[
            jax.ShapeDtypeStruct((lp, CONV_W), BF),
            jax.ShapeDtypeStruct((lp, D_MODEL), BF),
            jax.ShapeDtypeStruct((lp, D_MODEL), BF),
            jax.ShapeDtypeStruct((lp, D_MODEL), F32),
        ],
        scratch_shapes=[pltpu.VMEM((tm + 8, CONV_W), F32)],
        compiler_params=_params(),
    )(bch, y_attn, h, conv_w, g_a, g_c, w_out, g_post)


def _mlp_fwd(h2, g_pre, w_up_t, w_down, g_post, tm, name, exch=None, target=None):
    lp = h2.shape[0]
    sub = math.gcd(tm, BLOCK)
    n_sub, lead = tm // sub, BLOCK // sub
    n_t = n_sub if target is not None else 0

    def body(*refs):
        h_ref, gp_ref, wu_ref, wd_ref, gq_ref = refs[:5]
        t_refs = refs[5:5 + n_t]
        a_ref, up_ref, f_ref, last_ref = refs[5 + n_t:9 + n_t]
        h = h_ref[...]
        a = _rms_fwd(h, gp_ref[...]).astype(BF)
        a_ref[...] = a
        up = _dot_nt(a, wu_ref[...])
        up_ref[...] = up.astype(BF)
        act = jnp.square(jnp.maximum(up, 0.0)).astype(BF)
        f = _dot(act, wd_ref[...])
        f_ref[...] = f
        h3 = h + _rms_fwd(f, gq_ref[...])
        if target is None:
            last_ref[...] = h3
            return
        ls_ref = refs[9 + n_t]
        i = pl.program_id(0)

        @pl.when(i == 0)
        def _():
            ls_ref[...] = jnp.zeros((8, 128), F32)

        sq = jnp.zeros((8, D_MODEL), F32)
        for j in range(n_sub):
            on_tokens = i * n_sub + j >= lead
            d = jnp.where(on_tokens, h3[j * sub:(j + 1) * sub] - t_refs[j][...], 0.0)
            last_ref[j * sub:(j + 1) * sub, :] = d * (1.0 / D_MODEL)
            sq = sq + jnp.sum((d * d).reshape(sub // 8, 8, D_MODEL), axis=0)
        ls_ref[...] += sum(sq[:, k * 128:(k + 1) * 128] for k in range(D_MODEL // 128))

        @pl.when(i == lp // tm - 1)
        def _():
            ls_ref[...] = jnp.full((8, 128), jnp.sum(ls_ref[...]), F32)

    row = lambda w: pl.BlockSpec((tm, w), lambda i: (i, 0))
    piece = lambda j: pl.BlockSpec((sub, D_MODEL), lambda i: (jnp.maximum(i * n_sub + j - lead, 0), 0))
    out_specs = [row(D_MODEL), row(D_FF), row(D_MODEL), row(D_MODEL)]
    out_shape = [
        jax.ShapeDtypeStruct((lp, D_MODEL), BF),
        jax.ShapeDtypeStruct((lp, D_FF), BF),
        jax.ShapeDtypeStruct((lp, D_MODEL), F32),
        jax.ShapeDtypeStruct((lp, D_MODEL), F32),
    ]
    if target is not None:
        out_specs.append(_full_out((8, 128)))
        out_shape.append(jax.ShapeDtypeStruct((8, 128), F32))
    return _call(
        body, exch,
        name=name,
        grid=(lp // tm,),
        in_specs=[row(D_MODEL), _full((1, D_MODEL)), _full((D_FF, D_MODEL)), _full((D_FF, D_MODEL)), _full((1, D_MODEL))]
        + [piece(j) for j in range(n_t)],
        out_specs=out_specs,
        out_shape=out_shape,
        compiler_params=_params(),
    )(h2, g_pre, w_up_t, w_down, g_post, *([target] * n_t))


def _mlp_bwd_dx(dh3, f, up, h2, w_down, w_up_t, g_post, g_pre, tm, name, exch=None):
    lp = h2.shape[0]

    def body(dh3_ref, f_ref, up_ref, h2_ref, wd_ref, wu_ref, gq_ref, gp_ref, df_ref, dup_ref, dh2_ref, dg_ref):
        i = pl.program_id(0)

        @pl.when(i == 0)
        def _():
            dg_ref[...] = jnp.zeros((8, D_MODEL), F32)

        dh3 = dh3_ref[...]
        df, dgq = _rms_bwd(f_ref[...], gq_ref[...], dh3)
        dg_ref[ROW_MLP_POST:ROW_MLP_POST + 1, :] += dgq
        df = df.astype(BF)
        df_ref[...] = df
        dact = _dot_nt(df, wd_ref[...])
        dup = (dact * (2.0 * jnp.maximum(up_ref[...].astype(F32), 0.0))).astype(BF)
        dup_ref[...] = dup
        da = _dot(dup, wu_ref[...])
        dh, dgp = _rms_bwd(h2_ref[...], gp_ref[...], da)
        dg_ref[ROW_MLP_PRE:ROW_MLP_PRE + 1, :] += dgp
        dh2_ref[...] = dh3 + dh

    row = lambda w: pl.BlockSpec((tm, w), lambda i: (i, 0))
    return _call(
        body, exch,
        name=name,
        grid=(lp // tm,),
        in_specs=[
            row(D_MODEL), row(D_MODEL), row(D_FF), row(D_MODEL), _full((D_FF, D_MODEL)), _full((D_FF, D_MODEL)),
            _full((1, D_MODEL)), _full((1, D_MODEL)),
        ],
        out_specs=[row(D_MODEL), row(D_FF), row(D_MODEL), _full_out((8, D_MODEL))],
        out_shape=[
            jax.ShapeDtypeStruct((lp, D_MODEL), BF),
            jax.ShapeDtypeStruct((lp, D_FF), BF),
            jax.ShapeDtypeStruct((lp, D_MODEL), F32),
            jax.ShapeDtypeStruct((8, D_MODEL), F32),
        ],
        compiler_params=_params(),
    )(dh3, f, up, h2, w_down, w_up_t, g_post, g_pre)


def _mlp_bwd_dw(up, df, dup, a2, tm, name):
    lp = up.shape[0]
    nt = lp // tm
    nj = D_FF // D_MODEL

    def body(up_ref, df_ref, dup_ref, a_ref, dwd_ref, dwu_ref, accd, accu):
        i = pl.program_id(1)

        @pl.when(i == 0)
        def _():
            accd[...] = jnp.zeros_like(accd)
            accu[...] = jnp.zeros_like(accu)

        act = jnp.square(jnp.maximum(up_ref[...].astype(F32), 0.0)).astype(BF)
        accd[...] += _dot_tn(act, df_ref[...])
        accu[...] += _dot_tn(dup_ref[...], a_ref[...])

        @pl.when(i == nt - 1)
        def _():
            dwd_ref[...] = accd[...].astype(BF)
            dwu_ref[...] = accu[...].astype(BF)

    return pl.pallas_call(
        body,
        name=name,
        grid=(nj, nt),
        in_specs=[
            pl.BlockSpec((tm, D_MODEL), lambda j, i: (i, j)),
            pl.BlockSpec((tm, D_MODEL), lambda j, i: (i, 0)),
            pl.BlockSpec((tm, D_MODEL), lambda j, i: (i, j)),
            pl.BlockSpec((tm, D_MODEL), lambda j, i: (i, 0)),
        ],
        out_specs=[pl.BlockSpec((D_MODEL, D_MODEL), lambda j, i: (j, 0)), pl.BlockSpec((D_MODEL, D_MODEL), lambda j, i: (j, 0))],
        out_shape=[jax.ShapeDtypeStruct((D_FF, D_MODEL), BF), jax.ShapeDtypeStruct((D_FF, D_MODEL), BF)],
        scratch_shapes=[pltpu.VMEM((D_MODEL, D_MODEL), F32), pltpu.VMEM((D_MODEL, D_MODEL), F32)],
        compiler_params=_params(("arbitrary", "arbitrary")),
    )(up, df, dup, a2)


def _mix_out_bwd(dh2, z, y_attn, yc, bch, y, w_out, g_post, g_a, g_c, conv_w, tm, name, exch=None):
    lp = dh2.shape[0]
    nt = lp // tm

    def body(dh2_ref, z_ref, ya_ref, yc_ref, bch_ref, y_ref, w_ref, gp_ref, ga_ref, gc_ref, cw_ref,
             dya_ref, dbch_ref, dg_ref, dwo_ref, ext, acco):
        i = pl.program_id(0)
        dcw_ref = dg_ref.at[ROW_CONV:ROW_CONV + 3, 0:CONV_W]

        @pl.when(i == 0)
        def _():
            ext[tm:tm + 8, :] = jnp.zeros((8, CONV_W), F32)
            dg_ref[...] = jnp.zeros((8, D_MODEL), F32)
            acco[...] = jnp.zeros_like(acco)

        dz, dgp = _rms_bwd(z_ref[...].astype(F32), gp_ref[...], dh2_ref[...])
        dg_ref[ROW_MIX_POST:ROW_MIX_POST + 1, :] += dgp
        dz = dz.astype(BF)
        acco[...] += _dot_tn(y_ref[...], dz)
        dya_n = _dot_nt(dz, w_ref[0:ATTN_W, :])
        dyb_n = _dot_nt(dz, w_ref[ATTN_W:, :])
        dya, dga = _rms_bwd(ya_ref[...].astype(F32), ga_ref[...], dya_n)
        dg_ref[ROW_GROUP_G:ROW_GROUP_G + 1, 0:ATTN_W] += dga
        dya_ref[...] = dya
        b = bch_ref[:, 0:CONV_W].astype(F32)
        c = bch_ref[:, CONV_W:2 * CONV_W].astype(F32)
        hc = bch_ref[:, 2 * CONV_W:3 * CONV_W].astype(F32)
        u = c * hc
        yc_v = yc_ref[...].astype(F32)
        dyconv, dgc = _rms_bwd(b * yc_v, gc_ref[...], dyb_n)
        dg_ref[ROW_GROUP_G:ROW_GROUP_G + 1, ATTN_W:] += dgc
        dbch_ref[:, 0:CONV_W] = (dyconv * yc_v).astype(BF)
        dyc = dyconv * b
        ext[0:tm, :] = dyc
        d1 = ext[1:1 + tm, :]
        d2 = ext[2:2 + tm, :]
        du = cw_ref[2:3, :] * dyc + cw_ref[1:2, :] * d1 + cw_ref[0:1, :] * d2
        ext[tm:tm + 8, :] = dyc[0:8, :]
        dbch_ref[:, CONV_W:2 * CONV_W] = (du * hc).astype(BF)
        dbch_ref[:, 2 * CONV_W:3 * CONV_W] = (du * c).astype(BF)
        dcw_ref[0:1, :] += jnp.sum(u * d2, axis=0, keepdims=True)
        dcw_ref[1:2, :] += jnp.sum(u * d1, axis=0, keepdims=True)
        dcw_ref[2:3, :] += jnp.sum(u * dyc, axis=0, keepdims=True)

        @pl.when(i == nt - 1)
        def _():
            dwo_ref[...] = acco[...].astype(BF)

    row = lambda w: pl.BlockSpec((tm, w), lambda i: (nt - 1 - i, 0))
    return _call(
        body, exch,
        name=name,
        grid=(nt,),
        in_specs=[
            row(D_MODEL), row(D_MODEL), row(ATTN_W), row(CONV_W), row(3 * CONV_W), row(D_MODEL), _full((D_MODEL, D_MODEL)),
            _full((1, D_MODEL)), _full((1, ATTN_W)), _full((1, CONV_W)), _full((8, CONV_W)),
        ],
        out_specs=[row(ATTN_W), row(3 * CONV_W), _full_out((8, D_MODEL)), _full_out((D_MODEL, D_MODEL))],
        out_shape=[
            jax.ShapeDtypeStruct((lp, ATTN_W), F32),
            jax.ShapeDtypeStruct((lp, 3 * CONV_W), BF),
            jax.ShapeDtypeStruct((8, D_MODEL), F32),
            jax.ShapeDtypeStruct((D_MODEL, D_MODEL), BF),
        ],
        scratch_shapes=[pltpu.VMEM((tm + 8, CONV_W), F32), pltpu.VMEM((D_MODEL, D_MODEL), F32)],
        compiler_params=_params(),
    )(dh2, z, y_attn, yc, bch, y, w_out, g_post, g_a, g_c, conv_w)


def _attn_bwd(qkv, o, do, probs, p_sink, rope, name, exch=None):
    lp = qkv.shape[0]
    nb = lp // BLOCK

    def body(q_ref, kvc_ref, kvp_ref, o_ref, do_ref, p_ref, ps_ref, cq_ref, s1q_ref, s2q_ref, ck_ref, s1k_ref, s2k_ref,
             dq_ref, dkv_ref, dsink_ref, carry):
        i = pl.program_id(0)

        @pl.when(i == 0)
        def _():
            carry[...] = jnp.zeros_like(carry)
            dsink_ref[...] = jnp.zeros((8, 128), F32)

        def finish(tot):
            dk = _rope_t(tot[:, :128], ck_ref[...], s1k_ref[...], s2k_ref[...])
            dkv_ref[:, 0:128] = dk.astype(BF)
            dkv_ref[:, 128:256] = tot[:, 128:].astype(BF)

        @pl.when(i < nb)
        def _():
            tri, _ = _fold_masks(i)
            kvc, kvp = kvc_ref[...], kvp_ref[...]
            kk = jnp.concatenate([kvp[:, :128], kvc[:, :128]], axis=0)
            vv = jnp.concatenate([kvp[:, 128:], kvc[:, 128:]], axis=0)
            lane = lax.broadcasted_iota(jnp.int32, (BLOCK, 128), 1)
            lane2 = lax.broadcasted_iota(jnp.int32, (2 * BLOCK, 128), 1)
            rope_q = (cq_ref[...], s1q_ref[...], s2q_ref[...])
            deltas = jnp.zeros((BLOCK, 128), F32)
            folded = []
            for kvh in range(2):
                c0 = 256 * kvh
                q2 = jnp.concatenate([q_ref[:, c0:c0 + 128], q_ref[:, c0 + 128:c0 + 256]], axis=0)
                do2 = jnp.concatenate([do_ref[:, c0:c0 + 128], do_ref[:, c0 + 128:c0 + 256]], axis=0)
                o2 = jnp.concatenate([o_ref[:, c0:c0 + 128], o_ref[:, c0 + 128:c0 + 256]], axis=0).astype(F32)
                k4, v4 = _kv_operand(kk, kvh), _kv_operand(vv, kvh)
                prod = do2 * o2
                dob = do2.astype(BF)
                dp = _dot_nt(dob, v4)
                ds, pb = [], []
                for half in range(2):
                    heads = [4 * kvh + 2 * pair + half for pair in range(2)]
                    p = jnp.concatenate([p_ref[:, 128 * h:128 * (h + 1)] for h in heads], axis=0)
                    sel = (lane2 < HEAD_DIM) if half == 0 else (lane2 >= HEAD_DIM)
                    delta = jnp.sum(jnp.where(sel, prod, 0.0), axis=-1, keepdims=True)
                    dp_h = dp[:, 2 * half * BLOCK:2 * (half + 1) * BLOCK]
                    ds.append((p.astype(F32) * (jnp.where(tri, dp_h[:, :BLOCK], dp_h[:, BLOCK:]) - delta)).astype(BF))
                    pb.append(p)
                    for pair in range(2):
                        deltas = jnp.where(lane == heads[pair], delta[pair * BLOCK:(pair + 1) * BLOCK], deltas)
                ds4, p4 = _split4(ds, tri), _split4(pb, tri)
                dq2 = _dot(ds4, k4) * SCALE
                dq_ref[:, c0:c0 + 128] = _rope_t(dq2[:BLOCK], *rope_q).astype(BF)
                dq_ref[:, c0 + 128:c0 + 256] = _rope_t(dq2[BLOCK:], *rope_q).astype(BF)
                rk, rv = _dot_tn(ds4, q2), _dot_tn(p4, dob)
                own = (lane < HEAD_DIM) if kvh == 0 else (lane >= HEAD_DIM)
                group = []
                for r in (rk, rv):
                    for blk in range(2):
                        t = jnp.where(lane < HEAD_DIM, r[blk * BLOCK:(blk + 1) * BLOCK], r[(2 + blk) * BLOCK:(3 + blk) * BLOCK])
                        group.append(jnp.where(own, t + pltpu.roll(t, HEAD_DIM, 1), 0.0))
                folded.append(group)
            dsink_ref[ROW_SINK:ROW_SINK + 1, :] -= jnp.sum(ps_ref[...] * deltas, axis=0, keepdims=True)
            dk_p, dk_c, dv_p, dv_c = [folded[0][t] + folded[1][t] for t in range(4)]
            finish(carry[...] + jnp.concatenate([dk_p, dv_p], axis=1))
            carry[...] = jnp.concatenate([dk_c, dv_c], axis=1)

        @pl.when(i == nb)
        def _():
            finish(carry[...])

    qi = lambda i: jnp.minimum(i, nb - 1)
    ki = lambda i: jnp.maximum(i - 1, 0)
    tab_q = pl.BlockSpec((BLOCK, 128), lambda i: (qi(i), 0))
    tab_k = pl.BlockSpec((BLOCK, 128), lambda i: (ki(i), 0))
    return _call(
        body, exch,
        name=name,
        grid=(nb + 1,),
        in_specs=[
            pl.BlockSpec((BLOCK, ATTN_W), lambda i: (qi(i), 0)),
            pl.BlockSpec((BLOCK, 256), lambda i: (qi(i), 2)),
            pl.BlockSpec((BLOCK, 256), lambda i: (jnp.maximum(qi(i) - 1, 0), 2)),
            pl.BlockSpec((BLOCK, ATTN_W), lambda i: (qi(i), 0)),
            pl.BlockSpec((BLOCK, ATTN_W), lambda i: (qi(i), 0)),
            pl.BlockSpec((BLOCK, N_Q_HEADS * BLOCK), lambda i: (qi(i), 0)),
            tab_q, tab_q, tab_q, tab_q, tab_k, tab_k, tab_k,
        ],
        out_specs=[
            pl.BlockSpec((BLOCK, ATTN_W), lambda i: (qi(i), 0)),
            pl.BlockSpec((BLOCK, 256), lambda i: (ki(i), 0)),
            pl.BlockSpec((8, 128), lambda i: (0, 0)),
        ],
        out_shape=[
            jax.ShapeDtypeStruct((lp, ATTN_W), BF),
            jax.ShapeDtypeStruct((lp, 256), BF),
            jax.ShapeDtypeStruct((8, 128), F32),
        ],
        scratch_shapes=[pltpu.VMEM((BLOCK, 256), F32)],
        compiler_params=_params(),
    )(qkv, qkv, qkv, o, do, probs, p_sink, *rope, *rope)


def _in_proj_bwd_dx(dq, dkv, dbch, w_in_t, h, dh2, g, tm, name, exch=None):
    lp = h.shape[0]

    def body(dq_ref, dkv_ref, dbch_ref, w_ref, h_ref, dh2_ref, g_ref, dh_ref, dg_ref):
        i = pl.program_id(0)

        @pl.when(i == 0)
        def _():
            dg_ref[...] = jnp.zeros((8, D_MODEL), F32)

        da = _dot(jnp.concatenate([dq_ref[...], dkv_ref[...], dbch_ref[...]], axis=1), w_ref[...])
        dh, dg = _rms_bwd(h_ref[...], g_ref[...], da)
        dg_ref[ROW_MIX_PRE:ROW_MIX_PRE + 1, :] += dg
        dh_ref[...] = dh2_ref[...] + dh

    row = lambda w: pl.BlockSpec((tm, w), lambda i: (i, 0))
    return _call(
        body, exch,
        name=name,
        grid=(lp // tm,),
        in_specs=[row(ATTN_W), row(256), row(3 * CONV_W), _full((IN_W, D_MODEL)), row(D_MODEL), row(D_MODEL), _full((1, D_MODEL))],
        out_specs=[row(D_MODEL), _full_out((8, D_MODEL))],
        out_shape=[jax.ShapeDtypeStruct((lp, D_MODEL), F32), jax.ShapeDtypeStruct((8, D_MODEL), F32)],
        compiler_params=_params(),
    )(dq, dkv, dbch, w_in_t, h, dh2, g)


def _mix_bwd_dw(dq, dkv, dbch, a, tm, name, exch=None):
    lp = a.shape[0]
    nt = lp // tm

    def body(dq_ref, dkv_ref, dbch_ref, a_ref, dwi_ref, acci):
        i = pl.program_id(0)

        @pl.when(i == 0)
        def _():
            acci[...] = jnp.zeros_like(acci)

        a_v = a_ref[...]
        acci[0:512, :] += _dot_tn(dq_ref[...], a_v)
        acci[512:768, :] += _dot_tn(dkv_ref[...], a_v)
        acci[768:, :] += _dot_tn(dbch_ref[...], a_v)

        @pl.when(i == nt - 1)
        def _():
            dwi_ref[...] = acci[...].astype(BF)

    row = lambda w: pl.BlockSpec((tm, w), lambda i: (i, 0))
    return _call(
        body, exch,
        name=name,
        grid=(nt,),
        in_specs=[row(ATTN_W), row(256), row(3 * CONV_W), row(D_MODEL)],
        out_specs=[_full_out((IN_W, D_MODEL))],
        out_shape=[jax.ShapeDtypeStruct((IN_W, D_MODEL), BF)],
        scratch_shapes=[pltpu.VMEM((IN_W, D_MODEL), F32)],
        compiler_params=_params(),
    )(dq, dkv, dbch, a)


def _mesh_place():
    x, y, c = lax.axis_index("x"), lax.axis_index("y"), lax.axis_index("c")
    return x, y, c, 4 * x + 2 * y + c


def _peer(x, y, c, k):
    px = 1 - x if k & 4 else x
    py = 1 - y if k & 2 else y
    pc = 1 - c if k & 1 else c
    return (px, py, pc), 4 * px + 2 * py + pc


SIBLING = 1
SAME_CORE = (2, 4, 6)
OTHER_CORE = (3, 5, 7)


class _Exchange:
    def __init__(self, pieces, forward_lead=8):
        self.forward_lead = forward_lead
        self.srcs = [s for s, _ in pieces]
        self.to_all = [g for _, g in pieces]
        self.n = len(pieces)
        self.land_shapes = [
            jax.ShapeDtypeStruct((N_DEV,) + (s.shape if g else s.shape[1:]), s.dtype) for s, g in pieces]
        self.sem_shapes = [pltpu.SemaphoreType.DMA((self.n, N_DEV - 1)), pltpu.SemaphoreType.DMA((self.n, N_DEV - 1)),
                           pltpu.SemaphoreType.DMA((self.n,))]
        self.forwards = any(self.to_all)

    def _ops(self, srcs, lands, sems):
        send_sems, recv_sems, local_sems = sems
        x, y, c, me = _mesh_place()

        def remote(p, k, src, slot, to):
            return pltpu.make_async_remote_copy(
                src_ref=src, dst_ref=lands[p].at[slot], send_sem=send_sems.at[p, k - 1], recv_sem=recv_sems.at[p, k - 1],
                device_id=to, device_id_type=MESH)

        def own(p):
            return pltpu.make_async_copy(srcs[p] if self.to_all[p] else srcs[p].at[me], lands[p].at[me], local_sems.at[p])

        def direct(p, k):
            peer, pidx = _peer(x, y, c, k)
            return remote(p, k, srcs[p] if self.to_all[p] else srcs[p].at[pidx], me, peer)

        def forward(p, k):
            sibling, _ = _peer(x, y, c, SIBLING)
            _, origin = _peer(x, y, c, k ^ SIBLING)
            return remote(p, k, lands[p].at[origin], origin, sibling)

        def arrival(p, k):
            peer, pidx = _peer(x, y, c, k)
            return remote(p, k, lands[p].at[pidx], pidx, peer)

        return own, direct, forward, arrival

    def start(self, srcs, lands, sems):
        own, direct, _, _ = self._ops(srcs, lands, sems)
        for p in range(self.n):
            own(p).start()
            for k in ((SIBLING,) + SAME_CORE) if self.to_all[p] else range(1, N_DEV):
                direct(p, k).start()

    def forward(self, srcs, lands, sems):
        _, _, forward, arrival = self._ops(srcs, lands, sems)
        for p in range(self.n):
            if self.to_all[p]:
                for k in SAME_CORE:
                    arrival(p, k).wait_recv()
                    forward(p, k ^ SIBLING).start()

    def finish(self, srcs, lands, sems):
        own, direct, forward, arrival = self._ops(srcs, lands, sems)
        for p in range(self.n):
            for k in ((SIBLING,) + OTHER_CORE) if self.to_all[p] else range(1, N_DEV):
                arrival(p, k).wait_recv()
        for p in range(self.n):
            for k in range(1, N_DEV):
                (forward(p, k) if self.to_all[p] and k in OTHER_CORE else direct(p, k)).wait_send()
            own(p).wait()


class _LayerRows:
    def __init__(self, array, layer):
        self.array = array if array.ndim == 3 else array.reshape(DEPTH, 1, -1)
        self.layer = layer

    def spec(self):
        layer = self.layer
        return pl.BlockSpec((None,) + self.array.shape[1:], lambda *_: (layer, 0, 0), pipeline_mode=pl.Buffered(1))


def _call(body, exch, *, name, grid, in_specs, out_specs, out_shape, scratch_shapes=(), compiler_params, after=None):
    def with_layer_rows(args):
        specs = [a.spec() if isinstance(a, _LayerRows) else s for s, a in zip(in_specs, args)]
        return specs, [a.array if isinstance(a, _LayerRows) else a for a in args]

    if exch is None:
        def plain(*args):
            specs, args = with_layer_rows(args)
            return pl.pallas_call(body, name=name, grid=grid, in_specs=specs, out_specs=out_specs, out_shape=out_shape,
                                  scratch_shapes=scratch_shapes, compiler_params=compiler_params)(*args)
        return plain
    n_in, n_out, n_scr, n_x = len(in_specs), len(out_shape), len(scratch_shapes), exch.n
    steps = math.prod(grid)

    def carrying(*refs):
        a, b, c, d, e = n_in, n_in + n_x, n_in + n_x + n_out, n_in + 2 * n_x + n_out, n_in + 2 * n_x + n_out + n_scr
        ins, srcs, outs, lands, scr, sems = refs[:a], refs[a:b], refs[b:c], refs[c:d], refs[d:e], refs[e:]
        step = functools.reduce(lambda acc, t: acc * grid[t] + pl.program_id(t), range(len(grid)), 0)

        @pl.when(step == 0)
        def _():
            exch.start(srcs, lands, sems)

        body(*ins, *outs, *scr)

        if exch.forwards:
            @pl.when(step == max(0, steps - 1 - pl.cdiv(steps, exch.forward_lead)))
            def _():
                exch.forward(srcs, lands, sems)

        @pl.when(step == steps - 1)
        def _():
            exch.finish(srcs, lands, sems)
            if after is not None:
                after(lands, *ins, *outs, *scr)

    hbm = pl.BlockSpec(memory_space=pl.ANY)

    def run(*args):
        specs, args = with_layer_rows(args)
        res = pl.pallas_call(
            carrying, name=name, grid=grid, in_specs=specs + [hbm] * n_x, out_specs=list(out_specs) + [hbm] * n_x,
            out_shape=list(out_shape) + exch.land_shapes, scratch_shapes=list(scratch_shapes) + exch.sem_shapes,
            compiler_params=compiler_params)(*args, *exch.srcs)
        return list(res[:n_out]), list(res[n_out:])

    return run


def _sum_small(part):
    exch = _Exchange([(part, True)])

    def body(part_ref, out_ref, land, *sems):
        exch.start([part_ref], [land], sems)
        exch.forward([part_ref], [land], sems)
        exch.finish([part_ref], [land], sems)
        acc = land[0]
        for d in range(1, N_DEV):
            acc = acc + land[d]
        out_ref[...] = acc

    vmem = pl.BlockSpec(memory_space=pltpu.VMEM)
    return pl.pallas_call(
        body,
        name="sum_small",
        in_specs=[vmem],
        out_specs=vmem,
        out_shape=jax.ShapeDtypeStruct(part.shape, F32),
        scratch_shapes=[pltpu.VMEM(exch.land_shapes[0].shape, F32)] + exch.sem_shapes,
    )(part)


def _adamw(w, g, m, v):
    m = ADAM_B1 * m + (1.0 - ADAM_B1) * g
    v = ADAM_B2 * v + (1.0 - ADAM_B2) * jnp.square(g)
    m_hat = m / (1.0 - ADAM_B1 ** ADAM_STEP)
    v_hat = v / (1.0 - ADAM_B2 ** ADAM_STEP)
    delta = -ADAM_LR * (m_hat / (jnp.sqrt(v_hat) + ADAM_EPS) + ADAM_WD * w)
    return delta, m, v


def _landed_specs(tr, wd):
    return [pl.BlockSpec((N_DEV, tr, wd), lambda l, i, ll=ll: (0, jnp.where(l == ll, i, 0), 0)) for ll in range(DEPTH)]


def _device_sum(r_ref):
    acc = r_ref[0].astype(F32)
    for d in range(1, N_DEV):
        acc = acc + r_ref[d].astype(F32)
    return acc


def _sum_adamw(recv, w, m, v, tr, name, transposed=False):
    _, r, wd = recv[0].shape

    def body(*refs):
        w_ref, m_ref, v_ref, g_ref, d_ref, mo_ref, vo_ref = refs[DEPTH:]
        for ll in range(DEPTH):
            @pl.when(pl.program_id(0) == ll)
            def _(ll=ll):
                g = _device_sum(refs[ll])
                g = g.T if transposed else g
                g_ref[0] = g
                d_ref[0], mo_ref[0], vo_ref[0] = _adamw(w_ref[0], g, m_ref[0], v_ref[0])

    if transposed:
        blk = pl.BlockSpec((1, wd, tr), lambda l, i: (l, 0, i))
        shape = jax.ShapeDtypeStruct((DEPTH, wd, r), F32)
    else:
        blk = pl.BlockSpec((1, tr, wd), lambda l, i: (l, i, 0))
        shape = jax.ShapeDtypeStruct((DEPTH, r, wd), F32)
    return pl.pallas_call(
        body,
        name=name,
        grid=(DEPTH, r // tr),
        in_specs=_landed_specs(tr, wd) + [blk, blk, blk],
        out_specs=[blk] * 4,
        out_shape=[shape] * 4,
        compiler_params=_params(("arbitrary", "arbitrary")),
    )(*recv, w, m, v)


def _adamw_small(ws, gs, ms, vs):
    n = len(ws)

    def body(*refs):
        w_r, g_r, m_r, v_r = refs[:n], refs[n:2 * n], refs[2 * n:3 * n], refs[3 * n:4 * n]
        d_o, m_o, v_o = refs[4 * n:5 * n], refs[5 * n:6 * n], refs[6 * n:7 * n]
        for t in range(n):
            d_o[t][...], m_o[t][...], v_o[t][...] = _adamw(w_r[t][...], g_r[t][...], m_r[t][...], v_r[t][...])

    vmem = pl.BlockSpec(memory_space=pltpu.VMEM)
    shapes = [jax.ShapeDtypeStruct(w.shape, F32) for w in ws]
    outs = pl.pallas_call(
        body,
        name="adamw_small",
        in_specs=[vmem] * (4 * n),
        out_specs=[vmem] * (3 * n),
        out_shape=shapes * 3,
    )(*ws, *gs, *ms, *vs)
    return outs[:n], outs[n:2 * n], outs[2 * n:]


def kernel(x, meta_tokens, mix_pre_g, w_in, conv_w, sinks, attn_out_g, conv_out_g, w_out, mix_post_g, mlp_pre_g, w_up, w_down, mlp_post_g, loss_target, m_meta_tokens, m_mix_pre_g, m_w_in, m_conv_w, m_sinks, m_attn_out_g, m_conv_out_g, m_w_out, m_mix_post_g, m_mlp_pre_g, m_w_up, m_w_down, m_mlp_post_g, v_meta_tokens, v_mix_pre_g, v_w_in, v_conv_w, v_sinks, v_attn_out_g, v_conv_out_g, v_w_out, v_mix_post_g, v_mlp_pre_g, v_w_up, v_w_down, v_mlp_post_g):
    seq = x.shape[1]
    lp = BLOCK + seq
    tm = _row_tile(lp)
    tm_mlp = _row_tile(lp, (320, 256, 128))
    tm_dw_mlp = _row_tile(lp, (1664, 1040, 640, 384, 256, 128))
    tm_dw_mix = _row_tile(lp, (1664, 832, 640, 384, 256, 128))
    me = 4 * lax.axis_index("x") + 2 * lax.axis_index("y") + lax.axis_index("c")
    cshard = CONV_W // N_DEV
    mshard = D_MODEL // N_DEV

    gather_with = {
        ("in_proj_fwd", 0): [("down", 0)], ("attn_fwd", 0): [("out", 0), ("up", 0)],
        ("mlp_fwd", 0): [("in", 1), ("out", 1), ("up", 1), ("down", 1)],
    }
    tight = {("in_proj_fwd", 0), ("attn_fwd", 0)}
    scatter_with = {
        ("attn_bwd", 1): [("down", 1)], ("mix_bwd_dw", 1): [("out", 1)], ("mlp_bwd_dx", 0): [("up", 1), ("in", 1)],
        ("mix_out_bwd", 0): [("up", 0)], ("attn_bwd", 0): [("down", 0)], ("mix_bwd_dw", 0): [("out", 0)],
        ("in_proj_bwd_dx", 0): [("in", 0)],
    }
    shard = {"in": jnp.swapaxes(w_in, 1, 2).astype(BF), "out": w_out.astype(BF),
             "up": jnp.swapaxes(w_up, 1, 2).astype(BF), "down": w_down.astype(BF)}
    weight = {}
    grad = {}
    landed = {}

    def run(fn, kind, l, *args):
        key, name = (kind, l), f"{kind}_{l}"
        if key in gather_with:
            blocks = gather_with[key]
            lead = 16 if key in tight else 8
            outs, lands = fn(*args, name, _Exchange([(shard[n][k], True) for n, k in blocks], lead))
            for b, land in zip(blocks, lands):
                weight[b] = land.reshape(-1, D_MODEL)
            return outs
        if key in scatter_with:
            blocks = scatter_with[key]
            outs, lands = fn(*args, name, _Exchange([(grad[b].reshape(N_DEV, -1, D_MODEL), False) for b in blocks]))
            landed.update(zip(blocks, lands))
            return outs
        return fn(*args, name)

    small = jnp.zeros((24, 128), F32)
    small = small.at[0:N_META, :].set(meta_tokens)
    small = small.at[N_META:N_META + 6, 0:cshard].set(conv_w.reshape(6, cshard))
    first = _Exchange([(shard["in"][0], True), (small, True)], 16)
    h, rope, (first_in, g_small) = _build_h(x[0], _rope_table(lp), tm, first, 1, "build_h")
    weight[("in", 0)] = first_in.reshape(-1, D_MODEL)
    cw = g_small[:, N_META:N_META + 6, 0:cshard].reshape(N_DEV, DEPTH, 3, cshard)
    cw = jnp.transpose(cw, (1, 2, 0, 3)).reshape(DEPTH, 3, CONV_W)
    conv_full = jnp.concatenate([cw, jnp.zeros((DEPTH, 5, CONV_W), F32)], axis=1)

    row1 = _LayerRows

    saved = []
    for l in range(DEPTH):
        a, qkv, bch = run(_in_proj_fwd, "in_proj_fwd", l, h, row1(mix_pre_g, l), weight[("in", l)], rope, tm)
        y_attn, probs, p_sink = run(_attn_fwd, "attn_fwd", l, qkv, sinks[l].reshape(1, -1))
        yc, y, z, h2 = run(_mix_out_fwd, "mix_out_fwd", l, bch, y_attn, h, row1(conv_full, l), row1(attn_out_g, l),
                       row1(conv_out_g, l), weight[("out", l)], row1(mix_post_g, l), tm)
        mlp = _mlp_fwd if l < DEPTH - 1 else functools.partial(_mlp_fwd, target=loss_target[0])
        a2, up, f, *rest = run(mlp, "mlp_fwd", l, h2, row1(mlp_pre_g, l), weight[("up", l)], weight[("down", l)],
                               row1(mlp_post_g, l), tm_mlp)
        saved.append((h, a, qkv, bch, y_attn, probs, p_sink, yc, y, z, h2, a2, up, f))
        h = rest[0]
    dh, loss_part = rest[0], rest[1][0, 0] * (0.5 / D_MODEL)

    gsmall = [None] * DEPTH
    for l in reversed(range(DEPTH)):
        h0, a, qkv, bch, y_attn, probs, p_sink, yc, y, z, h2, a2, up, f = saved[l]
        df, dup, dh2, dg_mlp = run(_mlp_bwd_dx, "mlp_bwd_dx", l, dh, f, up, h2, weight[("down", l)], weight[("up", l)],
                                   row1(mlp_post_g, l), row1(mlp_pre_g, l), tm_mlp)
        grad[("down", l)], grad[("up", l)] = _mlp_bwd_dw(up, df, dup, a2, tm_dw_mlp, f"mlp_bwd_dw_{l}")
        dya, dbch, dg_mix, grad[("out", l)] = run(
            _mix_out_bwd, "mix_out_bwd", l, dh2, z, y_attn, yc, bch, y, weight[("out", l)], row1(mix_post_g, l),
            row1(attn_out_g, l), row1(conv_out_g, l), row1(conv_full, l), tm)
        dq, dkv, dsink = run(_attn_bwd, "attn_bwd", l, qkv, y_attn, dya, probs, p_sink, rope)
        grad[("in", l)], = run(_mix_bwd_dw, "mix_bwd_dw", l, dq, dkv, dbch, a, tm_dw_mix)
        dh, dg_in = run(_in_proj_bwd_dx, "in_proj_bwd_dx", l, dq, dkv, dbch, weight[("in", l)], h0, dh2,
                        row1(mix_pre_g, l), tm)
        tile_a =dg_mlp + dg_in + jnp.pad(dsink, ((0, 0), (0, D_MODEL - 128)))
        gsmall[l] = (tile_a, dg_mix)
    grad_x = dh[BLOCK:][None]

    loss_tile = jnp.zeros((8, D_MODEL), F32).at[ROW_LOSS, 0].set(loss_part)
    tot = _sum_small(jnp.concatenate(
        [gsmall[0][0] + loss_tile, gsmall[0][1], gsmall[1][0], gsmall[1][1], dh[LEAD_PAD:BLOCK]], axis=0))
    loss = tot[ROW_LOSS, 0]
    ta = [tot[16 * l:16 * l + 8] for l in range(DEPTH)]
    tb = [tot[16 * l + 8:16 * l + 16] for l in range(DEPTH)]
    pick = lambda tiles, r0, r1, c0, c1: jnp.stack([t[r0:r1, c0:c1] for t in tiles])
    g_mlp_post = pick(ta, ROW_MLP_POST, ROW_MLP_POST + 1, 0, D_MODEL).reshape(DEPTH, D_MODEL)
    g_mlp_pre = pick(ta, ROW_MLP_PRE, ROW_MLP_PRE + 1, 0, D_MODEL).reshape(DEPTH, D_MODEL)
    g_mix_pre = pick(ta, ROW_MIX_PRE, ROW_MIX_PRE + 1, 0, D_MODEL).reshape(DEPTH, D_MODEL)
    g_sinks = pick(ta, ROW_SINK, ROW_SINK + 1, 0, N_Q_HEADS).reshape(DEPTH, N_Q_HEADS)
    g_mix_post = pick(tb, ROW_MIX_POST, ROW_MIX_POST + 1, 0, D_MODEL).reshape(DEPTH, D_MODEL)
    g_attn_out = pick(tb, ROW_GROUP_G, ROW_GROUP_G + 1, 0, ATTN_W).reshape(DEPTH, ATTN_W)
    g_conv_out = pick(tb, ROW_GROUP_G, ROW_GROUP_G + 1, ATTN_W, D_MODEL).reshape(DEPTH, CONV_W)
    g_conv_full = pick(tb, ROW_CONV, ROW_CONV + 3, 0, CONV_W)
    g_conv = lax.dynamic_slice_in_dim(g_conv_full, me * cshard, cshard, axis=2)
    g_meta = lax.dynamic_slice_in_dim(tot[16 * DEPTH:16 * DEPTH + N_META], me * mshard, mshard, axis=1)

    r_in, r_out, r_up, r_down = [[landed[(n, l)] for l in range(DEPTH)] for n in ("in", "out", "up", "down")]
    t12 = lambda a: jnp.swapaxes(a, 1, 2)
    g_w_in, d_w_in, nm_w_in, nv_w_in = map(t12, _sum_adamw(r_in, t12(w_in), t12(m_w_in), t12(v_w_in), 96, "adamw_w_in"))
    g_w_up, d_w_up, nm_w_up, nv_w_up = _sum_adamw(r_up, w_up, m_w_up, v_w_up, 128, "adamw_w_up", transposed=True)
    g_w_out, d_w_out, nm_w_out, nv_w_out = _sum_adamw(r_out, w_out, m_w_out, v_w_out, 128, "adamw_w_out")
    g_w_down, d_w_down, nm_w_down, nv_w_down = _sum_adamw(r_down, w_down, m_w_down, v_w_down, 128, "adamw_w_down")

    ws = [meta_tokens, mix_pre_g, conv_w.reshape(6, cshard), sinks, attn_out_g, conv_out_g, mix_post_g, mlp_pre_g, mlp_post_g]
    gs = [g_meta, g_mix_pre, g_conv.reshape(6, cshard), g_sinks, g_attn_out, g_conv_out, g_mix_post, g_mlp_pre, g_mlp_post]
    ms = [m_meta_tokens, m_mix_pre_g, m_conv_w.reshape(6, cshard), m_sinks, m_attn_out_g, m_conv_out_g, m_mix_post_g,
          m_mlp_pre_g, m_mlp_post_g]
    vs = [v_meta_tokens, v_mix_pre_g, v_conv_w.reshape(6, cshard), v_sinks, v_attn_out_g, v_conv_out_g, v_mix_post_g,
          v_mlp_pre_g, v_mlp_post_g]
    ds, nms, nvs = _adamw_small(ws, gs, ms, vs)

    def order(meta, mix_pre, cv, sk, a_out, c_out, mix_post, mlp_pre, mlp_post, win, wout, wup, wdown):
        return [meta, mix_pre, win, cv.reshape(DEPTH, 3, cshard), sk, a_out, c_out, wout, mix_post, mlp_pre, wup, wdown, mlp_post]

    grads = order(*gs, g_w_in, g_w_out, g_w_up, g_w_down)
    deltas = order(*ds, d_w_in, d_w_out, d_w_up, d_w_down)
    new_m = order(*nms, nm_w_in, nm_w_out, nm_w_up, nm_w_down)
    new_v = order(*nvs, nv_w_in, nv_w_out, nv_w_up, nv_w_down)
    return (loss, grad_x, *grads, *deltas, *new_m, *new_v)
```

```python
import functools
import math

import jax
import jax.numpy as jnp
from jax import lax
from jax.experimental import pallas as pl
from jax.experimental.pallas import tpu as pltpu

F32 = jnp.float32
BF = jnp.bfloat16

D_MODEL = 1024
ATTN_W = 512
CONV_W = 512
HEAD_DIM = 64
N_Q_HEADS = 8
ROT_DIM = 16
D_FF = 4096
IN_W = 2304
N_META = 16
BLOCK = 128
LEAD_PAD = BLOCK - N_META
ROPE_THETA = 500000.0
EPS = 1e-6
N_DEV = 8
DEPTH = 2
NEG = -1e30
SCALE = HEAD_DIM ** -0.5

ADAM_LR = 0.001
ADAM_B1 = 0.9
ADAM_B2 = 0.999
ADAM_EPS = 1e-08
ADAM_WD = 0.01
ADAM_STEP = 10

ROW_MLP_POST, ROW_MLP_PRE, ROW_MIX_PRE, ROW_SINK, ROW_LOSS = 0, 1, 2, 3, 4
ROW_MIX_POST, ROW_GROUP_G, ROW_CONV = 0, 1, 2

VMEM_LIMIT = 56 * 1024 * 1024
MESH = pl.DeviceIdType.MESH


def _dot(a, b):
    return jnp.dot(a, b, preferred_element_type=F32)


def _dot_nt(a, b):
    return lax.dot_general(a, b, (((1,), (1,)), ((), ())), preferred_element_type=F32)


def _dot_tn(a, b):
    return lax.dot_general(a, b, (((0,), (0,)), ((), ())), preferred_element_type=F32)


def _rms_fwd(x, g):
    r = lax.rsqrt(jnp.mean(x * x, axis=-1, keepdims=True) + EPS)
    return x * r * g


def _rms_bwd(x, g, dy):
    r = lax.rsqrt(jnp.mean(x * x, axis=-1, keepdims=True) + EPS)
    xh = x * r
    t = dy * g
    dx = r * (t - xh * jnp.mean(t * xh, axis=-1, keepdims=True))
    dg = jnp.sum(dy * xh, axis=0, keepdims=True)
    return dx, dg


def _row_tile(lp, cands=(640, 512, 384, 256, 128)):
    for t in cands:
        if lp % t == 0:
            return t
    raise ValueError(f"row count {lp} is not a multiple of 128")


def _full(shape):
    n = len(shape)
    return pl.BlockSpec(shape, lambda *_: (0,) * n, pipeline_mode=pl.Buffered(1))


def _full_out(shape):
    n = len(shape)
    return pl.BlockSpec(shape, lambda *_: (0,) * n)


def _params(sem=("arbitrary",)):
    return pltpu.CompilerParams(dimension_semantics=sem, vmem_limit_bytes=VMEM_LIMIT)


def _rope_table(lp):
    half = ROT_DIM // 2
    pos = jnp.maximum(jnp.arange(lp) - LEAD_PAD, 0).astype(F32)
    inv_freq = jnp.power(jnp.float32(ROPE_THETA), -jnp.arange(0, ROT_DIM, 2, dtype=F32) / ROT_DIM)
    ang_t = jnp.concatenate([inv_freq, inv_freq])[:, None] * pos[None, :]
    row = lax.broadcasted_iota(jnp.int32, (ROT_DIM, lp), 0)
    cs_t = jnp.where(row < half, jnp.cos(ang_t), jnp.sin(ang_t))
    return jnp.pad(cs_t.T, ((0, 0), (0, 128 - ROT_DIM)))


def _rope_coeffs(t):
    half = ROT_DIM // 2
    lane = lax.broadcasted_iota(jnp.int32, t.shape, 1)
    cos_a = jnp.where(lane < half, t, 0.0)
    sin_a = pltpu.roll(jnp.where((lane >= half) & (lane < ROT_DIM), t, 0.0), 128 - half, 1)
    c = cos_a + pltpu.roll(cos_a, half, 1) + jnp.where((lane >= ROT_DIM) & (lane < HEAD_DIM), 1.0, 0.0)
    s2 = pltpu.roll(sin_a, half, 1)
    both = lambda u: u + pltpu.roll(u, HEAD_DIM, 1)
    return both(c), both(-sin_a), both(s2)


def _rope(t, c, s1, s2):
    return t * c + pltpu.roll(t, BLOCK - 8, 1) * s1 + pltpu.roll(t, 8, 1) * s2


def _rope_t(dt, c, s1, s2):
    return dt * c + pltpu.roll(dt * s1, 8, 1) + pltpu.roll(dt * s2, BLOCK - 8, 1)


def _build_h(x, rope_compact, tm, exch, small_piece, name):
    seq = x.shape[0]
    lp = BLOCK + seq
    nt = lp // tm
    n_sub = tm // BLOCK
    small_shape = exch.land_shapes[small_piece].shape

    def body(*refs):
        h_ref, c_ref, s1_ref, s2_ref = refs[n_sub + 1:n_sub + 5]
        for j in range(n_sub):
            h_ref[j * BLOCK:(j + 1) * BLOCK, :] = refs[j][...]
        c_ref[...], s1_ref[...], s2_ref[...] = _rope_coeffs(refs[n_sub][...])

    def after(lands, *refs):
        h_ref, buf = refs[n_sub + 1], refs[n_sub + 5]
        pltpu.sync_copy(lands[small_piece], buf)
        h_ref[0:LEAD_PAD, :] = jnp.zeros((LEAD_PAD, D_MODEL), F32)
        for d in range(N_DEV):
            h_ref[LEAD_PAD:BLOCK, d * 128:(d + 1) * 128] = buf[d, 0:N_META, :]

    tile = lambda i: (i + 1) % nt
    piece = lambda j: pl.BlockSpec((BLOCK, D_MODEL), lambda i: (jnp.maximum(tile(i) * n_sub + j - 1, 0), 0))
    rows = lambda w: pl.BlockSpec((tm, w), lambda i: (tile(i), 0))
    (h, *rope), lands = _call(
        body, exch,
        name=name,
        grid=(nt,),
        in_specs=[piece(j) for j in range(n_sub)] + [rows(128)],
        out_specs=[rows(D_MODEL)] + [rows(128)] * 3,
        out_shape=[jax.ShapeDtypeStruct((lp, D_MODEL), F32)] + [jax.ShapeDtypeStruct((lp, 128), F32)] * 3,
        scratch_shapes=[pltpu.VMEM(small_shape, F32)],
        compiler_params=_params(),
        after=after,
    )(*([x] * n_sub), rope_compact)
    return h, rope, lands


def _in_proj_fwd(h, g, w_in_t, rope, tm, name, exch=None):
    lp = h.shape[0]

    def body(h_ref, g_ref, w_ref, c_ref, s1_ref, s2_ref, a_ref, qkv_ref, bch_ref):
        a = _rms_fwd(h_ref[...], g_ref[...]).astype(BF)
        a_ref[...] = a
        proj = _dot_nt(a, w_ref[...])
        c, s1, s2 = c_ref[...], s1_ref[...], s2_ref[...]
        for j in range(5):
            t = _rope(proj[:, j * 128:(j + 1) * 128], c, s1, s2)
            qkv_ref[:, j * 128:(j + 1) * 128] = (t * SCALE if j < 4 else t).astype(BF)
        qkv_ref[:, 640:768] = proj[:, 640:768].astype(BF)
        bch_ref[...] = proj[:, 768:].astype(BF)

    row = lambda w: pl.BlockSpec((tm, w), lambda i: (i, 0))
    return _call(
        body, exch,
        name=name,
        grid=(lp // tm,),
        in_specs=[row(D_MODEL), _full((1, D_MODEL)), _full((IN_W, D_MODEL)), row(128), row(128), row(128)],
        out_specs=[row(D_MODEL), row(768), row(3 * CONV_W)],
        out_shape=[
            jax.ShapeDtypeStruct((lp, D_MODEL), BF),
            jax.ShapeDtypeStruct((lp, 768), BF),
            jax.ShapeDtypeStruct((lp, 3 * CONV_W), BF),
        ],
        compiler_params=_params(),
    )(h, g, w_in_t, *rope)


def _fold_masks(i):
    r = lax.broadcasted_iota(jnp.int32, (2 * BLOCK, BLOCK), 0) & (BLOCK - 1)
    c = lax.broadcasted_iota(jnp.int32, (2 * BLOCK, BLOCK), 1)
    tri = c > r
    ok = jnp.where(tri, (i - 1) * BLOCK + c, i * BLOCK + c) >= LEAD_PAD
    return tri, ok


def _kv_operand(x, kvh):
    lane = lax.broadcasted_iota(jnp.int32, x.shape, 1)
    zero = jnp.zeros_like(x)
    if kvh == 0:
        lo = jnp.where(lane < HEAD_DIM, x, zero)
        hi = pltpu.roll(lo, HEAD_DIM, 1)
    else:
        hi = jnp.where(lane >= HEAD_DIM, x, zero)
        lo = pltpu.roll(hi, HEAD_DIM, 1)
    return jnp.concatenate([lo, hi], axis=0)


def _split4(t, tri):
    zero = jnp.zeros_like(t[0])
    return jnp.concatenate(
        [jnp.where(tri, t[0], zero), jnp.where(tri, zero, t[0]), jnp.where(tri, t[1], zero), jnp.where(tri, zero, t[1])], axis=1)


def _sink_cols(sink_ref, kvh):
    first = lax.broadcasted_iota(jnp.int32, (2 * BLOCK, 1), 0) < BLOCK
    return [jnp.where(first, sink_ref[0, 4 * kvh + half], sink_ref[0, 4 * kvh + 2 + half]) for half in range(2)]


def _folded_exp(q2, k4, tri, ok, sks):
    s = _dot_nt(q2, k4)
    es, ss = [], []
    for half in range(2):
        s_h = s[:, 2 * half * BLOCK:2 * (half + 1) * BLOCK]
        sf = jnp.where(ok, jnp.where(tri, s_h[:, :BLOCK], s_h[:, BLOCK:]), NEG)
        m = jnp.maximum(jnp.max(sf, axis=-1, keepdims=True), sks[half])
        es.append(jnp.exp(sf - m))
        ss.append(jnp.exp(sks[half] - m))
    sums = _dot(jnp.concatenate(es, axis=0).astype(BF), jnp.ones((BLOCK, BLOCK), BF))
    invs = [1.0 / (sums[2 * half * BLOCK:2 * (half + 1) * BLOCK] + ss[half]) for half in range(2)]
    return es, ss, invs


def _attn_fwd(qkv, sink, name, exch=None):
    lp = qkv.shape[0]
    nb = lp // BLOCK
    per_step = 4

    def one_block(i, sink_ref, q_ref, kvc_ref, kvp_ref, o_ref, p_ref, ps_ref):
        tri, ok = _fold_masks(i)
        kvc, kvp = kvc_ref[...], kvp_ref[...]
        kk = jnp.concatenate([kvp[:, :128], kvc[:, :128]], axis=0)
        vv = jnp.concatenate([kvp[:, 128:], kvc[:, 128:]], axis=0)
        lane = lax.broadcasted_iota(jnp.int32, (BLOCK, 128), 1)
        p_sink = jnp.zeros((BLOCK, 128), F32)
        for kvh in range(2):
            q2 = jnp.concatenate([q_ref[:, 256 * kvh:256 * kvh + 128], q_ref[:, 256 * kvh + 128:256 * kvh + 256]], axis=0)
            es, ss, invs = _folded_exp(q2, _kv_operand(kk, kvh), tri, ok, _sink_cols(sink_ref, kvh))
            pb = [(es[half] * invs[half]).astype(BF) for half in range(2)]
            out = _dot(_split4(pb, tri), _kv_operand(vv, kvh))
            for pair in range(2):
                rows = slice(pair * BLOCK, (pair + 1) * BLOCK)
                o_ref[:, 256 * kvh + 128 * pair:256 * kvh + 128 * (pair + 1)] = out[rows].astype(BF)
                for half in range(2):
                    head = 4 * kvh + 2 * pair + half
                    p_ref[:, 128 * head:128 * (head + 1)] = pb[half][rows]
                    p_sink = jnp.where(lane == head, (ss[half] * invs[half][:, 0:1])[rows], p_sink)
        ps_ref[...] = p_sink

    def body(sink_ref, *refs):
        q_refs, kv_refs = refs[:per_step], refs[per_step:2 * per_step + 1]
        o_ref, p_ref, ps_ref = refs[2 * per_step + 1:]
        for j in range(per_step):
            rows = slice(j * BLOCK, (j + 1) * BLOCK)
            one_block(per_step * pl.program_id(0) + j, sink_ref, q_refs[j], kv_refs[j + 1], kv_refs[j],
                      o_ref.at[rows], p_ref.at[rows], ps_ref.at[rows])

    last = nb - 1
    blk = lambda j: (lambda s: jnp.minimum(per_step * s + j, last))
    out_rows = lambda w: pl.BlockSpec((per_step * BLOCK, w), lambda s: (s, 0))
    return _call(
        body, exch,
        name=name,
        grid=(pl.cdiv(nb, per_step),),
        in_specs=[pl.BlockSpec(memory_space=pltpu.SMEM)]
        + [pl.BlockSpec((BLOCK, ATTN_W), lambda s, j=j: (blk(j)(s), 0)) for j in range(per_step)]
        + [pl.BlockSpec((BLOCK, 256), lambda s: (jnp.maximum(per_step * s - 1, 0), 2))]
        + [pl.BlockSpec((BLOCK, 256), lambda s, j=j: (blk(j)(s), 2)) for j in range(per_step)],
        out_specs=[out_rows(ATTN_W), out_rows(N_Q_HEADS * BLOCK), out_rows(128)],
        out_shape=[jax.ShapeDtypeStruct((lp, ATTN_W), BF), jax.ShapeDtypeStruct((lp, N_Q_HEADS * BLOCK), BF),
                   jax.ShapeDtypeStruct((lp, 128), F32)],
        compiler_params=_params(),
    )(sink, *([qkv] * (2 * per_step + 1)))


def _mix_out_fwd(bch, y_attn, h, conv_w, g_a, g_c, w_out, g_post, tm, name, exch=None):
    lp = h.shape[0]

    def body(bch_ref, ya_ref, h_ref, cw_ref, ga_ref, gc_ref, w_ref, gp_ref, yc_ref, y_ref, z_ref, h2_ref, ext):
        i = pl.program_id(0)

        @pl.when(i == 0)
        def _():
            ext[0:8, :] = jnp.zeros((8, CONV_W), F32)

        b = bch_ref[:, 0:CONV_W].astype(F32)
        u = bch_ref[:, CONV_W:2 * CONV_W].astype(F32) * bch_ref[:, 2 * CONV_W:3 * CONV_W].astype(F32)
        ext[8:8 + tm, :] = u
        yc = cw_ref[0:1, :] * ext[6:6 + tm, :] + cw_ref[1:2, :] * ext[7:7 + tm, :] + cw_ref[2:3, :] * u
        ext[0:8, :] = u[tm - 8:tm, :]
        yc_ref[...] = yc.astype(BF)
        ya = _rms_fwd(ya_ref[...].astype(F32), ga_ref[...]).astype(BF)
        yb = _rms_fwd(b * yc, gc_ref[...]).astype(BF)
        y_ref[:, 0:ATTN_W] = ya
        y_ref[:, ATTN_W:] = yb
        z = _dot(ya, w_ref[0:ATTN_W, :]) + _dot(yb, w_ref[ATTN_W:, :])
        z_ref[...] = z.astype(BF)
        h2_ref[...] = h_ref[...] + _rms_fwd(z, gp_ref[...])

    row = lambda w: pl.BlockSpec((tm, w), lambda i: (i, 0))
    return _call(
        body, exch,
        name=name,
        grid=(lp // tm,),
        in_specs=[
            row(3 * CONV_W), row(ATTN_W), row(D_MODEL), _full((8, CONV_W)), _full((1, ATTN_W)), _full((1, CONV_W)),
            _full((D_MODEL, D_MODEL)), _full((1, D_MODEL)),
        ],
        out_specs=[row(CONV_W), row(D_MODEL), row(D_MODEL), row(D_MODEL)],
        out_shape=[
            jax.ShapeDtypeStruct((lp, CONV_W), BF),
            jax.ShapeDtypeStruct((lp, D_MODEL), BF),
            jax.ShapeDtypeStruct((lp, D_MODEL), BF),
            jax.ShapeDtypeStruct((lp, D_MODEL), F32),
        ],
        scratch_shapes=[pltpu.VMEM((tm + 8, CONV_W), F32)],
        compiler_params=_params(),
    )(bch, y_attn, h, conv_w, g_a, g_c, w_out, g_post)


def _mlp_fwd(h2, g_pre, w_up_t, w_down, g_post, tm, name, exch=None, target=None):
    lp = h2.shape[0]
    sub = math.gcd(tm, BLOCK)
    n_sub, lead = tm // sub, BLOCK // sub
    n_t = n_sub if target is not None else 0

    def body(*refs):
        h_ref, gp_ref, wu_ref, wd_ref, gq_ref = refs[:5]
        t_refs = refs[5:5 + n_t]
        a_ref, up_ref, f_ref, last_ref = refs[5 + n_t:9 + n_t]
        h = h_ref[...]
        a = _rms_fwd(h, gp_ref[...]).astype(BF)
        a_ref[...] = a
        up = _dot_nt(a, wu_ref[...])
        up_ref[...] = up.astype(BF)
        act = jnp.square(jnp.maximum(up, 0.0)).astype(BF)
        f = _dot(act, wd_ref[...])
        f_ref[...] = f
        h3 = h + _rms_fwd(f, gq_ref[...])
        if target is None:
            last_ref[...] = h3
            return
        ls_ref = refs[9 + n_t]
        i = pl.program_id(0)

        @pl.when(i == 0)
        def _():
            ls_ref[...] = jnp.zeros((8, 128), F32)

        sq = jnp.zeros((8, D_MODEL), F32)
        for j in range(n_sub):
            on_tokens = i * n_sub + j >= lead
            d = jnp.where(on_tokens, h3[j * sub:(j + 1) * sub] - t_refs[j][...], 0.0)
            last_ref[j * sub:(j + 1) * sub, :] = d * (1.0 / D_MODEL)
            sq = sq + jnp.sum((d * d).reshape(sub // 8, 8, D_MODEL), axis=0)
        ls_ref[...] += sum(sq[:, k * 128:(k + 1) * 128] for k in range(D_MODEL // 128))

        @pl.when(i == lp // tm - 1)
        def _():
            ls_ref[...] = jnp.full((8, 128), jnp.sum(ls_ref[...]), F32)

    row = lambda w: pl.BlockSpec((tm, w), lambda i: (i, 0))
    piece = lambda j: pl.BlockSpec((sub, D_MODEL), lambda i: (jnp.maximum(i * n_sub + j - lead, 0), 0))
    out_specs = [row(D_MODEL), row(D_FF), row(D_MODEL), row(D_MODEL)]
    out_shape = [
        jax.ShapeDtypeStruct((lp, D_MODEL), BF),
        jax.ShapeDtypeStruct((lp, D_FF), BF),
        jax.ShapeDtypeStruct((lp, D_MODEL), F32),
        jax.ShapeDtypeStruct((lp, D_MODEL), F32),
    ]
    if target is not None:
        out_specs.append(_full_out((8, 128)))
        out_shape.append(jax.ShapeDtypeStruct((8, 128), F32))
    return _call(
        body, exch,
        name=name,
        grid=(lp // tm,),
        in_specs=[row(D_MODEL), _full((1, D_MODEL)), _full((D_FF, D_MODEL)), _full((D_FF, D_MODEL)), _full((1, D_MODEL))]
        + [piece(j) for j in range(n_t)],
        out_specs=out_specs,
        out_shape=out_shape,
        compiler_params=_params(),
    )(h2, g_pre, w_up_t, w_down, g_post, *([target] * n_t))


def _mlp_bwd_dx(dh3, f, up, h2, w_down, w_up_t, g_post, g_pre, tm, name, exch=None):
    lp = h2.shape[0]

    def body(dh3_ref, f_ref, up_ref, h2_ref, wd_ref, wu_ref, gq_ref, gp_ref, df_ref, dup_ref, dh2_ref, dg_ref):
        i = pl.program_id(0)

        @pl.when(i == 0)
        def _():
            dg_ref[...] = jnp.zeros((8, D_MODEL), F32)

        dh3 = dh3_ref[...]
        df, dgq = _rms_bwd(f_ref[...], gq_ref[...], dh3)
        dg_ref[ROW_MLP_POST:ROW_MLP_POST + 1, :] += dgq
        df = df.astype(BF)
        df_ref[...] = df
        dact = _dot_nt(df, wd_ref[...])
        dup = (dact * (2.0 * jnp.maximum(up_ref[...].astype(F32), 0.0))).astype(BF)
        dup_ref[...] = dup
        da = _dot(dup, wu_ref[...])
        dh, dgp = _rms_bwd(h2_ref[...], gp_ref[...], da)
        dg_ref[ROW_MLP_PRE:ROW_MLP_PRE + 1, :] += dgp
        dh2_ref[...] = dh3 + dh

    row = lambda w: pl.BlockSpec((tm, w), lambda i: (i, 0))
    return _call(
        body, exch,
        name=name,
        grid=(lp // tm,),
        in_specs=[
            row(D_MODEL), row(D_MODEL), row(D_FF), row(D_MODEL), _full((D_FF, D_MODEL)), _full((D_FF, D_MODEL)),
            _full((1, D_MODEL)), _full((1, D_MODEL)),
        ],
        out_specs=[row(D_MODEL), row(D_FF), row(D_MODEL), _full_out((8, D_MODEL))],
        out_shape=[
            jax.ShapeDtypeStruct((lp, D_MODEL), BF),
            jax.ShapeDtypeStruct((lp, D_FF), BF),
            jax.ShapeDtypeStruct((lp, D_MODEL), F32),
            jax.ShapeDtypeStruct((8, D_MODEL), F32),
        ],
        compiler_params=_params(),
    )(dh3, f, up, h2, w_down, w_up_t, g_post, g_pre)


def _mlp_bwd_dw(up, df, dup, a2, tm, name):
    lp = up.shape[0]
    nt = lp // tm
    nj = D_FF // D_MODEL

    def body(up_ref, df_ref, dup_ref, a_ref, dwd_ref, dwu_ref, accd, accu):
        i = pl.program_id(1)

        @pl.when(i == 0)
        def _():
            accd[...] = jnp.zeros_like(accd)
            accu[...] = jnp.zeros_like(accu)

        act = jnp.square(jnp.maximum(up_ref[...].astype(F32), 0.0)).astype(BF)
        accd[...] += _dot_tn(act, df_ref[...])
        accu[...] += _dot_tn(dup_ref[...], a_ref[...])

        @pl.when(i == nt - 1)
        def _():
            dwd_ref[...] = accd[...].astype(BF)
            dwu_ref[...] = accu[...].astype(BF)

    return pl.pallas_call(
        body,
        name=name,
        grid=(nj, nt),
        in_specs=[
            pl.BlockSpec((tm, D_MODEL), lambda j, i: (i, j)),
            pl.BlockSpec((tm, D_MODEL), lambda j, i: (i, 0)),
            pl.BlockSpec((tm, D_MODEL), lambda j, i: (i, j)),
            pl.BlockSpec((tm, D_MODEL), lambda j, i: (i, 0)),
        ],
        out_specs=[pl.BlockSpec((D_MODEL, D_MODEL), lambda j, i: (j, 0)), pl.BlockSpec((D_MODEL, D_MODEL), lambda j, i: (j, 0))],
        out_shape=[jax.ShapeDtypeStruct((D_FF, D_MODEL), BF), jax.ShapeDtypeStruct((D_FF, D_MODEL), BF)],
        scratch_shapes=[pltpu.VMEM((D_MODEL, D_MODEL), F32), pltpu.VMEM((D_MODEL, D_MODEL), F32)],
        compiler_params=_params(("arbitrary", "arbitrary")),
    )(up, df, dup, a2)


def _mix_out_bwd(dh2, z, y_attn, yc, bch, y, w_out, g_post, g_a, g_c, conv_w, tm, name, exch=None):
    lp = dh2.shape[0]
    nt = lp // tm

    def body(dh2_ref, z_ref, ya_ref, yc_ref, bch_ref, y_ref, w_ref, gp_ref, ga_ref, gc_ref, cw_ref,
             dya_ref, dbch_ref, dg_ref, dwo_ref, ext, acco):
        i = pl.program_id(0)
        dcw_ref = dg_ref.at[ROW_CONV:ROW_CONV + 3, 0:CONV_W]

        @pl.when(i == 0)
        def _():
            ext[tm:tm + 8, :] = jnp.zeros((8, CONV_W), F32)
            dg_ref[...] = jnp.zeros((8, D_MODEL), F32)
            acco[...] = jnp.zeros_like(acco)

        dz, dgp = _rms_bwd(z_ref[...].astype(F32), gp_ref[...], dh2_ref[...])
        dg_ref[ROW_MIX_POST:ROW_MIX_POST + 1, :] += dgp
        dz = dz.astype(BF)
        acco[...] += _dot_tn(y_ref[...], dz)
        dya_n = _dot_nt(dz, w_ref[0:ATTN_W, :])
        dyb_n = _dot_nt(dz, w_ref[ATTN_W:, :])
        dya, dga = _rms_bwd(ya_ref[...].astype(F32), ga_ref[...], dya_n)
        dg_ref[ROW_GROUP_G:ROW_GROUP_G + 1, 0:ATTN_W] += dga
        dya_ref[...] = dya
        b = bch_ref[:, 0:CONV_W].astype(F32)
        c = bch_ref[:, CONV_W:2 * CONV_W].astype(F32)
        hc = bch_ref[:, 2 * CONV_W:3 * CONV_W].astype(F32)
        u = c * hc
        yc_v = yc_ref[...].astype(F32)
        dyconv, dgc = _rms_bwd(b * yc_v, gc_ref[...], dyb_n)
        dg_ref[ROW_GROUP_G:ROW_GROUP_G + 1, ATTN_W:] += dgc
        dbch_ref[:, 0:CONV_W] = (dyconv * yc_v).astype(BF)
        dyc = dyconv * b
        ext[0:tm, :] = dyc
        d1 = ext[1:1 + tm, :]
        d2 = ext[2:2 + tm, :]
        du = cw_ref[2:3, :] * dyc + cw_ref[1:2, :] * d1 + cw_ref[0:1, :] * d2
        ext[tm:tm + 8, :] = dyc[0:8, :]
        dbch_ref[:, CONV_W:2 * CONV_W] = (du * hc).astype(BF)
        dbch_ref[:, 2 * CONV_W:3 * CONV_W] = (du * c).astype(BF)
        dcw_ref[0:1, :] += jnp.sum(u * d2, axis=0, keepdims=True)
        dcw_ref[1:2, :] += jnp.sum(u * d1, axis=0, keepdims=True)
        dcw_ref[2:3, :] += jnp.sum(u * dyc, axis=0, keepdims=True)

        @pl.when(i == nt - 1)
        def _():
            dwo_ref[...] = acco[...].astype(BF)

    row = lambda w: pl.BlockSpec((tm, w), lambda i: (nt - 1 - i, 0))
    return _call(
        body, exch,
        name=name,
        grid=(nt,),
        in_specs=[
            row(D_MODEL), row(D_MODEL), row(ATTN_W), row(CONV_W), row(3 * CONV_W), row(D_MODEL), _full((D_MODEL, D_MODEL)),
            _full((1, D_MODEL)), _full((1, ATTN_W)), _full((1, CONV_W)), _full((8, CONV_W)),
        ],
        out_specs=[row(ATTN_W), row(3 * CONV_W), _full_out((8, D_MODEL)), _full_out((D_MODEL, D_MODEL))],
        out_shape=[
            jax.ShapeDtypeStruct((lp, ATTN_W), F32),
            jax.ShapeDtypeStruct((lp, 3 * CONV_W), BF),
            jax.ShapeDtypeStruct((8, D_MODEL), F32),
            jax.ShapeDtypeStruct((D_MODEL, D_MODEL), BF),
        ],
        scratch_shapes=[pltpu.VMEM((tm + 8, CONV_W), F32), pltpu.VMEM((D_MODEL, D_MODEL), F32)],
        compiler_params=_params(),
    )(dh2, z, y_attn, yc, bch, y, w_out, g_post, g_a, g_c, conv_w)


def _attn_bwd(qkv, o, do, probs, p_sink, rope, name, exch=None):
    lp = qkv.shape[0]
    nb = lp // BLOCK

    def body(q_ref, kvc_ref, kvp_ref, o_ref, do_ref, p_ref, ps_ref, cq_ref, s1q_ref, s2q_ref, ck_ref, s1k_ref, s2k_ref,
             dq_ref, dkv_ref, dsink_ref, carry):
        i = pl.program_id(0)

        @pl.when(i == 0)
        def _():
            carry[...] = jnp.zeros_like(carry)
            dsink_ref[...] = jnp.zeros((8, 128), F32)

        def finish(tot):
            dk = _rope_t(tot[:, :128], ck_ref[...], s1k_ref[...], s2k_ref[...])
            dkv_ref[:, 0:128] = dk.astype(BF)
            dkv_ref[:, 128:256] = tot[:, 128:].astype(BF)

        @pl.when(i < nb)
        def _():
            tri, _ = _fold_masks(i)
            kvc, kvp = kvc_ref[...], kvp_ref[...]
            kk = jnp.concatenate([kvp[:, :128], kvc[:, :128]], axis=0)
            vv = jnp.concatenate([kvp[:, 128:], kvc[:, 128:]], axis=0)
            lane = lax.broadcasted_iota(jnp.int32, (BLOCK, 128), 1)
            lane2 = lax.broadcasted_iota(jnp.int32, (2 * BLOCK, 128), 1)
            rope_q = (cq_ref[...], s1q_ref[...], s2q_ref[...])
            deltas = jnp.zeros((BLOCK, 128), F32)
            folded = []
            for kvh in range(2):
                c0 = 256 * kvh
                q2 = jnp.concatenate([q_ref[:, c0:c0 + 128], q_ref[:, c0 + 128:c0 + 256]], axis=0)
                do2 = jnp.concatenate([do_ref[:, c0:c0 + 128], do_ref[:, c0 + 128:c0 + 256]], axis=0)
                o2 = jnp.concatenate([o_ref[:, c0:c0 + 128], o_ref[:, c0 + 128:c0 + 256]], axis=0).astype(F32)
                k4, v4 = _kv_operand(kk, kvh), _kv_operand(vv, kvh)
                prod = do2 * o2
                dob = do2.astype(BF)
                dp = _dot_nt(dob, v4)
                ds, pb = [], []
                for half in range(2):
                    heads = [4 * kvh + 2 * pair + half for pair in range(2)]
                    p = jnp.concatenate([p_ref[:, 128 * h:128 * (h + 1)] for h in heads], axis=0)
                    sel = (lane2 < HEAD_DIM) if half == 0 else (lane2 >= HEAD_DIM)
                    delta = jnp.sum(jnp.where(sel, prod, 0.0), axis=-1, keepdims=True)
                    dp_h = dp[:, 2 * half * BLOCK:2 * (half + 1) * BLOCK]
                    ds.append((p.astype(F32) * (jnp.where(tri, dp_h[:, :BLOCK], dp_h[:, BLOCK:]) - delta)).astype(BF))
                    pb.append(p)
                    for pair in range(2):
                        deltas = jnp.where(lane == heads[pair], delta[pair * BLOCK:(pair + 1) * BLOCK], deltas)
                ds4, p4 = _split4(ds, tri), _split4(pb, tri)
                dq2 = _dot(ds4, k4) * SCALE
                dq_ref[:, c0:c0 + 128] = _rope_t(dq2[:BLOCK], *rope_q).astype(BF)
                dq_ref[:, c0 + 128:c0 + 256] = _rope_t(dq2[BLOCK:], *rope_q).astype(BF)
                rk, rv = _dot_tn(ds4, q2), _dot_tn(p4, dob)
                own = (lane < HEAD_DIM) if kvh == 0 else (lane >= HEAD_DIM)
                group = []
                for r in (rk, rv):
                    for blk in range(2):
                        t = jnp.where(lane < HEAD_DIM, r[blk * BLOCK:(blk + 1) * BLOCK], r[(2 + blk) * BLOCK:(3 + blk) * BLOCK])
                        group.append(jnp.where(own, t + pltpu.roll(t, HEAD_DIM, 1), 0.0))
                folded.append(group)
            dsink_ref[ROW_SINK:ROW_SINK + 1, :] -= jnp.sum(ps_ref[...] * deltas, axis=0, keepdims=True)
            dk_p, dk_c, dv_p, dv_c = [folded[0][t] + folded[1][t] for t in range(4)]
            finish(carry[...] + jnp.concatenate([dk_p, dv_p], axis=1))
            carry[...] = jnp.concatenate([dk_c, dv_c], axis=1)

        @pl.when(i == nb)
        def _():
            finish(carry[...])

    qi = lambda i: jnp.minimum(i, nb - 1)
    ki = lambda i: jnp.maximum(i - 1, 0)
    tab_q = pl.BlockSpec((BLOCK, 128), lambda i: (qi(i), 0))
    tab_k = pl.BlockSpec((BLOCK, 128), lambda i: (ki(i), 0))
    return _call(
        body, exch,
        name=name,
        grid=(nb + 1,),
        in_specs=[
            pl.BlockSpec((BLOCK, ATTN_W), lambda i: (qi(i), 0)),
            pl.BlockSpec((BLOCK, 256), lambda i: (qi(i), 2)),
            pl.BlockSpec((BLOCK, 256), lambda i: (jnp.maximum(qi(i) - 1, 0), 2)),
            pl.BlockSpec((BLOCK, ATTN_W), lambda i: (qi(i), 0)),
            pl.BlockSpec((BLOCK, ATTN_W), lambda i: (qi(i), 0)),
            pl.BlockSpec((BLOCK, N_Q_HEADS * BLOCK), lambda i: (qi(i), 0)),
            tab_q, tab_q, tab_q, tab_q, tab_k, tab_k, tab_k,
        ],
        out_specs=[
            pl.BlockSpec((BLOCK, ATTN_W), lambda i: (qi(i), 0)),
            pl.BlockSpec((BLOCK, 256), lambda i: (ki(i), 0)),
            pl.BlockSpec((8, 128), lambda i: (0, 0)),
        ],
        out_shape=[
            jax.ShapeDtypeStruct((lp, ATTN_W), BF),
            jax.ShapeDtypeStruct((lp, 256), BF),
            jax.ShapeDtypeStruct((8, 128), F32),
        ],
        scratch_shapes=[pltpu.VMEM((BLOCK, 256), F32)],
        compiler_params=_params(),
    )(qkv, qkv, qkv, o, do, probs, p_sink, *rope, *rope)


def _in_proj_bwd_dx(dq, dkv, dbch, w_in_t, h, dh2, g, tm, name, exch=None):
    lp = h.shape[0]

    def body(dq_ref, dkv_ref, dbch_ref, w_ref, h_ref, dh2_ref, g_ref, dh_ref, dg_ref):
        i = pl.program_id(0)

        @pl.when(i == 0)
        def _():
            dg_ref[...] = jnp.zeros((8, D_MODEL), F32)

        da = _dot(jnp.concatenate([dq_ref[...], dkv_ref[...], dbch_ref[...]], axis=1), w_ref[...])
        dh, dg = _rms_bwd(h_ref[...], g_ref[...], da)
        dg_ref[ROW_MIX_PRE:ROW_MIX_PRE + 1, :] += dg
        dh_ref[...] = dh2_ref[...] + dh

    row = lambda w: pl.BlockSpec((tm, w), lambda i: (i, 0))
    return _call(
        body, exch,
        name=name,
        grid=(lp // tm,),
        in_specs=[row(ATTN_W), row(256), row(3 * CONV_W), _full((IN_W, D_MODEL)), row(D_MODEL), row(D_MODEL), _full((1, D_MODEL))],
        out_specs=[row(D_MODEL), _full_out((8, D_MODEL))],
        out_shape=[jax.ShapeDtypeStruct((lp, D_MODEL), F32), jax.ShapeDtypeStruct((8, D_MODEL), F32)],
        compiler_params=_params(),
    )(dq, dkv, dbch, w_in_t, h, dh2, g)


def _mix_bwd_dw(dq, dkv, dbch, a, tm, name, exch=None):
    lp = a.shape[0]
    nt = lp // tm

    def body(dq_ref, dkv_ref, dbch_ref, a_ref, dwi_ref, acci):
        i = pl.program_id(0)

        @pl.when(i == 0)
        def _():
            acci[...] = jnp.zeros_like(acci)

        a_v = a_ref[...]
        acci[0:512, :] += _dot_tn(dq_ref[...], a_v)
        acci[512:768, :] += _dot_tn(dkv_ref[...], a_v)
        acci[768:, :] += _dot_tn(dbch_ref[...], a_v)

        @pl.when(i == nt - 1)
        def _():
            dwi_ref[...] = acci[...].astype(BF)

    row = lambda w: pl.BlockSpec((tm, w), lambda i: (i, 0))
    return _call(
        body, exch,
        name=name,
        grid=(nt,),
        in_specs=[row(ATTN_W), row(256), row(3 * CONV_W), row(D_MODEL)],
        out_specs=[_full_out((IN_W, D_MODEL))],
        out_shape=[jax.ShapeDtypeStruct((IN_W, D_MODEL), BF)],
        scratch_shapes=[pltpu.VMEM((IN_W, D_MODEL), F32)],
        compiler_params=_params(),
    )(dq, dkv, dbch, a)


def _mesh_place():
    x, y, c = lax.axis_index("x"), lax.axis_index("y"), lax.axis_index("c")
    return x, y, c, 4 * x + 2 * y + c


def _peer(x, y, c, k):
    px = 1 - x if k & 4 else x
    py = 1 - y if k & 2 else y
    pc = 1 - c if k & 1 else c
    return (px, py, pc), 4 * px + 2 * py + pc


SIBLING = 1
SAME_CORE = (2, 4, 6)
OTHER_CORE = (3, 5, 7)


class _Exchange:
    def __init__(self, pieces, forward_lead=8):
        self.forward_lead = forward_lead
        self.srcs = [s for s, _ in pieces]
        self.to_all = [g for _, g in pieces]
        self.n = len(pieces)
        self.land_shapes = [
            jax.ShapeDtypeStruct((N_DEV,) + (s.shape if g else s.shape[1:]), s.dtype) for s, g in pieces]
        self.sem_shapes = [pltpu.SemaphoreType.DMA((self.n, N_DEV - 1)), pltpu.SemaphoreType.DMA((self.n, N_DEV - 1)),
                           pltpu.SemaphoreType.DMA((self.n,))]
        self.forwards = any(self.to_all)

    def _ops(self, srcs, lands, sems):
        send_sems, recv_sems, local_sems = sems
        x, y, c, me = _mesh_place()

        def remote(p, k, src, slot, to):
            return pltpu.make_async_remote_copy(
                src_ref=src, dst_ref=lands[p].at[slot], send_sem=send_sems.at[p, k - 1], recv_sem=recv_sems.at[p, k - 1],
                device_id=to, device_id_type=MESH)

        def own(p):
            return pltpu.make_async_copy(srcs[p] if self.to_all[p] else srcs[p].at[me], lands[p].at[me], local_sems.at[p])

        def direct(p, k):
            peer, pidx = _peer(x, y, c, k)
            return remote(p, k, srcs[p] if self.to_all[p] else srcs[p].at[pidx], me, peer)

        def forward(p, k):
            sibling, _ = _peer(x, y, c, SIBLING)
            _, origin = _peer(x, y, c, k ^ SIBLING)
            return remote(p, k, lands[p].at[origin], origin, sibling)

        def arrival(p, k):
            peer, pidx = _peer(x, y, c, k)
            return remote(p, k, lands[p].at[pidx], pidx, peer)

        return own, direct, forward, arrival

    def start(self, srcs, lands, sems):
        own, direct, _, _ = self._ops(srcs, lands, sems)
        for p in range(self.n):
            own(p).start()
            for k in ((SIBLING,) + SAME_CORE) if self.to_all[p] else range(1, N_DEV):
                direct(p, k).start()

    def forward(self, srcs, lands, sems):
        _, _, forward, arrival = self._ops(srcs, lands, sems)
        for p in range(self.n):
            if self.to_all[p]:
                for k in SAME_CORE:
                    arrival(p, k).wait_recv()
                    forward(p, k ^ SIBLING).start()

    def finish(self, srcs, lands, sems):
        own, direct, forward, arrival = self._ops(srcs, lands, sems)
        for p in range(self.n):
            for k in ((SIBLING,) + OTHER_CORE) if self.to_all[p] else range(1, N_DEV):
                arrival(p, k).wait_recv()
        for p in range(self.n):
            for k in range(1, N_DEV):
                (forward(p, k) if self.to_all[p] and k in OTHER_CORE else direct(p, k)).wait_send()
            own(p).wait()


class _LayerRows:
    def __init__(self, array, layer):
        self.array = array if array.ndim == 3 else array.reshape(DEPTH, 1, -1)
        self.layer = layer

    def spec(self):
        layer = self.layer
        return pl.BlockSpec((None,) + self.array.shape[1:], lambda *_: (layer, 0, 0), pipeline_mode=pl.Buffered(1))


def _call(body, exch, *, name, grid, in_specs, out_specs, out_shape, scratch_shapes=(), compiler_params, after=None):
    def with_layer_rows(args):
        specs = [a.spec() if isinstance(a, _LayerRows) else s for s, a in zip(in_specs, args)]
        return specs, [a.array if isinstance(a, _LayerRows) else a for a in args]

    if exch is None:
        def plain(*args):
            specs, args = with_layer_rows(args)
            return pl.pallas_call(body, name=name, grid=grid, in_specs=specs, out_specs=out_specs, out_shape=out_shape,
                                  scratch_shapes=scratch_shapes, compiler_params=compiler_params)(*args)
        return plain
    n_in, n_out, n_scr, n_x = len(in_specs), len(out_shape), len(scratch_shapes), exch.n
    steps = math.prod(grid)

    def carrying(*refs):
        a, b, c, d, e = n_in, n_in + n_x, n_in + n_x + n_out, n_in + 2 * n_x + n_out, n_in + 2 * n_x + n_out + n_scr
        ins, srcs, outs, lands, scr, sems = refs[:a], refs[a:b], refs[b:c], refs[c:d], refs[d:e], refs[e:]
        step = functools.reduce(lambda acc, t: acc * grid[t] + pl.program_id(t), range(len(grid)), 0)

        @pl.when(step == 0)
        def _():
            exch.start(srcs, lands, sems)

        body(*ins, *outs, *scr)

        if exch.forwards:
            @pl.when(step == max(0, steps - 1 - pl.cdiv(steps, exch.forward_lead)))
            def _():
                exch.forward(srcs, lands, sems)

        @pl.when(step == steps - 1)
        def _():
            exch.finish(srcs, lands, sems)
            if after is not None:
                after(lands, *ins, *outs, *scr)

    hbm = pl.BlockSpec(memory_space=pl.ANY)

    def run(*args):
        specs, args = with_layer_rows(args)
        res = pl.pallas_call(
            carrying, name=name, grid=grid, in_specs=specs + [hbm] * n_x, out_specs=list(out_specs) + [hbm] * n_x,
            out_shape=list(out_shape) + exch.land_shapes, scratch_shapes=list(scratch_shapes) + exch.sem_shapes,
            compiler_params=compiler_params)(*args, *exch.srcs)
        return list(res[:n_out]), list(res[n_out:])

    return run


def _sum_small(part):
    exch = _Exchange([(part, True)])

    def body(part_ref, out_ref, land, *sems):
        exch.start([part_ref], [land], sems)
        exch.forward([part_ref], [land], sems)
        exch.finish([part_ref], [land], sems)
        acc = land[0]
        for d in range(1, N_DEV):
            acc = acc + land[d]
        out_ref[...] = acc

    vmem = pl.BlockSpec(memory_space=pltpu.VMEM)
    return pl.pallas_call(
        body,
        name="sum_small",
        in_specs=[vmem],
        out_specs=vmem,
        out_shape=jax.ShapeDtypeStruct(part.shape, F32),
        scratch_shapes=[pltpu.VMEM(exch.land_shapes[0].shape, F32)] + exch.sem_shapes,
    )(part)


def _adamw(w, g, m, v):
    m = ADAM_B1 * m + (1.0 - ADAM_B1) * g
    v = ADAM_B2 * v + (1.0 - ADAM_B2) * jnp.square(g)
    m_hat = m / (1.0 - ADAM_B1 ** ADAM_STEP)
    v_hat = v / (1.0 - ADAM_B2 ** ADAM_STEP)
    delta = -ADAM_LR * (m_hat / (jnp.sqrt(v_hat) + ADAM_EPS) + ADAM_WD * w)
    return delta, m, v


def _landed_specs(tr, wd):
    return [pl.BlockSpec((N_DEV, tr, wd), lambda l, i, ll=ll: (0, jnp.where(l == ll, i, 0), 0)) for ll in range(DEPTH)]


def _device_sum(r_ref):
    acc = r_ref[0].astype(F32)
    for d in range(1, N_DEV):
        acc = acc + r_ref[d].astype(F32)
    return acc


def _sum_adamw(recv, w, m, v, tr, name, transposed=False):
    _, r, wd = recv[0].shape

    def body(*refs):
        w_ref, m_ref, v_ref, g_ref, d_ref, mo_ref, vo_ref = refs[DEPTH:]
        for ll in range(DEPTH):
            @pl.when(pl.program_id(0) == ll)
            def _(ll=ll):
                g = _device_sum(refs[ll])
                g = g.T if transposed else g
                g_ref[0] = g
                d_ref[0], mo_ref[0], vo_ref[0] = _adamw(w_ref[0], g, m_ref[0], v_ref[0])

    if transposed:
        blk = pl.BlockSpec((1, wd, tr), lambda l, i: (l, 0, i))
        shape = jax.ShapeDtypeStruct((DEPTH, wd, r), F32)
    else:
        blk = pl.BlockSpec((1, tr, wd), lambda l, i: (l, i, 0))
        shape = jax.ShapeDtypeStruct((DEPTH, r, wd), F32)
    return pl.pallas_call(
        body,
        name=name,
        grid=(DEPTH, r // tr),
        in_specs=_landed_specs(tr, wd) + [blk, blk, blk],
        out_specs=[blk] * 4,
        out_shape=[shape] * 4,
        compiler_params=_params(("arbitrary", "arbitrary")),
    )(*recv, w, m, v)


def _adamw_small(ws, gs, ms, vs):
    n = len(ws)

    def body(*refs):
        w_r, g_r, m_r, v_r = refs[:n], refs[n:2 * n], refs[2 * n:3 * n], refs[3 * n:4 * n]
        d_o, m_o, v_o = refs[4 * n:5 * n], refs[5 * n:6 * n], refs[6 * n:7 * n]
        for t in range(n):
            d_o[t][...], m_o[t][...], v_o[t][...] = _adamw(w_r[t][...], g_r[t][...], m_r[t][...], v_r[t][...])

    vmem = pl.BlockSpec(memory_space=pltpu.VMEM)
    shapes = [jax.ShapeDtypeStruct(w.shape, F32) for w in ws]
    outs = pl.pallas_call(
        body,
        name="adamw_small",
        in_specs=[vmem] * (4 * n),
        out_specs=[vmem] * (3 * n),
        out_shape=shapes * 3,
    )(*ws, *gs, *ms, *vs)
    return outs[:n], outs[n:2 * n], outs[2 * n:]


def kernel(x, meta_tokens, mix_pre_g, w_in, conv_w, sinks, attn_out_g, conv_out_g, w_out, mix_post_g, mlp_pre_g, w_up, w_down, mlp_post_g, loss_target, m_meta_tokens, m_mix_pre_g, m_w_in, m_conv_w, m_sinks, m_attn_out_g, m_conv_out_g, m_w_out, m_mix_post_g, m_mlp_pre_g, m_w_up, m_w_down, m_mlp_post_g, v_meta_tokens, v_mix_pre_g, v_w_in, v_conv_w, v_sinks, v_attn_out_g, v_conv_out_g, v_w_out, v_mix_post_g, v_mlp_pre_g, v_w_up, v_w_down, v_mlp_post_g):
    seq = x.shape[1]
    lp = BLOCK + seq
    tm = _row_tile(lp)
    tm_mlp = _row_tile(lp, (320, 256, 128))
    tm_dw_mlp = _row_tile(lp, (1664, 1040, 640, 384, 256, 128))
    tm_dw_mix = _row_tile(lp, (1664, 832, 640, 384, 256, 128))
    me = 4 * lax.axis_index("x") + 2 * lax.axis_index("y") + lax.axis_index("c")
    cshard = CONV_W // N_DEV
    mshard = D_MODEL // N_DEV

    gather_with = {
        ("in_proj_fwd", 0): [("down", 0)], ("attn_fwd", 0): [("out", 0), ("up", 0)],
        ("mlp_fwd", 0): [("in", 1), ("out", 1), ("up", 1), ("down", 1)],
    }
    tight = {("in_proj_fwd", 0): 16, ("attn_fwd", 0): 16, ("mlp_fwd", 0): 12}
    scatter_with = {
        ("attn_bwd", 1): [("down", 1)], ("mix_bwd_dw", 1): [("out", 1)], ("mlp_bwd_dx", 0): [("up", 1), ("in", 1)],
        ("mix_out_bwd", 0): [("up", 0)], ("attn_bwd", 0): [("down", 0)], ("mix_bwd_dw", 0): [("out", 0)],
        ("in_proj_bwd_dx", 0): [("in", 0)],
    }
    shard = {"in": jnp.swapaxes(w_in, 1, 2).astype(BF), "out": w_out.astype(BF),
             "up": jnp.swapaxes(w_up, 1, 2).astype(BF), "down": w_down.astype(BF)}
    weight = {}
    grad = {}
    landed = {}

    def run(fn, kind, l, *args):
        key, name = (kind, l), f"{kind}_{l}"
        if key in gather_with:
            blocks = gather_with[key]
            lead = tight.get(key, 8)
            outs, lands = fn(*args, name, _Exchange([(shard[n][k], True) for n, k in blocks], lead))
            for b, land in zip(blocks, lands):
                weight[b] = land.reshape(-1, D_MODEL)
            return outs
        if key in scatter_with:
            blocks = scatter_with[key]
            outs, lands = fn(*args, name, _Exchange([(grad[b].reshape(N_DEV, -1, D_MODEL), False) for b in blocks]))
            landed.update(zip(blocks, lands))
            return outs
        return fn(*args, name)

    small = jnp.zeros((24, 128), F32)
    small = small.at[0:N_META, :].set(meta_tokens)
    small = small.at[N_META:N_META + 6, 0:cshard].set(conv_w.reshape(6, cshard))
    first = _Exchange([(shard["in"][0], True), (small, True)], 16)
    h, rope, (first_in, g_small) = _build_h(x[0], _rope_table(lp), tm, first, 1, "build_h")
    weight[("in", 0)] = first_in.reshape(-1, D_MODEL)
    cw = g_small[:, N_META:N_META + 6, 0:cshard].reshape(N_DEV, DEPTH, 3, cshard)
    cw = jnp.transpose(cw, (1, 2, 0, 3)).reshape(DEPTH, 3, CONV_W)
    conv_full = jnp.concatenate([cw, jnp.zeros((DEPTH, 5, CONV_W), F32)], axis=1)

    row1 = _LayerRows

    saved = []
    for l in range(DEPTH):
        a, qkv, bch = run(_in_proj_fwd, "in_proj_fwd", l, h, row1(mix_pre_g, l), weight[("in", l)], rope, tm)
        y_attn, probs, p_sink = run(_attn_fwd, "attn_fwd", l, qkv, sinks[l].reshape(1, -1))
        yc, y, z, h2 = run(_mix_out_fwd, "mix_out_fwd", l, bch, y_attn, h, row1(conv_full, l), row1(attn_out_g, l),
                       row1(conv_out_g, l), weight[("out", l)], row1(mix_post_g, l), tm)
        mlp = _mlp_fwd if l < DEPTH - 1 else functools.partial(_mlp_fwd, target=loss_target[0])
        a2, up, f, *rest = run(mlp, "mlp_fwd", l, h2, row1(mlp_pre_g, l), weight[("up", l)], weight[("down", l)],
                               row1(mlp_post_g, l), tm_mlp)
        saved.append((h, a, qkv, bch, y_attn, probs, p_sink, yc, y, z, h2, a2, up, f))
        h = rest[0]
    dh, loss_part = rest[0], rest[1][0, 0] * (0.5 / D_MODEL)

    gsmall = [None] * DEPTH
    for l in reversed(range(DEPTH)):
        h0, a, qkv, bch, y_attn, probs, p_sink, yc, y, z, h2, a2, up, f = saved[l]
        df, dup, dh2, dg_mlp = run(_mlp_bwd_dx, "mlp_bwd_dx", l, dh, f, up, h2, weight[("down", l)], weight[("up", l)],
                                   row1(mlp_post_g, l), row1(mlp_pre_g, l), tm_mlp)
        grad[("down", l)], grad[("up", l)] = _mlp_bwd_dw(up, df, dup, a2, tm_dw_mlp, f"mlp_bwd_dw_{l}")
        dya, dbch, dg_mix, grad[("out", l)] = run(
            _mix_out_bwd, "mix_out_bwd", l, dh2, z, y_attn, yc, bch, y, weight[("out", l)], row1(mix_post_g, l),
            row1(attn_out_g, l), row1(conv_out_g, l), row1(conv_full, l), tm)
        dq, dkv, dsink = run(_attn_bwd, "attn_bwd", l, qkv, y_attn, dya, probs, p_sink, rope)
        grad[("in", l)], = run(_mix_bwd_dw, "mix_bwd_dw", l, dq, dkv, dbch, a, tm_dw_mix)
        dh, dg_in = run(_in_proj_bwd_dx, "in_proj_bwd_dx", l, dq, dkv, dbch, weight[("in", l)], h0, dh2,
                        row1(mix_pre_g, l), tm)
        tile_a =dg_mlp + dg_in + jnp.pad(dsink, ((0, 0), (0, D_MODEL - 128)))
        gsmall[l] = (tile_a, dg_mix)
    grad_x = dh[BLOCK:][None]

    loss_tile = jnp.zeros((8, D_MODEL), F32).at[ROW_LOSS, 0].set(loss_part)
    tot = _sum_small(jnp.concatenate(
        [gsmall[0][0] + loss_tile, gsmall[0][1], gsmall[1][0], gsmall[1][1], dh[LEAD_PAD:BLOCK]], axis=0))
    loss = tot[ROW_LOSS, 0]
    ta = [tot[16 * l:16 * l + 8] for l in range(DEPTH)]
    tb = [tot[16 * l + 8:16 * l + 16] for l in range(DEPTH)]
    pick = lambda tiles, r0, r1, c0, c1: jnp.stack([t[r0:r1, c0:c1] for t in tiles])
    g_mlp_post = pick(ta, ROW_MLP_POST, ROW_MLP_POST + 1, 0, D_MODEL).reshape(DEPTH, D_MODEL)
    g_mlp_pre = pick(ta, ROW_MLP_PRE, ROW_MLP_PRE + 1, 0, D_MODEL).reshape(DEPTH, D_MODEL)
    g_mix_pre = pick(ta, ROW_MIX_PRE, ROW_MIX_PRE + 1, 0, D_MODEL).reshape(DEPTH, D_MODEL)
    g_sinks = pick(ta, ROW_SINK, ROW_SINK + 1, 0, N_Q_HEADS).reshape(DEPTH, N_Q_HEADS)
    g_mix_post = pick(tb, ROW_MIX_POST, ROW_MIX_POST + 1, 0, D_MODEL).reshape(DEPTH, D_MODEL)
    g_attn_out = pick(tb, ROW_GROUP_G, ROW_GROUP_G + 1, 0, ATTN_W).reshape(DEPTH, ATTN_W)
    g_conv_out = pick(tb, ROW_GROUP_G, ROW_GROUP_G + 1, ATTN_W, D_MODEL).reshape(DEPTH, CONV_W)
    g_conv_full = pick(tb, ROW_CONV, ROW_CONV + 3, 0, CONV_W)
    g_conv = lax.dynamic_slice_in_dim(g_conv_full, me * cshard, cshard, axis=2)
    g_meta = lax.dynamic_slice_in_dim(tot[16 * DEPTH:16 * DEPTH + N_META], me * mshard, mshard, axis=1)

    r_in, r_out, r_up, r_down = [[landed[(n, l)] for l in range(DEPTH)] for n in ("in", "out", "up", "down")]
    t12 = lambda a: jnp.swapaxes(a, 1, 2)
    g_w_in, d_w_in, nm_w_in, nv_w_in = map(t12, _sum_adamw(r_in, t12(w_in), t12(m_w_in), t12(v_w_in), 96, "adamw_w_in"))
    g_w_up, d_w_up, nm_w_up, nv_w_up = _sum_adamw(r_up, w_up, m_w_up, v_w_up, 128, "adamw_w_up", transposed=True)
    g_w_out, d_w_out, nm_w_out, nv_w_out = _sum_adamw(r_out, w_out, m_w_out, v_w_out, 128, "adamw_w_out")
    g_w_down, d_w_down, nm_w_down, nv_w_down = _sum_adamw(r_down, w_down, m_w_down, v_w_down, 128, "adamw_w_down")

    ws = [meta_tokens, mix_pre_g, conv_w.reshape(6, cshard), sinks, attn_out_g, conv_out_g, mix_post_g, mlp_pre_g, mlp_post_g]
    gs = [g_meta, g_mix_pre, g_conv.reshape(6, cshard), g_sinks, g_attn_out, g_conv_out, g_mix_post, g_mlp_pre, g_mlp_post]
    ms = [m_meta_tokens, m_mix_pre_g, m_conv_w.reshape(6, cshard), m_sinks, m_attn_out_g, m_conv_out_g, m_mix_post_g,
          m_mlp_pre_g, m_mlp_post_g]
    vs = [v_meta_tokens, v_mix_pre_g, v_conv_w.reshape(6, cshard), v_sinks, v_attn_out_g, v_conv_out_g, v_mix_post_g,
          v_mlp_pre_g, v_mlp_post_g]
    ds, nms, nvs = _adamw_small(ws, gs, ms, vs)

    def order(meta, mix_pre, cv, sk, a_out, c_out, mix_post, mlp_pre, mlp_post, win, wout, wup, wdown):
        return [meta, mix_pre, win, cv.reshape(DEPTH, 3, cshard), sk, a_out, c_out, wout, mix_post, mlp_pre, wup, wdown, mlp_post]

    grads = order(*gs, g_w_in, g_w_out, g_w_up, g_w_down)
    deltas = order(*ds, d_w_in, d_w_out, d_w_up, d_w_down)
    new_m = order(*nms, nm_w_in, nm_w_out, nm_w_up, nm_w_down)
    new_v = order(*nvs, nv_w_in, nv_w_out, nv_w_up, nv_w_down)
    return (loss, grad_x, *grads, *deltas, *new_m, *new_v)
```

```python
import functools
import math

import jax
import jax.numpy as jnp
from jax import lax
from jax.experimental import pallas as pl
from jax.experimental.pallas import tpu as pltpu

F32 = jnp.float32
BF = jnp.bfloat16

D_MODEL = 1024
ATTN_W = 512
CONV_W = 512
HEAD_DIM = 64
N_Q_HEADS = 8
ROT_DIM = 16
D_FF = 4096
IN_W = 2304
N_META = 16
BLOCK = 128
LEAD_PAD = BLOCK - N_META
ROPE_THETA = 500000.0
EPS = 1e-6
N_DEV = 8
DEPTH = 2
NEG = -1e30
SCALE = HEAD_DIM ** -0.5

ADAM_LR = 0.001
ADAM_B1 = 0.9
ADAM_B2 = 0.999
ADAM_EPS = 1e-08
ADAM_WD = 0.01
ADAM_STEP = 10

ROW_MLP_POST, ROW_MLP_PRE, ROW_MIX_PRE, ROW_SINK, ROW_LOSS = 0, 1, 2, 3, 4
ROW_MIX_POST, ROW_GROUP_G, ROW_CONV = 0, 1, 2

VMEM_LIMIT = 56 * 1024 * 1024
MESH = pl.DeviceIdType.MESH


def _dot(a, b):
    return jnp.dot(a, b, preferred_element_type=F32)


def _dot_nt(a, b):
    return lax.dot_general(a, b, (((1,), (1,)), ((), ())), preferred_element_type=F32)


def _dot_tn(a, b):
    return lax.dot_general(a, b, (((0,), (0,)), ((), ())), preferred_element_type=F32)


def _rms_fwd(x, g):
    r = lax.rsqrt(jnp.mean(x * x, axis=-1, keepdims=True) + EPS)
    return x * r * g


def _rms_bwd(x, g, dy):
    r = lax.rsqrt(jnp.mean(x * x, axis=-1, keepdims=True) + EPS)
    xh = x * r
    t = dy * g
    dx = r * (t - xh * jnp.mean(t * xh, axis=-1, keepdims=True))
    dg = jnp.sum(dy * xh, axis=0, keepdims=True)
    return dx, dg


def _row_tile(lp, cands=(640, 512, 384, 256, 128)):
    for t in cands:
        if lp % t == 0:
            return t
    raise ValueError(f"row count {lp} is not a multiple of 128")


def _full(shape):
    n = len(shape)
    return pl.BlockSpec(shape, lambda *_: (0,) * n, pipeline_mode=pl.Buffered(1))


def _full_out(shape):
    n = len(shape)
    return pl.BlockSpec(shape, lambda *_: (0,) * n)


def _params(sem=("arbitrary",)):
    return pltpu.CompilerParams(dimension_semantics=sem, vmem_limit_bytes=VMEM_LIMIT)


def _rope_table(lp):
    half = ROT_DIM // 2
    pos = jnp.maximum(jnp.arange(lp) - LEAD_PAD, 0).astype(F32)
    inv_freq = jnp.power(jnp.float32(ROPE_THETA), -jnp.arange(0, ROT_DIM, 2, dtype=F32) / ROT_DIM)
    ang_t = jnp.concatenate([inv_freq, inv_freq])[:, None] * pos[None, :]
    row = lax.broadcasted_iota(jnp.int32, (ROT_DIM, lp), 0)
    cs_t = jnp.where(row < half, jnp.cos(ang_t), jnp.sin(ang_t))
    return jnp.pad(cs_t.T, ((0, 0), (0, 128 - ROT_DIM)))


def _rope_coeffs(t):
    half = ROT_DIM // 2
    lane = lax.broadcasted_iota(jnp.int32, t.shape, 1)
    cos_a = jnp.where(lane < half, t, 0.0)
    sin_a = pltpu.roll(jnp.where((lane >= half) & (lane < ROT_DIM), t, 0.0), 128 - half, 1)
    c = cos_a + pltpu.roll(cos_a, half, 1) + jnp.where((lane >= ROT_DIM) & (lane < HEAD_DIM), 1.0, 0.0)
    s2 = pltpu.roll(sin_a, half, 1)
    both = lambda u: u + pltpu.roll(u, HEAD_DIM, 1)
    return both(c), both(-sin_a), both(s2)


def _rope(t, c, s1, s2):
    return t * c + pltpu.roll(t, BLOCK - 8, 1) * s1 + pltpu.roll(t, 8, 1) * s2


def _rope_t(dt, c, s1, s2):
    return dt * c + pltpu.roll(dt * s1, 8, 1) + pltpu.roll(dt * s2, BLOCK - 8, 1)


def _build_h(x, rope_compact, tm, exch, small_piece, name):
    seq = x.shape[0]
    lp = BLOCK + seq
    nt = lp // tm
    n_sub = tm // BLOCK
    small_shape = exch.land_shapes[small_piece].shape

    def body(*refs):
        h_ref, c_ref, s1_ref, s2_ref = refs[n_sub + 1:n_sub + 5]
        for j in range(n_sub):
            h_ref[j * BLOCK:(j + 1) * BLOCK, :] = refs[j][...]
        c_ref[...], s1_ref[...], s2_ref[...] = _rope_coeffs(refs[n_sub][...])

    def after(lands, *refs):
        h_ref, buf = refs[n_sub + 1], refs[n_sub + 5]
        pltpu.sync_copy(lands[small_piece], buf)
        h_ref[0:LEAD_PAD, :] = jnp.zeros((LEAD_PAD, D_MODEL), F32)
        for d in range(N_DEV):
            h_ref[LEAD_PAD:BLOCK, d * 128:(d + 1) * 128] = buf[d, 0:N_META, :]

    tile = lambda i: (i + 1) % nt
    piece = lambda j: pl.BlockSpec((BLOCK, D_MODEL), lambda i: (jnp.maximum(tile(i) * n_sub + j - 1, 0), 0))
    rows = lambda w: pl.BlockSpec((tm, w), lambda i: (tile(i), 0))
    (h, *rope), lands = _call(
        body, exch,
        name=name,
        grid=(nt,),
        in_specs=[piece(j) for j in range(n_sub)] + [rows(128)],
        out_specs=[rows(D_MODEL)] + [rows(128)] * 3,
        out_shape=[jax.ShapeDtypeStruct((lp, D_MODEL), F32)] + [jax.ShapeDtypeStruct((lp, 128), F32)] * 3,
        scratch_shapes=[pltpu.VMEM(small_shape, F32)],
        compiler_params=_params(),
        after=after,
    )(*([x] * n_sub), rope_compact)
    return h, rope, lands


def _in_proj_fwd(h, g, w_in_t, rope, tm, name, exch=None):
    lp = h.shape[0]

    def body(h_ref, g_ref, w_ref, c_ref, s1_ref, s2_ref, a_ref, qkv_ref, bch_ref):
        a = _rms_fwd(h_ref[...], g_ref[...]).astype(BF)
        a_ref[...] = a
        proj = _dot_nt(a, w_ref[...])
        c, s1, s2 = c_ref[...], s1_ref[...], s2_ref[...]
        for j in range(5):
            t = _rope(proj[:, j * 128:(j + 1) * 128], c, s1, s2)
            qkv_ref[:, j * 128:(j + 1) * 128] = (t * SCALE if j < 4 else t).astype(BF)
        qkv_ref[:, 640:768] = proj[:, 640:768].astype(BF)
        bch_ref[...] = proj[:, 768:].astype(BF)

    row = lambda w: pl.BlockSpec((tm, w), lambda i: (i, 0))
    return _call(
        body, exch,
        name=name,
        grid=(lp // tm,),
        in_specs=[row(D_MODEL), _full((1, D_MODEL)), _full((IN_W, D_MODEL)), row(128), row(128), row(128)],
        out_specs=[row(D_MODEL), row(768), row(3 * CONV_W)],
        out_shape=[
            jax.ShapeDtypeStruct((lp, D_MODEL), BF),
            jax.ShapeDtypeStruct((lp, 768), BF),
            jax.ShapeDtypeStruct((lp, 3 * CONV_W), BF),
        ],
        compiler_params=_params(),
    )(h, g, w_in_t, *rope)


def _fold_masks(i):
    r = lax.broadcasted_iota(jnp.int32, (2 * BLOCK, BLOCK), 0) & (BLOCK - 1)
    c = lax.broadcasted_iota(jnp.int32, (2 * BLOCK, BLOCK), 1)
    tri = c > r
    ok = jnp.where(tri, (i - 1) * BLOCK + c, i * BLOCK + c) >= LEAD_PAD
    return tri, ok


def _kv_operand(x, kvh):
    lane = lax.broadcasted_iota(jnp.int32, x.shape, 1)
    zero = jnp.zeros_like(x)
    if kvh == 0:
        lo = jnp.where(lane < HEAD_DIM, x, zero)
        hi = pltpu.roll(lo, HEAD_DIM, 1)
    else:
        hi = jnp.where(lane >= HEAD_DIM, x, zero)
        lo = pltpu.roll(hi, HEAD_DIM, 1)
    return jnp.concatenate([lo, hi], axis=0)


def _split4(t, tri):
    zero = jnp.zeros_like(t[0])
    return jnp.concatenate(
        [jnp.where(tri, t[0], zero), jnp.where(tri, zero, t[0]), jnp.where(tri, t[1], zero), jnp.where(tri, zero, t[1])], axis=1)


def _sink_cols(sink_ref, kvh):
    first = lax.broadcasted_iota(jnp.int32, (2 * BLOCK, 1), 0) < BLOCK
    return [jnp.where(first, sink_ref[0, 4 * kvh + half], sink_ref[0, 4 * kvh + 2 + half]) for half in range(2)]


def _folded_exp(q2, k4, tri, ok, sks):
    s = _dot_nt(q2, k4)
    es, ss = [], []
    for half in range(2):
        s_h = s[:, 2 * half * BLOCK:2 * (half + 1) * BLOCK]
        sf = jnp.where(ok, jnp.where(tri, s_h[:, :BLOCK], s_h[:, BLOCK:]), NEG)
        m = jnp.maximum(jnp.max(sf, axis=-1, keepdims=True), sks[half])
        es.append(jnp.exp(sf - m))
        ss.append(jnp.exp(sks[half] - m))
    sums = _dot(jnp.concatenate(es, axis=0).astype(BF), jnp.ones((BLOCK, BLOCK), BF))
    invs = [1.0 / (sums[2 * half * BLOCK:2 * (half + 1) * BLOCK] + ss[half]) for half in range(2)]
    return es, ss, invs


def _attn_fwd(qkv, sink, name, exch=None):
    lp = qkv.shape[0]
    nb = lp // BLOCK
    per_step = 4

    def one_block(i, sink_ref, q_ref, kvc_ref, kvp_ref, o_ref, p_ref, ps_ref):
        tri, ok = _fold_masks(i)
        kvc, kvp = kvc_ref[...], kvp_ref[...]
        kk = jnp.concatenate([kvp[:, :128], kvc[:, :128]], axis=0)
        vv = jnp.concatenate([kvp[:, 128:], kvc[:, 128:]], axis=0)
        lane = lax.broadcasted_iota(jnp.int32, (BLOCK, 128), 1)
        p_sink = jnp.zeros((BLOCK, 128), F32)
        for kvh in range(2):
            q2 = jnp.concatenate([q_ref[:, 256 * kvh:256 * kvh + 128], q_ref[:, 256 * kvh + 128:256 * kvh + 256]], axis=0)
            es, ss, invs = _folded_exp(q2, _kv_operand(kk, kvh), tri, ok, _sink_cols(sink_ref, kvh))
            pb = [(es[half] * invs[half]).astype(BF) for half in range(2)]
            out = _dot(_split4(pb, tri), _kv_operand(vv, kvh))
            for pair in range(2):
                rows = slice(pair * BLOCK, (pair + 1) * BLOCK)
                o_ref[:, 256 * kvh + 128 * pair:256 * kvh + 128 * (pair + 1)] = out[rows].astype(BF)
                for half in range(2):
                    head = 4 * kvh + 2 * pair + half
                    p_ref[:, 128 * head:128 * (head + 1)] = pb[half][rows]
                    p_sink = jnp.where(lane == head, (ss[half] * invs[half][:, 0:1])[rows], p_sink)
        ps_ref[...] = p_sink

    def body(sink_ref, *refs):
        q_refs, kv_refs = refs[:per_step], refs[per_step:2 * per_step + 1]
        o_ref, p_ref, ps_ref = refs[2 * per_step + 1:]
        for j in range(per_step):
            rows = slice(j * BLOCK, (j + 1) * BLOCK)
            one_block(per_step * pl.program_id(0) + j, sink_ref, q_refs[j], kv_refs[j + 1], kv_refs[j],
                      o_ref.at[rows], p_ref.at[rows], ps_ref.at[rows])

    last = nb - 1
    blk = lambda j: (lambda s: jnp.minimum(per_step * s + j, last))
    out_rows = lambda w: pl.BlockSpec((per_step * BLOCK, w), lambda s: (s, 0))
    return _call(
        body, exch,
        name=name,
        grid=(pl.cdiv(nb, per_step),),
        in_specs=[pl.BlockSpec(memory_space=pltpu.SMEM)]
        + [pl.BlockSpec((BLOCK, ATTN_W), lambda s, j=j: (blk(j)(s), 0)) for j in range(per_step)]
        + [pl.BlockSpec((BLOCK, 256), lambda s: (jnp.maximum(per_step * s - 1, 0), 2))]
        + [pl.BlockSpec((BLOCK, 256), lambda s, j=j: (blk(j)(s), 2)) for j in range(per_step)],
        out_specs=[out_rows(ATTN_W), out_rows(N_Q_HEADS * BLOCK), out_rows(128)],
        out_shape=[jax.ShapeDtypeStruct((lp, ATTN_W), BF), jax.ShapeDtypeStruct((lp, N_Q_HEADS * BLOCK), BF),
                   jax.ShapeDtypeStruct((lp, 128), F32)],
        compiler_params=_params(),
    )(sink, *([qkv] * (2 * per_step + 1)))


def _mix_out_fwd(bch, y_attn, h, conv_w, g_a, g_c, w_out, g_post, tm, name, exch=None):
    lp = h.shape[0]

    def body(bch_ref, ya_ref, h_ref, cw_ref, ga_ref, gc_ref, w_ref, gp_ref, yc_ref, y_ref, z_ref, h2_ref, ext):
        i = pl.program_id(0)

        @pl.when(i == 0)
        def _():
            ext[0:8, :] = jnp.zeros((8, CONV_W), F32)

        b = bch_ref[:, 0:CONV_W].astype(F32)
        u = bch_ref[:, CONV_W:2 * CONV_W].astype(F32) * bch_ref[:, 2 * CONV_W:3 * CONV_W].astype(F32)
        ext[8:8 + tm, :] = u
        yc = cw_ref[0:1, :] * ext[6:6 + tm, :] + cw_ref[1:2, :] * ext[7:7 + tm, :] + cw_ref[2:3, :] * u
        ext[0:8, :] = u[tm - 8:tm, :]
        yc_ref[...] = yc.astype(BF)
        ya = _rms_fwd(ya_ref[...].astype(F32), ga_ref[...]).astype(BF)
        yb = _rms_fwd(b * yc, gc_ref[...]).astype(BF)
        y_ref[:, 0:ATTN_W] = ya
        y_ref[:, ATTN_W:] = yb
        z = _dot(ya, w_ref[0:ATTN_W, :]) + _dot(yb, w_ref[ATTN_W:, :])
        z_ref[...] = z.astype(BF)
        h2_ref[...] = h_ref[...] + _rms_fwd(z, gp_ref[...])

    row = lambda w: pl.BlockSpec((tm, w), lambda i: (i, 0))
    return _call(
        body, exch,
        name=name,
        grid=(lp // tm,),
        in_specs=[
            row(3 * CONV_W), row(ATTN_W), row(D_MODEL), _full((8, CONV_W)), _full((1, ATTN_W)), _full((1, CONV_W)),
            _full((D_MODEL, D_MODEL)), _full((1, D_MODEL)),
        ],
        out_specs=[row(CONV_W), row(D_MODEL), row(D_MODEL), row(D_MODEL)],
        out_shape=[
            jax.ShapeDtypeStruct((lp, CONV_W), BF),
            jax.ShapeDtypeStruct((lp, D_MODEL), BF),
            jax.ShapeDtypeStruct((lp, D_MODEL), BF),
            jax.ShapeDtypeStruct((lp, D_MODEL), F32),
        ],
        scratch_shapes=[pltpu.VMEM((tm + 8, CONV_W), F32)],
        compiler_params=_params(),
    )(bch, y_attn, h, conv_w, g_a, g_c, w_out, g_post)


def _mlp_fwd(h2, g_pre, w_up_t, w_down, g_post, tm, name, exch=None, target=None):
    lp = h2.shape[0]
    sub = math.gcd(tm, BLOCK)
    n_sub, lead = tm // sub, BLOCK // sub
    n_t = n_sub if target is not None else 0

    def body(*refs):
        h_ref, gp_ref, wu_ref, wd_ref, gq_ref = refs[:5]
        t_refs = refs[5:5 + n_t]
        a_ref, up_ref, f_ref, last_ref = refs[5 + n_t:9 + n_t]
        h = h_ref[...]
        a = _rms_fwd(h, gp_ref[...]).astype(BF)
        a_ref[...] = a
        up = _dot_nt(a, wu_ref[...])
        up_ref[...] = up.astype(BF)
        act = jnp.square(jnp.maximum(up, 0.0)).astype(BF)
        f = _dot(act, wd_ref[...])
        f_ref[...] = f
        h3 = h + _rms_fwd(f, gq_ref[...])
        if target is None:
            last_ref[...] = h3
            return
        ls_ref = refs[9 + n_t]
        i = pl.program_id(0)

        @pl.when(i == 0)
        def _():
            ls_ref[...] = jnp.zeros((8, 128), F32)

        sq = jnp.zeros((8, D_MODEL), F32)
        for j in range(n_sub):
            on_tokens = i * n_sub + j >= lead
            d = jnp.where(on_tokens, h3[j * sub:(j + 1) * sub] - t_refs[j][...], 0.0)
            last_ref[j * sub:(j + 1) * sub, :] = d * (1.0 / D_MODEL)
            sq = sq + jnp.sum((d * d).reshape(sub // 8, 8, D_MODEL), axis=0)
        ls_ref[...] += sum(sq[:, k * 128:(k + 1) * 128] for k in range(D_MODEL // 128))

        @pl.when(i == lp // tm - 1)
        def _():
            ls_ref[...] = jnp.full((8, 128), jnp.sum(ls_ref[...]), F32)

    row = lambda w: pl.BlockSpec((tm, w), lambda i: (i, 0))
    piece = lambda j: pl.BlockSpec((sub, D_MODEL), lambda i: (jnp.maximum(i * n_sub + j - lead, 0), 0))
    out_specs = [row(D_MODEL), row(D_FF), row(D_MODEL), row(D_MODEL)]
    out_shape = [
        jax.ShapeDtypeStruct((lp, D_MODEL), BF),
        jax.ShapeDtypeStruct((lp, D_FF), BF),
        jax.ShapeDtypeStruct((lp, D_MODEL), F32),
        jax.ShapeDtypeStruct((lp, D_MODEL), F32),
    ]
    if target is not None:
        out_specs.append(_full_out((8, 128)))
        out_shape.append(jax.ShapeDtypeStruct((8, 128), F32))
    return _call(
        body, exch,
        name=name,
        grid=(lp // tm,),
        in_specs=[row(D_MODEL), _full((1, D_MODEL)), _full((D_FF, D_MODEL)), _full((D_FF, D_MODEL)), _full((1, D_MODEL))]
        + [piece(j) for j in range(n_t)],
        out_specs=out_specs,
        out_shape=out_shape,
        compiler_params=_params(),
    )(h2, g_pre, w_up_t, w_down, g_post, *([target] * n_t))


def _mlp_bwd_dx(dh3, f, up, h2, w_down, w_up_t, g_post, g_pre, tm, name, exch=None):
    lp = h2.shape[0]

    def body(dh3_ref, f_ref, up_ref, h2_ref, wd_ref, wu_ref, gq_ref, gp_ref, df_ref, dup_ref, dh2_ref, dg_ref):
        i = pl.program_id(0)

        @pl.when(i == 0)
        def _():
            dg_ref[...] = jnp.zeros((8, D_MODEL), F32)

        dh3 = dh3_ref[...]
        df, dgq = _rms_bwd(f_ref[...], gq_ref[...], dh3)
        dg_ref[ROW_MLP_POST:ROW_MLP_POST + 1, :] += dgq
        df = df.astype(BF)
        df_ref[...] = df
        dact = _dot_nt(df, wd_ref[...])
        dup = (dact * (2.0 * jnp.maximum(up_ref[...].astype(F32), 0.0))).astype(BF)
        dup_ref[...] = dup
        da = _dot(dup, wu_ref[...])
        dh, dgp = _rms_bwd(h2_ref[...], gp_ref[...], da)
        dg_ref[ROW_MLP_PRE:ROW_MLP_PRE + 1, :] += dgp
        dh2_ref[...] = dh3 + dh

    row = lambda w: pl.BlockSpec((tm, w), lambda i: (i, 0))
    return _call(
        body, exch,
        name=name,
        grid=(lp // tm,),
        in_specs=[
            row(D_MODEL), row(D_MODEL), row(D_FF), row(D_MODEL), _full((D_FF, D_MODEL)), _full((D_FF, D_MODEL)),
            _full((1, D_MODEL)), _full((1, D_MODEL)),
        ],
        out_specs=[row(D_MODEL), row(D_FF), row(D_MODEL), _full_out((8, D_MODEL))],
        out_shape=[
            jax.ShapeDtypeStruct((lp, D_MODEL), BF),
            jax.ShapeDtypeStruct((lp, D_FF), BF),
            jax.ShapeDtypeStruct((lp, D_MODEL), F32),
            jax.ShapeDtypeStruct((8, D_MODEL), F32),
        ],
        compiler_params=_params(),
    )(dh3, f, up, h2, w_down, w_up_t, g_post, g_pre)


def _mlp_bwd_dw(up, df, dup, a2, tm, name):
    lp = up.shape[0]
    nt = lp // tm
    nj = D_FF // D_MODEL

    def body(up_ref, df_ref, dup_ref, a_ref, dwd_ref, dwu_ref, accd, accu):
        i = pl.program_id(1)

        @pl.when(i == 0)
        def _():
            accd[...] = jnp.zeros_like(accd)
            accu[...] = jnp.zeros_like(accu)

        act = jnp.square(jnp.maximum(up_ref[...].astype(F32), 0.0)).astype(BF)
        accd[...] += _dot_tn(act, df_ref[...])
        accu[...] += _dot_tn(dup_ref[...], a_ref[...])

        @pl.when(i == nt - 1)
        def _():
            dwd_ref[...] = accd[...].astype(BF)
            dwu_ref[...] = accu[...].astype(BF)

    return pl.pallas_call(
        body,
        name=name,
        grid=(nj, nt),
        in_specs=[
            pl.BlockSpec((tm, D_MODEL), lambda j, i: (i, j)),
            pl.BlockSpec((tm, D_MODEL), lambda j, i: (i, 0)),
            pl.BlockSpec((tm, D_MODEL), lambda j, i: (i, j)),
            pl.BlockSpec((tm, D_MODEL), lambda j, i: (i, 0)),
        ],
        out_specs=[pl.BlockSpec((D_MODEL, D_MODEL), lambda j, i: (j, 0)), pl.BlockSpec((D_MODEL, D_MODEL), lambda j, i: (j, 0))],
        out_shape=[jax.ShapeDtypeStruct((D_FF, D_MODEL), BF), jax.ShapeDtypeStruct((D_FF, D_MODEL), BF)],
        scratch_shapes=[pltpu.VMEM((D_MODEL, D_MODEL), F32), pltpu.VMEM((D_MODEL, D_MODEL), F32)],
        compiler_params=_params(("arbitrary", "arbitrary")),
    )(up, df, dup, a2)


def _mix_out_bwd(dh2, z, y_attn, yc, bch, y, w_out, g_post, g_a, g_c, conv_w, tm, name, exch=None):
    lp = dh2.shape[0]
    nt = lp // tm

    def body(dh2_ref, z_ref, ya_ref, yc_ref, bch_ref, y_ref, w_ref, gp_ref, ga_ref, gc_ref, cw_ref,
             dya_ref, dbch_ref, dg_ref, dwo_ref, ext, acco):
        i = pl.program_id(0)
        dcw_ref = dg_ref.at[ROW_CONV:ROW_CONV + 3, 0:CONV_W]

        @pl.when(i == 0)
        def _():
            ext[tm:tm + 8, :] = jnp.zeros((8, CONV_W), F32)
            dg_ref[...] = jnp.zeros((8, D_MODEL), F32)
            acco[...] = jnp.zeros_like(acco)

        dz, dgp = _rms_bwd(z_ref[...].astype(F32), gp_ref[...], dh2_ref[...])
        dg_ref[ROW_MIX_POST:ROW_MIX_POST + 1, :] += dgp
        dz = dz.astype(BF)
        acco[...] += _dot_tn(y_ref[...], dz)
        dya_n = _dot_nt(dz, w_ref[0:ATTN_W, :])
        dyb_n = _dot_nt(dz, w_ref[ATTN_W:, :])
        dya, dga = _rms_bwd(ya_ref[...].astype(F32), ga_ref[...], dya_n)
        dg_ref[ROW_GROUP_G:ROW_GROUP_G + 1, 0:ATTN_W] += dga
        dya_ref[...] = dya
        b = bch_ref[:, 0:CONV_W].astype(F32)
        c = bch_ref[:, CONV_W:2 * CONV_W].astype(F32)
        hc = bch_ref[:, 2 * CONV_W:3 * CONV_W].astype(F32)
        u = c * hc
        yc_v = yc_ref[...].astype(F32)
        dyconv, dgc = _rms_bwd(b * yc_v, gc_ref[...], dyb_n)
        dg_ref[ROW_GROUP_G:ROW_GROUP_G + 1, ATTN_W:] += dgc
        dbch_ref[:, 0:CONV_W] = (dyconv * yc_v).astype(BF)
        dyc = dyconv * b
        ext[0:tm, :] = dyc
        d1 = ext[1:1 + tm, :]
        d2 = ext[2:2 + tm, :]
        du = cw_ref[2:3, :] * dyc + cw_ref[1:2, :] * d1 + cw_ref[0:1, :] * d2
        ext[tm:tm + 8, :] = dyc[0:8, :]
        dbch_ref[:, CONV_W:2 * CONV_W] = (du * hc).astype(BF)
        dbch_ref[:, 2 * CONV_W:3 * CONV_W] = (du * c).astype(BF)
        dcw_ref[0:1, :] += jnp.sum(u * d2, axis=0, keepdims=True)
        dcw_ref[1:2, :] += jnp.sum(u * d1, axis=0, keepdims=True)
        dcw_ref[2:3, :] += jnp.sum(u * dyc, axis=0, keepdims=True)

        @pl.when(i == nt - 1)
        def _():
            dwo_ref[...] = acco[...].astype(BF)

    row = lambda w: pl.BlockSpec((tm, w), lambda i: (nt - 1 - i, 0))
    return _call(
        body, exch,
        name=name,
        grid=(nt,),
        in_specs=[
            row(D_MODEL), row(D_MODEL), row(ATTN_W), row(CONV_W), row(3 * CONV_W), row(D_MODEL), _full((D_MODEL, D_MODEL)),
            _full((1, D_MODEL)), _full((1, ATTN_W)), _full((1, CONV_W)), _full((8, CONV_W)),
        ],
        out_specs=[row(ATTN_W), row(3 * CONV_W), _full_out((8, D_MODEL)), _full_out((D_MODEL, D_MODEL))],
        out_shape=[
            jax.ShapeDtypeStruct((lp, ATTN_W), F32),
            jax.ShapeDtypeStruct((lp, 3 * CONV_W), BF),
            jax.ShapeDtypeStruct((8, D_MODEL), F32),
            jax.ShapeDtypeStruct((D_MODEL, D_MODEL), BF),
        ],
        scratch_shapes=[pltpu.VMEM((tm + 8, CONV_W), F32), pltpu.VMEM((D_MODEL, D_MODEL), F32)],
        compiler_params=_params(),
    )(dh2, z, y_attn, yc, bch, y, w_out, g_post, g_a, g_c, conv_w)


def _attn_bwd(qkv, o, do, probs, p_sink, rope, name, exch=None):
    lp = qkv.shape[0]
    nb = lp // BLOCK

    def body(q_ref, kvc_ref, kvp_ref, o_ref, do_ref, p_ref, ps_ref, cq_ref, s1q_ref, s2q_ref, ck_ref, s1k_ref, s2k_ref,
             dq_ref, dkv_ref, dsink_ref, carry):
        i = pl.program_id(0)

        @pl.when(i == 0)
        def _():
            carry[...] = jnp.zeros_like(carry)
            dsink_ref[...] = jnp.zeros((8, 128), F32)

        def finish(tot):
            dk = _rope_t(tot[:, :128], ck_ref[...], s1k_ref[...], s2k_ref[...])
            dkv_ref[:, 0:128] = dk.astype(BF)
            dkv_ref[:, 128:256] = tot[:, 128:].astype(BF)

        @pl.when(i < nb)
        def _():
            tri, _ = _fold_masks(i)
            kvc, kvp = kvc_ref[...], kvp_ref[...]
            kk = jnp.concatenate([kvp[:, :128], kvc[:, :128]], axis=0)
            vv = jnp.concatenate([kvp[:, 128:], kvc[:, 128:]], axis=0)
            lane = lax.broadcasted_iota(jnp.int32, (BLOCK, 128), 1)
            lane2 = lax.broadcasted_iota(jnp.int32, (2 * BLOCK, 128), 1)
            rope_q = (cq_ref[...], s1q_ref[...], s2q_ref[...])
            deltas = jnp.zeros((BLOCK, 128), F32)
            folded = []
            for kvh in range(2):
                c0 = 256 * kvh
                q2 = jnp.concatenate([q_ref[:, c0:c0 + 128], q_ref[:, c0 + 128:c0 + 256]], axis=0)
                do2 = jnp.concatenate([do_ref[:, c0:c0 + 128], do_ref[:, c0 + 128:c0 + 256]], axis=0)
                o2 = jnp.concatenate([o_ref[:, c0:c0 + 128], o_ref[:, c0 + 128:c0 + 256]], axis=0).astype(F32)
                k4, v4 = _kv_operand(kk, kvh), _kv_operand(vv, kvh)
                prod = do2 * o2
                dob = do2.astype(BF)
                dp = _dot_nt(dob, v4)
                ds, pb = [], []
                for half in range(2):
                    heads = [4 * kvh + 2 * pair + half for pair in range(2)]
                    p = jnp.concatenate([p_ref[:, 128 * h:128 * (h + 1)] for h in heads], axis=0)
                    sel = (lane2 < HEAD_DIM) if half == 0 else (lane2 >= HEAD_DIM)
                    delta = jnp.sum(jnp.where(sel, prod, 0.0), axis=-1, keepdims=True)
                    dp_h = dp[:, 2 * half * BLOCK:2 * (half + 1) * BLOCK]
                    ds.append((p.astype(F32) * (jnp.where(tri, dp_h[:, :BLOCK], dp_h[:, BLOCK:]) - delta)).astype(BF))
                    pb.append(p)
                    for pair in range(2):
                        deltas = jnp.where(lane == heads[pair], delta[pair * BLOCK:(pair + 1) * BLOCK], deltas)
                ds4, p4 = _split4(ds, tri), _split4(pb, tri)
                dq2 = _dot(ds4, k4) * SCALE
                dq_ref[:, c0:c0 + 128] = _rope_t(dq2[:BLOCK], *rope_q).astype(BF)
                dq_ref[:, c0 + 128:c0 + 256] = _rope_t(dq2[BLOCK:], *rope_q).astype(BF)
                rk, rv = _dot_tn(ds4, q2), _dot_tn(p4, dob)
                own = (lane < HEAD_DIM) if kvh == 0 else (lane >= HEAD_DIM)
                group = []
                for r in (rk, rv):
                    for blk in range(2):
                        t = jnp.where(lane < HEAD_DIM, r[blk * BLOCK:(blk + 1) * BLOCK], r[(2 + blk) * BLOCK:(3 + blk) * BLOCK])
                        group.append(jnp.where(own, t + pltpu.roll(t, HEAD_DIM, 1), 0.0))
                folded.append(group)
            dsink_ref[ROW_SINK:ROW_SINK + 1, :] -= jnp.sum(ps_ref[...] * deltas, axis=0, keepdims=True)
            dk_p, dk_c, dv_p, dv_c = [folded[0][t] + folded[1][t] for t in range(4)]
            finish(carry[...] + jnp.concatenate([dk_p, dv_p], axis=1))
            carry[...] = jnp.concatenate([dk_c, dv_c], axis=1)

        @pl.when(i == nb)
        def _():
            finish(carry[...])

    qi = lambda i: jnp.minimum(i, nb - 1)
    ki = lambda i: jnp.maximum(i - 1, 0)
    tab_q = pl.BlockSpec((BLOCK, 128), lambda i: (qi(i), 0))
    tab_k = pl.BlockSpec((BLOCK, 128), lambda i: (ki(i), 0))
    return _call(
        body, exch,
        name=name,
        grid=(nb + 1,),
        in_specs=[
            pl.BlockSpec((BLOCK, ATTN_W), lambda i: (qi(i), 0)),
            pl.BlockSpec((BLOCK, 256), lambda i: (qi(i), 2)),
            pl.BlockSpec((BLOCK, 256), lambda i: (jnp.maximum(qi(i) - 1, 0), 2)),
            pl.BlockSpec((BLOCK, ATTN_W), lambda i: (qi(i), 0)),
            pl.BlockSpec((BLOCK, ATTN_W), lambda i: (qi(i), 0)),
            pl.BlockSpec((BLOCK, N_Q_HEADS * BLOCK), lambda i: (qi(i), 0)),
            tab_q, tab_q, tab_q, tab_q, tab_k, tab_k, tab_k,
        ],
        out_specs=[
            pl.BlockSpec((BLOCK, ATTN_W), lambda i: (qi(i), 0)),
            pl.BlockSpec((BLOCK, 256), lambda i: (ki(i), 0)),
            pl.BlockSpec((8, 128), lambda i: (0, 0)),
        ],
        out_shape=[
            jax.ShapeDtypeStruct((lp, ATTN_W), BF),
            jax.ShapeDtypeStruct((lp, 256), BF),
            jax.ShapeDtypeStruct((8, 128), F32),
        ],
        scratch_shapes=[pltpu.VMEM((BLOCK, 256), F32)],
        compiler_params=_params(),
    )(qkv, qkv, qkv, o, do, probs, p_sink, *rope, *rope)


def _in_proj_bwd_dx(dq, dkv, dbch, w_in_t, h, dh2, g, tm, name, exch=None):
    lp = h.shape[0]

    def body(dq_ref, dkv_ref, dbch_ref, w_ref, h_ref, dh2_ref, g_ref, dh_ref, dg_ref):
        i = pl.program_id(0)

        @pl.when(i == 0)
        def _():
            dg_ref[...] = jnp.zeros((8, D_MODEL), F32)

        da = _dot(jnp.concatenate([dq_ref[...], dkv_ref[...], dbch_ref[...]], axis=1), w_ref[...])
        dh, dg = _rms_bwd(h_ref[...], g_ref[...], da)
        dg_ref[ROW_MIX_PRE:ROW_MIX_PRE + 1, :] += dg
        dh_ref[...] = dh2_ref[...] + dh

    row = lambda w: pl.BlockSpec((tm, w), lambda i: (i, 0))
    return _call(
        body, exch,
        name=name,
        grid=(lp // tm,),
        in_specs=[row(ATTN_W), row(256), row(3 * CONV_W), _full((IN_W, D_MODEL)), row(D_MODEL), row(D_MODEL), _full((1, D_MODEL))],
        out_specs=[row(D_MODEL), _full_out((8, D_MODEL))],
        out_shape=[jax.ShapeDtypeStruct((lp, D_MODEL), F32), jax.ShapeDtypeStruct((8, D_MODEL), F32)],
        compiler_params=_params(),
    )(dq, dkv, dbch, w_in_t, h, dh2, g)


def _in_proj_bwd_first(dq, dkv, dbch, w_in_t, h, dh2, g, tm, name, exch=None):
    lp = h.shape[0]
    nt = lp // tm

    def body(dq_ref, dkv_ref, dbch_ref, w_ref, h_ref, dh2_ref, g_ref, gx_ref, meta_ref, dg_ref, stage, sem):
        i = pl.program_id(0)
        slot = i % 2

        @pl.when(i == 0)
        def _():
            dg_ref[...] = jnp.zeros((8, D_MODEL), F32)

        da = _dot(jnp.concatenate([dq_ref[...], dkv_ref[...], dbch_ref[...]], axis=1), w_ref[...])
        dh, dg = _rms_bwd(h_ref[...], g_ref[...], da)
        dg_ref[ROW_MIX_PRE:ROW_MIX_PRE + 1, :] += dg
        stage[slot] = dh2_ref[...] + dh

        def tile_out(s, r0):
            return pltpu.make_async_copy(stage.at[s], gx_ref.at[pl.ds(r0, tm)], sem.at[s])

        @pl.when(i == 0)
        def _():
            meta_ref[...] = stage[0, LEAD_PAD:BLOCK, :]
            first = pltpu.make_async_copy(stage.at[0, pl.ds(BLOCK, tm - BLOCK)], gx_ref.at[pl.ds(0, tm - BLOCK)], sem.at[0])
            first.start()
            first.wait()

        @pl.when(i > 0)
        def _():
            tile_out(slot, pl.multiple_of(i * tm - BLOCK, BLOCK)).start()

        @pl.when(i > 1)
        def _():
            tile_out(1 - slot, 0).wait()

        @pl.when(i == nt - 1)
        def _():
            tile_out(slot, 0).wait()

    row = lambda w: pl.BlockSpec((tm, w), lambda i: (i, 0))
    return _call(
        body, exch,
        name=name,
        grid=(nt,),
        in_specs=[row(ATTN_W), row(256), row(3 * CONV_W), _full((IN_W, D_MODEL)), row(D_MODEL), row(D_MODEL), _full((1, D_MODEL))],
        out_specs=[pl.BlockSpec(memory_space=pl.ANY), _full_out((N_META, D_MODEL)), _full_out((8, D_MODEL))],
        out_shape=[jax.ShapeDtypeStruct((lp - BLOCK, D_MODEL), F32), jax.ShapeDtypeStruct((N_META, D_MODEL), F32),
                   jax.ShapeDtypeStruct((8, D_MODEL), F32)],
        scratch_shapes=[pltpu.VMEM((2, tm, D_MODEL), F32), pltpu.SemaphoreType.DMA((2,))],
        compiler_params=_params(),
    )(dq, dkv, dbch, w_in_t, h, dh2, g)


def _mix_bwd_dw(dq, dkv, dbch, a, tm, name, exch=None):
    lp = a.shape[0]
    nt = lp // tm

    def body(dq_ref, dkv_ref, dbch_ref, a_ref, dwi_ref, acci):
        i = pl.program_id(0)

        @pl.when(i == 0)
        def _():
            acci[...] = jnp.zeros_like(acci)

        a_v = a_ref[...]
        acci[0:512, :] += _dot_tn(dq_ref[...], a_v)
        acci[512:768, :] += _dot_tn(dkv_ref[...], a_v)
        acci[768:, :] += _dot_tn(dbch_ref[...], a_v)

        @pl.when(i == nt - 1)
        def _():
            dwi_ref[...] = acci[...].astype(BF)

    row = lambda w: pl.BlockSpec((tm, w), lambda i: (i, 0))
    return _call(
        body, exch,
        name=name,
        grid=(nt,),
        in_specs=[row(ATTN_W), row(256), row(3 * CONV_W), row(D_MODEL)],
        out_specs=[_full_out((IN_W, D_MODEL))],
        out_shape=[jax.ShapeDtypeStruct((IN_W, D_MODEL), BF)],
        scratch_shapes=[pltpu.VMEM((IN_W, D_MODEL), F32)],
        compiler_params=_params(),
    )(dq, dkv, dbch, a)


def _mesh_place():
    x, y, c = lax.axis_index("x"), lax.axis_index("y"), lax.axis_index("c")
    return x, y, c, 4 * x + 2 * y + c


def _peer(x, y, c, k):
    px = 1 - x if k & 4 else x
    py = 1 - y if k & 2 else y
    pc = 1 - c if k & 1 else c
    return (px, py, pc), 4 * px + 2 * py + pc


SIBLING = 1
SAME_CORE = (2, 4, 6)
OTHER_CORE = (3, 5, 7)


class _Exchange:
    def __init__(self, pieces, forward_lead=8):
        self.forward_lead = forward_lead
        self.srcs = [s for s, _ in pieces]
        self.to_all = [g for _, g in pieces]
        self.n = len(pieces)
        self.land_shapes = [
            jax.ShapeDtypeStruct((N_DEV,) + (s.shape if g else s.shape[1:]), s.dtype) for s, g in pieces]
        self.sem_shapes = [pltpu.SemaphoreType.DMA((self.n, N_DEV - 1)), pltpu.SemaphoreType.DMA((self.n, N_DEV - 1)),
                           pltpu.SemaphoreType.DMA((self.n,))]
        self.forwards = any(self.to_all)

    def _ops(self, srcs, lands, sems):
        send_sems, recv_sems, local_sems = sems
        x, y, c, me = _mesh_place()

        def remote(p, k, src, slot, to):
            return pltpu.make_async_remote_copy(
                src_ref=src, dst_ref=lands[p].at[slot], send_sem=send_sems.at[p, k - 1], recv_sem=recv_sems.at[p, k - 1],
                device_id=to, device_id_type=MESH)

        def own(p):
            return pltpu.make_async_copy(srcs[p] if self.to_all[p] else srcs[p].at[me], lands[p].at[me], local_sems.at[p])

        def direct(p, k):
            peer, pidx = _peer(x, y, c, k)
            return remote(p, k, srcs[p] if self.to_all[p] else srcs[p].at[pidx], me, peer)

        def forward(p, k):
            sibling, _ = _peer(x, y, c, SIBLING)
            _, origin = _peer(x, y, c, k ^ SIBLING)
            return remote(p, k, lands[p].at[origin], origin, sibling)

        def arrival(p, k):
            peer, pidx = _peer(x, y, c, k)
            return remote(p, k, lands[p].at[pidx], pidx, peer)

        return own, direct, forward, arrival

    def start(self, srcs, lands, sems):
        own, direct, _, _ = self._ops(srcs, lands, sems)
        for p in range(self.n):
            own(p).start()
            for k in ((SIBLING,) + SAME_CORE) if self.to_all[p] else range(1, N_DEV):
                direct(p, k).start()

    def forward(self, srcs, lands, sems):
        _, _, forward, arrival = self._ops(srcs, lands, sems)
        for p in range(self.n):
            if self.to_all[p]:
                for k in SAME_CORE:
                    arrival(p, k).wait_recv()
                    forward(p, k ^ SIBLING).start()

    def finish(self, srcs, lands, sems):
        own, direct, forward, arrival = self._ops(srcs, lands, sems)
        for p in range(self.n):
            for k in ((SIBLING,) + OTHER_CORE) if self.to_all[p] else range(1, N_DEV):
                arrival(p, k).wait_recv()
        for p in range(self.n):
            for k in range(1, N_DEV):
                (forward(p, k) if self.to_all[p] and k in OTHER_CORE else direct(p, k)).wait_send()
            own(p).wait()


class _LayerRows:
    def __init__(self, array, layer):
        self.array = array if array.ndim == 3 else array.reshape(DEPTH, 1, -1)
        self.layer = layer

    def spec(self):
        layer = self.layer
        return pl.BlockSpec((None,) + self.array.shape[1:], lambda *_: (layer, 0, 0), pipeline_mode=pl.Buffered(1))


def _call(body, exch, *, name, grid, in_specs, out_specs, out_shape, scratch_shapes=(), compiler_params, after=None):
    def with_layer_rows(args):
        specs = [a.spec() if isinstance(a, _LayerRows) else s for s, a in zip(in_specs, args)]
        return specs, [a.array if isinstance(a, _LayerRows) else a for a in args]

    if exch is None:
        def plain(*args):
            specs, args = with_layer_rows(args)
            return pl.pallas_call(body, name=name, grid=grid, in_specs=specs, out_specs=out_specs, out_shape=out_shape,
                                  scratch_shapes=scratch_shapes, compiler_params=compiler_params)(*args)
        return plain
    n_in, n_out, n_scr, n_x = len(in_specs), len(out_shape), len(scratch_shapes), exch.n
    steps = math.prod(grid)

    def carrying(*refs):
        a, b, c, d, e = n_in, n_in + n_x, n_in + n_x + n_out, n_in + 2 * n_x + n_out, n_in + 2 * n_x + n_out + n_scr
        ins, srcs, outs, lands, scr, sems = refs[:a], refs[a:b], refs[b:c], refs[c:d], refs[d:e], refs[e:]
        step = functools.reduce(lambda acc, t: acc * grid[t] + pl.program_id(t), range(len(grid)), 0)

        @pl.when(step == 0)
        def _():
            exch.start(srcs, lands, sems)

        body(*ins, *outs, *scr)

        if exch.forwards:
            @pl.when(step == max(0, steps - 1 - pl.cdiv(steps, exch.forward_lead)))
            def _():
                exch.forward(srcs, lands, sems)

        @pl.when(step == steps - 1)
        def _():
            exch.finish(srcs, lands, sems)
            if after is not None:
                after(lands, *ins, *outs, *scr)

    hbm = pl.BlockSpec(memory_space=pl.ANY)

    def run(*args):
        specs, args = with_layer_rows(args)
        res = pl.pallas_call(
            carrying, name=name, grid=grid, in_specs=specs + [hbm] * n_x, out_specs=list(out_specs) + [hbm] * n_x,
            out_shape=list(out_shape) + exch.land_shapes, scratch_shapes=list(scratch_shapes) + exch.sem_shapes,
            compiler_params=compiler_params)(*args, *exch.srcs)
        return list(res[:n_out]), list(res[n_out:])

    return run


def _sum_small(part):
    exch = _Exchange([(part, True)])

    def body(part_ref, out_ref, land, *sems):
        exch.start([part_ref], [land], sems)
        exch.forward([part_ref], [land], sems)
        exch.finish([part_ref], [land], sems)
        acc = land[0]
        for d in range(1, N_DEV):
            acc = acc + land[d]
        out_ref[...] = acc

    vmem = pl.BlockSpec(memory_space=pltpu.VMEM)
    return pl.pallas_call(
        body,
        name="sum_small",
        in_specs=[vmem],
        out_specs=vmem,
        out_shape=jax.ShapeDtypeStruct(part.shape, F32),
        scratch_shapes=[pltpu.VMEM(exch.land_shapes[0].shape, F32)] + exch.sem_shapes,
    )(part)


def _adamw(w, g, m, v):
    m = ADAM_B1 * m + (1.0 - ADAM_B1) * g
    v = ADAM_B2 * v + (1.0 - ADAM_B2) * jnp.square(g)
    m_hat = m / (1.0 - ADAM_B1 ** ADAM_STEP)
    v_hat = v / (1.0 - ADAM_B2 ** ADAM_STEP)
    delta = -ADAM_LR * (m_hat / (jnp.sqrt(v_hat) + ADAM_EPS) + ADAM_WD * w)
    return delta, m, v


def _landed_specs(tr, wd):
    return [pl.BlockSpec((N_DEV, tr, wd), lambda l, i, ll=ll: (0, jnp.where(l == ll, i, 0), 0)) for ll in range(DEPTH)]


def _device_sum(r_ref):
    acc = r_ref[0].astype(F32)
    for d in range(1, N_DEV):
        acc = acc + r_ref[d].astype(F32)
    return acc


def _sum_adamw(recv, w, m, v, tr, name, transposed=False):
    _, r, wd = recv[0].shape

    def body(*refs):
        w_ref, m_ref, v_ref, g_ref, d_ref, mo_ref, vo_ref = refs[DEPTH:]
        for ll in range(DEPTH):
            @pl.when(pl.program_id(0) == ll)
            def _(ll=ll):
                g = _device_sum(refs[ll])
                g = g.T if transposed else g
                g_ref[0] = g
                d_ref[0], mo_ref[0], vo_ref[0] = _adamw(w_ref[0], g, m_ref[0], v_ref[0])

    if transposed:
        blk = pl.BlockSpec((1, wd, tr), lambda l, i: (l, 0, i))
        shape = jax.ShapeDtypeStruct((DEPTH, wd, r), F32)
    else:
        blk = pl.BlockSpec((1, tr, wd), lambda l, i: (l, i, 0))
        shape = jax.ShapeDtypeStruct((DEPTH, r, wd), F32)
    return pl.pallas_call(
        body,
        name=name,
        grid=(DEPTH, r // tr),
        in_specs=_landed_specs(tr, wd) + [blk, blk, blk],
        out_specs=[blk] * 4,
        out_shape=[shape] * 4,
        compiler_params=_params(("arbitrary", "arbitrary")),
    )(*recv, w, m, v)


def _adamw_small(ws, gs, ms, vs):
    n = len(ws)

    def body(*refs):
        w_r, g_r, m_r, v_r = refs[:n], refs[n:2 * n], refs[2 * n:3 * n], refs[3 * n:4 * n]
        d_o, m_o, v_o = refs[4 * n:5 * n], refs[5 * n:6 * n], refs[6 * n:7 * n]
        for t in range(n):
            d_o[t][...], m_o[t][...], v_o[t][...] = _adamw(w_r[t][...], g_r[t][...], m_r[t][...], v_r[t][...])

    vmem = pl.BlockSpec(memory_space=pltpu.VMEM)
    shapes = [jax.ShapeDtypeStruct(w.shape, F32) for w in ws]
    outs = pl.pallas_call(
        body,
        name="adamw_small",
        in_specs=[vmem] * (4 * n),
        out_specs=[vmem] * (3 * n),
        out_shape=shapes * 3,
    )(*ws, *gs, *ms, *vs)
    return outs[:n], outs[n:2 * n], outs[2 * n:]


def kernel(x, meta_tokens, mix_pre_g, w_in, conv_w, sinks, attn_out_g, conv_out_g, w_out, mix_post_g, mlp_pre_g, w_up, w_down, mlp_post_g, loss_target, m_meta_tokens, m_mix_pre_g, m_w_in, m_conv_w, m_sinks, m_attn_out_g, m_conv_out_g, m_w_out, m_mix_post_g, m_mlp_pre_g, m_w_up, m_w_down, m_mlp_post_g, v_meta_tokens, v_mix_pre_g, v_w_in, v_conv_w, v_sinks, v_attn_out_g, v_conv_out_g, v_w_out, v_mix_post_g, v_mlp_pre_g, v_w_up, v_w_down, v_mlp_post_g):
    seq = x.shape[1]
    lp = BLOCK + seq
    tm = _row_tile(lp)
    tm_mlp = _row_tile(lp, (320, 256, 128))
    tm_dw_mlp = _row_tile(lp, (1664, 1040, 640, 384, 256, 128))
    tm_dw_mix = _row_tile(lp, (1664, 832, 640, 384, 256, 128))
    me = 4 * lax.axis_index("x") + 2 * lax.axis_index("y") + lax.axis_index("c")
    cshard = CONV_W // N_DEV
    mshard = D_MODEL // N_DEV

    gather_with = {
        ("in_proj_fwd", 0): [("down", 0)], ("attn_fwd", 0): [("out", 0), ("up", 0)],
        ("mlp_fwd", 0): [("in", 1), ("out", 1), ("up", 1), ("down", 1)],
    }
    tight = {("in_proj_fwd", 0), ("attn_fwd", 0)}
    scatter_with = {
        ("attn_bwd", 1): [("down", 1)], ("mix_bwd_dw", 1): [("out", 1)], ("mlp_bwd_dx", 0): [("up", 1), ("in", 1)],
        ("mix_out_bwd", 0): [("up", 0)], ("attn_bwd", 0): [("down", 0)], ("mix_bwd_dw", 0): [("out", 0)],
        ("in_proj_bwd_dx", 0): [("in", 0)],
    }
    shard = {"in": jnp.swapaxes(w_in, 1, 2).astype(BF), "out": w_out.astype(BF),
             "up": jnp.swapaxes(w_up, 1, 2).astype(BF), "down": w_down.astype(BF)}
    weight = {}
    grad = {}
    landed = {}

    def run(fn, kind, l, *args):
        key, name = (kind, l), f"{kind}_{l}"
        if key in gather_with:
            blocks = gather_with[key]
            lead = 16 if key in tight else 8
            outs, lands = fn(*args, name, _Exchange([(shard[n][k], True) for n, k in blocks], lead))
            for b, land in zip(blocks, lands):
                weight[b] = land.reshape(-1, D_MODEL)
            return outs
        if key in scatter_with:
            blocks = scatter_with[key]
            outs, lands = fn(*args, name, _Exchange([(grad[b].reshape(N_DEV, -1, D_MODEL), False) for b in blocks]))
            landed.update(zip(blocks, lands))
            return outs
        return fn(*args, name)

    small = jnp.zeros((24, 128), F32)
    small = small.at[0:N_META, :].set(meta_tokens)
    small = small.at[N_META:N_META + 6, 0:cshard].set(conv_w.reshape(6, cshard))
    first = _Exchange([(shard["in"][0], True), (small, True)], 16)
    h, rope, (first_in, g_small) = _build_h(x[0], _rope_table(lp), tm, first, 1, "build_h")
    weight[("in", 0)] = first_in.reshape(-1, D_MODEL)
    cw = g_small[:, N_META:N_META + 6, 0:cshard].reshape(N_DEV, DEPTH, 3, cshard)
    cw = jnp.transpose(cw, (1, 2, 0, 3)).reshape(DEPTH, 3, CONV_W)
    conv_full = jnp.concatenate([cw, jnp.zeros((DEPTH, 5, CONV_W), F32)], axis=1)

    row1 = _LayerRows

    saved = []
    for l in range(DEPTH):
        a, qkv, bch = run(_in_proj_fwd, "in_proj_fwd", l, h, row1(mix_pre_g, l), weight[("in", l)], rope, tm)
        y_attn, probs, p_sink = run(_attn_fwd, "attn_fwd", l, qkv, sinks[l].reshape(1, -1))
        yc, y, z, h2 = run(_mix_out_fwd, "mix_out_fwd", l, bch, y_attn, h, row1(conv_full, l), row1(attn_out_g, l),
                       row1(conv_out_g, l), weight[("out", l)], row1(mix_post_g, l), tm)
        mlp = _mlp_fwd if l < DEPTH - 1 else functools.partial(_mlp_fwd, target=loss_target[0])
        a2, up, f, *rest = run(mlp, "mlp_fwd", l, h2, row1(mlp_pre_g, l), weight[("up", l)], weight[("down", l)],
                               row1(mlp_post_g, l), tm_mlp)
        saved.append((h, a, qkv, bch, y_attn, probs, p_sink, yc, y, z, h2, a2, up, f))
        h = rest[0]
    dh, loss_part = rest[0], rest[1][0, 0] * (0.5 / D_MODEL)

    gsmall = [None] * DEPTH
    for l in reversed(range(DEPTH)):
        h0, a, qkv, bch, y_attn, probs, p_sink, yc, y, z, h2, a2, up, f = saved[l]
        df, dup, dh2, dg_mlp = run(_mlp_bwd_dx, "mlp_bwd_dx", l, dh, f, up, h2, weight[("down", l)], weight[("up", l)],
                                   row1(mlp_post_g, l), row1(mlp_pre_g, l), tm_mlp)
        grad[("down", l)], grad[("up", l)] = _mlp_bwd_dw(up, df, dup, a2, tm_dw_mlp, f"mlp_bwd_dw_{l}")
        dya, dbch, dg_mix, grad[("out", l)] = run(
            _mix_out_bwd, "mix_out_bwd", l, dh2, z, y_attn, yc, bch, y, weight[("out", l)], row1(mix_post_g, l),
            row1(attn_out_g, l), row1(conv_out_g, l), row1(conv_full, l), tm)
        dq, dkv, dsink = run(_attn_bwd, "attn_bwd", l, qkv, y_attn, dya, probs, p_sink, rope)
        grad[("in", l)], = run(_mix_bwd_dw, "mix_bwd_dw", l, dq, dkv, dbch, a, tm_dw_mix)
        *dhs, dg_in = run(_in_proj_bwd_dx if l else _in_proj_bwd_first, "in_proj_bwd_dx", l, dq, dkv, dbch,
                          weight[("in", l)], h0, dh2, row1(mix_pre_g, l), tm)
        dh = dhs[0]
        tile_a =dg_mlp + dg_in + jnp.pad(dsink, ((0, 0), (0, D_MODEL - 128)))
        gsmall[l] = (tile_a, dg_mix)
    grad_x, d_meta = dhs[0][None], dhs[1]

    loss_tile = jnp.zeros((8, D_MODEL), F32).at[ROW_LOSS, 0].set(loss_part)
    tot = _sum_small(jnp.concatenate(
        [gsmall[0][0] + loss_tile, gsmall[0][1], gsmall[1][0], gsmall[1][1], d_meta], axis=0))
    loss = tot[ROW_LOSS, 0]
    ta = [tot[16 * l:16 * l + 8] for l in range(DEPTH)]
    tb = [tot[16 * l + 8:16 * l + 16] for l in range(DEPTH)]
    pick = lambda tiles, r0, r1, c0, c1: jnp.stack([t[r0:r1, c0:c1] for t in tiles])
    g_mlp_post = pick(ta, ROW_MLP_POST, ROW_MLP_POST + 1, 0, D_MODEL).reshape(DEPTH, D_MODEL)
    g_mlp_pre = pick(ta, ROW_MLP_PRE, ROW_MLP_PRE + 1, 0, D_MODEL).reshape(DEPTH, D_MODEL)
    g_mix_pre = pick(ta, ROW_MIX_PRE, ROW_MIX_PRE + 1, 0, D_MODEL).reshape(DEPTH, D_MODEL)
    g_sinks = pick(ta, ROW_SINK, ROW_SINK + 1, 0, N_Q_HEADS).reshape(DEPTH, N_Q_HEADS)
    g_mix_post = pick(tb, ROW_MIX_POST, ROW_MIX_POST + 1, 0, D_MODEL).reshape(DEPTH, D_MODEL)
    g_attn_out = pick(tb, ROW_GROUP_G, ROW_GROUP_G + 1, 0, ATTN_W).reshape(DEPTH, ATTN_W)
    g_conv_out = pick(tb, ROW_GROUP_G, ROW_GROUP_G + 1, ATTN_W, D_MODEL).reshape(DEPTH, CONV_W)
    g_conv_full = pick(tb, ROW_CONV, ROW_CONV + 3, 0, CONV_W)
    g_conv = lax.dynamic_slice_in_dim(g_conv_full, me * cshard, cshard, axis=2)
    g_meta = lax.dynamic_slice_in_dim(tot[16 * DEPTH:16 * DEPTH + N_META], me * mshard, mshard, axis=1)

    r_in, r_out, r_up, r_down = [[landed[(n, l)] for l in range(DEPTH)] for n in ("in", "out", "up", "down")]
    t12 = lambda a: jnp.swapaxes(a, 1, 2)
    g_w_in, d_w_in, nm_w_in, nv_w_in = map(t12, _sum_adamw(r_in, t12(w_in), t12(m_w_in), t12(v_w_in), 96, "adamw_w_in"))
    g_w_up, d_w_up, nm_w_up, nv_w_up = _sum_adamw(r_up, w_up, m_w_up, v_w_up, 128, "adamw_w_up", transposed=True)
    g_w_out, d_w_out, nm_w_out, nv_w_out = _sum_adamw(r_out, w_out, m_w_out, v_w_out, 128, "adamw_w_out")
    g_w_down, d_w_down, nm_w_down, nv_w_down = _sum_adamw(r_down, w_down, m_w_down, v_w_down, 128, "adamw_w_down")

    ws = [meta_tokens, mix_pre_g, conv_w.reshape(6, cshard), sinks, attn_out_g, conv_out_g, mix_post_g, mlp_pre_g, mlp_post_g]
    gs = [g_meta, g_mix_pre, g_conv.reshape(6, cshard), g_sinks, g_attn_out, g_conv_out, g_mix_post, g_mlp_pre, g_mlp_post]
    ms = [m_meta_tokens, m_mix_pre_g, m_conv_w.reshape(6, cshard), m_sinks, m_attn_out_g, m_conv_out_g, m_mix_post_g,
          m_mlp_pre_g, m_mlp_post_g]
    vs = [v_meta_tokens, v_mix_pre_g, v_conv_w.reshape(6, cshard), v_sinks, v_attn_out_g, v_conv_out_g, v_mix_post_g,
          v_mlp_pre_g, v_mlp_post_g]
    ds, nms, nvs = _adamw_small(ws, gs, ms, vs)

    def order(meta, mix_pre, cv, sk, a_out, c_out, mix_post, mlp_pre, mlp_post, win, wout, wup, wdown):
        return [meta, mix_pre, win, cv.reshape(DEPTH, 3, cshard), sk, a_out, c_out, wout, mix_post, mlp_pre, wup, wdown, mlp_post]

    grads = order(*gs, g_w_in, g_w_out, g_w_up, g_w_down)
    deltas = order(*ds, d_w_in, d_w_out, d_w_up, d_w_down)
    new_m = order(*nms, nm_w_in, nm_w_out, nm_w_up, nm_w_down)
    new_v = order(*nvs, nv_w_in, nv_w_out, nv_w_up, nv_w_down)
    return (loss, grad_x, *grads, *deltas, *new_m, *new_v)
```

```python
import functools
import math

import jax
import jax.numpy as jnp
import numpy as np
from jax import lax
from jax.experimental import pallas as pl
from jax.experimental.pallas import tpu as pltpu

F32 = jnp.float32
BF = jnp.bfloat16

D_MODEL = 1024
ATTN_W = 512
CONV_W = 512
HEAD_DIM = 64
N_Q_HEADS = 8
ROT_DIM = 16
D_FF = 4096
IN_W = 2304
N_META = 16
BLOCK = 128
LEAD_PAD = BLOCK - N_META
ROPE_THETA = 500000.0
EPS = 1e-6
N_DEV = 8
DEPTH = 2
NEG = -1e30
SCALE = HEAD_DIM ** -0.5

ADAM_LR = 0.001
ADAM_B1 = 0.9
ADAM_B2 = 0.999
ADAM_EPS = 1e-08
ADAM_WD = 0.01
ADAM_STEP = 10

ROW_MLP_POST, ROW_MLP_PRE, ROW_MIX_PRE, ROW_SINK, ROW_LOSS = 0, 1, 2, 3, 4
ROW_MIX_POST, ROW_GROUP_G, ROW_CONV = 0, 1, 2

VMEM_LIMIT = 56 * 1024 * 1024
MESH = pl.DeviceIdType.MESH


def _dot(a, b):
    return jnp.dot(a, b, preferred_element_type=F32)


def _dot_nt(a, b):
    return lax.dot_general(a, b, (((1,), (1,)), ((), ())), preferred_element_type=F32)


def _dot_tn(a, b):
    return lax.dot_general(a, b, (((0,), (0,)), ((), ())), preferred_element_type=F32)


def _rms_fwd(x, g):
    r = lax.rsqrt(jnp.mean(x * x, axis=-1, keepdims=True) + EPS)
    return x * r * g


def _rms_bwd(x, g, dy):
    r = lax.rsqrt(jnp.mean(x * x, axis=-1, keepdims=True) + EPS)
    xh = x * r
    t = dy * g
    dx = r * (t - xh * jnp.mean(t * xh, axis=-1, keepdims=True))
    dg = jnp.sum(dy * xh, axis=0, keepdims=True)
    return dx, dg


def _row_tile(lp, cands=(640, 512, 384, 256, 128)):
    for t in cands:
        if lp % t == 0:
            return t
    raise ValueError(f"row count {lp} is not a multiple of 128")


def _full(shape):
    n = len(shape)
    return pl.BlockSpec(shape, lambda *_: (0,) * n, pipeline_mode=pl.Buffered(1))


def _full_out(shape):
    n = len(shape)
    return pl.BlockSpec(shape, lambda *_: (0,) * n)


def _params(sem=("arbitrary",)):
    return pltpu.CompilerParams(dimension_semantics=sem, vmem_limit_bytes=VMEM_LIMIT)


def _rope_table(lp):
    pos = np.maximum(np.arange(lp) - LEAD_PAD, 0).astype(np.float32)
    inv_freq = np.power(np.float32(ROPE_THETA), -np.arange(0, ROT_DIM, 2, dtype=np.float32) / np.float32(ROT_DIM))
    ang = pos[:, None] * inv_freq.astype(np.float32)[None, :]
    table = np.zeros((lp, 128), np.float32)
    table[:, 0:ROT_DIM // 2] = np.cos(ang)
    table[:, ROT_DIM // 2:ROT_DIM] = np.sin(ang)
    return jnp.asarray(table)


def _rope_coeffs(t):
    half = ROT_DIM // 2
    lane = lax.broadcasted_iota(jnp.int32, t.shape, 1)
    cos_a = jnp.where(lane < half, t, 0.0)
    sin_a = pltpu.roll(jnp.where((lane >= half) & (lane < ROT_DIM), t, 0.0), 128 - half, 1)
    c = cos_a + pltpu.roll(cos_a, half, 1) + jnp.where((lane >= ROT_DIM) & (lane < HEAD_DIM), 1.0, 0.0)
    s2 = pltpu.roll(sin_a, half, 1)
    both = lambda u: u + pltpu.roll(u, HEAD_DIM, 1)
    return both(c), both(-sin_a), both(s2)


def _rope(t, c, s1, s2):
    return t * c + pltpu.roll(t, BLOCK - 8, 1) * s1 + pltpu.roll(t, 8, 1) * s2


def _rope_t(dt, c, s1, s2):
    return dt * c + pltpu.roll(dt * s1, 8, 1) + pltpu.roll(dt * s2, BLOCK - 8, 1)


def _build_h(x, rope_compact, tm, exch, small_piece, name):
    seq = x.shape[0]
    lp = BLOCK + seq
    nt = lp // tm
    n_sub = tm // BLOCK
    small_shape = exch.land_shapes[small_piece].shape

    def body(*refs):
        h_ref, c_ref, s1_ref, s2_ref = refs[n_sub + 1:n_sub + 5]
        for j in range(n_sub):
            h_ref[j * BLOCK:(j + 1) * BLOCK, :] = refs[j][...]
        c_ref[...], s1_ref[...], s2_ref[...] = _rope_coeffs(refs[n_sub][...])

    def after(lands, *refs):
        h_ref, buf = refs[n_sub + 1], refs[n_sub + 5]
        pltpu.sync_copy(lands[small_piece], buf)
        h_ref[0:LEAD_PAD, :] = jnp.zeros((LEAD_PAD, D_MODEL), F32)
        for d in range(N_DEV):
            h_ref[LEAD_PAD:BLOCK, d * 128:(d + 1) * 128] = buf[d, 0:N_META, :]

    tile = lambda i: (i + 1) % nt
    piece = lambda j: pl.BlockSpec((BLOCK, D_MODEL), lambda i: (jnp.maximum(tile(i) * n_sub + j - 1, 0), 0))
    rows = lambda w: pl.BlockSpec((tm, w), lambda i: (tile(i), 0))
    (h, *rope), lands = _call(
        body, exch,
        name=name,
        grid=(nt,),
        in_specs=[piece(j) for j in range(n_sub)] + [rows(128)],
        out_specs=[rows(D_MODEL)] + [rows(128)] * 3,
        out_shape=[jax.ShapeDtypeStruct((lp, D_MODEL), F32)] + [jax.ShapeDtypeStruct((lp, 128), F32)] * 3,
        scratch_shapes=[pltpu.VMEM(small_shape, F32)],
        compiler_params=_params(),
        after=after,
    )(*([x] * n_sub), rope_compact)
    return h, rope, lands


def _in_proj_fwd(h, g, w_in_t, rope, tm, name, exch=None):
    lp = h.shape[0]

    def body(h_ref, g_ref, w_ref, c_ref, s1_ref, s2_ref, a_ref, qkv_ref, bch_ref):
        a = _rms_fwd(h_ref[...], g_ref[...]).astype(BF)
        a_ref[...] = a
        proj = _dot_nt(a, w_ref[...])
        c, s1, s2 = c_ref[...], s1_ref[...], s2_ref[...]
        for j in range(5):
            t = _rope(proj[:, j * 128:(j + 1) * 128], c, s1, s2)
            qkv_ref[:, j * 128:(j + 1) * 128] = (t * SCALE if j < 4 else t).astype(BF)
        qkv_ref[:, 640:768] = proj[:, 640:768].astype(BF)
        bch_ref[...] = proj[:, 768:].astype(BF)

    row = lambda w: pl.BlockSpec((tm, w), lambda i: (i, 0))
    return _call(
        body, exch,
        name=name,
        grid=(lp // tm,),
        in_specs=[row(D_MODEL), _full((1, D_MODEL)), _full((IN_W, D_MODEL)), row(128), row(128), row(128)],
        out_specs=[row(D_MODEL), row(768), row(3 * CONV_W)],
        out_shape=[
            jax.ShapeDtypeStruct((lp, D_MODEL), BF),
            jax.ShapeDtypeStruct((lp, 768), BF),
            jax.ShapeDtypeStruct((lp, 3 * CONV_W), BF),
        ],
        compiler_params=_params(),
    )(h, g, w_in_t, *rope)


def _fold_masks(i):
    r = lax.broadcasted_iota(jnp.int32, (2 * BLOCK, BLOCK), 0) & (BLOCK - 1)
    c = lax.broadcasted_iota(jnp.int32, (2 * BLOCK, BLOCK), 1)
    tri = c > r
    ok = jnp.where(tri, (i - 1) * BLOCK + c, i * BLOCK + c) >= LEAD_PAD
    return tri, ok


def _kv_operand(x, kvh):
    lane = lax.broadcasted_iota(jnp.int32, x.shape, 1)
    zero = jnp.zeros_like(x)
    if kvh == 0:
        lo = jnp.where(lane < HEAD_DIM, x, zero)
        hi = pltpu.roll(lo, HEAD_DIM, 1)
    else:
        hi = jnp.where(lane >= HEAD_DIM, x, zero)
        lo = pltpu.roll(hi, HEAD_DIM, 1)
    return jnp.concatenate([lo, hi], axis=0)


def _split4(t, tri):
    zero = jnp.zeros_like(t[0])
    return jnp.concatenate(
        [jnp.where(tri, t[0], zero), jnp.where(tri, zero, t[0]), jnp.where(tri, t[1], zero), jnp.where(tri, zero, t[1])], axis=1)


def _sink_cols(sink_ref, kvh):
    first = lax.broadcasted_iota(jnp.int32, (2 * BLOCK, 1), 0) < BLOCK
    return [jnp.where(first, sink_ref[0, 4 * kvh + half], sink_ref[0, 4 * kvh + 2 + half]) for half in range(2)]


def _folded_exp(q2, k4, tri, ok, sks):
    s = _dot_nt(q2, k4)
    es, ss = [], []
    for half in range(2):
        s_h = s[:, 2 * half * BLOCK:2 * (half + 1) * BLOCK]
        sf = jnp.where(ok, jnp.where(tri, s_h[:, :BLOCK], s_h[:, BLOCK:]), NEG)
        m = jnp.maximum(jnp.max(sf, axis=-1, keepdims=True), sks[half])
        es.append(jnp.exp(sf - m))
        ss.append(jnp.exp(sks[half] - m))
    sums = _dot(jnp.concatenate(es, axis=0).astype(BF), jnp.ones((BLOCK, BLOCK), BF))
    invs = [1.0 / (sums[2 * half * BLOCK:2 * (half + 1) * BLOCK] + ss[half]) for half in range(2)]
    return es, ss, invs


def _attn_fwd(qkv, sink, name, exch=None):
    lp = qkv.shape[0]
    nb = lp // BLOCK
    per_step = 4

    def one_block(i, sink_ref, q_ref, kvc_ref, kvp_ref, o_ref, p_ref, ps_ref):
        tri, ok = _fold_masks(i)
        kvc, kvp = kvc_ref[...], kvp_ref[...]
        kk = jnp.concatenate([kvp[:, :128], kvc[:, :128]], axis=0)
        vv = jnp.concatenate([kvp[:, 128:], kvc[:, 128:]], axis=0)
        lane = lax.broadcasted_iota(jnp.int32, (BLOCK, 128), 1)
        p_sink = jnp.zeros((BLOCK, 128), F32)
        for kvh in range(2):
            q2 = jnp.concatenate([q_ref[:, 256 * kvh:256 * kvh + 128], q_ref[:, 256 * kvh + 128:256 * kvh + 256]], axis=0)
            es, ss, invs = _folded_exp(q2, _kv_operand(kk, kvh), tri, ok, _sink_cols(sink_ref, kvh))
            pb = [(es[half] * invs[half]).astype(BF) for half in range(2)]
            out = _dot(_split4(pb, tri), _kv_operand(vv, kvh))
            for pair in range(2):
                rows = slice(pair * BLOCK, (pair + 1) * BLOCK)
                o_ref[:, 256 * kvh + 128 * pair:256 * kvh + 128 * (pair + 1)] = out[rows].astype(BF)
                for half in range(2):
                    head = 4 * kvh + 2 * pair + half
                    p_ref[:, 128 * head:128 * (head + 1)] = pb[half][rows]
                    p_sink = jnp.where(lane == head, (ss[half] * invs[half][:, 0:1])[rows], p_sink)
        ps_ref[...] = p_sink

    def body(sink_ref, *refs):
        q_refs, kv_refs = refs[:per_step], refs[per_step:2 * per_step + 1]
        o_ref, p_ref, ps_ref = refs[2 * per_step + 1:]
        for j in range(per_step):
            rows = slice(j * BLOCK, (j + 1) * BLOCK)
            one_block(per_step * pl.program_id(0) + j, sink_ref, q_refs[j], kv_refs[j + 1], kv_refs[j],
                      o_ref.at[rows], p_ref.at[rows], ps_ref.at[rows])

    last = nb - 1
    blk = lambda j: (lambda s: jnp.minimum(per_step * s + j, last))
    out_rows = lambda w: pl.BlockSpec((per_step * BLOCK, w), lambda s: (s, 0))
    return _call(
        body, exch,
        name=name,
        grid=(pl.cdiv(nb, per_step),),
        in_specs=[pl.BlockSpec(memory_space=pltpu.SMEM)]
        + [pl.BlockSpec((BLOCK, ATTN_W), lambda s, j=j: (blk(j)(s), 0)) for j in range(per_step)]
        + [pl.BlockSpec((BLOCK, 256), lambda s: (jnp.maximum(per_step * s - 1, 0), 2))]
        + [pl.BlockSpec((BLOCK, 256), lambda s, j=j: (blk(j)(s), 2)) for j in range(per_step)],
        out_specs=[out_rows(ATTN_W), out_rows(N_Q_HEADS * BLOCK), out_rows(128)],
        out_shape=[jax.ShapeDtypeStruct((lp, ATTN_W), BF), jax.ShapeDtypeStruct((lp, N_Q_HEADS * BLOCK), BF),
                   jax.ShapeDtypeStruct((lp, 128), F32)],
        compiler_params=_params(),
    )(sink, *([qkv] * (2 * per_step + 1)))


def _mix_out_fwd(bch, y_attn, h, conv_w, g_a, g_c, w_out, g_post, tm, name, exch=None):
    lp = h.shape[0]

    def body(bch_ref, ya_ref, h_ref, cw_ref, ga_ref, gc_ref, w_ref, gp_ref, yc_ref, y_ref, z_ref, h2_ref, ext):
        i = pl.program_id(0)

        @pl.when(i == 0)
        def _():
            ext[0:8, :] = jnp.zeros((8, CONV_W), F32)

        b = bch_ref[:, 0:CONV_W].astype(F32)
        u = bch_ref[:, CONV_W:2 * CONV_W].astype(F32) * bch_ref[:, 2 * CONV_W:3 * CONV_W].astype(F32)
        ext[8:8 + tm, :] = u
        yc = cw_ref[0:1, :] * ext[6:6 + tm, :] + cw_ref[1:2, :] * ext[7:7 + tm, :] + cw_ref[2:3, :] * u
        ext[0:8, :] = u[tm - 8:tm, :]
        yc_ref[...] = yc.astype(BF)
        ya = _rms_fwd(ya_ref[...].astype(F32), ga_ref[...]).astype(BF)
        yb = _rms_fwd(b * yc, gc_ref[...]).astype(BF)
        y_ref[:, 0:ATTN_W] = ya
        y_ref[:, ATTN_W:] = yb
        z = _dot(ya, w_ref[0:ATTN_W, :]) + _dot(yb, w_ref[ATTN_W:, :])
        z_ref[...] = z.astype(BF)
        h2_ref[...] = h_ref[...] + _rms_fwd(z, gp_ref[...])

    row = lambda w: pl.BlockSpec((tm, w), lambda i: (i, 0))
    return _call(
        body, exch,
        name=name,
        grid=(lp // tm,),
        in_specs=[
            row(3 * CONV_W), row(ATTN_W), row(D_MODEL), _full((8, CONV_W)), _full((1, ATTN_W)), _full((1, CONV_W)),
            _full((D_MODEL, D_MODEL)), _full((1, D_MODEL)),
        ],
        out_specs=[row(CONV_W), row(D_MODEL), row(D_MODEL), row(D_MODEL)],
        out_shape=[
            jax.ShapeDtypeStruct((lp, CONV_W), BF),
            jax.ShapeDtypeStruct((lp, D_MODEL), BF),
            jax.ShapeDtypeStruct((lp, D_MODEL), BF),
            jax.ShapeDtypeStruct((lp, D_MODEL), F32),
        ],
        scratch_shapes=[pltpu.VMEM((tm + 8, CONV_W), F32)],
        compiler_params=_params(),
    )(bch, y_attn, h, conv_w, g_a, g_c, w_out, g_post)


def _mlp_fwd(h2, g_pre, w_up_t, w_down, g_post, tm, name, exch=None, target=None):
    lp = h2.shape[0]
    sub = math.gcd(tm, BLOCK)
    n_sub, lead = tm // sub, BLOCK // sub
    n_t = n_sub if target is not None else 0

    def body(*refs):
        h_ref, gp_ref, wu_ref, wd_ref, gq_ref = refs[:5]
        t_refs = refs[5:5 + n_t]
        a_ref, up_ref, f_ref, last_ref = refs[5 + n_t:9 + n_t]
        h = h_ref[...]
        a = _rms_fwd(h, gp_ref[...]).astype(BF)
        a_ref[...] = a
        up = _dot_nt(a, wu_ref[...])
        up_ref[...] = up.astype(BF)
        act = jnp.square(jnp.maximum(up, 0.0)).astype(BF)
        f = _dot(act, wd_ref[...])
        f_ref[...] = f
        h3 = h + _rms_fwd(f, gq_ref[...])
        if target is None:
            last_ref[...] = h3
            return
        ls_ref = refs[9 + n_t]
        i = pl.program_id(0)

        @pl.when(i == 0)
        def _():
            ls_ref[...] = jnp.zeros((8, 128), F32)

        sq = jnp.zeros((8, D_MODEL), F32)
        for j in range(n_sub):
            on_tokens = i * n_sub + j >= lead
            d = jnp.where(on_tokens, h3[j * sub:(j + 1) * sub] - t_refs[j][...], 0.0)
            last_ref[j * sub:(j + 1) * sub, :] = d * (1.0 / D_MODEL)
            sq = sq + jnp.sum((d * d).reshape(sub // 8, 8, D_MODEL), axis=0)
        ls_ref[...] += sum(sq[:, k * 128:(k + 1) * 128] for k in range(D_MODEL // 128))

        @pl.when(i == lp // tm - 1)
        def _():
            ls_ref[...] = jnp.full((8, 128), jnp.sum(ls_ref[...]), F32)

    row = lambda w: pl.BlockSpec((tm, w), lambda i: (i, 0))
    piece = lambda j: pl.BlockSpec((sub, D_MODEL), lambda i: (jnp.maximum(i * n_sub + j - lead, 0), 0))
    out_specs = [row(D_MODEL), row(D_FF), row(D_MODEL), row(D_MODEL)]
    out_shape = [
        jax.ShapeDtypeStruct((lp, D_MODEL), BF),
        jax.ShapeDtypeStruct((lp, D_FF), BF),
        jax.ShapeDtypeStruct((lp, D_MODEL), F32),
        jax.ShapeDtypeStruct((lp, D_MODEL), F32),
    ]
    if target is not None:
        out_specs.append(_full_out((8, 128)))
        out_shape.append(jax.ShapeDtypeStruct((8, 128), F32))
    return _call(
        body, exch,
        name=name,
        grid=(lp // tm,),
        in_specs=[row(D_MODEL), _full((1, D_MODEL)), _full((D_FF, D_MODEL)), _full((D_FF, D_MODEL)), _full((1, D_MODEL))]
        + [piece(j) for j in range(n_t)],
        out_specs=out_specs,
        out_shape=out_shape,
        compiler_params=_params(),
    )(h2, g_pre, w_up_t, w_down, g_post, *([target] * n_t))


def _mlp_bwd_dx(dh3, f, up, h2, w_down, w_up_t, g_post, g_pre, tm, name, exch=None):
    lp = h2.shape[0]

    def body(dh3_ref, f_ref, up_ref, h2_ref, wd_ref, wu_ref, gq_ref, gp_ref, df_ref, dup_ref, dh2_ref, dg_ref):
        i = pl.program_id(0)

        @pl.when(i == 0)
        def _():
            dg_ref[...] = jnp.zeros((8, D_MODEL), F32)

        dh3 = dh3_ref[...]
        df, dgq = _rms_bwd(f_ref[...], gq_ref[...], dh3)
        dg_ref[ROW_MLP_POST:ROW_MLP_POST + 1, :] += dgq
        df = df.astype(BF)
        df_ref[...] = df
        dact = _dot_nt(df, wd_ref[...])
        dup = (dact * (2.0 * jnp.maximum(up_ref[...].astype(F32), 0.0))).astype(BF)
        dup_ref[...] = dup
        da = _dot(dup, wu_ref[...])
        dh, dgp = _rms_bwd(h2_ref[...], gp_ref[...], da)
        dg_ref[ROW_MLP_PRE:ROW_MLP_PRE + 1, :] += dgp
        dh2_ref[...] = dh3 + dh

    row = lambda w: pl.BlockSpec((tm, w), lambda i: (i, 0))
    return _call(
        body, exch,
        name=name,
        grid=(lp // tm,),
        in_specs=[
            row(D_MODEL), row(D_MODEL), row(D_FF), row(D_MODEL), _full((D_FF, D_MODEL)), _full((D_FF, D_MODEL)),
            _full((1, D_MODEL)), _full((1, D_MODEL)),
        ],
        out_specs=[row(D_MODEL), row(D_FF), row(D_MODEL), _full_out((8, D_MODEL))],
        out_shape=[
            jax.ShapeDtypeStruct((lp, D_MODEL), BF),
            jax.ShapeDtypeStruct((lp, D_FF), BF),
            jax.ShapeDtypeStruct((lp, D_MODEL), F32),
            jax.ShapeDtypeStruct((8, D_MODEL), F32),
        ],
        compiler_params=_params(),
    )(dh3, f, up, h2, w_down, w_up_t, g_post, g_pre)


def _mlp_bwd_dw(up, df, dup, a2, tm, name):
    lp = up.shape[0]
    nt = lp // tm
    nj = D_FF // D_MODEL

    def body(up_ref, df_ref, dup_ref, a_ref, dwd_ref, dwu_ref, accd, accu):
        i = pl.program_id(1)

        @pl.when(i == 0)
        def _():
            accd[...] = jnp.zeros_like(accd)
            accu[...] = jnp.zeros_like(accu)

        act = jnp.square(jnp.maximum(up_ref[...].astype(F32), 0.0)).astype(BF)
        accd[...] += _dot_tn(act, df_ref[...])
        accu[...] += _dot_tn(dup_ref[...], a_ref[...])

        @pl.when(i == nt - 1)
        def _():
            dwd_ref[...] = accd[...].astype(BF)
            dwu_ref[...] = accu[...].astype(BF)

    return pl.pallas_call(
        body,
        name=name,
        grid=(nj, nt),
        in_specs=[
            pl.BlockSpec((tm, D_MODEL), lambda j, i: (i, j)),
            pl.BlockSpec((tm, D_MODEL), lambda j, i: (i, 0)),
            pl.BlockSpec((tm, D_MODEL), lambda j, i: (i, j)),
            pl.BlockSpec((tm, D_MODEL), lambda j, i: (i, 0)),
        ],
        out_specs=[pl.BlockSpec((D_MODEL, D_MODEL), lambda j, i: (j, 0)), pl.BlockSpec((D_MODEL, D_MODEL), lambda j, i: (j, 0))],
        out_shape=[jax.ShapeDtypeStruct((D_FF, D_MODEL), BF), jax.ShapeDtypeStruct((D_FF, D_MODEL), BF)],
        scratch_shapes=[pltpu.VMEM((D_MODEL, D_MODEL), F32), pltpu.VMEM((D_MODEL, D_MODEL), F32)],
        compiler_params=_params(("arbitrary", "arbitrary")),
    )(up, df, dup, a2)


def _mix_out_bwd(dh2, z, y_attn, yc, bch, y, w_out, g_post, g_a, g_c, conv_w, tm, name, exch=None):
    lp = dh2.shape[0]
    nt = lp // tm

    def body(dh2_ref, z_ref, ya_ref, yc_ref, bch_ref, y_ref, w_ref, gp_ref, ga_ref, gc_ref, cw_ref,
             dya_ref, dbch_ref, dg_ref, dwo_ref, ext, acco):
        i = pl.program_id(0)
        dcw_ref = dg_ref.at[ROW_CONV:ROW_CONV + 3, 0:CONV_W]

        @pl.when(i == 0)
        def _():
            ext[tm:tm + 8, :] = jnp.zeros((8, CONV_W), F32)
            dg_ref[...] = jnp.zeros((8, D_MODEL), F32)
            acco[...] = jnp.zeros_like(acco)

        dz, dgp = _rms_bwd(z_ref[...].astype(F32), gp_ref[...], dh2_ref[...])
        dg_ref[ROW_MIX_POST:ROW_MIX_POST + 1, :] += dgp
        dz = dz.astype(BF)
        acco[...] += _dot_tn(y_ref[...], dz)
        dya_n = _dot_nt(dz, w_ref[0:ATTN_W, :])
        dyb_n = _dot_nt(dz, w_ref[ATTN_W:, :])
        dya, dga = _rms_bwd(ya_ref[...].astype(F32), ga_ref[...], dya_n)
        dg_ref[ROW_GROUP_G:ROW_GROUP_G + 1, 0:ATTN_W] += dga
        dya_ref[...] = dya
        b = bch_ref[:, 0:CONV_W].astype(F32)
        c = bch_ref[:, CONV_W:2 * CONV_W].astype(F32)
        hc = bch_ref[:, 2 * CONV_W:3 * CONV_W].astype(F32)
        u = c * hc
        yc_v = yc_ref[...].astype(F32)
        dyconv, dgc = _rms_bwd(b * yc_v, gc_ref[...], dyb_n)
        dg_ref[ROW_GROUP_G:ROW_GROUP_G + 1, ATTN_W:] += dgc
        dbch_ref[:, 0:CONV_W] = (dyconv * yc_v).astype(BF)
        dyc = dyconv * b
        ext[0:tm, :] = dyc
        d1 = ext[1:1 + tm, :]
        d2 = ext[2:2 + tm, :]
        du = cw_ref[2:3, :] * dyc + cw_ref[1:2, :] * d1 + cw_ref[0:1, :] * d2
        ext[tm:tm + 8, :] = dyc[0:8, :]
        dbch_ref[:, CONV_W:2 * CONV_W] = (du * hc).astype(BF)
        dbch_ref[:, 2 * CONV_W:3 * CONV_W] = (du * c).astype(BF)
        dcw_ref[0:1, :] += jnp.sum(u * d2, axis=0, keepdims=True)
        dcw_ref[1:2, :] += jnp.sum(u * d1, axis=0, keepdims=True)
        dcw_ref[2:3, :] += jnp.sum(u * dyc, axis=0, keepdims=True)

        @pl.when(i == nt - 1)
        def _():
            dwo_ref[...] = acco[...].astype(BF)

    row = lambda w: pl.BlockSpec((tm, w), lambda i: (nt - 1 - i, 0))
    return _call(
        body, exch,
        name=name,
        grid=(nt,),
        in_specs=[
            row(D_MODEL), row(D_MODEL), row(ATTN_W), row(CONV_W), row(3 * CONV_W), row(D_MODEL), _full((D_MODEL, D_MODEL)),
            _full((1, D_MODEL)), _full((1, ATTN_W)), _full((1, CONV_W)), _full((8, CONV_W)),
        ],
        out_specs=[row(ATTN_W), row(3 * CONV_W), _full_out((8, D_MODEL)), _full_out((D_MODEL, D_MODEL))],
        out_shape=[
            jax.ShapeDtypeStruct((lp, ATTN_W), F32),
            jax.ShapeDtypeStruct((lp, 3 * CONV_W), BF),
            jax.ShapeDtypeStruct((8, D_MODEL), F32),
            jax.ShapeDtypeStruct((D_MODEL, D_MODEL), BF),
        ],
        scratch_shapes=[pltpu.VMEM((tm + 8, CONV_W), F32), pltpu.VMEM((D_MODEL, D_MODEL), F32)],
        compiler_params=_params(),
    )(dh2, z, y_attn, yc, bch, y, w_out, g_post, g_a, g_c, conv_w)


def _attn_bwd(qkv, o, do, probs, p_sink, rope, name, exch=None):
    lp = qkv.shape[0]
    nb = lp // BLOCK

    def body(q_ref, kvc_ref, kvp_ref, o_ref, do_ref, p_ref, ps_ref, cq_ref, s1q_ref, s2q_ref, ck_ref, s1k_ref, s2k_ref,
             dq_ref, dkv_ref, dsink_ref, carry):
        i = pl.program_id(0)

        @pl.when(i == 0)
        def _():
            carry[...] = jnp.zeros_like(carry)
            dsink_ref[...] = jnp.zeros((8, 128), F32)

        def finish(tot):
            dk = _rope_t(tot[:, :128], ck_ref[...], s1k_ref[...], s2k_ref[...])
            dkv_ref[:, 0:128] = dk.astype(BF)
            dkv_ref[:, 128:256] = tot[:, 128:].astype(BF)

        @pl.when(i < nb)
        def _():
            tri, _ = _fold_masks(i)
            kvc, kvp = kvc_ref[...], kvp_ref[...]
            kk = jnp.concatenate([kvp[:, :128], kvc[:, :128]], axis=0)
            vv = jnp.concatenate([kvp[:, 128:], kvc[:, 128:]], axis=0)
            lane = lax.broadcasted_iota(jnp.int32, (BLOCK, 128), 1)
            lane2 = lax.broadcasted_iota(jnp.int32, (2 * BLOCK, 128), 1)
            rope_q = (cq_ref[...], s1q_ref[...], s2q_ref[...])
            deltas = jnp.zeros((BLOCK, 128), F32)
            folded = []
            for kvh in range(2):
                c0 = 256 * kvh
                q2 = jnp.concatenate([q_ref[:, c0:c0 + 128], q_ref[:, c0 + 128:c0 + 256]], axis=0)
                do2 = jnp.concatenate([do_ref[:, c0:c0 + 128], do_ref[:, c0 + 128:c0 + 256]], axis=0)
                o2 = jnp.concatenate([o_ref[:, c0:c0 + 128], o_ref[:, c0 + 128:c0 + 256]], axis=0).astype(F32)
                k4, v4 = _kv_operand(kk, kvh), _kv_operand(vv, kvh)
                prod = do2 * o2
                dob = do2.astype(BF)
                dp = _dot_nt(dob, v4)
                ds, pb = [], []
                for half in range(2):
                    heads = [4 * kvh + 2 * pair + half for pair in range(2)]
                    p = jnp.concatenate([p_ref[:, 128 * h:128 * (h + 1)] for h in heads], axis=0)
                    sel = (lane2 < HEAD_DIM) if half == 0 else (lane2 >= HEAD_DIM)
                    delta = jnp.sum(jnp.where(sel, prod, 0.0), axis=-1, keepdims=True)
                    dp_h = dp[:, 2 * half * BLOCK:2 * (half + 1) * BLOCK]
                    ds.append((p.astype(F32) * (jnp.where(tri, dp_h[:, :BLOCK], dp_h[:, BLOCK:]) - delta)).astype(BF))
                    pb.append(p)
                    for pair in range(2):
                        deltas = jnp.where(lane == heads[pair], delta[pair * BLOCK:(pair + 1) * BLOCK], deltas)
                ds4, p4 = _split4(ds, tri), _split4(pb, tri)
                dq2 = _dot(ds4, k4) * SCALE
                dq_ref[:, c0:c0 + 128] = _rope_t(dq2[:BLOCK], *rope_q).astype(BF)
                dq_ref[:, c0 + 128:c0 + 256] = _rope_t(dq2[BLOCK:], *rope_q).astype(BF)
                rk, rv = _dot_tn(ds4, q2), _dot_tn(p4, dob)
                own = (lane < HEAD_DIM) if kvh == 0 else (lane >= HEAD_DIM)
                group = []
                for r in (rk, rv):
                    for blk in range(2):
                        t = jnp.where(lane < HEAD_DIM, r[blk * BLOCK:(blk + 1) * BLOCK], r[(2 + blk) * BLOCK:(3 + blk) * BLOCK])
                        group.append(jnp.where(own, t + pltpu.roll(t, HEAD_DIM, 1), 0.0))
                folded.append(group)
            dsink_ref[ROW_SINK:ROW_SINK + 1, :] -= jnp.sum(ps_ref[...] * deltas, axis=0, keepdims=True)
            dk_p, dk_c, dv_p, dv_c = [folded[0][t] + folded[1][t] for t in range(4)]
            finish(carry[...] + jnp.concatenate([dk_p, dv_p], axis=1))
            carry[...] = jnp.concatenate([dk_c, dv_c], axis=1)

        @pl.when(i == nb)
        def _():
            finish(carry[...])

    qi = lambda i: jnp.minimum(i, nb - 1)
    ki = lambda i: jnp.maximum(i - 1, 0)
    tab_q = pl.BlockSpec((BLOCK, 128), lambda i: (qi(i), 0))
    tab_k = pl.BlockSpec((BLOCK, 128), lambda i: (ki(i), 0))
    return _call(
        body, exch,
        name=name,
        grid=(nb + 1,),
        in_specs=[
            pl.BlockSpec((BLOCK, ATTN_W), lambda i: (qi(i), 0)),
            pl.BlockSpec((BLOCK, 256), lambda i: (qi(i), 2)),
            pl.BlockSpec((BLOCK, 256), lambda i: (jnp.maximum(qi(i) - 1, 0), 2)),
            pl.BlockSpec((BLOCK, ATTN_W), lambda i: (qi(i), 0)),
            pl.BlockSpec((BLOCK, ATTN_W), lambda i: (qi(i), 0)),
            pl.BlockSpec((BLOCK, N_Q_HEADS * BLOCK), lambda i: (qi(i), 0)),
            tab_q, tab_q, tab_q, tab_q, tab_k, tab_k, tab_k,
        ],
        out_specs=[
            pl.BlockSpec((BLOCK, ATTN_W), lambda i: (qi(i), 0)),
            pl.BlockSpec((BLOCK, 256), lambda i: (ki(i), 0)),
            pl.BlockSpec((8, 128), lambda i: (0, 0)),
        ],
        out_shape=[
            jax.ShapeDtypeStruct((lp, ATTN_W), BF),
            jax.ShapeDtypeStruct((lp, 256), BF),
            jax.ShapeDtypeStruct((8, 128), F32),
        ],
        scratch_shapes=[pltpu.VMEM((BLOCK, 256), F32)],
        compiler_params=_params(),
    )(qkv, qkv, qkv, o, do, probs, p_sink, *rope, *rope)


def _in_proj_bwd_dx(dq, dkv, dbch, w_in_t, h, dh2, g, tm, name, exch=None):
    lp = h.shape[0]

    def body(dq_ref, dkv_ref, dbch_ref, w_ref, h_ref, dh2_ref, g_ref, dh_ref, dg_ref):
        i = pl.program_id(0)

        @pl.when(i == 0)
        def _():
            dg_ref[...] = jnp.zeros((8, D_MODEL), F32)

        da = _dot(jnp.concatenate([dq_ref[...], dkv_ref[...], dbch_ref[...]], axis=1), w_ref[...])
        dh, dg = _rms_bwd(h_ref[...], g_ref[...], da)
        dg_ref[ROW_MIX_PRE:ROW_MIX_PRE + 1, :] += dg
        dh_ref[...] = dh2_ref[...] + dh

    row = lambda w: pl.BlockSpec((tm, w), lambda i: (i, 0))
    return _call(
        body, exch,
        name=name,
        grid=(lp // tm,),
        in_specs=[row(ATTN_W), row(256), row(3 * CONV_W), _full((IN_W, D_MODEL)), row(D_MODEL), row(D_MODEL), _full((1, D_MODEL))],
        out_specs=[row(D_MODEL), _full_out((8, D_MODEL))],
        out_shape=[jax.ShapeDtypeStruct((lp, D_MODEL), F32), jax.ShapeDtypeStruct((8, D_MODEL), F32)],
        compiler_params=_params(),
    )(dq, dkv, dbch, w_in_t, h, dh2, g)


def _in_proj_bwd_first(dq, dkv, dbch, w_in_t, h, dh2, g, tm, name, exch=None):
    lp = h.shape[0]
    nt = lp // tm

    def body(dq_ref, dkv_ref, dbch_ref, w_ref, h_ref, dh2_ref, g_ref, gx_ref, meta_ref, dg_ref, stage, sem):
        i = pl.program_id(0)
        slot = i % 2

        @pl.when(i == 0)
        def _():
            dg_ref[...] = jnp.zeros((8, D_MODEL), F32)

        da = _dot(jnp.concatenate([dq_ref[...], dkv_ref[...], dbch_ref[...]], axis=1), w_ref[...])
        dh, dg = _rms_bwd(h_ref[...], g_ref[...], da)
        dg_ref[ROW_MIX_PRE:ROW_MIX_PRE + 1, :] += dg
        stage[slot] = dh2_ref[...] + dh

        def tile_out(s, r0):
            return pltpu.make_async_copy(stage.at[s], gx_ref.at[pl.ds(r0, tm)], sem.at[s])

        @pl.when(i == 0)
        def _():
            meta_ref[...] = stage[0, LEAD_PAD:BLOCK, :]
            first = pltpu.make_async_copy(stage.at[0, pl.ds(BLOCK, tm - BLOCK)], gx_ref.at[pl.ds(0, tm - BLOCK)], sem.at[0])
            first.start()
            first.wait()

        @pl.when(i > 0)
        def _():
            tile_out(slot, pl.multiple_of(i * tm - BLOCK, BLOCK)).start()

        @pl.when(i > 1)
        def _():
            tile_out(1 - slot, 0).wait()

        @pl.when(i == nt - 1)
        def _():
            tile_out(slot, 0).wait()

    row = lambda w: pl.BlockSpec((tm, w), lambda i: (i, 0))
    return _call(
        body, exch,
        name=name,
        grid=(nt,),
        in_specs=[row(ATTN_W), row(256), row(3 * CONV_W), _full((IN_W, D_MODEL)), row(D_MODEL), row(D_MODEL), _full((1, D_MODEL))],
        out_specs=[pl.BlockSpec(memory_space=pl.ANY), _full_out((N_META, D_MODEL)), _full_out((8, D_MODEL))],
        out_shape=[jax.ShapeDtypeStruct((lp - BLOCK, D_MODEL), F32), jax.ShapeDtypeStruct((N_META, D_MODEL), F32),
                   jax.ShapeDtypeStruct((8, D_MODEL), F32)],
        scratch_shapes=[pltpu.VMEM((2, tm, D_MODEL), F32), pltpu.SemaphoreType.DMA((2,))],
        compiler_params=_params(),
    )(dq, dkv, dbch, w_in_t, h, dh2, g)


def _mix_bwd_dw(dq, dkv, dbch, a, tm, name, exch=None):
    lp = a.shape[0]
    nt = lp // tm

    def body(dq_ref, dkv_ref, dbch_ref, a_ref, dwi_ref, acci):
        i = pl.program_id(0)

        @pl.when(i == 0)
        def _():
            acci[...] = jnp.zeros_like(acci)

        a_v = a_ref[...]
        acci[0:512, :] += _dot_tn(dq_ref[...], a_v)
        acci[512:768, :] += _dot_tn(dkv_ref[...], a_v)
        acci[768:, :] += _dot_tn(dbch_ref[...], a_v)

        @pl.when(i == nt - 1)
        def _():
            dwi_ref[...] = acci[...].astype(BF)

    row = lambda w: pl.BlockSpec((tm, w), lambda i: (i, 0))
    return _call(
        body, exch,
        name=name,
        grid=(nt,),
        in_specs=[row(ATTN_W), row(256), row(3 * CONV_W), row(D_MODEL)],
        out_specs=[_full_out((IN_W, D_MODEL))],
        out_shape=[jax.ShapeDtypeStruct((IN_W, D_MODEL), BF)],
        scratch_shapes=[pltpu.VMEM((IN_W, D_MODEL), F32)],
        compiler_params=_params(),
    )(dq, dkv, dbch, a)


def _mesh_place():
    x, y, c = lax.axis_index("x"), lax.axis_index("y"), lax.axis_index("c")
    return x, y, c, 4 * x + 2 * y + c


def _peer(x, y, c, k):
    px = 1 - x if k & 4 else x
    py = 1 - y if k & 2 else y
    pc = 1 - c if k & 1 else c
    return (px, py, pc), 4 * px + 2 * py + pc


SIBLING = 1
SAME_CORE = (2, 4, 6)
OTHER_CORE = (3, 5, 7)


class _Exchange:
    def __init__(self, pieces, forward_lead=8):
        self.forward_lead = forward_lead
        self.srcs = [s for s, _ in pieces]
        self.to_all = [g for _, g in pieces]
        self.n = len(pieces)
        self.land_shapes = [
            jax.ShapeDtypeStruct((N_DEV,) + (s.shape if g else s.shape[1:]), s.dtype) for s, g in pieces]
        self.sem_shapes = [pltpu.SemaphoreType.DMA((self.n, N_DEV - 1)), pltpu.SemaphoreType.DMA((self.n, N_DEV - 1)),
                           pltpu.SemaphoreType.DMA((self.n,))]
        self.forwards = any(self.to_all)

    def _ops(self, srcs, lands, sems):
        send_sems, recv_sems, local_sems = sems
        x, y, c, me = _mesh_place()

        def remote(p, k, src, slot, to):
            return pltpu.make_async_remote_copy(
                src_ref=src, dst_ref=lands[p].at[slot], send_sem=send_sems.at[p, k - 1], recv_sem=recv_sems.at[p, k - 1],
                device_id=to, device_id_type=MESH)

        def own(p):
            return pltpu.make_async_copy(srcs[p] if self.to_all[p] else srcs[p].at[me], lands[p].at[me], local_sems.at[p])

        def direct(p, k):
            peer, pidx = _peer(x, y, c, k)
            return remote(p, k, srcs[p] if self.to_all[p] else srcs[p].at[pidx], me, peer)

        def forward(p, k):
            sibling, _ = _peer(x, y, c, SIBLING)
            _, origin = _peer(x, y, c, k ^ SIBLING)
            return remote(p, k, lands[p].at[origin], origin, sibling)

        def arrival(p, k):
            peer, pidx = _peer(x, y, c, k)
            return remote(p, k, lands[p].at[pidx], pidx, peer)

        return own, direct, forward, arrival

    def start(self, srcs, lands, sems):
        own, direct, _, _ = self._ops(srcs, lands, sems)
        for p in range(self.n):
            own(p).start()
            for k in ((SIBLING,) + SAME_CORE) if self.to_all[p] else range(1, N_DEV):
                direct(p, k).start()

    def forward(self, srcs, lands, sems):
        _, _, forward, arrival = self._ops(srcs, lands, sems)
        for p in range(self.n):
            if self.to_all[p]:
                for k in SAME_CORE:
                    arrival(p, k).wait_recv()
                    forward(p, k ^ SIBLING).start()

    def finish(self, srcs, lands, sems):
        own, direct, forward, arrival = self._ops(srcs, lands, sems)
        for p in range(self.n):
            for k in ((SIBLING,) + OTHER_CORE) if self.to_all[p] else range(1, N_DEV):
                arrival(p, k).wait_recv()
        for p in range(self.n):
            for k in range(1, N_DEV):
                (forward(p, k) if self.to_all[p] and k in OTHER_CORE else direct(p, k)).wait_send()
            own(p).wait()


class _LayerRows:
    def __init__(self, array, layer):
        self.array = array if array.ndim == 3 else array.reshape(DEPTH, 1, -1)
        self.layer = layer

    def spec(self):
        layer = self.layer
        return pl.BlockSpec((None,) + self.array.shape[1:], lambda *_: (layer, 0, 0), pipeline_mode=pl.Buffered(1))


def _call(body, exch, *, name, grid, in_specs, out_specs, out_shape, scratch_shapes=(), compiler_params, after=None):
    def with_layer_rows(args):
        specs = [a.spec() if isinstance(a, _LayerRows) else s for s, a in zip(in_specs, args)]
        return specs, [a.array if isinstance(a, _LayerRows) else a for a in args]

    if exch is None:
        def plain(*args):
            specs, args = with_layer_rows(args)
            return pl.pallas_call(body, name=name, grid=grid, in_specs=specs, out_specs=out_specs, out_shape=out_shape,
                                  scratch_shapes=scratch_shapes, compiler_params=compiler_params)(*args)
        return plain
    n_in, n_out, n_scr, n_x = len(in_specs), len(out_shape), len(scratch_shapes), exch.n
    steps = math.prod(grid)

    def carrying(*refs):
        a, b, c, d, e = n_in, n_in + n_x, n_in + n_x + n_out, n_in + 2 * n_x + n_out, n_in + 2 * n_x + n_out + n_scr
        ins, srcs, outs, lands, scr, sems = refs[:a], refs[a:b], refs[b:c], refs[c:d], refs[d:e], refs[e:]
        step = functools.reduce(lambda acc, t: acc * grid[t] + pl.program_id(t), range(len(grid)), 0)

        @pl.when(step == 0)
        def _():
            exch.start(srcs, lands, sems)

        body(*ins, *outs, *scr)

        if exch.forwards:
            @pl.when(step == max(0, steps - 1 - pl.cdiv(steps, exch.forward_lead)))
            def _():
                exch.forward(srcs, lands, sems)

        @pl.when(step == steps - 1)
        def _():
            exch.finish(srcs, lands, sems)
            if after is not None:
                after(lands, *ins, *outs, *scr)

    hbm = pl.BlockSpec(memory_space=pl.ANY)

    def run(*args):
        specs, args = with_layer_rows(args)
        res = pl.pallas_call(
            carrying, name=name, grid=grid, in_specs=specs + [hbm] * n_x, out_specs=list(out_specs) + [hbm] * n_x,
            out_shape=list(out_shape) + exch.land_shapes, scratch_shapes=list(scratch_shapes) + exch.sem_shapes,
            compiler_params=compiler_params)(*args, *exch.srcs)
        return list(res[:n_out]), list(res[n_out:])

    return run


def _sum_small(part):
    exch = _Exchange([(part, True)])

    def body(part_ref, out_ref, land, *sems):
        exch.start([part_ref], [land], sems)
        exch.forward([part_ref], [land], sems)
        exch.finish([part_ref], [land], sems)
        acc = land[0]
        for d in range(1, N_DEV):
            acc = acc + land[d]
        out_ref[...] = acc

    vmem = pl.BlockSpec(memory_space=pltpu.VMEM)
    return pl.pallas_call(
        body,
        name="sum_small",
        in_specs=[vmem],
        out_specs=vmem,
        out_shape=jax.ShapeDtypeStruct(part.shape, F32),
        scratch_shapes=[pltpu.VMEM(exch.land_shapes[0].shape, F32)] + exch.sem_shapes,
    )(part)


def _adamw(w, g, m, v):
    m = ADAM_B1 * m + (1.0 - ADAM_B1) * g
    v = ADAM_B2 * v + (1.0 - ADAM_B2) * jnp.square(g)
    m_hat = m / (1.0 - ADAM_B1 ** ADAM_STEP)
    v_hat = v / (1.0 - ADAM_B2 ** ADAM_STEP)
    delta = -ADAM_LR * (m_hat / (jnp.sqrt(v_hat) + ADAM_EPS) + ADAM_WD * w)
    return delta, m, v


def _landed_specs(tr, wd):
    return [pl.BlockSpec((N_DEV, tr, wd), lambda l, i, ll=ll: (0, jnp.where(l == ll, i, 0), 0)) for ll in range(DEPTH)]


def _device_sum(r_ref):
    acc = r_ref[0].astype(F32)
    for d in range(1, N_DEV):
        acc = acc + r_ref[d].astype(F32)
    return acc


def _sum_adamw(recv, w, m, v, tr, name, transposed=False):
    _, r, wd = recv[0].shape

    def body(*refs):
        w_ref, m_ref, v_ref, g_ref, d_ref, mo_ref, vo_ref = refs[DEPTH:]
        for ll in range(DEPTH):
            @pl.when(pl.program_id(0) == ll)
            def _(ll=ll):
                g = _device_sum(refs[ll])
                g = g.T if transposed else g
                g_ref[0] = g
                d_ref[0], mo_ref[0], vo_ref[0] = _adamw(w_ref[0], g, m_ref[0], v_ref[0])

    if transposed:
        blk = pl.BlockSpec((1, wd, tr), lambda l, i: (l, 0, i))
        shape = jax.ShapeDtypeStruct((DEPTH, wd, r), F32)
    else:
        blk = pl.BlockSpec((1, tr, wd), lambda l, i: (l, i, 0))
        shape = jax.ShapeDtypeStruct((DEPTH, r, wd), F32)
    return pl.pallas_call(
        body,
        name=name,
        grid=(DEPTH, r // tr),
        in_specs=_landed_specs(tr, wd) + [blk, blk, blk],
        out_specs=[blk] * 4,
        out_shape=[shape] * 4,
        compiler_params=_params(("arbitrary", "arbitrary")),
    )(*recv, w, m, v)


def _adamw_small(ws, gs, ms, vs):
    n = len(ws)

    def body(*refs):
        w_r, g_r, m_r, v_r = refs[:n], refs[n:2 * n], refs[2 * n:3 * n], refs[3 * n:4 * n]
        d_o, m_o, v_o = refs[4 * n:5 * n], refs[5 * n:6 * n], refs[6 * n:7 * n]
        for t in range(n):
            d_o[t][...], m_o[t][...], v_o[t][...] = _adamw(w_r[t][...], g_r[t][...], m_r[t][...], v_r[t][...])

    vmem = pl.BlockSpec(memory_space=pltpu.VMEM)
    shapes = [jax.ShapeDtypeStruct(w.shape, F32) for w in ws]
    outs = pl.pallas_call(
        body,
        name="adamw_small",
        in_specs=[vmem] * (4 * n),
        out_specs=[vmem] * (3 * n),
        out_shape=shapes * 3,
    )(*ws, *gs, *ms, *vs)
    return outs[:n], outs[n:2 * n], outs[2 * n:]


def kernel(x, meta_tokens, mix_pre_g, w_in, conv_w, sinks, attn_out_g, conv_out_g, w_out, mix_post_g, mlp_pre_g, w_up, w_down, mlp_post_g, loss_target, m_meta_tokens, m_mix_pre_g, m_w_in, m_conv_w, m_sinks, m_attn_out_g, m_conv_out_g, m_w_out, m_mix_post_g, m_mlp_pre_g, m_w_up, m_w_down, m_mlp_post_g, v_meta_tokens, v_mix_pre_g, v_w_in, v_conv_w, v_sinks, v_attn_out_g, v_conv_out_g, v_w_out, v_mix_post_g, v_mlp_pre_g, v_w_up, v_w_down, v_mlp_post_g):
    seq = x.shape[1]
    lp = BLOCK + seq
    tm = _row_tile(lp)
    tm_mlp = _row_tile(lp, (320, 256, 128))
    tm_dw_mlp = _row_tile(lp, (1664, 1040, 640, 384, 256, 128))
    tm_dw_mix = _row_tile(lp, (1664, 832, 640, 384, 256, 128))
    me = 4 * lax.axis_index("x") + 2 * lax.axis_index("y") + lax.axis_index("c")
    cshard = CONV_W // N_DEV
    mshard = D_MODEL // N_DEV

    gather_with = {
        ("in_proj_fwd", 0): [("down", 0)], ("attn_fwd", 0): [("out", 0), ("up", 0)],
        ("mlp_fwd", 0): [("in", 1), ("out", 1), ("up", 1), ("down", 1)],
    }
    tight = {("in_proj_fwd", 0), ("attn_fwd", 0)}
    scatter_with = {
        ("attn_bwd", 1): [("down", 1)], ("mix_bwd_dw", 1): [("out", 1)], ("mlp_bwd_dx", 0): [("up", 1), ("in", 1)],
        ("mix_out_bwd", 0): [("up", 0)], ("attn_bwd", 0): [("down", 0)], ("mix_bwd_dw", 0): [("out", 0)],
        ("in_proj_bwd_dx", 0): [("in", 0)],
    }
    shard = {"in": jnp.swapaxes(w_in, 1, 2).astype(BF), "out": w_out.astype(BF),
             "up": jnp.swapaxes(w_up, 1, 2).astype(BF), "down": w_down.astype(BF)}
    weight = {}
    grad = {}
    landed = {}

    def run(fn, kind, l, *args):
        key, name = (kind, l), f"{kind}_{l}"
        if key in gather_with:
            blocks = gather_with[key]
            lead = 16 if key in tight else 8
            outs, lands = fn(*args, name, _Exchange([(shard[n][k], True) for n, k in blocks], lead))
            for b, land in zip(blocks, lands):
                weight[b] = land.reshape(-1, D_MODEL)
            return outs
        if key in scatter_with:
            blocks = scatter_with[key]
            outs, lands = fn(*args, name, _Exchange([(grad[b].reshape(N_DEV, -1, D_MODEL), False) for b in blocks]))
            landed.update(zip(blocks, lands))
            return outs
        return fn(*args, name)

    small = jnp.zeros((24, 128), F32)
    small = small.at[0:N_META, :].set(meta_tokens)
    small = small.at[N_META:N_META + 6, 0:cshard].set(conv_w.reshape(6, cshard))
    first = _Exchange([(shard["in"][0], True), (small, True)], 16)
    h, rope, (first_in, g_small) = _build_h(x[0], _rope_table(lp), tm, first, 1, "build_h")
    weight[("in", 0)] = first_in.reshape(-1, D_MODEL)
    cw = g_small[:, N_META:N_META + 6, 0:cshard].reshape(N_DEV, DEPTH, 3, cshard)
    cw = jnp.transpose(cw, (1, 2, 0, 3)).reshape(DEPTH, 3, CONV_W)
    conv_full = jnp.concatenate([cw, jnp.zeros((DEPTH, 5, CONV_W), F32)], axis=1)

    row1 = _LayerRows

    saved = []
    for l in range(DEPTH):
        a, qkv, bch = run(_in_proj_fwd, "in_proj_fwd", l, h, row1(mix_pre_g, l), weight[("in", l)], rope, tm)
        y_attn, probs, p_sink = run(_attn_fwd, "attn_fwd", l, qkv, sinks[l].reshape(1, -1))
        yc, y, z, h2 = run(_mix_out_fwd, "mix_out_fwd", l, bch, y_attn, h, row1(conv_full, l), row1(attn_out_g, l),
                       row1(conv_out_g, l), weight[("out", l)], row1(mix_post_g, l), tm)
        mlp = _mlp_fwd if l < DEPTH - 1 else functools.partial(_mlp_fwd, target=loss_target[0])
        a2, up, f, *rest = run(mlp, "mlp_fwd", l, h2, row1(mlp_pre_g, l), weight[("up", l)], weight[("down", l)],
                               row1(mlp_post_g, l), tm_mlp)
        saved.append((h, a, qkv, bch, y_attn, probs, p_sink, yc, y, z, h2, a2, up, f))
        h = rest[0]
    dh, loss_part = rest[0], rest[1][0, 0] * (0.5 / D_MODEL)

    gsmall = [None] * DEPTH
    for l in reversed(range(DEPTH)):
        h0, a, qkv, bch, y_attn, probs, p_sink, yc, y, z, h2, a2, up, f = saved[l]
        df, dup, dh2, dg_mlp = run(_mlp_bwd_dx, "mlp_bwd_dx", l, dh, f, up, h2, weight[("down", l)], weight[("up", l)],
                                   row1(mlp_post_g, l), row1(mlp_pre_g, l), tm_mlp)
        grad[("down", l)], grad[("up", l)] = _mlp_bwd_dw(up, df, dup, a2, tm_dw_mlp, f"mlp_bwd_dw_{l}")
        dya, dbch, dg_mix, grad[("out", l)] = run(
            _mix_out_bwd, "mix_out_bwd", l, dh2, z, y_attn, yc, bch, y, weight[("out", l)], row1(mix_post_g, l),
            row1(attn_out_g, l), row1(conv_out_g, l), row1(conv_full, l), tm)
        dq, dkv, dsink = run(_attn_bwd, "attn_bwd", l, qkv, y_attn, dya, probs, p_sink, rope)
        grad[("in", l)], = run(_mix_bwd_dw, "mix_bwd_dw", l, dq, dkv, dbch, a, tm_dw_mix)
        *dhs, dg_in = run(_in_proj_bwd_dx if l else _in_proj_bwd_first, "in_proj_bwd_dx", l, dq, dkv, dbch,
                          weight[("in", l)], h0, dh2, row1(mix_pre_g, l), tm)
        dh = dhs[0]
        tile_a =dg_mlp + dg_in + jnp.pad(dsink, ((0, 0), (0, D_MODEL - 128)))
        gsmall[l] = (tile_a, dg_mix)
    grad_x, d_meta = dhs[0][None], dhs[1]

    loss_tile = jnp.zeros((8, D_MODEL), F32).at[ROW_LOSS, 0].set(loss_part)
    tot = _sum_small(jnp.concatenate(
        [gsmall[0][0] + loss_tile, gsmall[0][1], gsmall[1][0], gsmall[1][1], d_meta], axis=0))
    loss = tot[ROW_LOSS, 0]
    ta = [tot[16 * l:16 * l + 8] for l in range(DEPTH)]
    tb = [tot[16 * l + 8:16 * l + 16] for l in range(DEPTH)]
    pick = lambda tiles, r0, r1, c0, c1: jnp.stack([t[r0:r1, c0:c1] for t in tiles])
    g_mlp_post = pick(ta, ROW_MLP_POST, ROW_MLP_POST + 1, 0, D_MODEL).reshape(DEPTH, D_MODEL)
    g_mlp_pre = pick(ta, ROW_MLP_PRE, ROW_MLP_PRE + 1, 0, D_MODEL).reshape(DEPTH, D_MODEL)
    g_mix_pre = pick(ta, ROW_MIX_PRE, ROW_MIX_PRE + 1, 0, D_MODEL).reshape(DEPTH, D_MODEL)
    g_sinks = pick(ta, ROW_SINK, ROW_SINK + 1, 0, N_Q_HEADS).reshape(DEPTH, N_Q_HEADS)
    g_mix_post = pick(tb, ROW_MIX_POST, ROW_MIX_POST + 1, 0, D_MODEL).reshape(DEPTH, D_MODEL)
    g_attn_out = pick(tb, ROW_GROUP_G, ROW_GROUP_G + 1, 0, ATTN_W).reshape(DEPTH, ATTN_W)
    g_conv_out = pick(tb, ROW_GROUP_G, ROW_GROUP_G + 1, ATTN_W, D_MODEL).reshape(DEPTH, CONV_W)
    g_conv_full = pick(tb, ROW_CONV, ROW_CONV + 3, 0, CONV_W)
    g_conv = lax.dynamic_slice_in_dim(g_conv_full, me * cshard, cshard, axis=2)
    g_meta = lax.dynamic_slice_in_dim(tot[16 * DEPTH:16 * DEPTH + N_META], me * mshard, mshard, axis=1)

    r_in, r_out, r_up, r_down = [[landed[(n, l)] for l in range(DEPTH)] for n in ("in", "out", "up", "down")]
    t12 = lambda a: jnp.swapaxes(a, 1, 2)
    g_w_in, d_w_in, nm_w_in, nv_w_in = map(t12, _sum_adamw(r_in, t12(w_in), t12(m_w_in), t12(v_w_in), 96, "adamw_w_in"))
    g_w_up, d_w_up, nm_w_up, nv_w_up = _sum_adamw(r_up, w_up, m_w_up, v_w_up, 128, "adamw_w_up", transposed=True)
    g_w_out, d_w_out, nm_w_out, nv_w_out = _sum_adamw(r_out, w_out, m_w_out, v_w_out, 128, "adamw_w_out")
    g_w_down, d_w_down, nm_w_down, nv_w_down = _sum_adamw(r_down, w_down, m_w_down, v_w_down, 128, "adamw_w_down")

    ws = [meta_tokens, mix_pre_g, conv_w.reshape(6, cshard), sinks, attn_out_g, conv_out_g, mix_post_g, mlp_pre_g, mlp_post_g]
    gs = [g_meta, g_mix_pre, g_conv.reshape(6, cshard), g_sinks, g_attn_out, g_conv_out, g_mix_post, g_mlp_pre, g_mlp_post]
    ms = [m_meta_tokens, m_mix_pre_g, m_conv_w.reshape(6, cshard), m_sinks, m_attn_out_g, m_conv_out_g, m_mix_post_g,
          m_mlp_pre_g, m_mlp_post_g]
    vs = [v_meta_tokens, v_mix_pre_g, v_conv_w.reshape(6, cshard), v_sinks, v_attn_out_g, v_conv_out_g, v_mix_post_g,
          v_mlp_pre_g, v_mlp_post_g]
    ds, nms, nvs = _adamw_small(ws, gs, ms, vs)

    def order(meta, mix_pre, cv, sk, a_out, c_out, mix_post, mlp_pre, mlp_post, win, wout, wup, wdown):
        return [meta, mix_pre, win, cv.reshape(DEPTH, 3, cshard), sk, a_out, c_out, wout, mix_post, mlp_pre, wup, wdown, mlp_post]

    grads = order(*gs, g_w_in, g_w_out, g_w_up, g_w_down)
    deltas = order(*ds, d_w_in, d_w_out, d_w_up, d_w_down)
    new_m = order(*nms, nm_w_in, nm_w_out, nm_w_up, nm_w_down)
    new_v = order(*nvs, nv_w_in, nv_w_out, nv_w_up, nv_w_down)
    return (loss, grad_x, *grads, *deltas, *new_m, *new_v)
```
